```python
import math
import jax, jax.numpy as jnp
from jax import lax
import numpy as np

D_MODEL = 2048
BATCH = 4
SEQ = 2048
DEPTH = 1
DEC_BATCH = 32
DEC_SEQ = 1
PAST_LEN = 16384
PAGE_SIZE = 128

F32 = jnp.float32
HEAD_DIM = 128
RET_HEADS = D_MODEL // (2 * HEAD_DIM)
NSA_HEADS = D_MODEL // (2 * HEAD_DIM)
NSA_KV_HEADS = 2
RET_W = RET_HEADS * HEAD_DIM
NSA_W = NSA_HEADS * HEAD_DIM
KV_W = NSA_KV_HEADS * HEAD_DIM
MIX_W = RET_W + NSA_W
RET_CHUNK = 128
RET_ROPE_THETA = 10000.0
ROPE_THETA = 500000.0
ROT_DIM = HEAD_DIM // 4
CMP_BLOCK = 64
SEL_TOPK = 16
WINDOW = 512
WIN_Q_BLOCK = 128
SEL_Q_BLOCK = 64
N_GROUPS = 4
EXPERTS_PER_GROUP = 8
N_EXPERTS = N_GROUPS * EXPERTS_PER_GROUP
EXPERT_TOPK = 2
D_EXPERT = 512
MOE_BLOCK = 128
LN_EPS = 1e-5
GN_EPS = 1e-5
NEG = -1e30
DEEPNORM_ALPHA = (2 * DEPTH) ** 0.25
DEEPNORM_BETA = (8 * DEPTH) ** -0.25
SPLITS = (RET_W, RET_W, RET_W, RET_W, NSA_W, KV_W, KV_W, KV_W, KV_W, KV_W, KV_W, NSA_HEADS * 3)
IN_W = sum(SPLITS)

kernel_name = 'retnet_nsa_hymba_hmoe_deepnorm_step'


def _layer_norm(x, g, b):
    xf = x.astype(F32)
    mu = xf.mean(-1, keepdims=True)
    var = jnp.square(xf - mu).mean(-1, keepdims=True)
    return ((xf - mu) * lax.rsqrt(var + LN_EPS) * g + b).astype(x.dtype)


def _rope(x, pos, rot_dim, theta):
    half = rot_dim // 2
    inv = theta ** (-jnp.arange(0, rot_dim, 2, dtype=F32) / rot_dim)
    ang = pos[..., None].astype(F32) * inv
    cos = jnp.cos(ang)[:, :, None, :]
    sin = jnp.sin(ang)[:, :, None, :]
    xr = x[..., :rot_dim].astype(F32)
    x1, x2 = xr[..., :half], xr[..., half:]
    rot = jnp.concatenate([x1 * cos - x2 * sin, x2 * cos + x1 * sin], -1).astype(x.dtype)
    return jnp.concatenate([rot, x[..., rot_dim:]], -1)


def _heads(t, n):
    return t.reshape(t.shape[0], t.shape[1], n, HEAD_DIM)


def _project(x, w_in):
    p = jnp.einsum('bld,de->ble', x, w_in)
    cuts = [int(c) for c in np.cumsum(SPLITS)[:-1]]
    return jnp.split(p, cuts, axis=-1)


def _chunk_retention(q, k, v, s0):
    n, l, h, d = q.shape
    c = RET_CHUNK if l % RET_CHUNK == 0 else l
    nc = l // c
    log_g = jnp.log1p(-jnp.exp2(-5.0 - jnp.arange(h, dtype=F32)))
    i = jnp.arange(c, dtype=F32)
    rel = i[:, None] - i[None, :]
    dmask = jnp.where(rel[None] >= 0, jnp.exp(jnp.maximum(rel[None], 0.0) * log_g[:, None, None]), 0.0)
    q_dec = jnp.exp((i + 1.0)[None] * log_g[:, None])[..., None]
    k_dec = jnp.exp((c - 1.0 - i)[None] * log_g[:, None])[..., None]
    c_dec = jnp.exp(c * log_g)[:, None, None]

    def to_chunks(t):
        return t.astype(F32).reshape(n, nc, c, h, d).transpose(1, 0, 3, 2, 4)

    def step(s, qkv):
        qc, kc, vc = qkv
        att = jnp.einsum('bhid,bhjd->bhij', qc, kc) * dmask
        o = jnp.einsum('bhij,bhjd->bhid', att, vc) + jnp.einsum('bhid,bhde->bhie', qc * q_dec, s)
        s = c_dec * s + jnp.einsum('bhjd,bhje->bhde', kc * k_dec, vc)
        return s, o

    s, o = lax.scan(step, s0.astype(F32), (to_chunks(q), to_chunks(k), to_chunks(v)))
    return o.transpose(1, 0, 3, 2, 4).reshape(n, l, h, d), s


def _retention_group(rq, rk, rv, rg, pos, s0, gn_g):
    q = _rope(_heads(rq, RET_HEADS), pos, HEAD_DIM, RET_ROPE_THETA)
    k = _rope(_heads(rk, RET_HEADS), pos, HEAD_DIM, RET_ROPE_THETA) * (HEAD_DIM ** -0.5)
    v = _heads(rv, RET_HEADS)
    o, s = _chunk_retention(q, k, v, s0)
    mu = o.mean(-1, keepdims=True)
    var = jnp.square(o - mu).mean(-1, keepdims=True)
    on = (o - mu) * lax.rsqrt(var + GN_EPS) * gn_g.reshape(RET_HEADS, HEAD_DIM).astype(F32)
    out = jax.nn.silu(rg.astype(F32)) * on.reshape(rg.shape)
    return out.astype(rq.dtype), s


def _gqa_attend(q, k, v, mask):
    n, lq, h, d = q.shape
    kv = k.shape[2]
    qg = q.reshape(n, lq, kv, h // kv, d)
    s = jnp.einsum('nqkgd,nskd->nkgqs', qg, k).astype(F32) * (d ** -0.5)
    m = mask[:, None, None]
    p = jax.nn.softmax(jnp.where(m, s, NEG), axis=-1) * m
    o = jnp.einsum('nkgqs,nskd->nqkgd', p.astype(v.dtype), v)
    return o.reshape(n, lq, h, d), p


def _nsa_heads(nq, ck, sk, wk, cv, sv, wv, pos):
    rp = lambda t, nh: _rope(_heads(t, nh), pos, ROT_DIM, ROPE_THETA)
    return (rp(nq, NSA_HEADS), rp(ck, NSA_KV_HEADS), rp(sk, NSA_KV_HEADS), rp(wk, NSA_KV_HEADS),
            _heads(cv, NSA_KV_HEADS), _heads(sv, NSA_KV_HEADS), _heads(wv, NSA_KV_HEADS))


def _compress(rows, w):
    n, t, kv, d = rows.shape
    return jnp.einsum('nbjkd,jd->nbkd', rows.reshape(n, t // CMP_BLOCK, CMP_BLOCK, kv, d), w)


def _cmp_branch(q, pos, kc, vc):
    nb = kc.shape[1]
    blk_end = (jnp.arange(nb) + 1) * CMP_BLOCK - 1
    mask = blk_end[None, None, :] <= pos[:, :, None]
    o, p = _gqa_attend(q, kc, vc, mask)
    imp = p.sum(axis=2).transpose(0, 2, 1, 3)
    return o, imp


def _select_blocks(imp, pos, n_sel):
    nb = imp.shape[-1]
    imp = jnp.pad(imp, ((0, 0), (0, 0), (0, 0), (0, n_sel - nb)))
    blk = jnp.arange(n_sel)
    cur = (pos // CMP_BLOCK)[:, :, None, None]
    forced = (blk == 0) | (blk == cur) | (blk == cur - 1)
    score = jnp.where(blk > cur, NEG, jnp.where(forced, -NEG, imp))
    top, idx = lax.top_k(score, min(SEL_TOPK, n_sel))
    return idx, top > 0.5 * NEG


def _sel_attend(q, pos, ks, vs, idx, valid):
    n, lq, kv, kk, cb, d = ks.shape
    h = q.shape[2]
    kpos = idx[..., None] * CMP_BLOCK + jnp.arange(CMP_BLOCK)
    m = ((kpos <= pos[:, :, None, None, None]) & valid[..., None]).reshape(n, lq, kv, 1, kk * cb)
    qg = q.reshape(n, lq, kv, h // kv, d)
    kf = ks.reshape(n, lq, kv, kk * cb, d)
    vf = vs.reshape(n, lq, kv, kk * cb, d)
    s = jnp.einsum('nqkgd,nqkjd->nqkgj', qg, kf).astype(F32) * (d ** -0.5)
    p = jax.nn.softmax(jnp.where(m, s, NEG), axis=-1) * m
    o = jnp.einsum('nqkgj,nqkjd->nqkgd', p.astype(vf.dtype), vf)
    return o.reshape(n, lq, h, d)


def _sel_prompt(q, pos, k, v, idx, valid):
    b, s, h, d = q.shape
    kv = k.shape[2]
    nb = s // CMP_BLOCK
    nq = s // SEL_Q_BLOCK
    kb = k.reshape(b, nb, CMP_BLOCK, kv, d).transpose(0, 3, 1, 2, 4)
    vb = v.reshape(b, nb, CMP_BLOCK, kv, d).transpose(0, 3, 1, 2, 4)
    bi = jnp.arange(b)[:, None, None, None]
    hi = jnp.arange(kv)[None, None, :, None]

    def blockwise(t):
        return t.reshape(t.shape[0], nq, SEL_Q_BLOCK, *t.shape[2:]).swapaxes(0, 1)

    def one(args):
        qc, pc, ic, vc = args
        return _sel_attend(qc, pc, kb[bi, hi, ic], vb[bi, hi, ic], ic, vc)

    o = lax.map(one, (blockwise(q), blockwise(pos), blockwise(idx), blockwise(valid)))
    return o.swapaxes(0, 1).reshape(b, s, h, d)


def _win_prompt(q, k, v):
    b, s, h, d = q.shape
    kv = k.shape[2]
    nb = s // WIN_Q_BLOCK
    nprev = WINDOW // WIN_Q_BLOCK
    nw = nprev + 1
    padw = ((0, 0), (WINDOW, 0), (0, 0), (0, 0))
    kp = jnp.pad(k, padw).reshape(b, nb + nprev, WIN_Q_BLOCK, kv, d)
    vp = jnp.pad(v, padw).reshape(b, nb + nprev, WIN_Q_BLOCK, kv, d)
    kw = jnp.concatenate([kp[:, i:i + nb] for i in range(nw)], axis=2)
    vw = jnp.concatenate([vp[:, i:i + nb] for i in range(nw)], axis=2)
    qpos = jnp.arange(s).reshape(nb, WIN_Q_BLOCK)
    kpos = (jnp.arange(nb)[:, None] - nprev) * WIN_Q_BLOCK + jnp.arange(nw * WIN_Q_BLOCK)[None]
    qq, kk = qpos[:, :, None], kpos[:, None, :]
    mask = (kk <= qq) & (qq - kk < WINDOW) & (kk >= 0)
    mask = jnp.broadcast_to(mask[None], (b,) + mask.shape).reshape(b * nb, WIN_Q_BLOCK, nw * WIN_Q_BLOCK)
    o, _ = _gqa_attend(q.reshape(b * nb, WIN_Q_BLOCK, h, d), kw.reshape(b * nb, nw * WIN_Q_BLOCK, kv, d),
                       vw.reshape(b * nb, nw * WIN_Q_BLOCK, kv, d), mask)
    return o.reshape(b, s, h, d)


def _gather_selected(pool, new_rows, page_table, idx):
    n, l, kv, d = new_rows.shape
    n_pages = page_table.shape[1]
    past_blocks = n_pages * PAGE_SIZE // CMP_BLOCK
    nbn = -(-l // CMP_BLOCK)
    newb = jnp.pad(new_rows, ((0, 0), (0, nbn * CMP_BLOCK - l), (0, 0), (0, 0)))
    newb = newb.reshape(n, nbn, CMP_BLOCK, kv, d).transpose(0, 3, 1, 2, 4)
    bi = jnp.arange(n)[:, None, None, None]
    hi = jnp.arange(kv)[None, None, :, None]
    start = idx * CMP_BLOCK
    phys = page_table[bi, jnp.minimum(start // PAGE_SIZE, n_pages - 1)]
    off = (start % PAGE_SIZE)[..., None] + jnp.arange(CMP_BLOCK)
    past = pool[phys[..., None], off, hi[..., None]]
    new = newb[bi, hi, jnp.clip(idx - past_blocks, 0, nbn - 1)]
    return jnp.where((idx < past_blocks)[..., None, None], past, new)


def _nsa_combine(gl, o_cmp, o_sel, o_win):
    n, l = gl.shape[0], gl.shape[1]
    g = jax.nn.sigmoid(gl.astype(F32)).reshape(n, l, NSA_HEADS, 3, 1)
    o = g[..., 0, :] * o_cmp + g[..., 1, :] * o_sel + g[..., 2, :] * o_win
    return o.reshape(n, l, NSA_W).astype(o_cmp.dtype)


def _prompt_mixer(x, win_buf, w_in, w_cmp_k, w_cmp_v, gn_g):
    n, s, _ = x.shape
    pos = jnp.arange(s)[None]
    rq, rk, rv, rg, nq, ck, cv, sk, sv, wk, wv, gl = _project(x, w_in)
    s0 = jnp.zeros((n, RET_HEADS, HEAD_DIM, HEAD_DIM), F32)
    ret_out, s_fin = _retention_group(rq, rk, rv, rg, pos, s0, gn_g)
    q, ck, sk, wk, cv, sv, wv = _nsa_heads(nq, ck, sk, wk, cv, sv, wv, pos)
    o_cmp, imp = _cmp_branch(q, pos, _compress(ck, w_cmp_k), _compress(cv, w_cmp_v))
    idx, valid = _select_blocks(imp, pos, s // CMP_BLOCK)
    o_sel = _sel_prompt(q, pos, sk, sv, idx, valid)
    o_win = _win_prompt(q, wk, wv)
    feats = jnp.concatenate([ret_out, _nsa_combine(gl, o_cmp, o_sel, o_win)], -1)
    if s >= win_buf:
        bk, bv = wk[:, s - win_buf:], wv[:, s - win_buf:]
    else:
        padb = ((0, 0), (win_buf - s, 0), (0, 0), (0, 0))
        bk, bv = jnp.pad(wk, padb), jnp.pad(wv, padb)
    return feats, (ck, cv, sk, sv, bk, bv, s_fin)


def _sample_mixer(x, c_cmp_k, c_cmp_v, c_sel_k, c_sel_v, c_win_k, c_win_v, s_ret, page_table,
                  w_in, w_cmp_k, w_cmp_v, gn_g):
    n, l, _ = x.shape
    past = page_table.shape[1] * PAGE_SIZE
    pos = past + jnp.arange(l)[None]
    rq, rk, rv, rg, nq, ck, cv, sk, sv, wk, wv, gl = _project(x, w_in)
    ret_out, s_new = _retention_group(rq, rk, rv, rg, pos, s_ret, gn_g)
    q, ck, sk, wk, cv, sv, wv = _nsa_heads(nq, ck, sk, wk, cv, sv, wv, pos)
    rows = (n, past, NSA_KV_HEADS, HEAD_DIM)
    ck_all = jnp.concatenate([c_cmp_k[page_table].reshape(rows), ck], 1)
    cv_all = jnp.concatenate([c_cmp_v[page_table].reshape(rows), cv], 1)
    t = past + l
    ncb = t // CMP_BLOCK
    kc = _compress(ck_all[:, :ncb * CMP_BLOCK], w_cmp_k)
    vc = _compress(cv_all[:, :ncb * CMP_BLOCK], w_cmp_v)
    o_cmp, imp = _cmp_branch(q, pos, kc, vc)
    idx, valid = _select_blocks(imp, pos, -(-t // CMP_BLOCK))
    ks = _gather_selected(c_sel_k, sk, page_table, idx)
    vs = _gather_selected(c_sel_v, sv, page_table, idx)
    o_sel = _sel_attend(q, pos, ks, vs, idx, valid)
    wb = c_win_k.shape[1]
    kw = jnp.concatenate([c_win_k, wk], 1)
    vw = jnp.concatenate([c_win_v, wv], 1)
    kpos = past - wb + jnp.arange(wb + l)
    mask = (kpos[None, None, :] <= pos[:, :, None]) & (pos[:, :, None] - kpos[None, None, :] < WINDOW)
    o_win, _ = _gqa_attend(q, kw, vw, mask)
    feats = jnp.concatenate([ret_out, _nsa_combine(gl, o_cmp, o_sel, o_win)], -1)
    return feats, (ck, cv, sk, sv, kw[:, l:], vw[:, l:], s_new)


def _hmoe(x, w_group, b_group, w_expert, b_expert, w_gate, w_up, w_down):
    shp = x.shape
    d = shp[-1]
    xt = x.reshape(-1, d)
    t = xt.shape[0]
    g_logit = (xt @ w_group + b_group).astype(F32)
    grp = jnp.argmax(g_logit, -1)
    p_grp = jnp.take_along_axis(jax.nn.softmax(g_logit, -1), grp[:, None], 1)
    e_logit = (xt @ w_expert + b_expert).astype(F32).reshape(t, N_GROUPS, EXPERTS_PER_GROUP)
    e_logit = jnp.take_along_axis(e_logit, grp[:, None, None], 1)[:, 0]
    p_top, i_top = lax.top_k(jax.nn.softmax(e_logit, -1), EXPERT_TOPK)
    gate = p_grp * p_top / p_top.sum(-1, keepdims=True)
    eid = (grp[:, None] * EXPERTS_PER_GROUP + i_top).reshape(-1)
    a = eid.shape[0]
    order = jnp.argsort(eid)
    e_sorted = eid[order]
    tok = order // EXPERT_TOPK
    counts = jnp.bincount(eid, length=N_EXPERTS)
    starts = jnp.cumsum(counts) - counts
    padded = (counts + MOE_BLOCK - 1) // MOE_BLOCK * MOE_BLOCK
    pad_end = jnp.cumsum(padded)
    pad_start = pad_end - padded
    dest = pad_start[e_sorted] + jnp.arange(a) - starts[e_sorted]
    n_blk = -(-(a + N_EXPERTS * (MOE_BLOCK - 1)) // MOE_BLOCK)
    buf = jnp.zeros((n_blk * MOE_BLOCK, d), xt.dtype).at[dest].set(xt[tok])
    blk_e = jnp.minimum(jnp.searchsorted(pad_end, jnp.arange(n_blk) * MOE_BLOCK, side='right'), N_EXPERTS - 1)

    def expert_block(args):
        xb, e = args
        hb = jax.nn.silu(xb @ w_gate[e]) * (xb @ w_up[e])
        return hb @ w_down[e]

    yb = lax.map(expert_block, (buf.reshape(n_blk, MOE_BLOCK, d), blk_e))
    y = yb.reshape(-1, d)[dest] * gate.reshape(-1)[order][:, None].astype(xt.dtype)
    return jnp.zeros_like(xt).at[tok].add(y).reshape(shp)


def _post_block(x, feats, w_o, ln1_g, ln1_b, moe, ln2_g, ln2_b):
    h = _layer_norm(DEEPNORM_ALPHA * x + feats @ w_o, ln1_g, ln1_b)
    return _layer_norm(DEEPNORM_ALPHA * h + _hmoe(h, *moe), ln2_g, ln2_b)


def setup_inputs(seed: int = 0) -> dict:
    key = jax.random.key(seed)
    ks = jax.random.split(key, 32)
    n_pages = PAST_LEN // PAGE_SIZE
    used = DEC_BATCH * n_pages
    n_pool = used + max(1, used // 4)
    win_buf = min(WINDOW, PAST_LEN)
    nrm = lambda k, shape, scale: scale * jax.random.normal(k, shape, F32)
    pool = (DEPTH, n_pool, PAGE_SIZE, NSA_KV_HEADS, HEAD_DIM)
    wbuf = (DEPTH, DEC_BATCH, win_buf, NSA_KV_HEADS, HEAD_DIM)
    page_table = jax.random.permutation(ks[9], n_pool)[:used].reshape(DEC_BATCH, n_pages).astype(jnp.int32)
    return {
        'x_prompt': nrm(ks[0], (BATCH, SEQ, D_MODEL), 1.0),
        'x_sample': nrm(ks[1], (DEC_BATCH, DEC_SEQ, D_MODEL), 1.0),
        'cache_cmp_k': nrm(ks[2], pool, 1.0),
        'cache_cmp_v': nrm(ks[3], pool, 1.0),
        'cache_sel_k': nrm(ks[4], pool, 1.0),
        'cache_sel_v': nrm(ks[5], pool, 1.0),
        'cache_win_k': nrm(ks[6], wbuf, 1.0),
        'cache_win_v': nrm(ks[7], wbuf, 1.0),
        'state_ret': nrm(ks[8], (DEPTH, DEC_BATCH, RET_HEADS, HEAD_DIM, HEAD_DIM), 0.1),
        'page_table': page_table,
        'w_in': nrm(ks[10], (DEPTH, D_MODEL, IN_W), D_MODEL ** -0.5),
        'w_cmp_k': (1.0 + nrm(ks[11], (DEPTH, CMP_BLOCK, HEAD_DIM), 0.1)) / CMP_BLOCK,
        'w_cmp_v': (1.0 + nrm(ks[12], (DEPTH, CMP_BLOCK, HEAD_DIM), 0.1)) / CMP_BLOCK,
        'ret_gn_g': 1.0 + nrm(ks[13], (DEPTH, RET_W), 0.02),
        'w_o': nrm(ks[14], (DEPTH, MIX_W, D_MODEL), DEEPNORM_BETA * MIX_W ** -0.5),
        'ln1_g': 1.0 + nrm(ks[15], (DEPTH, D_MODEL), 0.02),
        'ln1_b': nrm(ks[16], (DEPTH, D_MODEL), 0.02),
        'w_group': nrm(ks[17], (DEPTH, D_MODEL, N_GROUPS), D_MODEL ** -0.5),
        'b_group': nrm(ks[18], (DEPTH, N_GROUPS), 0.01),
        'w_expert': nrm(ks[19], (DEPTH, D_MODEL, N_EXPERTS), D_MODEL ** -0.5),
        'b_expert': nrm(ks[20], (DEPTH, N_EXPERTS), 0.01),
        'w_gate': nrm(ks[21], (DEPTH, N_EXPERTS, D_MODEL, D_EXPERT), D_MODEL ** -0.5),
        'w_up': nrm(ks[22], (DEPTH, N_EXPERTS, D_MODEL, D_EXPERT), D_MODEL ** -0.5),
        'w_down': nrm(ks[23], (DEPTH, N_EXPERTS, D_EXPERT, D_MODEL), DEEPNORM_BETA * D_EXPERT ** -0.5),
        'ln2_g': 1.0 + nrm(ks[24], (DEPTH, D_MODEL), 0.02),
        'ln2_b': nrm(ks[25], (DEPTH, D_MODEL), 0.02),
    }


def reference(x_prompt, x_sample, cache_cmp_k, cache_cmp_v, cache_sel_k, cache_sel_v, cache_win_k, cache_win_v,
              state_ret, page_table, w_in, w_cmp_k, w_cmp_v, ret_gn_g, w_o, ln1_g, ln1_b, w_group, b_group,
              w_expert, b_expert, w_gate, w_up, w_down, ln2_g, ln2_b):
    win_buf = cache_win_k.shape[2]
    hp, hs = x_prompt, x_sample
    acc_p = [[] for _ in range(7)]
    acc_s = [[] for _ in range(7)]
    for l in range(DEPTH):
        fp, st_p = _prompt_mixer(hp, win_buf, w_in[l], w_cmp_k[l], w_cmp_v[l], ret_gn_g[l])
        fs, st_s = _sample_mixer(hs, cache_cmp_k[l], cache_cmp_v[l], cache_sel_k[l], cache_sel_v[l],
                                 cache_win_k[l], cache_win_v[l], state_ret[l], page_table,
                                 w_in[l], w_cmp_k[l], w_cmp_v[l], ret_gn_g[l])
        moe = (w_group[l], b_group[l], w_expert[l], b_expert[l], w_gate[l], w_up[l], w_down[l])
        hp = _post_block(hp, fp, w_o[l], ln1_g[l], ln1_b[l], moe, ln2_g[l], ln2_b[l])
        hs = _post_block(hs, fs, w_o[l], ln1_g[l], ln1_b[l], moe, ln2_g[l], ln2_b[l])
        for acc, t in zip(acc_p, st_p):
            acc.append(t)
        for acc, t in zip(acc_s, st_s):
            acc.append(t)
    p_cmp_k, p_cmp_v, p_sel_k, p_sel_v, p_win_k, p_win_v, p_ret = [jnp.stack(a) for a in acc_p]
    s_cmp_k, s_cmp_v, s_sel_k, s_sel_v, s_win_k, s_win_v, s_ret = [jnp.stack(a) for a in acc_s]
    return (hp, hs, p_cmp_k, p_cmp_v, p_sel_k, p_sel_v, p_win_k, p_win_v, p_ret.astype(state_ret.dtype),
            s_cmp_k, s_cmp_v, s_sel_k, s_sel_v, s_win_k, s_win_v, s_ret.astype(state_ret.dtype))
```

```python
import functools
import math

import jax
import jax.numpy as jnp
import numpy as np
from jax import lax
from jax.experimental import pallas as pl
from jax.experimental.pallas import tpu as pltpu

D_MODEL = 2048
DEPTH = 1
PAGE_SIZE = 128

F32 = jnp.float32
BF16 = jnp.bfloat16
HEAD_DIM = 128
RET_HEADS = D_MODEL // (2 * HEAD_DIM)
NSA_HEADS = D_MODEL // (2 * HEAD_DIM)
NSA_KV_HEADS = 2
RET_W = RET_HEADS * HEAD_DIM
NSA_W = NSA_HEADS * HEAD_DIM
KV_W = NSA_KV_HEADS * HEAD_DIM
MIX_W = RET_W + NSA_W
RET_CHUNK = 128
RET_ROPE_THETA = 10000.0
ROPE_THETA = 500000.0
ROT_DIM = HEAD_DIM // 4
CMP_BLOCK = 64
SEL_TOPK = 16
WINDOW = 512
WIN_Q_BLOCK = 128
SEL_Q_BLOCK = 64
N_GROUPS = 4
EXPERTS_PER_GROUP = 8
N_EXPERTS = N_GROUPS * EXPERTS_PER_GROUP
EXPERT_TOPK = 2
D_EXPERT = 512
MOE_BLOCK = 128
LN_EPS = 1e-5
GN_EPS = 1e-5
NEG = -1e30
DEEPNORM_ALPHA = (2 * DEPTH) ** 0.25
DEEPNORM_BETA = (8 * DEPTH) ** -0.25
SPLITS = (RET_W, RET_W, RET_W, RET_W, NSA_W, KV_W, KV_W, KV_W, KV_W, KV_W, KV_W, NSA_HEADS * 3)
IN_W = sum(SPLITS)
GATE_W = NSA_HEADS * 3
MAIN_W = IN_W - GATE_W
LANES = 128
VMEM_LIMIT = 48 * 1024 * 1024


def _matmul_kernel(x_ref, w_ref, o_ref):
    o_ref[...] = jnp.dot(x_ref[...].astype(BF16), w_ref[...].astype(BF16), preferred_element_type=F32)


def _matmul(x, w, n_out, tm, tn, name):
    t, k = x.shape
    return pl.pallas_call(
        _matmul_kernel,
        out_shape=jax.ShapeDtypeStruct((t, n_out), F32),
        grid=(t // tm, n_out // tn),
        in_specs=[pl.BlockSpec((tm, k), lambda i, j: (i, 0)),
                  pl.BlockSpec((k, tn), lambda i, j: (0, j))],
        out_specs=pl.BlockSpec((tm, tn), lambda i, j: (i, j)),
        compiler_params=pltpu.CompilerParams(dimension_semantics=("arbitrary", "arbitrary"),
                                             vmem_limit_bytes=VMEM_LIMIT),
        name=name,
    )(x, w)


def _project(x, w_in):
    n, l, d = x.shape
    xt = x.reshape(n * l, d)
    tm = min(512, n * l)
    main = _matmul(xt, w_in, MAIN_W, tm, 512, "in_proj")
    w_tail = jnp.pad(w_in[:, MAIN_W:], ((0, 0), (0, LANES - GATE_W)))
    tail = _matmul(xt, w_tail, LANES, tm, LANES, "gate_proj")[:, :GATE_W]
    cuts = [int(c) for c in np.cumsum(SPLITS)[:-2]]
    parts = jnp.split(main, cuts, axis=-1) + [tail]
    return [p.reshape(n, l, -1) for p in parts]


def _layer_norm(x, g, b):
    xf = x.astype(F32)
    mu = xf.mean(-1, keepdims=True)
    var = jnp.square(xf - mu).mean(-1, keepdims=True)
    return ((xf - mu) * lax.rsqrt(var + LN_EPS) * g + b).astype(x.dtype)


def _rope(x, pos, rot_dim, theta):
    half = rot_dim // 2
    inv = theta ** (-jnp.arange(0, rot_dim, 2, dtype=F32) / rot_dim)
    ang = pos[..., None].astype(F32) * inv
    cos = jnp.cos(ang)[:, :, None, :]
    sin = jnp.sin(ang)[:, :, None, :]
    xr = x[..., :rot_dim].astype(F32)
    x1, x2 = xr[..., :half], xr[..., half:]
    rot = jnp.concatenate([x1 * cos - x2 * sin, x2 * cos + x1 * sin], -1).astype(x.dtype)
    return jnp.concatenate([rot, x[..., rot_dim:]], -1)


def _heads(t, n):
    return t.reshape(t.shape[0], t.shape[1], n, HEAD_DIM)


def _chunk_retention(q, k, v, s0):
    n, l, h, d = q.shape
    c = RET_CHUNK if l % RET_CHUNK == 0 else l
    nc = l // c
    log_g = jnp.log1p(-jnp.exp2(-5.0 - jnp.arange(h, dtype=F32)))
    i = jnp.arange(c, dtype=F32)
    rel = i[:, None] - i[None, :]
    dmask = jnp.where(rel[None] >= 0, jnp.exp(jnp.maximum(rel[None], 0.0) * log_g[:, None, None]), 0.0)
    q_dec = jnp.exp((i + 1.0)[None] * log_g[:, None])[..., None]
    k_dec = jnp.exp((c - 1.0 - i)[None] * log_g[:, None])[..., None]
    c_dec = jnp.exp(c * log_g)[:, None, None]

    def to_chunks(t):
        return t.astype(F32).reshape(n, nc, c, h, d).transpose(1, 0, 3, 2, 4)

    def step(s, qkv):
        qc, kc, vc = qkv
        att = jnp.einsum('bhid,bhjd->bhij', qc, kc) * dmask
        o = jnp.einsum('bhij,bhjd->bhid', att, vc) + jnp.einsum('bhid,bhde->bhie', qc * q_dec, s)
        s = c_dec * s + jnp.einsum('bhjd,bhje->bhde', kc * k_dec, vc)
        return s, o

    s, o = lax.scan(step, s0.astype(F32), (to_chunks(q), to_chunks(k), to_chunks(v)))
    return o.transpose(1, 0, 3, 2, 4).reshape(n, l, h, d), s


def _retention_group(rq, rk, rv, rg, pos, s0, gn_g):
    q = _rope(_heads(rq, RET_HEADS), pos, HEAD_DIM, RET_ROPE_THETA)
    k = _rope(_heads(rk, RET_HEADS), pos, HEAD_DIM, RET_ROPE_THETA) * (HEAD_DIM ** -0.5)
    v = _heads(rv, RET_HEADS)
    o, s = _chunk_retention(q, k, v, s0)
    mu = o.mean(-1, keepdims=True)
    var = jnp.square(o - mu).mean(-1, keepdims=True)
    on = (o - mu) * lax.rsqrt(var + GN_EPS) * gn_g.reshape(RET_HEADS, HEAD_DIM).astype(F32)
    out = jax.nn.silu(rg.astype(F32)) * on.reshape(rg.shape)
    return out.astype(rq.dtype), s


def _gqa_attend(q, k, v, mask):
    n, lq, h, d = q.shape
    kv = k.shape[2]
    qg = q.reshape(n, lq, kv, h // kv, d)
    s = jnp.einsum('nqkgd,nskd->nkgqs', qg, k).astype(F32) * (d ** -0.5)
    m = mask[:, None, None]
    p = jax.nn.softmax(jnp.where(m, s, NEG), axis=-1) * m
    o = jnp.einsum('nkgqs,nskd->nqkgd', p.astype(v.dtype), v)
    return o.reshape(n, lq, h, d), p


def _nsa_heads(nq, ck, sk, wk, cv, sv, wv, pos):
    rp = lambda t, nh: _rope(_heads(t, nh), pos, ROT_DIM, ROPE_THETA)
    return (rp(nq, NSA_HEADS), rp(ck, NSA_KV_HEADS), rp(sk, NSA_KV_HEADS), rp(wk, NSA_KV_HEADS),
            _heads(cv, NSA_KV_HEADS), _heads(sv, NSA_KV_HEADS), _heads(wv, NSA_KV_HEADS))


def _compress(rows, w):
    n, t, kv, d = rows.shape
    return jnp.einsum('nbjkd,jd->nbkd', rows.reshape(n, t // CMP_BLOCK, CMP_BLOCK, kv, d), w)


def _cmp_branch(q, pos, kc, vc):
    nb = kc.shape[1]
    blk_end = (jnp.arange(nb) + 1) * CMP_BLOCK - 1
    mask = blk_end[None, None, :] <= pos[:, :, None]
    o, p = _gqa_attend(q, kc, vc, mask)
    imp = p.sum(axis=2).transpose(0, 2, 1, 3)
    return o, imp


def _select_blocks(imp, pos, n_sel):
    nb = imp.shape[-1]
    imp = jnp.pad(imp, ((0, 0), (0, 0), (0, 0), (0, n_sel - nb)))
    blk = jnp.arange(n_sel)
    cur = (pos // CMP_BLOCK)[:, :, None, None]
    forced = (blk == 0) | (blk == cur) | (blk == cur - 1)
    score = jnp.where(blk > cur, NEG, jnp.where(forced, -NEG, imp))
    top, idx = lax.top_k(score, min(SEL_TOPK, n_sel))
    return idx, top > 0.5 * NEG


def _sel_attend(q, pos, ks, vs, idx, valid):
    n, lq, kv, kk, cb, d = ks.shape
    h = q.shape[2]
    kpos = idx[..., None] * CMP_BLOCK + jnp.arange(CMP_BLOCK)
    m = ((kpos <= pos[:, :, None, None, None]) & valid[..., None]).reshape(n, lq, kv, 1, kk * cb)
    qg = q.reshape(n, lq, kv, h // kv, d)
    kf = ks.reshape(n, lq, kv, kk * cb, d)
    vf = vs.reshape(n, lq, kv, kk * cb, d)
    s = jnp.einsum('nqkgd,nqkjd->nqkgj', qg, kf).astype(F32) * (d ** -0.5)
    p = jax.nn.softmax(jnp.where(m, s, NEG), axis=-1) * m
    o = jnp.einsum('nqkgj,nqkjd->nqkgd', p.astype(vf.dtype), vf)
    return o.reshape(n, lq, h, d)


def _sel_prompt(q, pos, k, v, idx, valid):
    b, s, h, d = q.shape
    kv = k.shape[2]
    nb = s // CMP_BLOCK
    nq = s // SEL_Q_BLOCK
    kb = k.reshape(b, nb, CMP_BLOCK, kv, d).transpose(0, 3, 1, 2, 4)
    vb = v.reshape(b, nb, CMP_BLOCK, kv, d).transpose(0, 3, 1, 2, 4)
    bi = jnp.arange(b)[:, None, None, None]
    hi = jnp.arange(kv)[None, None, :, None]

    def blockwise(t):
        return t.reshape(t.shape[0], nq, SEL_Q_BLOCK, *t.shape[2:]).swapaxes(0, 1)

    def one(args):
        qc, pc, ic, vc = args
        return _sel_attend(qc, pc, kb[bi, hi, ic], vb[bi, hi, ic], ic, vc)

    o = lax.map(one, (blockwise(q), blockwise(pos), blockwise(idx), blockwise(valid)))
    return o.swapaxes(0, 1).reshape(b, s, h, d)


def _win_prompt(q, k, v):
    b, s, h, d = q.shape
    kv = k.shape[2]
    nb = s // WIN_Q_BLOCK
    nprev = WINDOW // WIN_Q_BLOCK
    nw = nprev + 1
    padw = ((0, 0), (WINDOW, 0), (0, 0), (0, 0))
    kp = jnp.pad(k, padw).reshape(b, nb + nprev, WIN_Q_BLOCK, kv, d)
    vp = jnp.pad(v, padw).reshape(b, nb + nprev, WIN_Q_BLOCK, kv, d)
    kw = jnp.concatenate([kp[:, i:i + nb] for i in range(nw)], axis=2)
    vw = jnp.concatenate([vp[:, i:i + nb] for i in range(nw)], axis=2)
    qpos = jnp.arange(s).reshape(nb, WIN_Q_BLOCK)
    kpos = (jnp.arange(nb)[:, None] - nprev) * WIN_Q_BLOCK + jnp.arange(nw * WIN_Q_BLOCK)[None]
    qq, kk = qpos[:, :, None], kpos[:, None, :]
    mask = (kk <= qq) & (qq - kk < WINDOW) & (kk >= 0)
    mask = jnp.broadcast_to(mask[None], (b,) + mask.shape).reshape(b * nb, WIN_Q_BLOCK, nw * WIN_Q_BLOCK)
    o, _ = _gqa_attend(q.reshape(b * nb, WIN_Q_BLOCK, h, d), kw.reshape(b * nb, nw * WIN_Q_BLOCK, kv, d),
                       vw.reshape(b * nb, nw * WIN_Q_BLOCK, kv, d), mask)
    return o.reshape(b, s, h, d)


def _gather_selected(pool, new_rows, page_table, idx):
    n, l, kv, d = new_rows.shape
    n_pages = page_table.shape[1]
    past_blocks = n_pages * PAGE_SIZE // CMP_BLOCK
    nbn = -(-l // CMP_BLOCK)
    newb = jnp.pad(new_rows, ((0, 0), (0, nbn * CMP_BLOCK - l), (0, 0), (0, 0)))
    newb = newb.reshape(n, nbn, CMP_BLOCK, kv, d).transpose(0, 3, 1, 2, 4)
    bi = jnp.arange(n)[:, None, None, None]
    hi = jnp.arange(kv)[None, None, :, None]
    start = idx * CMP_BLOCK
    phys = page_table[bi, jnp.minimum(start // PAGE_SIZE, n_pages - 1)]
    off = (start % PAGE_SIZE)[..., None] + jnp.arange(CMP_BLOCK)
    past = pool[phys[..., None], off, hi[..., None]]
    new = newb[bi, hi, jnp.clip(idx - past_blocks, 0, nbn - 1)]
    return jnp.where((idx < past_blocks)[..., None, None], past, new)


def _nsa_combine(gl, o_cmp, o_sel, o_win):
    n, l = gl.shape[0], gl.shape[1]
    g = jax.nn.sigmoid(gl.astype(F32)).reshape(n, l, NSA_HEADS, 3, 1)
    o = g[..., 0, :] * o_cmp + g[..., 1, :] * o_sel + g[..., 2, :] * o_win
    return o.reshape(n, l, NSA_W).astype(o_cmp.dtype)


def _prompt_mixer(x, win_buf, w_in, w_cmp_k, w_cmp_v, gn_g):
    n, s, _ = x.shape
    pos = jnp.arange(s)[None]
    rq, rk, rv, rg, nq, ck, cv, sk, sv, wk, wv, gl = _project(x, w_in)
    s0 = jnp.zeros((n, RET_HEADS, HEAD_DIM, HEAD_DIM), F32)
    ret_out, s_fin = _retention_group(rq, rk, rv, rg, pos, s0, gn_g)
    q, ck, sk, wk, cv, sv, wv = _nsa_heads(nq, ck, sk, wk, cv, sv, wv, pos)
    o_cmp, imp = _cmp_branch(q, pos, _compress(ck, w_cmp_k), _compress(cv, w_cmp_v))
    idx, valid = _select_blocks(imp, pos, s // CMP_BLOCK)
    o_sel = _sel_prompt(q, pos, sk, sv, idx, valid)
    o_win = _win_prompt(q, wk, wv)
    feats = jnp.concatenate([ret_out, _nsa_combine(gl, o_cmp, o_sel, o_win)], -1)
    if s >= win_buf:
        bk, bv = wk[:, s - win_buf:], wv[:, s - win_buf:]
    else:
        padb = ((0, 0), (win_buf - s, 0), (0, 0), (0, 0))
        bk, bv = jnp.pad(wk, padb), jnp.pad(wv, padb)
    return feats, (ck, cv, sk, sv, bk, bv, s_fin)


def _sample_mixer(x, c_cmp_k, c_cmp_v, c_sel_k, c_sel_v, c_win_k, c_win_v, s_ret, page_table,
                  w_in, w_cmp_k, w_cmp_v, gn_g):
    n, l, _ = x.shape
    past = page_table.shape[1] * PAGE_SIZE
    pos = past + jnp.arange(l)[None]
    rq, rk, rv, rg, nq, ck, cv, sk, sv, wk, wv, gl = _project(x, w_in)
    ret_out, s_new = _retention_group(rq, rk, rv, rg, pos, s_ret, gn_g)
    q, ck, sk, wk, cv, sv, wv = _nsa_heads(nq, ck, sk, wk, cv, sv, wv, pos)
    rows = (n, past, NSA_KV_HEADS, HEAD_DIM)
    ck_all = jnp.concatenate([c_cmp_k[page_table].reshape(rows), ck], 1)
    cv_all = jnp.concatenate([c_cmp_v[page_table].reshape(rows), cv], 1)
    t = past + l
    ncb = t // CMP_BLOCK
    kc = _compress(ck_all[:, :ncb * CMP_BLOCK], w_cmp_k)
    vc = _compress(cv_all[:, :ncb * CMP_BLOCK], w_cmp_v)
    o_cmp, imp = _cmp_branch(q, pos, kc, vc)
    idx, valid = _select_blocks(imp, pos, -(-t // CMP_BLOCK))
    ks = _gather_selected(c_sel_k, sk, page_table, idx)
    vs = _gather_selected(c_sel_v, sv, page_table, idx)
    o_sel = _sel_attend(q, pos, ks, vs, idx, valid)
    wb = c_win_k.shape[1]
    kw = jnp.concatenate([c_win_k, wk], 1)
    vw = jnp.concatenate([c_win_v, wv], 1)
    kpos = past - wb + jnp.arange(wb + l)
    mask = (kpos[None, None, :] <= pos[:, :, None]) & (pos[:, :, None] - kpos[None, None, :] < WINDOW)
    o_win, _ = _gqa_attend(q, kw, vw, mask)
    feats = jnp.concatenate([ret_out, _nsa_combine(gl, o_cmp, o_sel, o_win)], -1)
    return feats, (ck, cv, sk, sv, kw[:, l:], vw[:, l:], s_new)


def _hmoe(x, w_group, b_group, w_expert, b_expert, w_gate, w_up, w_down):
    shp = x.shape
    d = shp[-1]
    xt = x.reshape(-1, d)
    t = xt.shape[0]
    g_logit = (xt @ w_group + b_group).astype(F32)
    grp = jnp.argmax(g_logit, -1)
    p_grp = jnp.take_along_axis(jax.nn.softmax(g_logit, -1), grp[:, None], 1)
    e_logit = (xt @ w_expert + b_expert).astype(F32).reshape(t, N_GROUPS, EXPERTS_PER_GROUP)
    e_logit = jnp.take_along_axis(e_logit, grp[:, None, None], 1)[:, 0]
    p_top, i_top = lax.top_k(jax.nn.softmax(e_logit, -1), EXPERT_TOPK)
    gate = p_grp * p_top / p_top.sum(-1, keepdims=True)
    eid = (grp[:, None] * EXPERTS_PER_GROUP + i_top).reshape(-1)
    a = eid.shape[0]
    order = jnp.argsort(eid)
    e_sorted = eid[order]
    tok = order // EXPERT_TOPK
    counts = jnp.bincount(eid, length=N_EXPERTS)
    starts = jnp.cumsum(counts) - counts
    padded = (counts + MOE_BLOCK - 1) // MOE_BLOCK * MOE_BLOCK
    pad_end = jnp.cumsum(padded)
    pad_start = pad_end - padded
    dest = pad_start[e_sorted] + jnp.arange(a) - starts[e_sorted]
    n_blk = -(-(a + N_EXPERTS * (MOE_BLOCK - 1)) // MOE_BLOCK)
    buf = jnp.zeros((n_blk * MOE_BLOCK, d), xt.dtype).at[dest].set(xt[tok])
    blk_e = jnp.minimum(jnp.searchsorted(pad_end, jnp.arange(n_blk) * MOE_BLOCK, side='right'), N_EXPERTS - 1)

    def expert_block(args):
        xb, e = args
        hb = jax.nn.silu(xb @ w_gate[e]) * (xb @ w_up[e])
        return hb @ w_down[e]

    yb = lax.map(expert_block, (buf.reshape(n_blk, MOE_BLOCK, d), blk_e))
    y = yb.reshape(-1, d)[dest] * gate.reshape(-1)[order][:, None].astype(xt.dtype)
    return jnp.zeros_like(xt).at[tok].add(y).reshape(shp)


def _post_block(x, feats, w_o, ln1_g, ln1_b, moe, ln2_g, ln2_b):
    h = _layer_norm(DEEPNORM_ALPHA * x + feats @ w_o, ln1_g, ln1_b)
    return _layer_norm(DEEPNORM_ALPHA * h + _hmoe(h, *moe), ln2_g, ln2_b)


def kernel(x_prompt, x_sample, cache_cmp_k, cache_cmp_v, cache_sel_k, cache_sel_v, cache_win_k, cache_win_v,
           state_ret, page_table, w_in, w_cmp_k, w_cmp_v, ret_gn_g, w_o, ln1_g, ln1_b, w_group, b_group,
           w_expert, b_expert, w_gate, w_up, w_down, ln2_g, ln2_b):
    win_buf = cache_win_k.shape[2]
    hp, hs = x_prompt, x_sample
    acc_p = [[] for _ in range(7)]
    acc_s = [[] for _ in range(7)]
    for l in range(DEPTH):
        fp, st_p = _prompt_mixer(hp, win_buf, w_in[l], w_cmp_k[l], w_cmp_v[l], ret_gn_g[l])
        fs, st_s = _sample_mixer(hs, cache_cmp_k[l], cache_cmp_v[l], cache_sel_k[l], cache_sel_v[l],
                                 cache_win_k[l], cache_win_v[l], state_ret[l], page_table,
                                 w_in[l], w_cmp_k[l], w_cmp_v[l], ret_gn_g[l])
        moe = (w_group[l], b_group[l], w_expert[l], b_expert[l], w_gate[l], w_up[l], w_down[l])
        hp = _post_block(hp, fp, w_o[l], ln1_g[l], ln1_b[l], moe, ln2_g[l], ln2_b[l])
        hs = _post_block(hs, fs, w_o[l], ln1_g[l], ln1_b[l], moe, ln2_g[l], ln2_b[l])
        for acc, t in zip(acc_p, st_p):
            acc.append(t)
        for acc, t in zip(acc_s, st_s):
            acc.append(t)
    p_cmp_k, p_cmp_v, p_sel_k, p_sel_v, p_win_k, p_win_v, p_ret = [jnp.stack(a) for a in acc_p]
    s_cmp_k, s_cmp_v, s_sel_k, s_sel_v, s_win_k, s_win_v, s_ret = [jnp.stack(a) for a in acc_s]
    return (hp, hs, p_cmp_k, p_cmp_v, p_sel_k, p_sel_v, p_win_k, p_win_v, p_ret.astype(state_ret.dtype),
            s_cmp_k, s_cmp_v, s_sel_k, s_sel_v, s_win_k, s_win_v, s_ret.astype(state_ret.dtype))
```

```python
import functools
import math

import jax
import jax.numpy as jnp
import numpy as np
from jax import lax
from jax.experimental import pallas as pl
from jax.experimental.pallas import tpu as pltpu

D_MODEL = 2048
DEPTH = 1
PAGE_SIZE = 128

F32 = jnp.float32
BF16 = jnp.bfloat16
HEAD_DIM = 128
RET_HEADS = D_MODEL // (2 * HEAD_DIM)
NSA_HEADS = D_MODEL // (2 * HEAD_DIM)
NSA_KV_HEADS = 2
RET_W = RET_HEADS * HEAD_DIM
NSA_W = NSA_HEADS * HEAD_DIM
KV_W = NSA_KV_HEADS * HEAD_DIM
MIX_W = RET_W + NSA_W
RET_CHUNK = 128
RET_ROPE_THETA = 10000.0
ROPE_THETA = 500000.0
ROT_DIM = HEAD_DIM // 4
CMP_BLOCK = 64
SEL_TOPK = 16
WINDOW = 512
WIN_Q_BLOCK = 128
SEL_Q_BLOCK = 64
N_GROUPS = 4
EXPERTS_PER_GROUP = 8
N_EXPERTS = N_GROUPS * EXPERTS_PER_GROUP
EXPERT_TOPK = 2
D_EXPERT = 512
MOE_BLOCK = 128
LN_EPS = 1e-5
GN_EPS = 1e-5
NEG = -1e30
DEEPNORM_ALPHA = (2 * DEPTH) ** 0.25
DEEPNORM_BETA = (8 * DEPTH) ** -0.25
SPLITS = (RET_W, RET_W, RET_W, RET_W, NSA_W, KV_W, KV_W, KV_W, KV_W, KV_W, KV_W, NSA_HEADS * 3)
IN_W = sum(SPLITS)
GATE_W = NSA_HEADS * 3
MAIN_W = IN_W - GATE_W
LANES = 128
VMEM_LIMIT = 48 * 1024 * 1024


def _matmul_kernel(x_ref, w_ref, o_ref):
    o_ref[...] = jnp.dot(x_ref[...].astype(BF16), w_ref[...].astype(BF16), preferred_element_type=F32)


def _matmul(x, w, n_out, tm, tn, name):
    t, k = x.shape
    return pl.pallas_call(
        _matmul_kernel,
        out_shape=jax.ShapeDtypeStruct((t, n_out), F32),
        grid=(t // tm, n_out // tn),
        in_specs=[pl.BlockSpec((tm, k), lambda i, j: (i, 0)),
                  pl.BlockSpec((k, tn), lambda i, j: (0, j))],
        out_specs=pl.BlockSpec((tm, tn), lambda i, j: (i, j)),
        compiler_params=pltpu.CompilerParams(dimension_semantics=("arbitrary", "arbitrary"),
                                             vmem_limit_bytes=VMEM_LIMIT),
        name=name,
    )(x, w)


def _project(x, w_in):
    n, l, d = x.shape
    xt = x.reshape(n * l, d)
    tm = min(512, n * l)
    main = _matmul(xt, w_in, MAIN_W, tm, 512, "in_proj")
    gpg = GATE_W // NSA_KV_HEADS
    w_tail = jnp.concatenate(
        [jnp.pad(w_in[:, MAIN_W + k * gpg:MAIN_W + (k + 1) * gpg], ((0, 0), (0, LANES - gpg)))
         for k in range(NSA_KV_HEADS)], axis=1)
    gl_tiles = _matmul(xt, w_tail, NSA_KV_HEADS * LANES, tm, NSA_KV_HEADS * LANES, "gate_proj")
    cuts = [int(c) for c in np.cumsum(SPLITS)[:-2]]
    parts = [p.reshape(n, l, -1) for p in jnp.split(main, cuts, axis=-1)]
    return parts + [gl_tiles]


def _gate_logits(gl_tiles, n, l):
    gpg = GATE_W // NSA_KV_HEADS
    cols = [gl_tiles[:, k * LANES:k * LANES + gpg] for k in range(NSA_KV_HEADS)]
    return jnp.concatenate(cols, axis=1).reshape(n, l, GATE_W)


def _compress_kernel(k_ref, v_ref, wk_ref, wv_ref, ko_ref, vo_ref):
    r = ko_ref.shape[0]
    ko_ref[...] = jnp.sum(k_ref[...].reshape(r, CMP_BLOCK, KV_W) * wk_ref[...][None], axis=1)
    vo_ref[...] = jnp.sum(v_ref[...].reshape(r, CMP_BLOCK, KV_W) * wv_ref[...][None], axis=1)


def _compress_prompt(ck, cv, w_cmp_k, w_cmp_v):
    t = ck.shape[0]
    r = 32
    wk2 = jnp.tile(w_cmp_k, (1, NSA_KV_HEADS))
    wv2 = jnp.tile(w_cmp_v, (1, NSA_KV_HEADS))
    row = pl.BlockSpec((r * CMP_BLOCK, KV_W), lambda i: (i, 0))
    wsp = pl.BlockSpec((CMP_BLOCK, KV_W), lambda i: (0, 0))
    osp = pl.BlockSpec((r, KV_W), lambda i: (i, 0))
    return pl.pallas_call(
        _compress_kernel,
        out_shape=[jax.ShapeDtypeStruct((t // CMP_BLOCK, KV_W), F32)] * 2,
        grid=(t // (r * CMP_BLOCK),),
        in_specs=[row, row, wsp, wsp],
        out_specs=[osp, osp],
        compiler_params=pltpu.CompilerParams(dimension_semantics=("arbitrary",)),
        name="compress_prompt",
    )(ck, cv, wk2, wv2)


PAGES_PER_STEP = 8
BLOCKS_PER_PAGE = PAGE_SIZE // CMP_BLOCK


def _compress_paged_kernel(pt_ref, *refs):
    del pt_ref
    k_pages = refs[:PAGES_PER_STEP]
    v_pages = refs[PAGES_PER_STEP:2 * PAGES_PER_STEP]
    wk_ref, wv_ref, ko_ref, vo_ref = refs[2 * PAGES_PER_STEP:]
    wk = wk_ref[...][None]
    wv = wv_ref[...][None]
    ks = [jnp.sum(p[0].reshape(BLOCKS_PER_PAGE, CMP_BLOCK, KV_W) * wk, axis=1) for p in k_pages]
    vs = [jnp.sum(p[0].reshape(BLOCKS_PER_PAGE, CMP_BLOCK, KV_W) * wv, axis=1) for p in v_pages]
    ko_ref[0] = jnp.concatenate(ks, axis=0)
    vo_ref[0] = jnp.concatenate(vs, axis=0)


def _compress_paged(pool_k, pool_v, page_table, w_cmp_k, w_cmp_v):
    n, n_pages = page_table.shape
    n_pool = pool_k.shape[0]
    pk = pool_k.reshape(n_pool, PAGE_SIZE, KV_W)
    pv = pool_v.reshape(n_pool, PAGE_SIZE, KV_W)
    wk2 = jnp.tile(w_cmp_k, (1, NSA_KV_HEADS))
    wv2 = jnp.tile(w_cmp_v, (1, NSA_KV_HEADS))

    def page_spec(j):
        return pl.BlockSpec((1, PAGE_SIZE, KV_W), lambda s, i, pt: (pt[s, i * PAGES_PER_STEP + j], 0, 0))

    wsp = pl.BlockSpec((CMP_BLOCK, KV_W), lambda s, i, pt: (0, 0))
    rows = PAGES_PER_STEP * BLOCKS_PER_PAGE
    osp = pl.BlockSpec((1, rows, KV_W), lambda s, i, pt: (s, i, 0))
    specs = [page_spec(j) for j in range(PAGES_PER_STEP)]
    return pl.pallas_call(
        _compress_paged_kernel,
        out_shape=[jax.ShapeDtypeStruct((n, n_pages * BLOCKS_PER_PAGE, KV_W), F32)] * 2,
        grid_spec=pltpu.PrefetchScalarGridSpec(
            num_scalar_prefetch=1,
            grid=(n, n_pages // PAGES_PER_STEP),
            in_specs=specs + specs + [wsp, wsp],
            out_specs=[osp, osp]),
        compiler_params=pltpu.CompilerParams(dimension_semantics=("arbitrary", "arbitrary")),
        name="compress_paged",
    )(page_table, *([pk] * PAGES_PER_STEP), *([pv] * PAGES_PER_STEP), wk2, wv2)


NSA_TQ = 128
NSA_KEY_CHUNK = 512
NSA_GROUP = NSA_HEADS // NSA_KV_HEADS


def _dot_nt(a, b):
    return lax.dot_general(a, b, (((1,), (1,)), ((), ())), preferred_element_type=F32)


def _masked_softmax(s, m):
    sm = jnp.where(m, s, NEG)
    e = jnp.exp(sm - jnp.max(sm, axis=-1, keepdims=True))
    return jnp.where(m, e / jnp.sum(e, axis=-1, keepdims=True), 0.0)


def _select_mask(imp, pos):
    nb = imp.shape[1]
    blk = lax.broadcasted_iota(jnp.int32, (1, nb), 1)
    cur = jnp.right_shift(pos, int(math.log2(CMP_BLOCK)))
    forced = (blk == 0) | (blk == cur) | (blk == cur - 1)
    score = jnp.where(blk > cur, NEG, jnp.where(forced, -NEG, imp))
    rank = jnp.zeros(score.shape, jnp.int32)
    for i in range(nb):
        si = score[:, i:i + 1]
        ahead = (si > score) | ((si == score) & (blk > i))
        rank = rank + ahead.astype(jnp.int32)
    return (rank < SEL_TOPK) & (score > 0.5 * NEG)


def _nsa_prompt_kernel(q_ref, kc_ref, vc_ref, sk_ref, sv_ref, wk_ref, wv_ref, gl_ref, o_ref):
    tq, g, kc_n = NSA_TQ, NSA_GROUP, NSA_KEY_CHUNK
    qi = pl.program_id(2)
    q4 = q_ref[...] * (HEAD_DIM ** -0.5)
    qs = jnp.concatenate([q4[:, i * HEAD_DIM:(i + 1) * HEAD_DIM] for i in range(g)], axis=0).astype(BF16)
    pos = qi * tq + lax.broadcasted_iota(jnp.int32, (tq, 1), 0)
    pos4 = jnp.concatenate([pos] * g, axis=0)

    nb = kc_ref.shape[1]
    s_c = _dot_nt(qs, kc_ref[0].astype(BF16))
    blk = lax.broadcasted_iota(jnp.int32, (1, nb), 1)
    p_c = _masked_softmax(s_c, (blk + 1) * CMP_BLOCK - 1 <= pos4)
    o_cmp = jnp.dot(p_c.astype(BF16), vc_ref[0].astype(BF16), preferred_element_type=F32)
    imp = p_c[0:tq]
    for i in range(1, g):
        imp = imp + p_c[i * tq:(i + 1) * tq]

    sel = _select_mask(imp, pos).astype(BF16)
    sel4 = jnp.concatenate([sel] * g, axis=0)
    blk_col = lax.broadcasted_iota(jnp.int32, (nb, kc_n), 0)
    key_col = lax.broadcasted_iota(jnp.int32, (nb, kc_n), 1)
    key_row = lax.broadcasted_iota(jnp.int32, (1, kc_n), 1)

    def chunk(c, carry):
        m_i, l_i, acc = carry
        k0 = pl.multiple_of(c * kc_n, kc_n)
        s = _dot_nt(qs, sk_ref[pl.ds(k0, kc_n), :].astype(BF16))
        expand = (jnp.right_shift(key_col + k0, int(math.log2(CMP_BLOCK))) == blk_col).astype(BF16)
        chosen = jnp.dot(sel4, expand, preferred_element_type=F32)
        msk = (chosen > 0.5) & (key_row + k0 <= pos4)
        sm = jnp.where(msk, s, NEG)
        m_new = jnp.maximum(m_i, jnp.max(sm, axis=-1, keepdims=True))
        alpha = jnp.exp(m_i - m_new)
        p = jnp.where(msk, jnp.exp(sm - m_new), 0.0)
        l_new = alpha * l_i + jnp.sum(p, axis=-1, keepdims=True)
        pv = jnp.dot(p.astype(BF16), sv_ref[pl.ds(k0, kc_n), :].astype(BF16), preferred_element_type=F32)
        return m_new, l_new, alpha * acc + pv

    n_chunks = (qi * tq + tq + kc_n - 1) // kc_n
    init = (jnp.full((g * tq, 1), NEG, F32), jnp.zeros((g * tq, 1), F32), jnp.zeros((g * tq, HEAD_DIM), F32))
    _, l_f, acc_f = lax.fori_loop(0, n_chunks, chunk, init)
    o_sel = acc_f / l_f

    nwk = WINDOW + tq
    kstart = pl.multiple_of(jnp.maximum(qi * tq - WINDOW, 0), tq)
    s_w = _dot_nt(qs, wk_ref[pl.ds(kstart, nwk), :].astype(BF16))
    kpos = kstart + lax.broadcasted_iota(jnp.int32, (1, nwk), 1)
    p_w = _masked_softmax(s_w, (kpos <= pos4) & (pos4 - kpos < WINDOW))
    o_win = jnp.dot(p_w.astype(BF16), wv_ref[pl.ds(kstart, nwk), :].astype(BF16), preferred_element_type=F32)

    gate = 1.0 / (1.0 + jnp.exp(-gl_ref[...]))
    for i in range(g):
        rows = slice(i * tq, (i + 1) * tq)
        o_ref[:, i * HEAD_DIM:(i + 1) * HEAD_DIM] = (gate[:, 3 * i:3 * i + 1] * o_cmp[rows]
                                                     + gate[:, 3 * i + 1:3 * i + 2] * o_sel[rows]
                                                     + gate[:, 3 * i + 2:3 * i + 3] * o_win[rows])


def _nsa_prompt(q, kc, vc, sk, sv, wk, wv, gl, batch, seq):
    nq = seq // NSA_TQ
    gw = NSA_GROUP * HEAD_DIM
    row = lambda b, k, i: (b * nq + i, k)
    seq_spec = pl.BlockSpec((seq, HEAD_DIM), lambda b, k, i: (b, k))
    cmp_spec = pl.BlockSpec((1, seq // CMP_BLOCK, HEAD_DIM), lambda b, k, i: (b, 0, k))
    return pl.pallas_call(
        _nsa_prompt_kernel,
        out_shape=jax.ShapeDtypeStruct((batch * seq, NSA_W), F32),
        grid=(batch, NSA_KV_HEADS, nq),
        in_specs=[pl.BlockSpec((NSA_TQ, gw), row), cmp_spec, cmp_spec,
                  seq_spec, seq_spec, seq_spec, seq_spec,
                  pl.BlockSpec((NSA_TQ, LANES), row)],
        out_specs=pl.BlockSpec((NSA_TQ, gw), row),
        compiler_params=pltpu.CompilerParams(dimension_semantics=("arbitrary",) * 3,
                                             vmem_limit_bytes=VMEM_LIMIT),
        name="nsa_prompt",
    )(q, kc, vc, sk, sv, wk, wv, gl)


WO_TM = 256


def _wo_ln_kernel(fr_ref, fn_ref, x_ref, w_ref, g_ref, b_ref, o_ref):
    y = (DEEPNORM_ALPHA * x_ref[...]
         + jnp.dot(fr_ref[...].astype(BF16), w_ref[0:RET_W, :], preferred_element_type=F32)
         + jnp.dot(fn_ref[...].astype(BF16), w_ref[RET_W:MIX_W, :], preferred_element_type=F32))
    mu = jnp.mean(y, axis=-1, keepdims=True)
    var = jnp.mean(jnp.square(y - mu), axis=-1, keepdims=True)
    o_ref[...] = (y - mu) * lax.rsqrt(var + LN_EPS) * g_ref[...] + b_ref[...]


def _wo_ln(f_ret, f_nsa, x, w_o_bf16, ln_g, ln_b):
    t, d = x.shape
    tm = min(WO_TM, t)
    assert t % tm == 0
    row = lambda w: pl.BlockSpec((tm, w), lambda i: (i, 0))
    vec = pl.BlockSpec((1, d), lambda i: (0, 0))
    return pl.pallas_call(
        _wo_ln_kernel,
        out_shape=jax.ShapeDtypeStruct((t, d), F32),
        grid=(t // tm,),
        in_specs=[row(RET_W), row(NSA_W), row(d), pl.BlockSpec(w_o_bf16.shape, lambda i: (0, 0)), vec, vec],
        out_specs=row(d),
        compiler_params=pltpu.CompilerParams(dimension_semantics=("arbitrary",), vmem_limit_bytes=VMEM_LIMIT),
        name="wo_ln1",
    )(f_ret, f_nsa, x, w_o_bf16, ln_g.reshape(1, d), ln_b.reshape(1, d))


MOE_BM = 256


def _expert_kernel(blk_e_ref, n_used_ref, x_ref, wg_ref, wu_ref, wd_ref, o_ref):
    del blk_e_ref
    i = pl.program_id(0)

    @pl.when(i < n_used_ref[0])
    def _():
        xb = x_ref[...].astype(BF16)
        hg = jnp.dot(xb, wg_ref[0].astype(BF16), preferred_element_type=F32)
        hu = jnp.dot(xb, wu_ref[0].astype(BF16), preferred_element_type=F32)
        hb = hg * (1.0 / (1.0 + jnp.exp(-hg))) * hu
        o_ref[...] = jnp.dot(hb.astype(BF16), wd_ref[0].astype(BF16), preferred_element_type=F32)

    @pl.when(i >= n_used_ref[0])
    def _():
        o_ref[...] = jnp.zeros(o_ref.shape, F32)


def _expert_ffn(buf, blk_e, n_used, w_gate, w_up, w_down):
    rows, d = buf.shape
    n_blk = rows // MOE_BM
    de = w_gate.shape[2]
    xsp = pl.BlockSpec((MOE_BM, d), lambda i, be, nu: (i, 0))
    return pl.pallas_call(
        _expert_kernel,
        out_shape=jax.ShapeDtypeStruct((rows, d), F32),
        grid_spec=pltpu.PrefetchScalarGridSpec(
            num_scalar_prefetch=2,
            grid=(n_blk,),
            in_specs=[xsp,
                      pl.BlockSpec((1, d, de), lambda i, be, nu: (be[i], 0, 0)),
                      pl.BlockSpec((1, d, de), lambda i, be, nu: (be[i], 0, 0)),
                      pl.BlockSpec((1, de, d), lambda i, be, nu: (be[i], 0, 0))],
            out_specs=xsp),
        compiler_params=pltpu.CompilerParams(dimension_semantics=("arbitrary",), vmem_limit_bytes=VMEM_LIMIT),
        name="expert_ffn",
    )(blk_e, n_used, buf, w_gate, w_up, w_down)


def _layer_norm(x, g, b):
    xf = x.astype(F32)
    mu = xf.mean(-1, keepdims=True)
    var = jnp.square(xf - mu).mean(-1, keepdims=True)
    return ((xf - mu) * lax.rsqrt(var + LN_EPS) * g + b).astype(x.dtype)


def _rope(x, pos, rot_dim, theta):
    half = rot_dim // 2
    inv = theta ** (-jnp.arange(0, rot_dim, 2, dtype=F32) / rot_dim)
    ang = pos[..., None].astype(F32) * inv
    cos = jnp.cos(ang)[:, :, None, :]
    sin = jnp.sin(ang)[:, :, None, :]
    xr = x[..., :rot_dim].astype(F32)
    x1, x2 = xr[..., :half], xr[..., half:]
    rot = jnp.concatenate([x1 * cos - x2 * sin, x2 * cos + x1 * sin], -1).astype(x.dtype)
    return jnp.concatenate([rot, x[..., rot_dim:]], -1)


def _heads(t, n):
    return t.reshape(t.shape[0], t.shape[1], n, HEAD_DIM)


def _chunk_retention(q, k, v, s0):
    n, l, h, d = q.shape
    c = RET_CHUNK if l % RET_CHUNK == 0 else l
    nc = l // c
    log_g = jnp.log1p(-jnp.exp2(-5.0 - jnp.arange(h, dtype=F32)))
    i = jnp.arange(c, dtype=F32)
    rel = i[:, None] - i[None, :]
    dmask = jnp.where(rel[None] >= 0, jnp.exp(jnp.maximum(rel[None], 0.0) * log_g[:, None, None]), 0.0)
    q_dec = jnp.exp((i + 1.0)[None] * log_g[:, None])[..., None]
    k_dec = jnp.exp((c - 1.0 - i)[None] * log_g[:, None])[..., None]
    c_dec = jnp.exp(c * log_g)[:, None, None]

    def to_chunks(t):
        return t.astype(F32).reshape(n, nc, c, h, d).transpose(1, 0, 3, 2, 4)

    def step(s, qkv):
        qc, kc, vc = qkv
        att = jnp.einsum('bhid,bhjd->bhij', qc, kc) * dmask
        o = jnp.einsum('bhij,bhjd->bhid', att, vc) + jnp.einsum('bhid,bhde->bhie', qc * q_dec, s)
        s = c_dec * s + jnp.einsum('bhjd,bhje->bhde', kc * k_dec, vc)
        return s, o

    s, o = lax.scan(step, s0.astype(F32), (to_chunks(q), to_chunks(k), to_chunks(v)))
    return o.transpose(1, 0, 3, 2, 4).reshape(n, l, h, d), s


def _retention_group(rq, rk, rv, rg, pos, s0, gn_g):
    q = _rope(_heads(rq, RET_HEADS), pos, HEAD_DIM, RET_ROPE_THETA)
    k = _rope(_heads(rk, RET_HEADS), pos, HEAD_DIM, RET_ROPE_THETA) * (HEAD_DIM ** -0.5)
    v = _heads(rv, RET_HEADS)
    o, s = _chunk_retention(q, k, v, s0)
    mu = o.mean(-1, keepdims=True)
    var = jnp.square(o - mu).mean(-1, keepdims=True)
    on = (o - mu) * lax.rsqrt(var + GN_EPS) * gn_g.reshape(RET_HEADS, HEAD_DIM).astype(F32)
    out = jax.nn.silu(rg.astype(F32)) * on.reshape(rg.shape)
    return out.astype(rq.dtype), s


def _gqa_attend(q, k, v, mask):
    n, lq, h, d = q.shape
    kv = k.shape[2]
    qg = q.reshape(n, lq, kv, h // kv, d)
    s = jnp.einsum('nqkgd,nskd->nkgqs', qg, k).astype(F32) * (d ** -0.5)
    m = mask[:, None, None]
    p = jax.nn.softmax(jnp.where(m, s, NEG), axis=-1) * m
    o = jnp.einsum('nkgqs,nskd->nqkgd', p.astype(v.dtype), v)
    return o.reshape(n, lq, h, d), p


def _nsa_heads(nq, ck, sk, wk, cv, sv, wv, pos):
    rp = lambda t, nh: _rope(_heads(t, nh), pos, ROT_DIM, ROPE_THETA)
    return (rp(nq, NSA_HEADS), rp(ck, NSA_KV_HEADS), rp(sk, NSA_KV_HEADS), rp(wk, NSA_KV_HEADS),
            _heads(cv, NSA_KV_HEADS), _heads(sv, NSA_KV_HEADS), _heads(wv, NSA_KV_HEADS))


def _compress(rows, w):
    n, t, kv, d = rows.shape
    return jnp.einsum('nbjkd,jd->nbkd', rows.reshape(n, t // CMP_BLOCK, CMP_BLOCK, kv, d), w)


def _cmp_branch(q, pos, kc, vc):
    nb = kc.shape[1]
    blk_end = (jnp.arange(nb) + 1) * CMP_BLOCK - 1
    mask = blk_end[None, None, :] <= pos[:, :, None]
    o, p = _gqa_attend(q, kc, vc, mask)
    imp = p.sum(axis=2).transpose(0, 2, 1, 3)
    return o, imp


def _select_blocks(imp, pos, n_sel):
    nb = imp.shape[-1]
    imp = jnp.pad(imp, ((0, 0), (0, 0), (0, 0), (0, n_sel - nb)))
    blk = jnp.arange(n_sel)
    cur = (pos // CMP_BLOCK)[:, :, None, None]
    forced = (blk == 0) | (blk == cur) | (blk == cur - 1)
    score = jnp.where(blk > cur, NEG, jnp.where(forced, -NEG, imp))
    top, idx = lax.top_k(score, min(SEL_TOPK, n_sel))
    return idx, top > 0.5 * NEG


def _sel_attend(q, pos, ks, vs, idx, valid):
    n, lq, kv, kk, cb, d = ks.shape
    h = q.shape[2]
    kpos = idx[..., None] * CMP_BLOCK + jnp.arange(CMP_BLOCK)
    m = ((kpos <= pos[:, :, None, None, None]) & valid[..., None]).reshape(n, lq, kv, 1, kk * cb)
    qg = q.reshape(n, lq, kv, h // kv, d)
    kf = ks.reshape(n, lq, kv, kk * cb, d)
    vf = vs.reshape(n, lq, kv, kk * cb, d)
    s = jnp.einsum('nqkgd,nqkjd->nqkgj', qg, kf).astype(F32) * (d ** -0.5)
    p = jax.nn.softmax(jnp.where(m, s, NEG), axis=-1) * m
    o = jnp.einsum('nqkgj,nqkjd->nqkgd', p.astype(vf.dtype), vf)
    return o.reshape(n, lq, h, d)


def _sel_prompt(q, pos, k, v, idx, valid):
    b, s, h, d = q.shape
    kv = k.shape[2]
    nb = s // CMP_BLOCK
    nq = s // SEL_Q_BLOCK
    kb = k.reshape(b, nb, CMP_BLOCK, kv, d).transpose(0, 3, 1, 2, 4)
    vb = v.reshape(b, nb, CMP_BLOCK, kv, d).transpose(0, 3, 1, 2, 4)
    bi = jnp.arange(b)[:, None, None, None]
    hi = jnp.arange(kv)[None, None, :, None]

    def blockwise(t):
        return t.reshape(t.shape[0], nq, SEL_Q_BLOCK, *t.shape[2:]).swapaxes(0, 1)

    def one(args):
        qc, pc, ic, vc = args
        return _sel_attend(qc, pc, kb[bi, hi, ic], vb[bi, hi, ic], ic, vc)

    o = lax.map(one, (blockwise(q), blockwise(pos), blockwise(idx), blockwise(valid)))
    return o.swapaxes(0, 1).reshape(b, s, h, d)


def _win_prompt(q, k, v):
    b, s, h, d = q.shape
    kv = k.shape[2]
    nb = s // WIN_Q_BLOCK
    nprev = WINDOW // WIN_Q_BLOCK
    nw = nprev + 1
    padw = ((0, 0), (WINDOW, 0), (0, 0), (0, 0))
    kp = jnp.pad(k, padw).reshape(b, nb + nprev, WIN_Q_BLOCK, kv, d)
    vp = jnp.pad(v, padw).reshape(b, nb + nprev, WIN_Q_BLOCK, kv, d)
    kw = jnp.concatenate([kp[:, i:i + nb] for i in range(nw)], axis=2)
    vw = jnp.concatenate([vp[:, i:i + nb] for i in range(nw)], axis=2)
    qpos = jnp.arange(s).reshape(nb, WIN_Q_BLOCK)
    kpos = (jnp.arange(nb)[:, None] - nprev) * WIN_Q_BLOCK + jnp.arange(nw * WIN_Q_BLOCK)[None]
    qq, kk = qpos[:, :, None], kpos[:, None, :]
    mask = (kk <= qq) & (qq - kk < WINDOW) & (kk >= 0)
    mask = jnp.broadcast_to(mask[None], (b,) + mask.shape).reshape(b * nb, WIN_Q_BLOCK, nw * WIN_Q_BLOCK)
    o, _ = _gqa_attend(q.reshape(b * nb, WIN_Q_BLOCK, h, d), kw.reshape(b * nb, nw * WIN_Q_BLOCK, kv, d),
                       vw.reshape(b * nb, nw * WIN_Q_BLOCK, kv, d), mask)
    return o.reshape(b, s, h, d)


def _gather_selected(pool, new_rows, page_table, idx):
    n, l, kv, d = new_rows.shape
    n_pages = page_table.shape[1]
    past_blocks = n_pages * PAGE_SIZE // CMP_BLOCK
    nbn = -(-l // CMP_BLOCK)
    newb = jnp.pad(new_rows, ((0, 0), (0, nbn * CMP_BLOCK - l), (0, 0), (0, 0)))
    newb = newb.reshape(n, nbn, CMP_BLOCK, kv, d).transpose(0, 3, 1, 2, 4)
    bi = jnp.arange(n)[:, None, None, None]
    hi = jnp.arange(kv)[None, None, :, None]
    start = idx * CMP_BLOCK
    phys = page_table[bi, jnp.minimum(start // PAGE_SIZE, n_pages - 1)]
    off = (start % PAGE_SIZE)[..., None] + jnp.arange(CMP_BLOCK)
    past = pool[phys[..., None], off, hi[..., None]]
    new = newb[bi, hi, jnp.clip(idx - past_blocks, 0, nbn - 1)]
    return jnp.where((idx < past_blocks)[..., None, None], past, new)


def _nsa_combine(gl, o_cmp, o_sel, o_win):
    n, l = gl.shape[0], gl.shape[1]
    g = jax.nn.sigmoid(gl.astype(F32)).reshape(n, l, NSA_HEADS, 3, 1)
    o = g[..., 0, :] * o_cmp + g[..., 1, :] * o_sel + g[..., 2, :] * o_win
    return o.reshape(n, l, NSA_W).astype(o_cmp.dtype)


def _prompt_mixer(x, win_buf, w_in, w_cmp_k, w_cmp_v, gn_g):
    n, s, _ = x.shape
    pos = jnp.arange(s)[None]
    rq, rk, rv, rg, nq, ck, cv, sk, sv, wk, wv, gl_tiles = _project(x, w_in)
    s0 = jnp.zeros((n, RET_HEADS, HEAD_DIM, HEAD_DIM), F32)
    ret_out, s_fin = _retention_group(rq, rk, rv, rg, pos, s0, gn_g)
    q, ck, sk, wk, cv, sv, wv = _nsa_heads(nq, ck, sk, wk, cv, sv, wv, pos)
    flat = lambda a: a.reshape(n * s, -1)
    kc, vc = _compress_prompt(flat(ck), flat(cv), w_cmp_k, w_cmp_v)
    nsa = _nsa_prompt(flat(q), kc.reshape(n, s // CMP_BLOCK, KV_W), vc.reshape(n, s // CMP_BLOCK, KV_W),
                      flat(sk), flat(sv), flat(wk), flat(wv), gl_tiles, n, s)
    feats = (flat(ret_out), nsa)
    if s >= win_buf:
        bk, bv = wk[:, s - win_buf:], wv[:, s - win_buf:]
    else:
        padb = ((0, 0), (win_buf - s, 0), (0, 0), (0, 0))
        bk, bv = jnp.pad(wk, padb), jnp.pad(wv, padb)
    return feats, (ck, cv, sk, sv, bk, bv, s_fin)


def _sample_mixer(x, c_cmp_k, c_cmp_v, c_sel_k, c_sel_v, c_win_k, c_win_v, s_ret, page_table,
                  w_in, w_cmp_k, w_cmp_v, gn_g):
    n, l, _ = x.shape
    past = page_table.shape[1] * PAGE_SIZE
    pos = past + jnp.arange(l)[None]
    rq, rk, rv, rg, nq, ck, cv, sk, sv, wk, wv, gl_tiles = _project(x, w_in)
    gl = _gate_logits(gl_tiles, n, l)
    ret_out, s_new = _retention_group(rq, rk, rv, rg, pos, s_ret, gn_g)
    q, ck, sk, wk, cv, sv, wv = _nsa_heads(nq, ck, sk, wk, cv, sv, wv, pos)
    t = past + l
    ncb = t // CMP_BLOCK
    assert past % CMP_BLOCK == 0 and ncb * CMP_BLOCK == past
    kc, vc = _compress_paged(c_cmp_k, c_cmp_v, page_table, w_cmp_k, w_cmp_v)
    kc = kc.reshape(n, ncb, NSA_KV_HEADS, HEAD_DIM)
    vc = vc.reshape(n, ncb, NSA_KV_HEADS, HEAD_DIM)
    o_cmp, imp = _cmp_branch(q, pos, kc, vc)
    idx, valid = _select_blocks(imp, pos, -(-t // CMP_BLOCK))
    ks = _gather_selected(c_sel_k, sk, page_table, idx)
    vs = _gather_selected(c_sel_v, sv, page_table, idx)
    o_sel = _sel_attend(q, pos, ks, vs, idx, valid)
    wb = c_win_k.shape[1]
    kw = jnp.concatenate([c_win_k, wk], 1)
    vw = jnp.concatenate([c_win_v, wv], 1)
    kpos = past - wb + jnp.arange(wb + l)
    mask = (kpos[None, None, :] <= pos[:, :, None]) & (pos[:, :, None] - kpos[None, None, :] < WINDOW)
    o_win, _ = _gqa_attend(q, kw, vw, mask)
    feats = (ret_out.reshape(n * l, RET_W), _nsa_combine(gl, o_cmp, o_sel, o_win).reshape(n * l, NSA_W))
    return feats, (ck, cv, sk, sv, kw[:, l:], vw[:, l:], s_new)


def _hmoe(x, w_group, b_group, w_expert, b_expert, w_gate, w_up, w_down):
    shp = x.shape
    d = shp[-1]
    xt = x.reshape(-1, d)
    t = xt.shape[0]
    g_logit = (xt @ w_group + b_group).astype(F32)
    grp = jnp.argmax(g_logit, -1)
    p_grp = jnp.take_along_axis(jax.nn.softmax(g_logit, -1), grp[:, None], 1)
    e_logit = (xt @ w_expert + b_expert).astype(F32).reshape(t, N_GROUPS, EXPERTS_PER_GROUP)
    e_logit = jnp.take_along_axis(e_logit, grp[:, None, None], 1)[:, 0]
    p_top, i_top = lax.top_k(jax.nn.softmax(e_logit, -1), EXPERT_TOPK)
    gate = p_grp * p_top / p_top.sum(-1, keepdims=True)
    eid = (grp[:, None] * EXPERTS_PER_GROUP + i_top).reshape(-1)
    a = eid.shape[0]
    order = jnp.argsort(eid)
    e_sorted = eid[order]
    tok = order // EXPERT_TOPK
    counts = jnp.bincount(eid, length=N_EXPERTS)
    starts = jnp.cumsum(counts) - counts
    padded = (counts + MOE_BM - 1) // MOE_BM * MOE_BM
    pad_end = jnp.cumsum(padded)
    pad_start = pad_end - padded
    dest = pad_start[e_sorted] + jnp.arange(a) - starts[e_sorted]
    n_blk = -(-(a + N_EXPERTS * (MOE_BM - 1)) // MOE_BM)
    src = jnp.zeros((n_blk * MOE_BM,), jnp.int32).at[dest].set(tok.astype(jnp.int32))
    blk_e = jnp.minimum(jnp.searchsorted(pad_end, jnp.arange(n_blk) * MOE_BM, side='right'), N_EXPERTS - 1)
    n_used = (pad_end[-1:] // MOE_BM).astype(jnp.int32)
    yb = _expert_ffn(xt[src], blk_e.astype(jnp.int32), n_used, w_gate, w_up, w_down)
    slot = jnp.zeros((a,), jnp.int32).at[order].set(dest.astype(jnp.int32))
    y = yb[slot] * gate.reshape(-1)[:, None]
    return y.reshape(t, EXPERT_TOPK, d).sum(axis=1).reshape(shp)


def kernel(x_prompt, x_sample, cache_cmp_k, cache_cmp_v, cache_sel_k, cache_sel_v, cache_win_k, cache_win_v,
           state_ret, page_table, w_in, w_cmp_k, w_cmp_v, ret_gn_g, w_o, ln1_g, ln1_b, w_group, b_group,
           w_expert, b_expert, w_gate, w_up, w_down, ln2_g, ln2_b):
    win_buf = cache_win_k.shape[2]
    hp, hs = x_prompt, x_sample
    acc_p = [[] for _ in range(7)]
    acc_s = [[] for _ in range(7)]
    for l in range(DEPTH):
        fp, st_p = _prompt_mixer(hp, win_buf, w_in[l], w_cmp_k[l], w_cmp_v[l], ret_gn_g[l])
        fs, st_s = _sample_mixer(hs, cache_cmp_k[l], cache_cmp_v[l], cache_sel_k[l], cache_sel_v[l],
                                 cache_win_k[l], cache_win_v[l], state_ret[l], page_table,
                                 w_in[l], w_cmp_k[l], w_cmp_v[l], ret_gn_g[l])
        moe = (w_group[l], b_group[l], w_expert[l], b_expert[l], w_gate[l], w_up[l], w_down[l])
        w_o_bf16 = w_o[l].astype(BF16)
        tp = hp.shape[0] * hp.shape[1]
        h1p = _wo_ln(fp[0], fp[1], hp.reshape(tp, D_MODEL), w_o_bf16, ln1_g[l], ln1_b[l])
        h1s = _wo_ln(fs[0], fs[1], hs.reshape(-1, D_MODEL), w_o_bf16, ln1_g[l], ln1_b[l])
        h1 = jnp.concatenate([h1p, h1s], axis=0)
        h2 = _layer_norm(DEEPNORM_ALPHA * h1 + _hmoe(h1, *moe), ln2_g[l], ln2_b[l])
        hp = h2[:tp].reshape(hp.shape)
        hs = h2[tp:].reshape(hs.shape)
        for acc, t in zip(acc_p, st_p):
            acc.append(t)
        for acc, t in zip(acc_s, st_s):
            acc.append(t)
    p_cmp_k, p_cmp_v, p_sel_k, p_sel_v, p_win_k, p_win_v, p_ret = [jnp.stack(a) for a in acc_p]
    s_cmp_k, s_cmp_v, s_sel_k, s_sel_v, s_win_k, s_win_v, s_ret = [jnp.stack(a) for a in acc_s]
    return (hp, hs, p_cmp_k, p_cmp_v, p_sel_k, p_sel_v, p_win_k, p_win_v, p_ret.astype(state_ret.dtype),
            s_cmp_k, s_cmp_v, s_sel_k, s_sel_v, s_win_k, s_win_v, s_ret.astype(state_ret.dtype))
```

```python
import functools
import math

import jax
import jax.numpy as jnp
import numpy as np
from jax import lax
from jax.experimental import pallas as pl
from jax.experimental.pallas import tpu as pltpu

D_MODEL = 2048
DEPTH = 1
PAGE_SIZE = 128

F32 = jnp.float32
BF16 = jnp.bfloat16
HEAD_DIM = 128
RET_HEADS = D_MODEL // (2 * HEAD_DIM)
NSA_HEADS = D_MODEL // (2 * HEAD_DIM)
NSA_KV_HEADS = 2
RET_W = RET_HEADS * HEAD_DIM
NSA_W = NSA_HEADS * HEAD_DIM
KV_W = NSA_KV_HEADS * HEAD_DIM
MIX_W = RET_W + NSA_W
RET_CHUNK = 128
RET_ROPE_THETA = 10000.0
ROPE_THETA = 500000.0
ROT_DIM = HEAD_DIM // 4
CMP_BLOCK = 64
SEL_TOPK = 16
WINDOW = 512
WIN_Q_BLOCK = 128
SEL_Q_BLOCK = 64
N_GROUPS = 4
EXPERTS_PER_GROUP = 8
N_EXPERTS = N_GROUPS * EXPERTS_PER_GROUP
EXPERT_TOPK = 2
D_EXPERT = 512
MOE_BLOCK = 128
LN_EPS = 1e-5
GN_EPS = 1e-5
NEG = -1e30
DEEPNORM_ALPHA = (2 * DEPTH) ** 0.25
DEEPNORM_BETA = (8 * DEPTH) ** -0.25
SPLITS = (RET_W, RET_W, RET_W, RET_W, NSA_W, KV_W, KV_W, KV_W, KV_W, KV_W, KV_W, NSA_HEADS * 3)
IN_W = sum(SPLITS)
GATE_W = NSA_HEADS * 3
MAIN_W = IN_W - GATE_W
LANES = 128
VMEM_LIMIT = 48 * 1024 * 1024


def _matmul_kernel(x_ref, w_ref, o_ref):
    o_ref[...] = jnp.dot(x_ref[...].astype(BF16), w_ref[...].astype(BF16), preferred_element_type=F32)


def _matmul(x, w, n_out, tm, tn, name):
    t, k = x.shape
    return pl.pallas_call(
        _matmul_kernel,
        out_shape=jax.ShapeDtypeStruct((t, n_out), F32),
        grid=(t // tm, n_out // tn),
        in_specs=[pl.BlockSpec((tm, k), lambda i, j: (i, 0)),
                  pl.BlockSpec((k, tn), lambda i, j: (0, j))],
        out_specs=pl.BlockSpec((tm, tn), lambda i, j: (i, j)),
        compiler_params=pltpu.CompilerParams(dimension_semantics=("arbitrary", "arbitrary"),
                                             vmem_limit_bytes=VMEM_LIMIT),
        name=name,
    )(x, w)


def _project(x, w_in):
    n, l, d = x.shape
    xt = x.reshape(n * l, d)
    tm = min(512, n * l)
    main = _matmul(xt, w_in, MAIN_W, tm, 512, "in_proj")
    gpg = GATE_W // NSA_KV_HEADS
    w_tail = jnp.concatenate(
        [jnp.pad(w_in[:, MAIN_W + k * gpg:MAIN_W + (k + 1) * gpg], ((0, 0), (0, LANES - gpg)))
         for k in range(NSA_KV_HEADS)], axis=1)
    gl_tiles = _matmul(xt, w_tail, NSA_KV_HEADS * LANES, tm, NSA_KV_HEADS * LANES, "gate_proj")
    return main, gl_tiles


def _split_main(main, n, l, first=0):
    cuts = [0] + [int(c) for c in np.cumsum(SPLITS)[:-1]]
    return [main[:, cuts[i]:cuts[i + 1]].reshape(n, l, -1) for i in range(first, len(SPLITS) - 1)]


def _gate_logits(gl_tiles, n, l):
    gpg = GATE_W // NSA_KV_HEADS
    cols = [gl_tiles[:, k * LANES:k * LANES + gpg] for k in range(NSA_KV_HEADS)]
    return jnp.concatenate(cols, axis=1).reshape(n, l, GATE_W)


def _compress_kernel(k_ref, v_ref, wk_ref, wv_ref, ko_ref, vo_ref):
    r = ko_ref.shape[0]
    ko_ref[...] = jnp.sum(k_ref[...].reshape(r, CMP_BLOCK, KV_W) * wk_ref[...][None], axis=1)
    vo_ref[...] = jnp.sum(v_ref[...].reshape(r, CMP_BLOCK, KV_W) * wv_ref[...][None], axis=1)


def _compress_prompt(ck, cv, w_cmp_k, w_cmp_v):
    t = ck.shape[0]
    r = 32
    wk2 = jnp.tile(w_cmp_k, (1, NSA_KV_HEADS))
    wv2 = jnp.tile(w_cmp_v, (1, NSA_KV_HEADS))
    row = pl.BlockSpec((r * CMP_BLOCK, KV_W), lambda i: (i, 0))
    wsp = pl.BlockSpec((CMP_BLOCK, KV_W), lambda i: (0, 0))
    osp = pl.BlockSpec((r, KV_W), lambda i: (i, 0))
    return pl.pallas_call(
        _compress_kernel,
        out_shape=[jax.ShapeDtypeStruct((t // CMP_BLOCK, KV_W), F32)] * 2,
        grid=(t // (r * CMP_BLOCK),),
        in_specs=[row, row, wsp, wsp],
        out_specs=[osp, osp],
        compiler_params=pltpu.CompilerParams(dimension_semantics=("arbitrary",)),
        name="compress_prompt",
    )(ck, cv, wk2, wv2)


PAGES_PER_STEP = 8
BLOCKS_PER_PAGE = PAGE_SIZE // CMP_BLOCK


SUBLANES = 8
ROWS_PER_BLOCK = CMP_BLOCK * NSA_KV_HEADS
ROWS_PER_PAGE = PAGE_SIZE * NSA_KV_HEADS
BLOCKS_PER_TILE = SUBLANES // NSA_KV_HEADS


def _div(x, n):
    assert n & (n - 1) == 0
    return jnp.right_shift(x, n.bit_length() - 1)


def _mod(x, n):
    assert n & (n - 1) == 0
    return jnp.bitwise_and(x, n - 1)


def _interleaved(pool):
    return pool.reshape(pool.shape[:-3] + (pool.shape[-3] * NSA_KV_HEADS, HEAD_DIM))


def _compress_paged_kernel(pt_ref, *refs):
    del pt_ref
    k_pages = refs[:PAGES_PER_STEP]
    v_pages = refs[PAGES_PER_STEP:2 * PAGES_PER_STEP]
    wk_ref, wv_ref, ko_ref, vo_ref = refs[2 * PAGES_PER_STEP:]
    pair = _div(lax.broadcasted_iota(jnp.int32, (SUBLANES, HEAD_DIM), 0), NSA_KV_HEADS)

    def summaries(pages, w):
        sums = []
        for p in pages:
            for b in range(BLOCKS_PER_PAGE):
                y = p[0, b * ROWS_PER_BLOCK:(b + 1) * ROWS_PER_BLOCK, :] * w
                acc = jnp.sum(y.reshape(ROWS_PER_BLOCK // SUBLANES, SUBLANES, HEAD_DIM), axis=0)
                shift = SUBLANES // 2
                while shift >= NSA_KV_HEADS:
                    acc = acc + pltpu.roll(acc, shift, 0)
                    shift //= 2
                sums.append(acc)
        tiles = []
        for t in range(len(sums) // BLOCKS_PER_TILE):
            tile = sums[t * BLOCKS_PER_TILE]
            for j in range(1, BLOCKS_PER_TILE):
                tile = jnp.where(pair == j, sums[t * BLOCKS_PER_TILE + j], tile)
            tiles.append(tile)
        return jnp.concatenate(tiles, axis=0)

    ko_ref[0] = summaries(k_pages, wk_ref[...])
    vo_ref[0] = summaries(v_pages, wv_ref[...])


def _compress_paged(pool_k, pool_v, page_table, w_cmp_k, w_cmp_v):
    n, n_pages = page_table.shape
    pk = _interleaved(pool_k)
    pv = _interleaved(pool_v)
    wk2 = jnp.repeat(w_cmp_k, NSA_KV_HEADS, axis=0)
    wv2 = jnp.repeat(w_cmp_v, NSA_KV_HEADS, axis=0)

    def page_spec(j):
        return pl.BlockSpec((1, ROWS_PER_PAGE, HEAD_DIM), lambda s, i, pt: (pt[s, i * PAGES_PER_STEP + j], 0, 0))

    wsp = pl.BlockSpec((ROWS_PER_BLOCK, HEAD_DIM), lambda s, i, pt: (0, 0))
    rows = PAGES_PER_STEP * BLOCKS_PER_PAGE * NSA_KV_HEADS
    osp = pl.BlockSpec((1, rows, HEAD_DIM), lambda s, i, pt: (s, i, 0))
    specs = [page_spec(j) for j in range(PAGES_PER_STEP)]
    return pl.pallas_call(
        _compress_paged_kernel,
        out_shape=[jax.ShapeDtypeStruct((n, n_pages * BLOCKS_PER_PAGE * NSA_KV_HEADS, HEAD_DIM), F32)] * 2,
        grid_spec=pltpu.PrefetchScalarGridSpec(
            num_scalar_prefetch=1,
            grid=(n, n_pages // PAGES_PER_STEP),
            in_specs=specs + specs + [wsp, wsp],
            out_specs=[osp, osp]),
        compiler_params=pltpu.CompilerParams(dimension_semantics=("arbitrary", "arbitrary")),
        name="compress_paged",
    )(page_table, *([pk] * PAGES_PER_STEP), *([pv] * PAGES_PER_STEP), wk2, wv2)


SS_SEQ = 8
LOWEST = -3.0e38


def _sample_select_kernel(q_ref, kc_ref, vc_ref, ocmp_ref, sel_ref, *, pos):
    ss, nbk = q_ref.shape[0], kc_ref.shape[1]
    kv, g = NSA_KV_HEADS, NSA_GROUP
    head = lax.broadcasted_iota(jnp.int32, (NSA_HEADS, nbk), 0)
    col = lax.broadcasted_iota(jnp.int32, (NSA_HEADS, nbk), 1)
    m = (_mod(col, kv) == _div(head, g)) & ((_div(col, kv) + 1) * CMP_BLOCK - 1 <= pos)
    width = nbk + LANES
    rows = []
    for i in range(ss):
        q = (q_ref[i] * (HEAD_DIM ** -0.5)).astype(BF16)
        p = _masked_softmax(_dot_nt(q, kc_ref[i].astype(BF16)), m)
        ocmp_ref[i] = jnp.dot(p.astype(BF16), vc_ref[i].astype(BF16), preferred_element_type=F32)
        for k in range(kv):
            imp = jnp.sum(p[k * g:(k + 1) * g], axis=0, keepdims=True)
            rows.append(jnp.concatenate([imp, jnp.zeros((1, LANES), F32)], axis=1))
    nrow = ss * kv
    r_iota = lax.broadcasted_iota(jnp.int32, (nrow, width), 0)
    ccol = lax.broadcasted_iota(jnp.int32, (nrow, width), 1)
    cand = jnp.zeros((nrow, width), F32)
    for r, row in enumerate(rows):
        cand = jnp.where(r_iota == r, row, cand)
    cblk = _div(ccol, kv)
    cur = pos // CMP_BLOCK
    n_sel = -(-(pos + 1) // CMP_BLOCK)
    forced = (cblk == 0) | (cblk == cur) | (cblk == cur - 1)
    score = jnp.where(cblk > cur, NEG, jnp.where(forced, -NEG, cand))
    score = jnp.where((_mod(ccol, kv) == _mod(r_iota, kv)) & (cblk < n_sel), score, LOWEST)
    colf = ccol.astype(F32)
    lane = lax.broadcasted_iota(jnp.int32, (nrow, LANES), 1)
    sel = jnp.zeros((nrow, LANES), jnp.int32)
    for t in range(SEL_TOPK):
        mx = jnp.max(score, axis=1, keepdims=True)
        c = jnp.min(jnp.where(score == mx, colf, -LOWEST), axis=1, keepdims=True)
        picked = jnp.where(mx > 0.5 * NEG, _div(c.astype(jnp.int32), kv), -1)
        sel = jnp.where(lane == t, picked, sel)
        score = jnp.where(colf == c, LOWEST, score)
    sel_ref[...] = sel


def _sample_select(q, kc, vc, pos):
    n = q.shape[0]
    nbk = kc.shape[1]
    qsp = pl.BlockSpec((SS_SEQ, NSA_HEADS, HEAD_DIM), lambda i: (i, 0, 0))
    csp = pl.BlockSpec((SS_SEQ, nbk, HEAD_DIM), lambda i: (i, 0, 0))
    return pl.pallas_call(
        functools.partial(_sample_select_kernel, pos=pos),
        out_shape=[jax.ShapeDtypeStruct((n, NSA_HEADS, HEAD_DIM), F32),
                   jax.ShapeDtypeStruct((n * NSA_KV_HEADS, LANES), jnp.int32)],
        grid=(n // SS_SEQ,),
        in_specs=[qsp, csp, csp],
        out_specs=[qsp, pl.BlockSpec((SS_SEQ * NSA_KV_HEADS, LANES), lambda i: (i, 0))],
        compiler_params=pltpu.CompilerParams(dimension_semantics=("arbitrary",), vmem_limit_bytes=VMEM_LIMIT),
        name="sample_select",
    )(q, kc, vc)


N_SLOTS = NSA_KV_HEADS * SEL_TOPK


def _sample_attend_kernel(sel_ref, pt_ref, q_ref, knew_ref, vnew_ref, wknew_ref, wvnew_ref, wkb_ref, wvb_ref,
                          ocmp_ref, gl_ref, *rest, pos, past_blocks):
    del pt_ref
    kblk, vblk, o_ref = rest[:N_SLOTS], rest[N_SLOTS:2 * N_SLOTS], rest[2 * N_SLOTS]
    kv, g, rb = NSA_KV_HEADS, NSA_GROUP, ROWS_PER_BLOCK
    n = pl.program_id(0)
    q = q_ref[0] * (HEAD_DIM ** -0.5)
    row = lax.broadcasted_iota(jnp.int32, (rb, HEAD_DIM), 0)
    col = lax.broadcasted_iota(jnp.int32, (1, rb), 1)

    def new_block(ref):
        out = jnp.zeros((rb, HEAD_DIM), F32)
        for k in range(kv):
            out = jnp.where(row == k, ref[0, k:k + 1, :], out)
        return out

    def per_head_rows(ref):
        return jnp.concatenate([jnp.broadcast_to(ref[0, k:k + 1, :], (g, HEAD_DIM)) for k in range(kv)], axis=0)

    knew, vnew = new_block(knew_ref), new_block(vnew_ref)
    o_sel = []
    for k in range(kv):
        ks, vs, ms = [], [], []
        for j in range(SEL_TOPK):
            b = sel_ref[(n * kv + k) * SEL_TOPK + j]
            is_new = jnp.broadcast_to(b, (rb, HEAD_DIM)) >= past_blocks
            ks.append(jnp.where(is_new, knew, kblk[k * SEL_TOPK + j][0]))
            vs.append(jnp.where(is_new, vnew, vblk[k * SEL_TOPK + j][0]))
            first = jnp.where(b >= 0, b, 1 << 24) * CMP_BLOCK
            ms.append((first + _div(col, kv) <= pos) & (_mod(col, kv) == k))
        s = _dot_nt(q[k * g:(k + 1) * g].astype(BF16), jnp.concatenate(ks, axis=0).astype(BF16))
        p = _masked_softmax(s, jnp.concatenate(ms, axis=1))
        o_sel.append(jnp.dot(p.astype(BF16), jnp.concatenate(vs, axis=0).astype(BF16), preferred_element_type=F32))
    o_sel = jnp.concatenate(o_sel, axis=0)

    nwr = wkb_ref.shape[1]
    s_w = _dot_nt(q.astype(BF16), wkb_ref[0].astype(BF16))
    cw = lax.broadcasted_iota(jnp.int32, (NSA_HEADS, nwr), 1)
    hw = lax.broadcasted_iota(jnp.int32, (NSA_HEADS, nwr), 0)
    kpos = pos - nwr // kv + _div(cw, kv)
    mw = (kpos <= pos) & (pos - kpos < WINDOW) & (_mod(cw, kv) == _div(hw, g))
    s_n = jnp.sum(q * per_head_rows(wknew_ref), axis=1, keepdims=True)
    smw = jnp.where(mw, s_w, NEG)
    mx = jnp.maximum(jnp.max(smw, axis=1, keepdims=True), s_n)
    e_w = jnp.where(mw, jnp.exp(smw - mx), 0.0)
    e_n = jnp.exp(s_n - mx)
    den = jnp.sum(e_w, axis=1, keepdims=True) + e_n
    o_win = (jnp.dot(e_w.astype(BF16), wvb_ref[0].astype(BF16), preferred_element_type=F32)
             + e_n * per_head_rows(wvnew_ref)) / den

    gate = 1.0 / (1.0 + jnp.exp(-gl_ref[0]))
    o_ref[0] = gate[:, 0:1] * ocmp_ref[0] + gate[:, 1:2] * o_sel + gate[:, 2:3] * o_win


def _sample_attend(sel, page_table, q, sk, sv, wk, wv, win_k, win_v, pool_k, pool_v, o_cmp, gl, pos):
    n, n_pages = page_table.shape
    past_blocks = n_pages * BLOCKS_PER_PAGE

    def slot_spec(k, j):
        def imap(s, sel_r, pt_r):
            b = jnp.maximum(sel_r[(s * NSA_KV_HEADS + k) * SEL_TOPK + j], 0)
            page = jnp.minimum(_div(b, BLOCKS_PER_PAGE), n_pages - 1)
            return (pt_r[s * n_pages + page], _mod(b, BLOCKS_PER_PAGE), 0)
        return pl.BlockSpec((1, ROWS_PER_BLOCK, HEAD_DIM), imap)

    slots = [slot_spec(k, j) for k in range(NSA_KV_HEADS) for j in range(SEL_TOPK)]
    per_seq = lambda a: pl.BlockSpec((1,) + a.shape[1:], lambda s, sel_r, pt_r: (s, 0, 0))
    dense = [q, sk, sv, wk, wv, win_k, win_v, o_cmp, gl]
    return pl.pallas_call(
        functools.partial(_sample_attend_kernel, pos=pos, past_blocks=past_blocks),
        out_shape=jax.ShapeDtypeStruct((n, NSA_HEADS, HEAD_DIM), F32),
        grid_spec=pltpu.PrefetchScalarGridSpec(
            num_scalar_prefetch=2,
            grid=(n,),
            in_specs=[per_seq(a) for a in dense] + slots + slots,
            out_specs=pl.BlockSpec((1, NSA_HEADS, HEAD_DIM), lambda s, sel_r, pt_r: (s, 0, 0))),
        compiler_params=pltpu.CompilerParams(dimension_semantics=("arbitrary",), vmem_limit_bytes=VMEM_LIMIT),
        name="sample_attend",
    )(sel, page_table.reshape(-1), *dense, *([pool_k] * N_SLOTS), *([pool_v] * N_SLOTS))


def _retention_tables(seq):
    c = RET_CHUNK
    log_g = jnp.log1p(-jnp.exp2(-5.0 - jnp.arange(RET_HEADS, dtype=F32)))
    i = jnp.arange(c, dtype=F32)
    rel = i[:, None] - i[None, :]
    dmask = jnp.where(rel[None] >= 0, jnp.exp(jnp.maximum(rel[None], 0.0) * log_g[:, None, None]), 0.0)
    q_dec = jnp.exp((i + 1.0)[None] * log_g[:, None])[..., None]
    k_dec = jnp.exp((c - 1.0 - i)[None] * log_g[:, None])[..., None]
    c_dec = jnp.exp(c * log_g)[:, None, None]
    bc = lambda t: jnp.broadcast_to(t, (RET_HEADS, c, HEAD_DIM))
    inv = RET_ROPE_THETA ** (-jnp.arange(0, HEAD_DIM, 2, dtype=F32) / HEAD_DIM)
    ang = jnp.arange(seq)[:, None].astype(F32) * inv
    cos, sin = jnp.cos(ang), jnp.sin(ang)
    return (dmask, bc(q_dec), bc(k_dec), jnp.broadcast_to(c_dec, (RET_HEADS, 1, HEAD_DIM)),
            jnp.concatenate([cos, cos], -1), jnp.concatenate([-sin, sin], -1))


def _retention_kernel(q_ref, k_ref, v_ref, g_ref, cos_ref, sin_ref, dmask_ref, qdec_ref, kdec_ref, cdec_ref, gn_ref,
                      o_ref, st_ref, s_scr):
    c = pl.program_id(1)

    @pl.when(c == 0)
    def _():
        s_scr[...] = jnp.zeros(s_scr.shape, F32)

    cosf, sins = cos_ref[...], sin_ref[...]
    half = HEAD_DIM // 2
    for h in range(RET_HEADS):
        sl = slice(h * HEAD_DIM, (h + 1) * HEAD_DIM)
        qh, kh = q_ref[:, sl], k_ref[:, sl]
        qr = qh * cosf + pltpu.roll(qh, half, 1) * sins
        kr = (kh * cosf + pltpu.roll(kh, half, 1) * sins) * (HEAD_DIM ** -0.5)
        vb = v_ref[:, sl].astype(BF16)
        att = _dot_nt(qr.astype(BF16), kr.astype(BF16)) * dmask_ref[h]
        s_prev = s_scr[h]
        o = (jnp.dot(att.astype(BF16), vb, preferred_element_type=F32)
             + jnp.dot((qr * qdec_ref[h]).astype(BF16), s_prev.astype(BF16), preferred_element_type=F32))
        s_scr[h] = cdec_ref[h] * s_prev + lax.dot_general(
            (kr * kdec_ref[h]).astype(BF16), vb, (((0,), (0,)), ((), ())), preferred_element_type=F32)
        mu = jnp.mean(o, axis=-1, keepdims=True)
        var = jnp.mean(jnp.square(o - mu), axis=-1, keepdims=True)
        on = (o - mu) * lax.rsqrt(var + GN_EPS) * gn_ref[:, sl]
        gg = g_ref[:, sl]
        o_ref[:, sl] = gg * (1.0 / (1.0 + jnp.exp(-gg))) * on

    @pl.when(c == pl.num_programs(1) - 1)
    def _():
        st_ref[0] = s_scr[...]


def _retention_prompt(main, gn_g, batch, seq):
    nc = seq // RET_CHUNK
    dmask, q_dec, k_dec, c_dec, cosf, sins = _retention_tables(seq)
    col = lambda j: pl.BlockSpec((RET_CHUNK, RET_W), lambda b, c: (b * nc + c, j))
    pos_tab = pl.BlockSpec((RET_CHUNK, HEAD_DIM), lambda b, c: (c, 0))
    full = lambda a: pl.BlockSpec(a.shape, lambda b, c: (0,) * a.ndim)
    gn = gn_g.reshape(1, RET_W)
    return pl.pallas_call(
        _retention_kernel,
        out_shape=[jax.ShapeDtypeStruct((batch * seq, RET_W), F32),
                   jax.ShapeDtypeStruct((batch, RET_HEADS, HEAD_DIM, HEAD_DIM), F32)],
        grid=(batch, nc),
        in_specs=[col(0), col(1), col(2), col(3), pos_tab, pos_tab,
                  full(dmask), full(q_dec), full(k_dec), full(c_dec), full(gn)],
        out_specs=[pl.BlockSpec((RET_CHUNK, RET_W), lambda b, c: (b * nc + c, 0)),
                   pl.BlockSpec((1, RET_HEADS, HEAD_DIM, HEAD_DIM), lambda b, c: (b, 0, 0, 0))],
        scratch_shapes=[pltpu.VMEM((RET_HEADS, HEAD_DIM, HEAD_DIM), F32)],
        compiler_params=pltpu.CompilerParams(dimension_semantics=("arbitrary", "arbitrary"),
                                             vmem_limit_bytes=VMEM_LIMIT),
        name="retention_prompt",
    )(main, main, main, main, cosf, sins, dmask, q_dec, k_dec, c_dec, gn)


NSA_TQ = 128
NSA_KEY_CHUNK = 512
NSA_GROUP = NSA_HEADS // NSA_KV_HEADS


def _dot_nt(a, b):
    return lax.dot_general(a, b, (((1,), (1,)), ((), ())), preferred_element_type=F32)


def _masked_softmax(s, m):
    sm = jnp.where(m, s, NEG)
    e = jnp.exp(sm - jnp.max(sm, axis=-1, keepdims=True))
    return jnp.where(m, e / jnp.sum(e, axis=-1, keepdims=True), 0.0)


def _select_mask(imp, pos):
    nb = imp.shape[1]
    blk = lax.broadcasted_iota(jnp.int32, (1, nb), 1)
    cur = jnp.right_shift(pos, int(math.log2(CMP_BLOCK)))
    forced = (blk == 0) | (blk == cur) | (blk == cur - 1)
    score = jnp.where(blk > cur, NEG, jnp.where(forced, -NEG, imp))
    rank = jnp.zeros(score.shape, jnp.int32)
    for i in range(nb):
        si = score[:, i:i + 1]
        ahead = (si > score) | ((si == score) & (blk > i))
        rank = rank + ahead.astype(jnp.int32)
    return (rank < SEL_TOPK) & (score > 0.5 * NEG)


def _nsa_prompt_kernel(q_ref, kc_ref, vc_ref, sk_ref, sv_ref, wk_ref, wv_ref, gl_ref, o_ref):
    tq, g, kc_n = NSA_TQ, NSA_GROUP, NSA_KEY_CHUNK
    qi = pl.program_id(2)
    q4 = q_ref[...] * (HEAD_DIM ** -0.5)
    qs = jnp.concatenate([q4[:, i * HEAD_DIM:(i + 1) * HEAD_DIM] for i in range(g)], axis=0).astype(BF16)
    pos = qi * tq + lax.broadcasted_iota(jnp.int32, (tq, 1), 0)
    pos4 = jnp.concatenate([pos] * g, axis=0)

    nb = kc_ref.shape[1]
    s_c = _dot_nt(qs, kc_ref[0].astype(BF16))
    blk = lax.broadcasted_iota(jnp.int32, (1, nb), 1)
    p_c = _masked_softmax(s_c, (blk + 1) * CMP_BLOCK - 1 <= pos4)
    o_cmp = jnp.dot(p_c.astype(BF16), vc_ref[0].astype(BF16), preferred_element_type=F32)
    imp = p_c[0:tq]
    for i in range(1, g):
        imp = imp + p_c[i * tq:(i + 1) * tq]

    sel = _select_mask(imp, pos).astype(BF16)
    sel4 = jnp.concatenate([sel] * g, axis=0)
    blk_col = lax.broadcasted_iota(jnp.int32, (nb, kc_n), 0)
    key_col = lax.broadcasted_iota(jnp.int32, (nb, kc_n), 1)
    key_row = lax.broadcasted_iota(jnp.int32, (1, kc_n), 1)

    def chunk(c, carry):
        m_i, l_i, acc = carry
        k0 = pl.multiple_of(c * kc_n, kc_n)
        s = _dot_nt(qs, sk_ref[pl.ds(k0, kc_n), :].astype(BF16))
        expand = (jnp.right_shift(key_col + k0, int(math.log2(CMP_BLOCK))) == blk_col).astype(BF16)
        chosen = jnp.dot(sel4, expand, preferred_element_type=F32)
        msk = (chosen > 0.5) & (key_row + k0 <= pos4)
        sm = jnp.where(msk, s, NEG)
        m_new = jnp.maximum(m_i, jnp.max(sm, axis=-1, keepdims=True))
        alpha = jnp.exp(m_i - m_new)
        p = jnp.where(msk, jnp.exp(sm - m_new), 0.0)
        l_new = alpha * l_i + jnp.sum(p, axis=-1, keepdims=True)
        pv = jnp.dot(p.astype(BF16), sv_ref[pl.ds(k0, kc_n), :].astype(BF16), preferred_element_type=F32)
        return m_new, l_new, alpha * acc + pv

    n_chunks = (qi * tq + tq + kc_n - 1) // kc_n
    init = (jnp.full((g * tq, 1), NEG, F32), jnp.zeros((g * tq, 1), F32), jnp.zeros((g * tq, HEAD_DIM), F32))
    _, l_f, acc_f = lax.fori_loop(0, n_chunks, chunk, init)
    o_sel = acc_f / l_f

    nwk = WINDOW + tq
    kstart = pl.multiple_of(jnp.maximum(qi * tq - WINDOW, 0), tq)
    s_w = _dot_nt(qs, wk_ref[pl.ds(kstart, nwk), :].astype(BF16))
    kpos = kstart + lax.broadcasted_iota(jnp.int32, (1, nwk), 1)
    p_w = _masked_softmax(s_w, (kpos <= pos4) & (pos4 - kpos < WINDOW))
    o_win = jnp.dot(p_w.astype(BF16), wv_ref[pl.ds(kstart, nwk), :].astype(BF16), preferred_element_type=F32)

    gate = 1.0 / (1.0 + jnp.exp(-gl_ref[...]))
    for i in range(g):
        rows = slice(i * tq, (i + 1) * tq)
        o_ref[:, i * HEAD_DIM:(i + 1) * HEAD_DIM] = (gate[:, 3 * i:3 * i + 1] * o_cmp[rows]
                                                     + gate[:, 3 * i + 1:3 * i + 2] * o_sel[rows]
                                                     + gate[:, 3 * i + 2:3 * i + 3] * o_win[rows])


def _nsa_prompt(q, kc, vc, sk, sv, wk, wv, gl, batch, seq):
    nq = seq // NSA_TQ
    gw = NSA_GROUP * HEAD_DIM
    row = lambda b, k, i: (b * nq + i, k)
    seq_spec = pl.BlockSpec((seq, HEAD_DIM), lambda b, k, i: (b, k))
    cmp_spec = pl.BlockSpec((1, seq // CMP_BLOCK, HEAD_DIM), lambda b, k, i: (b, 0, k))
    return pl.pallas_call(
        _nsa_prompt_kernel,
        out_shape=jax.ShapeDtypeStruct((batch * seq, NSA_W), F32),
        grid=(batch, NSA_KV_HEADS, nq),
        in_specs=[pl.BlockSpec((NSA_TQ, gw), row), cmp_spec, cmp_spec,
                  seq_spec, seq_spec, seq_spec, seq_spec,
                  pl.BlockSpec((NSA_TQ, LANES), row)],
        out_specs=pl.BlockSpec((NSA_TQ, gw), row),
        compiler_params=pltpu.CompilerParams(dimension_semantics=("arbitrary",) * 3,
                                             vmem_limit_bytes=VMEM_LIMIT),
        name="nsa_prompt",
    )(q, kc, vc, sk, sv, wk, wv, gl)


WO_TM = 256


def _wo_ln_kernel(fr_ref, fn_ref, x_ref, w_ref, g_ref, b_ref, o_ref):
    y = (DEEPNORM_ALPHA * x_ref[...]
         + jnp.dot(fr_ref[...].astype(BF16), w_ref[0:RET_W, :], preferred_element_type=F32)
         + jnp.dot(fn_ref[...].astype(BF16), w_ref[RET_W:MIX_W, :], preferred_element_type=F32))
    mu = jnp.mean(y, axis=-1, keepdims=True)
    var = jnp.mean(jnp.square(y - mu), axis=-1, keepdims=True)
    o_ref[...] = (y - mu) * lax.rsqrt(var + LN_EPS) * g_ref[...] + b_ref[...]


def _wo_ln(f_ret, f_nsa, x, w_o_bf16, ln_g, ln_b):
    t, d = x.shape
    tm = min(WO_TM, t)
    assert t % tm == 0
    row = lambda w: pl.BlockSpec((tm, w), lambda i: (i, 0))
    vec = pl.BlockSpec((1, d), lambda i: (0, 0))
    return pl.pallas_call(
        _wo_ln_kernel,
        out_shape=jax.ShapeDtypeStruct((t, d), F32),
        grid=(t // tm,),
        in_specs=[row(RET_W), row(NSA_W), row(d), pl.BlockSpec(w_o_bf16.shape, lambda i: (0, 0)), vec, vec],
        out_specs=row(d),
        compiler_params=pltpu.CompilerParams(dimension_semantics=("arbitrary",), vmem_limit_bytes=VMEM_LIMIT),
        name="wo_ln1",
    )(f_ret, f_nsa, x, w_o_bf16, ln_g.reshape(1, d), ln_b.reshape(1, d))


MOE_BM = 256


def _expert_kernel(blk_e_ref, n_used_ref, x_ref, wg_ref, wu_ref, wd_ref, o_ref):
    del blk_e_ref
    i = pl.program_id(0)

    @pl.when(i < n_used_ref[0])
    def _():
        xb = x_ref[...].astype(BF16)
        hg = jnp.dot(xb, wg_ref[0].astype(BF16), preferred_element_type=F32)
        hu = jnp.dot(xb, wu_ref[0].astype(BF16), preferred_element_type=F32)
        hb = hg * (1.0 / (1.0 + jnp.exp(-hg))) * hu
        o_ref[...] = jnp.dot(hb.astype(BF16), wd_ref[0].astype(BF16), preferred_element_type=F32)

    @pl.when(i >= n_used_ref[0])
    def _():
        o_ref[...] = jnp.zeros(o_ref.shape, F32)


def _expert_ffn(buf, blk_e, n_used, w_gate, w_up, w_down):
    rows, d = buf.shape
    n_blk = rows // MOE_BM
    de = w_gate.shape[2]
    xsp = pl.BlockSpec((MOE_BM, d), lambda i, be, nu: (i, 0))
    return pl.pallas_call(
        _expert_kernel,
        out_shape=jax.ShapeDtypeStruct((rows, d), F32),
        grid_spec=pltpu.PrefetchScalarGridSpec(
            num_scalar_prefetch=2,
            grid=(n_blk,),
            in_specs=[xsp,
                      pl.BlockSpec((1, d, de), lambda i, be, nu: (be[i], 0, 0)),
                      pl.BlockSpec((1, d, de), lambda i, be, nu: (be[i], 0, 0)),
                      pl.BlockSpec((1, de, d), lambda i, be, nu: (be[i], 0, 0))],
            out_specs=xsp),
        compiler_params=pltpu.CompilerParams(dimension_semantics=("arbitrary",), vmem_limit_bytes=VMEM_LIMIT),
        name="expert_ffn",
    )(blk_e, n_used, buf, w_gate, w_up, w_down)


def _layer_norm(x, g, b):
    xf = x.astype(F32)
    mu = xf.mean(-1, keepdims=True)
    var = jnp.square(xf - mu).mean(-1, keepdims=True)
    return ((xf - mu) * lax.rsqrt(var + LN_EPS) * g + b).astype(x.dtype)


def _rope(x, pos, rot_dim, theta):
    half = rot_dim // 2
    inv = theta ** (-jnp.arange(0, rot_dim, 2, dtype=F32) / rot_dim)
    ang = pos[..., None].astype(F32) * inv
    cos = jnp.cos(ang)[:, :, None, :]
    sin = jnp.sin(ang)[:, :, None, :]
    xr = x[..., :rot_dim].astype(F32)
    x1, x2 = xr[..., :half], xr[..., half:]
    rot = jnp.concatenate([x1 * cos - x2 * sin, x2 * cos + x1 * sin], -1).astype(x.dtype)
    return jnp.concatenate([rot, x[..., rot_dim:]], -1)


def _heads(t, n):
    return t.reshape(t.shape[0], t.shape[1], n, HEAD_DIM)


def _chunk_retention(q, k, v, s0):
    n, l, h, d = q.shape
    c = RET_CHUNK if l % RET_CHUNK == 0 else l
    nc = l // c
    log_g = jnp.log1p(-jnp.exp2(-5.0 - jnp.arange(h, dtype=F32)))
    i = jnp.arange(c, dtype=F32)
    rel = i[:, None] - i[None, :]
    dmask = jnp.where(rel[None] >= 0, jnp.exp(jnp.maximum(rel[None], 0.0) * log_g[:, None, None]), 0.0)
    q_dec = jnp.exp((i + 1.0)[None] * log_g[:, None])[..., None]
    k_dec = jnp.exp((c - 1.0 - i)[None] * log_g[:, None])[..., None]
    c_dec = jnp.exp(c * log_g)[:, None, None]

    def to_chunks(t):
        return t.astype(F32).reshape(n, nc, c, h, d).transpose(1, 0, 3, 2, 4)

    def step(s, qkv):
        qc, kc, vc = qkv
        att = jnp.einsum('bhid,bhjd->bhij', qc, kc) * dmask
        o = jnp.einsum('bhij,bhjd->bhid', att, vc) + jnp.einsum('bhid,bhde->bhie', qc * q_dec, s)
        s = c_dec * s + jnp.einsum('bhjd,bhje->bhde', kc * k_dec, vc)
        return s, o

    s, o = lax.scan(step, s0.astype(F32), (to_chunks(q), to_chunks(k), to_chunks(v)))
    return o.transpose(1, 0, 3, 2, 4).reshape(n, l, h, d), s


def _retention_group(rq, rk, rv, rg, pos, s0, gn_g):
    q = _rope(_heads(rq, RET_HEADS), pos, HEAD_DIM, RET_ROPE_THETA)
    k = _rope(_heads(rk, RET_HEADS), pos, HEAD_DIM, RET_ROPE_THETA) * (HEAD_DIM ** -0.5)
    v = _heads(rv, RET_HEADS)
    o, s = _chunk_retention(q, k, v, s0)
    mu = o.mean(-1, keepdims=True)
    var = jnp.square(o - mu).mean(-1, keepdims=True)
    on = (o - mu) * lax.rsqrt(var + GN_EPS) * gn_g.reshape(RET_HEADS, HEAD_DIM).astype(F32)
    out = jax.nn.silu(rg.astype(F32)) * on.reshape(rg.shape)
    return out.astype(rq.dtype), s


def _gqa_attend(q, k, v, mask):
    n, lq, h, d = q.shape
    kv = k.shape[2]
    qg = q.reshape(n, lq, kv, h // kv, d)
    s = jnp.einsum('nqkgd,nskd->nkgqs', qg, k).astype(F32) * (d ** -0.5)
    m = mask[:, None, None]
    p = jax.nn.softmax(jnp.where(m, s, NEG), axis=-1) * m
    o = jnp.einsum('nkgqs,nskd->nqkgd', p.astype(v.dtype), v)
    return o.reshape(n, lq, h, d), p


def _nsa_heads(nq, ck, sk, wk, cv, sv, wv, pos):
    rp = lambda t, nh: _rope(_heads(t, nh), pos, ROT_DIM, ROPE_THETA)
    return (rp(nq, NSA_HEADS), rp(ck, NSA_KV_HEADS), rp(sk, NSA_KV_HEADS), rp(wk, NSA_KV_HEADS),
            _heads(cv, NSA_KV_HEADS), _heads(sv, NSA_KV_HEADS), _heads(wv, NSA_KV_HEADS))


def _compress(rows, w):
    n, t, kv, d = rows.shape
    return jnp.einsum('nbjkd,jd->nbkd', rows.reshape(n, t // CMP_BLOCK, CMP_BLOCK, kv, d), w)


def _cmp_branch(q, pos, kc, vc):
    nb = kc.shape[1]
    blk_end = (jnp.arange(nb) + 1) * CMP_BLOCK - 1
    mask = blk_end[None, None, :] <= pos[:, :, None]
    o, p = _gqa_attend(q, kc, vc, mask)
    imp = p.sum(axis=2).transpose(0, 2, 1, 3)
    return o, imp


def _select_blocks(imp, pos, n_sel):
    nb = imp.shape[-1]
    imp = jnp.pad(imp, ((0, 0), (0, 0), (0, 0), (0, n_sel - nb)))
    blk = jnp.arange(n_sel)
    cur = (pos // CMP_BLOCK)[:, :, None, None]
    forced = (blk == 0) | (blk == cur) | (blk == cur - 1)
    score = jnp.where(blk > cur, NEG, jnp.where(forced, -NEG, imp))
    top, idx = lax.top_k(score, min(SEL_TOPK, n_sel))
    return idx, top > 0.5 * NEG


def _sel_attend(q, pos, ks, vs, idx, valid):
    n, lq, kv, kk, cb, d = ks.shape
    h = q.shape[2]
    kpos = idx[..., None] * CMP_BLOCK + jnp.arange(CMP_BLOCK)
    m = ((kpos <= pos[:, :, None, None, None]) & valid[..., None]).reshape(n, lq, kv, 1, kk * cb)
    qg = q.reshape(n, lq, kv, h // kv, d)
    kf = ks.reshape(n, lq, kv, kk * cb, d)
    vf = vs.reshape(n, lq, kv, kk * cb, d)
    s = jnp.einsum('nqkgd,nqkjd->nqkgj', qg, kf).astype(F32) * (d ** -0.5)
    p = jax.nn.softmax(jnp.where(m, s, NEG), axis=-1) * m
    o = jnp.einsum('nqkgj,nqkjd->nqkgd', p.astype(vf.dtype), vf)
    return o.reshape(n, lq, h, d)


def _sel_prompt(q, pos, k, v, idx, valid):
    b, s, h, d = q.shape
    kv = k.shape[2]
    nb = s // CMP_BLOCK
    nq = s // SEL_Q_BLOCK
    kb = k.reshape(b, nb, CMP_BLOCK, kv, d).transpose(0, 3, 1, 2, 4)
    vb = v.reshape(b, nb, CMP_BLOCK, kv, d).transpose(0, 3, 1, 2, 4)
    bi = jnp.arange(b)[:, None, None, None]
    hi = jnp.arange(kv)[None, None, :, None]

    def blockwise(t):
        return t.reshape(t.shape[0], nq, SEL_Q_BLOCK, *t.shape[2:]).swapaxes(0, 1)

    def one(args):
        qc, pc, ic, vc = args
        return _sel_attend(qc, pc, kb[bi, hi, ic], vb[bi, hi, ic], ic, vc)

    o = lax.map(one, (blockwise(q), blockwise(pos), blockwise(idx), blockwise(valid)))
    return o.swapaxes(0, 1).reshape(b, s, h, d)


def _win_prompt(q, k, v):
    b, s, h, d = q.shape
    kv = k.shape[2]
    nb = s // WIN_Q_BLOCK
    nprev = WINDOW // WIN_Q_BLOCK
    nw = nprev + 1
    padw = ((0, 0), (WINDOW, 0), (0, 0), (0, 0))
    kp = jnp.pad(k, padw).reshape(b, nb + nprev, WIN_Q_BLOCK, kv, d)
    vp = jnp.pad(v, padw).reshape(b, nb + nprev, WIN_Q_BLOCK, kv, d)
    kw = jnp.concatenate([kp[:, i:i + nb] for i in range(nw)], axis=2)
    vw = jnp.concatenate([vp[:, i:i + nb] for i in range(nw)], axis=2)
    qpos = jnp.arange(s).reshape(nb, WIN_Q_BLOCK)
    kpos = (jnp.arange(nb)[:, None] - nprev) * WIN_Q_BLOCK + jnp.arange(nw * WIN_Q_BLOCK)[None]
    qq, kk = qpos[:, :, None], kpos[:, None, :]
    mask = (kk <= qq) & (qq - kk < WINDOW) & (kk >= 0)
    mask = jnp.broadcast_to(mask[None], (b,) + mask.shape).reshape(b * nb, WIN_Q_BLOCK, nw * WIN_Q_BLOCK)
    o, _ = _gqa_attend(q.reshape(b * nb, WIN_Q_BLOCK, h, d), kw.reshape(b * nb, nw * WIN_Q_BLOCK, kv, d),
                       vw.reshape(b * nb, nw * WIN_Q_BLOCK, kv, d), mask)
    return o.reshape(b, s, h, d)


def _gather_selected(pool, new_rows, page_table, idx):
    n, l, kv, d = new_rows.shape
    n_pages = page_table.shape[1]
    past_blocks = n_pages * PAGE_SIZE // CMP_BLOCK
    nbn = -(-l // CMP_BLOCK)
    newb = jnp.pad(new_rows, ((0, 0), (0, nbn * CMP_BLOCK - l), (0, 0), (0, 0)))
    newb = newb.reshape(n, nbn, CMP_BLOCK, kv, d).transpose(0, 3, 1, 2, 4)
    bi = jnp.arange(n)[:, None, None, None]
    hi = jnp.arange(kv)[None, None, :, None]
    start = idx * CMP_BLOCK
    phys = page_table[bi, jnp.minimum(start // PAGE_SIZE, n_pages - 1)]
    off = (start % PAGE_SIZE)[..., None] + jnp.arange(CMP_BLOCK)
    past = pool[phys[..., None], off, hi[..., None]]
    new = newb[bi, hi, jnp.clip(idx - past_blocks, 0, nbn - 1)]
    return jnp.where((idx < past_blocks)[..., None, None], past, new)


def _nsa_combine(gl, o_cmp, o_sel, o_win):
    n, l = gl.shape[0], gl.shape[1]
    g = jax.nn.sigmoid(gl.astype(F32)).reshape(n, l, NSA_HEADS, 3, 1)
    o = g[..., 0, :] * o_cmp + g[..., 1, :] * o_sel + g[..., 2, :] * o_win
    return o.reshape(n, l, NSA_W).astype(o_cmp.dtype)


def _prompt_mixer(x, win_buf, w_in, w_cmp_k, w_cmp_v, gn_g):
    n, s, _ = x.shape
    pos = jnp.arange(s)[None]
    main, gl_tiles = _project(x, w_in)
    ret_out, s_fin = _retention_prompt(main, gn_g, n, s)
    nq, ck, cv, sk, sv, wk, wv = _split_main(main, n, s, first=4)
    q, ck, sk, wk, cv, sv, wv = _nsa_heads(nq, ck, sk, wk, cv, sv, wv, pos)
    flat = lambda a: a.reshape(n * s, -1)
    kc, vc = _compress_prompt(flat(ck), flat(cv), w_cmp_k, w_cmp_v)
    nsa = _nsa_prompt(flat(q), kc.reshape(n, s // CMP_BLOCK, KV_W), vc.reshape(n, s // CMP_BLOCK, KV_W),
                      flat(sk), flat(sv), flat(wk), flat(wv), gl_tiles, n, s)
    feats = (ret_out, nsa)
    if s >= win_buf:
        bk, bv = wk[:, s - win_buf:], wv[:, s - win_buf:]
    else:
        padb = ((0, 0), (win_buf - s, 0), (0, 0), (0, 0))
        bk, bv = jnp.pad(wk, padb), jnp.pad(wv, padb)
    return feats, (ck, cv, sk, sv, bk, bv, s_fin)


def _sample_mixer(x, c_cmp_k, c_cmp_v, c_sel_k, c_sel_v, c_win_k, c_win_v, s_ret, page_table,
                  w_in, w_cmp_k, w_cmp_v, gn_g):
    n, l, _ = x.shape
    past = page_table.shape[1] * PAGE_SIZE
    pos = past + jnp.arange(l)[None]
    main, gl_tiles = _project(x, w_in)
    rq, rk, rv, rg, nq, ck, cv, sk, sv, wk, wv = _split_main(main, n, l)
    gl = _gate_logits(gl_tiles, n, l)
    ret_out, s_new = _retention_group(rq, rk, rv, rg, pos, s_ret, gn_g)
    q, ck, sk, wk, cv, sv, wv = _nsa_heads(nq, ck, sk, wk, cv, sv, wv, pos)
    assert l == 1 and past % CMP_BLOCK == 0 and c_win_k.shape[1] <= WINDOW
    kc, vc = _compress_paged(c_cmp_k, c_cmp_v, page_table, w_cmp_k, w_cmp_v)
    o_cmp, sel = _sample_select(q[:, 0], kc, vc, past)
    sel = sel[:, :SEL_TOPK].reshape(-1)
    gl_pad = jnp.pad(gl.reshape(n, NSA_HEADS, 3), ((0, 0), (0, 0), (0, LANES - 3)))
    nsa = _sample_attend(sel, page_table, q[:, 0], sk[:, 0], sv[:, 0], wk[:, 0], wv[:, 0],
                         _interleaved(c_win_k), _interleaved(c_win_v), _interleaved(c_sel_k), _interleaved(c_sel_v),
                         o_cmp, gl_pad, past)
    feats = (ret_out.reshape(n * l, RET_W), nsa.reshape(n * l, NSA_W))
    kw = jnp.concatenate([c_win_k, wk], 1)
    vw = jnp.concatenate([c_win_v, wv], 1)
    return feats, (ck, cv, sk, sv, kw[:, l:], vw[:, l:], s_new)


def _hmoe(x, w_group, b_group, w_expert, b_expert, w_gate, w_up, w_down):
    shp = x.shape
    d = shp[-1]
    xt = x.reshape(-1, d)
    t = xt.shape[0]
    g_logit = (xt @ w_group + b_group).astype(F32)
    grp = jnp.argmax(g_logit, -1)
    p_grp = jnp.take_along_axis(jax.nn.softmax(g_logit, -1), grp[:, None], 1)
    e_logit = (xt @ w_expert + b_expert).astype(F32).reshape(t, N_GROUPS, EXPERTS_PER_GROUP)
    e_logit = jnp.take_along_axis(e_logit, grp[:, None, None], 1)[:, 0]
    p_top, i_top = lax.top_k(jax.nn.softmax(e_logit, -1), EXPERT_TOPK)
    gate = p_grp * p_top / p_top.sum(-1, keepdims=True)
    eid = (grp[:, None] * EXPERTS_PER_GROUP + i_top).reshape(-1)
    a = eid.shape[0]
    order = jnp.argsort(eid)
    e_sorted = eid[order]
    tok = order // EXPERT_TOPK
    counts = jnp.bincount(eid, length=N_EXPERTS)
    starts = jnp.cumsum(counts) - counts
    padded = (counts + MOE_BM - 1) // MOE_BM * MOE_BM
    pad_end = jnp.cumsum(padded)
    pad_start = pad_end - padded
    dest = pad_start[e_sorted] + jnp.arange(a) - starts[e_sorted]
    n_blk = -(-(a + N_EXPERTS * (MOE_BM - 1)) // MOE_BM)
    src = jnp.zeros((n_blk * MOE_BM,), jnp.int32).at[dest].set(tok.astype(jnp.int32))
    blk_e = jnp.minimum(jnp.searchsorted(pad_end, jnp.arange(n_blk) * MOE_BM, side='right'), N_EXPERTS - 1)
    n_used = (pad_end[-1:] // MOE_BM).astype(jnp.int32)
    yb = _expert_ffn(xt[src], blk_e.astype(jnp.int32), n_used, w_gate, w_up, w_down)
    slot = jnp.zeros((a,), jnp.int32).at[order].set(dest.astype(jnp.int32))
    y = yb[slot] * gate.reshape(-1)[:, None]
    return y.reshape(t, EXPERT_TOPK, d).sum(axis=1).reshape(shp)


def kernel(x_prompt, x_sample, cache_cmp_k, cache_cmp_v, cache_sel_k, cache_sel_v, cache_win_k, cache_win_v,
           state_ret, page_table, w_in, w_cmp_k, w_cmp_v, ret_gn_g, w_o, ln1_g, ln1_b, w_group, b_group,
           w_expert, b_expert, w_gate, w_up, w_down, ln2_g, ln2_b):
    win_buf = cache_win_k.shape[2]
    hp, hs = x_prompt, x_sample
    acc_p = [[] for _ in range(7)]
    acc_s = [[] for _ in range(7)]
    for l in range(DEPTH):
        fp, st_p = _prompt_mixer(hp, win_buf, w_in[l], w_cmp_k[l], w_cmp_v[l], ret_gn_g[l])
        fs, st_s = _sample_mixer(hs, cache_cmp_k[l], cache_cmp_v[l], cache_sel_k[l], cache_sel_v[l],
                                 cache_win_k[l], cache_win_v[l], state_ret[l], page_table,
                                 w_in[l], w_cmp_k[l], w_cmp_v[l], ret_gn_g[l])
        moe = (w_group[l], b_group[l], w_expert[l], b_expert[l], w_gate[l], w_up[l], w_down[l])
        w_o_bf16 = w_o[l].astype(BF16)
        tp = hp.shape[0] * hp.shape[1]
        h1p = _wo_ln(fp[0], fp[1], hp.reshape(tp, D_MODEL), w_o_bf16, ln1_g[l], ln1_b[l])
        h1s = _wo_ln(fs[0], fs[1], hs.reshape(-1, D_MODEL), w_o_bf16, ln1_g[l], ln1_b[l])
        h1 = jnp.concatenate([h1p, h1s], axis=0)
        h2 = _layer_norm(DEEPNORM_ALPHA * h1 + _hmoe(h1, *moe), ln2_g[l], ln2_b[l])
        hp = h2[:tp].reshape(hp.shape)
        hs = h2[tp:].reshape(hs.shape)
        for acc, t in zip(acc_p, st_p):
            acc.append(t)
        for acc, t in zip(acc_s, st_s):
            acc.append(t)
    p_cmp_k, p_cmp_v, p_sel_k, p_sel_v, p_win_k, p_win_v, p_ret = [jnp.stack(a) for a in acc_p]
    s_cmp_k, s_cmp_v, s_sel_k, s_sel_v, s_win_k, s_win_v, s_ret = [jnp.stack(a) for a in acc_s]
    return (hp, hs, p_cmp_k, p_cmp_v, p_sel_k, p_sel_v, p_win_k, p_win_v, p_ret.astype(state_ret.dtype),
            s_cmp_k, s_cmp_v, s_sel_k, s_sel_v, s_win_k, s_win_v, s_ret.astype(state_ret.dtype))
```

```python
import functools
import math

import jax
import jax.numpy as jnp
import numpy as np
from jax import lax
from jax.experimental import pallas as pl
from jax.experimental.pallas import tpu as pltpu

D_MODEL = 2048
DEPTH = 1
PAGE_SIZE = 128

F32 = jnp.float32
BF16 = jnp.bfloat16
HEAD_DIM = 128
RET_HEADS = D_MODEL // (2 * HEAD_DIM)
NSA_HEADS = D_MODEL // (2 * HEAD_DIM)
NSA_KV_HEADS = 2
RET_W = RET_HEADS * HEAD_DIM
NSA_W = NSA_HEADS * HEAD_DIM
KV_W = NSA_KV_HEADS * HEAD_DIM
MIX_W = RET_W + NSA_W
RET_CHUNK = 128
RET_ROPE_THETA = 10000.0
ROPE_THETA = 500000.0
ROT_DIM = HEAD_DIM // 4
CMP_BLOCK = 64
SEL_TOPK = 16
WINDOW = 512
WIN_Q_BLOCK = 128
SEL_Q_BLOCK = 64
N_GROUPS = 4
EXPERTS_PER_GROUP = 8
N_EXPERTS = N_GROUPS * EXPERTS_PER_GROUP
EXPERT_TOPK = 2
D_EXPERT = 512
MOE_BLOCK = 128
LN_EPS = 1e-5
GN_EPS = 1e-5
NEG = -1e30
DEEPNORM_ALPHA = (2 * DEPTH) ** 0.25
DEEPNORM_BETA = (8 * DEPTH) ** -0.25
SPLITS = (RET_W, RET_W, RET_W, RET_W, NSA_W, KV_W, KV_W, KV_W, KV_W, KV_W, KV_W, NSA_HEADS * 3)
IN_W = sum(SPLITS)
GATE_W = NSA_HEADS * 3
MAIN_W = IN_W - GATE_W
LANES = 128
VMEM_LIMIT = 48 * 1024 * 1024


def _matmul_kernel(x_ref, w_ref, o_ref):
    o_ref[...] = jnp.dot(x_ref[...].astype(BF16), w_ref[...].astype(BF16), preferred_element_type=F32)


def _matmul(x, w, n_out, tm, tn, name):
    t, k = x.shape
    return pl.pallas_call(
        _matmul_kernel,
        out_shape=jax.ShapeDtypeStruct((t, n_out), F32),
        grid=(t // tm, n_out // tn),
        in_specs=[pl.BlockSpec((tm, k), lambda i, j: (i, 0)),
                  pl.BlockSpec((k, tn), lambda i, j: (0, j))],
        out_specs=pl.BlockSpec((tm, tn), lambda i, j: (i, j)),
        compiler_params=pltpu.CompilerParams(dimension_semantics=("arbitrary", "arbitrary"),
                                             vmem_limit_bytes=VMEM_LIMIT),
        name=name,
    )(x, w)


def _project(x, w_in):
    n, l, d = x.shape
    xt = x.reshape(n * l, d)
    tm = min(512, n * l)
    main = _matmul(xt, w_in, MAIN_W, tm, 512, "in_proj")
    gpg = GATE_W // NSA_KV_HEADS
    w_tail = jnp.concatenate(
        [jnp.pad(w_in[:, MAIN_W + k * gpg:MAIN_W + (k + 1) * gpg], ((0, 0), (0, LANES - gpg)))
         for k in range(NSA_KV_HEADS)], axis=1)
    gl_tiles = _matmul(xt, w_tail, NSA_KV_HEADS * LANES, tm, NSA_KV_HEADS * LANES, "gate_proj")
    return main, gl_tiles


def _split_main(main, n, l, first=0):
    cuts = [0] + [int(c) for c in np.cumsum(SPLITS)[:-1]]
    return [main[:, cuts[i]:cuts[i + 1]].reshape(n, l, -1) for i in range(first, len(SPLITS) - 1)]


def _gate_logits(gl_tiles, n, l):
    gpg = GATE_W // NSA_KV_HEADS
    cols = [gl_tiles[:, k * LANES:k * LANES + gpg] for k in range(NSA_KV_HEADS)]
    return jnp.concatenate(cols, axis=1).reshape(n, l, GATE_W)


def _compress_kernel(k_ref, v_ref, wk_ref, wv_ref, ko_ref, vo_ref):
    r = ko_ref.shape[0]
    ko_ref[...] = jnp.sum(k_ref[...].reshape(r, CMP_BLOCK, KV_W) * wk_ref[...][None], axis=1)
    vo_ref[...] = jnp.sum(v_ref[...].reshape(r, CMP_BLOCK, KV_W) * wv_ref[...][None], axis=1)


def _compress_prompt(ck, cv, w_cmp_k, w_cmp_v):
    t = ck.shape[0]
    r = 32
    wk2 = jnp.tile(w_cmp_k, (1, NSA_KV_HEADS))
    wv2 = jnp.tile(w_cmp_v, (1, NSA_KV_HEADS))
    row = pl.BlockSpec((r * CMP_BLOCK, KV_W), lambda i: (i, 0))
    wsp = pl.BlockSpec((CMP_BLOCK, KV_W), lambda i: (0, 0))
    osp = pl.BlockSpec((r, KV_W), lambda i: (i, 0))
    return pl.pallas_call(
        _compress_kernel,
        out_shape=[jax.ShapeDtypeStruct((t // CMP_BLOCK, KV_W), F32)] * 2,
        grid=(t // (r * CMP_BLOCK),),
        in_specs=[row, row, wsp, wsp],
        out_specs=[osp, osp],
        compiler_params=pltpu.CompilerParams(dimension_semantics=("arbitrary",)),
        name="compress_prompt",
    )(ck, cv, wk2, wv2)


PAGES_PER_STEP = 16
BLOCKS_PER_PAGE = PAGE_SIZE // CMP_BLOCK


SUBLANES = 8
ROWS_PER_BLOCK = CMP_BLOCK * NSA_KV_HEADS
ROWS_PER_PAGE = PAGE_SIZE * NSA_KV_HEADS
BLOCKS_PER_TILE = SUBLANES // NSA_KV_HEADS


def _div(x, n):
    assert n & (n - 1) == 0
    return jnp.right_shift(x, n.bit_length() - 1)


def _mod(x, n):
    assert n & (n - 1) == 0
    return jnp.bitwise_and(x, n - 1)


def _interleaved(pool):
    return pool.reshape(pool.shape[:-3] + (pool.shape[-3] * NSA_KV_HEADS, HEAD_DIM))


def _compress_paged_kernel(pt_ref, *refs):
    del pt_ref
    k_pages = refs[:PAGES_PER_STEP]
    v_pages = refs[PAGES_PER_STEP:2 * PAGES_PER_STEP]
    wk_ref, wv_ref, ko_ref, vo_ref = refs[2 * PAGES_PER_STEP:]
    pair = _div(lax.broadcasted_iota(jnp.int32, (SUBLANES, HEAD_DIM), 0), NSA_KV_HEADS)

    def summaries(pages, w):
        sums = []
        for p in pages:
            for b in range(BLOCKS_PER_PAGE):
                y = p[0, b * ROWS_PER_BLOCK:(b + 1) * ROWS_PER_BLOCK, :] * w
                acc = jnp.sum(y.reshape(ROWS_PER_BLOCK // SUBLANES, SUBLANES, HEAD_DIM), axis=0)
                shift = SUBLANES // 2
                while shift >= NSA_KV_HEADS:
                    acc = acc + pltpu.roll(acc, shift, 0)
                    shift //= 2
                sums.append(acc)
        tiles = []
        for t in range(len(sums) // BLOCKS_PER_TILE):
            tile = sums[t * BLOCKS_PER_TILE]
            for j in range(1, BLOCKS_PER_TILE):
                tile = jnp.where(pair == j, sums[t * BLOCKS_PER_TILE + j], tile)
            tiles.append(tile)
        return jnp.concatenate(tiles, axis=0)

    ko_ref[0] = summaries(k_pages, wk_ref[...])
    vo_ref[0] = summaries(v_pages, wv_ref[...])


def _compress_paged(pool_k, pool_v, page_table, w_cmp_k, w_cmp_v):
    n, n_pages = page_table.shape
    pk = _interleaved(pool_k)
    pv = _interleaved(pool_v)
    wk2 = jnp.repeat(w_cmp_k, NSA_KV_HEADS, axis=0)
    wv2 = jnp.repeat(w_cmp_v, NSA_KV_HEADS, axis=0)

    def page_spec(j):
        return pl.BlockSpec((1, ROWS_PER_PAGE, HEAD_DIM), lambda s, i, pt: (pt[s, i * PAGES_PER_STEP + j], 0, 0))

    wsp = pl.BlockSpec((ROWS_PER_BLOCK, HEAD_DIM), lambda s, i, pt: (0, 0))
    rows = PAGES_PER_STEP * BLOCKS_PER_PAGE * NSA_KV_HEADS
    osp = pl.BlockSpec((1, rows, HEAD_DIM), lambda s, i, pt: (s, i, 0))
    specs = [page_spec(j) for j in range(PAGES_PER_STEP)]
    return pl.pallas_call(
        _compress_paged_kernel,
        out_shape=[jax.ShapeDtypeStruct((n, n_pages * BLOCKS_PER_PAGE * NSA_KV_HEADS, HEAD_DIM), F32)] * 2,
        grid_spec=pltpu.PrefetchScalarGridSpec(
            num_scalar_prefetch=1,
            grid=(n, n_pages // PAGES_PER_STEP),
            in_specs=specs + specs + [wsp, wsp],
            out_specs=[osp, osp]),
        compiler_params=pltpu.CompilerParams(dimension_semantics=("arbitrary", "arbitrary")),
        name="compress_paged",
    )(page_table, *([pk] * PAGES_PER_STEP), *([pv] * PAGES_PER_STEP), wk2, wv2)


SS_SEQ = 8
LOWEST = -3.0e38


def _sample_select_kernel(q_ref, kc_ref, vc_ref, ocmp_ref, sel_ref, *, pos):
    ss, nbk = q_ref.shape[0], kc_ref.shape[1]
    kv, g = NSA_KV_HEADS, NSA_GROUP
    head = lax.broadcasted_iota(jnp.int32, (NSA_HEADS, nbk), 0)
    col = lax.broadcasted_iota(jnp.int32, (NSA_HEADS, nbk), 1)
    m = (_mod(col, kv) == _div(head, g)) & ((_div(col, kv) + 1) * CMP_BLOCK - 1 <= pos)
    width = nbk + LANES
    rows = []
    for i in range(ss):
        q = (q_ref[i] * (HEAD_DIM ** -0.5)).astype(BF16)
        p = _masked_softmax(_dot_nt(q, kc_ref[i].astype(BF16)), m)
        ocmp_ref[i] = jnp.dot(p.astype(BF16), vc_ref[i].astype(BF16), preferred_element_type=F32)
        for k in range(kv):
            imp = jnp.sum(p[k * g:(k + 1) * g], axis=0, keepdims=True)
            rows.append(jnp.concatenate([imp, jnp.zeros((1, LANES), F32)], axis=1))
    nrow = ss * kv
    r_iota = lax.broadcasted_iota(jnp.int32, (nrow, width), 0)
    ccol = lax.broadcasted_iota(jnp.int32, (nrow, width), 1)
    cand = jnp.zeros((nrow, width), F32)
    for r, row in enumerate(rows):
        cand = jnp.where(r_iota == r, row, cand)
    cblk = _div(ccol, kv)
    cur = pos // CMP_BLOCK
    n_sel = -(-(pos + 1) // CMP_BLOCK)
    forced = (cblk == 0) | (cblk == cur) | (cblk == cur - 1)
    score = jnp.where(cblk > cur, NEG, jnp.where(forced, -NEG, cand))
    score = jnp.where((_mod(ccol, kv) == _mod(r_iota, kv)) & (cblk < n_sel), score, LOWEST)
    colf = ccol.astype(F32)
    lane = lax.broadcasted_iota(jnp.int32, (nrow, LANES), 1)
    sel = jnp.zeros((nrow, LANES), jnp.int32)
    for t in range(SEL_TOPK):
        mx = jnp.max(score, axis=1, keepdims=True)
        c = jnp.min(jnp.where(score == mx, colf, -LOWEST), axis=1, keepdims=True)
        picked = jnp.where(mx > 0.5 * NEG, _div(c.astype(jnp.int32), kv), -1)
        sel = jnp.where(lane == t, picked, sel)
        score = jnp.where(colf == c, LOWEST, score)
    sel_ref[...] = sel


def _sample_select(q, kc, vc, pos):
    n = q.shape[0]
    nbk = kc.shape[1]
    qsp = pl.BlockSpec((SS_SEQ, NSA_HEADS, HEAD_DIM), lambda i: (i, 0, 0))
    csp = pl.BlockSpec((SS_SEQ, nbk, HEAD_DIM), lambda i: (i, 0, 0))
    return pl.pallas_call(
        functools.partial(_sample_select_kernel, pos=pos),
        out_shape=[jax.ShapeDtypeStruct((n, NSA_HEADS, HEAD_DIM), F32),
                   jax.ShapeDtypeStruct((n * NSA_KV_HEADS, LANES), jnp.int32)],
        grid=(n // SS_SEQ,),
        in_specs=[qsp, csp, csp],
        out_specs=[qsp, pl.BlockSpec((SS_SEQ * NSA_KV_HEADS, LANES), lambda i: (i, 0))],
        compiler_params=pltpu.CompilerParams(dimension_semantics=("arbitrary",), vmem_limit_bytes=VMEM_LIMIT),
        name="sample_select",
    )(q, kc, vc)


N_SLOTS = NSA_KV_HEADS * SEL_TOPK


def _sample_attend_kernel(sel_ref, pt_ref, q_ref, knew_ref, vnew_ref, wknew_ref, wvnew_ref, wkb_ref, wvb_ref,
                          ocmp_ref, gl_ref, *rest, pos, past_blocks):
    del pt_ref
    kblk, vblk, o_ref = rest[:N_SLOTS], rest[N_SLOTS:2 * N_SLOTS], rest[2 * N_SLOTS]
    kv, g, rb = NSA_KV_HEADS, NSA_GROUP, ROWS_PER_BLOCK
    n = pl.program_id(0)
    q = q_ref[0] * (HEAD_DIM ** -0.5)
    row = lax.broadcasted_iota(jnp.int32, (rb, HEAD_DIM), 0)
    col = lax.broadcasted_iota(jnp.int32, (1, rb), 1)

    def new_block(ref):
        out = jnp.zeros((rb, HEAD_DIM), F32)
        for k in range(kv):
            out = jnp.where(row == k, ref[0, k:k + 1, :], out)
        return out

    def per_head_rows(ref):
        return jnp.concatenate([jnp.broadcast_to(ref[0, k:k + 1, :], (g, HEAD_DIM)) for k in range(kv)], axis=0)

    knew, vnew = new_block(knew_ref), new_block(vnew_ref)
    o_sel = []
    for k in range(kv):
        ks, vs, ms = [], [], []
        for j in range(SEL_TOPK):
            b = sel_ref[(n * kv + k) * SEL_TOPK + j]
            is_new = jnp.broadcast_to(b, (rb, HEAD_DIM)) >= past_blocks
            ks.append(jnp.where(is_new, knew, kblk[k * SEL_TOPK + j][0]))
            vs.append(jnp.where(is_new, vnew, vblk[k * SEL_TOPK + j][0]))
            first = jnp.where(b >= 0, b, 1 << 24) * CMP_BLOCK
            ms.append((first + _div(col, kv) <= pos) & (_mod(col, kv) == k))
        s = _dot_nt(q[k * g:(k + 1) * g].astype(BF16), jnp.concatenate(ks, axis=0).astype(BF16))
        p = _masked_softmax(s, jnp.concatenate(ms, axis=1))
        o_sel.append(jnp.dot(p.astype(BF16), jnp.concatenate(vs, axis=0).astype(BF16), preferred_element_type=F32))
    o_sel = jnp.concatenate(o_sel, axis=0)

    nwr = wkb_ref.shape[1]
    s_w = _dot_nt(q.astype(BF16), wkb_ref[0].astype(BF16))
    cw = lax.broadcasted_iota(jnp.int32, (NSA_HEADS, nwr), 1)
    hw = lax.broadcasted_iota(jnp.int32, (NSA_HEADS, nwr), 0)
    kpos = pos - nwr // kv + _div(cw, kv)
    mw = (kpos <= pos) & (pos - kpos < WINDOW) & (_mod(cw, kv) == _div(hw, g))
    s_n = jnp.sum(q * per_head_rows(wknew_ref), axis=1, keepdims=True)
    smw = jnp.where(mw, s_w, NEG)
    mx = jnp.maximum(jnp.max(smw, axis=1, keepdims=True), s_n)
    e_w = jnp.where(mw, jnp.exp(smw - mx), 0.0)
    e_n = jnp.exp(s_n - mx)
    den = jnp.sum(e_w, axis=1, keepdims=True) + e_n
    o_win = (jnp.dot(e_w.astype(BF16), wvb_ref[0].astype(BF16), preferred_element_type=F32)
             + e_n * per_head_rows(wvnew_ref)) / den

    gate = 1.0 / (1.0 + jnp.exp(-gl_ref[0]))
    o_ref[0] = gate[:, 0:1] * ocmp_ref[0] + gate[:, 1:2] * o_sel + gate[:, 2:3] * o_win


def _sample_attend(sel, page_table, q, sk, sv, wk, wv, win_k, win_v, pool_k, pool_v, o_cmp, gl, pos):
    n, n_pages = page_table.shape
    past_blocks = n_pages * BLOCKS_PER_PAGE

    def slot_spec(k, j):
        def imap(s, sel_r, pt_r):
            b = jnp.maximum(sel_r[(s * NSA_KV_HEADS + k) * SEL_TOPK + j], 0)
            page = jnp.minimum(_div(b, BLOCKS_PER_PAGE), n_pages - 1)
            return (pt_r[s * n_pages + page], _mod(b, BLOCKS_PER_PAGE), 0)
        return pl.BlockSpec((1, ROWS_PER_BLOCK, HEAD_DIM), imap)

    slots = [slot_spec(k, j) for k in range(NSA_KV_HEADS) for j in range(SEL_TOPK)]
    per_seq = lambda a: pl.BlockSpec((1,) + a.shape[1:], lambda s, sel_r, pt_r: (s, 0, 0))
    dense = [q, sk, sv, wk, wv, win_k, win_v, o_cmp, gl]
    return pl.pallas_call(
        functools.partial(_sample_attend_kernel, pos=pos, past_blocks=past_blocks),
        out_shape=jax.ShapeDtypeStruct((n, NSA_HEADS, HEAD_DIM), F32),
        grid_spec=pltpu.PrefetchScalarGridSpec(
            num_scalar_prefetch=2,
            grid=(n,),
            in_specs=[per_seq(a) for a in dense] + slots + slots,
            out_specs=pl.BlockSpec((1, NSA_HEADS, HEAD_DIM), lambda s, sel_r, pt_r: (s, 0, 0))),
        compiler_params=pltpu.CompilerParams(dimension_semantics=("arbitrary",), vmem_limit_bytes=VMEM_LIMIT),
        name="sample_attend",
    )(sel, page_table.reshape(-1), *dense, *([pool_k] * N_SLOTS), *([pool_v] * N_SLOTS))


def _retention_tables(seq):
    c = RET_CHUNK
    log_g = jnp.log1p(-jnp.exp2(-5.0 - jnp.arange(RET_HEADS, dtype=F32)))
    i = jnp.arange(c, dtype=F32)
    rel = i[:, None] - i[None, :]
    dmask = jnp.where(rel[None] >= 0, jnp.exp(jnp.maximum(rel[None], 0.0) * log_g[:, None, None]), 0.0)
    q_dec = jnp.exp((i + 1.0)[None] * log_g[:, None])[..., None]
    k_dec = jnp.exp((c - 1.0 - i)[None] * log_g[:, None])[..., None]
    c_dec = jnp.exp(c * log_g)[:, None, None]
    bc = lambda t: jnp.broadcast_to(t, (RET_HEADS, c, HEAD_DIM))
    inv = RET_ROPE_THETA ** (-jnp.arange(0, HEAD_DIM, 2, dtype=F32) / HEAD_DIM)
    ang = jnp.arange(seq)[:, None].astype(F32) * inv
    cos, sin = jnp.cos(ang), jnp.sin(ang)
    return (dmask, bc(q_dec), bc(k_dec), jnp.broadcast_to(c_dec, (RET_HEADS, 1, HEAD_DIM)),
            jnp.concatenate([cos, cos], -1), jnp.concatenate([-sin, sin], -1))


def _retention_kernel(q_ref, k_ref, v_ref, g_ref, cos_ref, sin_ref, dmask_ref, qdec_ref, kdec_ref, cdec_ref, gn_ref,
                      o_ref, st_ref, s_scr):
    c = pl.program_id(1)

    @pl.when(c == 0)
    def _():
        s_scr[...] = jnp.zeros(s_scr.shape, F32)

    cosf, sins = cos_ref[...], sin_ref[...]
    half = HEAD_DIM // 2
    for h in range(RET_HEADS):
        sl = slice(h * HEAD_DIM, (h + 1) * HEAD_DIM)
        qh, kh = q_ref[:, sl], k_ref[:, sl]
        qr = qh * cosf + pltpu.roll(qh, half, 1) * sins
        kr = (kh * cosf + pltpu.roll(kh, half, 1) * sins) * (HEAD_DIM ** -0.5)
        vb = v_ref[:, sl].astype(BF16)
        att = _dot_nt(qr.astype(BF16), kr.astype(BF16)) * dmask_ref[h]
        s_prev = s_scr[h]
        o = (jnp.dot(att.astype(BF16), vb, preferred_element_type=F32)
             + jnp.dot((qr * qdec_ref[h]).astype(BF16), s_prev.astype(BF16), preferred_element_type=F32))
        s_scr[h] = cdec_ref[h] * s_prev + lax.dot_general(
            (kr * kdec_ref[h]).astype(BF16), vb, (((0,), (0,)), ((), ())), preferred_element_type=F32)
        mu = jnp.mean(o, axis=-1, keepdims=True)
        var = jnp.mean(jnp.square(o - mu), axis=-1, keepdims=True)
        on = (o - mu) * lax.rsqrt(var + GN_EPS) * gn_ref[:, sl]
        gg = g_ref[:, sl]
        o_ref[:, sl] = gg * (1.0 / (1.0 + jnp.exp(-gg))) * on

    @pl.when(c == pl.num_programs(1) - 1)
    def _():
        st_ref[0] = s_scr[...]


def _retention_prompt(main, gn_g, batch, seq):
    nc = seq // RET_CHUNK
    dmask, q_dec, k_dec, c_dec, cosf, sins = _retention_tables(seq)
    col = lambda j: pl.BlockSpec((RET_CHUNK, RET_W), lambda b, c: (b * nc + c, j))
    pos_tab = pl.BlockSpec((RET_CHUNK, HEAD_DIM), lambda b, c: (c, 0))
    full = lambda a: pl.BlockSpec(a.shape, lambda b, c: (0,) * a.ndim)
    gn = gn_g.reshape(1, RET_W)
    return pl.pallas_call(
        _retention_kernel,
        out_shape=[jax.ShapeDtypeStruct((batch * seq, RET_W), F32),
                   jax.ShapeDtypeStruct((batch, RET_HEADS, HEAD_DIM, HEAD_DIM), F32)],
        grid=(batch, nc),
        in_specs=[col(0), col(1), col(2), col(3), pos_tab, pos_tab,
                  full(dmask), full(q_dec), full(k_dec), full(c_dec), full(gn)],
        out_specs=[pl.BlockSpec((RET_CHUNK, RET_W), lambda b, c: (b * nc + c, 0)),
                   pl.BlockSpec((1, RET_HEADS, HEAD_DIM, HEAD_DIM), lambda b, c: (b, 0, 0, 0))],
        scratch_shapes=[pltpu.VMEM((RET_HEADS, HEAD_DIM, HEAD_DIM), F32)],
        compiler_params=pltpu.CompilerParams(dimension_semantics=("arbitrary", "arbitrary"),
                                             vmem_limit_bytes=VMEM_LIMIT),
        name="retention_prompt",
    )(main, main, main, main, cosf, sins, dmask, q_dec, k_dec, c_dec, gn)


NSA_TQ = 128
NSA_KEY_CHUNK = 512
NSA_GROUP = NSA_HEADS // NSA_KV_HEADS


def _dot_nt(a, b):
    return lax.dot_general(a, b, (((1,), (1,)), ((), ())), preferred_element_type=F32)


def _masked_softmax(s, m):
    sm = jnp.where(m, s, NEG)
    e = jnp.exp(sm - jnp.max(sm, axis=-1, keepdims=True))
    return jnp.where(m, e / jnp.sum(e, axis=-1, keepdims=True), 0.0)


def _select_mask(imp, pos):
    nb = imp.shape[1]
    blk = lax.broadcasted_iota(jnp.int32, (1, nb), 1)
    cur = jnp.right_shift(pos, int(math.log2(CMP_BLOCK)))
    forced = (blk == 0) | (blk == cur) | (blk == cur - 1)
    score = jnp.where(blk > cur, NEG, jnp.where(forced, -NEG, imp))
    rank = jnp.zeros(score.shape, jnp.int32)
    for i in range(nb):
        si = score[:, i:i + 1]
        ahead = (si > score) | ((si == score) & (blk > i))
        rank = rank + ahead.astype(jnp.int32)
    return (rank < SEL_TOPK) & (score > 0.5 * NEG)


def _nsa_prompt_kernel(q_ref, kc_ref, vc_ref, sk_ref, sv_ref, wk_ref, wv_ref, gl_ref, o_ref):
    tq, g, kc_n = NSA_TQ, NSA_GROUP, NSA_KEY_CHUNK
    qi = pl.program_id(2)
    q4 = q_ref[...] * (HEAD_DIM ** -0.5)
    qs = jnp.concatenate([q4[:, i * HEAD_DIM:(i + 1) * HEAD_DIM] for i in range(g)], axis=0).astype(BF16)
    pos = qi * tq + lax.broadcasted_iota(jnp.int32, (tq, 1), 0)
    pos4 = jnp.concatenate([pos] * g, axis=0)

    nb = kc_ref.shape[1]
    s_c = _dot_nt(qs, kc_ref[0].astype(BF16))
    blk = lax.broadcasted_iota(jnp.int32, (1, nb), 1)
    p_c = _masked_softmax(s_c, (blk + 1) * CMP_BLOCK - 1 <= pos4)
    o_cmp = jnp.dot(p_c.astype(BF16), vc_ref[0].astype(BF16), preferred_element_type=F32)
    imp = p_c[0:tq]
    for i in range(1, g):
        imp = imp + p_c[i * tq:(i + 1) * tq]

    sel = _select_mask(imp, pos).astype(BF16)
    sel4 = jnp.concatenate([sel] * g, axis=0)
    blk_col = lax.broadcasted_iota(jnp.int32, (nb, kc_n), 0)
    key_col = lax.broadcasted_iota(jnp.int32, (nb, kc_n), 1)
    key_row = lax.broadcasted_iota(jnp.int32, (1, kc_n), 1)

    def chunk(c, carry):
        m_i, l_i, acc = carry
        k0 = pl.multiple_of(c * kc_n, kc_n)
        s = _dot_nt(qs, sk_ref[pl.ds(k0, kc_n), :].astype(BF16))
        expand = (jnp.right_shift(key_col + k0, int(math.log2(CMP_BLOCK))) == blk_col).astype(BF16)
        chosen = jnp.dot(sel4, expand, preferred_element_type=F32)
        msk = (chosen > 0.5) & (key_row + k0 <= pos4)
        sm = jnp.where(msk, s, NEG)
        m_new = jnp.maximum(m_i, jnp.max(sm, axis=-1, keepdims=True))
        alpha = jnp.exp(m_i - m_new)
        p = jnp.where(msk, jnp.exp(sm - m_new), 0.0)
        l_new = alpha * l_i + jnp.sum(p, axis=-1, keepdims=True)
        pv = jnp.dot(p.astype(BF16), sv_ref[pl.ds(k0, kc_n), :].astype(BF16), preferred_element_type=F32)
        return m_new, l_new, alpha * acc + pv

    n_chunks = (qi * tq + tq + kc_n - 1) // kc_n
    init = (jnp.full((g * tq, 1), NEG, F32), jnp.zeros((g * tq, 1), F32), jnp.zeros((g * tq, HEAD_DIM), F32))
    _, l_f, acc_f = lax.fori_loop(0, n_chunks, chunk, init)
    o_sel = acc_f / l_f

    nwk = WINDOW + tq
    kstart = pl.multiple_of(jnp.maximum(qi * tq - WINDOW, 0), tq)
    s_w = _dot_nt(qs, wk_ref[pl.ds(kstart, nwk), :].astype(BF16))
    kpos = kstart + lax.broadcasted_iota(jnp.int32, (1, nwk), 1)
    p_w = _masked_softmax(s_w, (kpos <= pos4) & (pos4 - kpos < WINDOW))
    o_win = jnp.dot(p_w.astype(BF16), wv_ref[pl.ds(kstart, nwk), :].astype(BF16), preferred_element_type=F32)

    gate = 1.0 / (1.0 + jnp.exp(-gl_ref[...]))
    for i in range(g):
        rows = slice(i * tq, (i + 1) * tq)
        o_ref[:, i * HEAD_DIM:(i + 1) * HEAD_DIM] = (gate[:, 3 * i:3 * i + 1] * o_cmp[rows]
                                                     + gate[:, 3 * i + 1:3 * i + 2] * o_sel[rows]
                                                     + gate[:, 3 * i + 2:3 * i + 3] * o_win[rows])


def _nsa_prompt(q, kc, vc, sk, sv, wk, wv, gl, batch, seq):
    nq = seq // NSA_TQ
    gw = NSA_GROUP * HEAD_DIM
    row = lambda b, k, i: (b * nq + i, k)
    seq_spec = pl.BlockSpec((seq, HEAD_DIM), lambda b, k, i: (b, k))
    cmp_spec = pl.BlockSpec((1, seq // CMP_BLOCK, HEAD_DIM), lambda b, k, i: (b, 0, k))
    return pl.pallas_call(
        _nsa_prompt_kernel,
        out_shape=jax.ShapeDtypeStruct((batch * seq, NSA_W), F32),
        grid=(batch, NSA_KV_HEADS, nq),
        in_specs=[pl.BlockSpec((NSA_TQ, gw), row), cmp_spec, cmp_spec,
                  seq_spec, seq_spec, seq_spec, seq_spec,
                  pl.BlockSpec((NSA_TQ, LANES), row)],
        out_specs=pl.BlockSpec((NSA_TQ, gw), row),
        compiler_params=pltpu.CompilerParams(dimension_semantics=("arbitrary",) * 3,
                                             vmem_limit_bytes=VMEM_LIMIT),
        name="nsa_prompt",
    )(q, kc, vc, sk, sv, wk, wv, gl)


WO_TM = 256


def _layer_norm_rows(y, g, b):
    mu = jnp.mean(y, axis=-1, keepdims=True)
    var = jnp.mean(jnp.square(y - mu), axis=-1, keepdims=True)
    return (y - mu) * lax.rsqrt(var + LN_EPS) * g + b


EXPERT_LANE0 = N_GROUPS
R_EID, R_RANK, R_GATE = 0, EXPERT_TOPK, 2 * EXPERT_TOPK


def _route(h, wr_ref, br_ref, carry):
    tm = h.shape[0]
    logit = jnp.dot(h.astype(BF16), wr_ref[...], preferred_element_type=F32) + br_ref[...]
    lane = lax.broadcasted_iota(jnp.int32, (tm, LANES), 1)
    lanef = lane.astype(F32)
    first_lane = lambda hit: jnp.min(jnp.where(hit, lanef, float(LANES)), axis=1, keepdims=True)
    is_g = lane < N_GROUPS
    gl = jnp.where(is_g, logit, LOWEST)
    gmx = jnp.max(gl, axis=1, keepdims=True)
    grp = first_lane(gl == gmx)
    p_grp = 1.0 / jnp.sum(jnp.where(is_g, jnp.exp(gl - gmx), 0.0), axis=1, keepdims=True)
    lane_grp = jnp.right_shift(lane - EXPERT_LANE0, int(math.log2(EXPERTS_PER_GROUP)))
    in_grp = lane_grp.astype(F32) == grp
    el = jnp.where(in_grp, logit, LOWEST)
    ee = jnp.where(in_grp, jnp.exp(el - jnp.max(el, axis=1, keepdims=True)), 0.0)
    pe = jnp.where(in_grp, ee / jnp.sum(ee, axis=1, keepdims=True), -1.0)
    p1 = jnp.max(pe, axis=1, keepdims=True)
    l1 = first_lane(pe == p1)
    pe2 = jnp.where(lanef == l1, -1.0, pe)
    p2 = jnp.max(pe2, axis=1, keepdims=True)
    l2 = first_lane(pe2 == p2)
    den = p1 + p2
    o1, o2 = lanef == l1, lanef == l2
    onehot = jnp.where(o1 | o2, 1.0, 0.0)
    r = lax.broadcasted_iota(jnp.int32, (tm, tm), 0)
    c = lax.broadcasted_iota(jnp.int32, (tm, tm), 1)
    earlier = jnp.where(c < r, 1.0, 0.0).astype(BF16)
    prefix = jnp.dot(earlier, onehot.astype(BF16), preferred_element_type=F32) + carry[0:1, :]
    rank1 = jnp.sum(jnp.where(o1, prefix, 0.0), axis=1, keepdims=True)
    rank2 = jnp.sum(jnp.where(o2, prefix, 0.0), axis=1, keepdims=True)
    carry[0:1, :] = carry[0:1, :] + jnp.sum(onehot, axis=0, keepdims=True)
    fields = [l1 - EXPERT_LANE0, l2 - EXPERT_LANE0, rank1, rank2, p_grp * p1 / den, p_grp * p2 / den]
    rec = jnp.zeros((tm, LANES), F32)
    for j, f in enumerate(fields):
        rec = jnp.where(lane == j, f, rec)
    return rec


def _wo_ln_route_kernel(fr_ref, fn_ref, x_ref, w_ref, g_ref, b_ref, wr_ref, br_ref, base_ref,
                        h_ref, route_ref, cnt_ref, carry):
    i = pl.program_id(0)

    @pl.when(i == 0)
    def _():
        carry[...] = base_ref[...]

    y = (DEEPNORM_ALPHA * x_ref[...]
         + jnp.dot(fr_ref[...].astype(BF16), w_ref[0:RET_W, :], preferred_element_type=F32)
         + jnp.dot(fn_ref[...].astype(BF16), w_ref[RET_W:MIX_W, :], preferred_element_type=F32))
    h = _layer_norm_rows(y, g_ref[...], b_ref[...])
    h_ref[...] = h
    route_ref[...] = _route(h, wr_ref, br_ref, carry)

    @pl.when(i == pl.num_programs(0) - 1)
    def _():
        cnt_ref[...] = carry[...]


def _wo_ln_route(f_ret, f_nsa, x, w_o_bf16, ln_g, ln_b, w_route, b_route, base_counts):
    t, d = x.shape
    tm = min(WO_TM, t)
    assert t % tm == 0
    row = lambda w: pl.BlockSpec((tm, w), lambda i: (i, 0))
    full = lambda a: pl.BlockSpec(a.shape, lambda i: (0,) * a.ndim)
    lg, lb = ln_g.reshape(1, d), ln_b.reshape(1, d)
    return pl.pallas_call(
        _wo_ln_route_kernel,
        out_shape=[jax.ShapeDtypeStruct((t, d), F32), jax.ShapeDtypeStruct((t, LANES), F32),
                   jax.ShapeDtypeStruct((SUBLANES, LANES), F32)],
        grid=(t // tm,),
        in_specs=[row(RET_W), row(NSA_W), row(d), full(w_o_bf16), full(lg), full(lb),
                  full(w_route), full(b_route), full(base_counts)],
        out_specs=[row(d), row(LANES), pl.BlockSpec((SUBLANES, LANES), lambda i: (0, 0))],
        scratch_shapes=[pltpu.VMEM((SUBLANES, LANES), F32)],
        compiler_params=pltpu.CompilerParams(dimension_semantics=("arbitrary",), vmem_limit_bytes=VMEM_LIMIT),
        name="wo_ln1_route",
    )(f_ret, f_nsa, x, w_o_bf16, lg, lb, w_route, b_route, base_counts)


MOE_BM = 256


def _expert_kernel(blk_e_ref, n_used_ref, slot_ref, h_ref, wg_ref, wu_ref, wd_ref, y_ref,
                   asg, xbuf, obuf, gsem, ssem, *, plane, n_rows):
    del blk_e_ref
    bm = xbuf.shape[1]
    i = pl.program_id(0)
    n_used = n_used_ref[0]
    n_asg = slot_ref.shape[0]
    dump0 = EXPERT_TOPK * plane

    def gather(blk, buf_slot):
        def row(r, carry):
            a = asg[blk * bm + r]
            tok = jnp.right_shift(jnp.maximum(a, 0), 1)
            pltpu.make_async_copy(h_ref.at[pl.ds(tok, 1), :], xbuf.at[buf_slot, pl.ds(r, 1), :],
                                  gsem.at[buf_slot]).start()
            return carry
        lax.fori_loop(0, bm, row, 0, unroll=8)

    def scatter(blk, buf_slot):
        def row(r, carry):
            a = asg[blk * bm + r]
            dst = jnp.where(a >= 0, jnp.bitwise_and(a, 1) * plane + jnp.right_shift(a, 1),
                            dump0 + buf_slot * bm + r)
            pltpu.make_async_copy(obuf.at[buf_slot, pl.ds(r, 1), :], y_ref.at[pl.ds(dst, 1), :],
                                  ssem.at[buf_slot]).start()
            return carry
        lax.fori_loop(0, bm, row, 0, unroll=8)

    def wait_block(buf, sem, buf_slot):
        pltpu.make_async_copy(buf.at[buf_slot], buf.at[buf_slot], sem.at[buf_slot]).wait()

    @pl.when(i == 0)
    def _():
        def clear(r, carry):
            asg[r] = -1
            return carry
        lax.fori_loop(0, n_rows, clear, 0, unroll=8)

        def fill(a, carry):
            asg[slot_ref[a]] = a
            return carry
        lax.fori_loop(0, n_asg, fill, 0, unroll=8)
        gather(0, 0)
        n_tok = h_ref.shape[0]
        tail = plane - n_tok
        assert 0 <= tail <= bm
        obuf[1] = jnp.zeros(obuf.shape[1:], F32)
        fills = [(obuf.at[1], y_ref.at[pl.ds(dump0 + s * bm, bm), :]) for s in range(2)]
        if tail:
            fills += [(obuf.at[1, pl.ds(0, tail), :], y_ref.at[pl.ds(k * plane + n_tok, tail), :])
                      for k in range(EXPERT_TOPK)]
        copies = [pltpu.make_async_copy(src, dst, ssem.at[1]) for src, dst in fills]
        for cp in copies:
            cp.start()
        for cp in copies:
            cp.wait()

    slot = jnp.bitwise_and(i, 1)

    @pl.when(i + 1 < n_used)
    def _():
        gather(i + 1, 1 - slot)

    @pl.when(i < n_used)
    def _():
        wait_block(xbuf, gsem, slot)

        @pl.when(i >= 2)
        def _():
            wait_block(obuf, ssem, slot)

        xb = xbuf[slot].astype(BF16)
        hg = jnp.dot(xb, wg_ref[0].astype(BF16), preferred_element_type=F32)
        hu = jnp.dot(xb, wu_ref[0].astype(BF16), preferred_element_type=F32)
        hb = hg * (1.0 / (1.0 + jnp.exp(-hg))) * hu
        obuf[slot] = jnp.dot(hb.astype(BF16), wd_ref[0].astype(BF16), preferred_element_type=F32)
        scatter(i, slot)

    @pl.when(i == pl.num_programs(0) - 1)
    def _():
        @pl.when(n_used >= 2)
        def _():
            wait_block(obuf, ssem, jnp.bitwise_and(n_used, 1))
        wait_block(obuf, ssem, jnp.bitwise_and(n_used - 1, 1))


def _expert_ffn(h, slot, blk_e, n_used, w_gate, w_up, w_down, plane):
    t, d = h.shape
    n_asg = slot.shape[0]
    n_blk = -(-(n_asg + N_EXPERTS * (MOE_BM - 1)) // MOE_BM)
    de = w_gate.shape[2]
    wspec = lambda shape: pl.BlockSpec((1,) + shape, lambda i, be, nu, sl: (be[i], 0, 0))
    return pl.pallas_call(
        functools.partial(_expert_kernel, plane=plane, n_rows=n_blk * MOE_BM),
        out_shape=jax.ShapeDtypeStruct((EXPERT_TOPK * plane + 2 * MOE_BM, d), F32),
        grid_spec=pltpu.PrefetchScalarGridSpec(
            num_scalar_prefetch=3,
            grid=(n_blk,),
            in_specs=[pl.BlockSpec(memory_space=pl.ANY), wspec((d, de)), wspec((d, de)), wspec((de, d))],
            out_specs=pl.BlockSpec(memory_space=pl.ANY),
            scratch_shapes=[pltpu.SMEM((n_blk * MOE_BM,), jnp.int32),
                            pltpu.VMEM((2, MOE_BM, d), F32), pltpu.VMEM((2, MOE_BM, d), F32),
                            pltpu.SemaphoreType.DMA((2,)), pltpu.SemaphoreType.DMA((2,))]),
        compiler_params=pltpu.CompilerParams(dimension_semantics=("arbitrary",), vmem_limit_bytes=VMEM_LIMIT),
        name="expert_ffn",
    )(blk_e, n_used, slot, h, w_gate, w_up, w_down)


def _moe_ln_kernel(h_ref, y0_ref, y1_ref, route_ref, g_ref, b_ref, o_ref):
    rec = route_ref[...]
    y = (DEEPNORM_ALPHA * h_ref[...] + rec[:, R_GATE:R_GATE + 1] * y0_ref[...]
         + rec[:, R_GATE + 1:R_GATE + 2] * y1_ref[...])
    o_ref[...] = _layer_norm_rows(y, g_ref[...], b_ref[...])


def _moe_ln(h, y, route, ln_g, ln_b, row0, n_rows, plane):
    d = h.shape[1]
    tm = min(WO_TM, n_rows)
    assert n_rows % tm == 0 and row0 % tm == 0 and plane % tm == 0 and EXPERT_TOPK == 2
    off = row0 // tm
    row = lambda w, o: pl.BlockSpec((tm, w), lambda i: (i + o, 0))
    vec = pl.BlockSpec((1, d), lambda i: (0, 0))
    return pl.pallas_call(
        _moe_ln_kernel,
        out_shape=jax.ShapeDtypeStruct((n_rows, d), F32),
        grid=(n_rows // tm,),
        in_specs=[row(d, off), row(d, off), row(d, off + plane // tm), row(LANES, off), vec, vec],
        out_specs=row(d, 0),
        compiler_params=pltpu.CompilerParams(dimension_semantics=("arbitrary",), vmem_limit_bytes=VMEM_LIMIT),
        name="moe_ln2",
    )(h, y, y, route, ln_g.reshape(1, d), ln_b.reshape(1, d))


def _layer_norm(x, g, b):
    xf = x.astype(F32)
    mu = xf.mean(-1, keepdims=True)
    var = jnp.square(xf - mu).mean(-1, keepdims=True)
    return ((xf - mu) * lax.rsqrt(var + LN_EPS) * g + b).astype(x.dtype)


def _rope(x, pos, rot_dim, theta):
    half = rot_dim // 2
    inv = theta ** (-jnp.arange(0, rot_dim, 2, dtype=F32) / rot_dim)
    ang = pos[..., None].astype(F32) * inv
    cos = jnp.cos(ang)[:, :, None, :]
    sin = jnp.sin(ang)[:, :, None, :]
    xr = x[..., :rot_dim].astype(F32)
    x1, x2 = xr[..., :half], xr[..., half:]
    rot = jnp.concatenate([x1 * cos - x2 * sin, x2 * cos + x1 * sin], -1).astype(x.dtype)
    return jnp.concatenate([rot, x[..., rot_dim:]], -1)


def _heads(t, n):
    return t.reshape(t.shape[0], t.shape[1], n, HEAD_DIM)


def _chunk_retention(q, k, v, s0):
    n, l, h, d = q.shape
    c = RET_CHUNK if l % RET_CHUNK == 0 else l
    nc = l // c
    log_g = jnp.log1p(-jnp.exp2(-5.0 - jnp.arange(h, dtype=F32)))
    i = jnp.arange(c, dtype=F32)
    rel = i[:, None] - i[None, :]
    dmask = jnp.where(rel[None] >= 0, jnp.exp(jnp.maximum(rel[None], 0.0) * log_g[:, None, None]), 0.0)
    q_dec = jnp.exp((i + 1.0)[None] * log_g[:, None])[..., None]
    k_dec = jnp.exp((c - 1.0 - i)[None] * log_g[:, None])[..., None]
    c_dec = jnp.exp(c * log_g)[:, None, None]

    def to_chunks(t):
        return t.astype(F32).reshape(n, nc, c, h, d).transpose(1, 0, 3, 2, 4)

    def step(s, qkv):
        qc, kc, vc = qkv
        att = jnp.einsum('bhid,bhjd->bhij', qc, kc) * dmask
        o = jnp.einsum('bhij,bhjd->bhid', att, vc) + jnp.einsum('bhid,bhde->bhie', qc * q_dec, s)
        s = c_dec * s + jnp.einsum('bhjd,bhje->bhde', kc * k_dec, vc)
        return s, o

    s, o = lax.scan(step, s0.astype(F32), (to_chunks(q), to_chunks(k), to_chunks(v)))
    return o.transpose(1, 0, 3, 2, 4).reshape(n, l, h, d), s


def _retention_group(rq, rk, rv, rg, pos, s0, gn_g):
    q = _rope(_heads(rq, RET_HEADS), pos, HEAD_DIM, RET_ROPE_THETA)
    k = _rope(_heads(rk, RET_HEADS), pos, HEAD_DIM, RET_ROPE_THETA) * (HEAD_DIM ** -0.5)
    v = _heads(rv, RET_HEADS)
    o, s = _chunk_retention(q, k, v, s0)
    mu = o.mean(-1, keepdims=True)
    var = jnp.square(o - mu).mean(-1, keepdims=True)
    on = (o - mu) * lax.rsqrt(var + GN_EPS) * gn_g.reshape(RET_HEADS, HEAD_DIM).astype(F32)
    out = jax.nn.silu(rg.astype(F32)) * on.reshape(rg.shape)
    return out.astype(rq.dtype), s


def _gqa_attend(q, k, v, mask):
    n, lq, h, d = q.shape
    kv = k.shape[2]
    qg = q.reshape(n, lq, kv, h // kv, d)
    s = jnp.einsum('nqkgd,nskd->nkgqs', qg, k).astype(F32) * (d ** -0.5)
    m = mask[:, None, None]
    p = jax.nn.softmax(jnp.where(m, s, NEG), axis=-1) * m
    o = jnp.einsum('nkgqs,nskd->nqkgd', p.astype(v.dtype), v)
    return o.reshape(n, lq, h, d), p


def _nsa_heads(nq, ck, sk, wk, cv, sv, wv, pos):
    rp = lambda t, nh: _rope(_heads(t, nh), pos, ROT_DIM, ROPE_THETA)
    return (rp(nq, NSA_HEADS), rp(ck, NSA_KV_HEADS), rp(sk, NSA_KV_HEADS), rp(wk, NSA_KV_HEADS),
            _heads(cv, NSA_KV_HEADS), _heads(sv, NSA_KV_HEADS), _heads(wv, NSA_KV_HEADS))


def _compress(rows, w):
    n, t, kv, d = rows.shape
    return jnp.einsum('nbjkd,jd->nbkd', rows.reshape(n, t // CMP_BLOCK, CMP_BLOCK, kv, d), w)


def _cmp_branch(q, pos, kc, vc):
    nb = kc.shape[1]
    blk_end = (jnp.arange(nb) + 1) * CMP_BLOCK - 1
    mask = blk_end[None, None, :] <= pos[:, :, None]
    o, p = _gqa_attend(q, kc, vc, mask)
    imp = p.sum(axis=2).transpose(0, 2, 1, 3)
    return o, imp


def _select_blocks(imp, pos, n_sel):
    nb = imp.shape[-1]
    imp = jnp.pad(imp, ((0, 0), (0, 0), (0, 0), (0, n_sel - nb)))
    blk = jnp.arange(n_sel)
    cur = (pos // CMP_BLOCK)[:, :, None, None]
    forced = (blk == 0) | (blk == cur) | (blk == cur - 1)
    score = jnp.where(blk > cur, NEG, jnp.where(forced, -NEG, imp))
    top, idx = lax.top_k(score, min(SEL_TOPK, n_sel))
    return idx, top > 0.5 * NEG


def _sel_attend(q, pos, ks, vs, idx, valid):
    n, lq, kv, kk, cb, d = ks.shape
    h = q.shape[2]
    kpos = idx[..., None] * CMP_BLOCK + jnp.arange(CMP_BLOCK)
    m = ((kpos <= pos[:, :, None, None, None]) & valid[..., None]).reshape(n, lq, kv, 1, kk * cb)
    qg = q.reshape(n, lq, kv, h // kv, d)
    kf = ks.reshape(n, lq, kv, kk * cb, d)
    vf = vs.reshape(n, lq, kv, kk * cb, d)
    s = jnp.einsum('nqkgd,nqkjd->nqkgj', qg, kf).astype(F32) * (d ** -0.5)
    p = jax.nn.softmax(jnp.where(m, s, NEG), axis=-1) * m
    o = jnp.einsum('nqkgj,nqkjd->nqkgd', p.astype(vf.dtype), vf)
    return o.reshape(n, lq, h, d)


def _sel_prompt(q, pos, k, v, idx, valid):
    b, s, h, d = q.shape
    kv = k.shape[2]
    nb = s // CMP_BLOCK
    nq = s // SEL_Q_BLOCK
    kb = k.reshape(b, nb, CMP_BLOCK, kv, d).transpose(0, 3, 1, 2, 4)
    vb = v.reshape(b, nb, CMP_BLOCK, kv, d).transpose(0, 3, 1, 2, 4)
    bi = jnp.arange(b)[:, None, None, None]
    hi = jnp.arange(kv)[None, None, :, None]

    def blockwise(t):
        return t.reshape(t.shape[0], nq, SEL_Q_BLOCK, *t.shape[2:]).swapaxes(0, 1)

    def one(args):
        qc, pc, ic, vc = args
        return _sel_attend(qc, pc, kb[bi, hi, ic], vb[bi, hi, ic], ic, vc)

    o = lax.map(one, (blockwise(q), blockwise(pos), blockwise(idx), blockwise(valid)))
    return o.swapaxes(0, 1).reshape(b, s, h, d)


def _win_prompt(q, k, v):
    b, s, h, d = q.shape
    kv = k.shape[2]
    nb = s // WIN_Q_BLOCK
    nprev = WINDOW // WIN_Q_BLOCK
    nw = nprev + 1
    padw = ((0, 0), (WINDOW, 0), (0, 0), (0, 0))
    kp = jnp.pad(k, padw).reshape(b, nb + nprev, WIN_Q_BLOCK, kv, d)
    vp = jnp.pad(v, padw).reshape(b, nb + nprev, WIN_Q_BLOCK, kv, d)
    kw = jnp.concatenate([kp[:, i:i + nb] for i in range(nw)], axis=2)
    vw = jnp.concatenate([vp[:, i:i + nb] for i in range(nw)], axis=2)
    qpos = jnp.arange(s).reshape(nb, WIN_Q_BLOCK)
    kpos = (jnp.arange(nb)[:, None] - nprev) * WIN_Q_BLOCK + jnp.arange(nw * WIN_Q_BLOCK)[None]
    qq, kk = qpos[:, :, None], kpos[:, None, :]
    mask = (kk <= qq) & (qq - kk < WINDOW) & (kk >= 0)
    mask = jnp.broadcast_to(mask[None], (b,) + mask.shape).reshape(b * nb, WIN_Q_BLOCK, nw * WIN_Q_BLOCK)
    o, _ = _gqa_attend(q.reshape(b * nb, WIN_Q_BLOCK, h, d), kw.reshape(b * nb, nw * WIN_Q_BLOCK, kv, d),
                       vw.reshape(b * nb, nw * WIN_Q_BLOCK, kv, d), mask)
    return o.reshape(b, s, h, d)


def _gather_selected(pool, new_rows, page_table, idx):
    n, l, kv, d = new_rows.shape
    n_pages = page_table.shape[1]
    past_blocks = n_pages * PAGE_SIZE // CMP_BLOCK
    nbn = -(-l // CMP_BLOCK)
    newb = jnp.pad(new_rows, ((0, 0), (0, nbn * CMP_BLOCK - l), (0, 0), (0, 0)))
    newb = newb.reshape(n, nbn, CMP_BLOCK, kv, d).transpose(0, 3, 1, 2, 4)
    bi = jnp.arange(n)[:, None, None, None]
    hi = jnp.arange(kv)[None, None, :, None]
    start = idx * CMP_BLOCK
    phys = page_table[bi, jnp.minimum(start // PAGE_SIZE, n_pages - 1)]
    off = (start % PAGE_SIZE)[..., None] + jnp.arange(CMP_BLOCK)
    past = pool[phys[..., None], off, hi[..., None]]
    new = newb[bi, hi, jnp.clip(idx - past_blocks, 0, nbn - 1)]
    return jnp.where((idx < past_blocks)[..., None, None], past, new)


def _nsa_combine(gl, o_cmp, o_sel, o_win):
    n, l = gl.shape[0], gl.shape[1]
    g = jax.nn.sigmoid(gl.astype(F32)).reshape(n, l, NSA_HEADS, 3, 1)
    o = g[..., 0, :] * o_cmp + g[..., 1, :] * o_sel + g[..., 2, :] * o_win
    return o.reshape(n, l, NSA_W).astype(o_cmp.dtype)


def _prompt_mixer(x, win_buf, w_in, w_cmp_k, w_cmp_v, gn_g):
    n, s, _ = x.shape
    pos = jnp.arange(s)[None]
    main, gl_tiles = _project(x, w_in)
    ret_out, s_fin = _retention_prompt(main, gn_g, n, s)
    nq, ck, cv, sk, sv, wk, wv = _split_main(main, n, s, first=4)
    q, ck, sk, wk, cv, sv, wv = _nsa_heads(nq, ck, sk, wk, cv, sv, wv, pos)
    flat = lambda a: a.reshape(n * s, -1)
    kc, vc = _compress_prompt(flat(ck), flat(cv), w_cmp_k, w_cmp_v)
    nsa = _nsa_prompt(flat(q), kc.reshape(n, s // CMP_BLOCK, KV_W), vc.reshape(n, s // CMP_BLOCK, KV_W),
                      flat(sk), flat(sv), flat(wk), flat(wv), gl_tiles, n, s)
    feats = (ret_out, nsa)
    if s >= win_buf:
        bk, bv = wk[:, s - win_buf:], wv[:, s - win_buf:]
    else:
        padb = ((0, 0), (win_buf - s, 0), (0, 0), (0, 0))
        bk, bv = jnp.pad(wk, padb), jnp.pad(wv, padb)
    return feats, (ck, cv, sk, sv, bk, bv, s_fin)


def _sample_mixer(x, c_cmp_k, c_cmp_v, c_sel_k, c_sel_v, c_win_k, c_win_v, s_ret, page_table,
                  w_in, w_cmp_k, w_cmp_v, gn_g):
    n, l, _ = x.shape
    past = page_table.shape[1] * PAGE_SIZE
    pos = past + jnp.arange(l)[None]
    main, gl_tiles = _project(x, w_in)
    rq, rk, rv, rg, nq, ck, cv, sk, sv, wk, wv = _split_main(main, n, l)
    gl = _gate_logits(gl_tiles, n, l)
    ret_out, s_new = _retention_group(rq, rk, rv, rg, pos, s_ret, gn_g)
    q, ck, sk, wk, cv, sv, wv = _nsa_heads(nq, ck, sk, wk, cv, sv, wv, pos)
    assert l == 1 and past % CMP_BLOCK == 0 and c_win_k.shape[1] <= WINDOW
    kc, vc = _compress_paged(c_cmp_k, c_cmp_v, page_table, w_cmp_k, w_cmp_v)
    o_cmp, sel = _sample_select(q[:, 0], kc, vc, past)
    sel = sel[:, :SEL_TOPK].reshape(-1)
    gl_pad = jnp.pad(gl.reshape(n, NSA_HEADS, 3), ((0, 0), (0, 0), (0, LANES - 3)))
    nsa = _sample_attend(sel, page_table, q[:, 0], sk[:, 0], sv[:, 0], wk[:, 0], wv[:, 0],
                         _interleaved(c_win_k), _interleaved(c_win_v), _interleaved(c_sel_k), _interleaved(c_sel_v),
                         o_cmp, gl_pad, past)
    feats = (ret_out.reshape(n * l, RET_W), nsa.reshape(n * l, NSA_W))
    kw = jnp.concatenate([c_win_k, wk], 1)
    vw = jnp.concatenate([c_win_v, wv], 1)
    return feats, (ck, cv, sk, sv, kw[:, l:], vw[:, l:], s_new)


def _route_params(w_group, b_group, w_expert, b_expert):
    w = jnp.concatenate([w_group, w_expert], axis=1)
    b = jnp.concatenate([b_group, b_expert], axis=0)
    pad = LANES - w.shape[1]
    return jnp.pad(w, ((0, 0), (0, pad))).astype(BF16), jnp.pad(b, (0, pad)).reshape(1, LANES)


def _dispatch_plan(route, counts_tile):
    counts = counts_tile[0, EXPERT_LANE0:EXPERT_LANE0 + N_EXPERTS].astype(jnp.int32)
    padded = (counts + MOE_BM - 1) // MOE_BM * MOE_BM
    pad_end = jnp.cumsum(padded)
    pad_start = pad_end - padded
    n_asg = route.shape[0] * EXPERT_TOPK
    n_blk = -(-(n_asg + N_EXPERTS * (MOE_BM - 1)) // MOE_BM)
    blk_first = jnp.arange(n_blk, dtype=jnp.int32) * MOE_BM
    blk_e = jnp.minimum(jnp.sum(pad_end[None, :] <= blk_first[:, None], axis=1), N_EXPERTS - 1).astype(jnp.int32)
    n_used = (pad_end[-1:] // MOE_BM).astype(jnp.int32)
    eid = route[:, R_EID:R_EID + EXPERT_TOPK].astype(jnp.int32)
    rank = route[:, R_RANK:R_RANK + EXPERT_TOPK].astype(jnp.int32)
    start = jnp.sum(jnp.where(eid[..., None] == jnp.arange(N_EXPERTS), pad_start, 0), axis=-1)
    return (start + rank).reshape(-1).astype(jnp.int32), blk_e, n_used


def kernel(x_prompt, x_sample, cache_cmp_k, cache_cmp_v, cache_sel_k, cache_sel_v, cache_win_k, cache_win_v,
           state_ret, page_table, w_in, w_cmp_k, w_cmp_v, ret_gn_g, w_o, ln1_g, ln1_b, w_group, b_group,
           w_expert, b_expert, w_gate, w_up, w_down, ln2_g, ln2_b):
    win_buf = cache_win_k.shape[2]
    hp, hs = x_prompt, x_sample
    acc_p = [[] for _ in range(7)]
    acc_s = [[] for _ in range(7)]
    for l in range(DEPTH):
        fp, st_p = _prompt_mixer(hp, win_buf, w_in[l], w_cmp_k[l], w_cmp_v[l], ret_gn_g[l])
        fs, st_s = _sample_mixer(hs, cache_cmp_k[l], cache_cmp_v[l], cache_sel_k[l], cache_sel_v[l],
                                 cache_win_k[l], cache_win_v[l], state_ret[l], page_table,
                                 w_in[l], w_cmp_k[l], w_cmp_v[l], ret_gn_g[l])
        w_o_bf16 = w_o[l].astype(BF16)
        w_route, b_route = _route_params(w_group[l], b_group[l], w_expert[l], b_expert[l])
        tp = hp.shape[0] * hp.shape[1]
        ts = hs.shape[0] * hs.shape[1]
        no_counts = jnp.zeros((SUBLANES, LANES), F32)
        h1p, route_p, counts_p = _wo_ln_route(fp[0], fp[1], hp.reshape(tp, D_MODEL), w_o_bf16, ln1_g[l], ln1_b[l],
                                              w_route, b_route, no_counts)
        h1s, route_s, counts = _wo_ln_route(fs[0], fs[1], hs.reshape(ts, D_MODEL), w_o_bf16, ln1_g[l], ln1_b[l],
                                            w_route, b_route, counts_p)
        h1 = jnp.concatenate([h1p, h1s], axis=0)
        route = jnp.concatenate([route_p, route_s], axis=0)
        slot, blk_e, n_used = _dispatch_plan(route, counts)
        plane = -(-(tp + ts) // WO_TM) * WO_TM
        y = _expert_ffn(h1, slot, blk_e, n_used, w_gate[l], w_up[l], w_down[l], plane)
        hp = _moe_ln(h1, y, route, ln2_g[l], ln2_b[l], 0, tp, plane).reshape(hp.shape)
        hs = _moe_ln(h1, y, route, ln2_g[l], ln2_b[l], tp, ts, plane).reshape(hs.shape)
        for acc, t in zip(acc_p, st_p):
            acc.append(t)
        for acc, t in zip(acc_s, st_s):
            acc.append(t)
    p_cmp_k, p_cmp_v, p_sel_k, p_sel_v, p_win_k, p_win_v, p_ret = [jnp.stack(a) for a in acc_p]
    s_cmp_k, s_cmp_v, s_sel_k, s_sel_v, s_win_k, s_win_v, s_ret = [jnp.stack(a) for a in acc_s]
    return (hp, hs, p_cmp_k, p_cmp_v, p_sel_k, p_sel_v, p_win_k, p_win_v, p_ret.astype(state_ret.dtype),
            s_cmp_k, s_cmp_v, s_sel_k, s_sel_v, s_win_k, s_win_v, s_ret.astype(state_ret.dtype))
```

```python
import functools
import math

import jax
import jax.numpy as jnp
import numpy as np
from jax import lax
from jax.experimental import pallas as pl
from jax.experimental.pallas import tpu as pltpu

D_MODEL = 2048
DEPTH = 1
PAGE_SIZE = 128

F32 = jnp.float32
BF16 = jnp.bfloat16
HEAD_DIM = 128
RET_HEADS = D_MODEL // (2 * HEAD_DIM)
NSA_HEADS = D_MODEL // (2 * HEAD_DIM)
NSA_KV_HEADS = 2
RET_W = RET_HEADS * HEAD_DIM
NSA_W = NSA_HEADS * HEAD_DIM
KV_W = NSA_KV_HEADS * HEAD_DIM
MIX_W = RET_W + NSA_W
RET_CHUNK = 128
RET_ROPE_THETA = 10000.0
ROPE_THETA = 500000.0
ROT_DIM = HEAD_DIM // 4
CMP_BLOCK = 64
SEL_TOPK = 16
WINDOW = 512
WIN_Q_BLOCK = 128
SEL_Q_BLOCK = 64
N_GROUPS = 4
EXPERTS_PER_GROUP = 8
N_EXPERTS = N_GROUPS * EXPERTS_PER_GROUP
EXPERT_TOPK = 2
D_EXPERT = 512
MOE_BLOCK = 128
LN_EPS = 1e-5
GN_EPS = 1e-5
NEG = -1e30
DEEPNORM_ALPHA = (2 * DEPTH) ** 0.25
DEEPNORM_BETA = (8 * DEPTH) ** -0.25
SPLITS = (RET_W, RET_W, RET_W, RET_W, NSA_W, KV_W, KV_W, KV_W, KV_W, KV_W, KV_W, NSA_HEADS * 3)
IN_W = sum(SPLITS)
GATE_W = NSA_HEADS * 3
MAIN_W = IN_W - GATE_W
LANES = 128
VMEM_LIMIT = 48 * 1024 * 1024


def _matmul_kernel(x_ref, w_ref, o_ref):
    o_ref[...] = jnp.dot(x_ref[...].astype(BF16), w_ref[...].astype(BF16), preferred_element_type=F32)


def _matmul(x, w, n_out, tm, tn, name):
    t, k = x.shape
    return pl.pallas_call(
        _matmul_kernel,
        out_shape=jax.ShapeDtypeStruct((t, n_out), F32),
        grid=(t // tm, n_out // tn),
        in_specs=[pl.BlockSpec((tm, k), lambda i, j: (i, 0)),
                  pl.BlockSpec((k, tn), lambda i, j: (0, j))],
        out_specs=pl.BlockSpec((tm, tn), lambda i, j: (i, j)),
        compiler_params=pltpu.CompilerParams(dimension_semantics=("arbitrary", "arbitrary"),
                                             vmem_limit_bytes=VMEM_LIMIT),
        name=name,
    )(x, w)


def _project(x, w_in):
    n, l, d = x.shape
    xt = x.reshape(n * l, d)
    tm = min(1024, n * l)
    main = _matmul(xt, w_in.astype(BF16), MAIN_W, tm, 512, "in_proj")
    gpg = GATE_W // NSA_KV_HEADS
    w_tail = jnp.concatenate(
        [jnp.pad(w_in[:, MAIN_W + k * gpg:MAIN_W + (k + 1) * gpg], ((0, 0), (0, LANES - gpg)))
         for k in range(NSA_KV_HEADS)], axis=1)
    gl_tiles = _matmul(xt, w_tail, NSA_KV_HEADS * LANES, tm, NSA_KV_HEADS * LANES, "gate_proj")
    return main, gl_tiles


def _split_main(main, n, l, first=0):
    cuts = [0] + [int(c) for c in np.cumsum(SPLITS)[:-1]]
    return [main[:, cuts[i]:cuts[i + 1]].reshape(n, l, -1) for i in range(first, len(SPLITS) - 1)]


def _gate_logits(gl_tiles, n, l):
    gpg = GATE_W // NSA_KV_HEADS
    cols = [gl_tiles[:, k * LANES:k * LANES + gpg] for k in range(NSA_KV_HEADS)]
    return jnp.concatenate(cols, axis=1).reshape(n, l, GATE_W)


def _compress_kernel(k_ref, v_ref, wk_ref, wv_ref, ko_ref, vo_ref):
    r = ko_ref.shape[0]
    ko_ref[...] = jnp.sum(k_ref[...].reshape(r, CMP_BLOCK, KV_W) * wk_ref[...][None], axis=1)
    vo_ref[...] = jnp.sum(v_ref[...].reshape(r, CMP_BLOCK, KV_W) * wv_ref[...][None], axis=1)


def _compress_prompt(ck, cv, w_cmp_k, w_cmp_v):
    t = ck.shape[0]
    r = 32
    wk2 = jnp.tile(w_cmp_k, (1, NSA_KV_HEADS))
    wv2 = jnp.tile(w_cmp_v, (1, NSA_KV_HEADS))
    row = pl.BlockSpec((r * CMP_BLOCK, KV_W), lambda i: (i, 0))
    wsp = pl.BlockSpec((CMP_BLOCK, KV_W), lambda i: (0, 0))
    osp = pl.BlockSpec((r, KV_W), lambda i: (i, 0))
    return pl.pallas_call(
        _compress_kernel,
        out_shape=[jax.ShapeDtypeStruct((t // CMP_BLOCK, KV_W), F32)] * 2,
        grid=(t // (r * CMP_BLOCK),),
        in_specs=[row, row, wsp, wsp],
        out_specs=[osp, osp],
        compiler_params=pltpu.CompilerParams(dimension_semantics=("arbitrary",)),
        name="compress_prompt",
    )(ck, cv, wk2, wv2)


PAGES_PER_STEP = 16
BLOCKS_PER_PAGE = PAGE_SIZE // CMP_BLOCK


SUBLANES = 8
ROWS_PER_BLOCK = CMP_BLOCK * NSA_KV_HEADS
ROWS_PER_PAGE = PAGE_SIZE * NSA_KV_HEADS
BLOCKS_PER_TILE = SUBLANES // NSA_KV_HEADS


def _div(x, n):
    assert n & (n - 1) == 0
    return jnp.right_shift(x, n.bit_length() - 1)


def _mod(x, n):
    assert n & (n - 1) == 0
    return jnp.bitwise_and(x, n - 1)


def _interleaved(pool):
    return pool.reshape(pool.shape[:-3] + (pool.shape[-3] * NSA_KV_HEADS, HEAD_DIM))


def _compress_paged_kernel(pt_ref, *refs):
    del pt_ref
    k_pages = refs[:PAGES_PER_STEP]
    v_pages = refs[PAGES_PER_STEP:2 * PAGES_PER_STEP]
    wk_ref, wv_ref, ko_ref, vo_ref = refs[2 * PAGES_PER_STEP:]
    pair = _div(lax.broadcasted_iota(jnp.int32, (SUBLANES, HEAD_DIM), 0), NSA_KV_HEADS)

    def summaries(pages, w):
        sums = []
        for p in pages:
            for b in range(BLOCKS_PER_PAGE):
                y = p[0, b * ROWS_PER_BLOCK:(b + 1) * ROWS_PER_BLOCK, :] * w
                acc = jnp.sum(y.reshape(ROWS_PER_BLOCK // SUBLANES, SUBLANES, HEAD_DIM), axis=0)
                shift = SUBLANES // 2
                while shift >= NSA_KV_HEADS:
                    acc = acc + pltpu.roll(acc, shift, 0)
                    shift //= 2
                sums.append(acc)
        tiles = []
        for t in range(len(sums) // BLOCKS_PER_TILE):
            tile = sums[t * BLOCKS_PER_TILE]
            for j in range(1, BLOCKS_PER_TILE):
                tile = jnp.where(pair == j, sums[t * BLOCKS_PER_TILE + j], tile)
            tiles.append(tile)
        return jnp.concatenate(tiles, axis=0)

    ko_ref[0] = summaries(k_pages, wk_ref[...])
    vo_ref[0] = summaries(v_pages, wv_ref[...])


def _compress_paged(pool_k, pool_v, page_table, w_cmp_k, w_cmp_v):
    n, n_pages = page_table.shape
    pk = _interleaved(pool_k)
    pv = _interleaved(pool_v)
    wk2 = jnp.repeat(w_cmp_k, NSA_KV_HEADS, axis=0)
    wv2 = jnp.repeat(w_cmp_v, NSA_KV_HEADS, axis=0)

    def page_spec(j):
        return pl.BlockSpec((1, ROWS_PER_PAGE, HEAD_DIM), lambda s, i, pt: (pt[s, i * PAGES_PER_STEP + j], 0, 0))

    wsp = pl.BlockSpec((ROWS_PER_BLOCK, HEAD_DIM), lambda s, i, pt: (0, 0))
    rows = PAGES_PER_STEP * BLOCKS_PER_PAGE * NSA_KV_HEADS
    osp = pl.BlockSpec((1, rows, HEAD_DIM), lambda s, i, pt: (s, i, 0))
    specs = [page_spec(j) for j in range(PAGES_PER_STEP)]
    return pl.pallas_call(
        _compress_paged_kernel,
        out_shape=[jax.ShapeDtypeStruct((n, n_pages * BLOCKS_PER_PAGE * NSA_KV_HEADS, HEAD_DIM), F32)] * 2,
        grid_spec=pltpu.PrefetchScalarGridSpec(
            num_scalar_prefetch=1,
            grid=(n, n_pages // PAGES_PER_STEP),
            in_specs=specs + specs + [wsp, wsp],
            out_specs=[osp, osp]),
        compiler_params=pltpu.CompilerParams(dimension_semantics=("arbitrary", "arbitrary")),
        name="compress_paged",
    )(page_table, *([pk] * PAGES_PER_STEP), *([pv] * PAGES_PER_STEP), wk2, wv2)


SS_SEQ = 8
LOWEST = -3.0e38


def _sample_select_kernel(q_ref, kc_ref, vc_ref, ocmp_ref, sel_ref, *, pos):
    ss, nbk = q_ref.shape[0], kc_ref.shape[1]
    kv, g = NSA_KV_HEADS, NSA_GROUP
    head = lax.broadcasted_iota(jnp.int32, (NSA_HEADS, nbk), 0)
    col = lax.broadcasted_iota(jnp.int32, (NSA_HEADS, nbk), 1)
    m = (_mod(col, kv) == _div(head, g)) & ((_div(col, kv) + 1) * CMP_BLOCK - 1 <= pos)
    width = nbk + LANES
    rows = []
    for i in range(ss):
        q = (q_ref[i] * (HEAD_DIM ** -0.5)).astype(BF16)
        p = _masked_softmax(_dot_nt(q, kc_ref[i].astype(BF16)), m)
        ocmp_ref[i] = jnp.dot(p.astype(BF16), vc_ref[i].astype(BF16), preferred_element_type=F32)
        for k in range(kv):
            imp = jnp.sum(p[k * g:(k + 1) * g], axis=0, keepdims=True)
            rows.append(jnp.concatenate([imp, jnp.zeros((1, LANES), F32)], axis=1))
    nrow = ss * kv
    r_iota = lax.broadcasted_iota(jnp.int32, (nrow, width), 0)
    ccol = lax.broadcasted_iota(jnp.int32, (nrow, width), 1)
    cand = jnp.zeros((nrow, width), F32)
    for r, row in enumerate(rows):
        cand = jnp.where(r_iota == r, row, cand)
    cblk = _div(ccol, kv)
    cur = pos // CMP_BLOCK
    n_sel = -(-(pos + 1) // CMP_BLOCK)
    forced = (cblk == 0) | (cblk == cur) | (cblk == cur - 1)
    score = jnp.where(cblk > cur, NEG, jnp.where(forced, -NEG, cand))
    score = jnp.where((_mod(ccol, kv) == _mod(r_iota, kv)) & (cblk < n_sel), score, LOWEST)
    colf = ccol.astype(F32)
    lane = lax.broadcasted_iota(jnp.int32, (nrow, LANES), 1)
    sel = jnp.zeros((nrow, LANES), jnp.int32)
    for t in range(SEL_TOPK):
        mx = jnp.max(score, axis=1, keepdims=True)
        c = jnp.min(jnp.where(score == mx, colf, -LOWEST), axis=1, keepdims=True)
        picked = jnp.where(mx > 0.5 * NEG, _div(c.astype(jnp.int32), kv), -1)
        sel = jnp.where(lane == t, picked, sel)
        score = jnp.where(colf == c, LOWEST, score)
    sel_ref[...] = sel


def _sample_select(q, kc, vc, pos):
    n = q.shape[0]
    nbk = kc.shape[1]
    qsp = pl.BlockSpec((SS_SEQ, NSA_HEADS, HEAD_DIM), lambda i: (i, 0, 0))
    csp = pl.BlockSpec((SS_SEQ, nbk, HEAD_DIM), lambda i: (i, 0, 0))
    return pl.pallas_call(
        functools.partial(_sample_select_kernel, pos=pos),
        out_shape=[jax.ShapeDtypeStruct((n, NSA_HEADS, HEAD_DIM), F32),
                   jax.ShapeDtypeStruct((n * NSA_KV_HEADS, LANES), jnp.int32)],
        grid=(n // SS_SEQ,),
        in_specs=[qsp, csp, csp],
        out_specs=[qsp, pl.BlockSpec((SS_SEQ * NSA_KV_HEADS, LANES), lambda i: (i, 0))],
        compiler_params=pltpu.CompilerParams(dimension_semantics=("arbitrary",), vmem_limit_bytes=VMEM_LIMIT),
        name="sample_select",
    )(q, kc, vc)


N_SLOTS = NSA_KV_HEADS * SEL_TOPK


def _sample_attend_kernel(sel_ref, pt_ref, q_ref, knew_ref, vnew_ref, wknew_ref, wvnew_ref, wkb_ref, wvb_ref,
                          ocmp_ref, gl_ref, *rest, pos, past_blocks):
    del pt_ref
    kblk, vblk, o_ref = rest[:N_SLOTS], rest[N_SLOTS:2 * N_SLOTS], rest[2 * N_SLOTS]
    kv, g, rb = NSA_KV_HEADS, NSA_GROUP, ROWS_PER_BLOCK
    n = pl.program_id(0)
    q = q_ref[0] * (HEAD_DIM ** -0.5)
    row = lax.broadcasted_iota(jnp.int32, (rb, HEAD_DIM), 0)
    col = lax.broadcasted_iota(jnp.int32, (1, rb), 1)

    def new_block(ref):
        out = jnp.zeros((rb, HEAD_DIM), F32)
        for k in range(kv):
            out = jnp.where(row == k, ref[0, k:k + 1, :], out)
        return out

    def per_head_rows(ref):
        return jnp.concatenate([jnp.broadcast_to(ref[0, k:k + 1, :], (g, HEAD_DIM)) for k in range(kv)], axis=0)

    knew, vnew = new_block(knew_ref), new_block(vnew_ref)
    o_sel = []
    for k in range(kv):
        ks, vs, ms = [], [], []
        for j in range(SEL_TOPK):
            b = sel_ref[(n * kv + k) * SEL_TOPK + j]
            is_new = jnp.broadcast_to(b, (rb, HEAD_DIM)) >= past_blocks
            ks.append(jnp.where(is_new, knew, kblk[k * SEL_TOPK + j][0]))
            vs.append(jnp.where(is_new, vnew, vblk[k * SEL_TOPK + j][0]))
            first = jnp.where(b >= 0, b, 1 << 24) * CMP_BLOCK
            ms.append((first + _div(col, kv) <= pos) & (_mod(col, kv) == k))
        s = _dot_nt(q[k * g:(k + 1) * g].astype(BF16), jnp.concatenate(ks, axis=0).astype(BF16))
        p = _masked_softmax(s, jnp.concatenate(ms, axis=1))
        o_sel.append(jnp.dot(p.astype(BF16), jnp.concatenate(vs, axis=0).astype(BF16), preferred_element_type=F32))
    o_sel = jnp.concatenate(o_sel, axis=0)

    nwr = wkb_ref.shape[1]
    s_w = _dot_nt(q.astype(BF16), wkb_ref[0].astype(BF16))
    cw = lax.broadcasted_iota(jnp.int32, (NSA_HEADS, nwr), 1)
    hw = lax.broadcasted_iota(jnp.int32, (NSA_HEADS, nwr), 0)
    kpos = pos - nwr // kv + _div(cw, kv)
    mw = (kpos <= pos) & (pos - kpos < WINDOW) & (_mod(cw, kv) == _div(hw, g))
    s_n = jnp.sum(q * per_head_rows(wknew_ref), axis=1, keepdims=True)
    smw = jnp.where(mw, s_w, NEG)
    mx = jnp.maximum(jnp.max(smw, axis=1, keepdims=True), s_n)
    e_w = jnp.where(mw, jnp.exp(smw - mx), 0.0)
    e_n = jnp.exp(s_n - mx)
    den = jnp.sum(e_w, axis=1, keepdims=True) + e_n
    o_win = (jnp.dot(e_w.astype(BF16), wvb_ref[0].astype(BF16), preferred_element_type=F32)
             + e_n * per_head_rows(wvnew_ref)) / den

    gate = 1.0 / (1.0 + jnp.exp(-gl_ref[0]))
    o_ref[0] = gate[:, 0:1] * ocmp_ref[0] + gate[:, 1:2] * o_sel + gate[:, 2:3] * o_win


def _sample_attend(sel, page_table, q, sk, sv, wk, wv, win_k, win_v, pool_k, pool_v, o_cmp, gl, pos):
    n, n_pages = page_table.shape
    past_blocks = n_pages * BLOCKS_PER_PAGE

    def slot_spec(k, j):
        def imap(s, sel_r, pt_r):
            b = jnp.maximum(sel_r[(s * NSA_KV_HEADS + k) * SEL_TOPK + j], 0)
            page = jnp.minimum(_div(b, BLOCKS_PER_PAGE), n_pages - 1)
            return (pt_r[s * n_pages + page], _mod(b, BLOCKS_PER_PAGE), 0)
        return pl.BlockSpec((1, ROWS_PER_BLOCK, HEAD_DIM), imap)

    slots = [slot_spec(k, j) for k in range(NSA_KV_HEADS) for j in range(SEL_TOPK)]
    per_seq = lambda a: pl.BlockSpec((1,) + a.shape[1:], lambda s, sel_r, pt_r: (s, 0, 0))
    dense = [q, sk, sv, wk, wv, win_k, win_v, o_cmp, gl]
    return pl.pallas_call(
        functools.partial(_sample_attend_kernel, pos=pos, past_blocks=past_blocks),
        out_shape=jax.ShapeDtypeStruct((n, NSA_HEADS, HEAD_DIM), F32),
        grid_spec=pltpu.PrefetchScalarGridSpec(
            num_scalar_prefetch=2,
            grid=(n,),
            in_specs=[per_seq(a) for a in dense] + slots + slots,
            out_specs=pl.BlockSpec((1, NSA_HEADS, HEAD_DIM), lambda s, sel_r, pt_r: (s, 0, 0))),
        compiler_params=pltpu.CompilerParams(dimension_semantics=("arbitrary",), vmem_limit_bytes=VMEM_LIMIT),
        name="sample_attend",
    )(sel, page_table.reshape(-1), *dense, *([pool_k] * N_SLOTS), *([pool_v] * N_SLOTS))


def _retention_tables(seq):
    c = RET_CHUNK
    log_g = jnp.log1p(-jnp.exp2(-5.0 - jnp.arange(RET_HEADS, dtype=F32)))
    i = jnp.arange(c, dtype=F32)
    rel = i[:, None] - i[None, :]
    dmask = jnp.where(rel[None] >= 0, jnp.exp(jnp.maximum(rel[None], 0.0) * log_g[:, None, None]), 0.0)
    q_dec = jnp.exp((i + 1.0)[None] * log_g[:, None])[..., None]
    k_dec = jnp.exp((c - 1.0 - i)[None] * log_g[:, None])[..., None]
    c_dec = jnp.exp(c * log_g)[:, None, None]
    bc = lambda t: jnp.broadcast_to(t, (RET_HEADS, c, HEAD_DIM))
    inv = RET_ROPE_THETA ** (-jnp.arange(0, HEAD_DIM, 2, dtype=F32) / HEAD_DIM)
    ang = jnp.arange(seq)[:, None].astype(F32) * inv
    cos, sin = jnp.cos(ang), jnp.sin(ang)
    return (dmask, bc(q_dec), bc(k_dec), jnp.broadcast_to(c_dec, (RET_HEADS, 1, HEAD_DIM)),
            jnp.concatenate([cos, cos], -1), jnp.concatenate([-sin, sin], -1))


def _retention_kernel(q_ref, k_ref, v_ref, g_ref, cos_ref, sin_ref, dmask_ref, qdec_ref, kdec_ref, cdec_ref, gn_ref,
                      o_ref, st_ref, s_scr):
    c = pl.program_id(1)

    @pl.when(c == 0)
    def _():
        s_scr[...] = jnp.zeros(s_scr.shape, F32)

    cosf, sins = cos_ref[...], sin_ref[...]
    half = HEAD_DIM // 2
    for h in range(RET_HEADS):
        sl = slice(h * HEAD_DIM, (h + 1) * HEAD_DIM)
        qh, kh = q_ref[:, sl], k_ref[:, sl]
        qr = qh * cosf + pltpu.roll(qh, half, 1) * sins
        kr = (kh * cosf + pltpu.roll(kh, half, 1) * sins) * (HEAD_DIM ** -0.5)
        vb = v_ref[:, sl].astype(BF16)
        att = _dot_nt(qr.astype(BF16), kr.astype(BF16)) * dmask_ref[h]
        s_prev = s_scr[h]
        o = (jnp.dot(att.astype(BF16), vb, preferred_element_type=F32)
             + jnp.dot((qr * qdec_ref[h]).astype(BF16), s_prev.astype(BF16), preferred_element_type=F32))
        s_scr[h] = cdec_ref[h] * s_prev + lax.dot_general(
            (kr * kdec_ref[h]).astype(BF16), vb, (((0,), (0,)), ((), ())), preferred_element_type=F32)
        mu = jnp.mean(o, axis=-1, keepdims=True)
        var = jnp.mean(jnp.square(o - mu), axis=-1, keepdims=True)
        on = (o - mu) * lax.rsqrt(var + GN_EPS) * gn_ref[:, sl]
        gg = g_ref[:, sl]
        o_ref[:, sl] = gg * (1.0 / (1.0 + jnp.exp(-gg))) * on

    @pl.when(c == pl.num_programs(1) - 1)
    def _():
        st_ref[0] = s_scr[...]


def _retention_prompt(main, gn_g, batch, seq):
    nc = seq // RET_CHUNK
    dmask, q_dec, k_dec, c_dec, cosf, sins = _retention_tables(seq)
    col = lambda j: pl.BlockSpec((RET_CHUNK, RET_W), lambda b, c: (b * nc + c, j))
    pos_tab = pl.BlockSpec((RET_CHUNK, HEAD_DIM), lambda b, c: (c, 0))
    full = lambda a: pl.BlockSpec(a.shape, lambda b, c: (0,) * a.ndim)
    gn = gn_g.reshape(1, RET_W)
    return pl.pallas_call(
        _retention_kernel,
        out_shape=[jax.ShapeDtypeStruct((batch * seq, RET_W), F32),
                   jax.ShapeDtypeStruct((batch, RET_HEADS, HEAD_DIM, HEAD_DIM), F32)],
        grid=(batch, nc),
        in_specs=[col(0), col(1), col(2), col(3), pos_tab, pos_tab,
                  full(dmask), full(q_dec), full(k_dec), full(c_dec), full(gn)],
        out_specs=[pl.BlockSpec((RET_CHUNK, RET_W), lambda b, c: (b * nc + c, 0)),
                   pl.BlockSpec((1, RET_HEADS, HEAD_DIM, HEAD_DIM), lambda b, c: (b, 0, 0, 0))],
        scratch_shapes=[pltpu.VMEM((RET_HEADS, HEAD_DIM, HEAD_DIM), F32)],
        compiler_params=pltpu.CompilerParams(dimension_semantics=("arbitrary", "arbitrary"),
                                             vmem_limit_bytes=VMEM_LIMIT),
        name="retention_prompt",
    )(main, main, main, main, cosf, sins, dmask, q_dec, k_dec, c_dec, gn)


NSA_TQ = 128
NSA_KEY_CHUNK = 512
NSA_GROUP = NSA_HEADS // NSA_KV_HEADS


def _dot_nt(a, b):
    return lax.dot_general(a, b, (((1,), (1,)), ((), ())), preferred_element_type=F32)


def _masked_softmax(s, m):
    sm = jnp.where(m, s, NEG)
    e = jnp.exp(sm - jnp.max(sm, axis=-1, keepdims=True))
    return jnp.where(m, e / jnp.sum(e, axis=-1, keepdims=True), 0.0)


def _select_mask(imp, pos):
    nb = imp.shape[1]
    blk = lax.broadcasted_iota(jnp.int32, (1, nb), 1)
    cur = jnp.right_shift(pos, int(math.log2(CMP_BLOCK)))
    forced = (blk == 0) | (blk == cur) | (blk == cur - 1)
    score = jnp.where(blk > cur, NEG, jnp.where(forced, -NEG, imp))
    rank = jnp.zeros(score.shape, jnp.int32)
    for i in range(nb):
        si = score[:, i:i + 1]
        ahead = (si > score) | ((si == score) & (blk > i))
        rank = rank + ahead.astype(jnp.int32)
    return (rank < SEL_TOPK) & (score > 0.5 * NEG)


def _nsa_prompt_kernel(q_ref, kc_ref, vc_ref, sk_ref, sv_ref, wk_ref, wv_ref, gl_ref, o_ref):
    tq, g, kc_n = NSA_TQ, NSA_GROUP, NSA_KEY_CHUNK
    qi = pl.program_id(2)
    q4 = q_ref[...] * (HEAD_DIM ** -0.5)
    qs = jnp.concatenate([q4[:, i * HEAD_DIM:(i + 1) * HEAD_DIM] for i in range(g)], axis=0).astype(BF16)
    pos = qi * tq + lax.broadcasted_iota(jnp.int32, (tq, 1), 0)
    pos4 = jnp.concatenate([pos] * g, axis=0)

    nb = kc_ref.shape[1]
    s_c = _dot_nt(qs, kc_ref[0].astype(BF16))
    blk = lax.broadcasted_iota(jnp.int32, (1, nb), 1)
    p_c = _masked_softmax(s_c, (blk + 1) * CMP_BLOCK - 1 <= pos4)
    o_cmp = jnp.dot(p_c.astype(BF16), vc_ref[0].astype(BF16), preferred_element_type=F32)
    imp = p_c[0:tq]
    for i in range(1, g):
        imp = imp + p_c[i * tq:(i + 1) * tq]

    seln = jnp.where(_select_mask(imp, pos), 0.0, NEG).astype(BF16)
    q_aug = jnp.concatenate([qs, jnp.concatenate([seln] * g, axis=0)], axis=1)
    key_idx = lax.broadcasted_iota(jnp.int32, (kc_n, nb), 0)
    key_blk = lax.broadcasted_iota(jnp.int32, (kc_n, nb), 1)
    key_row = lax.broadcasted_iota(jnp.int32, (1, kc_n), 1)

    def scores(k0):
        member = jnp.right_shift(key_idx + k0, int(math.log2(CMP_BLOCK))) == key_blk
        k_aug = jnp.concatenate([sk_ref[pl.ds(k0, kc_n), :].astype(BF16),
                                 jnp.where(member, 1.0, 0.0).astype(BF16)], axis=1)
        return _dot_nt(q_aug, k_aug)

    def values(k0):
        return sv_ref[pl.ds(k0, kc_n), :].astype(BF16)

    assert kc_n % tq == 0
    c_diag = _div(qi, kc_n // tq)
    kd = pl.multiple_of(c_diag * kc_n, kc_n)
    s_d = jnp.where(key_row + kd <= pos4, scores(kd), NEG)
    m_d = jnp.max(s_d, axis=-1, keepdims=True)
    p_d = jnp.exp(s_d - m_d)
    first = (m_d, jnp.sum(p_d, axis=-1, keepdims=True),
             jnp.dot(p_d.astype(BF16), values(kd), preferred_element_type=F32))

    def chunk(c, carry):
        m_i, l_i, acc = carry
        k0 = pl.multiple_of(c * kc_n, kc_n)
        s = scores(k0)
        m_new = jnp.maximum(m_i, jnp.max(s, axis=-1, keepdims=True))
        alpha = jnp.exp(m_i - m_new)
        p = jnp.exp(s - m_new)
        l_new = alpha * l_i + jnp.sum(p, axis=-1, keepdims=True)
        return m_new, l_new, alpha * acc + jnp.dot(p.astype(BF16), values(k0), preferred_element_type=F32)

    _, l_f, acc_f = lax.fori_loop(0, c_diag, chunk, first)
    o_sel = acc_f / l_f

    nwk = WINDOW + tq
    kstart = pl.multiple_of(jnp.maximum(qi * tq - WINDOW, 0), tq)
    kpos = kstart + lax.broadcasted_iota(jnp.int32, (1, nwk), 1)
    band = jnp.where((kpos <= pos) & (pos - kpos < WINDOW), 0.0, NEG)
    s_w = _dot_nt(qs, wk_ref[pl.ds(kstart, nwk), :].astype(BF16)) + jnp.concatenate([band] * g, axis=0)
    e_w = jnp.exp(s_w - jnp.max(s_w, axis=-1, keepdims=True))
    o_win = (jnp.dot(e_w.astype(BF16), wv_ref[pl.ds(kstart, nwk), :].astype(BF16), preferred_element_type=F32)
             / jnp.sum(e_w, axis=-1, keepdims=True))

    gate = 1.0 / (1.0 + jnp.exp(-gl_ref[...]))
    for i in range(g):
        rows = slice(i * tq, (i + 1) * tq)
        o_ref[:, i * HEAD_DIM:(i + 1) * HEAD_DIM] = (gate[:, 3 * i:3 * i + 1] * o_cmp[rows]
                                                     + gate[:, 3 * i + 1:3 * i + 2] * o_sel[rows]
                                                     + gate[:, 3 * i + 2:3 * i + 3] * o_win[rows])


def _nsa_prompt(q, kc, vc, sk, sv, wk, wv, gl, batch, seq):
    nq = seq // NSA_TQ
    gw = NSA_GROUP * HEAD_DIM
    row = lambda b, k, i: (b * nq + i, k)
    seq_spec = pl.BlockSpec((seq, HEAD_DIM), lambda b, k, i: (b, k))
    cmp_spec = pl.BlockSpec((1, seq // CMP_BLOCK, HEAD_DIM), lambda b, k, i: (b, 0, k))
    return pl.pallas_call(
        _nsa_prompt_kernel,
        out_shape=jax.ShapeDtypeStruct((batch * seq, NSA_W), F32),
        grid=(batch, NSA_KV_HEADS, nq),
        in_specs=[pl.BlockSpec((NSA_TQ, gw), row), cmp_spec, cmp_spec,
                  seq_spec, seq_spec, seq_spec, seq_spec,
                  pl.BlockSpec((NSA_TQ, LANES), row)],
        out_specs=pl.BlockSpec((NSA_TQ, gw), row),
        compiler_params=pltpu.CompilerParams(dimension_semantics=("arbitrary",) * 3,
                                             vmem_limit_bytes=VMEM_LIMIT),
        name="nsa_prompt",
    )(q, kc, vc, sk, sv, wk, wv, gl)


WO_TM = 256


def _layer_norm_rows(y, g, b):
    mu = jnp.mean(y, axis=-1, keepdims=True)
    var = jnp.mean(jnp.square(y - mu), axis=-1, keepdims=True)
    return (y - mu) * lax.rsqrt(var + LN_EPS) * g + b


EXPERT_LANE0 = N_GROUPS
R_EID, R_RANK, R_GATE = 0, EXPERT_TOPK, 2 * EXPERT_TOPK


def _route(h, wr_ref, br_ref, carry):
    tm = h.shape[0]
    logit = jnp.dot(h.astype(BF16), wr_ref[...], preferred_element_type=F32) + br_ref[...]
    lane = lax.broadcasted_iota(jnp.int32, (tm, LANES), 1)
    lanef = lane.astype(F32)
    first_lane = lambda hit: jnp.min(jnp.where(hit, lanef, float(LANES)), axis=1, keepdims=True)
    is_g = lane < N_GROUPS
    gl = jnp.where(is_g, logit, LOWEST)
    gmx = jnp.max(gl, axis=1, keepdims=True)
    grp = first_lane(gl == gmx)
    p_grp = 1.0 / jnp.sum(jnp.where(is_g, jnp.exp(gl - gmx), 0.0), axis=1, keepdims=True)
    lane_grp = jnp.right_shift(lane - EXPERT_LANE0, int(math.log2(EXPERTS_PER_GROUP)))
    in_grp = lane_grp.astype(F32) == grp
    el = jnp.where(in_grp, logit, LOWEST)
    ee = jnp.where(in_grp, jnp.exp(el - jnp.max(el, axis=1, keepdims=True)), 0.0)
    pe = jnp.where(in_grp, ee / jnp.sum(ee, axis=1, keepdims=True), -1.0)
    p1 = jnp.max(pe, axis=1, keepdims=True)
    l1 = first_lane(pe == p1)
    pe2 = jnp.where(lanef == l1, -1.0, pe)
    p2 = jnp.max(pe2, axis=1, keepdims=True)
    l2 = first_lane(pe2 == p2)
    den = p1 + p2
    o1, o2 = lanef == l1, lanef == l2
    onehot = jnp.where(o1 | o2, 1.0, 0.0)
    r = lax.broadcasted_iota(jnp.int32, (tm, tm), 0)
    c = lax.broadcasted_iota(jnp.int32, (tm, tm), 1)
    earlier = jnp.where(c < r, 1.0, 0.0).astype(BF16)
    prefix = jnp.dot(earlier, onehot.astype(BF16), preferred_element_type=F32) + carry[0:1, :]
    rank1 = jnp.sum(jnp.where(o1, prefix, 0.0), axis=1, keepdims=True)
    rank2 = jnp.sum(jnp.where(o2, prefix, 0.0), axis=1, keepdims=True)
    carry[0:1, :] = carry[0:1, :] + jnp.sum(onehot, axis=0, keepdims=True)
    fields = [l1 - EXPERT_LANE0, l2 - EXPERT_LANE0, rank1, rank2, p_grp * p1 / den, p_grp * p2 / den]
    rec = jnp.zeros((tm, LANES), F32)
    for j, f in enumerate(fields):
        rec = jnp.where(lane == j, f, rec)
    return rec


def _wo_ln_route_kernel(fr_ref, fn_ref, x_ref, w_ref, g_ref, b_ref, wr_ref, br_ref, base_ref,
                        h_ref, route_ref, cnt_ref, carry):
    i = pl.program_id(0)

    @pl.when(i == 0)
    def _():
        carry[...] = base_ref[...]

    y = (DEEPNORM_ALPHA * x_ref[...]
         + jnp.dot(fr_ref[...].astype(BF16), w_ref[0:RET_W, :], preferred_element_type=F32)
         + jnp.dot(fn_ref[...].astype(BF16), w_ref[RET_W:MIX_W, :], preferred_element_type=F32))
    h = _layer_norm_rows(y, g_ref[...], b_ref[...])
    h_ref[...] = h
    route_ref[...] = _route(h, wr_ref, br_ref, carry)

    @pl.when(i == pl.num_programs(0) - 1)
    def _():
        cnt_ref[...] = carry[...]


def _wo_ln_route(f_ret, f_nsa, x, w_o_bf16, ln_g, ln_b, w_route, b_route, base_counts):
    t, d = x.shape
    tm = min(WO_TM, t)
    assert t % tm == 0
    row = lambda w: pl.BlockSpec((tm, w), lambda i: (i, 0))
    full = lambda a: pl.BlockSpec(a.shape, lambda i: (0,) * a.ndim)
    lg, lb = ln_g.reshape(1, d), ln_b.reshape(1, d)
    return pl.pallas_call(
        _wo_ln_route_kernel,
        out_shape=[jax.ShapeDtypeStruct((t, d), F32), jax.ShapeDtypeStruct((t, LANES), F32),
                   jax.ShapeDtypeStruct((SUBLANES, LANES), F32)],
        grid=(t // tm,),
        in_specs=[row(RET_W), row(NSA_W), row(d), full(w_o_bf16), full(lg), full(lb),
                  full(w_route), full(b_route), full(base_counts)],
        out_specs=[row(d), row(LANES), pl.BlockSpec((SUBLANES, LANES), lambda i: (0, 0))],
        scratch_shapes=[pltpu.VMEM((SUBLANES, LANES), F32)],
        compiler_params=pltpu.CompilerParams(dimension_semantics=("arbitrary",), vmem_limit_bytes=VMEM_LIMIT),
        name="wo_ln1_route",
    )(f_ret, f_nsa, x, w_o_bf16, lg, lb, w_route, b_route, base_counts)


MOE_BM = 256
MOE_SUB = 64
TABLE_UNROLL = 8


def _expert_kernel(blk_e_ref, n_used_ref, slot_ref, h_ref, wg_ref, wu_ref, wd_ref, y_ref,
                   src_tok, dst_row, xbuf, obuf, gsem, ssem, *, plane, n_rows):
    del blk_e_ref
    bm = xbuf.shape[1] * xbuf.shape[2]
    i = pl.program_id(0)
    n_used = n_used_ref[0]
    n_asg = slot_ref.shape[0]
    dump0 = EXPERT_TOPK * plane
    assert bm & (bm - 1) == 0

    n_sub, sub = xbuf.shape[1], xbuf.shape[2]

    def gather(blk, buf_slot):
        def sub_block(j, carry):
            base = blk * bm + j * sub
            for u in range(sub):
                pltpu.make_async_copy(h_ref.at[pl.ds(src_tok[base + u], 1), :],
                                      xbuf.at[buf_slot, j, pl.ds(u, 1), :], gsem.at[buf_slot]).start()
            return carry
        lax.fori_loop(0, n_sub, sub_block, 0)

    def scatter(blk, buf_slot):
        def sub_block(j, carry):
            base = blk * bm + j * sub
            for u in range(sub):
                pltpu.make_async_copy(obuf.at[buf_slot, j, pl.ds(u, 1), :],
                                      y_ref.at[pl.ds(dst_row[base + u], 1), :], ssem.at[buf_slot]).start()
            return carry
        lax.fori_loop(0, n_sub, sub_block, 0)

    def wait_block(buf, sem, buf_slot):
        pltpu.make_async_copy(buf.at[buf_slot], buf.at[buf_slot], sem.at[buf_slot]).wait()

    @pl.when(i == 0)
    def _():
        def clear(t, carry):
            for u in range(TABLE_UNROLL):
                r = t * TABLE_UNROLL + u
                src_tok[r] = 0
                dst_row[r] = dump0 + jnp.bitwise_and(r, 2 * bm - 1)
            return carry
        lax.fori_loop(0, n_used_ref[1], clear, 0)

        def fill(t, carry):
            for u in range(TABLE_UNROLL):
                a = t * TABLE_UNROLL + u
                tok = jnp.right_shift(a, 1)
                src_tok[slot_ref[a]] = tok
                dst_row[slot_ref[a]] = jnp.bitwise_and(a, 1) * plane + tok
            return carry
        lax.fori_loop(0, n_used_ref[2], fill, 0)
        gather(0, 0)
        n_tok = h_ref.shape[0]
        tail = plane - n_tok
        assert 0 <= tail <= bm
        obuf[1] = jnp.zeros(obuf.shape[1:], F32)
        spans = [(dump0, 2 * bm)] + ([(k * plane + n_tok, tail) for k in range(EXPERT_TOPK)] if tail else [])
        copies = []
        for first, count in spans:
            for j in range(-(-count // sub)):
                rows = min(sub, count - j * sub)
                copies.append(pltpu.make_async_copy(obuf.at[1, j % n_sub, pl.ds(0, rows), :],
                                                    y_ref.at[pl.ds(first + j * sub, rows), :], ssem.at[1]))
        for cp in copies:
            cp.start()
        for cp in copies:
            cp.wait()

    slot = jnp.bitwise_and(i, 1)

    @pl.when(i + 1 < n_used)
    def _():
        gather(i + 1, 1 - slot)

    @pl.when(i < n_used)
    def _():
        wait_block(xbuf, gsem, slot)

        @pl.when(i >= 2)
        def _():
            wait_block(obuf, ssem, slot)

        xb = xbuf[slot].reshape(bm, xbuf.shape[3]).astype(BF16)
        hg = jnp.dot(xb, wg_ref[0].astype(BF16), preferred_element_type=F32)
        hu = jnp.dot(xb, wu_ref[0].astype(BF16), preferred_element_type=F32)
        hb = hg * (1.0 / (1.0 + jnp.exp(-hg))) * hu
        yb = jnp.dot(hb.astype(BF16), wd_ref[0].astype(BF16), preferred_element_type=F32)
        obuf[slot] = yb.reshape(obuf.shape[1:])
        scatter(i, slot)

    @pl.when(i == pl.num_programs(0) - 1)
    def _():
        @pl.when(n_used >= 2)
        def _():
            wait_block(obuf, ssem, jnp.bitwise_and(n_used, 1))
        wait_block(obuf, ssem, jnp.bitwise_and(n_used - 1, 1))


def _expert_ffn(h, slot, blk_e, n_used, w_gate, w_up, w_down, plane):
    t, d = h.shape
    n_asg = slot.shape[0]
    assert n_asg == t * EXPERT_TOPK and EXPERT_TOPK == 2
    n_blk = -(-(n_asg + N_EXPERTS * (MOE_BM - 1)) // MOE_BM)
    de = w_gate.shape[2]
    assert n_asg % TABLE_UNROLL == 0 and MOE_BM % TABLE_UNROLL == 0
    n_used = jnp.concatenate([n_used, jnp.array([n_blk * MOE_BM // TABLE_UNROLL, n_asg // TABLE_UNROLL], jnp.int32)])
    wspec = lambda shape: pl.BlockSpec((1,) + shape, lambda i, be, nu, sl: (be[i], 0, 0))
    return pl.pallas_call(
        functools.partial(_expert_kernel, plane=plane, n_rows=n_blk * MOE_BM),
        out_shape=jax.ShapeDtypeStruct((EXPERT_TOPK * plane + 2 * MOE_BM, d), F32),
        grid_spec=pltpu.PrefetchScalarGridSpec(
            num_scalar_prefetch=3,
            grid=(n_blk,),
            in_specs=[pl.BlockSpec(memory_space=pl.ANY), wspec((d, de)), wspec((d, de)), wspec((de, d))],
            out_specs=pl.BlockSpec(memory_space=pl.ANY),
            scratch_shapes=[pltpu.SMEM((n_blk * MOE_BM,), jnp.int32), pltpu.SMEM((n_blk * MOE_BM,), jnp.int32),
                            pltpu.VMEM((2, MOE_BM // MOE_SUB, MOE_SUB, d), F32),
                            pltpu.VMEM((2, MOE_BM // MOE_SUB, MOE_SUB, d), F32),
                            pltpu.SemaphoreType.DMA((2,)), pltpu.SemaphoreType.DMA((2,))]),
        compiler_params=pltpu.CompilerParams(dimension_semantics=("arbitrary",), vmem_limit_bytes=VMEM_LIMIT),
        name="expert_ffn",
    )(blk_e, n_used, slot, h, w_gate, w_up, w_down)


def _moe_ln_kernel(h_ref, y0_ref, y1_ref, route_ref, g_ref, b_ref, o_ref):
    rec = route_ref[...]
    y = (DEEPNORM_ALPHA * h_ref[...] + rec[:, R_GATE:R_GATE + 1] * y0_ref[...]
         + rec[:, R_GATE + 1:R_GATE + 2] * y1_ref[...])
    o_ref[...] = _layer_norm_rows(y, g_ref[...], b_ref[...])


def _moe_ln(h, y, route, ln_g, ln_b, row0, n_rows, plane):
    d = h.shape[1]
    tm = min(WO_TM, n_rows)
    assert n_rows % tm == 0 and row0 % tm == 0 and plane % tm == 0 and EXPERT_TOPK == 2
    off = row0 // tm
    row = lambda w, o: pl.BlockSpec((tm, w), lambda i: (i + o, 0))
    vec = pl.BlockSpec((1, d), lambda i: (0, 0))
    return pl.pallas_call(
        _moe_ln_kernel,
        out_shape=jax.ShapeDtypeStruct((n_rows, d), F32),
        grid=(n_rows // tm,),
        in_specs=[row(d, off), row(d, off), row(d, off + plane // tm), row(LANES, off), vec, vec],
        out_specs=row(d, 0),
        compiler_params=pltpu.CompilerParams(dimension_semantics=("arbitrary",), vmem_limit_bytes=VMEM_LIMIT),
        name="moe_ln2",
    )(h, y, y, route, ln_g.reshape(1, d), ln_b.reshape(1, d))


def _layer_norm(x, g, b):
    xf = x.astype(F32)
    mu = xf.mean(-1, keepdims=True)
    var = jnp.square(xf - mu).mean(-1, keepdims=True)
    return ((xf - mu) * lax.rsqrt(var + LN_EPS) * g + b).astype(x.dtype)


def _rope(x, pos, rot_dim, theta):
    half = rot_dim // 2
    inv = theta ** (-jnp.arange(0, rot_dim, 2, dtype=F32) / rot_dim)
    ang = pos[..., None].astype(F32) * inv
    cos = jnp.cos(ang)[:, :, None, :]
    sin = jnp.sin(ang)[:, :, None, :]
    xr = x[..., :rot_dim].astype(F32)
    x1, x2 = xr[..., :half], xr[..., half:]
    rot = jnp.concatenate([x1 * cos - x2 * sin, x2 * cos + x1 * sin], -1).astype(x.dtype)
    return jnp.concatenate([rot, x[..., rot_dim:]], -1)


def _heads(t, n):
    return t.reshape(t.shape[0], t.shape[1], n, HEAD_DIM)


def _chunk_retention(q, k, v, s0):
    n, l, h, d = q.shape
    c = RET_CHUNK if l % RET_CHUNK == 0 else l
    nc = l // c
    log_g = jnp.log1p(-jnp.exp2(-5.0 - jnp.arange(h, dtype=F32)))
    i = jnp.arange(c, dtype=F32)
    rel = i[:, None] - i[None, :]
    dmask = jnp.where(rel[None] >= 0, jnp.exp(jnp.maximum(rel[None], 0.0) * log_g[:, None, None]), 0.0)
    q_dec = jnp.exp((i + 1.0)[None] * log_g[:, None])[..., None]
    k_dec = jnp.exp((c - 1.0 - i)[None] * log_g[:, None])[..., None]
    c_dec = jnp.exp(c * log_g)[:, None, None]

    def to_chunks(t):
        return t.astype(F32).reshape(n, nc, c, h, d).transpose(1, 0, 3, 2, 4)

    def step(s, qkv):
        qc, kc, vc = qkv
        att = jnp.einsum('bhid,bhjd->bhij', qc, kc) * dmask
        o = jnp.einsum('bhij,bhjd->bhid', att, vc) + jnp.einsum('bhid,bhde->bhie', qc * q_dec, s)
        s = c_dec * s + jnp.einsum('bhjd,bhje->bhde', kc * k_dec, vc)
        return s, o

    s, o = lax.scan(step, s0.astype(F32), (to_chunks(q), to_chunks(k), to_chunks(v)))
    return o.transpose(1, 0, 3, 2, 4).reshape(n, l, h, d), s


def _retention_group(rq, rk, rv, rg, pos, s0, gn_g):
    q = _rope(_heads(rq, RET_HEADS), pos, HEAD_DIM, RET_ROPE_THETA)
    k = _rope(_heads(rk, RET_HEADS), pos, HEAD_DIM, RET_ROPE_THETA) * (HEAD_DIM ** -0.5)
    v = _heads(rv, RET_HEADS)
    o, s = _chunk_retention(q, k, v, s0)
    mu = o.mean(-1, keepdims=True)
    var = jnp.square(o - mu).mean(-1, keepdims=True)
    on = (o - mu) * lax.rsqrt(var + GN_EPS) * gn_g.reshape(RET_HEADS, HEAD_DIM).astype(F32)
    out = jax.nn.silu(rg.astype(F32)) * on.reshape(rg.shape)
    return out.astype(rq.dtype), s


def _gqa_attend(q, k, v, mask):
    n, lq, h, d = q.shape
    kv = k.shape[2]
    qg = q.reshape(n, lq, kv, h // kv, d)
    s = jnp.einsum('nqkgd,nskd->nkgqs', qg, k).astype(F32) * (d ** -0.5)
    m = mask[:, None, None]
    p = jax.nn.softmax(jnp.where(m, s, NEG), axis=-1) * m
    o = jnp.einsum('nkgqs,nskd->nqkgd', p.astype(v.dtype), v)
    return o.reshape(n, lq, h, d), p


def _nsa_heads(nq, ck, sk, wk, cv, sv, wv, pos):
    rp = lambda t, nh: _rope(_heads(t, nh), pos, ROT_DIM, ROPE_THETA)
    return (rp(nq, NSA_HEADS), rp(ck, NSA_KV_HEADS), rp(sk, NSA_KV_HEADS), rp(wk, NSA_KV_HEADS),
            _heads(cv, NSA_KV_HEADS), _heads(sv, NSA_KV_HEADS), _heads(wv, NSA_KV_HEADS))


def _compress(rows, w):
    n, t, kv, d = rows.shape
    return jnp.einsum('nbjkd,jd->nbkd', rows.reshape(n, t // CMP_BLOCK, CMP_BLOCK, kv, d), w)


def _cmp_branch(q, pos, kc, vc):
    nb = kc.shape[1]
    blk_end = (jnp.arange(nb) + 1) * CMP_BLOCK - 1
    mask = blk_end[None, None, :] <= pos[:, :, None]
    o, p = _gqa_attend(q, kc, vc, mask)
    imp = p.sum(axis=2).transpose(0, 2, 1, 3)
    return o, imp


def _select_blocks(imp, pos, n_sel):
    nb = imp.shape[-1]
    imp = jnp.pad(imp, ((0, 0), (0, 0), (0, 0), (0, n_sel - nb)))
    blk = jnp.arange(n_sel)
    cur = (pos // CMP_BLOCK)[:, :, None, None]
    forced = (blk == 0) | (blk == cur) | (blk == cur - 1)
    score = jnp.where(blk > cur, NEG, jnp.where(forced, -NEG, imp))
    top, idx = lax.top_k(score, min(SEL_TOPK, n_sel))
    return idx, top > 0.5 * NEG


def _sel_attend(q, pos, ks, vs, idx, valid):
    n, lq, kv, kk, cb, d = ks.shape
    h = q.shape[2]
    kpos = idx[..., None] * CMP_BLOCK + jnp.arange(CMP_BLOCK)
    m = ((kpos <= pos[:, :, None, None, None]) & valid[..., None]).reshape(n, lq, kv, 1, kk * cb)
    qg = q.reshape(n, lq, kv, h // kv, d)
    kf = ks.reshape(n, lq, kv, kk * cb, d)
    vf = vs.reshape(n, lq, kv, kk * cb, d)
    s = jnp.einsum('nqkgd,nqkjd->nqkgj', qg, kf).astype(F32) * (d ** -0.5)
    p = jax.nn.softmax(jnp.where(m, s, NEG), axis=-1) * m
    o = jnp.einsum('nqkgj,nqkjd->nqkgd', p.astype(vf.dtype), vf)
    return o.reshape(n, lq, h, d)


def _sel_prompt(q, pos, k, v, idx, valid):
    b, s, h, d = q.shape
    kv = k.shape[2]
    nb = s // CMP_BLOCK
    nq = s // SEL_Q_BLOCK
    kb = k.reshape(b, nb, CMP_BLOCK, kv, d).transpose(0, 3, 1, 2, 4)
    vb = v.reshape(b, nb, CMP_BLOCK, kv, d).transpose(0, 3, 1, 2, 4)
    bi = jnp.arange(b)[:, None, None, None]
    hi = jnp.arange(kv)[None, None, :, None]

    def blockwise(t):
        return t.reshape(t.shape[0], nq, SEL_Q_BLOCK, *t.shape[2:]).swapaxes(0, 1)

    def one(args):
        qc, pc, ic, vc = args
        return _sel_attend(qc, pc, kb[bi, hi, ic], vb[bi, hi, ic], ic, vc)

    o = lax.map(one, (blockwise(q), blockwise(pos), blockwise(idx), blockwise(valid)))
    return o.swapaxes(0, 1).reshape(b, s, h, d)


def _win_prompt(q, k, v):
    b, s, h, d = q.shape
    kv = k.shape[2]
    nb = s // WIN_Q_BLOCK
    nprev = WINDOW // WIN_Q_BLOCK
    nw = nprev + 1
    padw = ((0, 0), (WINDOW, 0), (0, 0), (0, 0))
    kp = jnp.pad(k, padw).reshape(b, nb + nprev, WIN_Q_BLOCK, kv, d)
    vp = jnp.pad(v, padw).reshape(b, nb + nprev, WIN_Q_BLOCK, kv, d)
    kw = jnp.concatenate([kp[:, i:i + nb] for i in range(nw)], axis=2)
    vw = jnp.concatenate([vp[:, i:i + nb] for i in range(nw)], axis=2)
    qpos = jnp.arange(s).reshape(nb, WIN_Q_BLOCK)
    kpos = (jnp.arange(nb)[:, None] - nprev) * WIN_Q_BLOCK + jnp.arange(nw * WIN_Q_BLOCK)[None]
    qq, kk = qpos[:, :, None], kpos[:, None, :]
    mask = (kk <= qq) & (qq - kk < WINDOW) & (kk >= 0)
    mask = jnp.broadcast_to(mask[None], (b,) + mask.shape).reshape(b * nb, WIN_Q_BLOCK, nw * WIN_Q_BLOCK)
    o, _ = _gqa_attend(q.reshape(b * nb, WIN_Q_BLOCK, h, d), kw.reshape(b * nb, nw * WIN_Q_BLOCK, kv, d),
                       vw.reshape(b * nb, nw * WIN_Q_BLOCK, kv, d), mask)
    return o.reshape(b, s, h, d)


def _gather_selected(pool, new_rows, page_table, idx):
    n, l, kv, d = new_rows.shape
    n_pages = page_table.shape[1]
    past_blocks = n_pages * PAGE_SIZE // CMP_BLOCK
    nbn = -(-l // CMP_BLOCK)
    newb = jnp.pad(new_rows, ((0, 0), (0, nbn * CMP_BLOCK - l), (0, 0), (0, 0)))
    newb = newb.reshape(n, nbn, CMP_BLOCK, kv, d).transpose(0, 3, 1, 2, 4)
    bi = jnp.arange(n)[:, None, None, None]
    hi = jnp.arange(kv)[None, None, :, None]
    start = idx * CMP_BLOCK
    phys = page_table[bi, jnp.minimum(start // PAGE_SIZE, n_pages - 1)]
    off = (start % PAGE_SIZE)[..., None] + jnp.arange(CMP_BLOCK)
    past = pool[phys[..., None], off, hi[..., None]]
    new = newb[bi, hi, jnp.clip(idx - past_blocks, 0, nbn - 1)]
    return jnp.where((idx < past_blocks)[..., None, None], past, new)


def _nsa_combine(gl, o_cmp, o_sel, o_win):
    n, l = gl.shape[0], gl.shape[1]
    g = jax.nn.sigmoid(gl.astype(F32)).reshape(n, l, NSA_HEADS, 3, 1)
    o = g[..., 0, :] * o_cmp + g[..., 1, :] * o_sel + g[..., 2, :] * o_win
    return o.reshape(n, l, NSA_W).astype(o_cmp.dtype)


def _prompt_mixer(x, win_buf, w_in, w_cmp_k, w_cmp_v, gn_g):
    n, s, _ = x.shape
    pos = jnp.arange(s)[None]
    main, gl_tiles = _project(x, w_in)
    ret_out, s_fin = _retention_prompt(main, gn_g, n, s)
    nq, ck, cv, sk, sv, wk, wv = _split_main(main, n, s, first=4)
    q, ck, sk, wk, cv, sv, wv = _nsa_heads(nq, ck, sk, wk, cv, sv, wv, pos)
    flat = lambda a: a.reshape(n * s, -1)
    kc, vc = _compress_prompt(flat(ck), flat(cv), w_cmp_k, w_cmp_v)
    nsa = _nsa_prompt(flat(q), kc.reshape(n, s // CMP_BLOCK, KV_W), vc.reshape(n, s // CMP_BLOCK, KV_W),
                      flat(sk), flat(sv), flat(wk), flat(wv), gl_tiles, n, s)
    feats = (ret_out, nsa)
    if s >= win_buf:
        bk, bv = wk[:, s - win_buf:], wv[:, s - win_buf:]
    else:
        padb = ((0, 0), (win_buf - s, 0), (0, 0), (0, 0))
        bk, bv = jnp.pad(wk, padb), jnp.pad(wv, padb)
    return feats, (ck, cv, sk, sv, bk, bv, s_fin)


def _sample_mixer(x, c_cmp_k, c_cmp_v, c_sel_k, c_sel_v, c_win_k, c_win_v, s_ret, page_table,
                  w_in, w_cmp_k, w_cmp_v, gn_g):
    n, l, _ = x.shape
    past = page_table.shape[1] * PAGE_SIZE
    pos = past + jnp.arange(l)[None]
    main, gl_tiles = _project(x, w_in)
    rq, rk, rv, rg, nq, ck, cv, sk, sv, wk, wv = _split_main(main, n, l)
    gl = _gate_logits(gl_tiles, n, l)
    ret_out, s_new = _retention_group(rq, rk, rv, rg, pos, s_ret, gn_g)
    q, ck, sk, wk, cv, sv, wv = _nsa_heads(nq, ck, sk, wk, cv, sv, wv, pos)
    assert l == 1 and past % CMP_BLOCK == 0 and c_win_k.shape[1] <= WINDOW
    kc, vc = _compress_paged(c_cmp_k, c_cmp_v, page_table, w_cmp_k, w_cmp_v)
    o_cmp, sel = _sample_select(q[:, 0], kc, vc, past)
    sel = sel[:, :SEL_TOPK].reshape(-1)
    gl_pad = jnp.pad(gl.reshape(n, NSA_HEADS, 3), ((0, 0), (0, 0), (0, LANES - 3)))
    nsa = _sample_attend(sel, page_table, q[:, 0], sk[:, 0], sv[:, 0], wk[:, 0], wv[:, 0],
                         _interleaved(c_win_k), _interleaved(c_win_v), _interleaved(c_sel_k), _interleaved(c_sel_v),
                         o_cmp, gl_pad, past)
    feats = (ret_out.reshape(n * l, RET_W), nsa.reshape(n * l, NSA_W))
    kw = jnp.concatenate([c_win_k, wk], 1)
    vw = jnp.concatenate([c_win_v, wv], 1)
    return feats, (ck, cv, sk, sv, kw[:, l:], vw[:, l:], s_new)


def _route_params(w_group, b_group, w_expert, b_expert):
    w = jnp.concatenate([w_group, w_expert], axis=1)
    b = jnp.concatenate([b_group, b_expert], axis=0)
    pad = LANES - w.shape[1]
    return jnp.pad(w, ((0, 0), (0, pad))).astype(BF16), jnp.pad(b, (0, pad)).reshape(1, LANES)


def _dispatch_plan(route, counts_tile):
    counts = counts_tile[0, EXPERT_LANE0:EXPERT_LANE0 + N_EXPERTS].astype(jnp.int32)
    padded = (counts + MOE_BM - 1) // MOE_BM * MOE_BM
    pad_end = jnp.cumsum(padded)
    pad_start = pad_end - padded
    n_asg = route.shape[0] * EXPERT_TOPK
    n_blk = -(-(n_asg + N_EXPERTS * (MOE_BM - 1)) // MOE_BM)
    blk_first = jnp.arange(n_blk, dtype=jnp.int32) * MOE_BM
    blk_e = jnp.minimum(jnp.sum(pad_end[None, :] <= blk_first[:, None], axis=1), N_EXPERTS - 1).astype(jnp.int32)
    n_used = (pad_end[-1:] // MOE_BM).astype(jnp.int32)
    eid = route[:, R_EID:R_EID + EXPERT_TOPK].astype(jnp.int32)
    rank = route[:, R_RANK:R_RANK + EXPERT_TOPK].astype(jnp.int32)
    start = jnp.sum(jnp.where(eid[..., None] == jnp.arange(N_EXPERTS), pad_start, 0), axis=-1)
    return (start + rank).reshape(-1).astype(jnp.int32), blk_e, n_used


def kernel(x_prompt, x_sample, cache_cmp_k, cache_cmp_v, cache_sel_k, cache_sel_v, cache_win_k, cache_win_v,
           state_ret, page_table, w_in, w_cmp_k, w_cmp_v, ret_gn_g, w_o, ln1_g, ln1_b, w_group, b_group,
           w_expert, b_expert, w_gate, w_up, w_down, ln2_g, ln2_b):
    win_buf = cache_win_k.shape[2]
    hp, hs = x_prompt, x_sample
    acc_p = [[] for _ in range(7)]
    acc_s = [[] for _ in range(7)]
    for l in range(DEPTH):
        fp, st_p = _prompt_mixer(hp, win_buf, w_in[l], w_cmp_k[l], w_cmp_v[l], ret_gn_g[l])
        fs, st_s = _sample_mixer(hs, cache_cmp_k[l], cache_cmp_v[l], cache_sel_k[l], cache_sel_v[l],
                                 cache_win_k[l], cache_win_v[l], state_ret[l], page_table,
                                 w_in[l], w_cmp_k[l], w_cmp_v[l], ret_gn_g[l])
        w_o_bf16 = w_o[l].astype(BF16)
        w_route, b_route = _route_params(w_group[l], b_group[l], w_expert[l], b_expert[l])
        tp = hp.shape[0] * hp.shape[1]
        ts = hs.shape[0] * hs.shape[1]
        no_counts = jnp.zeros((SUBLANES, LANES), F32)
        h1p, route_p, counts_p = _wo_ln_route(fp[0], fp[1], hp.reshape(tp, D_MODEL), w_o_bf16, ln1_g[l], ln1_b[l],
                                              w_route, b_route, no_counts)
        h1s, route_s, counts = _wo_ln_route(fs[0], fs[1], hs.reshape(ts, D_MODEL), w_o_bf16, ln1_g[l], ln1_b[l],
                                            w_route, b_route, counts_p)
        h1 = jnp.concatenate([h1p, h1s], axis=0)
        route = jnp.concatenate([route_p, route_s], axis=0)
        slot, blk_e, n_used = _dispatch_plan(route, counts)
        plane = -(-(tp + ts) // WO_TM) * WO_TM
        y = _expert_ffn(h1, slot, blk_e, n_used, w_gate[l], w_up[l], w_down[l], plane)
        hp = _moe_ln(h1, y, route, ln2_g[l], ln2_b[l], 0, tp, plane).reshape(hp.shape)
        hs = _moe_ln(h1, y, route, ln2_g[l], ln2_b[l], tp, ts, plane).reshape(hs.shape)
        for acc, t in zip(acc_p, st_p):
            acc.append(t)
        for acc, t in zip(acc_s, st_s):
            acc.append(t)
    p_cmp_k, p_cmp_v, p_sel_k, p_sel_v, p_win_k, p_win_v, p_ret = [jnp.stack(a) for a in acc_p]
    s_cmp_k, s_cmp_v, s_sel_k, s_sel_v, s_win_k, s_win_v, s_ret = [jnp.stack(a) for a in acc_s]
    return (hp, hs, p_cmp_k, p_cmp_v, p_sel_k, p_sel_v, p_win_k, p_win_v, p_ret.astype(state_ret.dtype),
            s_cmp_k, s_cmp_v, s_sel_k, s_sel_v, s_win_k, s_win_v, s_ret.astype(state_ret.dtype))
```

```python
import functools
import math

import jax
import jax.numpy as jnp
import numpy as np
from jax import lax
from jax.experimental import pallas as pl
from jax.experimental.pallas import tpu as pltpu

D_MODEL = 2048
DEPTH = 1
PAGE_SIZE = 128

F32 = jnp.float32
BF16 = jnp.bfloat16
HEAD_DIM = 128
RET_HEADS = D_MODEL // (2 * HEAD_DIM)
NSA_HEADS = D_MODEL // (2 * HEAD_DIM)
NSA_KV_HEADS = 2
RET_W = RET_HEADS * HEAD_DIM
NSA_W = NSA_HEADS * HEAD_DIM
KV_W = NSA_KV_HEADS * HEAD_DIM
MIX_W = RET_W + NSA_W
RET_CHUNK = 128
RET_ROPE_THETA = 10000.0
ROPE_THETA = 500000.0
ROT_DIM = HEAD_DIM // 4
CMP_BLOCK = 64
SEL_TOPK = 16
WINDOW = 512
WIN_Q_BLOCK = 128
SEL_Q_BLOCK = 64
N_GROUPS = 4
EXPERTS_PER_GROUP = 8
N_EXPERTS = N_GROUPS * EXPERTS_PER_GROUP
EXPERT_TOPK = 2
D_EXPERT = 512
MOE_BLOCK = 128
LN_EPS = 1e-5
GN_EPS = 1e-5
NEG = -1e30
DEEPNORM_ALPHA = (2 * DEPTH) ** 0.25
DEEPNORM_BETA = (8 * DEPTH) ** -0.25
SPLITS = (RET_W, RET_W, RET_W, RET_W, NSA_W, KV_W, KV_W, KV_W, KV_W, KV_W, KV_W, NSA_HEADS * 3)
IN_W = sum(SPLITS)
GATE_W = NSA_HEADS * 3
MAIN_W = IN_W - GATE_W
LANES = 128
VMEM_LIMIT = 48 * 1024 * 1024


PROJ_TN = 512
GL_COL0 = MAIN_W
PROJ_W = -(-(MAIN_W + NSA_KV_HEADS * LANES) // PROJ_TN) * PROJ_TN
ROTATED_SPLITS = (4, 5, 7, 9)


def _rotated_heads():
    cuts = np.cumsum((0,) + SPLITS)
    tiles = []
    for j in range(PROJ_W // PROJ_TN):
        heads = []
        for h in range(PROJ_TN // HEAD_DIM):
            c0 = j * PROJ_TN + h * HEAD_DIM
            split = int(np.searchsorted(cuts, c0, side="right")) - 1
            heads.append(split in ROTATED_SPLITS and c0 < MAIN_W)
        tiles.append(tuple(heads))
    return tiles


def _nsa_rope_tables(pos):
    half = ROT_DIM // 2
    inv = ROPE_THETA ** (-jnp.arange(0, ROT_DIM, 2, dtype=F32) / ROT_DIM)
    ang = pos[:, None].astype(F32) * inv
    cos, sin = jnp.cos(ang), jnp.sin(ang)
    rest = HEAD_DIM - ROT_DIM
    zeros = jnp.zeros((pos.shape[0], half), F32)
    pad = lambda t, fill: jnp.pad(t, ((0, 0), (0, rest)), constant_values=fill)
    return (pad(jnp.concatenate([cos, cos], 1), 1.0), pad(jnp.concatenate([zeros, sin], 1), 0.0),
            pad(jnp.concatenate([-sin, zeros], 1), 0.0))


def _proj_kernel(x_ref, w_ref, cos_ref, up_ref, dn_ref, o_ref, *, patterns):
    j = pl.program_id(1)
    acc = jnp.dot(x_ref[...].astype(BF16), w_ref[...], preferred_element_type=F32)
    half = ROT_DIM // 2

    def rotated(heads):
        cos, up, dn = cos_ref[...], up_ref[...], dn_ref[...]
        parts = []
        for h, rot in enumerate(heads):
            xh = acc[:, h * HEAD_DIM:(h + 1) * HEAD_DIM]
            if rot:
                xh = xh * cos + pltpu.roll(xh, half, 1) * up + pltpu.roll(xh, HEAD_DIM - half, 1) * dn
            parts.append(xh)
        return jnp.concatenate(parts, axis=1)

    for heads in sorted(set(patterns)):
        tiles = [t for t, p in enumerate(patterns) if p == heads]
        hit = functools.reduce(jnp.logical_or, [j == t for t in tiles])

        @pl.when(hit)
        def _(heads=heads):
            o_ref[...] = rotated(heads) if any(heads) else acc


def _project(x, w_in, pos):
    n, l, d = x.shape
    t = n * l
    xt = x.reshape(t, d)
    tm = min(1024, t)
    gpg = GATE_W // NSA_KV_HEADS
    gate_tiles = [jnp.pad(w_in[:, MAIN_W + k * gpg:MAIN_W + (k + 1) * gpg], ((0, 0), (0, LANES - gpg)))
                  for k in range(NSA_KV_HEADS)]
    fill = jnp.zeros((d, PROJ_W - MAIN_W - NSA_KV_HEADS * LANES), F32)
    w = jnp.concatenate([w_in[:, :MAIN_W]] + gate_tiles + [fill], axis=1).astype(BF16)
    tables = _nsa_rope_tables(pos)
    pb = pos.shape[0] // tm
    tab = pl.BlockSpec((tm, HEAD_DIM), lambda i, j: (i % pb, 0))
    return pl.pallas_call(
        functools.partial(_proj_kernel, patterns=_rotated_heads()),
        out_shape=jax.ShapeDtypeStruct((t, PROJ_W), F32),
        grid=(t // tm, PROJ_W // PROJ_TN),
        in_specs=[pl.BlockSpec((tm, d), lambda i, j: (i, 0)), pl.BlockSpec((d, PROJ_TN), lambda i, j: (0, j)),
                  tab, tab, tab],
        out_specs=pl.BlockSpec((tm, PROJ_TN), lambda i, j: (i, j)),
        compiler_params=pltpu.CompilerParams(dimension_semantics=("arbitrary", "arbitrary"),
                                             vmem_limit_bytes=VMEM_LIMIT),
        name="in_proj",
    )(xt, w, *tables)


def _split_main(main, n, l, first=0):
    cuts = [0] + [int(c) for c in np.cumsum(SPLITS)[:-1]]
    return [main[:, cuts[i]:cuts[i + 1]].reshape(n, l, -1) for i in range(first, len(SPLITS) - 1)]


def _split_col0(i):
    return int(np.cumsum((0,) + SPLITS)[i])


def _gate_logits(main, n, l):
    gpg = GATE_W // NSA_KV_HEADS
    cols = [main[:, GL_COL0 + k * LANES:GL_COL0 + k * LANES + gpg] for k in range(NSA_KV_HEADS)]
    return jnp.concatenate(cols, axis=1).reshape(n, l, GATE_W)


def _compress_kernel(k_ref, v_ref, wk_ref, wv_ref, ko_ref, vo_ref):
    r = ko_ref.shape[0]
    ko_ref[...] = jnp.sum(k_ref[...].reshape(r, CMP_BLOCK, KV_W) * wk_ref[...][None], axis=1)
    vo_ref[...] = jnp.sum(v_ref[...].reshape(r, CMP_BLOCK, KV_W) * wv_ref[...][None], axis=1)


def _compress_prompt(main, w_cmp_k, w_cmp_v):
    t = main.shape[0]
    r = 32
    wk2 = jnp.tile(w_cmp_k, (1, NSA_KV_HEADS))
    wv2 = jnp.tile(w_cmp_v, (1, NSA_KV_HEADS))
    kcol, vcol = _split_col0(5) // KV_W, _split_col0(6) // KV_W
    wsp = pl.BlockSpec((CMP_BLOCK, KV_W), lambda i: (0, 0))
    osp = pl.BlockSpec((r, KV_W), lambda i: (i, 0))
    return pl.pallas_call(
        _compress_kernel,
        out_shape=[jax.ShapeDtypeStruct((t // CMP_BLOCK, KV_W), F32)] * 2,
        grid=(t // (r * CMP_BLOCK),),
        in_specs=[pl.BlockSpec((r * CMP_BLOCK, KV_W), lambda i: (i, kcol)),
                  pl.BlockSpec((r * CMP_BLOCK, KV_W), lambda i: (i, vcol)), wsp, wsp],
        out_specs=[osp, osp],
        compiler_params=pltpu.CompilerParams(dimension_semantics=("arbitrary",)),
        name="compress_prompt",
    )(main, main, wk2, wv2)


PAGES_PER_STEP = 16
BLOCKS_PER_PAGE = PAGE_SIZE // CMP_BLOCK


SUBLANES = 8
ROWS_PER_BLOCK = CMP_BLOCK * NSA_KV_HEADS
ROWS_PER_PAGE = PAGE_SIZE * NSA_KV_HEADS
BLOCKS_PER_TILE = SUBLANES // NSA_KV_HEADS


def _div(x, n):
    assert n & (n - 1) == 0
    return jnp.right_shift(x, n.bit_length() - 1)


def _mod(x, n):
    assert n & (n - 1) == 0
    return jnp.bitwise_and(x, n - 1)


def _interleaved(pool):
    return pool.reshape(pool.shape[:-3] + (pool.shape[-3] * NSA_KV_HEADS, HEAD_DIM))


def _compress_paged_kernel(pt_ref, *refs):
    del pt_ref
    k_pages = refs[:PAGES_PER_STEP]
    v_pages = refs[PAGES_PER_STEP:2 * PAGES_PER_STEP]
    wk_ref, wv_ref, ko_ref, vo_ref = refs[2 * PAGES_PER_STEP:]
    pair = _div(lax.broadcasted_iota(jnp.int32, (SUBLANES, HEAD_DIM), 0), NSA_KV_HEADS)

    def summaries(pages, w):
        sums = []
        for p in pages:
            for b in range(BLOCKS_PER_PAGE):
                y = p[0, b * ROWS_PER_BLOCK:(b + 1) * ROWS_PER_BLOCK, :] * w
                acc = jnp.sum(y.reshape(ROWS_PER_BLOCK // SUBLANES, SUBLANES, HEAD_DIM), axis=0)
                shift = SUBLANES // 2
                while shift >= NSA_KV_HEADS:
                    acc = acc + pltpu.roll(acc, shift, 0)
                    shift //= 2
                sums.append(acc)
        tiles = []
        for t in range(len(sums) // BLOCKS_PER_TILE):
            tile = sums[t * BLOCKS_PER_TILE]
            for j in range(1, BLOCKS_PER_TILE):
                tile = jnp.where(pair == j, sums[t * BLOCKS_PER_TILE + j], tile)
            tiles.append(tile)
        return jnp.concatenate(tiles, axis=0)

    ko_ref[0] = summaries(k_pages, wk_ref[...])
    vo_ref[0] = summaries(v_pages, wv_ref[...])


def _compress_paged(pool_k, pool_v, page_table, w_cmp_k, w_cmp_v):
    n, n_pages = page_table.shape
    pk = _interleaved(pool_k)
    pv = _interleaved(pool_v)
    wk2 = jnp.repeat(w_cmp_k, NSA_KV_HEADS, axis=0)
    wv2 = jnp.repeat(w_cmp_v, NSA_KV_HEADS, axis=0)

    def page_spec(j):
        return pl.BlockSpec((1, ROWS_PER_PAGE, HEAD_DIM), lambda s, i, pt: (pt[s, i * PAGES_PER_STEP + j], 0, 0))

    wsp = pl.BlockSpec((ROWS_PER_BLOCK, HEAD_DIM), lambda s, i, pt: (0, 0))
    rows = PAGES_PER_STEP * BLOCKS_PER_PAGE * NSA_KV_HEADS
    osp = pl.BlockSpec((1, rows, HEAD_DIM), lambda s, i, pt: (s, i, 0))
    specs = [page_spec(j) for j in range(PAGES_PER_STEP)]
    return pl.pallas_call(
        _compress_paged_kernel,
        out_shape=[jax.ShapeDtypeStruct((n, n_pages * BLOCKS_PER_PAGE * NSA_KV_HEADS, HEAD_DIM), F32)] * 2,
        grid_spec=pltpu.PrefetchScalarGridSpec(
            num_scalar_prefetch=1,
            grid=(n, n_pages // PAGES_PER_STEP),
            in_specs=specs + specs + [wsp, wsp],
            out_specs=[osp, osp]),
        compiler_params=pltpu.CompilerParams(dimension_semantics=("arbitrary", "arbitrary")),
        name="compress_paged",
    )(page_table, *([pk] * PAGES_PER_STEP), *([pv] * PAGES_PER_STEP), wk2, wv2)


SS_SEQ = 8
LOWEST = -3.0e38


def _sample_select_kernel(q_ref, kc_ref, vc_ref, ocmp_ref, sel_ref, *, pos):
    ss, nbk = q_ref.shape[0], kc_ref.shape[1]
    kv, g = NSA_KV_HEADS, NSA_GROUP
    head = lax.broadcasted_iota(jnp.int32, (NSA_HEADS, nbk), 0)
    col = lax.broadcasted_iota(jnp.int32, (NSA_HEADS, nbk), 1)
    m = (_mod(col, kv) == _div(head, g)) & ((_div(col, kv) + 1) * CMP_BLOCK - 1 <= pos)
    width = nbk + LANES
    rows = []
    for i in range(ss):
        q = (q_ref[i] * (HEAD_DIM ** -0.5)).astype(BF16)
        p = _masked_softmax(_dot_nt(q, kc_ref[i].astype(BF16)), m)
        ocmp_ref[i] = jnp.dot(p.astype(BF16), vc_ref[i].astype(BF16), preferred_element_type=F32)
        for k in range(kv):
            imp = jnp.sum(p[k * g:(k + 1) * g], axis=0, keepdims=True)
            rows.append(jnp.concatenate([imp, jnp.zeros((1, LANES), F32)], axis=1))
    nrow = ss * kv
    r_iota = lax.broadcasted_iota(jnp.int32, (nrow, width), 0)
    ccol = lax.broadcasted_iota(jnp.int32, (nrow, width), 1)
    cand = jnp.zeros((nrow, width), F32)
    for r, row in enumerate(rows):
        cand = jnp.where(r_iota == r, row, cand)
    cblk = _div(ccol, kv)
    cur = pos // CMP_BLOCK
    n_sel = -(-(pos + 1) // CMP_BLOCK)
    forced = (cblk == 0) | (cblk == cur) | (cblk == cur - 1)
    score = jnp.where(cblk > cur, NEG, jnp.where(forced, -NEG, cand))
    score = jnp.where((_mod(ccol, kv) == _mod(r_iota, kv)) & (cblk < n_sel), score, LOWEST)
    colf = ccol.astype(F32)
    lane = lax.broadcasted_iota(jnp.int32, (nrow, LANES), 1)
    sel = jnp.zeros((nrow, LANES), jnp.int32)
    for t in range(SEL_TOPK):
        mx = jnp.max(score, axis=1, keepdims=True)
        c = jnp.min(jnp.where(score == mx, colf, -LOWEST), axis=1, keepdims=True)
        picked = jnp.where(mx > 0.5 * NEG, _div(c.astype(jnp.int32), kv), -1)
        sel = jnp.where(lane == t, picked, sel)
        score = jnp.where(colf == c, LOWEST, score)
    sel_ref[...] = sel


def _sample_select(q, kc, vc, pos):
    n = q.shape[0]
    nbk = kc.shape[1]
    qsp = pl.BlockSpec((SS_SEQ, NSA_HEADS, HEAD_DIM), lambda i: (i, 0, 0))
    csp = pl.BlockSpec((SS_SEQ, nbk, HEAD_DIM), lambda i: (i, 0, 0))
    return pl.pallas_call(
        functools.partial(_sample_select_kernel, pos=pos),
        out_shape=[jax.ShapeDtypeStruct((n, NSA_HEADS, HEAD_DIM), F32),
                   jax.ShapeDtypeStruct((n * NSA_KV_HEADS, LANES), jnp.int32)],
        grid=(n // SS_SEQ,),
        in_specs=[qsp, csp, csp],
        out_specs=[qsp, pl.BlockSpec((SS_SEQ * NSA_KV_HEADS, LANES), lambda i: (i, 0))],
        compiler_params=pltpu.CompilerParams(dimension_semantics=("arbitrary",), vmem_limit_bytes=VMEM_LIMIT),
        name="sample_select",
    )(q, kc, vc)


N_SLOTS = NSA_KV_HEADS * SEL_TOPK


def _sample_attend_kernel(sel_ref, pt_ref, q_ref, knew_ref, vnew_ref, wknew_ref, wvnew_ref, wkb_ref, wvb_ref,
                          ocmp_ref, gl_ref, *rest, pos, past_blocks):
    del pt_ref
    kblk, vblk, o_ref = rest[:N_SLOTS], rest[N_SLOTS:2 * N_SLOTS], rest[2 * N_SLOTS]
    kv, g, rb = NSA_KV_HEADS, NSA_GROUP, ROWS_PER_BLOCK
    n = pl.program_id(0)
    q = q_ref[0] * (HEAD_DIM ** -0.5)
    row = lax.broadcasted_iota(jnp.int32, (rb, HEAD_DIM), 0)
    col = lax.broadcasted_iota(jnp.int32, (1, rb), 1)

    def new_block(ref):
        out = jnp.zeros((rb, HEAD_DIM), F32)
        for k in range(kv):
            out = jnp.where(row == k, ref[0, k:k + 1, :], out)
        return out

    def per_head_rows(ref):
        return jnp.concatenate([jnp.broadcast_to(ref[0, k:k + 1, :], (g, HEAD_DIM)) for k in range(kv)], axis=0)

    knew, vnew = new_block(knew_ref), new_block(vnew_ref)
    o_sel = []
    for k in range(kv):
        ks, vs, ms = [], [], []
        for j in range(SEL_TOPK):
            b = sel_ref[(n * kv + k) * SEL_TOPK + j]
            is_new = jnp.broadcast_to(b, (rb, HEAD_DIM)) >= past_blocks
            ks.append(jnp.where(is_new, knew, kblk[k * SEL_TOPK + j][0]))
            vs.append(jnp.where(is_new, vnew, vblk[k * SEL_TOPK + j][0]))
            first = jnp.where(b >= 0, b, 1 << 24) * CMP_BLOCK
            ms.append((first + _div(col, kv) <= pos) & (_mod(col, kv) == k))
        s = _dot_nt(q[k * g:(k + 1) * g].astype(BF16), jnp.concatenate(ks, axis=0).astype(BF16))
        p = _masked_softmax(s, jnp.concatenate(ms, axis=1))
        o_sel.append(jnp.dot(p.astype(BF16), jnp.concatenate(vs, axis=0).astype(BF16), preferred_element_type=F32))
    o_sel = jnp.concatenate(o_sel, axis=0)

    nwr = wkb_ref.shape[1]
    s_w = _dot_nt(q.astype(BF16), wkb_ref[0].astype(BF16))
    cw = lax.broadcasted_iota(jnp.int32, (NSA_HEADS, nwr), 1)
    hw = lax.broadcasted_iota(jnp.int32, (NSA_HEADS, nwr), 0)
    kpos = pos - nwr // kv + _div(cw, kv)
    mw = (kpos <= pos) & (pos - kpos < WINDOW) & (_mod(cw, kv) == _div(hw, g))
    s_n = jnp.sum(q * per_head_rows(wknew_ref), axis=1, keepdims=True)
    smw = jnp.where(mw, s_w, NEG)
    mx = jnp.maximum(jnp.max(smw, axis=1, keepdims=True), s_n)
    e_w = jnp.where(mw, jnp.exp(smw - mx), 0.0)
    e_n = jnp.exp(s_n - mx)
    den = jnp.sum(e_w, axis=1, keepdims=True) + e_n
    o_win = (jnp.dot(e_w.astype(BF16), wvb_ref[0].astype(BF16), preferred_element_type=F32)
             + e_n * per_head_rows(wvnew_ref)) / den

    gate = 1.0 / (1.0 + jnp.exp(-gl_ref[0]))
    o_ref[0] = gate[:, 0:1] * ocmp_ref[0] + gate[:, 1:2] * o_sel + gate[:, 2:3] * o_win


def _sample_attend(sel, page_table, q, sk, sv, wk, wv, win_k, win_v, pool_k, pool_v, o_cmp, gl, pos):
    n, n_pages = page_table.shape
    past_blocks = n_pages * BLOCKS_PER_PAGE

    def slot_spec(k, j):
        def imap(s, sel_r, pt_r):
            b = jnp.maximum(sel_r[(s * NSA_KV_HEADS + k) * SEL_TOPK + j], 0)
            page = jnp.minimum(_div(b, BLOCKS_PER_PAGE), n_pages - 1)
            return (pt_r[s * n_pages + page], _mod(b, BLOCKS_PER_PAGE), 0)
        return pl.BlockSpec((1, ROWS_PER_BLOCK, HEAD_DIM), imap)

    slots = [slot_spec(k, j) for k in range(NSA_KV_HEADS) for j in range(SEL_TOPK)]
    per_seq = lambda a: pl.BlockSpec((1,) + a.shape[1:], lambda s, sel_r, pt_r: (s, 0, 0))
    dense = [q, sk, sv, wk, wv, win_k, win_v, o_cmp, gl]
    return pl.pallas_call(
        functools.partial(_sample_attend_kernel, pos=pos, past_blocks=past_blocks),
        out_shape=jax.ShapeDtypeStruct((n, NSA_HEADS, HEAD_DIM), F32),
        grid_spec=pltpu.PrefetchScalarGridSpec(
            num_scalar_prefetch=2,
            grid=(n,),
            in_specs=[per_seq(a) for a in dense] + slots + slots,
            out_specs=pl.BlockSpec((1, NSA_HEADS, HEAD_DIM), lambda s, sel_r, pt_r: (s, 0, 0))),
        compiler_params=pltpu.CompilerParams(dimension_semantics=("arbitrary",), vmem_limit_bytes=VMEM_LIMIT),
        name="sample_attend",
    )(sel, page_table.reshape(-1), *dense, *([pool_k] * N_SLOTS), *([pool_v] * N_SLOTS))


def _retention_tables(seq):
    c = RET_CHUNK
    log_g = jnp.log1p(-jnp.exp2(-5.0 - jnp.arange(RET_HEADS, dtype=F32)))
    i = jnp.arange(c, dtype=F32)
    rel = i[:, None] - i[None, :]
    dmask = jnp.where(rel[None] >= 0, jnp.exp(jnp.maximum(rel[None], 0.0) * log_g[:, None, None]), 0.0)
    q_dec = jnp.exp((i + 1.0)[None] * log_g[:, None])[..., None]
    k_dec = jnp.exp((c - 1.0 - i)[None] * log_g[:, None])[..., None]
    c_dec = jnp.exp(c * log_g)[:, None, None]
    bc = lambda t: jnp.broadcast_to(t, (RET_HEADS, c, HEAD_DIM))
    inv = RET_ROPE_THETA ** (-jnp.arange(0, HEAD_DIM, 2, dtype=F32) / HEAD_DIM)
    ang = jnp.arange(seq)[:, None].astype(F32) * inv
    cos, sin = jnp.cos(ang), jnp.sin(ang)
    return (dmask, bc(q_dec), bc(k_dec), jnp.broadcast_to(c_dec, (RET_HEADS, 1, HEAD_DIM)),
            jnp.concatenate([cos, cos], -1), jnp.concatenate([-sin, sin], -1))


def _retention_kernel(q_ref, k_ref, v_ref, g_ref, cos_ref, sin_ref, dmask_ref, qdec_ref, kdec_ref, cdec_ref, gn_ref,
                      o_ref, st_ref, s_scr):
    c = pl.program_id(1)

    @pl.when(c == 0)
    def _():
        s_scr[...] = jnp.zeros(s_scr.shape, F32)

    cosf, sins = cos_ref[...], sin_ref[...]
    half = HEAD_DIM // 2
    for h in range(RET_HEADS):
        sl = slice(h * HEAD_DIM, (h + 1) * HEAD_DIM)
        qh, kh = q_ref[:, sl], k_ref[:, sl]
        qr = qh * cosf + pltpu.roll(qh, half, 1) * sins
        kr = (kh * cosf + pltpu.roll(kh, half, 1) * sins) * (HEAD_DIM ** -0.5)
        vb = v_ref[:, sl].astype(BF16)
        att = _dot_nt(qr.astype(BF16), kr.astype(BF16)) * dmask_ref[h]
        s_prev = s_scr[h]
        o = (jnp.dot(att.astype(BF16), vb, preferred_element_type=F32)
             + jnp.dot((qr * qdec_ref[h]).astype(BF16), s_prev.astype(BF16), preferred_element_type=F32))
        s_scr[h] = cdec_ref[h] * s_prev + lax.dot_general(
            (kr * kdec_ref[h]).astype(BF16), vb, (((0,), (0,)), ((), ())), preferred_element_type=F32)
        mu = jnp.mean(o, axis=-1, keepdims=True)
        var = jnp.mean(jnp.square(o - mu), axis=-1, keepdims=True)
        on = (o - mu) * lax.rsqrt(var + GN_EPS) * gn_ref[:, sl]
        gg = g_ref[:, sl]
        o_ref[:, sl] = gg * (1.0 / (1.0 + jnp.exp(-gg))) * on

    @pl.when(c == pl.num_programs(1) - 1)
    def _():
        st_ref[0] = s_scr[...]


def _retention_prompt(main, gn_g, batch, seq):
    nc = seq // RET_CHUNK
    dmask, q_dec, k_dec, c_dec, cosf, sins = _retention_tables(seq)
    col = lambda j: pl.BlockSpec((RET_CHUNK, RET_W), lambda b, c: (b * nc + c, j))
    pos_tab = pl.BlockSpec((RET_CHUNK, HEAD_DIM), lambda b, c: (c, 0))
    full = lambda a: pl.BlockSpec(a.shape, lambda b, c: (0,) * a.ndim)
    gn = gn_g.reshape(1, RET_W)
    return pl.pallas_call(
        _retention_kernel,
        out_shape=[jax.ShapeDtypeStruct((batch * seq, RET_W), F32),
                   jax.ShapeDtypeStruct((batch, RET_HEADS, HEAD_DIM, HEAD_DIM), F32)],
        grid=(batch, nc),
        in_specs=[col(0), col(1), col(2), col(3), pos_tab, pos_tab,
                  full(dmask), full(q_dec), full(k_dec), full(c_dec), full(gn)],
        out_specs=[pl.BlockSpec((RET_CHUNK, RET_W), lambda b, c: (b * nc + c, 0)),
                   pl.BlockSpec((1, RET_HEADS, HEAD_DIM, HEAD_DIM), lambda b, c: (b, 0, 0, 0))],
        scratch_shapes=[pltpu.VMEM((RET_HEADS, HEAD_DIM, HEAD_DIM), F32)],
        compiler_params=pltpu.CompilerParams(dimension_semantics=("arbitrary", "arbitrary"),
                                             vmem_limit_bytes=VMEM_LIMIT),
        name="retention_prompt",
    )(main, main, main, main, cosf, sins, dmask, q_dec, k_dec, c_dec, gn)


NSA_TQ = 128
NSA_KEY_CHUNK = 512
NSA_GROUP = NSA_HEADS // NSA_KV_HEADS


def _dot_nt(a, b):
    return lax.dot_general(a, b, (((1,), (1,)), ((), ())), preferred_element_type=F32)


def _masked_softmax(s, m):
    sm = jnp.where(m, s, NEG)
    e = jnp.exp(sm - jnp.max(sm, axis=-1, keepdims=True))
    return jnp.where(m, e / jnp.sum(e, axis=-1, keepdims=True), 0.0)


def _select_mask(imp, pos):
    nb = imp.shape[1]
    blk = lax.broadcasted_iota(jnp.int32, (1, nb), 1)
    cur = jnp.right_shift(pos, int(math.log2(CMP_BLOCK)))
    forced = (blk == 0) | (blk == cur) | (blk == cur - 1)
    score = jnp.where(blk > cur, NEG, jnp.where(forced, -NEG, imp))
    rank = jnp.zeros(score.shape, jnp.int32)
    for i in range(nb):
        si = score[:, i:i + 1]
        ahead = (si > score) | ((si == score) & (blk > i))
        rank = rank + ahead.astype(jnp.int32)
    return (rank < SEL_TOPK) & (score > 0.5 * NEG)


def _nsa_prompt_kernel(q_ref, kc_ref, vc_ref, sk_ref, sv_ref, wk_ref, wv_ref, gl_ref, o_ref):
    tq, g, kc_n = NSA_TQ, NSA_GROUP, NSA_KEY_CHUNK
    qi = pl.program_id(2)
    q4 = q_ref[...] * (HEAD_DIM ** -0.5)
    qs = jnp.concatenate([q4[:, i * HEAD_DIM:(i + 1) * HEAD_DIM] for i in range(g)], axis=0).astype(BF16)
    pos = qi * tq + lax.broadcasted_iota(jnp.int32, (tq, 1), 0)
    pos4 = jnp.concatenate([pos] * g, axis=0)

    nb = kc_ref.shape[1]
    s_c = _dot_nt(qs, kc_ref[0].astype(BF16))
    blk = lax.broadcasted_iota(jnp.int32, (1, nb), 1)
    p_c = _masked_softmax(s_c, (blk + 1) * CMP_BLOCK - 1 <= pos4)
    o_cmp = jnp.dot(p_c.astype(BF16), vc_ref[0].astype(BF16), preferred_element_type=F32)
    imp = p_c[0:tq]
    for i in range(1, g):
        imp = imp + p_c[i * tq:(i + 1) * tq]

    seln = jnp.where(_select_mask(imp, pos), 0.0, NEG).astype(BF16)
    q_aug = jnp.concatenate([qs, jnp.concatenate([seln] * g, axis=0)], axis=1)
    key_idx = lax.broadcasted_iota(jnp.int32, (kc_n, nb), 0)
    key_blk = lax.broadcasted_iota(jnp.int32, (kc_n, nb), 1)
    key_row = lax.broadcasted_iota(jnp.int32, (1, kc_n), 1)

    def scores(k0):
        member = jnp.right_shift(key_idx + k0, int(math.log2(CMP_BLOCK))) == key_blk
        k_aug = jnp.concatenate([sk_ref[pl.ds(k0, kc_n), :].astype(BF16),
                                 jnp.where(member, 1.0, 0.0).astype(BF16)], axis=1)
        return _dot_nt(q_aug, k_aug)

    def values(k0):
        return sv_ref[pl.ds(k0, kc_n), :].astype(BF16)

    assert kc_n % tq == 0
    c_diag = _div(qi, kc_n // tq)
    kd = pl.multiple_of(c_diag * kc_n, kc_n)
    s_d = jnp.where(key_row + kd <= pos4, scores(kd), NEG)
    m_d = jnp.max(s_d, axis=-1, keepdims=True)
    p_d = jnp.exp(s_d - m_d)
    first = (m_d, jnp.sum(p_d, axis=-1, keepdims=True),
             jnp.dot(p_d.astype(BF16), values(kd), preferred_element_type=F32))

    def chunk(c, carry):
        m_i, l_i, acc = carry
        k0 = pl.multiple_of(c * kc_n, kc_n)
        s = scores(k0)
        m_new = jnp.maximum(m_i, jnp.max(s, axis=-1, keepdims=True))
        alpha = jnp.exp(m_i - m_new)
        p = jnp.exp(s - m_new)
        l_new = alpha * l_i + jnp.sum(p, axis=-1, keepdims=True)
        return m_new, l_new, alpha * acc + jnp.dot(p.astype(BF16), values(k0), preferred_element_type=F32)

    _, l_f, acc_f = lax.fori_loop(0, c_diag, chunk, first)
    o_sel = acc_f / l_f

    nwk = WINDOW + tq
    kstart = pl.multiple_of(jnp.maximum(qi * tq - WINDOW, 0), tq)
    kpos = kstart + lax.broadcasted_iota(jnp.int32, (1, nwk), 1)
    band = jnp.where((kpos <= pos) & (pos - kpos < WINDOW), 0.0, NEG)
    s_w = _dot_nt(qs, wk_ref[pl.ds(kstart, nwk), :].astype(BF16)) + jnp.concatenate([band] * g, axis=0)
    e_w = jnp.exp(s_w - jnp.max(s_w, axis=-1, keepdims=True))
    o_win = (jnp.dot(e_w.astype(BF16), wv_ref[pl.ds(kstart, nwk), :].astype(BF16), preferred_element_type=F32)
             / jnp.sum(e_w, axis=-1, keepdims=True))

    gate = 1.0 / (1.0 + jnp.exp(-gl_ref[...]))
    for i in range(g):
        rows = slice(i * tq, (i + 1) * tq)
        o_ref[:, i * HEAD_DIM:(i + 1) * HEAD_DIM] = (gate[:, 3 * i:3 * i + 1] * o_cmp[rows]
                                                     + gate[:, 3 * i + 1:3 * i + 2] * o_sel[rows]
                                                     + gate[:, 3 * i + 2:3 * i + 3] * o_win[rows])


def _nsa_prompt(main, kc, vc, batch, seq):
    nq = seq // NSA_TQ
    gw = NSA_GROUP * HEAD_DIM
    qcol, glcol = _split_col0(4) // gw, GL_COL0 // LANES
    seq_spec = lambda split: pl.BlockSpec((seq, HEAD_DIM),
                                          lambda b, k, i: (b, _split_col0(split) // HEAD_DIM + k))
    cmp_spec = pl.BlockSpec((1, seq // CMP_BLOCK, HEAD_DIM), lambda b, k, i: (b, 0, k))
    return pl.pallas_call(
        _nsa_prompt_kernel,
        out_shape=jax.ShapeDtypeStruct((batch * seq, NSA_W), F32),
        grid=(batch, NSA_KV_HEADS, nq),
        in_specs=[pl.BlockSpec((NSA_TQ, gw), lambda b, k, i: (b * nq + i, qcol + k)), cmp_spec, cmp_spec,
                  seq_spec(7), seq_spec(8), seq_spec(9), seq_spec(10),
                  pl.BlockSpec((NSA_TQ, LANES), lambda b, k, i: (b * nq + i, glcol + k))],
        out_specs=pl.BlockSpec((NSA_TQ, gw), lambda b, k, i: (b * nq + i, k)),
        compiler_params=pltpu.CompilerParams(dimension_semantics=("arbitrary",) * 3,
                                             vmem_limit_bytes=VMEM_LIMIT),
        name="nsa_prompt",
    )(main, kc, vc, main, main, main, main, main)


WO_TM = 256


def _layer_norm_rows(y, g, b):
    mu = jnp.mean(y, axis=-1, keepdims=True)
    var = jnp.mean(jnp.square(y - mu), axis=-1, keepdims=True)
    return (y - mu) * lax.rsqrt(var + LN_EPS) * g + b


EXPERT_LANE0 = N_GROUPS
R_EID, R_RANK, R_GATE = 0, EXPERT_TOPK, 2 * EXPERT_TOPK


def _route(h, wr_ref, br_ref, carry):
    tm = h.shape[0]
    logit = jnp.dot(h.astype(BF16), wr_ref[...], preferred_element_type=F32) + br_ref[...]
    lane = lax.broadcasted_iota(jnp.int32, (tm, LANES), 1)
    lanef = lane.astype(F32)
    first_lane = lambda hit: jnp.min(jnp.where(hit, lanef, float(LANES)), axis=1, keepdims=True)
    is_g = lane < N_GROUPS
    gl = jnp.where(is_g, logit, LOWEST)
    gmx = jnp.max(gl, axis=1, keepdims=True)
    grp = first_lane(gl == gmx)
    p_grp = 1.0 / jnp.sum(jnp.where(is_g, jnp.exp(gl - gmx), 0.0), axis=1, keepdims=True)
    lane_grp = jnp.right_shift(lane - EXPERT_LANE0, int(math.log2(EXPERTS_PER_GROUP)))
    in_grp = lane_grp.astype(F32) == grp
    el = jnp.where(in_grp, logit, LOWEST)
    ee = jnp.where(in_grp, jnp.exp(el - jnp.max(el, axis=1, keepdims=True)), 0.0)
    pe = jnp.where(in_grp, ee / jnp.sum(ee, axis=1, keepdims=True), -1.0)
    p1 = jnp.max(pe, axis=1, keepdims=True)
    l1 = first_lane(pe == p1)
    pe2 = jnp.where(lanef == l1, -1.0, pe)
    p2 = jnp.max(pe2, axis=1, keepdims=True)
    l2 = first_lane(pe2 == p2)
    den = p1 + p2
    o1, o2 = lanef == l1, lanef == l2
    onehot = jnp.where(o1 | o2, 1.0, 0.0)
    r = lax.broadcasted_iota(jnp.int32, (tm, tm), 0)
    c = lax.broadcasted_iota(jnp.int32, (tm, tm), 1)
    earlier = jnp.where(c < r, 1.0, 0.0).astype(BF16)
    prefix = jnp.dot(earlier, onehot.astype(BF16), preferred_element_type=F32) + carry[0:1, :]
    rank1 = jnp.sum(jnp.where(o1, prefix, 0.0), axis=1, keepdims=True)
    rank2 = jnp.sum(jnp.where(o2, prefix, 0.0), axis=1, keepdims=True)
    carry[0:1, :] = carry[0:1, :] + jnp.sum(onehot, axis=0, keepdims=True)
    fields = [l1 - EXPERT_LANE0, l2 - EXPERT_LANE0, rank1, rank2, p_grp * p1 / den, p_grp * p2 / den]
    rec = jnp.zeros((tm, LANES), F32)
    for j, f in enumerate(fields):
        rec = jnp.where(lane == j, f, rec)
    return rec


def _wo_ln_route_kernel(fr_ref, fn_ref, x_ref, w_ref, g_ref, b_ref, wr_ref, br_ref, base_ref,
                        h_ref, route_ref, cnt_ref, carry):
    i = pl.program_id(0)

    @pl.when(i == 0)
    def _():
        carry[...] = base_ref[...]

    y = (DEEPNORM_ALPHA * x_ref[...]
         + jnp.dot(fr_ref[...].astype(BF16), w_ref[0:RET_W, :], preferred_element_type=F32)
         + jnp.dot(fn_ref[...].astype(BF16), w_ref[RET_W:MIX_W, :], preferred_element_type=F32))
    h = _layer_norm_rows(y, g_ref[...], b_ref[...])
    h_ref[...] = h
    route_ref[...] = _route(h, wr_ref, br_ref, carry)

    @pl.when(i == pl.num_programs(0) - 1)
    def _():
        cnt_ref[...] = carry[...]


def _wo_ln_route(f_ret, f_nsa, x, w_o_bf16, ln_g, ln_b, w_route, b_route, base_counts):
    t, d = x.shape
    tm = min(WO_TM, t)
    assert t % tm == 0
    row = lambda w: pl.BlockSpec((tm, w), lambda i: (i, 0))
    full = lambda a: pl.BlockSpec(a.shape, lambda i: (0,) * a.ndim)
    lg, lb = ln_g.reshape(1, d), ln_b.reshape(1, d)
    return pl.pallas_call(
        _wo_ln_route_kernel,
        out_shape=[jax.ShapeDtypeStruct((t, d), F32), jax.ShapeDtypeStruct((t, LANES), F32),
                   jax.ShapeDtypeStruct((SUBLANES, LANES), F32)],
        grid=(t // tm,),
        in_specs=[row(RET_W), row(NSA_W), row(d), full(w_o_bf16), full(lg), full(lb),
                  full(w_route), full(b_route), full(base_counts)],
        out_specs=[row(d), row(LANES), pl.BlockSpec((SUBLANES, LANES), lambda i: (0, 0))],
        scratch_shapes=[pltpu.VMEM((SUBLANES, LANES), F32)],
        compiler_params=pltpu.CompilerParams(dimension_semantics=("arbitrary",), vmem_limit_bytes=VMEM_LIMIT),
        name="wo_ln1_route",
    )(f_ret, f_nsa, x, w_o_bf16, lg, lb, w_route, b_route, base_counts)


MOE_BM = 256
MOE_SUB = 64
TABLE_UNROLL = 8


def _expert_kernel(blk_e_ref, n_used_ref, slot_ref, h_ref, wg_ref, wu_ref, wd_ref, y_ref,
                   src_tok, dst_row, xbuf, obuf, gsem, ssem, *, plane, n_rows):
    del blk_e_ref
    bm = xbuf.shape[1] * xbuf.shape[2]
    i = pl.program_id(0)
    n_used = n_used_ref[0]
    n_asg = slot_ref.shape[0]
    dump0 = EXPERT_TOPK * plane
    assert bm & (bm - 1) == 0

    n_sub, sub = xbuf.shape[1], xbuf.shape[2]

    def gather(blk, buf_slot):
        def sub_block(j, carry):
            base = blk * bm + j * sub
            for u in range(sub):
                pltpu.make_async_copy(h_ref.at[pl.ds(src_tok[base + u], 1), :],
                                      xbuf.at[buf_slot, j, pl.ds(u, 1), :], gsem.at[buf_slot]).start()
            return carry
        lax.fori_loop(0, n_sub, sub_block, 0)

    def scatter(blk, buf_slot):
        def sub_block(j, carry):
            base = blk * bm + j * sub
            for u in range(sub):
                pltpu.make_async_copy(obuf.at[buf_slot, j, pl.ds(u, 1), :],
                                      y_ref.at[pl.ds(dst_row[base + u], 1), :], ssem.at[buf_slot]).start()
            return carry
        lax.fori_loop(0, n_sub, sub_block, 0)

    def wait_block(buf, sem, buf_slot):
        pltpu.make_async_copy(buf.at[buf_slot], buf.at[buf_slot], sem.at[buf_slot]).wait()

    @pl.when(i == 0)
    def _():
        def clear(t, carry):
            for u in range(TABLE_UNROLL):
                r = t * TABLE_UNROLL + u
                src_tok[r] = 0
                dst_row[r] = dump0 + jnp.bitwise_and(r, 2 * bm - 1)
            return carry
        lax.fori_loop(0, n_used_ref[1], clear, 0)

        def fill(t, carry):
            for u in range(TABLE_UNROLL):
                a = t * TABLE_UNROLL + u
                tok = jnp.right_shift(a, 1)
                src_tok[slot_ref[a]] = tok
                dst_row[slot_ref[a]] = jnp.bitwise_and(a, 1) * plane + tok
            return carry
        lax.fori_loop(0, n_used_ref[2], fill, 0)
        gather(0, 0)
        n_tok = h_ref.shape[0]
        tail = plane - n_tok
        assert 0 <= tail <= bm
        obuf[1] = jnp.zeros(obuf.shape[1:], F32)
        spans = [(dump0, 2 * bm)] + ([(k * plane + n_tok, tail) for k in range(EXPERT_TOPK)] if tail else [])
        copies = []
        for first, count in spans:
            for j in range(-(-count // sub)):
                rows = min(sub, count - j * sub)
                copies.append(pltpu.make_async_copy(obuf.at[1, j % n_sub, pl.ds(0, rows), :],
                                                    y_ref.at[pl.ds(first + j * sub, rows), :], ssem.at[1]))
        for cp in copies:
            cp.start()
        for cp in copies:
            cp.wait()

    slot = jnp.bitwise_and(i, 1)

    @pl.when(i + 1 < n_used)
    def _():
        gather(i + 1, 1 - slot)

    @pl.when(i < n_used)
    def _():
        wait_block(xbuf, gsem, slot)

        @pl.when(i >= 2)
        def _():
            wait_block(obuf, ssem, slot)

        xb = xbuf[slot].reshape(bm, xbuf.shape[3]).astype(BF16)
        hg = jnp.dot(xb, wg_ref[0].astype(BF16), preferred_element_type=F32)
        hu = jnp.dot(xb, wu_ref[0].astype(BF16), preferred_element_type=F32)
        hb = hg * (1.0 / (1.0 + jnp.exp(-hg))) * hu
        yb = jnp.dot(hb.astype(BF16), wd_ref[0].astype(BF16), preferred_element_type=F32)
        obuf[slot] = yb.reshape(obuf.shape[1:])
        scatter(i, slot)

    @pl.when(i == pl.num_programs(0) - 1)
    def _():
        @pl.when(n_used >= 2)
        def _():
            wait_block(obuf, ssem, jnp.bitwise_and(n_used, 1))
        wait_block(obuf, ssem, jnp.bitwise_and(n_used - 1, 1))


def _expert_ffn(h, slot, blk_e, n_used, w_gate, w_up, w_down, plane):
    t, d = h.shape
    n_asg = slot.shape[0]
    assert n_asg == t * EXPERT_TOPK and EXPERT_TOPK == 2
    n_blk = -(-(n_asg + N_EXPERTS * (MOE_BM - 1)) // MOE_BM)
    de = w_gate.shape[2]
    assert n_asg % TABLE_UNROLL == 0 and MOE_BM % TABLE_UNROLL == 0
    n_used = jnp.concatenate([n_used, jnp.array([n_blk * MOE_BM // TABLE_UNROLL, n_asg // TABLE_UNROLL], jnp.int32)])
    wspec = lambda shape: pl.BlockSpec((1,) + shape, lambda i, be, nu, sl: (be[i], 0, 0))
    return pl.pallas_call(
        functools.partial(_expert_kernel, plane=plane, n_rows=n_blk * MOE_BM),
        out_shape=jax.ShapeDtypeStruct((EXPERT_TOPK * plane + 2 * MOE_BM, d), F32),
        grid_spec=pltpu.PrefetchScalarGridSpec(
            num_scalar_prefetch=3,
            grid=(n_blk,),
            in_specs=[pl.BlockSpec(memory_space=pl.ANY), wspec((d, de)), wspec((d, de)), wspec((de, d))],
            out_specs=pl.BlockSpec(memory_space=pl.ANY),
            scratch_shapes=[pltpu.SMEM((n_blk * MOE_BM,), jnp.int32), pltpu.SMEM((n_blk * MOE_BM,), jnp.int32),
                            pltpu.VMEM((2, MOE_BM // MOE_SUB, MOE_SUB, d), F32),
                            pltpu.VMEM((2, MOE_BM // MOE_SUB, MOE_SUB, d), F32),
                            pltpu.SemaphoreType.DMA((2,)), pltpu.SemaphoreType.DMA((2,))]),
        compiler_params=pltpu.CompilerParams(dimension_semantics=("arbitrary",), vmem_limit_bytes=VMEM_LIMIT),
        name="expert_ffn",
    )(blk_e, n_used, slot, h, w_gate, w_up, w_down)


def _moe_ln_kernel(h_ref, y0_ref, y1_ref, route_ref, g_ref, b_ref, o_ref):
    rec = route_ref[...]
    y = (DEEPNORM_ALPHA * h_ref[...] + rec[:, R_GATE:R_GATE + 1] * y0_ref[...]
         + rec[:, R_GATE + 1:R_GATE + 2] * y1_ref[...])
    o_ref[...] = _layer_norm_rows(y, g_ref[...], b_ref[...])


def _moe_ln(h, y, route, ln_g, ln_b, row0, n_rows, plane):
    d = h.shape[1]
    tm = min(WO_TM, n_rows)
    assert n_rows % tm == 0 and row0 % tm == 0 and plane % tm == 0 and EXPERT_TOPK == 2
    off = row0 // tm
    row = lambda w, o: pl.BlockSpec((tm, w), lambda i: (i + o, 0))
    vec = pl.BlockSpec((1, d), lambda i: (0, 0))
    return pl.pallas_call(
        _moe_ln_kernel,
        out_shape=jax.ShapeDtypeStruct((n_rows, d), F32),
        grid=(n_rows // tm,),
        in_specs=[row(d, off), row(d, off), row(d, off + plane // tm), row(LANES, off), vec, vec],
        out_specs=row(d, 0),
        compiler_params=pltpu.CompilerParams(dimension_semantics=("arbitrary",), vmem_limit_bytes=VMEM_LIMIT),
        name="moe_ln2",
    )(h, y, y, route, ln_g.reshape(1, d), ln_b.reshape(1, d))


def _layer_norm(x, g, b):
    xf = x.astype(F32)
    mu = xf.mean(-1, keepdims=True)
    var = jnp.square(xf - mu).mean(-1, keepdims=True)
    return ((xf - mu) * lax.rsqrt(var + LN_EPS) * g + b).astype(x.dtype)


def _rope(x, pos, rot_dim, theta):
    half = rot_dim // 2
    inv = theta ** (-jnp.arange(0, rot_dim, 2, dtype=F32) / rot_dim)
    ang = pos[..., None].astype(F32) * inv
    cos = jnp.cos(ang)[:, :, None, :]
    sin = jnp.sin(ang)[:, :, None, :]
    xr = x[..., :rot_dim].astype(F32)
    x1, x2 = xr[..., :half], xr[..., half:]
    rot = jnp.concatenate([x1 * cos - x2 * sin, x2 * cos + x1 * sin], -1).astype(x.dtype)
    return jnp.concatenate([rot, x[..., rot_dim:]], -1)


def _heads(t, n):
    return t.reshape(t.shape[0], t.shape[1], n, HEAD_DIM)


def _chunk_retention(q, k, v, s0):
    n, l, h, d = q.shape
    c = RET_CHUNK if l % RET_CHUNK == 0 else l
    nc = l // c
    log_g = jnp.log1p(-jnp.exp2(-5.0 - jnp.arange(h, dtype=F32)))
    i = jnp.arange(c, dtype=F32)
    rel = i[:, None] - i[None, :]
    dmask = jnp.where(rel[None] >= 0, jnp.exp(jnp.maximum(rel[None], 0.0) * log_g[:, None, None]), 0.0)
    q_dec = jnp.exp((i + 1.0)[None] * log_g[:, None])[..., None]
    k_dec = jnp.exp((c - 1.0 - i)[None] * log_g[:, None])[..., None]
    c_dec = jnp.exp(c * log_g)[:, None, None]

    def to_chunks(t):
        return t.astype(F32).reshape(n, nc, c, h, d).transpose(1, 0, 3, 2, 4)

    def step(s, qkv):
        qc, kc, vc = qkv
        att = jnp.einsum('bhid,bhjd->bhij', qc, kc) * dmask
        o = jnp.einsum('bhij,bhjd->bhid', att, vc) + jnp.einsum('bhid,bhde->bhie', qc * q_dec, s)
        s = c_dec * s + jnp.einsum('bhjd,bhje->bhde', kc * k_dec, vc)
        return s, o

    s, o = lax.scan(step, s0.astype(F32), (to_chunks(q), to_chunks(k), to_chunks(v)))
    return o.transpose(1, 0, 3, 2, 4).reshape(n, l, h, d), s


def _retention_group(rq, rk, rv, rg, pos, s0, gn_g):
    q = _rope(_heads(rq, RET_HEADS), pos, HEAD_DIM, RET_ROPE_THETA)
    k = _rope(_heads(rk, RET_HEADS), pos, HEAD_DIM, RET_ROPE_THETA) * (HEAD_DIM ** -0.5)
    v = _heads(rv, RET_HEADS)
    o, s = _chunk_retention(q, k, v, s0)
    mu = o.mean(-1, keepdims=True)
    var = jnp.square(o - mu).mean(-1, keepdims=True)
    on = (o - mu) * lax.rsqrt(var + GN_EPS) * gn_g.reshape(RET_HEADS, HEAD_DIM).astype(F32)
    out = jax.nn.silu(rg.astype(F32)) * on.reshape(rg.shape)
    return out.astype(rq.dtype), s


def _gqa_attend(q, k, v, mask):
    n, lq, h, d = q.shape
    kv = k.shape[2]
    qg = q.reshape(n, lq, kv, h // kv, d)
    s = jnp.einsum('nqkgd,nskd->nkgqs', qg, k).astype(F32) * (d ** -0.5)
    m = mask[:, None, None]
    p = jax.nn.softmax(jnp.where(m, s, NEG), axis=-1) * m
    o = jnp.einsum('nkgqs,nskd->nqkgd', p.astype(v.dtype), v)
    return o.reshape(n, lq, h, d), p


def _nsa_heads(nq, ck, sk, wk, cv, sv, wv, pos):
    rp = lambda t, nh: _rope(_heads(t, nh), pos, ROT_DIM, ROPE_THETA)
    return (rp(nq, NSA_HEADS), rp(ck, NSA_KV_HEADS), rp(sk, NSA_KV_HEADS), rp(wk, NSA_KV_HEADS),
            _heads(cv, NSA_KV_HEADS), _heads(sv, NSA_KV_HEADS), _heads(wv, NSA_KV_HEADS))


def _compress(rows, w):
    n, t, kv, d = rows.shape
    return jnp.einsum('nbjkd,jd->nbkd', rows.reshape(n, t // CMP_BLOCK, CMP_BLOCK, kv, d), w)


def _cmp_branch(q, pos, kc, vc):
    nb = kc.shape[1]
    blk_end = (jnp.arange(nb) + 1) * CMP_BLOCK - 1
    mask = blk_end[None, None, :] <= pos[:, :, None]
    o, p = _gqa_attend(q, kc, vc, mask)
    imp = p.sum(axis=2).transpose(0, 2, 1, 3)
    return o, imp


def _select_blocks(imp, pos, n_sel):
    nb = imp.shape[-1]
    imp = jnp.pad(imp, ((0, 0), (0, 0), (0, 0), (0, n_sel - nb)))
    blk = jnp.arange(n_sel)
    cur = (pos // CMP_BLOCK)[:, :, None, None]
    forced = (blk == 0) | (blk == cur) | (blk == cur - 1)
    score = jnp.where(blk > cur, NEG, jnp.where(forced, -NEG, imp))
    top, idx = lax.top_k(score, min(SEL_TOPK, n_sel))
    return idx, top > 0.5 * NEG


def _sel_attend(q, pos, ks, vs, idx, valid):
    n, lq, kv, kk, cb, d = ks.shape
    h = q.shape[2]
    kpos = idx[..., None] * CMP_BLOCK + jnp.arange(CMP_BLOCK)
    m = ((kpos <= pos[:, :, None, None, None]) & valid[..., None]).reshape(n, lq, kv, 1, kk * cb)
    qg = q.reshape(n, lq, kv, h // kv, d)
    kf = ks.reshape(n, lq, kv, kk * cb, d)
    vf = vs.reshape(n, lq, kv, kk * cb, d)
    s = jnp.einsum('nqkgd,nqkjd->nqkgj', qg, kf).astype(F32) * (d ** -0.5)
    p = jax.nn.softmax(jnp.where(m, s, NEG), axis=-1) * m
    o = jnp.einsum('nqkgj,nqkjd->nqkgd', p.astype(vf.dtype), vf)
    return o.reshape(n, lq, h, d)


def _sel_prompt(q, pos, k, v, idx, valid):
    b, s, h, d = q.shape
    kv = k.shape[2]
    nb = s // CMP_BLOCK
    nq = s // SEL_Q_BLOCK
    kb = k.reshape(b, nb, CMP_BLOCK, kv, d).transpose(0, 3, 1, 2, 4)
    vb = v.reshape(b, nb, CMP_BLOCK, kv, d).transpose(0, 3, 1, 2, 4)
    bi = jnp.arange(b)[:, None, None, None]
    hi = jnp.arange(kv)[None, None, :, None]

    def blockwise(t):
        return t.reshape(t.shape[0], nq, SEL_Q_BLOCK, *t.shape[2:]).swapaxes(0, 1)

    def one(args):
        qc, pc, ic, vc = args
        return _sel_attend(qc, pc, kb[bi, hi, ic], vb[bi, hi, ic], ic, vc)

    o = lax.map(one, (blockwise(q), blockwise(pos), blockwise(idx), blockwise(valid)))
    return o.swapaxes(0, 1).reshape(b, s, h, d)


def _win_prompt(q, k, v):
    b, s, h, d = q.shape
    kv = k.shape[2]
    nb = s // WIN_Q_BLOCK
    nprev = WINDOW // WIN_Q_BLOCK
    nw = nprev + 1
    padw = ((0, 0), (WINDOW, 0), (0, 0), (0, 0))
    kp = jnp.pad(k, padw).reshape(b, nb + nprev, WIN_Q_BLOCK, kv, d)
    vp = jnp.pad(v, padw).reshape(b, nb + nprev, WIN_Q_BLOCK, kv, d)
    kw = jnp.concatenate([kp[:, i:i + nb] for i in range(nw)], axis=2)
    vw = jnp.concatenate([vp[:, i:i + nb] for i in range(nw)], axis=2)
    qpos = jnp.arange(s).reshape(nb, WIN_Q_BLOCK)
    kpos = (jnp.arange(nb)[:, None] - nprev) * WIN_Q_BLOCK + jnp.arange(nw * WIN_Q_BLOCK)[None]
    qq, kk = qpos[:, :, None], kpos[:, None, :]
    mask = (kk <= qq) & (qq - kk < WINDOW) & (kk >= 0)
    mask = jnp.broadcast_to(mask[None], (b,) + mask.shape).reshape(b * nb, WIN_Q_BLOCK, nw * WIN_Q_BLOCK)
    o, _ = _gqa_attend(q.reshape(b * nb, WIN_Q_BLOCK, h, d), kw.reshape(b * nb, nw * WIN_Q_BLOCK, kv, d),
                       vw.reshape(b * nb, nw * WIN_Q_BLOCK, kv, d), mask)
    return o.reshape(b, s, h, d)


def _gather_selected(pool, new_rows, page_table, idx):
    n, l, kv, d = new_rows.shape
    n_pages = page_table.shape[1]
    past_blocks = n_pages * PAGE_SIZE // CMP_BLOCK
    nbn = -(-l // CMP_BLOCK)
    newb = jnp.pad(new_rows, ((0, 0), (0, nbn * CMP_BLOCK - l), (0, 0), (0, 0)))
    newb = newb.reshape(n, nbn, CMP_BLOCK, kv, d).transpose(0, 3, 1, 2, 4)
    bi = jnp.arange(n)[:, None, None, None]
    hi = jnp.arange(kv)[None, None, :, None]
    start = idx * CMP_BLOCK
    phys = page_table[bi, jnp.minimum(start // PAGE_SIZE, n_pages - 1)]
    off = (start % PAGE_SIZE)[..., None] + jnp.arange(CMP_BLOCK)
    past = pool[phys[..., None], off, hi[..., None]]
    new = newb[bi, hi, jnp.clip(idx - past_blocks, 0, nbn - 1)]
    return jnp.where((idx < past_blocks)[..., None, None], past, new)


def _nsa_combine(gl, o_cmp, o_sel, o_win):
    n, l = gl.shape[0], gl.shape[1]
    g = jax.nn.sigmoid(gl.astype(F32)).reshape(n, l, NSA_HEADS, 3, 1)
    o = g[..., 0, :] * o_cmp + g[..., 1, :] * o_sel + g[..., 2, :] * o_win
    return o.reshape(n, l, NSA_W).astype(o_cmp.dtype)


def _prompt_mixer(x, win_buf, w_in, w_cmp_k, w_cmp_v, gn_g):
    n, s, _ = x.shape
    main = _project(x, w_in, jnp.arange(s))
    ret_out, s_fin = _retention_prompt(main, gn_g, n, s)
    kc, vc = _compress_prompt(main, w_cmp_k, w_cmp_v)
    nsa = _nsa_prompt(main, kc.reshape(n, s // CMP_BLOCK, KV_W), vc.reshape(n, s // CMP_BLOCK, KV_W), n, s)
    ck, cv, sk, sv, wk, wv = [_heads(t, NSA_KV_HEADS) for t in _split_main(main, n, s, first=5)]
    feats = (ret_out, nsa)
    if s >= win_buf:
        bk, bv = wk[:, s - win_buf:], wv[:, s - win_buf:]
    else:
        padb = ((0, 0), (win_buf - s, 0), (0, 0), (0, 0))
        bk, bv = jnp.pad(wk, padb), jnp.pad(wv, padb)
    return feats, (ck, cv, sk, sv, bk, bv, s_fin)


def _sample_mixer(x, c_cmp_k, c_cmp_v, c_sel_k, c_sel_v, c_win_k, c_win_v, s_ret, page_table,
                  w_in, w_cmp_k, w_cmp_v, gn_g):
    n, l, _ = x.shape
    past = page_table.shape[1] * PAGE_SIZE
    pos = past + jnp.arange(l)[None]
    assert l == 1
    main = _project(x, w_in, jnp.full((n,), past, jnp.int32))
    rq, rk, rv, rg, nq, ck, cv, sk, sv, wk, wv = _split_main(main, n, l)
    gl = _gate_logits(main, n, l)
    ret_out, s_new = _retention_group(rq, rk, rv, rg, pos, s_ret, gn_g)
    q = _heads(nq, NSA_HEADS)
    ck, cv, sk, sv, wk, wv = [_heads(t, NSA_KV_HEADS) for t in (ck, cv, sk, sv, wk, wv)]
    assert l == 1 and past % CMP_BLOCK == 0 and c_win_k.shape[1] <= WINDOW
    kc, vc = _compress_paged(c_cmp_k, c_cmp_v, page_table, w_cmp_k, w_cmp_v)
    o_cmp, sel = _sample_select(q[:, 0], kc, vc, past)
    sel = sel[:, :SEL_TOPK].reshape(-1)
    gl_pad = jnp.pad(gl.reshape(n, NSA_HEADS, 3), ((0, 0), (0, 0), (0, LANES - 3)))
    nsa = _sample_attend(sel, page_table, q[:, 0], sk[:, 0], sv[:, 0], wk[:, 0], wv[:, 0],
                         _interleaved(c_win_k), _interleaved(c_win_v), _interleaved(c_sel_k), _interleaved(c_sel_v),
                         o_cmp, gl_pad, past)
    feats = (ret_out.reshape(n * l, RET_W), nsa.reshape(n * l, NSA_W))
    kw = jnp.concatenate([c_win_k, wk], 1)
    vw = jnp.concatenate([c_win_v, wv], 1)
    return feats, (ck, cv, sk, sv, kw[:, l:], vw[:, l:], s_new)


def _route_params(w_group, b_group, w_expert, b_expert):
    w = jnp.concatenate([w_group, w_expert], axis=1)
    b = jnp.concatenate([b_group, b_expert], axis=0)
    pad = LANES - w.shape[1]
    return jnp.pad(w, ((0, 0), (0, pad))).astype(BF16), jnp.pad(b, (0, pad)).reshape(1, LANES)


def _dispatch_plan(route, counts_tile):
    counts = counts_tile[0, EXPERT_LANE0:EXPERT_LANE0 + N_EXPERTS].astype(jnp.int32)
    padded = (counts + MOE_BM - 1) // MOE_BM * MOE_BM
    pad_end = jnp.cumsum(padded)
    pad_start = pad_end - padded
    n_asg = route.shape[0] * EXPERT_TOPK
    n_blk = -(-(n_asg + N_EXPERTS * (MOE_BM - 1)) // MOE_BM)
    blk_first = jnp.arange(n_blk, dtype=jnp.int32) * MOE_BM
    blk_e = jnp.minimum(jnp.sum(pad_end[None, :] <= blk_first[:, None], axis=1), N_EXPERTS - 1).astype(jnp.int32)
    n_used = (pad_end[-1:] // MOE_BM).astype(jnp.int32)
    eid = route[:, R_EID:R_EID + EXPERT_TOPK].astype(jnp.int32)
    rank = route[:, R_RANK:R_RANK + EXPERT_TOPK].astype(jnp.int32)
    start = jnp.sum(jnp.where(eid[..., None] == jnp.arange(N_EXPERTS), pad_start, 0), axis=-1)
    return (start + rank).reshape(-1).astype(jnp.int32), blk_e, n_used


def kernel(x_prompt, x_sample, cache_cmp_k, cache_cmp_v, cache_sel_k, cache_sel_v, cache_win_k, cache_win_v,
           state_ret, page_table, w_in, w_cmp_k, w_cmp_v, ret_gn_g, w_o, ln1_g, ln1_b, w_group, b_group,
           w_expert, b_expert, w_gate, w_up, w_down, ln2_g, ln2_b):
    win_buf = cache_win_k.shape[2]
    hp, hs = x_prompt, x_sample
    acc_p = [[] for _ in range(7)]
    acc_s = [[] for _ in range(7)]
    for l in range(DEPTH):
        fp, st_p = _prompt_mixer(hp, win_buf, w_in[l], w_cmp_k[l], w_cmp_v[l], ret_gn_g[l])
        fs, st_s = _sample_mixer(hs, cache_cmp_k[l], cache_cmp_v[l], cache_sel_k[l], cache_sel_v[l],
                                 cache_win_k[l], cache_win_v[l], state_ret[l], page_table,
                                 w_in[l], w_cmp_k[l], w_cmp_v[l], ret_gn_g[l])
        w_o_bf16 = w_o[l].astype(BF16)
        w_route, b_route = _route_params(w_group[l], b_group[l], w_expert[l], b_expert[l])
        tp = hp.shape[0] * hp.shape[1]
        ts = hs.shape[0] * hs.shape[1]
        no_counts = jnp.zeros((SUBLANES, LANES), F32)
        h1p, route_p, counts_p = _wo_ln_route(fp[0], fp[1], hp.reshape(tp, D_MODEL), w_o_bf16, ln1_g[l], ln1_b[l],
                                              w_route, b_route, no_counts)
        h1s, route_s, counts = _wo_ln_route(fs[0], fs[1], hs.reshape(ts, D_MODEL), w_o_bf16, ln1_g[l], ln1_b[l],
                                            w_route, b_route, counts_p)
        h1 = jnp.concatenate([h1p, h1s], axis=0)
        route = jnp.concatenate([route_p, route_s], axis=0)
        slot, blk_e, n_used = _dispatch_plan(route, counts)
        plane = -(-(tp + ts) // WO_TM) * WO_TM
        y = _expert_ffn(h1, slot, blk_e, n_used, w_gate[l], w_up[l], w_down[l], plane)
        hp = _moe_ln(h1, y, route, ln2_g[l], ln2_b[l], 0, tp, plane).reshape(hp.shape)
        hs = _moe_ln(h1, y, route, ln2_g[l], ln2_b[l], tp, ts, plane).reshape(hs.shape)
        for acc, t in zip(acc_p, st_p):
            acc.append(t)
        for acc, t in zip(acc_s, st_s):
            acc.append(t)
    p_cmp_k, p_cmp_v, p_sel_k, p_sel_v, p_win_k, p_win_v, p_ret = [jnp.stack(a) for a in acc_p]
    s_cmp_k, s_cmp_v, s_sel_k, s_sel_v, s_win_k, s_win_v, s_ret = [jnp.stack(a) for a in acc_s]
    return (hp, hs, p_cmp_k, p_cmp_v, p_sel_k, p_sel_v, p_win_k, p_win_v, p_ret.astype(state_ret.dtype),
            s_cmp_k, s_cmp_v, s_sel_k, s_sel_v, s_win_k, s_win_v, s_ret.astype(state_ret.dtype))
```

```python
import functools
import math

import jax
import jax.numpy as jnp
import numpy as np
from jax import lax
from jax.experimental import pallas as pl
from jax.experimental.pallas import tpu as pltpu

D_MODEL = 2048
DEPTH = 1
PAGE_SIZE = 128

F32 = jnp.float32
BF16 = jnp.bfloat16
HEAD_DIM = 128
RET_HEADS = D_MODEL // (2 * HEAD_DIM)
NSA_HEADS = D_MODEL // (2 * HEAD_DIM)
NSA_KV_HEADS = 2
RET_W = RET_HEADS * HEAD_DIM
NSA_W = NSA_HEADS * HEAD_DIM
KV_W = NSA_KV_HEADS * HEAD_DIM
MIX_W = RET_W + NSA_W
RET_CHUNK = 128
RET_ROPE_THETA = 10000.0
ROPE_THETA = 500000.0
ROT_DIM = HEAD_DIM // 4
CMP_BLOCK = 64
SEL_TOPK = 16
WINDOW = 512
WIN_Q_BLOCK = 128
SEL_Q_BLOCK = 64
N_GROUPS = 4
EXPERTS_PER_GROUP = 8
N_EXPERTS = N_GROUPS * EXPERTS_PER_GROUP
EXPERT_TOPK = 2
D_EXPERT = 512
MOE_BLOCK = 128
LN_EPS = 1e-5
GN_EPS = 1e-5
NEG = -1e30
DEEPNORM_ALPHA = (2 * DEPTH) ** 0.25
DEEPNORM_BETA = (8 * DEPTH) ** -0.25
SPLITS = (RET_W, RET_W, RET_W, RET_W, NSA_W, KV_W, KV_W, KV_W, KV_W, KV_W, KV_W, NSA_HEADS * 3)
IN_W = sum(SPLITS)
GATE_W = NSA_HEADS * 3
MAIN_W = IN_W - GATE_W
LANES = 128
VMEM_LIMIT = 48 * 1024 * 1024


PROJ_TN = 512
GL_COL0 = MAIN_W
PROJ_W = -(-(MAIN_W + NSA_KV_HEADS * LANES) // PROJ_TN) * PROJ_TN
ROTATED_SPLITS = (4, 5, 7, 9)


def _rotated_heads():
    cuts = np.cumsum((0,) + SPLITS)
    tiles = []
    for j in range(PROJ_W // PROJ_TN):
        heads = []
        for h in range(PROJ_TN // HEAD_DIM):
            c0 = j * PROJ_TN + h * HEAD_DIM
            split = int(np.searchsorted(cuts, c0, side="right")) - 1
            heads.append(split in ROTATED_SPLITS and c0 < MAIN_W)
        tiles.append(tuple(heads))
    return tiles


def _nsa_rope_tables(pos):
    half = ROT_DIM // 2
    inv = ROPE_THETA ** (-jnp.arange(0, ROT_DIM, 2, dtype=F32) / ROT_DIM)
    ang = pos[:, None].astype(F32) * inv
    cos, sin = jnp.cos(ang), jnp.sin(ang)
    rest = HEAD_DIM - ROT_DIM
    zeros = jnp.zeros((pos.shape[0], half), F32)
    pad = lambda t, fill: jnp.pad(t, ((0, 0), (0, rest)), constant_values=fill)
    return (pad(jnp.concatenate([cos, cos], 1), 1.0), pad(jnp.concatenate([zeros, sin], 1), 0.0),
            pad(jnp.concatenate([-sin, zeros], 1), 0.0))


def _proj_kernel(x_ref, w_ref, cos_ref, up_ref, dn_ref, o_ref, *, patterns):
    j = pl.program_id(1)
    acc = jnp.dot(x_ref[...].astype(BF16), w_ref[...], preferred_element_type=F32)
    half = ROT_DIM // 2

    def rotated(heads):
        cos, up, dn = cos_ref[...], up_ref[...], dn_ref[...]
        parts = []
        for h, rot in enumerate(heads):
            xh = acc[:, h * HEAD_DIM:(h + 1) * HEAD_DIM]
            if rot:
                xh = xh * cos + pltpu.roll(xh, half, 1) * up + pltpu.roll(xh, HEAD_DIM - half, 1) * dn
            parts.append(xh)
        return jnp.concatenate(parts, axis=1)

    for heads in sorted(set(patterns)):
        tiles = [t for t, p in enumerate(patterns) if p == heads]
        hit = functools.reduce(jnp.logical_or, [j == t for t in tiles])

        @pl.when(hit)
        def _(heads=heads):
            o_ref[...] = rotated(heads) if any(heads) else acc


def _project(x, w_in, pos):
    n, l, d = x.shape
    t = n * l
    xt = x.reshape(t, d)
    tm = min(1024, t)
    gpg = GATE_W // NSA_KV_HEADS
    gate_tiles = [jnp.pad(w_in[:, MAIN_W + k * gpg:MAIN_W + (k + 1) * gpg], ((0, 0), (0, LANES - gpg)))
                  for k in range(NSA_KV_HEADS)]
    fill = jnp.zeros((d, PROJ_W - MAIN_W - NSA_KV_HEADS * LANES), F32)
    w = jnp.concatenate([w_in[:, :MAIN_W]] + gate_tiles + [fill], axis=1).astype(BF16)
    tables = _nsa_rope_tables(pos)
    pb = pos.shape[0] // tm
    tab = pl.BlockSpec((tm, HEAD_DIM), lambda i, j: (i % pb, 0))
    return pl.pallas_call(
        functools.partial(_proj_kernel, patterns=_rotated_heads()),
        out_shape=jax.ShapeDtypeStruct((t, PROJ_W), F32),
        grid=(t // tm, PROJ_W // PROJ_TN),
        in_specs=[pl.BlockSpec((tm, d), lambda i, j: (i, 0)), pl.BlockSpec((d, PROJ_TN), lambda i, j: (0, j)),
                  tab, tab, tab],
        out_specs=pl.BlockSpec((tm, PROJ_TN), lambda i, j: (i, j)),
        compiler_params=pltpu.CompilerParams(dimension_semantics=("arbitrary", "arbitrary"),
                                             vmem_limit_bytes=VMEM_LIMIT),
        name="in_proj",
    )(xt, w, *tables)


def _split_main(main, n, l, first=0):
    cuts = [0] + [int(c) for c in np.cumsum(SPLITS)[:-1]]
    return [main[:, cuts[i]:cuts[i + 1]].reshape(n, l, -1) for i in range(first, len(SPLITS) - 1)]


def _split_col0(i):
    return int(np.cumsum((0,) + SPLITS)[i])


def _gate_logits(main, n, l):
    gpg = GATE_W // NSA_KV_HEADS
    cols = [main[:, GL_COL0 + k * LANES:GL_COL0 + k * LANES + gpg] for k in range(NSA_KV_HEADS)]
    return jnp.concatenate(cols, axis=1).reshape(n, l, GATE_W)


def _compress_kernel(k_ref, v_ref, wk_ref, wv_ref, ko_ref, vo_ref):
    r = ko_ref.shape[0]
    ko_ref[...] = jnp.sum(k_ref[...].reshape(r, CMP_BLOCK, KV_W) * wk_ref[...][None], axis=1)
    vo_ref[...] = jnp.sum(v_ref[...].reshape(r, CMP_BLOCK, KV_W) * wv_ref[...][None], axis=1)


def _compress_prompt(main, w_cmp_k, w_cmp_v):
    t = main.shape[0]
    r = 32
    wk2 = jnp.tile(w_cmp_k, (1, NSA_KV_HEADS))
    wv2 = jnp.tile(w_cmp_v, (1, NSA_KV_HEADS))
    kcol, vcol = _split_col0(5) // KV_W, _split_col0(6) // KV_W
    wsp = pl.BlockSpec((CMP_BLOCK, KV_W), lambda i: (0, 0))
    osp = pl.BlockSpec((r, KV_W), lambda i: (i, 0))
    return pl.pallas_call(
        _compress_kernel,
        out_shape=[jax.ShapeDtypeStruct((t // CMP_BLOCK, KV_W), F32)] * 2,
        grid=(t // (r * CMP_BLOCK),),
        in_specs=[pl.BlockSpec((r * CMP_BLOCK, KV_W), lambda i: (i, kcol)),
                  pl.BlockSpec((r * CMP_BLOCK, KV_W), lambda i: (i, vcol)), wsp, wsp],
        out_specs=[osp, osp],
        compiler_params=pltpu.CompilerParams(dimension_semantics=("arbitrary",)),
        name="compress_prompt",
    )(main, main, wk2, wv2)


PAGES_PER_STEP = 16
BLOCKS_PER_PAGE = PAGE_SIZE // CMP_BLOCK


SUBLANES = 8
ROWS_PER_BLOCK = CMP_BLOCK * NSA_KV_HEADS
ROWS_PER_PAGE = PAGE_SIZE * NSA_KV_HEADS
BLOCKS_PER_TILE = SUBLANES // NSA_KV_HEADS


def _div(x, n):
    assert n & (n - 1) == 0
    return jnp.right_shift(x, n.bit_length() - 1)


def _mod(x, n):
    assert n & (n - 1) == 0
    return jnp.bitwise_and(x, n - 1)


def _interleaved(pool):
    return pool.reshape(pool.shape[:-3] + (pool.shape[-3] * NSA_KV_HEADS, HEAD_DIM))


def _compress_paged_kernel(pt_ref, pk_ref, pv_ref, wk_ref, wv_ref, ko_ref, vo_ref, kbuf, vbuf, sem):
    s, i = pl.program_id(0), pl.program_id(1)
    n_i = pl.num_programs(1)
    step = s * n_i + i
    last = pl.num_programs(0) * n_i - 1
    pair = _div(lax.broadcasted_iota(jnp.int32, (SUBLANES, HEAD_DIM), 0), NSA_KV_HEADS)

    def fetch(seq, blk, slot):
        for j in range(PAGES_PER_STEP):
            page = pt_ref[seq, blk * PAGES_PER_STEP + j]
            pltpu.make_async_copy(pk_ref.at[page], kbuf.at[slot, j], sem.at[slot]).start(priority=j % 2)
            pltpu.make_async_copy(pv_ref.at[page], vbuf.at[slot, j], sem.at[slot]).start(priority=(j + 1) % 2)

    @pl.when(step == 0)
    def _():
        fetch(0, 0, 0)

    slot = jnp.bitwise_and(step, 1)

    @pl.when(step < last)
    def _():
        wrap = i == n_i - 1
        fetch(jnp.where(wrap, s + 1, s), jnp.where(wrap, 0, i + 1), 1 - slot)

    pltpu.make_async_copy(kbuf.at[slot], kbuf.at[slot], sem.at[slot]).wait()
    pltpu.make_async_copy(vbuf.at[slot], vbuf.at[slot], sem.at[slot]).wait()

    def summaries(buf, w):
        sums = []
        for j in range(PAGES_PER_STEP):
            for b in range(BLOCKS_PER_PAGE):
                y = buf[slot, j, b * ROWS_PER_BLOCK:(b + 1) * ROWS_PER_BLOCK, :] * w
                acc = jnp.sum(y.reshape(ROWS_PER_BLOCK // SUBLANES, SUBLANES, HEAD_DIM), axis=0)
                shift = SUBLANES // 2
                while shift >= NSA_KV_HEADS:
                    acc = acc + pltpu.roll(acc, shift, 0)
                    shift //= 2
                sums.append(acc)
        tiles = []
        for t in range(len(sums) // BLOCKS_PER_TILE):
            tile = sums[t * BLOCKS_PER_TILE]
            for j in range(1, BLOCKS_PER_TILE):
                tile = jnp.where(pair == j, sums[t * BLOCKS_PER_TILE + j], tile)
            tiles.append(tile)
        return jnp.concatenate(tiles, axis=0)

    ko_ref[0] = summaries(kbuf, wk_ref[...])
    vo_ref[0] = summaries(vbuf, wv_ref[...])


def _compress_paged(pool_k, pool_v, page_table, w_cmp_k, w_cmp_v):
    n, n_pages = page_table.shape
    wk2 = jnp.repeat(w_cmp_k, NSA_KV_HEADS, axis=0)
    wv2 = jnp.repeat(w_cmp_v, NSA_KV_HEADS, axis=0)
    anywhere = pl.BlockSpec(memory_space=pl.ANY)
    wsp = pl.BlockSpec((ROWS_PER_BLOCK, HEAD_DIM), lambda s, i, pt: (0, 0))
    rows = PAGES_PER_STEP * BLOCKS_PER_PAGE * NSA_KV_HEADS
    osp = pl.BlockSpec((1, rows, HEAD_DIM), lambda s, i, pt: (s, i, 0))
    page_buf = pltpu.VMEM((2, PAGES_PER_STEP, ROWS_PER_PAGE, HEAD_DIM), F32)
    return pl.pallas_call(
        _compress_paged_kernel,
        out_shape=[jax.ShapeDtypeStruct((n, n_pages * BLOCKS_PER_PAGE * NSA_KV_HEADS, HEAD_DIM), F32)] * 2,
        grid_spec=pltpu.PrefetchScalarGridSpec(
            num_scalar_prefetch=1,
            grid=(n, n_pages // PAGES_PER_STEP),
            in_specs=[anywhere, anywhere, wsp, wsp],
            out_specs=[osp, osp],
            scratch_shapes=[page_buf, page_buf, pltpu.SemaphoreType.DMA((2,))]),
        compiler_params=pltpu.CompilerParams(dimension_semantics=("arbitrary", "arbitrary")),
        name="compress_paged",
    )(page_table, _interleaved(pool_k), _interleaved(pool_v), wk2, wv2)


SS_SEQ = 8
LOWEST = -3.0e38


def _sample_select_kernel(q_ref, kc_ref, vc_ref, ocmp_ref, sel_ref, *, pos):
    ss, nbk = q_ref.shape[0], kc_ref.shape[1]
    kv, g = NSA_KV_HEADS, NSA_GROUP
    head = lax.broadcasted_iota(jnp.int32, (NSA_HEADS, nbk), 0)
    col = lax.broadcasted_iota(jnp.int32, (NSA_HEADS, nbk), 1)
    m = (_mod(col, kv) == _div(head, g)) & ((_div(col, kv) + 1) * CMP_BLOCK - 1 <= pos)
    width = nbk + LANES
    rows = []
    for i in range(ss):
        q = (q_ref[i] * (HEAD_DIM ** -0.5)).astype(BF16)
        p = _masked_softmax(_dot_nt(q, kc_ref[i].astype(BF16)), m)
        ocmp_ref[i] = jnp.dot(p.astype(BF16), vc_ref[i].astype(BF16), preferred_element_type=F32)
        for k in range(kv):
            imp = jnp.sum(p[k * g:(k + 1) * g], axis=0, keepdims=True)
            rows.append(jnp.concatenate([imp, jnp.zeros((1, LANES), F32)], axis=1))
    nrow = ss * kv
    r_iota = lax.broadcasted_iota(jnp.int32, (nrow, width), 0)
    ccol = lax.broadcasted_iota(jnp.int32, (nrow, width), 1)
    cand = jnp.zeros((nrow, width), F32)
    for r, row in enumerate(rows):
        cand = jnp.where(r_iota == r, row, cand)
    cblk = _div(ccol, kv)
    cur = pos // CMP_BLOCK
    n_sel = -(-(pos + 1) // CMP_BLOCK)
    forced = (cblk == 0) | (cblk == cur) | (cblk == cur - 1)
    score = jnp.where(cblk > cur, NEG, jnp.where(forced, -NEG, cand))
    score = jnp.where((_mod(ccol, kv) == _mod(r_iota, kv)) & (cblk < n_sel), score, LOWEST)
    colf = ccol.astype(F32)
    lane = lax.broadcasted_iota(jnp.int32, (nrow, LANES), 1)
    sel = jnp.zeros((nrow, LANES), jnp.int32)
    for t in range(SEL_TOPK):
        mx = jnp.max(score, axis=1, keepdims=True)
        c = jnp.min(jnp.where(score == mx, colf, -LOWEST), axis=1, keepdims=True)
        picked = jnp.where(mx > 0.5 * NEG, _div(c.astype(jnp.int32), kv), -1)
        sel = jnp.where(lane == t, picked, sel)
        score = jnp.where(colf == c, LOWEST, score)
    sel_ref[...] = sel


def _sample_select(q, kc, vc, pos):
    n = q.shape[0]
    nbk = kc.shape[1]
    qsp = pl.BlockSpec((SS_SEQ, NSA_HEADS, HEAD_DIM), lambda i: (i, 0, 0))
    csp = pl.BlockSpec((SS_SEQ, nbk, HEAD_DIM), lambda i: (i, 0, 0))
    return pl.pallas_call(
        functools.partial(_sample_select_kernel, pos=pos),
        out_shape=[jax.ShapeDtypeStruct((n, NSA_HEADS, HEAD_DIM), F32),
                   jax.ShapeDtypeStruct((n * NSA_KV_HEADS, LANES), jnp.int32)],
        grid=(n // SS_SEQ,),
        in_specs=[qsp, csp, csp],
        out_specs=[qsp, pl.BlockSpec((SS_SEQ * NSA_KV_HEADS, LANES), lambda i: (i, 0))],
        compiler_params=pltpu.CompilerParams(dimension_semantics=("arbitrary",), vmem_limit_bytes=VMEM_LIMIT),
        name="sample_select",
    )(q, kc, vc)


N_SLOTS = NSA_KV_HEADS * SEL_TOPK


def _sample_attend_kernel(sel_ref, pt_ref, q_ref, knew_ref, vnew_ref, wknew_ref, wvnew_ref, wkb_ref, wvb_ref,
                          ocmp_ref, gl_ref, *rest, pos, past_blocks):
    del pt_ref
    kblk, vblk, o_ref = rest[:N_SLOTS], rest[N_SLOTS:2 * N_SLOTS], rest[2 * N_SLOTS]
    kv, g, rb = NSA_KV_HEADS, NSA_GROUP, ROWS_PER_BLOCK
    n = pl.program_id(0)
    q = q_ref[0] * (HEAD_DIM ** -0.5)
    row = lax.broadcasted_iota(jnp.int32, (rb, HEAD_DIM), 0)
    col = lax.broadcasted_iota(jnp.int32, (1, rb), 1)

    def new_block(ref):
        out = jnp.zeros((rb, HEAD_DIM), F32)
        for k in range(kv):
            out = jnp.where(row == k, ref[0, k:k + 1, :], out)
        return out

    def per_head_rows(ref):
        return jnp.concatenate([jnp.broadcast_to(ref[0, k:k + 1, :], (g, HEAD_DIM)) for k in range(kv)], axis=0)

    knew, vnew = new_block(knew_ref), new_block(vnew_ref)
    o_sel = []
    for k in range(kv):
        ks, vs, ms = [], [], []
        for j in range(SEL_TOPK):
            b = sel_ref[(n * kv + k) * SEL_TOPK + j]
            is_new = jnp.broadcast_to(b, (rb, HEAD_DIM)) >= past_blocks
            ks.append(jnp.where(is_new, knew, kblk[k * SEL_TOPK + j][0]))
            vs.append(jnp.where(is_new, vnew, vblk[k * SEL_TOPK + j][0]))
            first = jnp.where(b >= 0, b, 1 << 24) * CMP_BLOCK
            ms.append((first + _div(col, kv) <= pos) & (_mod(col, kv) == k))
        s = _dot_nt(q[k * g:(k + 1) * g].astype(BF16), jnp.concatenate(ks, axis=0).astype(BF16))
        p = _masked_softmax(s, jnp.concatenate(ms, axis=1))
        o_sel.append(jnp.dot(p.astype(BF16), jnp.concatenate(vs, axis=0).astype(BF16), preferred_element_type=F32))
    o_sel = jnp.concatenate(o_sel, axis=0)

    nwr = wkb_ref.shape[1]
    s_w = _dot_nt(q.astype(BF16), wkb_ref[0].astype(BF16))
    cw = lax.broadcasted_iota(jnp.int32, (NSA_HEADS, nwr), 1)
    hw = lax.broadcasted_iota(jnp.int32, (NSA_HEADS, nwr), 0)
    kpos = pos - nwr // kv + _div(cw, kv)
    mw = (kpos <= pos) & (pos - kpos < WINDOW) & (_mod(cw, kv) == _div(hw, g))
    s_n = jnp.sum(q * per_head_rows(wknew_ref), axis=1, keepdims=True)
    smw = jnp.where(mw, s_w, NEG)
    mx = jnp.maximum(jnp.max(smw, axis=1, keepdims=True), s_n)
    e_w = jnp.where(mw, jnp.exp(smw - mx), 0.0)
    e_n = jnp.exp(s_n - mx)
    den = jnp.sum(e_w, axis=1, keepdims=True) + e_n
    o_win = (jnp.dot(e_w.astype(BF16), wvb_ref[0].astype(BF16), preferred_element_type=F32)
             + e_n * per_head_rows(wvnew_ref)) / den

    gate = 1.0 / (1.0 + jnp.exp(-gl_ref[0]))
    o_ref[0] = gate[:, 0:1] * ocmp_ref[0] + gate[:, 1:2] * o_sel + gate[:, 2:3] * o_win


def _sample_attend(sel, page_table, q, sk, sv, wk, wv, win_k, win_v, pool_k, pool_v, o_cmp, gl, pos):
    n, n_pages = page_table.shape
    past_blocks = n_pages * BLOCKS_PER_PAGE

    def slot_spec(k, j):
        def imap(s, sel_r, pt_r):
            b = jnp.maximum(sel_r[(s * NSA_KV_HEADS + k) * SEL_TOPK + j], 0)
            page = jnp.minimum(_div(b, BLOCKS_PER_PAGE), n_pages - 1)
            return (pt_r[s * n_pages + page], _mod(b, BLOCKS_PER_PAGE), 0)
        return pl.BlockSpec((1, ROWS_PER_BLOCK, HEAD_DIM), imap)

    slots = [slot_spec(k, j) for k in range(NSA_KV_HEADS) for j in range(SEL_TOPK)]
    per_seq = lambda a: pl.BlockSpec((1,) + a.shape[1:], lambda s, sel_r, pt_r: (s, 0, 0))
    dense = [q, sk, sv, wk, wv, win_k, win_v, o_cmp, gl]
    return pl.pallas_call(
        functools.partial(_sample_attend_kernel, pos=pos, past_blocks=past_blocks),
        out_shape=jax.ShapeDtypeStruct((n, NSA_HEADS, HEAD_DIM), F32),
        grid_spec=pltpu.PrefetchScalarGridSpec(
            num_scalar_prefetch=2,
            grid=(n,),
            in_specs=[per_seq(a) for a in dense] + slots + slots,
            out_specs=pl.BlockSpec((1, NSA_HEADS, HEAD_DIM), lambda s, sel_r, pt_r: (s, 0, 0))),
        compiler_params=pltpu.CompilerParams(dimension_semantics=("arbitrary",), vmem_limit_bytes=VMEM_LIMIT),
        name="sample_attend",
    )(sel, page_table.reshape(-1), *dense, *([pool_k] * N_SLOTS), *([pool_v] * N_SLOTS))


def _retention_tables(seq):
    c = RET_CHUNK
    log_g = jnp.log1p(-jnp.exp2(-5.0 - jnp.arange(RET_HEADS, dtype=F32)))
    i = jnp.arange(c, dtype=F32)
    rel = i[:, None] - i[None, :]
    dmask = jnp.where(rel[None] >= 0, jnp.exp(jnp.maximum(rel[None], 0.0) * log_g[:, None, None]), 0.0)
    q_dec = jnp.exp((i + 1.0)[None] * log_g[:, None])[..., None]
    k_dec = jnp.exp((c - 1.0 - i)[None] * log_g[:, None])[..., None]
    c_dec = jnp.exp(c * log_g)[:, None, None]
    bc = lambda t: jnp.broadcast_to(t, (RET_HEADS, c, HEAD_DIM))
    inv = RET_ROPE_THETA ** (-jnp.arange(0, HEAD_DIM, 2, dtype=F32) / HEAD_DIM)
    ang = jnp.arange(seq)[:, None].astype(F32) * inv
    cos, sin = jnp.cos(ang), jnp.sin(ang)
    return (dmask, bc(q_dec), bc(k_dec), jnp.broadcast_to(c_dec, (RET_HEADS, 1, HEAD_DIM)),
            jnp.concatenate([cos, cos], -1), jnp.concatenate([-sin, sin], -1))


def _retention_kernel(q_ref, k_ref, v_ref, g_ref, cos_ref, sin_ref, dmask_ref, qdec_ref, kdec_ref, cdec_ref, gn_ref,
                      o_ref, st_ref, s_scr):
    c = pl.program_id(1)

    @pl.when(c == 0)
    def _():
        s_scr[...] = jnp.zeros(s_scr.shape, F32)

    cosf, sins = cos_ref[...], sin_ref[...]
    half = HEAD_DIM // 2
    for h in range(RET_HEADS):
        sl = slice(h * HEAD_DIM, (h + 1) * HEAD_DIM)
        qh, kh = q_ref[:, sl], k_ref[:, sl]
        qr = qh * cosf + pltpu.roll(qh, half, 1) * sins
        kr = (kh * cosf + pltpu.roll(kh, half, 1) * sins) * (HEAD_DIM ** -0.5)
        vb = v_ref[:, sl].astype(BF16)
        att = _dot_nt(qr.astype(BF16), kr.astype(BF16)) * dmask_ref[h]
        s_prev = s_scr[h]
        o = (jnp.dot(att.astype(BF16), vb, preferred_element_type=F32)
             + jnp.dot((qr * qdec_ref[h]).astype(BF16), s_prev.astype(BF16), preferred_element_type=F32))
        s_scr[h] = cdec_ref[h] * s_prev + lax.dot_general(
            (kr * kdec_ref[h]).astype(BF16), vb, (((0,), (0,)), ((), ())), preferred_element_type=F32)
        mu = jnp.mean(o, axis=-1, keepdims=True)
        var = jnp.mean(jnp.square(o - mu), axis=-1, keepdims=True)
        on = (o - mu) * lax.rsqrt(var + GN_EPS) * gn_ref[:, sl]
        gg = g_ref[:, sl]
        o_ref[:, sl] = gg * (1.0 / (1.0 + jnp.exp(-gg))) * on

    @pl.when(c == pl.num_programs(1) - 1)
    def _():
        st_ref[0] = s_scr[...]


def _retention_prompt(main, gn_g, batch, seq):
    nc = seq // RET_CHUNK
    dmask, q_dec, k_dec, c_dec, cosf, sins = _retention_tables(seq)
    col = lambda j: pl.BlockSpec((RET_CHUNK, RET_W), lambda b, c: (b * nc + c, j))
    pos_tab = pl.BlockSpec((RET_CHUNK, HEAD_DIM), lambda b, c: (c, 0))
    full = lambda a: pl.BlockSpec(a.shape, lambda b, c: (0,) * a.ndim)
    gn = gn_g.reshape(1, RET_W)
    return pl.pallas_call(
        _retention_kernel,
        out_shape=[jax.ShapeDtypeStruct((batch * seq, RET_W), F32),
                   jax.ShapeDtypeStruct((batch, RET_HEADS, HEAD_DIM, HEAD_DIM), F32)],
        grid=(batch, nc),
        in_specs=[col(0), col(1), col(2), col(3), pos_tab, pos_tab,
                  full(dmask), full(q_dec), full(k_dec), full(c_dec), full(gn)],
        out_specs=[pl.BlockSpec((RET_CHUNK, RET_W), lambda b, c: (b * nc + c, 0)),
                   pl.BlockSpec((1, RET_HEADS, HEAD_DIM, HEAD_DIM), lambda b, c: (b, 0, 0, 0))],
        scratch_shapes=[pltpu.VMEM((RET_HEADS, HEAD_DIM, HEAD_DIM), F32)],
        compiler_params=pltpu.CompilerParams(dimension_semantics=("arbitrary", "arbitrary"),
                                             vmem_limit_bytes=VMEM_LIMIT),
        name="retention_prompt",
    )(main, main, main, main, cosf, sins, dmask, q_dec, k_dec, c_dec, gn)


NSA_TQ = 128
NSA_KEY_CHUNK = 512
NSA_GROUP = NSA_HEADS // NSA_KV_HEADS


def _dot_nt(a, b):
    return lax.dot_general(a, b, (((1,), (1,)), ((), ())), preferred_element_type=F32)


def _masked_softmax(s, m):
    sm = jnp.where(m, s, NEG)
    e = jnp.exp(sm - jnp.max(sm, axis=-1, keepdims=True))
    return jnp.where(m, e / jnp.sum(e, axis=-1, keepdims=True), 0.0)


def _select_mask_t(imp_t, pos_t):
    nb = imp_t.shape[0]
    blk = lax.broadcasted_iota(jnp.int32, imp_t.shape, 0)
    cur = jnp.right_shift(pos_t, int(math.log2(CMP_BLOCK)))
    forced = (blk == 0) | (blk == cur) | (blk == cur - 1)
    score = jnp.where(blk > cur, NEG, jnp.where(forced, -NEG, imp_t))
    rank = jnp.zeros(score.shape, jnp.int32)
    for i in range(nb):
        si = score[i:i + 1, :]
        ahead = (si > score) | ((si == score) & (blk > i))
        rank = rank + ahead.astype(jnp.int32)
    return (rank < SEL_TOPK) & (score > 0.5 * NEG)


def _nsa_prompt_kernel(q_ref, kc_ref, vc_ref, sk_ref, sv_ref, wk_ref, wv_ref, gl_ref, o_ref):
    tq, g, kc_n = NSA_TQ, NSA_GROUP, NSA_KEY_CHUNK
    qi = pl.program_id(2)
    q4 = q_ref[...] * (HEAD_DIM ** -0.5)
    qs = jnp.concatenate([q4[:, i * HEAD_DIM:(i + 1) * HEAD_DIM] for i in range(g)], axis=0).astype(BF16)
    pos = qi * tq + lax.broadcasted_iota(jnp.int32, (tq, 1), 0)
    pos4 = jnp.concatenate([pos] * g, axis=0)

    nb = kc_ref.shape[1]
    assert tq == LANES
    s_c = _dot_nt(kc_ref[0].astype(BF16), qs)
    blk_t = lax.broadcasted_iota(jnp.int32, (nb, g * tq), 0)
    pos_t = qi * tq + lax.broadcasted_iota(jnp.int32, (1, tq), 1)
    m_c = (blk_t + 1) * CMP_BLOCK - 1 <= jnp.concatenate([pos_t] * g, axis=1)
    sm_c = jnp.where(m_c, s_c, NEG)
    e_c = jnp.exp(sm_c - jnp.max(sm_c, axis=0, keepdims=True))
    p_c = jnp.where(m_c, e_c / jnp.sum(e_c, axis=0, keepdims=True), 0.0)
    o_cmp = lax.dot_general(p_c.astype(BF16), vc_ref[0].astype(BF16), (((0,), (0,)), ((), ())),
                            preferred_element_type=F32)
    imp_t = p_c[:, 0:tq]
    for i in range(1, g):
        imp_t = imp_t + p_c[:, i * tq:(i + 1) * tq]

    seln_t = jnp.where(_select_mask_t(imp_t, pos_t), 0.0, NEG)
    seln = jnp.concatenate([seln_t, jnp.zeros((tq - nb, tq), F32)], axis=0).T[:, :nb].astype(BF16)
    q_aug = jnp.concatenate([qs, jnp.concatenate([seln] * g, axis=0)], axis=1)
    key_idx = lax.broadcasted_iota(jnp.int32, (kc_n, nb), 0)
    key_blk = lax.broadcasted_iota(jnp.int32, (kc_n, nb), 1)
    key_row = lax.broadcasted_iota(jnp.int32, (1, kc_n), 1)

    def scores(k0):
        member = jnp.right_shift(key_idx + k0, int(math.log2(CMP_BLOCK))) == key_blk
        k_aug = jnp.concatenate([sk_ref[pl.ds(k0, kc_n), :].astype(BF16),
                                 jnp.where(member, 1.0, 0.0).astype(BF16)], axis=1)
        return _dot_nt(q_aug, k_aug)

    def values(k0):
        return sv_ref[pl.ds(k0, kc_n), :].astype(BF16)

    assert kc_n % tq == 0
    c_diag = _div(qi, kc_n // tq)
    kd = pl.multiple_of(c_diag * kc_n, kc_n)
    s_d = jnp.where(key_row + kd <= pos4, scores(kd), NEG)
    m_d = jnp.max(s_d, axis=-1, keepdims=True)
    p_d = jnp.exp(s_d - m_d)
    first = (m_d, jnp.sum(p_d, axis=-1, keepdims=True),
             jnp.dot(p_d.astype(BF16), values(kd), preferred_element_type=F32))

    def chunk(c, carry):
        m_i, l_i, acc = carry
        k0 = pl.multiple_of(c * kc_n, kc_n)
        s = scores(k0)
        m_new = jnp.maximum(m_i, jnp.max(s, axis=-1, keepdims=True))
        alpha = jnp.exp(m_i - m_new)
        p = jnp.exp(s - m_new)
        l_new = alpha * l_i + jnp.sum(p, axis=-1, keepdims=True)
        return m_new, l_new, alpha * acc + jnp.dot(p.astype(BF16), values(k0), preferred_element_type=F32)

    _, l_f, acc_f = lax.fori_loop(0, c_diag, chunk, first)
    o_sel = acc_f / l_f

    nwk = WINDOW + tq
    kstart = pl.multiple_of(jnp.maximum(qi * tq - WINDOW, 0), tq)
    kpos = kstart + lax.broadcasted_iota(jnp.int32, (1, nwk), 1)
    band = jnp.where((kpos <= pos) & (pos - kpos < WINDOW), 0.0, NEG)
    s_w = _dot_nt(qs, wk_ref[pl.ds(kstart, nwk), :].astype(BF16)) + jnp.concatenate([band] * g, axis=0)
    e_w = jnp.exp(s_w - jnp.max(s_w, axis=-1, keepdims=True))
    o_win = (jnp.dot(e_w.astype(BF16), wv_ref[pl.ds(kstart, nwk), :].astype(BF16), preferred_element_type=F32)
             / jnp.sum(e_w, axis=-1, keepdims=True))

    gate = 1.0 / (1.0 + jnp.exp(-gl_ref[...]))
    for i in range(g):
        rows = slice(i * tq, (i + 1) * tq)
        o_ref[:, i * HEAD_DIM:(i + 1) * HEAD_DIM] = (gate[:, 3 * i:3 * i + 1] * o_cmp[rows]
                                                     + gate[:, 3 * i + 1:3 * i + 2] * o_sel[rows]
                                                     + gate[:, 3 * i + 2:3 * i + 3] * o_win[rows])


def _nsa_prompt(main, kc, vc, batch, seq):
    nq = seq // NSA_TQ
    gw = NSA_GROUP * HEAD_DIM
    qcol, glcol = _split_col0(4) // gw, GL_COL0 // LANES
    seq_spec = lambda split: pl.BlockSpec((seq, HEAD_DIM),
                                          lambda b, k, i: (b, _split_col0(split) // HEAD_DIM + k))
    cmp_spec = pl.BlockSpec((1, seq // CMP_BLOCK, HEAD_DIM), lambda b, k, i: (b, 0, k))
    return pl.pallas_call(
        _nsa_prompt_kernel,
        out_shape=jax.ShapeDtypeStruct((batch * seq, NSA_W), F32),
        grid=(batch, NSA_KV_HEADS, nq),
        in_specs=[pl.BlockSpec((NSA_TQ, gw), lambda b, k, i: (b * nq + i, qcol + k)), cmp_spec, cmp_spec,
                  seq_spec(7), seq_spec(8), seq_spec(9), seq_spec(10),
                  pl.BlockSpec((NSA_TQ, LANES), lambda b, k, i: (b * nq + i, glcol + k))],
        out_specs=pl.BlockSpec((NSA_TQ, gw), lambda b, k, i: (b * nq + i, k)),
        compiler_params=pltpu.CompilerParams(dimension_semantics=("arbitrary",) * 3,
                                             vmem_limit_bytes=VMEM_LIMIT),
        name="nsa_prompt",
    )(main, kc, vc, main, main, main, main, main)


WO_TM = 512


def _layer_norm_rows(y, g, b):
    mu = jnp.mean(y, axis=-1, keepdims=True)
    var = jnp.mean(jnp.square(y - mu), axis=-1, keepdims=True)
    return (y - mu) * lax.rsqrt(var + LN_EPS) * g + b


EXPERT_LANE0 = N_GROUPS
R_EID, R_RANK, R_GATE = 0, EXPERT_TOPK, 2 * EXPERT_TOPK


def _route(h, wr_ref, br_ref, carry):
    tm = h.shape[0]
    logit = jnp.dot(h.astype(BF16), wr_ref[...], preferred_element_type=F32) + br_ref[...]
    lane = lax.broadcasted_iota(jnp.int32, (tm, LANES), 1)
    lanef = lane.astype(F32)
    first_lane = lambda hit: jnp.min(jnp.where(hit, lanef, float(LANES)), axis=1, keepdims=True)
    is_g = lane < N_GROUPS
    gl = jnp.where(is_g, logit, LOWEST)
    gmx = jnp.max(gl, axis=1, keepdims=True)
    grp = first_lane(gl == gmx)
    p_grp = 1.0 / jnp.sum(jnp.where(is_g, jnp.exp(gl - gmx), 0.0), axis=1, keepdims=True)
    lane_grp = jnp.right_shift(lane - EXPERT_LANE0, int(math.log2(EXPERTS_PER_GROUP)))
    in_grp = lane_grp.astype(F32) == grp
    el = jnp.where(in_grp, logit, LOWEST)
    ee = jnp.where(in_grp, jnp.exp(el - jnp.max(el, axis=1, keepdims=True)), 0.0)
    pe = jnp.where(in_grp, ee / jnp.sum(ee, axis=1, keepdims=True), -1.0)
    p1 = jnp.max(pe, axis=1, keepdims=True)
    l1 = first_lane(pe == p1)
    pe2 = jnp.where(lanef == l1, -1.0, pe)
    p2 = jnp.max(pe2, axis=1, keepdims=True)
    l2 = first_lane(pe2 == p2)
    den = p1 + p2
    o1, o2 = lanef == l1, lanef == l2
    onehot = jnp.where(o1 | o2, 1.0, 0.0)
    r = lax.broadcasted_iota(jnp.int32, (tm, tm), 0)
    c = lax.broadcasted_iota(jnp.int32, (tm, tm), 1)
    earlier = jnp.where(c < r, 1.0, 0.0).astype(BF16)
    prefix = jnp.dot(earlier, onehot.astype(BF16), preferred_element_type=F32) + carry[0:1, :]
    rank1 = jnp.sum(jnp.where(o1, prefix, 0.0), axis=1, keepdims=True)
    rank2 = jnp.sum(jnp.where(o2, prefix, 0.0), axis=1, keepdims=True)
    carry[0:1, :] = carry[0:1, :] + jnp.sum(onehot, axis=0, keepdims=True)
    fields = [l1 - EXPERT_LANE0, l2 - EXPERT_LANE0, rank1, rank2, p_grp * p1 / den, p_grp * p2 / den]
    rec = jnp.zeros((tm, LANES), F32)
    for j, f in enumerate(fields):
        rec = jnp.where(lane == j, f, rec)
    return rec


def _wo_ln_route_kernel(fr_ref, fn_ref, x_ref, w_ref, g_ref, b_ref, wr_ref, br_ref, base_ref,
                        h_ref, route_ref, cnt_ref, carry):
    i = pl.program_id(0)

    @pl.when(i == 0)
    def _():
        carry[...] = base_ref[...]

    y = (DEEPNORM_ALPHA * x_ref[...]
         + jnp.dot(fr_ref[...].astype(BF16), w_ref[0:RET_W, :], preferred_element_type=F32)
         + jnp.dot(fn_ref[...].astype(BF16), w_ref[RET_W:MIX_W, :], preferred_element_type=F32))
    h = _layer_norm_rows(y, g_ref[...], b_ref[...])
    h_ref[...] = h
    route_ref[...] = _route(h, wr_ref, br_ref, carry)

    @pl.when(i == pl.num_programs(0) - 1)
    def _():
        cnt_ref[...] = carry[...]


def _wo_ln_route(f_ret, f_nsa, x, w_o_bf16, ln_g, ln_b, w_route, b_route, base_counts):
    t, d = x.shape
    tm = min(WO_TM, t)
    assert t % tm == 0
    row = lambda w: pl.BlockSpec((tm, w), lambda i: (i, 0))
    full = lambda a: pl.BlockSpec(a.shape, lambda i: (0,) * a.ndim)
    lg, lb = ln_g.reshape(1, d), ln_b.reshape(1, d)
    return pl.pallas_call(
        _wo_ln_route_kernel,
        out_shape=[jax.ShapeDtypeStruct((t, d), F32), jax.ShapeDtypeStruct((t, LANES), F32),
                   jax.ShapeDtypeStruct((SUBLANES, LANES), F32)],
        grid=(t // tm,),
        in_specs=[row(RET_W), row(NSA_W), row(d), full(w_o_bf16), full(lg), full(lb),
                  full(w_route), full(b_route), full(base_counts)],
        out_specs=[row(d), row(LANES), pl.BlockSpec((SUBLANES, LANES), lambda i: (0, 0))],
        scratch_shapes=[pltpu.VMEM((SUBLANES, LANES), F32)],
        compiler_params=pltpu.CompilerParams(dimension_semantics=("arbitrary",), vmem_limit_bytes=VMEM_LIMIT),
        name="wo_ln1_route",
    )(f_ret, f_nsa, x, w_o_bf16, lg, lb, w_route, b_route, base_counts)


MOE_BM = 256
MOE_SUB = 64
TABLE_UNROLL = 8


def _expert_kernel(blk_e_ref, n_used_ref, slot_ref, h_ref, wg_ref, wu_ref, wd_ref, y_ref,
                   src_tok, dst_row, xbuf, obuf, gsem, ssem, *, plane, n_rows):
    del blk_e_ref
    bm = xbuf.shape[1] * xbuf.shape[2]
    i = pl.program_id(0)
    n_used = n_used_ref[0]
    n_asg = slot_ref.shape[0]
    dump0 = EXPERT_TOPK * plane
    assert bm & (bm - 1) == 0

    n_sub, sub = xbuf.shape[1], xbuf.shape[2]

    def gather(blk, buf_slot):
        def sub_block(j, carry):
            base = blk * bm + j * sub
            for u in range(sub):
                pltpu.make_async_copy(h_ref.at[pl.ds(src_tok[base + u], 1), :],
                                      xbuf.at[buf_slot, j, pl.ds(u, 1), :], gsem.at[buf_slot]).start(priority=u % 2)
            return carry
        lax.fori_loop(0, n_sub, sub_block, 0)

    def scatter(blk, buf_slot):
        def sub_block(j, carry):
            base = blk * bm + j * sub
            for u in range(sub):
                pltpu.make_async_copy(obuf.at[buf_slot, j, pl.ds(u, 1), :],
                                      y_ref.at[pl.ds(dst_row[base + u], 1), :], ssem.at[buf_slot]).start(priority=u % 2)
            return carry
        lax.fori_loop(0, n_sub, sub_block, 0)

    def wait_block(buf, sem, buf_slot):
        pltpu.make_async_copy(buf.at[buf_slot], buf.at[buf_slot], sem.at[buf_slot]).wait()

    @pl.when(i == 0)
    def _():
        def clear(t, carry):
            for u in range(TABLE_UNROLL):
                r = t * TABLE_UNROLL + u
                src_tok[r] = 0
                dst_row[r] = dump0 + jnp.bitwise_and(r, 2 * bm - 1)
            return carry
        lax.fori_loop(0, n_used_ref[1], clear, 0)

        def fill(t, carry):
            for u in range(TABLE_UNROLL):
                a = t * TABLE_UNROLL + u
                tok = jnp.right_shift(a, 1)
                src_tok[slot_ref[a]] = tok
                dst_row[slot_ref[a]] = jnp.bitwise_and(a, 1) * plane + tok
            return carry
        lax.fori_loop(0, n_used_ref[2], fill, 0)
        gather(0, 0)
        n_tok = h_ref.shape[0]
        tail = plane - n_tok
        assert tail >= 0
        obuf[1] = jnp.zeros(obuf.shape[1:], F32)
        spans = [(dump0, 2 * bm)] + ([(k * plane + n_tok, tail) for k in range(EXPERT_TOPK)] if tail else [])
        copies = []
        for first, count in spans:
            for j in range(-(-count // sub)):
                rows = min(sub, count - j * sub)
                copies.append(pltpu.make_async_copy(obuf.at[1, j % n_sub, pl.ds(0, rows), :],
                                                    y_ref.at[pl.ds(first + j * sub, rows), :], ssem.at[1]))
        for cp in copies:
            cp.start()
        for cp in copies:
            cp.wait()

    slot = jnp.bitwise_and(i, 1)

    @pl.when(i + 1 < n_used)
    def _():
        gather(i + 1, 1 - slot)

    @pl.when(i < n_used)
    def _():
        wait_block(xbuf, gsem, slot)

        @pl.when(i >= 2)
        def _():
            wait_block(obuf, ssem, slot)

        xb = xbuf[slot].reshape(bm, xbuf.shape[3]).astype(BF16)
        hg = jnp.dot(xb, wg_ref[0].astype(BF16), preferred_element_type=F32)
        hu = jnp.dot(xb, wu_ref[0].astype(BF16), preferred_element_type=F32)
        hb = hg * (1.0 / (1.0 + jnp.exp(-hg))) * hu
        yb = jnp.dot(hb.astype(BF16), wd_ref[0].astype(BF16), preferred_element_type=F32)
        obuf[slot] = yb.reshape(obuf.shape[1:])
        scatter(i, slot)

    @pl.when(i == pl.num_programs(0) - 1)
    def _():
        @pl.when(n_used >= 2)
        def _():
            wait_block(obuf, ssem, jnp.bitwise_and(n_used, 1))
        wait_block(obuf, ssem, jnp.bitwise_and(n_used - 1, 1))


def _expert_ffn(h, slot, blk_e, n_used, w_gate, w_up, w_down, plane):
    t, d = h.shape
    n_asg = slot.shape[0]
    assert n_asg == t * EXPERT_TOPK and EXPERT_TOPK == 2
    n_blk = -(-(n_asg + N_EXPERTS * (MOE_BM - 1)) // MOE_BM)
    de = w_gate.shape[2]
    assert n_asg % TABLE_UNROLL == 0 and MOE_BM % TABLE_UNROLL == 0
    n_used = jnp.concatenate([n_used, jnp.array([n_blk * MOE_BM // TABLE_UNROLL, n_asg // TABLE_UNROLL], jnp.int32)])
    wspec = lambda shape: pl.BlockSpec((1,) + shape, lambda i, be, nu, sl: (be[i], 0, 0))
    return pl.pallas_call(
        functools.partial(_expert_kernel, plane=plane, n_rows=n_blk * MOE_BM),
        out_shape=jax.ShapeDtypeStruct((EXPERT_TOPK * plane + 2 * MOE_BM, d), F32),
        grid_spec=pltpu.PrefetchScalarGridSpec(
            num_scalar_prefetch=3,
            grid=(n_blk,),
            in_specs=[pl.BlockSpec(memory_space=pl.ANY), wspec((d, de)), wspec((d, de)), wspec((de, d))],
            out_specs=pl.BlockSpec(memory_space=pl.ANY),
            scratch_shapes=[pltpu.SMEM((n_blk * MOE_BM,), jnp.int32), pltpu.SMEM((n_blk * MOE_BM,), jnp.int32),
                            pltpu.VMEM((2, MOE_BM // MOE_SUB, MOE_SUB, d), F32),
                            pltpu.VMEM((2, MOE_BM // MOE_SUB, MOE_SUB, d), F32),
                            pltpu.SemaphoreType.DMA((2,)), pltpu.SemaphoreType.DMA((2,))]),
        compiler_params=pltpu.CompilerParams(dimension_semantics=("arbitrary",), vmem_limit_bytes=VMEM_LIMIT),
        name="expert_ffn",
    )(blk_e, n_used, slot, h, w_gate, w_up, w_down)


def _moe_ln_kernel(h_ref, y0_ref, y1_ref, route_ref, g_ref, b_ref, o_ref):
    rec = route_ref[...]
    y = (DEEPNORM_ALPHA * h_ref[...] + rec[:, R_GATE:R_GATE + 1] * y0_ref[...]
         + rec[:, R_GATE + 1:R_GATE + 2] * y1_ref[...])
    o_ref[...] = _layer_norm_rows(y, g_ref[...], b_ref[...])


def _moe_ln(h, y, route, ln_g, ln_b, row0, n_rows, plane):
    d = h.shape[1]
    tm = min(WO_TM, n_rows)
    assert n_rows % tm == 0 and row0 % tm == 0 and plane % tm == 0 and EXPERT_TOPK == 2
    off = row0 // tm
    row = lambda w, o: pl.BlockSpec((tm, w), lambda i: (i + o, 0))
    vec = pl.BlockSpec((1, d), lambda i: (0, 0))
    return pl.pallas_call(
        _moe_ln_kernel,
        out_shape=jax.ShapeDtypeStruct((n_rows, d), F32),
        grid=(n_rows // tm,),
        in_specs=[row(d, off), row(d, off), row(d, off + plane // tm), row(LANES, off), vec, vec],
        out_specs=row(d, 0),
        compiler_params=pltpu.CompilerParams(dimension_semantics=("arbitrary",), vmem_limit_bytes=VMEM_LIMIT),
        name="moe_ln2",
    )(h, y, y, route, ln_g.reshape(1, d), ln_b.reshape(1, d))


def _layer_norm(x, g, b):
    xf = x.astype(F32)
    mu = xf.mean(-1, keepdims=True)
    var = jnp.square(xf - mu).mean(-1, keepdims=True)
    return ((xf - mu) * lax.rsqrt(var + LN_EPS) * g + b).astype(x.dtype)


def _rope(x, pos, rot_dim, theta):
    half = rot_dim // 2
    inv = theta ** (-jnp.arange(0, rot_dim, 2, dtype=F32) / rot_dim)
    ang = pos[..., None].astype(F32) * inv
    cos = jnp.cos(ang)[:, :, None, :]
    sin = jnp.sin(ang)[:, :, None, :]
    xr = x[..., :rot_dim].astype(F32)
    x1, x2 = xr[..., :half], xr[..., half:]
    rot = jnp.concatenate([x1 * cos - x2 * sin, x2 * cos + x1 * sin], -1).astype(x.dtype)
    return jnp.concatenate([rot, x[..., rot_dim:]], -1)


def _heads(t, n):
    return t.reshape(t.shape[0], t.shape[1], n, HEAD_DIM)


def _chunk_retention(q, k, v, s0):
    n, l, h, d = q.shape
    c = RET_CHUNK if l % RET_CHUNK == 0 else l
    nc = l // c
    log_g = jnp.log1p(-jnp.exp2(-5.0 - jnp.arange(h, dtype=F32)))
    i = jnp.arange(c, dtype=F32)
    rel = i[:, None] - i[None, :]
    dmask = jnp.where(rel[None] >= 0, jnp.exp(jnp.maximum(rel[None], 0.0) * log_g[:, None, None]), 0.0)
    q_dec = jnp.exp((i + 1.0)[None] * log_g[:, None])[..., None]
    k_dec = jnp.exp((c - 1.0 - i)[None] * log_g[:, None])[..., None]
    c_dec = jnp.exp(c * log_g)[:, None, None]

    def to_chunks(t):
        return t.astype(F32).reshape(n, nc, c, h, d).transpose(1, 0, 3, 2, 4)

    def step(s, qkv):
        qc, kc, vc = qkv
        att = jnp.einsum('bhid,bhjd->bhij', qc, kc) * dmask
        o = jnp.einsum('bhij,bhjd->bhid', att, vc) + jnp.einsum('bhid,bhde->bhie', qc * q_dec, s)
        s = c_dec * s + jnp.einsum('bhjd,bhje->bhde', kc * k_dec, vc)
        return s, o

    s, o = lax.scan(step, s0.astype(F32), (to_chunks(q), to_chunks(k), to_chunks(v)))
    return o.transpose(1, 0, 3, 2, 4).reshape(n, l, h, d), s


def _retention_group(rq, rk, rv, rg, pos, s0, gn_g):
    q = _rope(_heads(rq, RET_HEADS), pos, HEAD_DIM, RET_ROPE_THETA)
    k = _rope(_heads(rk, RET_HEADS), pos, HEAD_DIM, RET_ROPE_THETA) * (HEAD_DIM ** -0.5)
    v = _heads(rv, RET_HEADS)
    o, s = _chunk_retention(q, k, v, s0)
    mu = o.mean(-1, keepdims=True)
    var = jnp.square(o - mu).mean(-1, keepdims=True)
    on = (o - mu) * lax.rsqrt(var + GN_EPS) * gn_g.reshape(RET_HEADS, HEAD_DIM).astype(F32)
    out = jax.nn.silu(rg.astype(F32)) * on.reshape(rg.shape)
    return out.astype(rq.dtype), s


def _gqa_attend(q, k, v, mask):
    n, lq, h, d = q.shape
    kv = k.shape[2]
    qg = q.reshape(n, lq, kv, h // kv, d)
    s = jnp.einsum('nqkgd,nskd->nkgqs', qg, k).astype(F32) * (d ** -0.5)
    m = mask[:, None, None]
    p = jax.nn.softmax(jnp.where(m, s, NEG), axis=-1) * m
    o = jnp.einsum('nkgqs,nskd->nqkgd', p.astype(v.dtype), v)
    return o.reshape(n, lq, h, d), p


def _nsa_heads(nq, ck, sk, wk, cv, sv, wv, pos):
    rp = lambda t, nh: _rope(_heads(t, nh), pos, ROT_DIM, ROPE_THETA)
    return (rp(nq, NSA_HEADS), rp(ck, NSA_KV_HEADS), rp(sk, NSA_KV_HEADS), rp(wk, NSA_KV_HEADS),
            _heads(cv, NSA_KV_HEADS), _heads(sv, NSA_KV_HEADS), _heads(wv, NSA_KV_HEADS))


def _compress(rows, w):
    n, t, kv, d = rows.shape
    return jnp.einsum('nbjkd,jd->nbkd', rows.reshape(n, t // CMP_BLOCK, CMP_BLOCK, kv, d), w)


def _cmp_branch(q, pos, kc, vc):
    nb = kc.shape[1]
    blk_end = (jnp.arange(nb) + 1) * CMP_BLOCK - 1
    mask = blk_end[None, None, :] <= pos[:, :, None]
    o, p = _gqa_attend(q, kc, vc, mask)
    imp = p.sum(axis=2).transpose(0, 2, 1, 3)
    return o, imp


def _select_blocks(imp, pos, n_sel):
    nb = imp.shape[-1]
    imp = jnp.pad(imp, ((0, 0), (0, 0), (0, 0), (0, n_sel - nb)))
    blk = jnp.arange(n_sel)
    cur = (pos // CMP_BLOCK)[:, :, None, None]
    forced = (blk == 0) | (blk == cur) | (blk == cur - 1)
    score = jnp.where(blk > cur, NEG, jnp.where(forced, -NEG, imp))
    top, idx = lax.top_k(score, min(SEL_TOPK, n_sel))
    return idx, top > 0.5 * NEG


def _sel_attend(q, pos, ks, vs, idx, valid):
    n, lq, kv, kk, cb, d = ks.shape
    h = q.shape[2]
    kpos = idx[..., None] * CMP_BLOCK + jnp.arange(CMP_BLOCK)
    m = ((kpos <= pos[:, :, None, None, None]) & valid[..., None]).reshape(n, lq, kv, 1, kk * cb)
    qg = q.reshape(n, lq, kv, h // kv, d)
    kf = ks.reshape(n, lq, kv, kk * cb, d)
    vf = vs.reshape(n, lq, kv, kk * cb, d)
    s = jnp.einsum('nqkgd,nqkjd->nqkgj', qg, kf).astype(F32) * (d ** -0.5)
    p = jax.nn.softmax(jnp.where(m, s, NEG), axis=-1) * m
    o = jnp.einsum('nqkgj,nqkjd->nqkgd', p.astype(vf.dtype), vf)
    return o.reshape(n, lq, h, d)


def _sel_prompt(q, pos, k, v, idx, valid):
    b, s, h, d = q.shape
    kv = k.shape[2]
    nb = s // CMP_BLOCK
    nq = s // SEL_Q_BLOCK
    kb = k.reshape(b, nb, CMP_BLOCK, kv, d).transpose(0, 3, 1, 2, 4)
    vb = v.reshape(b, nb, CMP_BLOCK, kv, d).transpose(0, 3, 1, 2, 4)
    bi = jnp.arange(b)[:, None, None, None]
    hi = jnp.arange(kv)[None, None, :, None]

    def blockwise(t):
        return t.reshape(t.shape[0], nq, SEL_Q_BLOCK, *t.shape[2:]).swapaxes(0, 1)

    def one(args):
        qc, pc, ic, vc = args
        return _sel_attend(qc, pc, kb[bi, hi, ic], vb[bi, hi, ic], ic, vc)

    o = lax.map(one, (blockwise(q), blockwise(pos), blockwise(idx), blockwise(valid)))
    return o.swapaxes(0, 1).reshape(b, s, h, d)


def _win_prompt(q, k, v):
    b, s, h, d = q.shape
    kv = k.shape[2]
    nb = s // WIN_Q_BLOCK
    nprev = WINDOW // WIN_Q_BLOCK
    nw = nprev + 1
    padw = ((0, 0), (WINDOW, 0), (0, 0), (0, 0))
    kp = jnp.pad(k, padw).reshape(b, nb + nprev, WIN_Q_BLOCK, kv, d)
    vp = jnp.pad(v, padw).reshape(b, nb + nprev, WIN_Q_BLOCK, kv, d)
    kw = jnp.concatenate([kp[:, i:i + nb] for i in range(nw)], axis=2)
    vw = jnp.concatenate([vp[:, i:i + nb] for i in range(nw)], axis=2)
    qpos = jnp.arange(s).reshape(nb, WIN_Q_BLOCK)
    kpos = (jnp.arange(nb)[:, None] - nprev) * WIN_Q_BLOCK + jnp.arange(nw * WIN_Q_BLOCK)[None]
    qq, kk = qpos[:, :, None], kpos[:, None, :]
    mask = (kk <= qq) & (qq - kk < WINDOW) & (kk >= 0)
    mask = jnp.broadcast_to(mask[None], (b,) + mask.shape).reshape(b * nb, WIN_Q_BLOCK, nw * WIN_Q_BLOCK)
    o, _ = _gqa_attend(q.reshape(b * nb, WIN_Q_BLOCK, h, d), kw.reshape(b * nb, nw * WIN_Q_BLOCK, kv, d),
                       vw.reshape(b * nb, nw * WIN_Q_BLOCK, kv, d), mask)
    return o.reshape(b, s, h, d)


def _gather_selected(pool, new_rows, page_table, idx):
    n, l, kv, d = new_rows.shape
    n_pages = page_table.shape[1]
    past_blocks = n_pages * PAGE_SIZE // CMP_BLOCK
    nbn = -(-l // CMP_BLOCK)
    newb = jnp.pad(new_rows, ((0, 0), (0, nbn * CMP_BLOCK - l), (0, 0), (0, 0)))
    newb = newb.reshape(n, nbn, CMP_BLOCK, kv, d).transpose(0, 3, 1, 2, 4)
    bi = jnp.arange(n)[:, None, None, None]
    hi = jnp.arange(kv)[None, None, :, None]
    start = idx * CMP_BLOCK
    phys = page_table[bi, jnp.minimum(start // PAGE_SIZE, n_pages - 1)]
    off = (start % PAGE_SIZE)[..., None] + jnp.arange(CMP_BLOCK)
    past = pool[phys[..., None], off, hi[..., None]]
    new = newb[bi, hi, jnp.clip(idx - past_blocks, 0, nbn - 1)]
    return jnp.where((idx < past_blocks)[..., None, None], past, new)


def _nsa_combine(gl, o_cmp, o_sel, o_win):
    n, l = gl.shape[0], gl.shape[1]
    g = jax.nn.sigmoid(gl.astype(F32)).reshape(n, l, NSA_HEADS, 3, 1)
    o = g[..., 0, :] * o_cmp + g[..., 1, :] * o_sel + g[..., 2, :] * o_win
    return o.reshape(n, l, NSA_W).astype(o_cmp.dtype)


def _prompt_mixer(x, win_buf, w_in, w_cmp_k, w_cmp_v, gn_g):
    n, s, _ = x.shape
    main = _project(x, w_in, jnp.arange(s))
    ret_out, s_fin = _retention_prompt(main, gn_g, n, s)
    kc, vc = _compress_prompt(main, w_cmp_k, w_cmp_v)
    nsa = _nsa_prompt(main, kc.reshape(n, s // CMP_BLOCK, KV_W), vc.reshape(n, s // CMP_BLOCK, KV_W), n, s)
    ck, cv, sk, sv, wk, wv = [_heads(t, NSA_KV_HEADS) for t in _split_main(main, n, s, first=5)]
    feats = (ret_out, nsa)
    if s >= win_buf:
        bk, bv = wk[:, s - win_buf:], wv[:, s - win_buf:]
    else:
        padb = ((0, 0), (win_buf - s, 0), (0, 0), (0, 0))
        bk, bv = jnp.pad(wk, padb), jnp.pad(wv, padb)
    return feats, (ck, cv, sk, sv, bk, bv, s_fin)


def _sample_mixer(x, c_cmp_k, c_cmp_v, c_sel_k, c_sel_v, c_win_k, c_win_v, s_ret, page_table,
                  w_in, w_cmp_k, w_cmp_v, gn_g):
    n, l, _ = x.shape
    past = page_table.shape[1] * PAGE_SIZE
    pos = past + jnp.arange(l)[None]
    assert l == 1
    main = _project(x, w_in, jnp.full((n,), past, jnp.int32))
    rq, rk, rv, rg, nq, ck, cv, sk, sv, wk, wv = _split_main(main, n, l)
    gl = _gate_logits(main, n, l)
    ret_out, s_new = _retention_group(rq, rk, rv, rg, pos, s_ret, gn_g)
    q = _heads(nq, NSA_HEADS)
    ck, cv, sk, sv, wk, wv = [_heads(t, NSA_KV_HEADS) for t in (ck, cv, sk, sv, wk, wv)]
    assert l == 1 and past % CMP_BLOCK == 0 and c_win_k.shape[1] <= WINDOW
    kc, vc = _compress_paged(c_cmp_k, c_cmp_v, page_table, w_cmp_k, w_cmp_v)
    o_cmp, sel = _sample_select(q[:, 0], kc, vc, past)
    sel = sel[:, :SEL_TOPK].reshape(-1)
    gl_pad = jnp.pad(gl.reshape(n, NSA_HEADS, 3), ((0, 0), (0, 0), (0, LANES - 3)))
    nsa = _sample_attend(sel, page_table, q[:, 0], sk[:, 0], sv[:, 0], wk[:, 0], wv[:, 0],
                         _interleaved(c_win_k), _interleaved(c_win_v), _interleaved(c_sel_k), _interleaved(c_sel_v),
                         o_cmp, gl_pad, past)
    feats = (ret_out.reshape(n * l, RET_W), nsa.reshape(n * l, NSA_W))
    kw = jnp.concatenate([c_win_k, wk], 1)
    vw = jnp.concatenate([c_win_v, wv], 1)
    return feats, (ck, cv, sk, sv, kw[:, l:], vw[:, l:], s_new)


def _route_params(w_group, b_group, w_expert, b_expert):
    w = jnp.concatenate([w_group, w_expert], axis=1)
    b = jnp.concatenate([b_group, b_expert], axis=0)
    pad = LANES - w.shape[1]
    return jnp.pad(w, ((0, 0), (0, pad))).astype(BF16), jnp.pad(b, (0, pad)).reshape(1, LANES)


def _dispatch_plan(route, counts_tile):
    counts = counts_tile[0, EXPERT_LANE0:EXPERT_LANE0 + N_EXPERTS].astype(jnp.int32)
    padded = (counts + MOE_BM - 1) // MOE_BM * MOE_BM
    pad_end = jnp.cumsum(padded)
    pad_start = pad_end - padded
    n_asg = route.shape[0] * EXPERT_TOPK
    n_blk = -(-(n_asg + N_EXPERTS * (MOE_BM - 1)) // MOE_BM)
    blk_first = jnp.arange(n_blk, dtype=jnp.int32) * MOE_BM
    blk_e = jnp.minimum(jnp.sum(pad_end[None, :] <= blk_first[:, None], axis=1), N_EXPERTS - 1).astype(jnp.int32)
    n_used = (pad_end[-1:] // MOE_BM).astype(jnp.int32)
    eid = route[:, R_EID:R_EID + EXPERT_TOPK].astype(jnp.int32)
    rank = route[:, R_RANK:R_RANK + EXPERT_TOPK].astype(jnp.int32)
    start = jnp.sum(jnp.where(eid[..., None] == jnp.arange(N_EXPERTS), pad_start, 0), axis=-1)
    return (start + rank).reshape(-1).astype(jnp.int32), blk_e, n_used


def kernel(x_prompt, x_sample, cache_cmp_k, cache_cmp_v, cache_sel_k, cache_sel_v, cache_win_k, cache_win_v,
           state_ret, page_table, w_in, w_cmp_k, w_cmp_v, ret_gn_g, w_o, ln1_g, ln1_b, w_group, b_group,
           w_expert, b_expert, w_gate, w_up, w_down, ln2_g, ln2_b):
    win_buf = cache_win_k.shape[2]
    hp, hs = x_prompt, x_sample
    acc_p = [[] for _ in range(7)]
    acc_s = [[] for _ in range(7)]
    for l in range(DEPTH):
        fp, st_p = _prompt_mixer(hp, win_buf, w_in[l], w_cmp_k[l], w_cmp_v[l], ret_gn_g[l])
        fs, st_s = _sample_mixer(hs, cache_cmp_k[l], cache_cmp_v[l], cache_sel_k[l], cache_sel_v[l],
                                 cache_win_k[l], cache_win_v[l], state_ret[l], page_table,
                                 w_in[l], w_cmp_k[l], w_cmp_v[l], ret_gn_g[l])
        w_o_bf16 = w_o[l].astype(BF16)
        w_route, b_route = _route_params(w_group[l], b_group[l], w_expert[l], b_expert[l])
        tp = hp.shape[0] * hp.shape[1]
        ts = hs.shape[0] * hs.shape[1]
        no_counts = jnp.zeros((SUBLANES, LANES), F32)
        h1p, route_p, counts_p = _wo_ln_route(fp[0], fp[1], hp.reshape(tp, D_MODEL), w_o_bf16, ln1_g[l], ln1_b[l],
                                              w_route, b_route, no_counts)
        h1s, route_s, counts = _wo_ln_route(fs[0], fs[1], hs.reshape(ts, D_MODEL), w_o_bf16, ln1_g[l], ln1_b[l],
                                            w_route, b_route, counts_p)
        h1 = jnp.concatenate([h1p, h1s], axis=0)
        route = jnp.concatenate([route_p, route_s], axis=0)
        slot, blk_e, n_used = _dispatch_plan(route, counts)
        plane = -(-(tp + ts) // WO_TM) * WO_TM
        y = _expert_ffn(h1, slot, blk_e, n_used, w_gate[l], w_up[l], w_down[l], plane)
        hp = _moe_ln(h1, y, route, ln2_g[l], ln2_b[l], 0, tp, plane).reshape(hp.shape)
        hs = _moe_ln(h1, y, route, ln2_g[l], ln2_b[l], tp, ts, plane).reshape(hs.shape)
        for acc, t in zip(acc_p, st_p):
            acc.append(t)
        for acc, t in zip(acc_s, st_s):
            acc.append(t)
    p_cmp_k, p_cmp_v, p_sel_k, p_sel_v, p_win_k, p_win_v, p_ret = [jnp.stack(a) for a in acc_p]
    s_cmp_k, s_cmp_v, s_sel_k, s_sel_v, s_win_k, s_win_v, s_ret = [jnp.stack(a) for a in acc_s]
    return (hp, hs, p_cmp_k, p_cmp_v, p_sel_k, p_sel_v, p_win_k, p_win_v, p_ret.astype(state_ret.dtype),
            s_cmp_k, s_cmp_v, s_sel_k, s_sel_v, s_win_k, s_win_v, s_ret.astype(state_ret.dtype))
```

```python
import functools
import math

import jax
import jax.numpy as jnp
import numpy as np
from jax import lax
from jax.experimental import pallas as pl
from jax.experimental.pallas import tpu as pltpu

D_MODEL = 2048
DEPTH = 1
PAGE_SIZE = 128

F32 = jnp.float32
BF16 = jnp.bfloat16
HEAD_DIM = 128
RET_HEADS = D_MODEL // (2 * HEAD_DIM)
NSA_HEADS = D_MODEL // (2 * HEAD_DIM)
NSA_KV_HEADS = 2
RET_W = RET_HEADS * HEAD_DIM
NSA_W = NSA_HEADS * HEAD_DIM
KV_W = NSA_KV_HEADS * HEAD_DIM
MIX_W = RET_W + NSA_W
RET_CHUNK = 128
RET_ROPE_THETA = 10000.0
ROPE_THETA = 500000.0
ROT_DIM = HEAD_DIM // 4
CMP_BLOCK = 64
SEL_TOPK = 16
WINDOW = 512
WIN_Q_BLOCK = 128
SEL_Q_BLOCK = 64
N_GROUPS = 4
EXPERTS_PER_GROUP = 8
N_EXPERTS = N_GROUPS * EXPERTS_PER_GROUP
EXPERT_TOPK = 2
D_EXPERT = 512
MOE_BLOCK = 128
LN_EPS = 1e-5
GN_EPS = 1e-5
NEG = -1e30
DEEPNORM_ALPHA = (2 * DEPTH) ** 0.25
DEEPNORM_BETA = (8 * DEPTH) ** -0.25
SPLITS = (RET_W, RET_W, RET_W, RET_W, NSA_W, KV_W, KV_W, KV_W, KV_W, KV_W, KV_W, NSA_HEADS * 3)
IN_W = sum(SPLITS)
GATE_W = NSA_HEADS * 3
MAIN_W = IN_W - GATE_W
LANES = 128
VMEM_LIMIT = 48 * 1024 * 1024


PROJ_TN = 512
GL_COL0 = MAIN_W
PROJ_W = -(-(MAIN_W + NSA_KV_HEADS * LANES) // PROJ_TN) * PROJ_TN
ROTATED_SPLITS = (4, 5, 7, 9)


def _rotated_heads():
    cuts = np.cumsum((0,) + SPLITS)
    tiles = []
    for j in range(PROJ_W // PROJ_TN):
        heads = []
        for h in range(PROJ_TN // HEAD_DIM):
            c0 = j * PROJ_TN + h * HEAD_DIM
            split = int(np.searchsorted(cuts, c0, side="right")) - 1
            heads.append(split in ROTATED_SPLITS and c0 < MAIN_W)
        tiles.append(tuple(heads))
    return tiles


def _nsa_rope_tables(pos):
    half = ROT_DIM // 2
    inv = ROPE_THETA ** (-jnp.arange(0, ROT_DIM, 2, dtype=F32) / ROT_DIM)
    ang = pos[:, None].astype(F32) * inv
    cos, sin = jnp.cos(ang), jnp.sin(ang)
    rest = HEAD_DIM - ROT_DIM
    zeros = jnp.zeros((pos.shape[0], half), F32)
    pad = lambda t, fill: jnp.pad(t, ((0, 0), (0, rest)), constant_values=fill)
    return (pad(jnp.concatenate([cos, cos], 1), 1.0), pad(jnp.concatenate([zeros, sin], 1), 0.0),
            pad(jnp.concatenate([-sin, zeros], 1), 0.0))


def _proj_kernel(x_ref, w_ref, wt_ref, cos_ref, up_ref, dn_ref, o_ref, *, patterns):
    j = pl.program_id(1)
    n_main = MAIN_W // PROJ_TN
    xb = x_ref[...].astype(BF16)
    half = ROT_DIM // 2

    def rotated(acc, heads):
        cos, up, dn = cos_ref[...], up_ref[...], dn_ref[...]
        parts = []
        for h, rot in enumerate(heads):
            xh = acc[:, h * HEAD_DIM:(h + 1) * HEAD_DIM]
            if rot:
                xh = xh * cos + pltpu.roll(xh, half, 1) * up + pltpu.roll(xh, HEAD_DIM - half, 1) * dn
            parts.append(xh)
        return jnp.concatenate(parts, axis=1)

    for heads in sorted(set(patterns[:n_main])):
        tiles = [t for t, p in enumerate(patterns[:n_main]) if p == heads]
        hit = functools.reduce(jnp.logical_or, [j == t for t in tiles])

        @pl.when(hit)
        def _(heads=heads):
            acc = jnp.dot(xb, w_ref[...], preferred_element_type=F32)
            o_ref[...] = rotated(acc, heads) if any(heads) else acc

    @pl.when(j >= n_main)
    def _():
        o_ref[...] = jnp.dot(xb, wt_ref[...], preferred_element_type=F32)


def _project(x, w_in, pos):
    n, l, d = x.shape
    t = n * l
    xt = x.reshape(t, d)
    tm = min(1024, t)
    gpg = GATE_W // NSA_KV_HEADS
    n_main = MAIN_W // PROJ_TN
    assert PROJ_W == MAIN_W + PROJ_TN and not any(any(p) for p in _rotated_heads()[n_main:])
    gate_tiles = [jnp.pad(w_in[:, MAIN_W + k * gpg:MAIN_W + (k + 1) * gpg], ((0, 0), (0, LANES - gpg)))
                  for k in range(NSA_KV_HEADS)]
    fill = jnp.zeros((d, PROJ_TN - NSA_KV_HEADS * LANES), F32)
    w_tail = jnp.concatenate(gate_tiles + [fill], axis=1).astype(BF16)
    tables = _nsa_rope_tables(pos)
    pb = pos.shape[0] // tm
    tab = pl.BlockSpec((tm, HEAD_DIM), lambda i, j: (i % pb, 0))
    return pl.pallas_call(
        functools.partial(_proj_kernel, patterns=_rotated_heads()),
        out_shape=jax.ShapeDtypeStruct((t, PROJ_W), F32),
        grid=(t // tm, PROJ_W // PROJ_TN),
        in_specs=[pl.BlockSpec((tm, d), lambda i, j: (i, 0)),
                  pl.BlockSpec((d, PROJ_TN), lambda i, j: (0, jnp.minimum(j, n_main - 1))),
                  pl.BlockSpec((d, PROJ_TN), lambda i, j: (0, 0)), tab, tab, tab],
        out_specs=pl.BlockSpec((tm, PROJ_TN), lambda i, j: (i, j)),
        compiler_params=pltpu.CompilerParams(dimension_semantics=("arbitrary", "arbitrary"),
                                             vmem_limit_bytes=VMEM_LIMIT),
        name="in_proj",
    )(xt, w_in.astype(BF16), w_tail, *tables)


def _split_main(main, n, l, first=0):
    cuts = [0] + [int(c) for c in np.cumsum(SPLITS)[:-1]]
    return [main[:, cuts[i]:cuts[i + 1]].reshape(n, l, -1) for i in range(first, len(SPLITS) - 1)]


def _split_col0(i):
    return int(np.cumsum((0,) + SPLITS)[i])


def _gate_logits(main, n, l):
    gpg = GATE_W // NSA_KV_HEADS
    cols = [main[:, GL_COL0 + k * LANES:GL_COL0 + k * LANES + gpg] for k in range(NSA_KV_HEADS)]
    return jnp.concatenate(cols, axis=1).reshape(n, l, GATE_W)


def _compress_kernel(k_ref, v_ref, wk_ref, wv_ref, ko_ref, vo_ref):
    r = ko_ref.shape[0]
    ko_ref[...] = jnp.sum(k_ref[...].reshape(r, CMP_BLOCK, KV_W) * wk_ref[...][None], axis=1)
    vo_ref[...] = jnp.sum(v_ref[...].reshape(r, CMP_BLOCK, KV_W) * wv_ref[...][None], axis=1)


def _compress_prompt(main, w_cmp_k, w_cmp_v):
    t = main.shape[0]
    r = 32
    wk2 = jnp.tile(w_cmp_k, (1, NSA_KV_HEADS))
    wv2 = jnp.tile(w_cmp_v, (1, NSA_KV_HEADS))
    kcol, vcol = _split_col0(5) // KV_W, _split_col0(6) // KV_W
    wsp = pl.BlockSpec((CMP_BLOCK, KV_W), lambda i: (0, 0))
    osp = pl.BlockSpec((r, KV_W), lambda i: (i, 0))
    return pl.pallas_call(
        _compress_kernel,
        out_shape=[jax.ShapeDtypeStruct((t // CMP_BLOCK, KV_W), F32)] * 2,
        grid=(t // (r * CMP_BLOCK),),
        in_specs=[pl.BlockSpec((r * CMP_BLOCK, KV_W), lambda i: (i, kcol)),
                  pl.BlockSpec((r * CMP_BLOCK, KV_W), lambda i: (i, vcol)), wsp, wsp],
        out_specs=[osp, osp],
        compiler_params=pltpu.CompilerParams(dimension_semantics=("arbitrary",)),
        name="compress_prompt",
    )(main, main, wk2, wv2)


PAGES_PER_STEP = 16
BLOCKS_PER_PAGE = PAGE_SIZE // CMP_BLOCK


SUBLANES = 8
ROWS_PER_BLOCK = CMP_BLOCK * NSA_KV_HEADS
ROWS_PER_PAGE = PAGE_SIZE * NSA_KV_HEADS
BLOCKS_PER_TILE = SUBLANES // NSA_KV_HEADS


def _div(x, n):
    assert n & (n - 1) == 0
    return jnp.right_shift(x, n.bit_length() - 1)


def _mod(x, n):
    assert n & (n - 1) == 0
    return jnp.bitwise_and(x, n - 1)


def _interleaved(pool):
    return pool.reshape(pool.shape[:-3] + (pool.shape[-3] * NSA_KV_HEADS, HEAD_DIM))


def _compress_paged_kernel(pt_ref, pk_ref, pv_ref, wk_ref, wv_ref, ko_ref, vo_ref, kbuf, vbuf, sem):
    s, i = pl.program_id(0), pl.program_id(1)
    n_i = pl.num_programs(1)
    step = s * n_i + i
    last = pl.num_programs(0) * n_i - 1
    pair = _div(lax.broadcasted_iota(jnp.int32, (SUBLANES, HEAD_DIM), 0), NSA_KV_HEADS)

    def fetch(seq, blk, slot):
        for j in range(PAGES_PER_STEP):
            page = pt_ref[seq, blk * PAGES_PER_STEP + j]
            pltpu.make_async_copy(pk_ref.at[page], kbuf.at[slot, j], sem.at[slot]).start(priority=j % 2)
            pltpu.make_async_copy(pv_ref.at[page], vbuf.at[slot, j], sem.at[slot]).start(priority=(j + 1) % 2)

    @pl.when(step == 0)
    def _():
        fetch(0, 0, 0)

    slot = jnp.bitwise_and(step, 1)

    @pl.when(step < last)
    def _():
        wrap = i == n_i - 1
        fetch(jnp.where(wrap, s + 1, s), jnp.where(wrap, 0, i + 1), 1 - slot)

    pltpu.make_async_copy(kbuf.at[slot], kbuf.at[slot], sem.at[slot]).wait()
    pltpu.make_async_copy(vbuf.at[slot], vbuf.at[slot], sem.at[slot]).wait()

    def summaries(buf, w):
        sums = []
        for j in range(PAGES_PER_STEP):
            for b in range(BLOCKS_PER_PAGE):
                y = buf[slot, j, b * ROWS_PER_BLOCK:(b + 1) * ROWS_PER_BLOCK, :] * w
                acc = jnp.sum(y.reshape(ROWS_PER_BLOCK // SUBLANES, SUBLANES, HEAD_DIM), axis=0)
                shift = SUBLANES // 2
                while shift >= NSA_KV_HEADS:
                    acc = acc + pltpu.roll(acc, shift, 0)
                    shift //= 2
                sums.append(acc)
        tiles = []
        for t in range(len(sums) // BLOCKS_PER_TILE):
            tile = sums[t * BLOCKS_PER_TILE]
            for j in range(1, BLOCKS_PER_TILE):
                tile = jnp.where(pair == j, sums[t * BLOCKS_PER_TILE + j], tile)
            tiles.append(tile)
        return jnp.concatenate(tiles, axis=0)

    ko_ref[0] = summaries(kbuf, wk_ref[...])
    vo_ref[0] = summaries(vbuf, wv_ref[...])


def _compress_paged(pool_k, pool_v, page_table, w_cmp_k, w_cmp_v):
    n, n_pages = page_table.shape
    wk2 = jnp.repeat(w_cmp_k, NSA_KV_HEADS, axis=0)
    wv2 = jnp.repeat(w_cmp_v, NSA_KV_HEADS, axis=0)
    anywhere = pl.BlockSpec(memory_space=pl.ANY)
    wsp = pl.BlockSpec((ROWS_PER_BLOCK, HEAD_DIM), lambda s, i, pt: (0, 0))
    rows = PAGES_PER_STEP * BLOCKS_PER_PAGE * NSA_KV_HEADS
    osp = pl.BlockSpec((1, rows, HEAD_DIM), lambda s, i, pt: (s, i, 0))
    page_buf = pltpu.VMEM((2, PAGES_PER_STEP, ROWS_PER_PAGE, HEAD_DIM), F32)
    return pl.pallas_call(
        _compress_paged_kernel,
        out_shape=[jax.ShapeDtypeStruct((n, n_pages * BLOCKS_PER_PAGE * NSA_KV_HEADS, HEAD_DIM), F32)] * 2,
        grid_spec=pltpu.PrefetchScalarGridSpec(
            num_scalar_prefetch=1,
            grid=(n, n_pages // PAGES_PER_STEP),
            in_specs=[anywhere, anywhere, wsp, wsp],
            out_specs=[osp, osp],
            scratch_shapes=[page_buf, page_buf, pltpu.SemaphoreType.DMA((2,))]),
        compiler_params=pltpu.CompilerParams(dimension_semantics=("arbitrary", "arbitrary")),
        name="compress_paged",
    )(page_table, _interleaved(pool_k), _interleaved(pool_v), wk2, wv2)


SS_SEQ = 8
LOWEST = -3.0e38


def _sample_select_kernel(q_ref, kc_ref, vc_ref, ocmp_ref, sel_ref, *, pos):
    ss, nbk = q_ref.shape[0], kc_ref.shape[1]
    kv, g = NSA_KV_HEADS, NSA_GROUP
    head = lax.broadcasted_iota(jnp.int32, (NSA_HEADS, nbk), 0)
    col = lax.broadcasted_iota(jnp.int32, (NSA_HEADS, nbk), 1)
    m = (_mod(col, kv) == _div(head, g)) & ((_div(col, kv) + 1) * CMP_BLOCK - 1 <= pos)
    width = nbk + LANES
    rows = []
    for i in range(ss):
        q = (q_ref[i] * (HEAD_DIM ** -0.5)).astype(BF16)
        p = _masked_softmax(_dot_nt(q, kc_ref[i].astype(BF16)), m)
        ocmp_ref[i] = jnp.dot(p.astype(BF16), vc_ref[i].astype(BF16), preferred_element_type=F32)
        for k in range(kv):
            imp = jnp.sum(p[k * g:(k + 1) * g], axis=0, keepdims=True)
            rows.append(jnp.concatenate([imp, jnp.zeros((1, LANES), F32)], axis=1))
    nrow = ss * kv
    r_iota = lax.broadcasted_iota(jnp.int32, (nrow, width), 0)
    ccol = lax.broadcasted_iota(jnp.int32, (nrow, width), 1)
    cand = jnp.zeros((nrow, width), F32)
    for r, row in enumerate(rows):
        cand = jnp.where(r_iota == r, row, cand)
    cblk = _div(ccol, kv)
    cur = pos // CMP_BLOCK
    n_sel = -(-(pos + 1) // CMP_BLOCK)
    forced = (cblk == 0) | (cblk == cur) | (cblk == cur - 1)
    score = jnp.where(cblk > cur, NEG, jnp.where(forced, -NEG, cand))
    score = jnp.where((_mod(ccol, kv) == _mod(r_iota, kv)) & (cblk < n_sel), score, LOWEST)
    colf = ccol.astype(F32)
    lane = lax.broadcasted_iota(jnp.int32, (nrow, LANES), 1)
    sel = jnp.zeros((nrow, LANES), jnp.int32)
    for t in range(SEL_TOPK):
        mx = jnp.max(score, axis=1, keepdims=True)
        c = jnp.min(jnp.where(score == mx, colf, -LOWEST), axis=1, keepdims=True)
        picked = jnp.where(mx > 0.5 * NEG, _div(c.astype(jnp.int32), kv), -1)
        sel = jnp.where(lane == t, picked, sel)
        score = jnp.where(colf == c, LOWEST, score)
    sel_ref[...] = sel


def _sample_select(q, kc, vc, pos):
    n = q.shape[0]
    nbk = kc.shape[1]
    qsp = pl.BlockSpec((SS_SEQ, NSA_HEADS, HEAD_DIM), lambda i: (i, 0, 0))
    csp = pl.BlockSpec((SS_SEQ, nbk, HEAD_DIM), lambda i: (i, 0, 0))
    return pl.pallas_call(
        functools.partial(_sample_select_kernel, pos=pos),
        out_shape=[jax.ShapeDtypeStruct((n, NSA_HEADS, HEAD_DIM), F32),
                   jax.ShapeDtypeStruct((n * NSA_KV_HEADS, LANES), jnp.int32)],
        grid=(n // SS_SEQ,),
        in_specs=[qsp, csp, csp],
        out_specs=[qsp, pl.BlockSpec((SS_SEQ * NSA_KV_HEADS, LANES), lambda i: (i, 0))],
        compiler_params=pltpu.CompilerParams(dimension_semantics=("arbitrary",), vmem_limit_bytes=VMEM_LIMIT),
        name="sample_select",
    )(q, kc, vc)


N_SLOTS = NSA_KV_HEADS * SEL_TOPK


def _sample_attend_kernel(sel_ref, pt_ref, q_ref, knew_ref, vnew_ref, wknew_ref, wvnew_ref, wkb_ref, wvb_ref,
                          ocmp_ref, gl_ref, *rest, pos, past_blocks):
    del pt_ref
    kblk, vblk, o_ref = rest[:N_SLOTS], rest[N_SLOTS:2 * N_SLOTS], rest[2 * N_SLOTS]
    kv, g, rb = NSA_KV_HEADS, NSA_GROUP, ROWS_PER_BLOCK
    n = pl.program_id(0)
    q = q_ref[0] * (HEAD_DIM ** -0.5)
    row = lax.broadcasted_iota(jnp.int32, (rb, HEAD_DIM), 0)
    col = lax.broadcasted_iota(jnp.int32, (1, rb), 1)

    def new_block(ref):
        out = jnp.zeros((rb, HEAD_DIM), F32)
        for k in range(kv):
            out = jnp.where(row == k, ref[0, k:k + 1, :], out)
        return out

    def per_head_rows(ref):
        return jnp.concatenate([jnp.broadcast_to(ref[0, k:k + 1, :], (g, HEAD_DIM)) for k in range(kv)], axis=0)

    knew, vnew = new_block(knew_ref), new_block(vnew_ref)
    o_sel = []
    for k in range(kv):
        ks, vs, ms = [], [], []
        for j in range(SEL_TOPK):
            b = sel_ref[(n * kv + k) * SEL_TOPK + j]
            is_new = jnp.broadcast_to(b, (rb, HEAD_DIM)) >= past_blocks
            ks.append(jnp.where(is_new, knew, kblk[k * SEL_TOPK + j][0]))
            vs.append(jnp.where(is_new, vnew, vblk[k * SEL_TOPK + j][0]))
            first = jnp.where(b >= 0, b, 1 << 24) * CMP_BLOCK
            ms.append((first + _div(col, kv) <= pos) & (_mod(col, kv) == k))
        s = _dot_nt(q[k * g:(k + 1) * g].astype(BF16), jnp.concatenate(ks, axis=0).astype(BF16))
        p = _masked_softmax(s, jnp.concatenate(ms, axis=1))
        o_sel.append(jnp.dot(p.astype(BF16), jnp.concatenate(vs, axis=0).astype(BF16), preferred_element_type=F32))
    o_sel = jnp.concatenate(o_sel, axis=0)

    nwr = wkb_ref.shape[1]
    s_w = _dot_nt(q.astype(BF16), wkb_ref[0].astype(BF16))
    cw = lax.broadcasted_iota(jnp.int32, (NSA_HEADS, nwr), 1)
    hw = lax.broadcasted_iota(jnp.int32, (NSA_HEADS, nwr), 0)
    kpos = pos - nwr // kv + _div(cw, kv)
    mw = (kpos <= pos) & (pos - kpos < WINDOW) & (_mod(cw, kv) == _div(hw, g))
    s_n = jnp.sum(q * per_head_rows(wknew_ref), axis=1, keepdims=True)
    smw = jnp.where(mw, s_w, NEG)
    mx = jnp.maximum(jnp.max(smw, axis=1, keepdims=True), s_n)
    e_w = jnp.where(mw, jnp.exp(smw - mx), 0.0)
    e_n = jnp.exp(s_n - mx)
    den = jnp.sum(e_w, axis=1, keepdims=True) + e_n
    o_win = (jnp.dot(e_w.astype(BF16), wvb_ref[0].astype(BF16), preferred_element_type=F32)
             + e_n * per_head_rows(wvnew_ref)) / den

    gate = 1.0 / (1.0 + jnp.exp(-gl_ref[0]))
    o_ref[0] = gate[:, 0:1] * ocmp_ref[0] + gate[:, 1:2] * o_sel + gate[:, 2:3] * o_win


def _sample_attend(sel, page_table, q, sk, sv, wk, wv, win_k, win_v, pool_k, pool_v, o_cmp, gl, pos):
    n, n_pages = page_table.shape
    past_blocks = n_pages * BLOCKS_PER_PAGE

    def slot_spec(k, j):
        def imap(s, sel_r, pt_r):
            b = jnp.maximum(sel_r[(s * NSA_KV_HEADS + k) * SEL_TOPK + j], 0)
            page = jnp.minimum(_div(b, BLOCKS_PER_PAGE), n_pages - 1)
            return (pt_r[s * n_pages + page], _mod(b, BLOCKS_PER_PAGE), 0)
        return pl.BlockSpec((1, ROWS_PER_BLOCK, HEAD_DIM), imap)

    slots = [slot_spec(k, j) for k in range(NSA_KV_HEADS) for j in range(SEL_TOPK)]
    per_seq = lambda a: pl.BlockSpec((1,) + a.shape[1:], lambda s, sel_r, pt_r: (s, 0, 0))
    dense = [q, sk, sv, wk, wv, win_k, win_v, o_cmp, gl]
    return pl.pallas_call(
        functools.partial(_sample_attend_kernel, pos=pos, past_blocks=past_blocks),
        out_shape=jax.ShapeDtypeStruct((n, NSA_HEADS, HEAD_DIM), F32),
        grid_spec=pltpu.PrefetchScalarGridSpec(
            num_scalar_prefetch=2,
            grid=(n,),
            in_specs=[per_seq(a) for a in dense] + slots + slots,
            out_specs=pl.BlockSpec((1, NSA_HEADS, HEAD_DIM), lambda s, sel_r, pt_r: (s, 0, 0))),
        compiler_params=pltpu.CompilerParams(dimension_semantics=("arbitrary",), vmem_limit_bytes=VMEM_LIMIT),
        name="sample_attend",
    )(sel, page_table.reshape(-1), *dense, *([pool_k] * N_SLOTS), *([pool_v] * N_SLOTS))


def _retention_tables(seq):
    c = RET_CHUNK
    log_g = jnp.log1p(-jnp.exp2(-5.0 - jnp.arange(RET_HEADS, dtype=F32)))
    i = jnp.arange(c, dtype=F32)
    rel = i[:, None] - i[None, :]
    dmask = jnp.where(rel[None] >= 0, jnp.exp(jnp.maximum(rel[None], 0.0) * log_g[:, None, None]), 0.0)
    q_dec = jnp.exp((i + 1.0)[None] * log_g[:, None])[..., None]
    k_dec = jnp.exp((c - 1.0 - i)[None] * log_g[:, None])[..., None]
    c_dec = jnp.exp(c * log_g)[:, None, None]
    bc = lambda t: jnp.broadcast_to(t, (RET_HEADS, c, HEAD_DIM))
    inv = RET_ROPE_THETA ** (-jnp.arange(0, HEAD_DIM, 2, dtype=F32) / HEAD_DIM)
    ang = jnp.arange(seq)[:, None].astype(F32) * inv
    cos, sin = jnp.cos(ang), jnp.sin(ang)
    return (dmask, bc(q_dec), bc(k_dec), jnp.broadcast_to(c_dec, (RET_HEADS, 1, HEAD_DIM)),
            jnp.concatenate([cos, cos], -1), jnp.concatenate([-sin, sin], -1))


def _retention_kernel(q_ref, k_ref, v_ref, g_ref, cos_ref, sin_ref, dmask_ref, qdec_ref, kdec_ref, cdec_ref, gn_ref,
                      o_ref, st_ref, s_scr):
    c = pl.program_id(1)

    @pl.when(c == 0)
    def _():
        s_scr[...] = jnp.zeros(s_scr.shape, F32)

    cosf, sins = cos_ref[...], sin_ref[...]
    half = HEAD_DIM // 2
    for h in range(RET_HEADS):
        sl = slice(h * HEAD_DIM, (h + 1) * HEAD_DIM)
        qh, kh = q_ref[:, sl], k_ref[:, sl]
        qr = qh * cosf + pltpu.roll(qh, half, 1) * sins
        kr = (kh * cosf + pltpu.roll(kh, half, 1) * sins) * (HEAD_DIM ** -0.5)
        vb = v_ref[:, sl].astype(BF16)
        att = _dot_nt(qr.astype(BF16), kr.astype(BF16)) * dmask_ref[h]
        s_prev = s_scr[h]
        o = (jnp.dot(att.astype(BF16), vb, preferred_element_type=F32)
             + jnp.dot((qr * qdec_ref[h]).astype(BF16), s_prev.astype(BF16), preferred_element_type=F32))
        s_scr[h] = cdec_ref[h] * s_prev + lax.dot_general(
            (kr * kdec_ref[h]).astype(BF16), vb, (((0,), (0,)), ((), ())), preferred_element_type=F32)
        mu = jnp.mean(o, axis=-1, keepdims=True)
        var = jnp.mean(jnp.square(o - mu), axis=-1, keepdims=True)
        on = (o - mu) * lax.rsqrt(var + GN_EPS) * gn_ref[:, sl]
        gg = g_ref[:, sl]
        o_ref[:, sl] = gg * (1.0 / (1.0 + jnp.exp(-gg))) * on

    @pl.when(c == pl.num_programs(1) - 1)
    def _():
        st_ref[0] = s_scr[...]


def _retention_prompt(main, gn_g, batch, seq):
    nc = seq // RET_CHUNK
    dmask, q_dec, k_dec, c_dec, cosf, sins = _retention_tables(seq)
    col = lambda j: pl.BlockSpec((RET_CHUNK, RET_W), lambda b, c: (b * nc + c, j))
    pos_tab = pl.BlockSpec((RET_CHUNK, HEAD_DIM), lambda b, c: (c, 0))
    full = lambda a: pl.BlockSpec(a.shape, lambda b, c: (0,) * a.ndim)
    gn = gn_g.reshape(1, RET_W)
    return pl.pallas_call(
        _retention_kernel,
        out_shape=[jax.ShapeDtypeStruct((batch * seq, RET_W), F32),
                   jax.ShapeDtypeStruct((batch, RET_HEADS, HEAD_DIM, HEAD_DIM), F32)],
        grid=(batch, nc),
        in_specs=[col(0), col(1), col(2), col(3), pos_tab, pos_tab,
                  full(dmask), full(q_dec), full(k_dec), full(c_dec), full(gn)],
        out_specs=[pl.BlockSpec((RET_CHUNK, RET_W), lambda b, c: (b * nc + c, 0)),
                   pl.BlockSpec((1, RET_HEADS, HEAD_DIM, HEAD_DIM), lambda b, c: (b, 0, 0, 0))],
        scratch_shapes=[pltpu.VMEM((RET_HEADS, HEAD_DIM, HEAD_DIM), F32)],
        compiler_params=pltpu.CompilerParams(dimension_semantics=("arbitrary", "arbitrary"),
                                             vmem_limit_bytes=VMEM_LIMIT),
        name="retention_prompt",
    )(main, main, main, main, cosf, sins, dmask, q_dec, k_dec, c_dec, gn)


NSA_TQ = 128
NSA_KEY_CHUNK = 512
NSA_GROUP = NSA_HEADS // NSA_KV_HEADS


def _dot_nt(a, b):
    return lax.dot_general(a, b, (((1,), (1,)), ((), ())), preferred_element_type=F32)


def _masked_softmax(s, m):
    sm = jnp.where(m, s, NEG)
    e = jnp.exp(sm - jnp.max(sm, axis=-1, keepdims=True))
    return jnp.where(m, e / jnp.sum(e, axis=-1, keepdims=True), 0.0)


def _select_mask_t(imp_t, pos_t):
    nb = imp_t.shape[0]
    blk = lax.broadcasted_iota(jnp.int32, imp_t.shape, 0)
    cur = jnp.right_shift(pos_t, int(math.log2(CMP_BLOCK)))
    forced = (blk == 0) | (blk == cur) | (blk == cur - 1)
    score = jnp.where(blk > cur, NEG, jnp.where(forced, -NEG, imp_t))
    rank = jnp.zeros(score.shape, jnp.int32)
    for i in range(nb):
        si = score[i:i + 1, :]
        ahead = (si > score) | ((si == score) & (blk > i))
        rank = rank + ahead.astype(jnp.int32)
    return (rank < SEL_TOPK) & (score > 0.5 * NEG)


def _nsa_prompt_kernel(q_ref, kc_ref, vc_ref, sk_ref, sv_ref, wk_ref, wv_ref, gl_ref, o_ref):
    tq, g, kc_n = NSA_TQ, NSA_GROUP, NSA_KEY_CHUNK
    qi = pl.program_id(2)
    q4 = q_ref[...] * (HEAD_DIM ** -0.5)
    qs = jnp.concatenate([q4[:, i * HEAD_DIM:(i + 1) * HEAD_DIM] for i in range(g)], axis=0).astype(BF16)
    pos = qi * tq + lax.broadcasted_iota(jnp.int32, (tq, 1), 0)
    pos4 = jnp.concatenate([pos] * g, axis=0)

    nb = kc_ref.shape[1]
    assert tq == LANES
    s_c = _dot_nt(kc_ref[0].astype(BF16), qs)
    blk_t = lax.broadcasted_iota(jnp.int32, (nb, g * tq), 0)
    pos_t = qi * tq + lax.broadcasted_iota(jnp.int32, (1, tq), 1)
    m_c = (blk_t + 1) * CMP_BLOCK - 1 <= jnp.concatenate([pos_t] * g, axis=1)
    sm_c = jnp.where(m_c, s_c, NEG)
    e_c = jnp.exp(sm_c - jnp.max(sm_c, axis=0, keepdims=True))
    p_c = jnp.where(m_c, e_c / jnp.sum(e_c, axis=0, keepdims=True), 0.0)
    o_cmp = lax.dot_general(p_c.astype(BF16), vc_ref[0].astype(BF16), (((0,), (0,)), ((), ())),
                            preferred_element_type=F32)
    imp_t = p_c[:, 0:tq]
    for i in range(1, g):
        imp_t = imp_t + p_c[:, i * tq:(i + 1) * tq]

    seln_t = jnp.where(_select_mask_t(imp_t, pos_t), 0.0, NEG)
    seln = jnp.concatenate([seln_t, jnp.zeros((tq - nb, tq), F32)], axis=0).T[:, :nb].astype(BF16)
    q_aug = jnp.concatenate([qs, jnp.concatenate([seln] * g, axis=0)], axis=1)
    key_idx = lax.broadcasted_iota(jnp.int32, (kc_n, nb), 0)
    key_blk = lax.broadcasted_iota(jnp.int32, (kc_n, nb), 1)
    key_row = lax.broadcasted_iota(jnp.int32, (1, kc_n), 1)

    def scores(k0):
        member = jnp.right_shift(key_idx + k0, int(math.log2(CMP_BLOCK))) == key_blk
        k_aug = jnp.concatenate([sk_ref[pl.ds(k0, kc_n), :].astype(BF16),
                                 jnp.where(member, 1.0, 0.0).astype(BF16)], axis=1)
        return _dot_nt(q_aug, k_aug)

    def values(k0):
        return sv_ref[pl.ds(k0, kc_n), :].astype(BF16)

    assert kc_n % tq == 0
    c_diag = _div(qi, kc_n // tq)
    kd = pl.multiple_of(c_diag * kc_n, kc_n)
    s_d = jnp.where(key_row + kd <= pos4, scores(kd), NEG)
    m_d = jnp.max(s_d, axis=-1, keepdims=True)
    p_d = jnp.exp(s_d - m_d)
    first = (m_d, jnp.sum(p_d, axis=-1, keepdims=True),
             jnp.dot(p_d.astype(BF16), values(kd), preferred_element_type=F32))

    def chunk(c, carry):
        m_i, l_i, acc = carry
        k0 = pl.multiple_of(c * kc_n, kc_n)
        s = scores(k0)
        m_new = jnp.maximum(m_i, jnp.max(s, axis=-1, keepdims=True))
        alpha = jnp.exp(m_i - m_new)
        p = jnp.exp(s - m_new)
        l_new = alpha * l_i + jnp.sum(p, axis=-1, keepdims=True)
        return m_new, l_new, alpha * acc + jnp.dot(p.astype(BF16), values(k0), preferred_element_type=F32)

    _, l_f, acc_f = lax.fori_loop(0, c_diag, chunk, first)
    o_sel = acc_f / l_f

    nwk = WINDOW + tq
    kstart = pl.multiple_of(jnp.maximum(qi * tq - WINDOW, 0), tq)
    kpos = kstart + lax.broadcasted_iota(jnp.int32, (1, nwk), 1)
    band = jnp.where((kpos <= pos) & (pos - kpos < WINDOW), 0.0, NEG)
    s_w = _dot_nt(qs, wk_ref[pl.ds(kstart, nwk), :].astype(BF16)) + jnp.concatenate([band] * g, axis=0)
    e_w = jnp.exp(s_w - jnp.max(s_w, axis=-1, keepdims=True))
    o_win = (jnp.dot(e_w.astype(BF16), wv_ref[pl.ds(kstart, nwk), :].astype(BF16), preferred_element_type=F32)
             / jnp.sum(e_w, axis=-1, keepdims=True))

    gate = 1.0 / (1.0 + jnp.exp(-gl_ref[...]))
    for i in range(g):
        rows = slice(i * tq, (i + 1) * tq)
        o_ref[:, i * HEAD_DIM:(i + 1) * HEAD_DIM] = (gate[:, 3 * i:3 * i + 1] * o_cmp[rows]
                                                     + gate[:, 3 * i + 1:3 * i + 2] * o_sel[rows]
                                                     + gate[:, 3 * i + 2:3 * i + 3] * o_win[rows])


def _nsa_prompt(main, kc, vc, batch, seq):
    nq = seq // NSA_TQ
    gw = NSA_GROUP * HEAD_DIM
    qcol, glcol = _split_col0(4) // gw, GL_COL0 // LANES
    seq_spec = lambda split: pl.BlockSpec((seq, HEAD_DIM),
                                          lambda b, k, i: (b, _split_col0(split) // HEAD_DIM + k))
    cmp_spec = pl.BlockSpec((1, seq // CMP_BLOCK, HEAD_DIM), lambda b, k, i: (b, 0, k))
    return pl.pallas_call(
        _nsa_prompt_kernel,
        out_shape=jax.ShapeDtypeStruct((batch * seq, NSA_W), F32),
        grid=(batch, NSA_KV_HEADS, nq),
        in_specs=[pl.BlockSpec((NSA_TQ, gw), lambda b, k, i: (b * nq + i, qcol + k)), cmp_spec, cmp_spec,
                  seq_spec(7), seq_spec(8), seq_spec(9), seq_spec(10),
                  pl.BlockSpec((NSA_TQ, LANES), lambda b, k, i: (b * nq + i, glcol + k))],
        out_specs=pl.BlockSpec((NSA_TQ, gw), lambda b, k, i: (b * nq + i, k)),
        compiler_params=pltpu.CompilerParams(dimension_semantics=("arbitrary",) * 3,
                                             vmem_limit_bytes=VMEM_LIMIT),
        name="nsa_prompt",
    )(main, kc, vc, main, main, main, main, main)


WO_TM = 512


def _layer_norm_rows(y, g, b):
    mu = jnp.mean(y, axis=-1, keepdims=True)
    var = jnp.mean(jnp.square(y - mu), axis=-1, keepdims=True)
    return (y - mu) * lax.rsqrt(var + LN_EPS) * g + b


EXPERT_LANE0 = N_GROUPS
R_EID, R_RANK, R_GATE = 0, EXPERT_TOPK, 2 * EXPERT_TOPK


def _route(h, wr_ref, br_ref, carry):
    tm = h.shape[0]
    logit = jnp.dot(h.astype(BF16), wr_ref[...], preferred_element_type=F32) + br_ref[...]
    lane = lax.broadcasted_iota(jnp.int32, (tm, LANES), 1)
    lanef = lane.astype(F32)
    first_lane = lambda hit: jnp.min(jnp.where(hit, lanef, float(LANES)), axis=1, keepdims=True)
    is_g = lane < N_GROUPS
    gl = jnp.where(is_g, logit, LOWEST)
    gmx = jnp.max(gl, axis=1, keepdims=True)
    grp = first_lane(gl == gmx)
    p_grp = 1.0 / jnp.sum(jnp.where(is_g, jnp.exp(gl - gmx), 0.0), axis=1, keepdims=True)
    lane_grp = jnp.right_shift(lane - EXPERT_LANE0, int(math.log2(EXPERTS_PER_GROUP)))
    in_grp = lane_grp.astype(F32) == grp
    el = jnp.where(in_grp, logit, LOWEST)
    ee = jnp.where(in_grp, jnp.exp(el - jnp.max(el, axis=1, keepdims=True)), 0.0)
    pe = jnp.where(in_grp, ee / jnp.sum(ee, axis=1, keepdims=True), -1.0)
    p1 = jnp.max(pe, axis=1, keepdims=True)
    l1 = first_lane(pe == p1)
    pe2 = jnp.where(lanef == l1, -1.0, pe)
    p2 = jnp.max(pe2, axis=1, keepdims=True)
    l2 = first_lane(pe2 == p2)
    den = p1 + p2
    o1, o2 = lanef == l1, lanef == l2
    onehot = jnp.where(o1 | o2, 1.0, 0.0)
    r = lax.broadcasted_iota(jnp.int32, (tm, tm), 0)
    c = lax.broadcasted_iota(jnp.int32, (tm, tm), 1)
    earlier = jnp.where(c < r, 1.0, 0.0).astype(BF16)
    prefix = jnp.dot(earlier, onehot.astype(BF16), preferred_element_type=F32) + carry[0:1, :]
    rank1 = jnp.sum(jnp.where(o1, prefix, 0.0), axis=1, keepdims=True)
    rank2 = jnp.sum(jnp.where(o2, prefix, 0.0), axis=1, keepdims=True)
    carry[0:1, :] = carry[0:1, :] + jnp.sum(onehot, axis=0, keepdims=True)
    fields = [l1 - EXPERT_LANE0, l2 - EXPERT_LANE0, rank1, rank2, p_grp * p1 / den, p_grp * p2 / den]
    rec = jnp.zeros((tm, LANES), F32)
    for j, f in enumerate(fields):
        rec = jnp.where(lane == j, f, rec)
    return rec


def _pack_bf16_pairs(x):
    w = x.shape[1] // 2
    return pltpu.pack_elementwise([x[:, :w], x[:, w:]], packed_dtype=jnp.bfloat16)


def _unpack_bf16_pairs(p):
    halves = [pltpu.unpack_elementwise(p, index=i, packed_dtype=jnp.bfloat16, unpacked_dtype=F32) for i in (0, 1)]
    return jnp.concatenate(halves, axis=1)


def _wo_ln_route_kernel(fr_ref, fn_ref, x_ref, w_ref, g_ref, b_ref, wr_ref, br_ref, base_ref,
                        h_ref, hp_ref, route_ref, cnt_ref, carry):
    i = pl.program_id(0)

    @pl.when(i == 0)
    def _():
        carry[...] = base_ref[...]

    y = (DEEPNORM_ALPHA * x_ref[...]
         + jnp.dot(fr_ref[...].astype(BF16), w_ref[0:RET_W, :], preferred_element_type=F32)
         + jnp.dot(fn_ref[...].astype(BF16), w_ref[RET_W:MIX_W, :], preferred_element_type=F32))
    h = _layer_norm_rows(y, g_ref[...], b_ref[...])
    h_ref[...] = h
    hp_ref[...] = _pack_bf16_pairs(h)
    route_ref[...] = _route(h, wr_ref, br_ref, carry)

    @pl.when(i == pl.num_programs(0) - 1)
    def _():
        cnt_ref[...] = carry[...]


def _wo_ln_route(f_ret, f_nsa, x, w_o_bf16, ln_g, ln_b, w_route, b_route, base_counts):
    t, d = x.shape
    tm = min(WO_TM, t)
    assert t % tm == 0
    row = lambda w: pl.BlockSpec((tm, w), lambda i: (i, 0))
    full = lambda a: pl.BlockSpec(a.shape, lambda i: (0,) * a.ndim)
    lg, lb = ln_g.reshape(1, d), ln_b.reshape(1, d)
    return pl.pallas_call(
        _wo_ln_route_kernel,
        out_shape=[jax.ShapeDtypeStruct((t, d), F32), jax.ShapeDtypeStruct((t, d // 2), jnp.uint32),
                   jax.ShapeDtypeStruct((t, LANES), F32), jax.ShapeDtypeStruct((SUBLANES, LANES), F32)],
        grid=(t // tm,),
        in_specs=[row(RET_W), row(NSA_W), row(d), full(w_o_bf16), full(lg), full(lb),
                  full(w_route), full(b_route), full(base_counts)],
        out_specs=[row(d), row(d // 2), row(LANES), pl.BlockSpec((SUBLANES, LANES), lambda i: (0, 0))],
        scratch_shapes=[pltpu.VMEM((SUBLANES, LANES), F32)],
        compiler_params=pltpu.CompilerParams(dimension_semantics=("arbitrary",), vmem_limit_bytes=VMEM_LIMIT),
        name="wo_ln1_route",
    )(f_ret, f_nsa, x, w_o_bf16, lg, lb, w_route, b_route, base_counts)


MOE_BM = 256
MOE_SUB = 64
TABLE_UNROLL = 8


def _expert_kernel(blk_e_ref, n_used_ref, slot_ref, h_ref, wg_ref, wu_ref, wd_ref, y_ref,
                   src_tok, dst_row, xbuf, obuf, gsem, ssem, *, plane, n_rows):
    del blk_e_ref
    bm = xbuf.shape[1] * xbuf.shape[2]
    i = pl.program_id(0)
    n_used = n_used_ref[0]
    n_asg = slot_ref.shape[0]
    dump0 = EXPERT_TOPK * plane
    assert bm & (bm - 1) == 0

    n_sub, sub = xbuf.shape[1], xbuf.shape[2]

    def gather(blk, buf_slot):
        def sub_block(j, carry):
            base = blk * bm + j * sub
            for u in range(sub):
                pltpu.make_async_copy(h_ref.at[pl.ds(src_tok[base + u], 1), :],
                                      xbuf.at[buf_slot, j, pl.ds(u, 1), :], gsem.at[buf_slot]).start(priority=u % 2)
            return carry
        lax.fori_loop(0, n_sub, sub_block, 0)

    def scatter(blk, buf_slot):
        def sub_block(j, carry):
            base = blk * bm + j * sub
            for u in range(sub):
                pltpu.make_async_copy(obuf.at[buf_slot, j, pl.ds(u, 1), :],
                                      y_ref.at[pl.ds(dst_row[base + u], 1), :], ssem.at[buf_slot]).start(priority=u % 2)
            return carry
        lax.fori_loop(0, n_sub, sub_block, 0)

    def wait_block(buf, sem, buf_slot):
        pltpu.make_async_copy(buf.at[buf_slot], buf.at[buf_slot], sem.at[buf_slot]).wait()

    @pl.when(i == 0)
    def _():
        def clear(t, carry):
            for u in range(TABLE_UNROLL):
                r = t * TABLE_UNROLL + u
                src_tok[r] = 0
                dst_row[r] = dump0 + jnp.bitwise_and(r, 2 * bm - 1)
            return carry
        lax.fori_loop(0, n_used_ref[1], clear, 0)

        def fill(t, carry):
            for u in range(TABLE_UNROLL):
                a = t * TABLE_UNROLL + u
                tok = jnp.right_shift(a, 1)
                src_tok[slot_ref[a]] = tok
                dst_row[slot_ref[a]] = jnp.bitwise_and(a, 1) * plane + tok
            return carry
        lax.fori_loop(0, n_used_ref[2], fill, 0)
        gather(0, 0)
        n_tok = h_ref.shape[0]
        tail = plane - n_tok
        assert tail >= 0
        obuf[1] = jnp.zeros(obuf.shape[1:], obuf.dtype)
        spans = [(dump0, 2 * bm)] + ([(k * plane + n_tok, tail) for k in range(EXPERT_TOPK)] if tail else [])
        copies = []
        for first, count in spans:
            for j in range(-(-count // sub)):
                rows = min(sub, count - j * sub)
                copies.append(pltpu.make_async_copy(obuf.at[1, j % n_sub, pl.ds(0, rows), :],
                                                    y_ref.at[pl.ds(first + j * sub, rows), :], ssem.at[1]))
        for cp in copies:
            cp.start()
        for cp in copies:
            cp.wait()

    slot = jnp.bitwise_and(i, 1)

    @pl.when(i + 1 < n_used)
    def _():
        gather(i + 1, 1 - slot)

    @pl.when(i < n_used)
    def _():
        wait_block(xbuf, gsem, slot)

        @pl.when(i >= 2)
        def _():
            wait_block(obuf, ssem, slot)

        xb = _unpack_bf16_pairs(xbuf[slot].reshape(bm, xbuf.shape[3])).astype(BF16)
        hg = jnp.dot(xb, wg_ref[0].astype(BF16), preferred_element_type=F32)
        hu = jnp.dot(xb, wu_ref[0].astype(BF16), preferred_element_type=F32)
        hb = hg * (1.0 / (1.0 + jnp.exp(-hg))) * hu
        yb = jnp.dot(hb.astype(BF16), wd_ref[0].astype(BF16), preferred_element_type=F32)
        obuf[slot] = _pack_bf16_pairs(yb).reshape(obuf.shape[1:])
        scatter(i, slot)

    @pl.when(i == pl.num_programs(0) - 1)
    def _():
        @pl.when(n_used >= 2)
        def _():
            wait_block(obuf, ssem, jnp.bitwise_and(n_used, 1))
        wait_block(obuf, ssem, jnp.bitwise_and(n_used - 1, 1))


def _expert_ffn(h, slot, blk_e, n_used, w_gate, w_up, w_down, plane):
    t, dp = h.shape
    d = 2 * dp
    n_asg = slot.shape[0]
    assert n_asg == t * EXPERT_TOPK and EXPERT_TOPK == 2
    n_blk = -(-(n_asg + N_EXPERTS * (MOE_BM - 1)) // MOE_BM)
    de = w_gate.shape[2]
    assert n_asg % TABLE_UNROLL == 0 and MOE_BM % TABLE_UNROLL == 0
    n_used = jnp.concatenate([n_used, jnp.array([n_blk * MOE_BM // TABLE_UNROLL, n_asg // TABLE_UNROLL], jnp.int32)])
    wspec = lambda shape: pl.BlockSpec((1,) + shape, lambda i, be, nu, sl: (be[i], 0, 0))
    return pl.pallas_call(
        functools.partial(_expert_kernel, plane=plane, n_rows=n_blk * MOE_BM),
        out_shape=jax.ShapeDtypeStruct((EXPERT_TOPK * plane + 2 * MOE_BM, dp), jnp.uint32),
        grid_spec=pltpu.PrefetchScalarGridSpec(
            num_scalar_prefetch=3,
            grid=(n_blk,),
            in_specs=[pl.BlockSpec(memory_space=pl.ANY), wspec((d, de)), wspec((d, de)), wspec((de, d))],
            out_specs=pl.BlockSpec(memory_space=pl.ANY),
            scratch_shapes=[pltpu.SMEM((n_blk * MOE_BM,), jnp.int32), pltpu.SMEM((n_blk * MOE_BM,), jnp.int32),
                            pltpu.VMEM((2, MOE_BM // MOE_SUB, MOE_SUB, dp), jnp.uint32),
                            pltpu.VMEM((2, MOE_BM // MOE_SUB, MOE_SUB, dp), jnp.uint32),
                            pltpu.SemaphoreType.DMA((2,)), pltpu.SemaphoreType.DMA((2,))]),
        compiler_params=pltpu.CompilerParams(dimension_semantics=("arbitrary",), vmem_limit_bytes=VMEM_LIMIT),
        name="expert_ffn",
    )(blk_e, n_used, slot, h, w_gate, w_up, w_down)


def _moe_ln_kernel(h_ref, y0_ref, y1_ref, route_ref, g_ref, b_ref, o_ref):
    rec = route_ref[...]
    y = (DEEPNORM_ALPHA * h_ref[...] + rec[:, R_GATE:R_GATE + 1] * _unpack_bf16_pairs(y0_ref[...])
         + rec[:, R_GATE + 1:R_GATE + 2] * _unpack_bf16_pairs(y1_ref[...]))
    o_ref[...] = _layer_norm_rows(y, g_ref[...], b_ref[...])


def _moe_ln(h, y, route, ln_g, ln_b, row0, plane):
    n_rows, d = h.shape
    tm = min(WO_TM, n_rows)
    assert n_rows % tm == 0 and row0 % tm == 0 and plane % tm == 0 and EXPERT_TOPK == 2
    off = row0 // tm
    row = lambda w, o: pl.BlockSpec((tm, w), lambda i: (i + o, 0))
    vec = pl.BlockSpec((1, d), lambda i: (0, 0))
    return pl.pallas_call(
        _moe_ln_kernel,
        out_shape=jax.ShapeDtypeStruct((n_rows, d), F32),
        grid=(n_rows // tm,),
        in_specs=[row(d, 0), row(d // 2, off), row(d // 2, off + plane // tm), row(LANES, off), vec, vec],
        out_specs=row(d, 0),
        compiler_params=pltpu.CompilerParams(dimension_semantics=("arbitrary",), vmem_limit_bytes=VMEM_LIMIT),
        name="moe_ln2",
    )(h, y, y, route, ln_g.reshape(1, d), ln_b.reshape(1, d))


def _layer_norm(x, g, b):
    xf = x.astype(F32)
    mu = xf.mean(-1, keepdims=True)
    var = jnp.square(xf - mu).mean(-1, keepdims=True)
    return ((xf - mu) * lax.rsqrt(var + LN_EPS) * g + b).astype(x.dtype)


def _rope(x, pos, rot_dim, theta):
    half = rot_dim // 2
    inv = theta ** (-jnp.arange(0, rot_dim, 2, dtype=F32) / rot_dim)
    ang = pos[..., None].astype(F32) * inv
    cos = jnp.cos(ang)[:, :, None, :]
    sin = jnp.sin(ang)[:, :, None, :]
    xr = x[..., :rot_dim].astype(F32)
    x1, x2 = xr[..., :half], xr[..., half:]
    rot = jnp.concatenate([x1 * cos - x2 * sin, x2 * cos + x1 * sin], -1).astype(x.dtype)
    return jnp.concatenate([rot, x[..., rot_dim:]], -1)


def _heads(t, n):
    return t.reshape(t.shape[0], t.shape[1], n, HEAD_DIM)


def _chunk_retention(q, k, v, s0):
    n, l, h, d = q.shape
    c = RET_CHUNK if l % RET_CHUNK == 0 else l
    nc = l // c
    log_g = jnp.log1p(-jnp.exp2(-5.0 - jnp.arange(h, dtype=F32)))
    i = jnp.arange(c, dtype=F32)
    rel = i[:, None] - i[None, :]
    dmask = jnp.where(rel[None] >= 0, jnp.exp(jnp.maximum(rel[None], 0.0) * log_g[:, None, None]), 0.0)
    q_dec = jnp.exp((i + 1.0)[None] * log_g[:, None])[..., None]
    k_dec = jnp.exp((c - 1.0 - i)[None] * log_g[:, None])[..., None]
    c_dec = jnp.exp(c * log_g)[:, None, None]

    def to_chunks(t):
        return t.astype(F32).reshape(n, nc, c, h, d).transpose(1, 0, 3, 2, 4)

    def step(s, qkv):
        qc, kc, vc = qkv
        att = jnp.einsum('bhid,bhjd->bhij', qc, kc) * dmask
        o = jnp.einsum('bhij,bhjd->bhid', att, vc) + jnp.einsum('bhid,bhde->bhie', qc * q_dec, s)
        s = c_dec * s + jnp.einsum('bhjd,bhje->bhde', kc * k_dec, vc)
        return s, o

    s, o = lax.scan(step, s0.astype(F32), (to_chunks(q), to_chunks(k), to_chunks(v)))
    return o.transpose(1, 0, 3, 2, 4).reshape(n, l, h, d), s


def _retention_group(rq, rk, rv, rg, pos, s0, gn_g):
    q = _rope(_heads(rq, RET_HEADS), pos, HEAD_DIM, RET_ROPE_THETA)
    k = _rope(_heads(rk, RET_HEADS), pos, HEAD_DIM, RET_ROPE_THETA) * (HEAD_DIM ** -0.5)
    v = _heads(rv, RET_HEADS)
    o, s = _chunk_retention(q, k, v, s0)
    mu = o.mean(-1, keepdims=True)
    var = jnp.square(o - mu).mean(-1, keepdims=True)
    on = (o - mu) * lax.rsqrt(var + GN_EPS) * gn_g.reshape(RET_HEADS, HEAD_DIM).astype(F32)
    out = jax.nn.silu(rg.astype(F32)) * on.reshape(rg.shape)
    return out.astype(rq.dtype), s


def _gqa_attend(q, k, v, mask):
    n, lq, h, d = q.shape
    kv = k.shape[2]
    qg = q.reshape(n, lq, kv, h // kv, d)
    s = jnp.einsum('nqkgd,nskd->nkgqs', qg, k).astype(F32) * (d ** -0.5)
    m = mask[:, None, None]
    p = jax.nn.softmax(jnp.where(m, s, NEG), axis=-1) * m
    o = jnp.einsum('nkgqs,nskd->nqkgd', p.astype(v.dtype), v)
    return o.reshape(n, lq, h, d), p


def _nsa_heads(nq, ck, sk, wk, cv, sv, wv, pos):
    rp = lambda t, nh: _rope(_heads(t, nh), pos, ROT_DIM, ROPE_THETA)
    return (rp(nq, NSA_HEADS), rp(ck, NSA_KV_HEADS), rp(sk, NSA_KV_HEADS), rp(wk, NSA_KV_HEADS),
            _heads(cv, NSA_KV_HEADS), _heads(sv, NSA_KV_HEADS), _heads(wv, NSA_KV_HEADS))


def _compress(rows, w):
    n, t, kv, d = rows.shape
    return jnp.einsum('nbjkd,jd->nbkd', rows.reshape(n, t // CMP_BLOCK, CMP_BLOCK, kv, d), w)


def _cmp_branch(q, pos, kc, vc):
    nb = kc.shape[1]
    blk_end = (jnp.arange(nb) + 1) * CMP_BLOCK - 1
    mask = blk_end[None, None, :] <= pos[:, :, None]
    o, p = _gqa_attend(q, kc, vc, mask)
    imp = p.sum(axis=2).transpose(0, 2, 1, 3)
    return o, imp


def _select_blocks(imp, pos, n_sel):
    nb = imp.shape[-1]
    imp = jnp.pad(imp, ((0, 0), (0, 0), (0, 0), (0, n_sel - nb)))
    blk = jnp.arange(n_sel)
    cur = (pos // CMP_BLOCK)[:, :, None, None]
    forced = (blk == 0) | (blk == cur) | (blk == cur - 1)
    score = jnp.where(blk > cur, NEG, jnp.where(forced, -NEG, imp))
    top, idx = lax.top_k(score, min(SEL_TOPK, n_sel))
    return idx, top > 0.5 * NEG


def _sel_attend(q, pos, ks, vs, idx, valid):
    n, lq, kv, kk, cb, d = ks.shape
    h = q.shape[2]
    kpos = idx[..., None] * CMP_BLOCK + jnp.arange(CMP_BLOCK)
    m = ((kpos <= pos[:, :, None, None, None]) & valid[..., None]).reshape(n, lq, kv, 1, kk * cb)
    qg = q.reshape(n, lq, kv, h // kv, d)
    kf = ks.reshape(n, lq, kv, kk * cb, d)
    vf = vs.reshape(n, lq, kv, kk * cb, d)
    s = jnp.einsum('nqkgd,nqkjd->nqkgj', qg, kf).astype(F32) * (d ** -0.5)
    p = jax.nn.softmax(jnp.where(m, s, NEG), axis=-1) * m
    o = jnp.einsum('nqkgj,nqkjd->nqkgd', p.astype(vf.dtype), vf)
    return o.reshape(n, lq, h, d)


def _sel_prompt(q, pos, k, v, idx, valid):
    b, s, h, d = q.shape
    kv = k.shape[2]
    nb = s // CMP_BLOCK
    nq = s // SEL_Q_BLOCK
    kb = k.reshape(b, nb, CMP_BLOCK, kv, d).transpose(0, 3, 1, 2, 4)
    vb = v.reshape(b, nb, CMP_BLOCK, kv, d).transpose(0, 3, 1, 2, 4)
    bi = jnp.arange(b)[:, None, None, None]
    hi = jnp.arange(kv)[None, None, :, None]

    def blockwise(t):
        return t.reshape(t.shape[0], nq, SEL_Q_BLOCK, *t.shape[2:]).swapaxes(0, 1)

    def one(args):
        qc, pc, ic, vc = args
        return _sel_attend(qc, pc, kb[bi, hi, ic], vb[bi, hi, ic], ic, vc)

    o = lax.map(one, (blockwise(q), blockwise(pos), blockwise(idx), blockwise(valid)))
    return o.swapaxes(0, 1).reshape(b, s, h, d)


def _win_prompt(q, k, v):
    b, s, h, d = q.shape
    kv = k.shape[2]
    nb = s // WIN_Q_BLOCK
    nprev = WINDOW // WIN_Q_BLOCK
    nw = nprev + 1
    padw = ((0, 0), (WINDOW, 0), (0, 0), (0, 0))
    kp = jnp.pad(k, padw).reshape(b, nb + nprev, WIN_Q_BLOCK, kv, d)
    vp = jnp.pad(v, padw).reshape(b, nb + nprev, WIN_Q_BLOCK, kv, d)
    kw = jnp.concatenate([kp[:, i:i + nb] for i in range(nw)], axis=2)
    vw = jnp.concatenate([vp[:, i:i + nb] for i in range(nw)], axis=2)
    qpos = jnp.arange(s).reshape(nb, WIN_Q_BLOCK)
    kpos = (jnp.arange(nb)[:, None] - nprev) * WIN_Q_BLOCK + jnp.arange(nw * WIN_Q_BLOCK)[None]
    qq, kk = qpos[:, :, None], kpos[:, None, :]
    mask = (kk <= qq) & (qq - kk < WINDOW) & (kk >= 0)
    mask = jnp.broadcast_to(mask[None], (b,) + mask.shape).reshape(b * nb, WIN_Q_BLOCK, nw * WIN_Q_BLOCK)
    o, _ = _gqa_attend(q.reshape(b * nb, WIN_Q_BLOCK, h, d), kw.reshape(b * nb, nw * WIN_Q_BLOCK, kv, d),
                       vw.reshape(b * nb, nw * WIN_Q_BLOCK, kv, d), mask)
    return o.reshape(b, s, h, d)


def _gather_selected(pool, new_rows, page_table, idx):
    n, l, kv, d = new_rows.shape
    n_pages = page_table.shape[1]
    past_blocks = n_pages * PAGE_SIZE // CMP_BLOCK
    nbn = -(-l // CMP_BLOCK)
    newb = jnp.pad(new_rows, ((0, 0), (0, nbn * CMP_BLOCK - l), (0, 0), (0, 0)))
    newb = newb.reshape(n, nbn, CMP_BLOCK, kv, d).transpose(0, 3, 1, 2, 4)
    bi = jnp.arange(n)[:, None, None, None]
    hi = jnp.arange(kv)[None, None, :, None]
    start = idx * CMP_BLOCK
    phys = page_table[bi, jnp.minimum(start // PAGE_SIZE, n_pages - 1)]
    off = (start % PAGE_SIZE)[..., None] + jnp.arange(CMP_BLOCK)
    past = pool[phys[..., None], off, hi[..., None]]
    new = newb[bi, hi, jnp.clip(idx - past_blocks, 0, nbn - 1)]
    return jnp.where((idx < past_blocks)[..., None, None], past, new)


def _nsa_combine(gl, o_cmp, o_sel, o_win):
    n, l = gl.shape[0], gl.shape[1]
    g = jax.nn.sigmoid(gl.astype(F32)).reshape(n, l, NSA_HEADS, 3, 1)
    o = g[..., 0, :] * o_cmp + g[..., 1, :] * o_sel + g[..., 2, :] * o_win
    return o.reshape(n, l, NSA_W).astype(o_cmp.dtype)


def _prompt_mixer(x, win_buf, w_in, w_cmp_k, w_cmp_v, gn_g):
    n, s, _ = x.shape
    main = _project(x, w_in, jnp.arange(s))
    ret_out, s_fin = _retention_prompt(main, gn_g, n, s)
    kc, vc = _compress_prompt(main, w_cmp_k, w_cmp_v)
    nsa = _nsa_prompt(main, kc.reshape(n, s // CMP_BLOCK, KV_W), vc.reshape(n, s // CMP_BLOCK, KV_W), n, s)
    ck, cv, sk, sv, wk, wv = [_heads(t, NSA_KV_HEADS) for t in _split_main(main, n, s, first=5)]
    feats = (ret_out, nsa)
    if s >= win_buf:
        bk, bv = wk[:, s - win_buf:], wv[:, s - win_buf:]
    else:
        padb = ((0, 0), (win_buf - s, 0), (0, 0), (0, 0))
        bk, bv = jnp.pad(wk, padb), jnp.pad(wv, padb)
    return feats, (ck, cv, sk, sv, bk, bv, s_fin)


def _sample_mixer(x, c_cmp_k, c_cmp_v, c_sel_k, c_sel_v, c_win_k, c_win_v, s_ret, page_table,
                  w_in, w_cmp_k, w_cmp_v, gn_g):
    n, l, _ = x.shape
    past = page_table.shape[1] * PAGE_SIZE
    pos = past + jnp.arange(l)[None]
    assert l == 1
    main = _project(x, w_in, jnp.full((n,), past, jnp.int32))
    rq, rk, rv, rg, nq, ck, cv, sk, sv, wk, wv = _split_main(main, n, l)
    gl = _gate_logits(main, n, l)
    ret_out, s_new = _retention_group(rq, rk, rv, rg, pos, s_ret, gn_g)
    q = _heads(nq, NSA_HEADS)
    ck, cv, sk, sv, wk, wv = [_heads(t, NSA_KV_HEADS) for t in (ck, cv, sk, sv, wk, wv)]
    assert l == 1 and past % CMP_BLOCK == 0 and c_win_k.shape[1] <= WINDOW
    kc, vc = _compress_paged(c_cmp_k, c_cmp_v, page_table, w_cmp_k, w_cmp_v)
    o_cmp, sel = _sample_select(q[:, 0], kc, vc, past)
    sel = sel[:, :SEL_TOPK].reshape(-1)
    gl_pad = jnp.pad(gl.reshape(n, NSA_HEADS, 3), ((0, 0), (0, 0), (0, LANES - 3)))
    nsa = _sample_attend(sel, page_table, q[:, 0], sk[:, 0], sv[:, 0], wk[:, 0], wv[:, 0],
                         _interleaved(c_win_k), _interleaved(c_win_v), _interleaved(c_sel_k), _interleaved(c_sel_v),
                         o_cmp, gl_pad, past)
    feats = (ret_out.reshape(n * l, RET_W), nsa.reshape(n * l, NSA_W))
    kw = jnp.concatenate([c_win_k, wk], 1)
    vw = jnp.concatenate([c_win_v, wv], 1)
    return feats, (ck, cv, sk, sv, kw[:, l:], vw[:, l:], s_new)


def _route_params(w_group, b_group, w_expert, b_expert):
    w = jnp.concatenate([w_group, w_expert], axis=1)
    b = jnp.concatenate([b_group, b_expert], axis=0)
    pad = LANES - w.shape[1]
    return jnp.pad(w, ((0, 0), (0, pad))).astype(BF16), jnp.pad(b, (0, pad)).reshape(1, LANES)


def _dispatch_plan(route, counts_tile):
    counts = counts_tile[0, EXPERT_LANE0:EXPERT_LANE0 + N_EXPERTS].astype(jnp.int32)
    padded = (counts + MOE_BM - 1) // MOE_BM * MOE_BM
    pad_end = jnp.cumsum(padded)
    pad_start = pad_end - padded
    n_asg = route.shape[0] * EXPERT_TOPK
    n_blk = -(-(n_asg + N_EXPERTS * (MOE_BM - 1)) // MOE_BM)
    blk_first = jnp.arange(n_blk, dtype=jnp.int32) * MOE_BM
    blk_e = jnp.minimum(jnp.sum(pad_end[None, :] <= blk_first[:, None], axis=1), N_EXPERTS - 1).astype(jnp.int32)
    n_used = (pad_end[-1:] // MOE_BM).astype(jnp.int32)
    eid = route[:, R_EID:R_EID + EXPERT_TOPK].astype(jnp.int32)
    rank = route[:, R_RANK:R_RANK + EXPERT_TOPK].astype(jnp.int32)
    start = jnp.sum(jnp.where(eid[..., None] == jnp.arange(N_EXPERTS), pad_start, 0), axis=-1)
    return (start + rank).reshape(-1).astype(jnp.int32), blk_e, n_used


def kernel(x_prompt, x_sample, cache_cmp_k, cache_cmp_v, cache_sel_k, cache_sel_v, cache_win_k, cache_win_v,
           state_ret, page_table, w_in, w_cmp_k, w_cmp_v, ret_gn_g, w_o, ln1_g, ln1_b, w_group, b_group,
           w_expert, b_expert, w_gate, w_up, w_down, ln2_g, ln2_b):
    win_buf = cache_win_k.shape[2]
    hp, hs = x_prompt, x_sample
    acc_p = [[] for _ in range(7)]
    acc_s = [[] for _ in range(7)]
    for l in range(DEPTH):
        fp, st_p = _prompt_mixer(hp, win_buf, w_in[l], w_cmp_k[l], w_cmp_v[l], ret_gn_g[l])
        fs, st_s = _sample_mixer(hs, cache_cmp_k[l], cache_cmp_v[l], cache_sel_k[l], cache_sel_v[l],
                                 cache_win_k[l], cache_win_v[l], state_ret[l], page_table,
                                 w_in[l], w_cmp_k[l], w_cmp_v[l], ret_gn_g[l])
        w_o_bf16 = w_o[l].astype(BF16)
        w_route, b_route = _route_params(w_group[l], b_group[l], w_expert[l], b_expert[l])
        tp = hp.shape[0] * hp.shape[1]
        ts = hs.shape[0] * hs.shape[1]
        no_counts = jnp.zeros((SUBLANES, LANES), F32)
        h1p, packed_p, route_p, counts_p = _wo_ln_route(fp[0], fp[1], hp.reshape(tp, D_MODEL), w_o_bf16, ln1_g[l],
                                                        ln1_b[l], w_route, b_route, no_counts)
        h1s, packed_s, route_s, counts = _wo_ln_route(fs[0], fs[1], hs.reshape(ts, D_MODEL), w_o_bf16, ln1_g[l],
                                                      ln1_b[l], w_route, b_route, counts_p)
        packed = jnp.concatenate([packed_p, packed_s], axis=0)
        route = jnp.concatenate([route_p, route_s], axis=0)
        slot, blk_e, n_used = _dispatch_plan(route, counts)
        plane = -(-(tp + ts) // WO_TM) * WO_TM
        y = _expert_ffn(packed, slot, blk_e, n_used, w_gate[l], w_up[l], w_down[l], plane)
        hp = _moe_ln(h1p, y, route, ln2_g[l], ln2_b[l], 0, plane).reshape(hp.shape)
        hs = _moe_ln(h1s, y, route, ln2_g[l], ln2_b[l], tp, plane).reshape(hs.shape)
        for acc, t in zip(acc_p, st_p):
            acc.append(t)
        for acc, t in zip(acc_s, st_s):
            acc.append(t)
    p_cmp_k, p_cmp_v, p_sel_k, p_sel_v, p_win_k, p_win_v, p_ret = [jnp.stack(a) for a in acc_p]
    s_cmp_k, s_cmp_v, s_sel_k, s_sel_v, s_win_k, s_win_v, s_ret = [jnp.stack(a) for a in acc_s]
    return (hp, hs, p_cmp_k, p_cmp_v, p_sel_k, p_sel_v, p_win_k, p_win_v, p_ret.astype(state_ret.dtype),
            s_cmp_k, s_cmp_v, s_sel_k, s_sel_v, s_win_k, s_win_v, s_ret.astype(state_ret.dtype))
```

```python
import functools
import math

import jax
import jax.numpy as jnp
import numpy as np
from jax import lax
from jax.experimental import pallas as pl
from jax.experimental.pallas import tpu as pltpu

D_MODEL = 2048
DEPTH = 1
PAGE_SIZE = 128

F32 = jnp.float32
BF16 = jnp.bfloat16
HEAD_DIM = 128
RET_HEADS = D_MODEL // (2 * HEAD_DIM)
NSA_HEADS = D_MODEL // (2 * HEAD_DIM)
NSA_KV_HEADS = 2
RET_W = RET_HEADS * HEAD_DIM
NSA_W = NSA_HEADS * HEAD_DIM
KV_W = NSA_KV_HEADS * HEAD_DIM
MIX_W = RET_W + NSA_W
RET_CHUNK = 128
RET_ROPE_THETA = 10000.0
ROPE_THETA = 500000.0
ROT_DIM = HEAD_DIM // 4
CMP_BLOCK = 64
SEL_TOPK = 16
WINDOW = 512
WIN_Q_BLOCK = 128
SEL_Q_BLOCK = 64
N_GROUPS = 4
EXPERTS_PER_GROUP = 8
N_EXPERTS = N_GROUPS * EXPERTS_PER_GROUP
EXPERT_TOPK = 2
D_EXPERT = 512
MOE_BLOCK = 128
LN_EPS = 1e-5
GN_EPS = 1e-5
NEG = -1e30
DEEPNORM_ALPHA = (2 * DEPTH) ** 0.25
DEEPNORM_BETA = (8 * DEPTH) ** -0.25
SPLITS = (RET_W, RET_W, RET_W, RET_W, NSA_W, KV_W, KV_W, KV_W, KV_W, KV_W, KV_W, NSA_HEADS * 3)
IN_W = sum(SPLITS)
GATE_W = NSA_HEADS * 3
MAIN_W = IN_W - GATE_W
LANES = 128
VMEM_LIMIT = 48 * 1024 * 1024


PROJ_TN = 512
GL_COL0 = MAIN_W
PROJ_W = -(-(MAIN_W + NSA_KV_HEADS * LANES) // PROJ_TN) * PROJ_TN
ROTATED_SPLITS = (4, 5, 7, 9)


def _rotated_heads():
    cuts = np.cumsum((0,) + SPLITS)
    tiles = []
    for j in range(PROJ_W // PROJ_TN):
        heads = []
        for h in range(PROJ_TN // HEAD_DIM):
            c0 = j * PROJ_TN + h * HEAD_DIM
            split = int(np.searchsorted(cuts, c0, side="right")) - 1
            heads.append(split in ROTATED_SPLITS and c0 < MAIN_W)
        tiles.append(tuple(heads))
    return tiles


def _nsa_rope_tables(pos):
    half = ROT_DIM // 2
    inv = ROPE_THETA ** (-jnp.arange(0, ROT_DIM, 2, dtype=F32) / ROT_DIM)
    ang = pos[:, None].astype(F32) * inv
    cos, sin = jnp.cos(ang), jnp.sin(ang)
    rest = HEAD_DIM - ROT_DIM
    zeros = jnp.zeros((pos.shape[0], half), F32)
    pad = lambda t, fill: jnp.pad(t, ((0, 0), (0, rest)), constant_values=fill)
    return (pad(jnp.concatenate([cos, cos], 1), 1.0), pad(jnp.concatenate([zeros, sin], 1), 0.0),
            pad(jnp.concatenate([-sin, zeros], 1), 0.0))


def _proj_kernel(x_ref, w_ref, wt_ref, cos_ref, up_ref, dn_ref, o_ref, *, patterns):
    j = pl.program_id(1)
    n_main = MAIN_W // PROJ_TN
    xb = x_ref[...].astype(BF16)
    half = ROT_DIM // 2

    def rotated(acc, heads):
        cos, up, dn = cos_ref[...], up_ref[...], dn_ref[...]
        parts = []
        for h, rot in enumerate(heads):
            xh = acc[:, h * HEAD_DIM:(h + 1) * HEAD_DIM]
            if rot:
                xh = xh * cos + pltpu.roll(xh, half, 1) * up + pltpu.roll(xh, HEAD_DIM - half, 1) * dn
            parts.append(xh)
        return jnp.concatenate(parts, axis=1)

    for heads in sorted(set(patterns[:n_main])):
        tiles = [t for t, p in enumerate(patterns[:n_main]) if p == heads]
        hit = functools.reduce(jnp.logical_or, [j == t for t in tiles])

        @pl.when(hit)
        def _(heads=heads):
            acc = jnp.dot(xb, w_ref[...], preferred_element_type=F32)
            o_ref[...] = rotated(acc, heads) if any(heads) else acc

    @pl.when(j >= n_main)
    def _():
        o_ref[...] = jnp.dot(xb, wt_ref[...], preferred_element_type=F32)


def _project(x, w_in, pos):
    n, l, d = x.shape
    t = n * l
    xt = x.reshape(t, d)
    tm = min(1024, t)
    gpg = GATE_W // NSA_KV_HEADS
    n_main = MAIN_W // PROJ_TN
    assert PROJ_W == MAIN_W + PROJ_TN and not any(any(p) for p in _rotated_heads()[n_main:])
    gate_tiles = [jnp.pad(w_in[:, MAIN_W + k * gpg:MAIN_W + (k + 1) * gpg], ((0, 0), (0, LANES - gpg)))
                  for k in range(NSA_KV_HEADS)]
    fill = jnp.zeros((d, PROJ_TN - NSA_KV_HEADS * LANES), F32)
    w_tail = jnp.concatenate(gate_tiles + [fill], axis=1).astype(BF16)
    tables = _nsa_rope_tables(pos)
    pb = pos.shape[0] // tm
    tab = pl.BlockSpec((tm, HEAD_DIM), lambda i, j: (i % pb, 0))
    return pl.pallas_call(
        functools.partial(_proj_kernel, patterns=_rotated_heads()),
        out_shape=jax.ShapeDtypeStruct((t, PROJ_W), F32),
        grid=(t // tm, PROJ_W // PROJ_TN),
        in_specs=[pl.BlockSpec((tm, d), lambda i, j: (i, 0)),
                  pl.BlockSpec((d, PROJ_TN), lambda i, j: (0, jnp.minimum(j, n_main - 1))),
                  pl.BlockSpec((d, PROJ_TN), lambda i, j: (0, 0)), tab, tab, tab],
        out_specs=pl.BlockSpec((tm, PROJ_TN), lambda i, j: (i, j)),
        compiler_params=pltpu.CompilerParams(dimension_semantics=("arbitrary", "arbitrary"),
                                             vmem_limit_bytes=VMEM_LIMIT),
        name="in_proj",
    )(xt, w_in.astype(BF16), w_tail, *tables)


def _split_main(main, n, l, first=0):
    cuts = [0] + [int(c) for c in np.cumsum(SPLITS)[:-1]]
    return [main[:, cuts[i]:cuts[i + 1]].reshape(n, l, -1) for i in range(first, len(SPLITS) - 1)]


def _split_col0(i):
    return int(np.cumsum((0,) + SPLITS)[i])


def _gate_logits(main, n, l):
    gpg = GATE_W // NSA_KV_HEADS
    cols = [main[:, GL_COL0 + k * LANES:GL_COL0 + k * LANES + gpg] for k in range(NSA_KV_HEADS)]
    return jnp.concatenate(cols, axis=1).reshape(n, l, GATE_W)


def _compress_kernel(k_ref, v_ref, wk_ref, wv_ref, ko_ref, vo_ref):
    r = ko_ref.shape[0]
    ko_ref[...] = jnp.sum(k_ref[...].reshape(r, CMP_BLOCK, KV_W) * wk_ref[...][None], axis=1)
    vo_ref[...] = jnp.sum(v_ref[...].reshape(r, CMP_BLOCK, KV_W) * wv_ref[...][None], axis=1)


def _compress_prompt(main, w_cmp_k, w_cmp_v):
    t = main.shape[0]
    r = 32
    wk2 = jnp.tile(w_cmp_k, (1, NSA_KV_HEADS))
    wv2 = jnp.tile(w_cmp_v, (1, NSA_KV_HEADS))
    kcol, vcol = _split_col0(5) // KV_W, _split_col0(6) // KV_W
    wsp = pl.BlockSpec((CMP_BLOCK, KV_W), lambda i: (0, 0))
    osp = pl.BlockSpec((r, KV_W), lambda i: (i, 0))
    return pl.pallas_call(
        _compress_kernel,
        out_shape=[jax.ShapeDtypeStruct((t // CMP_BLOCK, KV_W), F32)] * 2,
        grid=(t // (r * CMP_BLOCK),),
        in_specs=[pl.BlockSpec((r * CMP_BLOCK, KV_W), lambda i: (i, kcol)),
                  pl.BlockSpec((r * CMP_BLOCK, KV_W), lambda i: (i, vcol)), wsp, wsp],
        out_specs=[osp, osp],
        compiler_params=pltpu.CompilerParams(dimension_semantics=("arbitrary",)),
        name="compress_prompt",
    )(main, main, wk2, wv2)


PAGES_PER_STEP = 16
BLOCKS_PER_PAGE = PAGE_SIZE // CMP_BLOCK


SUBLANES = 8
ROWS_PER_BLOCK = CMP_BLOCK * NSA_KV_HEADS
ROWS_PER_PAGE = PAGE_SIZE * NSA_KV_HEADS
BLOCKS_PER_TILE = SUBLANES // NSA_KV_HEADS


def _div(x, n):
    assert n & (n - 1) == 0
    return jnp.right_shift(x, n.bit_length() - 1)


def _mod(x, n):
    assert n & (n - 1) == 0
    return jnp.bitwise_and(x, n - 1)


def _interleaved(pool):
    return pool.reshape(pool.shape[:-3] + (pool.shape[-3] * NSA_KV_HEADS, HEAD_DIM))


def _compress_paged_kernel(pt_ref, pk_ref, pv_ref, wk_ref, wv_ref, ko_ref, vo_ref, kbuf, vbuf, sem):
    s, i = pl.program_id(0), pl.program_id(1)
    n_i = pl.num_programs(1)
    step = s * n_i + i
    last = pl.num_programs(0) * n_i - 1
    pair = _div(lax.broadcasted_iota(jnp.int32, (SUBLANES, HEAD_DIM), 0), NSA_KV_HEADS)

    def fetch(seq, blk, slot):
        for j in range(PAGES_PER_STEP):
            page = pt_ref[seq, blk * PAGES_PER_STEP + j]
            pltpu.make_async_copy(pk_ref.at[page], kbuf.at[slot, j], sem.at[slot]).start(priority=j % 2)
            pltpu.make_async_copy(pv_ref.at[page], vbuf.at[slot, j], sem.at[slot]).start(priority=(j + 1) % 2)

    @pl.when(step == 0)
    def _():
        fetch(0, 0, 0)

    slot = jnp.bitwise_and(step, 1)

    @pl.when(step < last)
    def _():
        wrap = i == n_i - 1
        fetch(jnp.where(wrap, s + 1, s), jnp.where(wrap, 0, i + 1), 1 - slot)

    pltpu.make_async_copy(kbuf.at[slot], kbuf.at[slot], sem.at[slot]).wait()
    pltpu.make_async_copy(vbuf.at[slot], vbuf.at[slot], sem.at[slot]).wait()

    def summaries(buf, w):
        sums = []
        for j in range(PAGES_PER_STEP):
            for b in range(BLOCKS_PER_PAGE):
                y = buf[slot, j, b * ROWS_PER_BLOCK:(b + 1) * ROWS_PER_BLOCK, :] * w
                acc = jnp.sum(y.reshape(ROWS_PER_BLOCK // SUBLANES, SUBLANES, HEAD_DIM), axis=0)
                shift = SUBLANES // 2
                while shift >= NSA_KV_HEADS:
                    acc = acc + pltpu.roll(acc, shift, 0)
                    shift //= 2
                sums.append(acc)
        tiles = []
        for t in range(len(sums) // BLOCKS_PER_TILE):
            tile = sums[t * BLOCKS_PER_TILE]
            for j in range(1, BLOCKS_PER_TILE):
                tile = jnp.where(pair == j, sums[t * BLOCKS_PER_TILE + j], tile)
            tiles.append(tile)
        return jnp.concatenate(tiles, axis=0)

    ko_ref[0] = summaries(kbuf, wk_ref[...])
    vo_ref[0] = summaries(vbuf, wv_ref[...])


def _compress_paged(pool_k, pool_v, page_table, w_cmp_k, w_cmp_v):
    n, n_pages = page_table.shape
    wk2 = jnp.repeat(w_cmp_k, NSA_KV_HEADS, axis=0)
    wv2 = jnp.repeat(w_cmp_v, NSA_KV_HEADS, axis=0)
    anywhere = pl.BlockSpec(memory_space=pl.ANY)
    wsp = pl.BlockSpec((ROWS_PER_BLOCK, HEAD_DIM), lambda s, i, pt: (0, 0))
    rows = PAGES_PER_STEP * BLOCKS_PER_PAGE * NSA_KV_HEADS
    osp = pl.BlockSpec((1, rows, HEAD_DIM), lambda s, i, pt: (s, i, 0))
    page_buf = pltpu.VMEM((2, PAGES_PER_STEP, ROWS_PER_PAGE, HEAD_DIM), F32)
    return pl.pallas_call(
        _compress_paged_kernel,
        out_shape=[jax.ShapeDtypeStruct((n, n_pages * BLOCKS_PER_PAGE * NSA_KV_HEADS, HEAD_DIM), F32)] * 2,
        grid_spec=pltpu.PrefetchScalarGridSpec(
            num_scalar_prefetch=1,
            grid=(n, n_pages // PAGES_PER_STEP),
            in_specs=[anywhere, anywhere, wsp, wsp],
            out_specs=[osp, osp],
            scratch_shapes=[page_buf, page_buf, pltpu.SemaphoreType.DMA((2,))]),
        compiler_params=pltpu.CompilerParams(dimension_semantics=("arbitrary", "arbitrary")),
        name="compress_paged",
    )(page_table, _interleaved(pool_k), _interleaved(pool_v), wk2, wv2)


SS_SEQ = 8
LOWEST = -3.0e38


def _sample_select_kernel(q_ref, kc_ref, vc_ref, ocmp_ref, sel_ref, *, pos):
    ss, nbk = q_ref.shape[0], kc_ref.shape[1]
    kv, g = NSA_KV_HEADS, NSA_GROUP
    head = lax.broadcasted_iota(jnp.int32, (NSA_HEADS, nbk), 0)
    col = lax.broadcasted_iota(jnp.int32, (NSA_HEADS, nbk), 1)
    m = (_mod(col, kv) == _div(head, g)) & ((_div(col, kv) + 1) * CMP_BLOCK - 1 <= pos)
    width = nbk + LANES
    rows = []
    for i in range(ss):
        q = (q_ref[i] * (HEAD_DIM ** -0.5)).astype(BF16)
        p = _masked_softmax(_dot_nt(q, kc_ref[i].astype(BF16)), m)
        ocmp_ref[i] = jnp.dot(p.astype(BF16), vc_ref[i].astype(BF16), preferred_element_type=F32)
        for k in range(kv):
            imp = jnp.sum(p[k * g:(k + 1) * g], axis=0, keepdims=True)
            rows.append(jnp.concatenate([imp, jnp.zeros((1, LANES), F32)], axis=1))
    nrow = ss * kv
    r_iota = lax.broadcasted_iota(jnp.int32, (nrow, width), 0)
    ccol = lax.broadcasted_iota(jnp.int32, (nrow, width), 1)
    cand = jnp.zeros((nrow, width), F32)
    for r, row in enumerate(rows):
        cand = jnp.where(r_iota == r, row, cand)
    cblk = _div(ccol, kv)
    cur = pos // CMP_BLOCK
    n_sel = -(-(pos + 1) // CMP_BLOCK)
    forced = (cblk == 0) | (cblk == cur) | (cblk == cur - 1)
    score = jnp.where(cblk > cur, NEG, jnp.where(forced, -NEG, cand))
    score = jnp.where((_mod(ccol, kv) == _mod(r_iota, kv)) & (cblk < n_sel), score, LOWEST)
    colf = ccol.astype(F32)
    lane = lax.broadcasted_iota(jnp.int32, (nrow, LANES), 1)
    sel = jnp.zeros((nrow, LANES), jnp.int32)
    for t in range(SEL_TOPK):
        mx = jnp.max(score, axis=1, keepdims=True)
        c = jnp.min(jnp.where(score == mx, colf, -LOWEST), axis=1, keepdims=True)
        picked = jnp.where(mx > 0.5 * NEG, _div(c.astype(jnp.int32), kv), -1)
        sel = jnp.where(lane == t, picked, sel)
        score = jnp.where(colf == c, LOWEST, score)
    sel_ref[...] = sel


def _sample_select(q, kc, vc, pos):
    n = q.shape[0]
    nbk = kc.shape[1]
    qsp = pl.BlockSpec((SS_SEQ, NSA_HEADS, HEAD_DIM), lambda i: (i, 0, 0))
    csp = pl.BlockSpec((SS_SEQ, nbk, HEAD_DIM), lambda i: (i, 0, 0))
    return pl.pallas_call(
        functools.partial(_sample_select_kernel, pos=pos),
        out_shape=[jax.ShapeDtypeStruct((n, NSA_HEADS, HEAD_DIM), F32),
                   jax.ShapeDtypeStruct((n * NSA_KV_HEADS, LANES), jnp.int32)],
        grid=(n // SS_SEQ,),
        in_specs=[qsp, csp, csp],
        out_specs=[qsp, pl.BlockSpec((SS_SEQ * NSA_KV_HEADS, LANES), lambda i: (i, 0))],
        compiler_params=pltpu.CompilerParams(dimension_semantics=("arbitrary",), vmem_limit_bytes=VMEM_LIMIT),
        name="sample_select",
    )(q, kc, vc)


N_SLOTS = NSA_KV_HEADS * SEL_TOPK


def _sample_attend_kernel(sel_ref, pt_ref, q_ref, knew_ref, vnew_ref, wknew_ref, wvnew_ref, wkb_ref, wvb_ref,
                          ocmp_ref, gl_ref, *rest, pos, past_blocks):
    del pt_ref
    kblk, vblk, o_ref = rest[:N_SLOTS], rest[N_SLOTS:2 * N_SLOTS], rest[2 * N_SLOTS]
    kv, g, rb = NSA_KV_HEADS, NSA_GROUP, ROWS_PER_BLOCK
    n = pl.program_id(0)
    q = q_ref[0] * (HEAD_DIM ** -0.5)
    row = lax.broadcasted_iota(jnp.int32, (rb, HEAD_DIM), 0)
    col = lax.broadcasted_iota(jnp.int32, (1, rb), 1)

    def new_block(ref):
        out = jnp.zeros((rb, HEAD_DIM), F32)
        for k in range(kv):
            out = jnp.where(row == k, ref[0, k:k + 1, :], out)
        return out

    def per_head_rows(ref):
        return jnp.concatenate([jnp.broadcast_to(ref[0, k:k + 1, :], (g, HEAD_DIM)) for k in range(kv)], axis=0)

    knew, vnew = new_block(knew_ref), new_block(vnew_ref)
    o_sel = []
    for k in range(kv):
        ks, vs, ms = [], [], []
        for j in range(SEL_TOPK):
            b = sel_ref[(n * kv + k) * SEL_TOPK + j]
            is_new = jnp.broadcast_to(b, (rb, HEAD_DIM)) >= past_blocks
            ks.append(jnp.where(is_new, knew, kblk[k * SEL_TOPK + j][0]))
            vs.append(jnp.where(is_new, vnew, vblk[k * SEL_TOPK + j][0]))
            first = jnp.where(b >= 0, b, 1 << 24) * CMP_BLOCK
            ms.append((first + _div(col, kv) <= pos) & (_mod(col, kv) == k))
        s = _dot_nt(q[k * g:(k + 1) * g].astype(BF16), jnp.concatenate(ks, axis=0).astype(BF16))
        p = _masked_softmax(s, jnp.concatenate(ms, axis=1))
        o_sel.append(jnp.dot(p.astype(BF16), jnp.concatenate(vs, axis=0).astype(BF16), preferred_element_type=F32))
    o_sel = jnp.concatenate(o_sel, axis=0)

    nwr = wkb_ref.shape[1]
    s_w = _dot_nt(q.astype(BF16), wkb_ref[0].astype(BF16))
    cw = lax.broadcasted_iota(jnp.int32, (NSA_HEADS, nwr), 1)
    hw = lax.broadcasted_iota(jnp.int32, (NSA_HEADS, nwr), 0)
    kpos = pos - nwr // kv + _div(cw, kv)
    mw = (kpos <= pos) & (pos - kpos < WINDOW) & (_mod(cw, kv) == _div(hw, g))
    s_n = jnp.sum(q * per_head_rows(wknew_ref), axis=1, keepdims=True)
    smw = jnp.where(mw, s_w, NEG)
    mx = jnp.maximum(jnp.max(smw, axis=1, keepdims=True), s_n)
    e_w = jnp.where(mw, jnp.exp(smw - mx), 0.0)
    e_n = jnp.exp(s_n - mx)
    den = jnp.sum(e_w, axis=1, keepdims=True) + e_n
    o_win = (jnp.dot(e_w.astype(BF16), wvb_ref[0].astype(BF16), preferred_element_type=F32)
             + e_n * per_head_rows(wvnew_ref)) / den

    gate = 1.0 / (1.0 + jnp.exp(-gl_ref[0]))
    o_ref[0] = gate[:, 0:1] * ocmp_ref[0] + gate[:, 1:2] * o_sel + gate[:, 2:3] * o_win


def _sample_attend(sel, page_table, q, sk, sv, wk, wv, win_k, win_v, pool_k, pool_v, o_cmp, gl, pos):
    n, n_pages = page_table.shape
    past_blocks = n_pages * BLOCKS_PER_PAGE

    def slot_spec(k, j):
        def imap(s, sel_r, pt_r):
            b = jnp.maximum(sel_r[(s * NSA_KV_HEADS + k) * SEL_TOPK + j], 0)
            page = jnp.minimum(_div(b, BLOCKS_PER_PAGE), n_pages - 1)
            return (pt_r[s * n_pages + page], _mod(b, BLOCKS_PER_PAGE), 0)
        return pl.BlockSpec((1, ROWS_PER_BLOCK, HEAD_DIM), imap)

    slots = [slot_spec(k, j) for k in range(NSA_KV_HEADS) for j in range(SEL_TOPK)]
    per_seq = lambda a: pl.BlockSpec((1,) + a.shape[1:], lambda s, sel_r, pt_r: (s, 0, 0))
    dense = [q, sk, sv, wk, wv, win_k, win_v, o_cmp, gl]
    return pl.pallas_call(
        functools.partial(_sample_attend_kernel, pos=pos, past_blocks=past_blocks),
        out_shape=jax.ShapeDtypeStruct((n, NSA_HEADS, HEAD_DIM), F32),
        grid_spec=pltpu.PrefetchScalarGridSpec(
            num_scalar_prefetch=2,
            grid=(n,),
            in_specs=[per_seq(a) for a in dense] + slots + slots,
            out_specs=pl.BlockSpec((1, NSA_HEADS, HEAD_DIM), lambda s, sel_r, pt_r: (s, 0, 0))),
        compiler_params=pltpu.CompilerParams(dimension_semantics=("arbitrary",), vmem_limit_bytes=VMEM_LIMIT),
        name="sample_attend",
    )(sel, page_table.reshape(-1), *dense, *([pool_k] * N_SLOTS), *([pool_v] * N_SLOTS))


def _retention_tables(seq):
    c = RET_CHUNK
    log_g = jnp.log1p(-jnp.exp2(-5.0 - jnp.arange(RET_HEADS, dtype=F32)))
    i = jnp.arange(c, dtype=F32)
    rel = i[:, None] - i[None, :]
    dmask = jnp.where(rel[None] >= 0, jnp.exp(jnp.maximum(rel[None], 0.0) * log_g[:, None, None]), 0.0)
    q_dec = jnp.exp((i + 1.0)[None] * log_g[:, None])[..., None]
    k_dec = jnp.exp((c - 1.0 - i)[None] * log_g[:, None])[..., None]
    c_dec = jnp.exp(c * log_g)[:, None, None]
    bc = lambda t: jnp.broadcast_to(t, (RET_HEADS, c, HEAD_DIM))
    inv = RET_ROPE_THETA ** (-jnp.arange(0, HEAD_DIM, 2, dtype=F32) / HEAD_DIM)
    ang = jnp.arange(seq)[:, None].astype(F32) * inv
    cos, sin = jnp.cos(ang), jnp.sin(ang)
    return (dmask, bc(q_dec), bc(k_dec), jnp.broadcast_to(c_dec, (RET_HEADS, 1, HEAD_DIM)),
            jnp.concatenate([cos, cos], -1), jnp.concatenate([-sin, sin], -1))


def _retention_kernel(q_ref, k_ref, v_ref, g_ref, cos_ref, sin_ref, dmask_ref, qdec_ref, kdec_ref, cdec_ref, gn_ref,
                      o_ref, st_ref, s_scr):
    c = pl.program_id(1)

    @pl.when(c == 0)
    def _():
        s_scr[...] = jnp.zeros(s_scr.shape, F32)

    cosf, sins = cos_ref[...], sin_ref[...]
    half = HEAD_DIM // 2
    for h in range(RET_HEADS):
        sl = slice(h * HEAD_DIM, (h + 1) * HEAD_DIM)
        qh, kh = q_ref[:, sl], k_ref[:, sl]
        qr = qh * cosf + pltpu.roll(qh, half, 1) * sins
        kr = (kh * cosf + pltpu.roll(kh, half, 1) * sins) * (HEAD_DIM ** -0.5)
        vb = v_ref[:, sl].astype(BF16)
        att = _dot_nt(qr.astype(BF16), kr.astype(BF16)) * dmask_ref[h]
        s_prev = s_scr[h]
        o = (jnp.dot(att.astype(BF16), vb, preferred_element_type=F32)
             + jnp.dot((qr * qdec_ref[h]).astype(BF16), s_prev.astype(BF16), preferred_element_type=F32))
        s_scr[h] = cdec_ref[h] * s_prev + lax.dot_general(
            (kr * kdec_ref[h]).astype(BF16), vb, (((0,), (0,)), ((), ())), preferred_element_type=F32)
        mu = jnp.mean(o, axis=-1, keepdims=True)
        var = jnp.mean(jnp.square(o - mu), axis=-1, keepdims=True)
        on = (o - mu) * lax.rsqrt(var + GN_EPS) * gn_ref[:, sl]
        gg = g_ref[:, sl]
        o_ref[:, sl] = gg * (1.0 / (1.0 + jnp.exp(-gg))) * on

    @pl.when(c == pl.num_programs(1) - 1)
    def _():
        st_ref[0] = s_scr[...]


def _retention_prompt(main, gn_g, batch, seq):
    nc = seq // RET_CHUNK
    dmask, q_dec, k_dec, c_dec, cosf, sins = _retention_tables(seq)
    col = lambda j: pl.BlockSpec((RET_CHUNK, RET_W), lambda b, c: (b * nc + c, j))
    pos_tab = pl.BlockSpec((RET_CHUNK, HEAD_DIM), lambda b, c: (c, 0))
    full = lambda a: pl.BlockSpec(a.shape, lambda b, c: (0,) * a.ndim)
    gn = gn_g.reshape(1, RET_W)
    return pl.pallas_call(
        _retention_kernel,
        out_shape=[jax.ShapeDtypeStruct((batch * seq, RET_W), F32),
                   jax.ShapeDtypeStruct((batch, RET_HEADS, HEAD_DIM, HEAD_DIM), F32)],
        grid=(batch, nc),
        in_specs=[col(0), col(1), col(2), col(3), pos_tab, pos_tab,
                  full(dmask), full(q_dec), full(k_dec), full(c_dec), full(gn)],
        out_specs=[pl.BlockSpec((RET_CHUNK, RET_W), lambda b, c: (b * nc + c, 0)),
                   pl.BlockSpec((1, RET_HEADS, HEAD_DIM, HEAD_DIM), lambda b, c: (b, 0, 0, 0))],
        scratch_shapes=[pltpu.VMEM((RET_HEADS, HEAD_DIM, HEAD_DIM), F32)],
        compiler_params=pltpu.CompilerParams(dimension_semantics=("arbitrary", "arbitrary"),
                                             vmem_limit_bytes=VMEM_LIMIT),
        name="retention_prompt",
    )(main, main, main, main, cosf, sins, dmask, q_dec, k_dec, c_dec, gn)


NSA_TQ = 128
NSA_KEY_CHUNK = 512
NSA_GROUP = NSA_HEADS // NSA_KV_HEADS


def _dot_nt(a, b):
    return lax.dot_general(a, b, (((1,), (1,)), ((), ())), preferred_element_type=F32)


def _masked_softmax(s, m):
    sm = jnp.where(m, s, NEG)
    e = jnp.exp(sm - jnp.max(sm, axis=-1, keepdims=True))
    return jnp.where(m, e / jnp.sum(e, axis=-1, keepdims=True), 0.0)


def _select_mask_t(imp_t, pos_t):
    nb = imp_t.shape[0]
    blk = lax.broadcasted_iota(jnp.int32, imp_t.shape, 0)
    cur = jnp.right_shift(pos_t, int(math.log2(CMP_BLOCK)))
    forced = (blk == 0) | (blk == cur) | (blk == cur - 1)
    score = jnp.where(blk > cur, NEG, jnp.where(forced, -NEG, imp_t))
    rank = jnp.zeros(score.shape, jnp.int32)
    for i in range(nb):
        si = score[i:i + 1, :]
        ahead = (si > score) | ((si == score) & (blk > i))
        rank = rank + ahead.astype(jnp.int32)
    return (rank < SEL_TOPK) & (score > 0.5 * NEG)


def _nsa_prompt_kernel(q_ref, kc_ref, vc_ref, sk_ref, sv_ref, wk_ref, wv_ref, gl_ref, o_ref):
    tq, g, kc_n = NSA_TQ, NSA_GROUP, NSA_KEY_CHUNK
    qi = pl.program_id(2)
    q4 = q_ref[...] * (HEAD_DIM ** -0.5)
    qs = jnp.concatenate([q4[:, i * HEAD_DIM:(i + 1) * HEAD_DIM] for i in range(g)], axis=0).astype(BF16)
    pos = qi * tq + lax.broadcasted_iota(jnp.int32, (tq, 1), 0)
    pos4 = jnp.concatenate([pos] * g, axis=0)

    nb = kc_ref.shape[1]
    assert tq == LANES
    s_c = _dot_nt(kc_ref[0].astype(BF16), qs)
    blk_t = lax.broadcasted_iota(jnp.int32, (nb, g * tq), 0)
    pos_t = qi * tq + lax.broadcasted_iota(jnp.int32, (1, tq), 1)
    m_c = (blk_t + 1) * CMP_BLOCK - 1 <= jnp.concatenate([pos_t] * g, axis=1)
    sm_c = jnp.where(m_c, s_c, NEG)
    e_c = jnp.exp(sm_c - jnp.max(sm_c, axis=0, keepdims=True))
    p_c = jnp.where(m_c, e_c / jnp.sum(e_c, axis=0, keepdims=True), 0.0)
    o_cmp = lax.dot_general(p_c.astype(BF16), vc_ref[0].astype(BF16), (((0,), (0,)), ((), ())),
                            preferred_element_type=F32)
    imp_t = p_c[:, 0:tq]
    for i in range(1, g):
        imp_t = imp_t + p_c[:, i * tq:(i + 1) * tq]

    seln_t = jnp.where(_select_mask_t(imp_t, pos_t), 0.0, NEG)
    seln = jnp.concatenate([seln_t, jnp.zeros((tq - nb, tq), F32)], axis=0).T[:, :nb].astype(BF16)
    q_aug = jnp.concatenate([qs, jnp.concatenate([seln] * g, axis=0)], axis=1)
    key_idx = lax.broadcasted_iota(jnp.int32, (kc_n, nb), 0)
    key_blk = lax.broadcasted_iota(jnp.int32, (kc_n, nb), 1)
    key_row = lax.broadcasted_iota(jnp.int32, (1, kc_n), 1)

    def scores(k0):
        member = jnp.right_shift(key_idx + k0, int(math.log2(CMP_BLOCK))) == key_blk
        k_aug = jnp.concatenate([sk_ref[pl.ds(k0, kc_n), :].astype(BF16),
                                 jnp.where(member, 1.0, 0.0).astype(BF16)], axis=1)
        return _dot_nt(q_aug, k_aug)

    def values(k0):
        return sv_ref[pl.ds(k0, kc_n), :].astype(BF16)

    assert kc_n % tq == 0
    c_diag = _div(qi, kc_n // tq)
    kd = pl.multiple_of(c_diag * kc_n, kc_n)
    s_d = jnp.where(key_row + kd <= pos4, scores(kd), NEG)
    m_d = jnp.max(s_d, axis=-1, keepdims=True)
    p_d = jnp.exp(s_d - m_d)
    first = (m_d, jnp.sum(p_d, axis=-1, keepdims=True),
             jnp.dot(p_d.astype(BF16), values(kd), preferred_element_type=F32))

    def chunk(c, carry):
        m_i, l_i, acc = carry
        k0 = pl.multiple_of(c * kc_n, kc_n)
        s = scores(k0)
        m_new = jnp.maximum(m_i, jnp.max(s, axis=-1, keepdims=True))
        alpha = jnp.exp(m_i - m_new)
        p = jnp.exp(s - m_new)
        l_new = alpha * l_i + jnp.sum(p, axis=-1, keepdims=True)
        return m_new, l_new, alpha * acc + jnp.dot(p.astype(BF16), values(k0), preferred_element_type=F32)

    _, l_f, acc_f = lax.fori_loop(0, c_diag, chunk, first)
    o_sel = acc_f / l_f

    nwk = WINDOW + tq
    kstart = pl.multiple_of(jnp.maximum(qi * tq - WINDOW, 0), tq)
    kpos = kstart + lax.broadcasted_iota(jnp.int32, (1, nwk), 1)
    band = jnp.where((kpos <= pos) & (pos - kpos < WINDOW), 0.0, NEG)
    s_w = _dot_nt(qs, wk_ref[pl.ds(kstart, nwk), :].astype(BF16)) + jnp.concatenate([band] * g, axis=0)
    e_w = jnp.exp(s_w - jnp.max(s_w, axis=-1, keepdims=True))
    o_win = (jnp.dot(e_w.astype(BF16), wv_ref[pl.ds(kstart, nwk), :].astype(BF16), preferred_element_type=F32)
             / jnp.sum(e_w, axis=-1, keepdims=True))

    gate = 1.0 / (1.0 + jnp.exp(-gl_ref[...]))
    for i in range(g):
        rows = slice(i * tq, (i + 1) * tq)
        o_ref[:, i * HEAD_DIM:(i + 1) * HEAD_DIM] = (gate[:, 3 * i:3 * i + 1] * o_cmp[rows]
                                                     + gate[:, 3 * i + 1:3 * i + 2] * o_sel[rows]
                                                     + gate[:, 3 * i + 2:3 * i + 3] * o_win[rows])


def _nsa_prompt(main, kc, vc, batch, seq):
    nq = seq // NSA_TQ
    gw = NSA_GROUP * HEAD_DIM
    qcol, glcol = _split_col0(4) // gw, GL_COL0 // LANES
    seq_spec = lambda split: pl.BlockSpec((seq, HEAD_DIM),
                                          lambda b, k, i: (b, _split_col0(split) // HEAD_DIM + k))
    cmp_spec = pl.BlockSpec((1, seq // CMP_BLOCK, HEAD_DIM), lambda b, k, i: (b, 0, k))
    return pl.pallas_call(
        _nsa_prompt_kernel,
        out_shape=jax.ShapeDtypeStruct((batch * seq, NSA_W), F32),
        grid=(batch, NSA_KV_HEADS, nq),
        in_specs=[pl.BlockSpec((NSA_TQ, gw), lambda b, k, i: (b * nq + i, qcol + k)), cmp_spec, cmp_spec,
                  seq_spec(7), seq_spec(8), seq_spec(9), seq_spec(10),
                  pl.BlockSpec((NSA_TQ, LANES), lambda b, k, i: (b * nq + i, glcol + k))],
        out_specs=pl.BlockSpec((NSA_TQ, gw), lambda b, k, i: (b * nq + i, k)),
        compiler_params=pltpu.CompilerParams(dimension_semantics=("arbitrary",) * 3,
                                             vmem_limit_bytes=VMEM_LIMIT),
        name="nsa_prompt",
    )(main, kc, vc, main, main, main, main, main)


WO_TM = 512


def _layer_norm_rows(y, g, b):
    mu = jnp.mean(y, axis=-1, keepdims=True)
    var = jnp.mean(jnp.square(y - mu), axis=-1, keepdims=True)
    return (y - mu) * lax.rsqrt(var + LN_EPS) * g + b


EXPERT_LANE0 = N_GROUPS
R_EID, R_RANK, R_GATE = 0, EXPERT_TOPK, 2 * EXPERT_TOPK


def _route(h, wr_ref, br_ref, carry):
    tm = h.shape[0]
    logit = jnp.dot(h.astype(BF16), wr_ref[...], preferred_element_type=F32) + br_ref[...]
    lane = lax.broadcasted_iota(jnp.int32, (tm, LANES), 1)
    lanef = lane.astype(F32)
    first_lane = lambda hit: jnp.min(jnp.where(hit, lanef, float(LANES)), axis=1, keepdims=True)
    is_g = lane < N_GROUPS
    gl = jnp.where(is_g, logit, LOWEST)
    gmx = jnp.max(gl, axis=1, keepdims=True)
    grp = first_lane(gl == gmx)
    p_grp = 1.0 / jnp.sum(jnp.where(is_g, jnp.exp(gl - gmx), 0.0), axis=1, keepdims=True)
    lane_grp = jnp.right_shift(lane - EXPERT_LANE0, int(math.log2(EXPERTS_PER_GROUP)))
    in_grp = lane_grp.astype(F32) == grp
    el = jnp.where(in_grp, logit, LOWEST)
    ee = jnp.where(in_grp, jnp.exp(el - jnp.max(el, axis=1, keepdims=True)), 0.0)
    pe = jnp.where(in_grp, ee / jnp.sum(ee, axis=1, keepdims=True), -1.0)
    p1 = jnp.max(pe, axis=1, keepdims=True)
    l1 = first_lane(pe == p1)
    pe2 = jnp.where(lanef == l1, -1.0, pe)
    p2 = jnp.max(pe2, axis=1, keepdims=True)
    l2 = first_lane(pe2 == p2)
    den = p1 + p2
    o1, o2 = lanef == l1, lanef == l2
    onehot = jnp.where(o1 | o2, 1.0, 0.0)
    r = lax.broadcasted_iota(jnp.int32, (tm, tm), 0)
    c = lax.broadcasted_iota(jnp.int32, (tm, tm), 1)
    earlier = jnp.where(c < r, 1.0, 0.0).astype(BF16)
    prefix = jnp.dot(earlier, onehot.astype(BF16), preferred_element_type=F32) + carry[0:1, :]
    rank1 = jnp.sum(jnp.where(o1, prefix, 0.0), axis=1, keepdims=True)
    rank2 = jnp.sum(jnp.where(o2, prefix, 0.0), axis=1, keepdims=True)
    carry[0:1, :] = carry[0:1, :] + jnp.sum(onehot, axis=0, keepdims=True)
    fields = [l1 - EXPERT_LANE0, l2 - EXPERT_LANE0, rank1, rank2, p_grp * p1 / den, p_grp * p2 / den]
    rec = jnp.zeros((tm, LANES), F32)
    for j, f in enumerate(fields):
        rec = jnp.where(lane == j, f, rec)
    return rec


def _pack_bf16_pairs(x):
    w = x.shape[1] // 2
    return pltpu.pack_elementwise([x[:, :w], x[:, w:]], packed_dtype=jnp.bfloat16)


def _rows_to_tiles(ref, index, p):
    r = p.shape[0]
    for s in range(SUBLANES):
        ref[index + (pl.ds(s, r, stride=SUBLANES), slice(None))] = p[:, s * LANES:(s + 1) * LANES]


def _tiles_to_rows(ref, index, r):
    return jnp.concatenate([ref[index + (pl.ds(s, r, stride=SUBLANES), slice(None))] for s in range(SUBLANES)],
                           axis=1)


def _unpack_bf16_pairs(p):
    halves = [pltpu.unpack_elementwise(p, index=i, packed_dtype=jnp.bfloat16, unpacked_dtype=F32) for i in (0, 1)]
    return jnp.concatenate(halves, axis=1)


def _wo_ln_route_kernel(fr_ref, fn_ref, x_ref, w_ref, g_ref, b_ref, wr_ref, br_ref, base_ref,
                        h_ref, hp_ref, route_ref, cnt_ref, carry):
    i = pl.program_id(0)

    @pl.when(i == 0)
    def _():
        carry[...] = base_ref[...]

    y = (DEEPNORM_ALPHA * x_ref[...]
         + jnp.dot(fr_ref[...].astype(BF16), w_ref[0:RET_W, :], preferred_element_type=F32)
         + jnp.dot(fn_ref[...].astype(BF16), w_ref[RET_W:MIX_W, :], preferred_element_type=F32))
    h = _layer_norm_rows(y, g_ref[...], b_ref[...])
    h_ref[...] = h
    _rows_to_tiles(hp_ref, (), _pack_bf16_pairs(h))
    route_ref[...] = _route(h, wr_ref, br_ref, carry)

    @pl.when(i == pl.num_programs(0) - 1)
    def _():
        cnt_ref[...] = carry[...]


def _wo_ln_route(f_ret, f_nsa, x, w_o_bf16, ln_g, ln_b, w_route, b_route, base_counts):
    t, d = x.shape
    tm = min(WO_TM, t)
    assert t % tm == 0
    row = lambda w: pl.BlockSpec((tm, w), lambda i: (i, 0))
    full = lambda a: pl.BlockSpec(a.shape, lambda i: (0,) * a.ndim)
    lg, lb = ln_g.reshape(1, d), ln_b.reshape(1, d)
    return pl.pallas_call(
        _wo_ln_route_kernel,
        out_shape=[jax.ShapeDtypeStruct((t, d), F32), jax.ShapeDtypeStruct((t * SUBLANES, LANES), jnp.uint32),
                   jax.ShapeDtypeStruct((t, LANES), F32), jax.ShapeDtypeStruct((SUBLANES, LANES), F32)],
        grid=(t // tm,),
        in_specs=[row(RET_W), row(NSA_W), row(d), full(w_o_bf16), full(lg), full(lb),
                  full(w_route), full(b_route), full(base_counts)],
        out_specs=[row(d), pl.BlockSpec((tm * SUBLANES, LANES), lambda i: (i, 0)), row(LANES),
                   pl.BlockSpec((SUBLANES, LANES), lambda i: (0, 0))],
        scratch_shapes=[pltpu.VMEM((SUBLANES, LANES), F32)],
        compiler_params=pltpu.CompilerParams(dimension_semantics=("arbitrary",), vmem_limit_bytes=VMEM_LIMIT),
        name="wo_ln1_route",
    )(f_ret, f_nsa, x, w_o_bf16, lg, lb, w_route, b_route, base_counts)


MOE_BM = 256
MOE_SUB = 64
TABLE_UNROLL = 8


def _expert_kernel(blk_e_ref, n_used_ref, slot_ref, h_ref, wg_ref, wu_ref, wd_ref, y_ref,
                   src_tok, dst_row, xbuf, obuf, gsem, ssem, *, plane, n_rows):
    del blk_e_ref
    n_sub, sub = xbuf.shape[1], xbuf.shape[2] // SUBLANES
    bm = n_sub * sub
    i = pl.program_id(0)
    n_used = n_used_ref[0]
    n_asg = slot_ref.shape[0]
    dump0 = EXPERT_TOPK * plane
    assert bm & (bm - 1) == 0

    tile = lambda row: pl.ds(pl.multiple_of(row * SUBLANES, SUBLANES), SUBLANES)

    def gather(blk, buf_slot):
        def sub_block(j, carry):
            base = blk * bm + j * sub
            for u in range(sub):
                pltpu.make_async_copy(h_ref.at[tile(src_tok[base + u]), :],
                                      xbuf.at[buf_slot, j, tile(u), :], gsem.at[buf_slot]).start(priority=u % 2)
            return carry
        lax.fori_loop(0, n_sub, sub_block, 0)

    def scatter(blk, buf_slot):
        def sub_block(j, carry):
            base = blk * bm + j * sub
            for u in range(sub):
                pltpu.make_async_copy(obuf.at[buf_slot, j, tile(u), :],
                                      y_ref.at[tile(dst_row[base + u]), :], ssem.at[buf_slot]).start(priority=u % 2)
            return carry
        lax.fori_loop(0, n_sub, sub_block, 0)

    def wait_block(buf, sem, buf_slot):
        pltpu.make_async_copy(buf.at[buf_slot], buf.at[buf_slot], sem.at[buf_slot]).wait()

    @pl.when(i == 0)
    def _():
        def clear(t, carry):
            for u in range(TABLE_UNROLL):
                r = t * TABLE_UNROLL + u
                src_tok[r] = 0
                dst_row[r] = dump0 + jnp.bitwise_and(r, 2 * bm - 1)
            return carry
        lax.fori_loop(0, n_used_ref[1], clear, 0)

        def fill(t, carry):
            for u in range(TABLE_UNROLL):
                a = t * TABLE_UNROLL + u
                tok = jnp.right_shift(a, 1)
                src_tok[slot_ref[a]] = tok
                dst_row[slot_ref[a]] = jnp.bitwise_and(a, 1) * plane + tok
            return carry
        lax.fori_loop(0, n_used_ref[2], fill, 0)
        gather(0, 0)
        n_tok = h_ref.shape[0] // SUBLANES
        tail = plane - n_tok
        assert tail >= 0
        obuf[1] = jnp.zeros(obuf.shape[1:], obuf.dtype)
        spans = [(dump0, 2 * bm)] + ([(k * plane + n_tok, tail) for k in range(EXPERT_TOPK)] if tail else [])
        copies = []
        for first, count in spans:
            for j in range(-(-count // sub)):
                rows = min(sub, count - j * sub)
                copies.append(pltpu.make_async_copy(
                    obuf.at[1, j % n_sub, pl.ds(0, rows * SUBLANES), :],
                    y_ref.at[pl.ds((first + j * sub) * SUBLANES, rows * SUBLANES), :], ssem.at[1]))
        for cp in copies:
            cp.start()
        for cp in copies:
            cp.wait()

    slot = jnp.bitwise_and(i, 1)

    @pl.when(i + 1 < n_used)
    def _():
        gather(i + 1, 1 - slot)

    @pl.when(i < n_used)
    def _():
        wait_block(xbuf, gsem, slot)

        @pl.when(i >= 2)
        def _():
            wait_block(obuf, ssem, slot)

        packed = jnp.concatenate([_tiles_to_rows(xbuf, (slot, j), sub) for j in range(n_sub)], axis=0)
        xb = _unpack_bf16_pairs(packed).astype(BF16)
        hg = jnp.dot(xb, wg_ref[0].astype(BF16), preferred_element_type=F32)
        hu = jnp.dot(xb, wu_ref[0].astype(BF16), preferred_element_type=F32)
        hb = hg * (1.0 / (1.0 + jnp.exp(-hg))) * hu
        yb = jnp.dot(hb.astype(BF16), wd_ref[0].astype(BF16), preferred_element_type=F32)
        yp = _pack_bf16_pairs(yb)
        for j in range(n_sub):
            _rows_to_tiles(obuf, (slot, j), yp[j * sub:(j + 1) * sub])
        scatter(i, slot)

    @pl.when(i == pl.num_programs(0) - 1)
    def _():
        @pl.when(n_used >= 2)
        def _():
            wait_block(obuf, ssem, jnp.bitwise_and(n_used, 1))
        wait_block(obuf, ssem, jnp.bitwise_and(n_used - 1, 1))


def _expert_ffn(h, slot, blk_e, n_used, w_gate, w_up, w_down, plane):
    t = h.shape[0] // SUBLANES
    dp = SUBLANES * LANES
    d = 2 * dp
    assert w_gate.shape[1] == d
    n_asg = slot.shape[0]
    assert n_asg == t * EXPERT_TOPK and EXPERT_TOPK == 2
    n_blk = -(-(n_asg + N_EXPERTS * (MOE_BM - 1)) // MOE_BM)
    de = w_gate.shape[2]
    assert n_asg % TABLE_UNROLL == 0 and MOE_BM % TABLE_UNROLL == 0
    n_used = jnp.concatenate([n_used, jnp.array([n_blk * MOE_BM // TABLE_UNROLL, n_asg // TABLE_UNROLL], jnp.int32)])
    wspec = lambda shape: pl.BlockSpec((1,) + shape, lambda i, be, nu, sl: (be[i], 0, 0))
    return pl.pallas_call(
        functools.partial(_expert_kernel, plane=plane, n_rows=n_blk * MOE_BM),
        out_shape=jax.ShapeDtypeStruct(((EXPERT_TOPK * plane + 2 * MOE_BM) * SUBLANES, LANES), jnp.uint32),
        grid_spec=pltpu.PrefetchScalarGridSpec(
            num_scalar_prefetch=3,
            grid=(n_blk,),
            in_specs=[pl.BlockSpec(memory_space=pl.ANY), wspec((d, de)), wspec((d, de)), wspec((de, d))],
            out_specs=pl.BlockSpec(memory_space=pl.ANY),
            scratch_shapes=[pltpu.SMEM((n_blk * MOE_BM,), jnp.int32), pltpu.SMEM((n_blk * MOE_BM,), jnp.int32),
                            pltpu.VMEM((2, MOE_BM // MOE_SUB, MOE_SUB * SUBLANES, LANES), jnp.uint32),
                            pltpu.VMEM((2, MOE_BM // MOE_SUB, MOE_SUB * SUBLANES, LANES), jnp.uint32),
                            pltpu.SemaphoreType.DMA((2,)), pltpu.SemaphoreType.DMA((2,))]),
        compiler_params=pltpu.CompilerParams(dimension_semantics=("arbitrary",), vmem_limit_bytes=VMEM_LIMIT),
        name="expert_ffn",
    )(blk_e, n_used, slot, h, w_gate, w_up, w_down)


def _moe_ln_kernel(h_ref, y0_ref, y1_ref, route_ref, g_ref, b_ref, o_ref):
    rec = route_ref[...]
    tm = h_ref.shape[0]
    y = (DEEPNORM_ALPHA * h_ref[...]
         + rec[:, R_GATE:R_GATE + 1] * _unpack_bf16_pairs(_tiles_to_rows(y0_ref, (), tm))
         + rec[:, R_GATE + 1:R_GATE + 2] * _unpack_bf16_pairs(_tiles_to_rows(y1_ref, (), tm)))
    o_ref[...] = _layer_norm_rows(y, g_ref[...], b_ref[...])


def _moe_ln(h, y, route, ln_g, ln_b, row0, plane):
    n_rows, d = h.shape
    tm = min(WO_TM, n_rows)
    assert n_rows % tm == 0 and row0 % tm == 0 and plane % tm == 0 and EXPERT_TOPK == 2
    off = row0 // tm
    row = lambda w, o: pl.BlockSpec((tm, w), lambda i: (i + o, 0))
    tiles = lambda o: pl.BlockSpec((tm * SUBLANES, LANES), lambda i: (i + o, 0))
    vec = pl.BlockSpec((1, d), lambda i: (0, 0))
    return pl.pallas_call(
        _moe_ln_kernel,
        out_shape=jax.ShapeDtypeStruct((n_rows, d), F32),
        grid=(n_rows // tm,),
        in_specs=[row(d, 0), tiles(off), tiles(off + plane // tm), row(LANES, off), vec, vec],
        out_specs=row(d, 0),
        compiler_params=pltpu.CompilerParams(dimension_semantics=("arbitrary",), vmem_limit_bytes=VMEM_LIMIT),
        name="moe_ln2",
    )(h, y, y, route, ln_g.reshape(1, d), ln_b.reshape(1, d))


def _layer_norm(x, g, b):
    xf = x.astype(F32)
    mu = xf.mean(-1, keepdims=True)
    var = jnp.square(xf - mu).mean(-1, keepdims=True)
    return ((xf - mu) * lax.rsqrt(var + LN_EPS) * g + b).astype(x.dtype)


def _rope(x, pos, rot_dim, theta):
    half = rot_dim // 2
    inv = theta ** (-jnp.arange(0, rot_dim, 2, dtype=F32) / rot_dim)
    ang = pos[..., None].astype(F32) * inv
    cos = jnp.cos(ang)[:, :, None, :]
    sin = jnp.sin(ang)[:, :, None, :]
    xr = x[..., :rot_dim].astype(F32)
    x1, x2 = xr[..., :half], xr[..., half:]
    rot = jnp.concatenate([x1 * cos - x2 * sin, x2 * cos + x1 * sin], -1).astype(x.dtype)
    return jnp.concatenate([rot, x[..., rot_dim:]], -1)


def _heads(t, n):
    return t.reshape(t.shape[0], t.shape[1], n, HEAD_DIM)


def _chunk_retention(q, k, v, s0):
    n, l, h, d = q.shape
    c = RET_CHUNK if l % RET_CHUNK == 0 else l
    nc = l // c
    log_g = jnp.log1p(-jnp.exp2(-5.0 - jnp.arange(h, dtype=F32)))
    i = jnp.arange(c, dtype=F32)
    rel = i[:, None] - i[None, :]
    dmask = jnp.where(rel[None] >= 0, jnp.exp(jnp.maximum(rel[None], 0.0) * log_g[:, None, None]), 0.0)
    q_dec = jnp.exp((i + 1.0)[None] * log_g[:, None])[..., None]
    k_dec = jnp.exp((c - 1.0 - i)[None] * log_g[:, None])[..., None]
    c_dec = jnp.exp(c * log_g)[:, None, None]

    def to_chunks(t):
        return t.astype(F32).reshape(n, nc, c, h, d).transpose(1, 0, 3, 2, 4)

    def step(s, qkv):
        qc, kc, vc = qkv
        att = jnp.einsum('bhid,bhjd->bhij', qc, kc) * dmask
        o = jnp.einsum('bhij,bhjd->bhid', att, vc) + jnp.einsum('bhid,bhde->bhie', qc * q_dec, s)
        s = c_dec * s + jnp.einsum('bhjd,bhje->bhde', kc * k_dec, vc)
        return s, o

    s, o = lax.scan(step, s0.astype(F32), (to_chunks(q), to_chunks(k), to_chunks(v)))
    return o.transpose(1, 0, 3, 2, 4).reshape(n, l, h, d), s


def _retention_group(rq, rk, rv, rg, pos, s0, gn_g):
    q = _rope(_heads(rq, RET_HEADS), pos, HEAD_DIM, RET_ROPE_THETA)
    k = _rope(_heads(rk, RET_HEADS), pos, HEAD_DIM, RET_ROPE_THETA) * (HEAD_DIM ** -0.5)
    v = _heads(rv, RET_HEADS)
    o, s = _chunk_retention(q, k, v, s0)
    mu = o.mean(-1, keepdims=True)
    var = jnp.square(o - mu).mean(-1, keepdims=True)
    on = (o - mu) * lax.rsqrt(var + GN_EPS) * gn_g.reshape(RET_HEADS, HEAD_DIM).astype(F32)
    out = jax.nn.silu(rg.astype(F32)) * on.reshape(rg.shape)
    return out.astype(rq.dtype), s


def _gqa_attend(q, k, v, mask):
    n, lq, h, d = q.shape
    kv = k.shape[2]
    qg = q.reshape(n, lq, kv, h // kv, d)
    s = jnp.einsum('nqkgd,nskd->nkgqs', qg, k).astype(F32) * (d ** -0.5)
    m = mask[:, None, None]
    p = jax.nn.softmax(jnp.where(m, s, NEG), axis=-1) * m
    o = jnp.einsum('nkgqs,nskd->nqkgd', p.astype(v.dtype), v)
    return o.reshape(n, lq, h, d), p


def _nsa_heads(nq, ck, sk, wk, cv, sv, wv, pos):
    rp = lambda t, nh: _rope(_heads(t, nh), pos, ROT_DIM, ROPE_THETA)
    return (rp(nq, NSA_HEADS), rp(ck, NSA_KV_HEADS), rp(sk, NSA_KV_HEADS), rp(wk, NSA_KV_HEADS),
            _heads(cv, NSA_KV_HEADS), _heads(sv, NSA_KV_HEADS), _heads(wv, NSA_KV_HEADS))


def _compress(rows, w):
    n, t, kv, d = rows.shape
    return jnp.einsum('nbjkd,jd->nbkd', rows.reshape(n, t // CMP_BLOCK, CMP_BLOCK, kv, d), w)


def _cmp_branch(q, pos, kc, vc):
    nb = kc.shape[1]
    blk_end = (jnp.arange(nb) + 1) * CMP_BLOCK - 1
    mask = blk_end[None, None, :] <= pos[:, :, None]
    o, p = _gqa_attend(q, kc, vc, mask)
    imp = p.sum(axis=2).transpose(0, 2, 1, 3)
    return o, imp


def _select_blocks(imp, pos, n_sel):
    nb = imp.shape[-1]
    imp = jnp.pad(imp, ((0, 0), (0, 0), (0, 0), (0, n_sel - nb)))
    blk = jnp.arange(n_sel)
    cur = (pos // CMP_BLOCK)[:, :, None, None]
    forced = (blk == 0) | (blk == cur) | (blk == cur - 1)
    score = jnp.where(blk > cur, NEG, jnp.where(forced, -NEG, imp))
    top, idx = lax.top_k(score, min(SEL_TOPK, n_sel))
    return idx, top > 0.5 * NEG


def _sel_attend(q, pos, ks, vs, idx, valid):
    n, lq, kv, kk, cb, d = ks.shape
    h = q.shape[2]
    kpos = idx[..., None] * CMP_BLOCK + jnp.arange(CMP_BLOCK)
    m = ((kpos <= pos[:, :, None, None, None]) & valid[..., None]).reshape(n, lq, kv, 1, kk * cb)
    qg = q.reshape(n, lq, kv, h // kv, d)
    kf = ks.reshape(n, lq, kv, kk * cb, d)
    vf = vs.reshape(n, lq, kv, kk * cb, d)
    s = jnp.einsum('nqkgd,nqkjd->nqkgj', qg, kf).astype(F32) * (d ** -0.5)
    p = jax.nn.softmax(jnp.where(m, s, NEG), axis=-1) * m
    o = jnp.einsum('nqkgj,nqkjd->nqkgd', p.astype(vf.dtype), vf)
    return o.reshape(n, lq, h, d)


def _sel_prompt(q, pos, k, v, idx, valid):
    b, s, h, d = q.shape
    kv = k.shape[2]
    nb = s // CMP_BLOCK
    nq = s // SEL_Q_BLOCK
    kb = k.reshape(b, nb, CMP_BLOCK, kv, d).transpose(0, 3, 1, 2, 4)
    vb = v.reshape(b, nb, CMP_BLOCK, kv, d).transpose(0, 3, 1, 2, 4)
    bi = jnp.arange(b)[:, None, None, None]
    hi = jnp.arange(kv)[None, None, :, None]

    def blockwise(t):
        return t.reshape(t.shape[0], nq, SEL_Q_BLOCK, *t.shape[2:]).swapaxes(0, 1)

    def one(args):
        qc, pc, ic, vc = args
        return _sel_attend(qc, pc, kb[bi, hi, ic], vb[bi, hi, ic], ic, vc)

    o = lax.map(one, (blockwise(q), blockwise(pos), blockwise(idx), blockwise(valid)))
    return o.swapaxes(0, 1).reshape(b, s, h, d)


def _win_prompt(q, k, v):
    b, s, h, d = q.shape
    kv = k.shape[2]
    nb = s // WIN_Q_BLOCK
    nprev = WINDOW // WIN_Q_BLOCK
    nw = nprev + 1
    padw = ((0, 0), (WINDOW, 0), (0, 0), (0, 0))
    kp = jnp.pad(k, padw).reshape(b, nb + nprev, WIN_Q_BLOCK, kv, d)
    vp = jnp.pad(v, padw).reshape(b, nb + nprev, WIN_Q_BLOCK, kv, d)
    kw = jnp.concatenate([kp[:, i:i + nb] for i in range(nw)], axis=2)
    vw = jnp.concatenate([vp[:, i:i + nb] for i in range(nw)], axis=2)
    qpos = jnp.arange(s).reshape(nb, WIN_Q_BLOCK)
    kpos = (jnp.arange(nb)[:, None] - nprev) * WIN_Q_BLOCK + jnp.arange(nw * WIN_Q_BLOCK)[None]
    qq, kk = qpos[:, :, None], kpos[:, None, :]
    mask = (kk <= qq) & (qq - kk < WINDOW) & (kk >= 0)
    mask = jnp.broadcast_to(mask[None], (b,) + mask.shape).reshape(b * nb, WIN_Q_BLOCK, nw * WIN_Q_BLOCK)
    o, _ = _gqa_attend(q.reshape(b * nb, WIN_Q_BLOCK, h, d), kw.reshape(b * nb, nw * WIN_Q_BLOCK, kv, d),
                       vw.reshape(b * nb, nw * WIN_Q_BLOCK, kv, d), mask)
    return o.reshape(b, s, h, d)


def _gather_selected(pool, new_rows, page_table, idx):
    n, l, kv, d = new_rows.shape
    n_pages = page_table.shape[1]
    past_blocks = n_pages * PAGE_SIZE // CMP_BLOCK
    nbn = -(-l // CMP_BLOCK)
    newb = jnp.pad(new_rows, ((0, 0), (0, nbn * CMP_BLOCK - l), (0, 0), (0, 0)))
    newb = newb.reshape(n, nbn, CMP_BLOCK, kv, d).transpose(0, 3, 1, 2, 4)
    bi = jnp.arange(n)[:, None, None, None]
    hi = jnp.arange(kv)[None, None, :, None]
    start = idx * CMP_BLOCK
    phys = page_table[bi, jnp.minimum(start // PAGE_SIZE, n_pages - 1)]
    off = (start % PAGE_SIZE)[..., None] + jnp.arange(CMP_BLOCK)
    past = pool[phys[..., None], off, hi[..., None]]
    new = newb[bi, hi, jnp.clip(idx - past_blocks, 0, nbn - 1)]
    return jnp.where((idx < past_blocks)[..., None, None], past, new)


def _nsa_combine(gl, o_cmp, o_sel, o_win):
    n, l = gl.shape[0], gl.shape[1]
    g = jax.nn.sigmoid(gl.astype(F32)).reshape(n, l, NSA_HEADS, 3, 1)
    o = g[..., 0, :] * o_cmp + g[..., 1, :] * o_sel + g[..., 2, :] * o_win
    return o.reshape(n, l, NSA_W).astype(o_cmp.dtype)


def _prompt_mixer(x, win_buf, w_in, w_cmp_k, w_cmp_v, gn_g):
    n, s, _ = x.shape
    main = _project(x, w_in, jnp.arange(s))
    ret_out, s_fin = _retention_prompt(main, gn_g, n, s)
    kc, vc = _compress_prompt(main, w_cmp_k, w_cmp_v)
    nsa = _nsa_prompt(main, kc.reshape(n, s // CMP_BLOCK, KV_W), vc.reshape(n, s // CMP_BLOCK, KV_W), n, s)
    ck, cv, sk, sv, wk, wv = [_heads(t, NSA_KV_HEADS) for t in _split_main(main, n, s, first=5)]
    feats = (ret_out, nsa)
    if s >= win_buf:
        bk, bv = wk[:, s - win_buf:], wv[:, s - win_buf:]
    else:
        padb = ((0, 0), (win_buf - s, 0), (0, 0), (0, 0))
        bk, bv = jnp.pad(wk, padb), jnp.pad(wv, padb)
    return feats, (ck, cv, sk, sv, bk, bv, s_fin)


def _sample_mixer(x, c_cmp_k, c_cmp_v, c_sel_k, c_sel_v, c_win_k, c_win_v, s_ret, page_table,
                  w_in, w_cmp_k, w_cmp_v, gn_g):
    n, l, _ = x.shape
    past = page_table.shape[1] * PAGE_SIZE
    pos = past + jnp.arange(l)[None]
    assert l == 1
    main = _project(x, w_in, jnp.full((n,), past, jnp.int32))
    rq, rk, rv, rg, nq, ck, cv, sk, sv, wk, wv = _split_main(main, n, l)
    gl = _gate_logits(main, n, l)
    ret_out, s_new = _retention_group(rq, rk, rv, rg, pos, s_ret, gn_g)
    q = _heads(nq, NSA_HEADS)
    ck, cv, sk, sv, wk, wv = [_heads(t, NSA_KV_HEADS) for t in (ck, cv, sk, sv, wk, wv)]
    assert l == 1 and past % CMP_BLOCK == 0 and c_win_k.shape[1] <= WINDOW
    kc, vc = _compress_paged(c_cmp_k, c_cmp_v, page_table, w_cmp_k, w_cmp_v)
    o_cmp, sel = _sample_select(q[:, 0], kc, vc, past)
    sel = sel[:, :SEL_TOPK].reshape(-1)
    gl_pad = jnp.pad(gl.reshape(n, NSA_HEADS, 3), ((0, 0), (0, 0), (0, LANES - 3)))
    nsa = _sample_attend(sel, page_table, q[:, 0], sk[:, 0], sv[:, 0], wk[:, 0], wv[:, 0],
                         _interleaved(c_win_k), _interleaved(c_win_v), _interleaved(c_sel_k), _interleaved(c_sel_v),
                         o_cmp, gl_pad, past)
    feats = (ret_out.reshape(n * l, RET_W), nsa.reshape(n * l, NSA_W))
    kw = jnp.concatenate([c_win_k, wk], 1)
    vw = jnp.concatenate([c_win_v, wv], 1)
    return feats, (ck, cv, sk, sv, kw[:, l:], vw[:, l:], s_new)


def _route_params(w_group, b_group, w_expert, b_expert):
    w = jnp.concatenate([w_group, w_expert], axis=1)
    b = jnp.concatenate([b_group, b_expert], axis=0)
    pad = LANES - w.shape[1]
    return jnp.pad(w, ((0, 0), (0, pad))).astype(BF16), jnp.pad(b, (0, pad)).reshape(1, LANES)


def _dispatch_plan(route, counts_tile):
    counts = counts_tile[0, EXPERT_LANE0:EXPERT_LANE0 + N_EXPERTS].astype(jnp.int32)
    padded = (counts + MOE_BM - 1) // MOE_BM * MOE_BM
    pad_end = jnp.cumsum(padded)
    pad_start = pad_end - padded
    n_asg = route.shape[0] * EXPERT_TOPK
    n_blk = -(-(n_asg + N_EXPERTS * (MOE_BM - 1)) // MOE_BM)
    blk_first = jnp.arange(n_blk, dtype=jnp.int32) * MOE_BM
    blk_e = jnp.minimum(jnp.sum(pad_end[None, :] <= blk_first[:, None], axis=1), N_EXPERTS - 1).astype(jnp.int32)
    n_used = (pad_end[-1:] // MOE_BM).astype(jnp.int32)
    eid = route[:, R_EID:R_EID + EXPERT_TOPK].astype(jnp.int32)
    rank = route[:, R_RANK:R_RANK + EXPERT_TOPK].astype(jnp.int32)
    start = jnp.sum(jnp.where(eid[..., None] == jnp.arange(N_EXPERTS), pad_start, 0), axis=-1)
    return (start + rank).reshape(-1).astype(jnp.int32), blk_e, n_used


def kernel(x_prompt, x_sample, cache_cmp_k, cache_cmp_v, cache_sel_k, cache_sel_v, cache_win_k, cache_win_v,
           state_ret, page_table, w_in, w_cmp_k, w_cmp_v, ret_gn_g, w_o, ln1_g, ln1_b, w_group, b_group,
           w_expert, b_expert, w_gate, w_up, w_down, ln2_g, ln2_b):
    win_buf = cache_win_k.shape[2]
    hp, hs = x_prompt, x_sample
    acc_p = [[] for _ in range(7)]
    acc_s = [[] for _ in range(7)]
    for l in range(DEPTH):
        fp, st_p = _prompt_mixer(hp, win_buf, w_in[l], w_cmp_k[l], w_cmp_v[l], ret_gn_g[l])
        fs, st_s = _sample_mixer(hs, cache_cmp_k[l], cache_cmp_v[l], cache_sel_k[l], cache_sel_v[l],
                                 cache_win_k[l], cache_win_v[l], state_ret[l], page_table,
                                 w_in[l], w_cmp_k[l], w_cmp_v[l], ret_gn_g[l])
        w_o_bf16 = w_o[l].astype(BF16)
        w_route, b_route = _route_params(w_group[l], b_group[l], w_expert[l], b_expert[l])
        tp = hp.shape[0] * hp.shape[1]
        ts = hs.shape[0] * hs.shape[1]
        no_counts = jnp.zeros((SUBLANES, LANES), F32)
        h1p, packed_p, route_p, counts_p = _wo_ln_route(fp[0], fp[1], hp.reshape(tp, D_MODEL), w_o_bf16, ln1_g[l],
                                                        ln1_b[l], w_route, b_route, no_counts)
        h1s, packed_s, route_s, counts = _wo_ln_route(fs[0], fs[1], hs.reshape(ts, D_MODEL), w_o_bf16, ln1_g[l],
                                                      ln1_b[l], w_route, b_route, counts_p)
        packed = jnp.concatenate([packed_p, packed_s], axis=0)
        route = jnp.concatenate([route_p, route_s], axis=0)
        slot, blk_e, n_used = _dispatch_plan(route, counts)
        plane = -(-(tp + ts) // WO_TM) * WO_TM
        y = _expert_ffn(packed, slot, blk_e, n_used, w_gate[l], w_up[l], w_down[l], plane)
        hp = _moe_ln(h1p, y, route, ln2_g[l], ln2_b[l], 0, plane).reshape(hp.shape)
        hs = _moe_ln(h1s, y, route, ln2_g[l], ln2_b[l], tp, plane).reshape(hs.shape)
        for acc, t in zip(acc_p, st_p):
            acc.append(t)
        for acc, t in zip(acc_s, st_s):
            acc.append(t)
    p_cmp_k, p_cmp_v, p_sel_k, p_sel_v, p_win_k, p_win_v, p_ret = [jnp.stack(a) for a in acc_p]
    s_cmp_k, s_cmp_v, s_sel_k, s_sel_v, s_win_k, s_win_v, s_ret = [jnp.stack(a) for a in acc_s]
    return (hp, hs, p_cmp_k, p_cmp_v, p_sel_k, p_sel_v, p_win_k, p_win_v, p_ret.astype(state_ret.dtype),
            s_cmp_k, s_cmp_v, s_sel_k, s_sel_v, s_win_k, s_win_v, s_ret.astype(state_ret.dtype))
```

```python
import functools
import math

import jax
import jax.numpy as jnp
import numpy as np
from jax import lax
from jax.experimental import pallas as pl
from jax.experimental.pallas import tpu as pltpu

D_MODEL = 2048
DEPTH = 1
PAGE_SIZE = 128

F32 = jnp.float32
BF16 = jnp.bfloat16
HEAD_DIM = 128
RET_HEADS = D_MODEL // (2 * HEAD_DIM)
NSA_HEADS = D_MODEL // (2 * HEAD_DIM)
NSA_KV_HEADS = 2
RET_W = RET_HEADS * HEAD_DIM
NSA_W = NSA_HEADS * HEAD_DIM
KV_W = NSA_KV_HEADS * HEAD_DIM
MIX_W = RET_W + NSA_W
RET_CHUNK = 128
RET_ROPE_THETA = 10000.0
ROPE_THETA = 500000.0
ROT_DIM = HEAD_DIM // 4
CMP_BLOCK = 64
SEL_TOPK = 16
WINDOW = 512
WIN_Q_BLOCK = 128
SEL_Q_BLOCK = 64
N_GROUPS = 4
EXPERTS_PER_GROUP = 8
N_EXPERTS = N_GROUPS * EXPERTS_PER_GROUP
EXPERT_TOPK = 2
D_EXPERT = 512
MOE_BLOCK = 128
LN_EPS = 1e-5
GN_EPS = 1e-5
NEG = -1e30
DEEPNORM_ALPHA = (2 * DEPTH) ** 0.25
DEEPNORM_BETA = (8 * DEPTH) ** -0.25
SPLITS = (RET_W, RET_W, RET_W, RET_W, NSA_W, KV_W, KV_W, KV_W, KV_W, KV_W, KV_W, NSA_HEADS * 3)
IN_W = sum(SPLITS)
GATE_W = NSA_HEADS * 3
MAIN_W = IN_W - GATE_W
LANES = 128
VMEM_LIMIT = 48 * 1024 * 1024


PROJ_TN = 512
GL_COL0 = MAIN_W
PROJ_W = -(-(MAIN_W + NSA_KV_HEADS * LANES) // PROJ_TN) * PROJ_TN
ROTATED_SPLITS = (4, 5, 7, 9)


def _rotated_heads():
    cuts = np.cumsum((0,) + SPLITS)
    tiles = []
    for j in range(PROJ_W // PROJ_TN):
        heads = []
        for h in range(PROJ_TN // HEAD_DIM):
            c0 = j * PROJ_TN + h * HEAD_DIM
            split = int(np.searchsorted(cuts, c0, side="right")) - 1
            heads.append(split in ROTATED_SPLITS and c0 < MAIN_W)
        tiles.append(tuple(heads))
    return tiles


def _nsa_rope_tables(pos):
    half = ROT_DIM // 2
    inv = ROPE_THETA ** (-jnp.arange(0, ROT_DIM, 2, dtype=F32) / ROT_DIM)
    ang = pos[:, None].astype(F32) * inv
    cos, sin = jnp.cos(ang), jnp.sin(ang)
    rest = HEAD_DIM - ROT_DIM
    zeros = jnp.zeros((pos.shape[0], half), F32)
    pad = lambda t, fill: jnp.pad(t, ((0, 0), (0, rest)), constant_values=fill)
    return (pad(jnp.concatenate([cos, cos], 1), 1.0), pad(jnp.concatenate([zeros, sin], 1), 0.0),
            pad(jnp.concatenate([-sin, zeros], 1), 0.0))


KV_SPLITS = (5, 6, 7, 8, 9, 10)


def _kv_heads():
    cuts = np.cumsum((0,) + SPLITS)
    tiles = []
    for j in range(MAIN_W // PROJ_TN):
        heads = []
        for h in range(PROJ_TN // HEAD_DIM):
            c0 = j * PROJ_TN + h * HEAD_DIM
            split = int(np.searchsorted(cuts, c0, side="right")) - 1
            heads.append((KV_SPLITS.index(split), (c0 - int(cuts[split])) // HEAD_DIM) if split in KV_SPLITS else None)
        tiles.append(tuple(heads))
    return tiles


def _proj_kernel(x_ref, w_ref, wt_ref, cos_ref, up_ref, dn_ref, o_ref, *kv_refs, patterns, kv_heads):
    j = pl.program_id(1)
    n_main = MAIN_W // PROJ_TN
    xb = x_ref[...].astype(BF16)
    half = ROT_DIM // 2

    def rotated(acc, heads):
        cos, up, dn = cos_ref[...], up_ref[...], dn_ref[...]
        parts = []
        for h, rot in enumerate(heads):
            xh = acc[:, h * HEAD_DIM:(h + 1) * HEAD_DIM]
            if rot:
                xh = xh * cos + pltpu.roll(xh, half, 1) * up + pltpu.roll(xh, HEAD_DIM - half, 1) * dn
            parts.append(xh)
        return jnp.concatenate(parts, axis=1)

    plans = list(zip(patterns[:n_main], kv_heads))
    for plan in sorted(set(plans), key=repr):
        tiles = [t for t, p in enumerate(plans) if p == plan]
        hit = functools.reduce(jnp.logical_or, [j == t for t in tiles])

        @pl.when(hit)
        def _(plan=plan):
            heads, dests = plan
            acc = jnp.dot(xb, w_ref[...], preferred_element_type=F32)
            vals = rotated(acc, heads) if any(heads) else acc
            o_ref[...] = vals
            for h, dest in enumerate(dests):
                if dest is not None:
                    out, k = dest
                    kv_refs[out][pl.ds(k, vals.shape[0], stride=NSA_KV_HEADS), :] = (
                        vals[:, h * HEAD_DIM:(h + 1) * HEAD_DIM])

    @pl.when(j >= n_main)
    def _():
        o_ref[...] = jnp.dot(xb, wt_ref[...], preferred_element_type=F32)


def _project(x, w_in, pos):
    n, l, d = x.shape
    t = n * l
    xt = x.reshape(t, d)
    tm = min(1024, t)
    gpg = GATE_W // NSA_KV_HEADS
    n_main = MAIN_W // PROJ_TN
    assert PROJ_W == MAIN_W + PROJ_TN and not any(any(p) for p in _rotated_heads()[n_main:])
    gate_tiles = [jnp.pad(w_in[:, MAIN_W + k * gpg:MAIN_W + (k + 1) * gpg], ((0, 0), (0, LANES - gpg)))
                  for k in range(NSA_KV_HEADS)]
    fill = jnp.zeros((d, PROJ_TN - NSA_KV_HEADS * LANES), F32)
    w_tail = jnp.concatenate(gate_tiles + [fill], axis=1).astype(BF16)
    tables = _nsa_rope_tables(pos)
    pb = pos.shape[0] // tm
    tab = pl.BlockSpec((tm, HEAD_DIM), lambda i, j: (i % pb, 0))
    kv_rows = pl.BlockSpec((tm * NSA_KV_HEADS, HEAD_DIM), lambda i, j: (i, 0))
    outs = pl.pallas_call(
        functools.partial(_proj_kernel, patterns=_rotated_heads(), kv_heads=_kv_heads()),
        out_shape=[jax.ShapeDtypeStruct((t, PROJ_W), F32)]
        + [jax.ShapeDtypeStruct((t * NSA_KV_HEADS, HEAD_DIM), F32)] * len(KV_SPLITS),
        grid=(t // tm, PROJ_W // PROJ_TN),
        in_specs=[pl.BlockSpec((tm, d), lambda i, j: (i, 0)),
                  pl.BlockSpec((d, PROJ_TN), lambda i, j: (0, jnp.minimum(j, n_main - 1))),
                  pl.BlockSpec((d, PROJ_TN), lambda i, j: (0, 0)), tab, tab, tab],
        out_specs=[pl.BlockSpec((tm, PROJ_TN), lambda i, j: (i, j))] + [kv_rows] * len(KV_SPLITS),
        compiler_params=pltpu.CompilerParams(dimension_semantics=("arbitrary", "arbitrary"),
                                             vmem_limit_bytes=VMEM_LIMIT),
        name="in_proj",
    )(xt, w_in.astype(BF16), w_tail, *tables)
    return outs[0], [o.reshape(n, l, NSA_KV_HEADS, HEAD_DIM) for o in outs[1:]]


def _split_main(main, n, l, first=0):
    cuts = [0] + [int(c) for c in np.cumsum(SPLITS)[:-1]]
    return [main[:, cuts[i]:cuts[i + 1]].reshape(n, l, -1) for i in range(first, len(SPLITS) - 1)]


def _split_col0(i):
    return int(np.cumsum((0,) + SPLITS)[i])


def _gate_logits(main, n, l):
    gpg = GATE_W // NSA_KV_HEADS
    cols = [main[:, GL_COL0 + k * LANES:GL_COL0 + k * LANES + gpg] for k in range(NSA_KV_HEADS)]
    return jnp.concatenate(cols, axis=1).reshape(n, l, GATE_W)


def _compress_kernel(k_ref, v_ref, wk_ref, wv_ref, ko_ref, vo_ref):
    r = ko_ref.shape[0]
    ko_ref[...] = jnp.sum(k_ref[...].reshape(r, CMP_BLOCK, KV_W) * wk_ref[...][None], axis=1)
    vo_ref[...] = jnp.sum(v_ref[...].reshape(r, CMP_BLOCK, KV_W) * wv_ref[...][None], axis=1)


def _compress_prompt(main, w_cmp_k, w_cmp_v):
    t = main.shape[0]
    r = 32
    wk2 = jnp.tile(w_cmp_k, (1, NSA_KV_HEADS))
    wv2 = jnp.tile(w_cmp_v, (1, NSA_KV_HEADS))
    kcol, vcol = _split_col0(5) // KV_W, _split_col0(6) // KV_W
    wsp = pl.BlockSpec((CMP_BLOCK, KV_W), lambda i: (0, 0))
    osp = pl.BlockSpec((r, KV_W), lambda i: (i, 0))
    return pl.pallas_call(
        _compress_kernel,
        out_shape=[jax.ShapeDtypeStruct((t // CMP_BLOCK, KV_W), F32)] * 2,
        grid=(t // (r * CMP_BLOCK),),
        in_specs=[pl.BlockSpec((r * CMP_BLOCK, KV_W), lambda i: (i, kcol)),
                  pl.BlockSpec((r * CMP_BLOCK, KV_W), lambda i: (i, vcol)), wsp, wsp],
        out_specs=[osp, osp],
        compiler_params=pltpu.CompilerParams(dimension_semantics=("arbitrary",)),
        name="compress_prompt",
    )(main, main, wk2, wv2)


PAGES_PER_STEP = 16
BLOCKS_PER_PAGE = PAGE_SIZE // CMP_BLOCK


SUBLANES = 8
ROWS_PER_BLOCK = CMP_BLOCK * NSA_KV_HEADS
ROWS_PER_PAGE = PAGE_SIZE * NSA_KV_HEADS
BLOCKS_PER_TILE = SUBLANES // NSA_KV_HEADS


def _div(x, n):
    assert n & (n - 1) == 0
    return jnp.right_shift(x, n.bit_length() - 1)


def _mod(x, n):
    assert n & (n - 1) == 0
    return jnp.bitwise_and(x, n - 1)


def _interleaved(pool):
    return pool.reshape(pool.shape[:-3] + (pool.shape[-3] * NSA_KV_HEADS, HEAD_DIM))


def _compress_paged_kernel(pt_ref, pk_ref, pv_ref, wk_ref, wv_ref, ko_ref, vo_ref, kbuf, vbuf, sem):
    s, i = pl.program_id(0), pl.program_id(1)
    n_i = pl.num_programs(1)
    step = s * n_i + i
    last = pl.num_programs(0) * n_i - 1
    pair = _div(lax.broadcasted_iota(jnp.int32, (SUBLANES, HEAD_DIM), 0), NSA_KV_HEADS)

    def fetch(seq, blk, slot):
        for j in range(PAGES_PER_STEP):
            page = pt_ref[seq, blk * PAGES_PER_STEP + j]
            pltpu.make_async_copy(pk_ref.at[page], kbuf.at[slot, j], sem.at[slot]).start(priority=j % 2)
            pltpu.make_async_copy(pv_ref.at[page], vbuf.at[slot, j], sem.at[slot]).start(priority=(j + 1) % 2)

    @pl.when(step == 0)
    def _():
        fetch(0, 0, 0)

    slot = jnp.bitwise_and(step, 1)

    @pl.when(step < last)
    def _():
        wrap = i == n_i - 1
        fetch(jnp.where(wrap, s + 1, s), jnp.where(wrap, 0, i + 1), 1 - slot)

    pltpu.make_async_copy(kbuf.at[slot], kbuf.at[slot], sem.at[slot]).wait()
    pltpu.make_async_copy(vbuf.at[slot], vbuf.at[slot], sem.at[slot]).wait()

    def summaries(buf, w):
        sums = []
        for j in range(PAGES_PER_STEP):
            for b in range(BLOCKS_PER_PAGE):
                y = buf[slot, j, b * ROWS_PER_BLOCK:(b + 1) * ROWS_PER_BLOCK, :] * w
                acc = jnp.sum(y.reshape(ROWS_PER_BLOCK // SUBLANES, SUBLANES, HEAD_DIM), axis=0)
                shift = SUBLANES // 2
                while shift >= NSA_KV_HEADS:
                    acc = acc + pltpu.roll(acc, shift, 0)
                    shift //= 2
                sums.append(acc)
        tiles = []
        for t in range(len(sums) // BLOCKS_PER_TILE):
            tile = sums[t * BLOCKS_PER_TILE]
            for j in range(1, BLOCKS_PER_TILE):
                tile = jnp.where(pair == j, sums[t * BLOCKS_PER_TILE + j], tile)
            tiles.append(tile)
        return jnp.concatenate(tiles, axis=0)

    ko_ref[0] = summaries(kbuf, wk_ref[...])
    vo_ref[0] = summaries(vbuf, wv_ref[...])


def _compress_paged(pool_k, pool_v, page_table, w_cmp_k, w_cmp_v):
    n, n_pages = page_table.shape
    wk2 = jnp.repeat(w_cmp_k, NSA_KV_HEADS, axis=0)
    wv2 = jnp.repeat(w_cmp_v, NSA_KV_HEADS, axis=0)
    anywhere = pl.BlockSpec(memory_space=pl.ANY)
    wsp = pl.BlockSpec((ROWS_PER_BLOCK, HEAD_DIM), lambda s, i, pt: (0, 0))
    rows = PAGES_PER_STEP * BLOCKS_PER_PAGE * NSA_KV_HEADS
    osp = pl.BlockSpec((1, rows, HEAD_DIM), lambda s, i, pt: (s, i, 0))
    page_buf = pltpu.VMEM((2, PAGES_PER_STEP, ROWS_PER_PAGE, HEAD_DIM), F32)
    return pl.pallas_call(
        _compress_paged_kernel,
        out_shape=[jax.ShapeDtypeStruct((n, n_pages * BLOCKS_PER_PAGE * NSA_KV_HEADS, HEAD_DIM), F32)] * 2,
        grid_spec=pltpu.PrefetchScalarGridSpec(
            num_scalar_prefetch=1,
            grid=(n, n_pages // PAGES_PER_STEP),
            in_specs=[anywhere, anywhere, wsp, wsp],
            out_specs=[osp, osp],
            scratch_shapes=[page_buf, page_buf, pltpu.SemaphoreType.DMA((2,))]),
        compiler_params=pltpu.CompilerParams(dimension_semantics=("arbitrary", "arbitrary")),
        name="compress_paged",
    )(page_table, _interleaved(pool_k), _interleaved(pool_v), wk2, wv2)


SS_SEQ = 8
LOWEST = -3.0e38


def _sample_select_kernel(q_ref, kc_ref, vc_ref, ocmp_ref, sel_ref, *, pos):
    ss, nbk = q_ref.shape[0], kc_ref.shape[1]
    kv, g = NSA_KV_HEADS, NSA_GROUP
    head = lax.broadcasted_iota(jnp.int32, (NSA_HEADS, nbk), 0)
    col = lax.broadcasted_iota(jnp.int32, (NSA_HEADS, nbk), 1)
    m = (_mod(col, kv) == _div(head, g)) & ((_div(col, kv) + 1) * CMP_BLOCK - 1 <= pos)
    width = nbk + LANES
    rows = []
    for i in range(ss):
        q = (q_ref[i] * (HEAD_DIM ** -0.5)).astype(BF16)
        p = _masked_softmax(_dot_nt(q, kc_ref[i].astype(BF16)), m)
        ocmp_ref[i] = jnp.dot(p.astype(BF16), vc_ref[i].astype(BF16), preferred_element_type=F32)
        for k in range(kv):
            imp = jnp.sum(p[k * g:(k + 1) * g], axis=0, keepdims=True)
            rows.append(jnp.concatenate([imp, jnp.zeros((1, LANES), F32)], axis=1))
    nrow = ss * kv
    r_iota = lax.broadcasted_iota(jnp.int32, (nrow, width), 0)
    ccol = lax.broadcasted_iota(jnp.int32, (nrow, width), 1)
    cand = jnp.zeros((nrow, width), F32)
    for r, row in enumerate(rows):
        cand = jnp.where(r_iota == r, row, cand)
    cblk = _div(ccol, kv)
    cur = pos // CMP_BLOCK
    n_sel = -(-(pos + 1) // CMP_BLOCK)
    forced = (cblk == 0) | (cblk == cur) | (cblk == cur - 1)
    score = jnp.where(cblk > cur, NEG, jnp.where(forced, -NEG, cand))
    score = jnp.where((_mod(ccol, kv) == _mod(r_iota, kv)) & (cblk < n_sel), score, LOWEST)
    colf = ccol.astype(F32)
    lane = lax.broadcasted_iota(jnp.int32, (nrow, LANES), 1)
    sel = jnp.zeros((nrow, LANES), jnp.int32)
    for t in range(SEL_TOPK):
        mx = jnp.max(score, axis=1, keepdims=True)
        c = jnp.min(jnp.where(score == mx, colf, -LOWEST), axis=1, keepdims=True)
        picked = jnp.where(mx > 0.5 * NEG, _div(c.astype(jnp.int32), kv), -1)
        sel = jnp.where(lane == t, picked, sel)
        score = jnp.where(colf == c, LOWEST, score)
    sel_ref[...] = sel


def _sample_select(q, kc, vc, pos):
    n = q.shape[0]
    nbk = kc.shape[1]
    qsp = pl.BlockSpec((SS_SEQ, NSA_HEADS, HEAD_DIM), lambda i: (i, 0, 0))
    csp = pl.BlockSpec((SS_SEQ, nbk, HEAD_DIM), lambda i: (i, 0, 0))
    return pl.pallas_call(
        functools.partial(_sample_select_kernel, pos=pos),
        out_shape=[jax.ShapeDtypeStruct((n, NSA_HEADS, HEAD_DIM), F32),
                   jax.ShapeDtypeStruct((n * NSA_KV_HEADS, LANES), jnp.int32)],
        grid=(n // SS_SEQ,),
        in_specs=[qsp, csp, csp],
        out_specs=[qsp, pl.BlockSpec((SS_SEQ * NSA_KV_HEADS, LANES), lambda i: (i, 0))],
        compiler_params=pltpu.CompilerParams(dimension_semantics=("arbitrary",), vmem_limit_bytes=VMEM_LIMIT),
        name="sample_select",
    )(q, kc, vc)


N_SLOTS = NSA_KV_HEADS * SEL_TOPK


def _sample_attend_kernel(sel_ref, pt_ref, q_ref, knew_ref, vnew_ref, wknew_ref, wvnew_ref, wkb_ref, wvb_ref,
                          ocmp_ref, gl_ref, *rest, pos, past_blocks):
    del pt_ref
    kblk, vblk, o_ref = rest[:N_SLOTS], rest[N_SLOTS:2 * N_SLOTS], rest[2 * N_SLOTS]
    kv, g, rb = NSA_KV_HEADS, NSA_GROUP, ROWS_PER_BLOCK
    n = pl.program_id(0)
    q = q_ref[0] * (HEAD_DIM ** -0.5)
    row = lax.broadcasted_iota(jnp.int32, (rb, HEAD_DIM), 0)
    col = lax.broadcasted_iota(jnp.int32, (1, rb), 1)

    def new_block(ref):
        out = jnp.zeros((rb, HEAD_DIM), F32)
        for k in range(kv):
            out = jnp.where(row == k, ref[0, k:k + 1, :], out)
        return out

    def per_head_rows(ref):
        return jnp.concatenate([jnp.broadcast_to(ref[0, k:k + 1, :], (g, HEAD_DIM)) for k in range(kv)], axis=0)

    knew, vnew = new_block(knew_ref), new_block(vnew_ref)
    o_sel = []
    for k in range(kv):
        ks, vs, ms = [], [], []
        for j in range(SEL_TOPK):
            b = sel_ref[(n * kv + k) * SEL_TOPK + j]
            is_new = jnp.broadcast_to(b, (rb, HEAD_DIM)) >= past_blocks
            ks.append(jnp.where(is_new, knew, kblk[k * SEL_TOPK + j][0]))
            vs.append(jnp.where(is_new, vnew, vblk[k * SEL_TOPK + j][0]))
            first = jnp.where(b >= 0, b, 1 << 24) * CMP_BLOCK
            ms.append((first + _div(col, kv) <= pos) & (_mod(col, kv) == k))
        s = _dot_nt(q[k * g:(k + 1) * g].astype(BF16), jnp.concatenate(ks, axis=0).astype(BF16))
        p = _masked_softmax(s, jnp.concatenate(ms, axis=1))
        o_sel.append(jnp.dot(p.astype(BF16), jnp.concatenate(vs, axis=0).astype(BF16), preferred_element_type=F32))
    o_sel = jnp.concatenate(o_sel, axis=0)

    nwr = wkb_ref.shape[1]
    s_w = _dot_nt(q.astype(BF16), wkb_ref[0].astype(BF16))
    cw = lax.broadcasted_iota(jnp.int32, (NSA_HEADS, nwr), 1)
    hw = lax.broadcasted_iota(jnp.int32, (NSA_HEADS, nwr), 0)
    kpos = pos - nwr // kv + _div(cw, kv)
    mw = (kpos <= pos) & (pos - kpos < WINDOW) & (_mod(cw, kv) == _div(hw, g))
    s_n = jnp.sum(q * per_head_rows(wknew_ref), axis=1, keepdims=True)
    smw = jnp.where(mw, s_w, NEG)
    mx = jnp.maximum(jnp.max(smw, axis=1, keepdims=True), s_n)
    e_w = jnp.where(mw, jnp.exp(smw - mx), 0.0)
    e_n = jnp.exp(s_n - mx)
    den = jnp.sum(e_w, axis=1, keepdims=True) + e_n
    o_win = (jnp.dot(e_w.astype(BF16), wvb_ref[0].astype(BF16), preferred_element_type=F32)
             + e_n * per_head_rows(wvnew_ref)) / den

    gate = 1.0 / (1.0 + jnp.exp(-gl_ref[0]))
    o_ref[0] = gate[:, 0:1] * ocmp_ref[0] + gate[:, 1:2] * o_sel + gate[:, 2:3] * o_win


def _sample_attend(sel, page_table, q, sk, sv, wk, wv, win_k, win_v, pool_k, pool_v, o_cmp, gl, pos):
    n, n_pages = page_table.shape
    past_blocks = n_pages * BLOCKS_PER_PAGE

    def slot_spec(k, j):
        def imap(s, sel_r, pt_r):
            b = jnp.maximum(sel_r[(s * NSA_KV_HEADS + k) * SEL_TOPK + j], 0)
            page = jnp.minimum(_div(b, BLOCKS_PER_PAGE), n_pages - 1)
            return (pt_r[s * n_pages + page], _mod(b, BLOCKS_PER_PAGE), 0)
        return pl.BlockSpec((1, ROWS_PER_BLOCK, HEAD_DIM), imap)

    slots = [slot_spec(k, j) for k in range(NSA_KV_HEADS) for j in range(SEL_TOPK)]
    per_seq = lambda a: pl.BlockSpec((1,) + a.shape[1:], lambda s, sel_r, pt_r: (s, 0, 0))
    dense = [q, sk, sv, wk, wv, win_k, win_v, o_cmp, gl]
    return pl.pallas_call(
        functools.partial(_sample_attend_kernel, pos=pos, past_blocks=past_blocks),
        out_shape=jax.ShapeDtypeStruct((n, NSA_HEADS, HEAD_DIM), F32),
        grid_spec=pltpu.PrefetchScalarGridSpec(
            num_scalar_prefetch=2,
            grid=(n,),
            in_specs=[per_seq(a) for a in dense] + slots + slots,
            out_specs=pl.BlockSpec((1, NSA_HEADS, HEAD_DIM), lambda s, sel_r, pt_r: (s, 0, 0))),
        compiler_params=pltpu.CompilerParams(dimension_semantics=("arbitrary",), vmem_limit_bytes=VMEM_LIMIT),
        name="sample_attend",
    )(sel, page_table.reshape(-1), *dense, *([pool_k] * N_SLOTS), *([pool_v] * N_SLOTS))


def _retention_tables(seq):
    c = RET_CHUNK
    log_g = jnp.log1p(-jnp.exp2(-5.0 - jnp.arange(RET_HEADS, dtype=F32)))
    i = jnp.arange(c, dtype=F32)
    rel = i[:, None] - i[None, :]
    dmask = jnp.where(rel[None] >= 0, jnp.exp(jnp.maximum(rel[None], 0.0) * log_g[:, None, None]), 0.0)
    q_dec = jnp.exp((i + 1.0)[None] * log_g[:, None])[..., None]
    k_dec = jnp.exp((c - 1.0 - i)[None] * log_g[:, None])[..., None]
    c_dec = jnp.exp(c * log_g)[:, None, None]
    bc = lambda t: jnp.broadcast_to(t, (RET_HEADS, c, HEAD_DIM))
    inv = RET_ROPE_THETA ** (-jnp.arange(0, HEAD_DIM, 2, dtype=F32) / HEAD_DIM)
    ang = jnp.arange(seq)[:, None].astype(F32) * inv
    cos, sin = jnp.cos(ang), jnp.sin(ang)
    return (dmask, bc(q_dec), bc(k_dec), jnp.broadcast_to(c_dec, (RET_HEADS, 1, HEAD_DIM)),
            jnp.concatenate([cos, cos], -1), jnp.concatenate([-sin, sin], -1))


def _retention_kernel(q_ref, k_ref, v_ref, g_ref, cos_ref, sin_ref, dmask_ref, qdec_ref, kdec_ref, cdec_ref, gn_ref,
                      o_ref, st_ref, s_scr):
    c = pl.program_id(1)

    @pl.when(c == 0)
    def _():
        s_scr[...] = jnp.zeros(s_scr.shape, F32)

    cosf, sins = cos_ref[...], sin_ref[...]
    half = HEAD_DIM // 2
    for h in range(RET_HEADS):
        sl = slice(h * HEAD_DIM, (h + 1) * HEAD_DIM)
        qh, kh = q_ref[:, sl], k_ref[:, sl]
        qr = qh * cosf + pltpu.roll(qh, half, 1) * sins
        kr = (kh * cosf + pltpu.roll(kh, half, 1) * sins) * (HEAD_DIM ** -0.5)
        vb = v_ref[:, sl].astype(BF16)
        att = _dot_nt(qr.astype(BF16), kr.astype(BF16)) * dmask_ref[h]
        s_prev = s_scr[h]
        o = (jnp.dot(att.astype(BF16), vb, preferred_element_type=F32)
             + jnp.dot((qr * qdec_ref[h]).astype(BF16), s_prev.astype(BF16), preferred_element_type=F32))
        s_scr[h] = cdec_ref[h] * s_prev + lax.dot_general(
            (kr * kdec_ref[h]).astype(BF16), vb, (((0,), (0,)), ((), ())), preferred_element_type=F32)
        mu = jnp.mean(o, axis=-1, keepdims=True)
        var = jnp.mean(jnp.square(o - mu), axis=-1, keepdims=True)
        on = (o - mu) * lax.rsqrt(var + GN_EPS) * gn_ref[:, sl]
        gg = g_ref[:, sl]
        o_ref[:, sl] = gg * (1.0 / (1.0 + jnp.exp(-gg))) * on

    @pl.when(c == pl.num_programs(1) - 1)
    def _():
        st_ref[0] = s_scr[...]


def _retention_prompt(main, gn_g, batch, seq):
    nc = seq // RET_CHUNK
    dmask, q_dec, k_dec, c_dec, cosf, sins = _retention_tables(seq)
    col = lambda j: pl.BlockSpec((RET_CHUNK, RET_W), lambda b, c: (b * nc + c, j))
    pos_tab = pl.BlockSpec((RET_CHUNK, HEAD_DIM), lambda b, c: (c, 0))
    full = lambda a: pl.BlockSpec(a.shape, lambda b, c: (0,) * a.ndim)
    gn = gn_g.reshape(1, RET_W)
    return pl.pallas_call(
        _retention_kernel,
        out_shape=[jax.ShapeDtypeStruct((batch * seq, RET_W), F32),
                   jax.ShapeDtypeStruct((batch, RET_HEADS, HEAD_DIM, HEAD_DIM), F32)],
        grid=(batch, nc),
        in_specs=[col(0), col(1), col(2), col(3), pos_tab, pos_tab,
                  full(dmask), full(q_dec), full(k_dec), full(c_dec), full(gn)],
        out_specs=[pl.BlockSpec((RET_CHUNK, RET_W), lambda b, c: (b * nc + c, 0)),
                   pl.BlockSpec((1, RET_HEADS, HEAD_DIM, HEAD_DIM), lambda b, c: (b, 0, 0, 0))],
        scratch_shapes=[pltpu.VMEM((RET_HEADS, HEAD_DIM, HEAD_DIM), F32)],
        compiler_params=pltpu.CompilerParams(dimension_semantics=("arbitrary", "arbitrary"),
                                             vmem_limit_bytes=VMEM_LIMIT),
        name="retention_prompt",
    )(main, main, main, main, cosf, sins, dmask, q_dec, k_dec, c_dec, gn)


NSA_TQ = 128
NSA_KEY_CHUNK = 512
NSA_GROUP = NSA_HEADS // NSA_KV_HEADS


def _dot_nt(a, b):
    return lax.dot_general(a, b, (((1,), (1,)), ((), ())), preferred_element_type=F32)


def _masked_softmax(s, m):
    sm = jnp.where(m, s, NEG)
    e = jnp.exp(sm - jnp.max(sm, axis=-1, keepdims=True))
    return jnp.where(m, e / jnp.sum(e, axis=-1, keepdims=True), 0.0)


def _select_mask_t(imp_t, pos_t):
    nb = imp_t.shape[0]
    blk = lax.broadcasted_iota(jnp.int32, imp_t.shape, 0)
    cur = jnp.right_shift(pos_t, int(math.log2(CMP_BLOCK)))
    forced = (blk == 0) | (blk == cur) | (blk == cur - 1)
    score = jnp.where(blk > cur, NEG, jnp.where(forced, -NEG, imp_t))
    rank = jnp.zeros(score.shape, jnp.int32)
    for i in range(nb):
        si = score[i:i + 1, :]
        ahead = (si > score) | ((si == score) & (blk > i))
        rank = rank + ahead.astype(jnp.int32)
    return (rank < SEL_TOPK) & (score > 0.5 * NEG)


def _nsa_prompt_kernel(q_ref, kc_ref, vc_ref, sk_ref, sv_ref, wk_ref, wv_ref, gl_ref, o_ref):
    tq, g, kc_n = NSA_TQ, NSA_GROUP, NSA_KEY_CHUNK
    qi = pl.program_id(2)
    q4 = q_ref[...] * (HEAD_DIM ** -0.5)
    qs = jnp.concatenate([q4[:, i * HEAD_DIM:(i + 1) * HEAD_DIM] for i in range(g)], axis=0).astype(BF16)
    pos = qi * tq + lax.broadcasted_iota(jnp.int32, (tq, 1), 0)
    pos4 = jnp.concatenate([pos] * g, axis=0)

    nb = kc_ref.shape[1]
    assert tq == LANES
    s_c = _dot_nt(kc_ref[0].astype(BF16), qs)
    blk_t = lax.broadcasted_iota(jnp.int32, (nb, g * tq), 0)
    pos_t = qi * tq + lax.broadcasted_iota(jnp.int32, (1, tq), 1)
    m_c = (blk_t + 1) * CMP_BLOCK - 1 <= jnp.concatenate([pos_t] * g, axis=1)
    sm_c = jnp.where(m_c, s_c, NEG)
    e_c = jnp.exp(sm_c - jnp.max(sm_c, axis=0, keepdims=True))
    p_c = jnp.where(m_c, e_c / jnp.sum(e_c, axis=0, keepdims=True), 0.0)
    o_cmp = lax.dot_general(p_c.astype(BF16), vc_ref[0].astype(BF16), (((0,), (0,)), ((), ())),
                            preferred_element_type=F32)
    imp_t = p_c[:, 0:tq]
    for i in range(1, g):
        imp_t = imp_t + p_c[:, i * tq:(i + 1) * tq]

    seln_t = jnp.where(_select_mask_t(imp_t, pos_t), 0.0, NEG)
    seln = jnp.concatenate([seln_t, jnp.zeros((tq - nb, tq), F32)], axis=0).T[:, :nb].astype(BF16)
    q_aug = jnp.concatenate([qs, jnp.concatenate([seln] * g, axis=0)], axis=1)
    key_idx = lax.broadcasted_iota(jnp.int32, (kc_n, nb), 0)
    key_blk = lax.broadcasted_iota(jnp.int32, (kc_n, nb), 1)
    key_row = lax.broadcasted_iota(jnp.int32, (1, kc_n), 1)

    def scores(k0):
        member = jnp.right_shift(key_idx + k0, int(math.log2(CMP_BLOCK))) == key_blk
        k_aug = jnp.concatenate([sk_ref[pl.ds(k0, kc_n), :].astype(BF16),
                                 jnp.where(member, 1.0, 0.0).astype(BF16)], axis=1)
        return _dot_nt(q_aug, k_aug)

    def values(k0):
        return sv_ref[pl.ds(k0, kc_n), :].astype(BF16)

    assert kc_n % tq == 0
    c_diag = _div(qi, kc_n // tq)
    kd = pl.multiple_of(c_diag * kc_n, kc_n)
    s_d = jnp.where(key_row + kd <= pos4, scores(kd), NEG)
    m_d = jnp.max(s_d, axis=-1, keepdims=True)
    p_d = jnp.exp(s_d - m_d)
    first = (m_d, jnp.sum(p_d, axis=-1, keepdims=True),
             jnp.dot(p_d.astype(BF16), values(kd), preferred_element_type=F32))

    def chunk(c, carry):
        m_i, l_i, acc = carry
        k0 = pl.multiple_of(c * kc_n, kc_n)
        s = scores(k0)
        m_new = jnp.maximum(m_i, jnp.max(s, axis=-1, keepdims=True))
        alpha = jnp.exp(m_i - m_new)
        p = jnp.exp(s - m_new)
        l_new = alpha * l_i + jnp.sum(p, axis=-1, keepdims=True)
        return m_new, l_new, alpha * acc + jnp.dot(p.astype(BF16), values(k0), preferred_element_type=F32)

    _, l_f, acc_f = lax.fori_loop(0, c_diag, chunk, first)
    o_sel = acc_f / l_f

    nwk = WINDOW + tq
    kstart = pl.multiple_of(jnp.maximum(qi * tq - WINDOW, 0), tq)
    kpos = kstart + lax.broadcasted_iota(jnp.int32, (1, nwk), 1)
    band = jnp.where((kpos <= pos) & (pos - kpos < WINDOW), 0.0, NEG)
    s_w = _dot_nt(qs, wk_ref[pl.ds(kstart, nwk), :].astype(BF16)) + jnp.concatenate([band] * g, axis=0)
    e_w = jnp.exp(s_w - jnp.max(s_w, axis=-1, keepdims=True))
    o_win = (jnp.dot(e_w.astype(BF16), wv_ref[pl.ds(kstart, nwk), :].astype(BF16), preferred_element_type=F32)
             / jnp.sum(e_w, axis=-1, keepdims=True))

    gate = 1.0 / (1.0 + jnp.exp(-gl_ref[...]))
    for i in range(g):
        rows = slice(i * tq, (i + 1) * tq)
        o_ref[:, i * HEAD_DIM:(i + 1) * HEAD_DIM] = (gate[:, 3 * i:3 * i + 1] * o_cmp[rows]
                                                     + gate[:, 3 * i + 1:3 * i + 2] * o_sel[rows]
                                                     + gate[:, 3 * i + 2:3 * i + 3] * o_win[rows])


def _nsa_prompt(main, kc, vc, batch, seq):
    nq = seq // NSA_TQ
    gw = NSA_GROUP * HEAD_DIM
    qcol, glcol = _split_col0(4) // gw, GL_COL0 // LANES
    seq_spec = lambda split: pl.BlockSpec((seq, HEAD_DIM),
                                          lambda b, k, i: (b, _split_col0(split) // HEAD_DIM + k))
    cmp_spec = pl.BlockSpec((1, seq // CMP_BLOCK, HEAD_DIM), lambda b, k, i: (b, 0, k))
    return pl.pallas_call(
        _nsa_prompt_kernel,
        out_shape=jax.ShapeDtypeStruct((batch * seq, NSA_W), F32),
        grid=(batch, NSA_KV_HEADS, nq),
        in_specs=[pl.BlockSpec((NSA_TQ, gw), lambda b, k, i: (b * nq + i, qcol + k)), cmp_spec, cmp_spec,
                  seq_spec(7), seq_spec(8), seq_spec(9), seq_spec(10),
                  pl.BlockSpec((NSA_TQ, LANES), lambda b, k, i: (b * nq + i, glcol + k))],
        out_specs=pl.BlockSpec((NSA_TQ, gw), lambda b, k, i: (b * nq + i, k)),
        compiler_params=pltpu.CompilerParams(dimension_semantics=("arbitrary",) * 3,
                                             vmem_limit_bytes=VMEM_LIMIT),
        name="nsa_prompt",
    )(main, kc, vc, main, main, main, main, main)


WO_TM = 512


def _layer_norm_rows(y, g, b):
    mu = jnp.mean(y, axis=-1, keepdims=True)
    var = jnp.mean(jnp.square(y - mu), axis=-1, keepdims=True)
    return (y - mu) * lax.rsqrt(var + LN_EPS) * g + b


EXPERT_LANE0 = N_GROUPS
R_EID, R_RANK, R_GATE = 0, EXPERT_TOPK, 2 * EXPERT_TOPK


def _route(h, wr_ref, br_ref, carry):
    tm = h.shape[0]
    logit = jnp.dot(h.astype(BF16), wr_ref[...], preferred_element_type=F32) + br_ref[...]
    lane = lax.broadcasted_iota(jnp.int32, (tm, LANES), 1)
    lanef = lane.astype(F32)
    first_lane = lambda hit: jnp.min(jnp.where(hit, lanef, float(LANES)), axis=1, keepdims=True)
    is_g = lane < N_GROUPS
    gl = jnp.where(is_g, logit, LOWEST)
    gmx = jnp.max(gl, axis=1, keepdims=True)
    grp = first_lane(gl == gmx)
    p_grp = 1.0 / jnp.sum(jnp.where(is_g, jnp.exp(gl - gmx), 0.0), axis=1, keepdims=True)
    lane_grp = jnp.right_shift(lane - EXPERT_LANE0, int(math.log2(EXPERTS_PER_GROUP)))
    in_grp = lane_grp.astype(F32) == grp
    el = jnp.where(in_grp, logit, LOWEST)
    ee = jnp.where(in_grp, jnp.exp(el - jnp.max(el, axis=1, keepdims=True)), 0.0)
    pe = jnp.where(in_grp, ee / jnp.sum(ee, axis=1, keepdims=True), -1.0)
    p1 = jnp.max(pe, axis=1, keepdims=True)
    l1 = first_lane(pe == p1)
    pe2 = jnp.where(lanef == l1, -1.0, pe)
    p2 = jnp.max(pe2, axis=1, keepdims=True)
    l2 = first_lane(pe2 == p2)
    den = p1 + p2
    o1, o2 = lanef == l1, lanef == l2
    onehot = jnp.where(o1 | o2, 1.0, 0.0)
    r = lax.broadcasted_iota(jnp.int32, (tm, tm), 0)
    c = lax.broadcasted_iota(jnp.int32, (tm, tm), 1)
    earlier = jnp.where(c < r, 1.0, 0.0).astype(BF16)
    prefix = jnp.dot(earlier, onehot.astype(BF16), preferred_element_type=F32) + carry[0:1, :]
    rank1 = jnp.sum(jnp.where(o1, prefix, 0.0), axis=1, keepdims=True)
    rank2 = jnp.sum(jnp.where(o2, prefix, 0.0), axis=1, keepdims=True)
    carry[0:1, :] = carry[0:1, :] + jnp.sum(onehot, axis=0, keepdims=True)
    fields = [l1 - EXPERT_LANE0, l2 - EXPERT_LANE0, rank1, rank2, p_grp * p1 / den, p_grp * p2 / den]
    rec = jnp.zeros((tm, LANES), F32)
    for j, f in enumerate(fields):
        rec = jnp.where(lane == j, f, rec)
    return rec


def _pack_bf16_pairs(x):
    w = x.shape[1] // 2
    return pltpu.pack_elementwise([x[:, :w], x[:, w:]], packed_dtype=jnp.bfloat16)


def _rows_to_tiles(ref, index, p):
    r = p.shape[0]
    for s in range(SUBLANES):
        ref[index + (pl.ds(s, r, stride=SUBLANES), slice(None))] = p[:, s * LANES:(s + 1) * LANES]


def _tiles_to_rows(ref, index, r):
    return jnp.concatenate([ref[index + (pl.ds(s, r, stride=SUBLANES), slice(None))] for s in range(SUBLANES)],
                           axis=1)


def _unpack_bf16_pairs(p):
    halves = [pltpu.unpack_elementwise(p, index=i, packed_dtype=jnp.bfloat16, unpacked_dtype=F32) for i in (0, 1)]
    return jnp.concatenate(halves, axis=1)


def _wo_ln_route_kernel(fr_ref, fn_ref, x_ref, w_ref, g_ref, b_ref, wr_ref, br_ref, base_ref,
                        h_ref, hp_ref, route_ref, cnt_ref, carry):
    i = pl.program_id(0)

    @pl.when(i == 0)
    def _():
        carry[...] = base_ref[...]

    y = (DEEPNORM_ALPHA * x_ref[...]
         + jnp.dot(fr_ref[...].astype(BF16), w_ref[0:RET_W, :], preferred_element_type=F32)
         + jnp.dot(fn_ref[...].astype(BF16), w_ref[RET_W:MIX_W, :], preferred_element_type=F32))
    h = _layer_norm_rows(y, g_ref[...], b_ref[...])
    h_ref[...] = h
    _rows_to_tiles(hp_ref, (), _pack_bf16_pairs(h))
    route_ref[...] = _route(h, wr_ref, br_ref, carry)

    @pl.when(i == pl.num_programs(0) - 1)
    def _():
        cnt_ref[...] = carry[...]


def _wo_ln_route(f_ret, f_nsa, x, w_o_bf16, ln_g, ln_b, w_route, b_route, base_counts):
    t, d = x.shape
    tm = min(WO_TM, t)
    assert t % tm == 0
    row = lambda w: pl.BlockSpec((tm, w), lambda i: (i, 0))
    full = lambda a: pl.BlockSpec(a.shape, lambda i: (0,) * a.ndim)
    lg, lb = ln_g.reshape(1, d), ln_b.reshape(1, d)
    return pl.pallas_call(
        _wo_ln_route_kernel,
        out_shape=[jax.ShapeDtypeStruct((t, d), F32), jax.ShapeDtypeStruct((t * SUBLANES, LANES), jnp.uint32),
                   jax.ShapeDtypeStruct((t, LANES), F32), jax.ShapeDtypeStruct((SUBLANES, LANES), F32)],
        grid=(t // tm,),
        in_specs=[row(RET_W), row(NSA_W), row(d), full(w_o_bf16), full(lg), full(lb),
                  full(w_route), full(b_route), full(base_counts)],
        out_specs=[row(d), pl.BlockSpec((tm * SUBLANES, LANES), lambda i: (i, 0)), row(LANES),
                   pl.BlockSpec((SUBLANES, LANES), lambda i: (0, 0))],
        scratch_shapes=[pltpu.VMEM((SUBLANES, LANES), F32)],
        compiler_params=pltpu.CompilerParams(dimension_semantics=("arbitrary",), vmem_limit_bytes=VMEM_LIMIT),
        name="wo_ln1_route",
    )(f_ret, f_nsa, x, w_o_bf16, lg, lb, w_route, b_route, base_counts)


MOE_BM = 256
MOE_SUB = 64
TABLE_UNROLL = 8


def _expert_kernel(blk_e_ref, n_used_ref, slot_ref, h_ref, wg_ref, wu_ref, wd_ref, y_ref,
                   src_tok, dst_row, xbuf, obuf, gsem, ssem, *, plane, n_rows):
    del blk_e_ref
    n_sub, sub = xbuf.shape[1], xbuf.shape[2] // SUBLANES
    bm = n_sub * sub
    i = pl.program_id(0)
    n_used = n_used_ref[0]
    n_asg = slot_ref.shape[0]
    dump0 = EXPERT_TOPK * plane
    assert bm & (bm - 1) == 0

    tile = lambda row: pl.ds(pl.multiple_of(row * SUBLANES, SUBLANES), SUBLANES)

    def gather(blk, buf_slot):
        def sub_block(j, carry):
            base = blk * bm + j * sub
            for u in range(sub):
                pltpu.make_async_copy(h_ref.at[tile(src_tok[base + u]), :],
                                      xbuf.at[buf_slot, j, tile(u), :], gsem.at[buf_slot]).start(priority=u % 2)
            return carry
        lax.fori_loop(0, n_sub, sub_block, 0)

    def scatter(blk, buf_slot):
        def sub_block(j, carry):
            base = blk * bm + j * sub
            for u in range(sub):
                pltpu.make_async_copy(obuf.at[buf_slot, j, tile(u), :],
                                      y_ref.at[tile(dst_row[base + u]), :], ssem.at[buf_slot]).start(priority=u % 2)
            return carry
        lax.fori_loop(0, n_sub, sub_block, 0)

    def wait_block(buf, sem, buf_slot):
        pltpu.make_async_copy(buf.at[buf_slot], buf.at[buf_slot], sem.at[buf_slot]).wait()

    @pl.when(i == 0)
    def _():
        def clear(t, carry):
            for u in range(TABLE_UNROLL):
                r = t * TABLE_UNROLL + u
                src_tok[r] = 0
                dst_row[r] = dump0 + jnp.bitwise_and(r, 2 * bm - 1)
            return carry
        lax.fori_loop(0, n_used_ref[1], clear, 0)

        def fill(t, carry):
            for u in range(TABLE_UNROLL):
                a = t * TABLE_UNROLL + u
                tok = jnp.right_shift(a, 1)
                src_tok[slot_ref[a]] = tok
                dst_row[slot_ref[a]] = jnp.bitwise_and(a, 1) * plane + tok
            return carry
        lax.fori_loop(0, n_used_ref[2], fill, 0)
        gather(0, 0)
        n_tok = h_ref.shape[0] // SUBLANES
        tail = plane - n_tok
        assert tail >= 0
        obuf[1] = jnp.zeros(obuf.shape[1:], obuf.dtype)
        spans = [(dump0, 2 * bm)] + ([(k * plane + n_tok, tail) for k in range(EXPERT_TOPK)] if tail else [])
        copies = []
        for first, count in spans:
            for j in range(-(-count // sub)):
                rows = min(sub, count - j * sub)
                copies.append(pltpu.make_async_copy(
                    obuf.at[1, j % n_sub, pl.ds(0, rows * SUBLANES), :],
                    y_ref.at[pl.ds((first + j * sub) * SUBLANES, rows * SUBLANES), :], ssem.at[1]))
        for cp in copies:
            cp.start()
        for cp in copies:
            cp.wait()

    slot = jnp.bitwise_and(i, 1)

    @pl.when(i + 1 < n_used)
    def _():
        gather(i + 1, 1 - slot)

    @pl.when(i < n_used)
    def _():
        wait_block(xbuf, gsem, slot)

        @pl.when(i >= 2)
        def _():
            wait_block(obuf, ssem, slot)

        packed = jnp.concatenate([_tiles_to_rows(xbuf, (slot, j), sub) for j in range(n_sub)], axis=0)
        xb = _unpack_bf16_pairs(packed).astype(BF16)
        hg = jnp.dot(xb, wg_ref[0].astype(BF16), preferred_element_type=F32)
        hu = jnp.dot(xb, wu_ref[0].astype(BF16), preferred_element_type=F32)
        hb = hg * (1.0 / (1.0 + jnp.exp(-hg))) * hu
        yb = jnp.dot(hb.astype(BF16), wd_ref[0].astype(BF16), preferred_element_type=F32)
        yp = _pack_bf16_pairs(yb)
        for j in range(n_sub):
            _rows_to_tiles(obuf, (slot, j), yp[j * sub:(j + 1) * sub])
        scatter(i, slot)

    @pl.when(i == pl.num_programs(0) - 1)
    def _():
        @pl.when(n_used >= 2)
        def _():
            wait_block(obuf, ssem, jnp.bitwise_and(n_used, 1))
        wait_block(obuf, ssem, jnp.bitwise_and(n_used - 1, 1))


def _expert_ffn(h, slot, blk_e, n_used, w_gate, w_up, w_down, plane):
    t = h.shape[0] // SUBLANES
    dp = SUBLANES * LANES
    d = 2 * dp
    assert w_gate.shape[1] == d
    n_asg = slot.shape[0]
    assert n_asg == t * EXPERT_TOPK and EXPERT_TOPK == 2
    n_blk = -(-(n_asg + N_EXPERTS * (MOE_BM - 1)) // MOE_BM)
    de = w_gate.shape[2]
    assert n_asg % TABLE_UNROLL == 0 and MOE_BM % TABLE_UNROLL == 0
    n_used = jnp.concatenate([n_used, jnp.array([n_blk * MOE_BM // TABLE_UNROLL, n_asg // TABLE_UNROLL], jnp.int32)])
    wspec = lambda shape: pl.BlockSpec((1,) + shape, lambda i, be, nu, sl: (be[i], 0, 0))
    return pl.pallas_call(
        functools.partial(_expert_kernel, plane=plane, n_rows=n_blk * MOE_BM),
        out_shape=jax.ShapeDtypeStruct(((EXPERT_TOPK * plane + 2 * MOE_BM) * SUBLANES, LANES), jnp.uint32),
        grid_spec=pltpu.PrefetchScalarGridSpec(
            num_scalar_prefetch=3,
            grid=(n_blk,),
            in_specs=[pl.BlockSpec(memory_space=pl.ANY), wspec((d, de)), wspec((d, de)), wspec((de, d))],
            out_specs=pl.BlockSpec(memory_space=pl.ANY),
            scratch_shapes=[pltpu.SMEM((n_blk * MOE_BM,), jnp.int32), pltpu.SMEM((n_blk * MOE_BM,), jnp.int32),
                            pltpu.VMEM((2, MOE_BM // MOE_SUB, MOE_SUB * SUBLANES, LANES), jnp.uint32),
                            pltpu.VMEM((2, MOE_BM // MOE_SUB, MOE_SUB * SUBLANES, LANES), jnp.uint32),
                            pltpu.SemaphoreType.DMA((2,)), pltpu.SemaphoreType.DMA((2,))]),
        compiler_params=pltpu.CompilerParams(dimension_semantics=("arbitrary",), vmem_limit_bytes=VMEM_LIMIT),
        name="expert_ffn",
    )(blk_e, n_used, slot, h, w_gate, w_up, w_down)


def _moe_ln_kernel(h_ref, y0_ref, y1_ref, route_ref, g_ref, b_ref, o_ref):
    rec = route_ref[...]
    tm = h_ref.shape[0]
    y = (DEEPNORM_ALPHA * h_ref[...]
         + rec[:, R_GATE:R_GATE + 1] * _unpack_bf16_pairs(_tiles_to_rows(y0_ref, (), tm))
         + rec[:, R_GATE + 1:R_GATE + 2] * _unpack_bf16_pairs(_tiles_to_rows(y1_ref, (), tm)))
    o_ref[...] = _layer_norm_rows(y, g_ref[...], b_ref[...])


def _moe_ln(h, y, route, ln_g, ln_b, row0, plane):
    n_rows, d = h.shape
    tm = min(WO_TM, n_rows)
    assert n_rows % tm == 0 and row0 % tm == 0 and plane % tm == 0 and EXPERT_TOPK == 2
    off = row0 // tm
    row = lambda w, o: pl.BlockSpec((tm, w), lambda i: (i + o, 0))
    tiles = lambda o: pl.BlockSpec((tm * SUBLANES, LANES), lambda i: (i + o, 0))
    vec = pl.BlockSpec((1, d), lambda i: (0, 0))
    return pl.pallas_call(
        _moe_ln_kernel,
        out_shape=jax.ShapeDtypeStruct((n_rows, d), F32),
        grid=(n_rows // tm,),
        in_specs=[row(d, 0), tiles(off), tiles(off + plane // tm), row(LANES, off), vec, vec],
        out_specs=row(d, 0),
        compiler_params=pltpu.CompilerParams(dimension_semantics=("arbitrary",), vmem_limit_bytes=VMEM_LIMIT),
        name="moe_ln2",
    )(h, y, y, route, ln_g.reshape(1, d), ln_b.reshape(1, d))


def _layer_norm(x, g, b):
    xf = x.astype(F32)
    mu = xf.mean(-1, keepdims=True)
    var = jnp.square(xf - mu).mean(-1, keepdims=True)
    return ((xf - mu) * lax.rsqrt(var + LN_EPS) * g + b).astype(x.dtype)


def _rope(x, pos, rot_dim, theta):
    half = rot_dim // 2
    inv = theta ** (-jnp.arange(0, rot_dim, 2, dtype=F32) / rot_dim)
    ang = pos[..., None].astype(F32) * inv
    cos = jnp.cos(ang)[:, :, None, :]
    sin = jnp.sin(ang)[:, :, None, :]
    xr = x[..., :rot_dim].astype(F32)
    x1, x2 = xr[..., :half], xr[..., half:]
    rot = jnp.concatenate([x1 * cos - x2 * sin, x2 * cos + x1 * sin], -1).astype(x.dtype)
    return jnp.concatenate([rot, x[..., rot_dim:]], -1)


def _heads(t, n):
    return t.reshape(t.shape[0], t.shape[1], n, HEAD_DIM)


def _chunk_retention(q, k, v, s0):
    n, l, h, d = q.shape
    c = RET_CHUNK if l % RET_CHUNK == 0 else l
    nc = l // c
    log_g = jnp.log1p(-jnp.exp2(-5.0 - jnp.arange(h, dtype=F32)))
    i = jnp.arange(c, dtype=F32)
    rel = i[:, None] - i[None, :]
    dmask = jnp.where(rel[None] >= 0, jnp.exp(jnp.maximum(rel[None], 0.0) * log_g[:, None, None]), 0.0)
    q_dec = jnp.exp((i + 1.0)[None] * log_g[:, None])[..., None]
    k_dec = jnp.exp((c - 1.0 - i)[None] * log_g[:, None])[..., None]
    c_dec = jnp.exp(c * log_g)[:, None, None]

    def to_chunks(t):
        return t.astype(F32).reshape(n, nc, c, h, d).transpose(1, 0, 3, 2, 4)

    def step(s, qkv):
        qc, kc, vc = qkv
        att = jnp.einsum('bhid,bhjd->bhij', qc, kc) * dmask
        o = jnp.einsum('bhij,bhjd->bhid', att, vc) + jnp.einsum('bhid,bhde->bhie', qc * q_dec, s)
        s = c_dec * s + jnp.einsum('bhjd,bhje->bhde', kc * k_dec, vc)
        return s, o

    s, o = lax.scan(step, s0.astype(F32), (to_chunks(q), to_chunks(k), to_chunks(v)))
    return o.transpose(1, 0, 3, 2, 4).reshape(n, l, h, d), s


def _retention_group(rq, rk, rv, rg, pos, s0, gn_g):
    q = _rope(_heads(rq, RET_HEADS), pos, HEAD_DIM, RET_ROPE_THETA)
    k = _rope(_heads(rk, RET_HEADS), pos, HEAD_DIM, RET_ROPE_THETA) * (HEAD_DIM ** -0.5)
    v = _heads(rv, RET_HEADS)
    o, s = _chunk_retention(q, k, v, s0)
    mu = o.mean(-1, keepdims=True)
    var = jnp.square(o - mu).mean(-1, keepdims=True)
    on = (o - mu) * lax.rsqrt(var + GN_EPS) * gn_g.reshape(RET_HEADS, HEAD_DIM).astype(F32)
    out = jax.nn.silu(rg.astype(F32)) * on.reshape(rg.shape)
    return out.astype(rq.dtype), s


def _gqa_attend(q, k, v, mask):
    n, lq, h, d = q.shape
    kv = k.shape[2]
    qg = q.reshape(n, lq, kv, h // kv, d)
    s = jnp.einsum('nqkgd,nskd->nkgqs', qg, k).astype(F32) * (d ** -0.5)
    m = mask[:, None, None]
    p = jax.nn.softmax(jnp.where(m, s, NEG), axis=-1) * m
    o = jnp.einsum('nkgqs,nskd->nqkgd', p.astype(v.dtype), v)
    return o.reshape(n, lq, h, d), p


def _nsa_heads(nq, ck, sk, wk, cv, sv, wv, pos):
    rp = lambda t, nh: _rope(_heads(t, nh), pos, ROT_DIM, ROPE_THETA)
    return (rp(nq, NSA_HEADS), rp(ck, NSA_KV_HEADS), rp(sk, NSA_KV_HEADS), rp(wk, NSA_KV_HEADS),
            _heads(cv, NSA_KV_HEADS), _heads(sv, NSA_KV_HEADS), _heads(wv, NSA_KV_HEADS))


def _compress(rows, w):
    n, t, kv, d = rows.shape
    return jnp.einsum('nbjkd,jd->nbkd', rows.reshape(n, t // CMP_BLOCK, CMP_BLOCK, kv, d), w)


def _cmp_branch(q, pos, kc, vc):
    nb = kc.shape[1]
    blk_end = (jnp.arange(nb) + 1) * CMP_BLOCK - 1
    mask = blk_end[None, None, :] <= pos[:, :, None]
    o, p = _gqa_attend(q, kc, vc, mask)
    imp = p.sum(axis=2).transpose(0, 2, 1, 3)
    return o, imp


def _select_blocks(imp, pos, n_sel):
    nb = imp.shape[-1]
    imp = jnp.pad(imp, ((0, 0), (0, 0), (0, 0), (0, n_sel - nb)))
    blk = jnp.arange(n_sel)
    cur = (pos // CMP_BLOCK)[:, :, None, None]
    forced = (blk == 0) | (blk == cur) | (blk == cur - 1)
    score = jnp.where(blk > cur, NEG, jnp.where(forced, -NEG, imp))
    top, idx = lax.top_k(score, min(SEL_TOPK, n_sel))
    return idx, top > 0.5 * NEG


def _sel_attend(q, pos, ks, vs, idx, valid):
    n, lq, kv, kk, cb, d = ks.shape
    h = q.shape[2]
    kpos = idx[..., None] * CMP_BLOCK + jnp.arange(CMP_BLOCK)
    m = ((kpos <= pos[:, :, None, None, None]) & valid[..., None]).reshape(n, lq, kv, 1, kk * cb)
    qg = q.reshape(n, lq, kv, h // kv, d)
    kf = ks.reshape(n, lq, kv, kk * cb, d)
    vf = vs.reshape(n, lq, kv, kk * cb, d)
    s = jnp.einsum('nqkgd,nqkjd->nqkgj', qg, kf).astype(F32) * (d ** -0.5)
    p = jax.nn.softmax(jnp.where(m, s, NEG), axis=-1) * m
    o = jnp.einsum('nqkgj,nqkjd->nqkgd', p.astype(vf.dtype), vf)
    return o.reshape(n, lq, h, d)


def _sel_prompt(q, pos, k, v, idx, valid):
    b, s, h, d = q.shape
    kv = k.shape[2]
    nb = s // CMP_BLOCK
    nq = s // SEL_Q_BLOCK
    kb = k.reshape(b, nb, CMP_BLOCK, kv, d).transpose(0, 3, 1, 2, 4)
    vb = v.reshape(b, nb, CMP_BLOCK, kv, d).transpose(0, 3, 1, 2, 4)
    bi = jnp.arange(b)[:, None, None, None]
    hi = jnp.arange(kv)[None, None, :, None]

    def blockwise(t):
        return t.reshape(t.shape[0], nq, SEL_Q_BLOCK, *t.shape[2:]).swapaxes(0, 1)

    def one(args):
        qc, pc, ic, vc = args
        return _sel_attend(qc, pc, kb[bi, hi, ic], vb[bi, hi, ic], ic, vc)

    o = lax.map(one, (blockwise(q), blockwise(pos), blockwise(idx), blockwise(valid)))
    return o.swapaxes(0, 1).reshape(b, s, h, d)


def _win_prompt(q, k, v):
    b, s, h, d = q.shape
    kv = k.shape[2]
    nb = s // WIN_Q_BLOCK
    nprev = WINDOW // WIN_Q_BLOCK
    nw = nprev + 1
    padw = ((0, 0), (WINDOW, 0), (0, 0), (0, 0))
    kp = jnp.pad(k, padw).reshape(b, nb + nprev, WIN_Q_BLOCK, kv, d)
    vp = jnp.pad(v, padw).reshape(b, nb + nprev, WIN_Q_BLOCK, kv, d)
    kw = jnp.concatenate([kp[:, i:i + nb] for i in range(nw)], axis=2)
    vw = jnp.concatenate([vp[:, i:i + nb] for i in range(nw)], axis=2)
    qpos = jnp.arange(s).reshape(nb, WIN_Q_BLOCK)
    kpos = (jnp.arange(nb)[:, None] - nprev) * WIN_Q_BLOCK + jnp.arange(nw * WIN_Q_BLOCK)[None]
    qq, kk = qpos[:, :, None], kpos[:, None, :]
    mask = (kk <= qq) & (qq - kk < WINDOW) & (kk >= 0)
    mask = jnp.broadcast_to(mask[None], (b,) + mask.shape).reshape(b * nb, WIN_Q_BLOCK, nw * WIN_Q_BLOCK)
    o, _ = _gqa_attend(q.reshape(b * nb, WIN_Q_BLOCK, h, d), kw.reshape(b * nb, nw * WIN_Q_BLOCK, kv, d),
                       vw.reshape(b * nb, nw * WIN_Q_BLOCK, kv, d), mask)
    return o.reshape(b, s, h, d)


def _gather_selected(pool, new_rows, page_table, idx):
    n, l, kv, d = new_rows.shape
    n_pages = page_table.shape[1]
    past_blocks = n_pages * PAGE_SIZE // CMP_BLOCK
    nbn = -(-l // CMP_BLOCK)
    newb = jnp.pad(new_rows, ((0, 0), (0, nbn * CMP_BLOCK - l), (0, 0), (0, 0)))
    newb = newb.reshape(n, nbn, CMP_BLOCK, kv, d).transpose(0, 3, 1, 2, 4)
    bi = jnp.arange(n)[:, None, None, None]
    hi = jnp.arange(kv)[None, None, :, None]
    start = idx * CMP_BLOCK
    phys = page_table[bi, jnp.minimum(start // PAGE_SIZE, n_pages - 1)]
    off = (start % PAGE_SIZE)[..., None] + jnp.arange(CMP_BLOCK)
    past = pool[phys[..., None], off, hi[..., None]]
    new = newb[bi, hi, jnp.clip(idx - past_blocks, 0, nbn - 1)]
    return jnp.where((idx < past_blocks)[..., None, None], past, new)


def _nsa_combine(gl, o_cmp, o_sel, o_win):
    n, l = gl.shape[0], gl.shape[1]
    g = jax.nn.sigmoid(gl.astype(F32)).reshape(n, l, NSA_HEADS, 3, 1)
    o = g[..., 0, :] * o_cmp + g[..., 1, :] * o_sel + g[..., 2, :] * o_win
    return o.reshape(n, l, NSA_W).astype(o_cmp.dtype)


def _prompt_mixer(x, win_buf, w_in, w_cmp_k, w_cmp_v, gn_g):
    n, s, _ = x.shape
    main, (ck, cv, sk, sv, wk, wv) = _project(x, w_in, jnp.arange(s))
    ret_out, s_fin = _retention_prompt(main, gn_g, n, s)
    kc, vc = _compress_prompt(main, w_cmp_k, w_cmp_v)
    nsa = _nsa_prompt(main, kc.reshape(n, s // CMP_BLOCK, KV_W), vc.reshape(n, s // CMP_BLOCK, KV_W), n, s)
    feats = (ret_out, nsa)
    if s >= win_buf:
        bk, bv = wk[:, s - win_buf:], wv[:, s - win_buf:]
    else:
        padb = ((0, 0), (win_buf - s, 0), (0, 0), (0, 0))
        bk, bv = jnp.pad(wk, padb), jnp.pad(wv, padb)
    return feats, (ck, cv, sk, sv, bk, bv, s_fin)


def _sample_mixer(x, c_cmp_k, c_cmp_v, c_sel_k, c_sel_v, c_win_k, c_win_v, s_ret, page_table,
                  w_in, w_cmp_k, w_cmp_v, gn_g):
    n, l, _ = x.shape
    past = page_table.shape[1] * PAGE_SIZE
    pos = past + jnp.arange(l)[None]
    assert l == 1
    main, (ck, cv, sk, sv, wk, wv) = _project(x, w_in, jnp.full((n,), past, jnp.int32))
    rq, rk, rv, rg, nq = _split_main(main, n, l)[:5]
    gl = _gate_logits(main, n, l)
    ret_out, s_new = _retention_group(rq, rk, rv, rg, pos, s_ret, gn_g)
    q = _heads(nq, NSA_HEADS)
    assert l == 1 and past % CMP_BLOCK == 0 and c_win_k.shape[1] <= WINDOW
    kc, vc = _compress_paged(c_cmp_k, c_cmp_v, page_table, w_cmp_k, w_cmp_v)
    o_cmp, sel = _sample_select(q[:, 0], kc, vc, past)
    sel = sel[:, :SEL_TOPK].reshape(-1)
    gl_pad = jnp.pad(gl.reshape(n, NSA_HEADS, 3), ((0, 0), (0, 0), (0, LANES - 3)))
    nsa = _sample_attend(sel, page_table, q[:, 0], sk[:, 0], sv[:, 0], wk[:, 0], wv[:, 0],
                         _interleaved(c_win_k), _interleaved(c_win_v), _interleaved(c_sel_k), _interleaved(c_sel_v),
                         o_cmp, gl_pad, past)
    feats = (ret_out.reshape(n * l, RET_W), nsa.reshape(n * l, NSA_W))
    kw = jnp.concatenate([c_win_k, wk], 1)
    vw = jnp.concatenate([c_win_v, wv], 1)
    return feats, (ck, cv, sk, sv, kw[:, l:], vw[:, l:], s_new)


def _route_params(w_group, b_group, w_expert, b_expert):
    w = jnp.concatenate([w_group, w_expert], axis=1)
    b = jnp.concatenate([b_group, b_expert], axis=0)
    pad = LANES - w.shape[1]
    return jnp.pad(w, ((0, 0), (0, pad))).astype(BF16), jnp.pad(b, (0, pad)).reshape(1, LANES)


def _dispatch_plan(route, counts_tile):
    counts = counts_tile[0, EXPERT_LANE0:EXPERT_LANE0 + N_EXPERTS].astype(jnp.int32)
    padded = (counts + MOE_BM - 1) // MOE_BM * MOE_BM
    pad_end = jnp.cumsum(padded)
    pad_start = pad_end - padded
    n_asg = route.shape[0] * EXPERT_TOPK
    n_blk = -(-(n_asg + N_EXPERTS * (MOE_BM - 1)) // MOE_BM)
    blk_first = jnp.arange(n_blk, dtype=jnp.int32) * MOE_BM
    blk_e = jnp.minimum(jnp.sum(pad_end[None, :] <= blk_first[:, None], axis=1), N_EXPERTS - 1).astype(jnp.int32)
    n_used = (pad_end[-1:] // MOE_BM).astype(jnp.int32)
    eid = route[:, R_EID:R_EID + EXPERT_TOPK].astype(jnp.int32)
    rank = route[:, R_RANK:R_RANK + EXPERT_TOPK].astype(jnp.int32)
    start = jnp.sum(jnp.where(eid[..., None] == jnp.arange(N_EXPERTS), pad_start, 0), axis=-1)
    return (start + rank).reshape(-1).astype(jnp.int32), blk_e, n_used


def kernel(x_prompt, x_sample, cache_cmp_k, cache_cmp_v, cache_sel_k, cache_sel_v, cache_win_k, cache_win_v,
           state_ret, page_table, w_in, w_cmp_k, w_cmp_v, ret_gn_g, w_o, ln1_g, ln1_b, w_group, b_group,
           w_expert, b_expert, w_gate, w_up, w_down, ln2_g, ln2_b):
    win_buf = cache_win_k.shape[2]
    hp, hs = x_prompt, x_sample
    acc_p = [[] for _ in range(7)]
    acc_s = [[] for _ in range(7)]
    for l in range(DEPTH):
        fp, st_p = _prompt_mixer(hp, win_buf, w_in[l], w_cmp_k[l], w_cmp_v[l], ret_gn_g[l])
        fs, st_s = _sample_mixer(hs, cache_cmp_k[l], cache_cmp_v[l], cache_sel_k[l], cache_sel_v[l],
                                 cache_win_k[l], cache_win_v[l], state_ret[l], page_table,
                                 w_in[l], w_cmp_k[l], w_cmp_v[l], ret_gn_g[l])
        w_o_bf16 = w_o[l].astype(BF16)
        w_route, b_route = _route_params(w_group[l], b_group[l], w_expert[l], b_expert[l])
        tp = hp.shape[0] * hp.shape[1]
        ts = hs.shape[0] * hs.shape[1]
        no_counts = jnp.zeros((SUBLANES, LANES), F32)
        h1p, packed_p, route_p, counts_p = _wo_ln_route(fp[0], fp[1], hp.reshape(tp, D_MODEL), w_o_bf16, ln1_g[l],
                                                        ln1_b[l], w_route, b_route, no_counts)
        h1s, packed_s, route_s, counts = _wo_ln_route(fs[0], fs[1], hs.reshape(ts, D_MODEL), w_o_bf16, ln1_g[l],
                                                      ln1_b[l], w_route, b_route, counts_p)
        packed = jnp.concatenate([packed_p, packed_s], axis=0)
        route = jnp.concatenate([route_p, route_s], axis=0)
        slot, blk_e, n_used = _dispatch_plan(route, counts)
        plane = -(-(tp + ts) // WO_TM) * WO_TM
        y = _expert_ffn(packed, slot, blk_e, n_used, w_gate[l], w_up[l], w_down[l], plane)
        hp = _moe_ln(h1p, y, route, ln2_g[l], ln2_b[l], 0, plane).reshape(hp.shape)
        hs = _moe_ln(h1s, y, route, ln2_g[l], ln2_b[l], tp, plane).reshape(hs.shape)
        for acc, t in zip(acc_p, st_p):
            acc.append(t)
        for acc, t in zip(acc_s, st_s):
            acc.append(t)
    p_cmp_k, p_cmp_v, p_sel_k, p_sel_v, p_win_k, p_win_v, p_ret = [jnp.stack(a) for a in acc_p]
    s_cmp_k, s_cmp_v, s_sel_k, s_sel_v, s_win_k, s_win_v, s_ret = [jnp.stack(a) for a in acc_s]
    return (hp, hs, p_cmp_k, p_cmp_v, p_sel_k, p_sel_v, p_win_k, p_win_v, p_ret.astype(state_ret.dtype),
            s_cmp_k, s_cmp_v, s_sel_k, s_sel_v, s_win_k, s_win_v, s_ret.astype(state_ret.dtype))
```

```python
import functools
import math

import jax
import jax.numpy as jnp
import numpy as np
from jax import lax
from jax.experimental import pallas as pl
from jax.experimental.pallas import tpu as pltpu

D_MODEL = 2048
DEPTH = 1
PAGE_SIZE = 128

F32 = jnp.float32
BF16 = jnp.bfloat16
HEAD_DIM = 128
RET_HEADS = D_MODEL // (2 * HEAD_DIM)
NSA_HEADS = D_MODEL // (2 * HEAD_DIM)
NSA_KV_HEADS = 2
RET_W = RET_HEADS * HEAD_DIM
NSA_W = NSA_HEADS * HEAD_DIM
KV_W = NSA_KV_HEADS * HEAD_DIM
MIX_W = RET_W + NSA_W
RET_CHUNK = 128
RET_ROPE_THETA = 10000.0
ROPE_THETA = 500000.0
ROT_DIM = HEAD_DIM // 4
CMP_BLOCK = 64
SEL_TOPK = 16
WINDOW = 512
WIN_Q_BLOCK = 128
SEL_Q_BLOCK = 64
N_GROUPS = 4
EXPERTS_PER_GROUP = 8
N_EXPERTS = N_GROUPS * EXPERTS_PER_GROUP
EXPERT_TOPK = 2
D_EXPERT = 512
MOE_BLOCK = 128
LN_EPS = 1e-5
GN_EPS = 1e-5
NEG = -1e30
DEEPNORM_ALPHA = (2 * DEPTH) ** 0.25
DEEPNORM_BETA = (8 * DEPTH) ** -0.25
SPLITS = (RET_W, RET_W, RET_W, RET_W, NSA_W, KV_W, KV_W, KV_W, KV_W, KV_W, KV_W, NSA_HEADS * 3)
IN_W = sum(SPLITS)
GATE_W = NSA_HEADS * 3
MAIN_W = IN_W - GATE_W
LANES = 128
VMEM_LIMIT = 48 * 1024 * 1024


PROJ_TN = 512
GL_COL0 = MAIN_W
PROJ_W = -(-(MAIN_W + NSA_KV_HEADS * LANES) // PROJ_TN) * PROJ_TN
ROTATED_SPLITS = (4, 5, 7, 9)


def _rotated_heads():
    cuts = np.cumsum((0,) + SPLITS)
    tiles = []
    for j in range(PROJ_W // PROJ_TN):
        heads = []
        for h in range(PROJ_TN // HEAD_DIM):
            c0 = j * PROJ_TN + h * HEAD_DIM
            split = int(np.searchsorted(cuts, c0, side="right")) - 1
            heads.append(split in ROTATED_SPLITS and c0 < MAIN_W)
        tiles.append(tuple(heads))
    return tiles


def _nsa_rope_tables(pos):
    half = ROT_DIM // 2
    inv = ROPE_THETA ** (-jnp.arange(0, ROT_DIM, 2, dtype=F32) / ROT_DIM)
    ang = pos[:, None].astype(F32) * inv
    cos, sin = jnp.cos(ang), jnp.sin(ang)
    rest = HEAD_DIM - ROT_DIM
    zeros = jnp.zeros((pos.shape[0], half), F32)
    pad = lambda t, fill: jnp.pad(t, ((0, 0), (0, rest)), constant_values=fill)
    return (pad(jnp.concatenate([cos, cos], 1), 1.0), pad(jnp.concatenate([zeros, sin], 1), 0.0),
            pad(jnp.concatenate([-sin, zeros], 1), 0.0))


KV_SPLITS = (5, 6, 7, 8, 9, 10)


def _kv_heads():
    cuts = np.cumsum((0,) + SPLITS)
    tiles = []
    for j in range(MAIN_W // PROJ_TN):
        heads = []
        for h in range(PROJ_TN // HEAD_DIM):
            c0 = j * PROJ_TN + h * HEAD_DIM
            split = int(np.searchsorted(cuts, c0, side="right")) - 1
            heads.append((KV_SPLITS.index(split), (c0 - int(cuts[split])) // HEAD_DIM) if split in KV_SPLITS else None)
        tiles.append(tuple(heads))
    return tiles


def _proj_kernel(x_ref, w_ref, wt_ref, cos_ref, up_ref, dn_ref, o_ref, *kv_refs, patterns, kv_heads):
    j = pl.program_id(1)
    n_main = MAIN_W // PROJ_TN
    xb = x_ref[...].astype(BF16)
    half = ROT_DIM // 2

    def rotated(acc, heads):
        cos, up, dn = cos_ref[...], up_ref[...], dn_ref[...]
        parts = []
        for h, rot in enumerate(heads):
            xh = acc[:, h * HEAD_DIM:(h + 1) * HEAD_DIM]
            if rot:
                xh = xh * cos + pltpu.roll(xh, half, 1) * up + pltpu.roll(xh, HEAD_DIM - half, 1) * dn
            parts.append(xh)
        return jnp.concatenate(parts, axis=1)

    plans = list(zip(patterns[:n_main], kv_heads))
    for plan in sorted(set(plans), key=repr):
        tiles = [t for t, p in enumerate(plans) if p == plan]
        hit = functools.reduce(jnp.logical_or, [j == t for t in tiles])

        @pl.when(hit)
        def _(plan=plan):
            heads, dests = plan
            acc = jnp.dot(xb, w_ref[...], preferred_element_type=F32)
            vals = rotated(acc, heads) if any(heads) else acc
            o_ref[...] = vals
            for h, dest in enumerate(dests):
                if dest is not None:
                    out, k = dest
                    kv_refs[out][pl.ds(k, vals.shape[0], stride=NSA_KV_HEADS), :] = (
                        vals[:, h * HEAD_DIM:(h + 1) * HEAD_DIM])

    @pl.when(j >= n_main)
    def _():
        o_ref[...] = jnp.dot(xb, wt_ref[...], preferred_element_type=F32)


def _project(x, w_in, pos):
    n, l, d = x.shape
    t = n * l
    xt = x.reshape(t, d)
    tm = min(1024, t)
    gpg = GATE_W // NSA_KV_HEADS
    n_main = MAIN_W // PROJ_TN
    assert PROJ_W == MAIN_W + PROJ_TN and not any(any(p) for p in _rotated_heads()[n_main:])
    gate_tiles = [jnp.pad(w_in[:, MAIN_W + k * gpg:MAIN_W + (k + 1) * gpg], ((0, 0), (0, LANES - gpg)))
                  for k in range(NSA_KV_HEADS)]
    fill = jnp.zeros((d, PROJ_TN - NSA_KV_HEADS * LANES), F32)
    w_tail = jnp.concatenate(gate_tiles + [fill], axis=1).astype(BF16)
    tables = _nsa_rope_tables(pos)
    pb = pos.shape[0] // tm
    tab = pl.BlockSpec((tm, HEAD_DIM), lambda i, j: (i % pb, 0))
    kv_rows = pl.BlockSpec((tm * NSA_KV_HEADS, HEAD_DIM), lambda i, j: (i, 0))
    outs = pl.pallas_call(
        functools.partial(_proj_kernel, patterns=_rotated_heads(), kv_heads=_kv_heads()),
        out_shape=[jax.ShapeDtypeStruct((t, PROJ_W), F32)]
        + [jax.ShapeDtypeStruct((t * NSA_KV_HEADS, HEAD_DIM), F32)] * len(KV_SPLITS),
        grid=(t // tm, PROJ_W // PROJ_TN),
        in_specs=[pl.BlockSpec((tm, d), lambda i, j: (i, 0)),
                  pl.BlockSpec((d, PROJ_TN), lambda i, j: (0, jnp.minimum(j, n_main - 1))),
                  pl.BlockSpec((d, PROJ_TN), lambda i, j: (0, 0)), tab, tab, tab],
        out_specs=[pl.BlockSpec((tm, PROJ_TN), lambda i, j: (i, j))] + [kv_rows] * len(KV_SPLITS),
        compiler_params=pltpu.CompilerParams(dimension_semantics=("arbitrary", "arbitrary"),
                                             vmem_limit_bytes=VMEM_LIMIT),
        name="in_proj",
    )(xt, w_in.astype(BF16), w_tail, *tables)
    return outs[0], [o.reshape(n, l, NSA_KV_HEADS, HEAD_DIM) for o in outs[1:]]


def _split_main(main, n, l, first=0):
    cuts = [0] + [int(c) for c in np.cumsum(SPLITS)[:-1]]
    return [main[:, cuts[i]:cuts[i + 1]].reshape(n, l, -1) for i in range(first, len(SPLITS) - 1)]


def _split_col0(i):
    return int(np.cumsum((0,) + SPLITS)[i])


def _gate_logits(main, n, l):
    gpg = GATE_W // NSA_KV_HEADS
    cols = [main[:, GL_COL0 + k * LANES:GL_COL0 + k * LANES + gpg] for k in range(NSA_KV_HEADS)]
    return jnp.concatenate(cols, axis=1).reshape(n, l, GATE_W)


def _compress_kernel(k_ref, v_ref, wk_ref, wv_ref, ko_ref, vo_ref):
    r = ko_ref.shape[0]
    ko_ref[...] = jnp.sum(k_ref[...].reshape(r, CMP_BLOCK, KV_W) * wk_ref[...][None], axis=1)
    vo_ref[...] = jnp.sum(v_ref[...].reshape(r, CMP_BLOCK, KV_W) * wv_ref[...][None], axis=1)


def _compress_prompt(main, w_cmp_k, w_cmp_v):
    t = main.shape[0]
    r = 32
    wk2 = jnp.tile(w_cmp_k, (1, NSA_KV_HEADS))
    wv2 = jnp.tile(w_cmp_v, (1, NSA_KV_HEADS))
    kcol, vcol = _split_col0(5) // KV_W, _split_col0(6) // KV_W
    wsp = pl.BlockSpec((CMP_BLOCK, KV_W), lambda i: (0, 0))
    osp = pl.BlockSpec((r, KV_W), lambda i: (i, 0))
    return pl.pallas_call(
        _compress_kernel,
        out_shape=[jax.ShapeDtypeStruct((t // CMP_BLOCK, KV_W), F32)] * 2,
        grid=(t // (r * CMP_BLOCK),),
        in_specs=[pl.BlockSpec((r * CMP_BLOCK, KV_W), lambda i: (i, kcol)),
                  pl.BlockSpec((r * CMP_BLOCK, KV_W), lambda i: (i, vcol)), wsp, wsp],
        out_specs=[osp, osp],
        compiler_params=pltpu.CompilerParams(dimension_semantics=("arbitrary",)),
        name="compress_prompt",
    )(main, main, wk2, wv2)


PAGES_PER_STEP = 16
BLOCKS_PER_PAGE = PAGE_SIZE // CMP_BLOCK


SUBLANES = 8
ROWS_PER_BLOCK = CMP_BLOCK * NSA_KV_HEADS
ROWS_PER_PAGE = PAGE_SIZE * NSA_KV_HEADS
BLOCKS_PER_TILE = SUBLANES // NSA_KV_HEADS


def _div(x, n):
    assert n & (n - 1) == 0
    return jnp.right_shift(x, n.bit_length() - 1)


def _mod(x, n):
    assert n & (n - 1) == 0
    return jnp.bitwise_and(x, n - 1)


def _interleaved(pool):
    return pool.reshape(pool.shape[:-3] + (pool.shape[-3] * NSA_KV_HEADS, HEAD_DIM))


def _compress_paged_kernel(pt_ref, pk_ref, pv_ref, wk_ref, wv_ref, ko_ref, vo_ref, kbuf, vbuf, sem):
    s, i = pl.program_id(0), pl.program_id(1)
    n_i = pl.num_programs(1)
    step = s * n_i + i
    last = pl.num_programs(0) * n_i - 1
    pair = _div(lax.broadcasted_iota(jnp.int32, (SUBLANES, HEAD_DIM), 0), NSA_KV_HEADS)

    def fetch(seq, blk, slot):
        for j in range(PAGES_PER_STEP):
            page = pt_ref[seq, blk * PAGES_PER_STEP + j]
            pltpu.make_async_copy(pk_ref.at[page], kbuf.at[slot, j], sem.at[slot]).start(priority=j % 2)
            pltpu.make_async_copy(pv_ref.at[page], vbuf.at[slot, j], sem.at[slot]).start(priority=(j + 1) % 2)

    @pl.when(step == 0)
    def _():
        fetch(0, 0, 0)

    slot = jnp.bitwise_and(step, 1)

    @pl.when(step < last)
    def _():
        wrap = i == n_i - 1
        fetch(jnp.where(wrap, s + 1, s), jnp.where(wrap, 0, i + 1), 1 - slot)

    pltpu.make_async_copy(kbuf.at[slot], kbuf.at[slot], sem.at[slot]).wait()
    pltpu.make_async_copy(vbuf.at[slot], vbuf.at[slot], sem.at[slot]).wait()

    def summaries(buf, w):
        sums = []
        for j in range(PAGES_PER_STEP):
            for b in range(BLOCKS_PER_PAGE):
                y = buf[slot, j, b * ROWS_PER_BLOCK:(b + 1) * ROWS_PER_BLOCK, :] * w
                acc = jnp.sum(y.reshape(ROWS_PER_BLOCK // SUBLANES, SUBLANES, HEAD_DIM), axis=0)
                shift = SUBLANES // 2
                while shift >= NSA_KV_HEADS:
                    acc = acc + pltpu.roll(acc, shift, 0)
                    shift //= 2
                sums.append(acc)
        tiles = []
        for t in range(len(sums) // BLOCKS_PER_TILE):
            tile = sums[t * BLOCKS_PER_TILE]
            for j in range(1, BLOCKS_PER_TILE):
                tile = jnp.where(pair == j, sums[t * BLOCKS_PER_TILE + j], tile)
            tiles.append(tile)
        return jnp.concatenate(tiles, axis=0)

    ko_ref[0] = summaries(kbuf, wk_ref[...])
    vo_ref[0] = summaries(vbuf, wv_ref[...])


def _compress_paged(pool_k, pool_v, page_table, w_cmp_k, w_cmp_v):
    n, n_pages = page_table.shape
    wk2 = jnp.repeat(w_cmp_k, NSA_KV_HEADS, axis=0)
    wv2 = jnp.repeat(w_cmp_v, NSA_KV_HEADS, axis=0)
    anywhere = pl.BlockSpec(memory_space=pl.ANY)
    wsp = pl.BlockSpec((ROWS_PER_BLOCK, HEAD_DIM), lambda s, i, pt: (0, 0))
    rows = PAGES_PER_STEP * BLOCKS_PER_PAGE * NSA_KV_HEADS
    osp = pl.BlockSpec((1, rows, HEAD_DIM), lambda s, i, pt: (s, i, 0))
    page_buf = pltpu.VMEM((2, PAGES_PER_STEP, ROWS_PER_PAGE, HEAD_DIM), F32)
    return pl.pallas_call(
        _compress_paged_kernel,
        out_shape=[jax.ShapeDtypeStruct((n, n_pages * BLOCKS_PER_PAGE * NSA_KV_HEADS, HEAD_DIM), F32)] * 2,
        grid_spec=pltpu.PrefetchScalarGridSpec(
            num_scalar_prefetch=1,
            grid=(n, n_pages // PAGES_PER_STEP),
            in_specs=[anywhere, anywhere, wsp, wsp],
            out_specs=[osp, osp],
            scratch_shapes=[page_buf, page_buf, pltpu.SemaphoreType.DMA((2,))]),
        compiler_params=pltpu.CompilerParams(dimension_semantics=("arbitrary", "arbitrary")),
        name="compress_paged",
    )(page_table, _interleaved(pool_k), _interleaved(pool_v), wk2, wv2)


SS_SEQ = 8
LOWEST = -3.0e38


def _sample_select_kernel(q_ref, kc_ref, vc_ref, ocmp_ref, sel_ref, *, pos):
    ss, nbk = q_ref.shape[0], kc_ref.shape[1]
    kv, g = NSA_KV_HEADS, NSA_GROUP
    head = lax.broadcasted_iota(jnp.int32, (NSA_HEADS, nbk), 0)
    col = lax.broadcasted_iota(jnp.int32, (NSA_HEADS, nbk), 1)
    m = (_mod(col, kv) == _div(head, g)) & ((_div(col, kv) + 1) * CMP_BLOCK - 1 <= pos)
    width = nbk + LANES
    rows = []
    for i in range(ss):
        q = (q_ref[i] * (HEAD_DIM ** -0.5)).astype(BF16)
        p = _masked_softmax(_dot_nt(q, kc_ref[i].astype(BF16)), m)
        ocmp_ref[i] = jnp.dot(p.astype(BF16), vc_ref[i].astype(BF16), preferred_element_type=F32)
        for k in range(kv):
            imp = jnp.sum(p[k * g:(k + 1) * g], axis=0, keepdims=True)
            rows.append(jnp.concatenate([imp, jnp.zeros((1, LANES), F32)], axis=1))
    nrow = ss * kv
    r_iota = lax.broadcasted_iota(jnp.int32, (nrow, width), 0)
    ccol = lax.broadcasted_iota(jnp.int32, (nrow, width), 1)
    cand = jnp.zeros((nrow, width), F32)
    for r, row in enumerate(rows):
        cand = jnp.where(r_iota == r, row, cand)
    cblk = _div(ccol, kv)
    cur = pos // CMP_BLOCK
    n_sel = -(-(pos + 1) // CMP_BLOCK)
    forced = (cblk == 0) | (cblk == cur) | (cblk == cur - 1)
    score = jnp.where(cblk > cur, NEG, jnp.where(forced, -NEG, cand))
    score = jnp.where((_mod(ccol, kv) == _mod(r_iota, kv)) & (cblk < n_sel), score, LOWEST)
    colf = ccol.astype(F32)
    lane = lax.broadcasted_iota(jnp.int32, (nrow, LANES), 1)
    sel = jnp.zeros((nrow, LANES), jnp.int32)
    for t in range(SEL_TOPK):
        mx = jnp.max(score, axis=1, keepdims=True)
        c = jnp.min(jnp.where(score == mx, colf, -LOWEST), axis=1, keepdims=True)
        picked = jnp.where(mx > 0.5 * NEG, _div(c.astype(jnp.int32), kv), -1)
        sel = jnp.where(lane == t, picked, sel)
        score = jnp.where(colf == c, LOWEST, score)
    sel_ref[...] = sel


def _sample_select(q, kc, vc, pos):
    n = q.shape[0]
    nbk = kc.shape[1]
    qsp = pl.BlockSpec((SS_SEQ, NSA_HEADS, HEAD_DIM), lambda i: (i, 0, 0))
    csp = pl.BlockSpec((SS_SEQ, nbk, HEAD_DIM), lambda i: (i, 0, 0))
    return pl.pallas_call(
        functools.partial(_sample_select_kernel, pos=pos),
        out_shape=[jax.ShapeDtypeStruct((n, NSA_HEADS, HEAD_DIM), F32),
                   jax.ShapeDtypeStruct((n * NSA_KV_HEADS, LANES), jnp.int32)],
        grid=(n // SS_SEQ,),
        in_specs=[qsp, csp, csp],
        out_specs=[qsp, pl.BlockSpec((SS_SEQ * NSA_KV_HEADS, LANES), lambda i: (i, 0))],
        compiler_params=pltpu.CompilerParams(dimension_semantics=("arbitrary",), vmem_limit_bytes=VMEM_LIMIT),
        name="sample_select",
    )(q, kc, vc)


N_SLOTS = NSA_KV_HEADS * SEL_TOPK


def _sample_attend_kernel(sel_ref, pt_ref, q_ref, knew_ref, vnew_ref, wknew_ref, wvnew_ref, wkb_ref, wvb_ref,
                          ocmp_ref, gl_ref, *rest, pos, past_blocks):
    del pt_ref
    kblk, vblk, o_ref = rest[:N_SLOTS], rest[N_SLOTS:2 * N_SLOTS], rest[2 * N_SLOTS]
    kv, g, rb = NSA_KV_HEADS, NSA_GROUP, ROWS_PER_BLOCK
    n = pl.program_id(0)
    q = q_ref[0] * (HEAD_DIM ** -0.5)
    row = lax.broadcasted_iota(jnp.int32, (rb, HEAD_DIM), 0)
    col = lax.broadcasted_iota(jnp.int32, (1, rb), 1)

    def new_block(ref):
        out = jnp.zeros((rb, HEAD_DIM), F32)
        for k in range(kv):
            out = jnp.where(row == k, ref[0, k:k + 1, :], out)
        return out

    def per_head_rows(ref):
        return jnp.concatenate([jnp.broadcast_to(ref[0, k:k + 1, :], (g, HEAD_DIM)) for k in range(kv)], axis=0)

    knew, vnew = new_block(knew_ref), new_block(vnew_ref)
    o_sel = []
    for k in range(kv):
        ks, vs, ms = [], [], []
        for j in range(SEL_TOPK):
            b = sel_ref[(n * kv + k) * SEL_TOPK + j]
            is_new = jnp.broadcast_to(b, (rb, HEAD_DIM)) >= past_blocks
            ks.append(jnp.where(is_new, knew, kblk[k * SEL_TOPK + j][0]))
            vs.append(jnp.where(is_new, vnew, vblk[k * SEL_TOPK + j][0]))
            first = jnp.where(b >= 0, b, 1 << 24) * CMP_BLOCK
            ms.append((first + _div(col, kv) <= pos) & (_mod(col, kv) == k))
        s = _dot_nt(q[k * g:(k + 1) * g].astype(BF16), jnp.concatenate(ks, axis=0).astype(BF16))
        p = _masked_softmax(s, jnp.concatenate(ms, axis=1))
        o_sel.append(jnp.dot(p.astype(BF16), jnp.concatenate(vs, axis=0).astype(BF16), preferred_element_type=F32))
    o_sel = jnp.concatenate(o_sel, axis=0)

    nwr = wkb_ref.shape[1]
    s_w = _dot_nt(q.astype(BF16), wkb_ref[0].astype(BF16))
    cw = lax.broadcasted_iota(jnp.int32, (NSA_HEADS, nwr), 1)
    hw = lax.broadcasted_iota(jnp.int32, (NSA_HEADS, nwr), 0)
    kpos = pos - nwr // kv + _div(cw, kv)
    mw = (kpos <= pos) & (pos - kpos < WINDOW) & (_mod(cw, kv) == _div(hw, g))
    s_n = jnp.sum(q * per_head_rows(wknew_ref), axis=1, keepdims=True)
    smw = jnp.where(mw, s_w, NEG)
    mx = jnp.maximum(jnp.max(smw, axis=1, keepdims=True), s_n)
    e_w = jnp.where(mw, jnp.exp(smw - mx), 0.0)
    e_n = jnp.exp(s_n - mx)
    den = jnp.sum(e_w, axis=1, keepdims=True) + e_n
    o_win = (jnp.dot(e_w.astype(BF16), wvb_ref[0].astype(BF16), preferred_element_type=F32)
             + e_n * per_head_rows(wvnew_ref)) / den

    gate = 1.0 / (1.0 + jnp.exp(-gl_ref[0]))
    o_ref[0] = gate[:, 0:1] * ocmp_ref[0] + gate[:, 1:2] * o_sel + gate[:, 2:3] * o_win


def _sample_attend(sel, page_table, q, sk, sv, wk, wv, win_k, win_v, pool_k, pool_v, o_cmp, gl, pos):
    n, n_pages = page_table.shape
    past_blocks = n_pages * BLOCKS_PER_PAGE

    def slot_spec(k, j):
        def imap(s, sel_r, pt_r):
            b = jnp.maximum(sel_r[(s * NSA_KV_HEADS + k) * SEL_TOPK + j], 0)
            page = jnp.minimum(_div(b, BLOCKS_PER_PAGE), n_pages - 1)
            return (pt_r[s * n_pages + page], _mod(b, BLOCKS_PER_PAGE), 0)
        return pl.BlockSpec((1, ROWS_PER_BLOCK, HEAD_DIM), imap)

    slots = [slot_spec(k, j) for k in range(NSA_KV_HEADS) for j in range(SEL_TOPK)]
    per_seq = lambda a: pl.BlockSpec((1,) + a.shape[1:], lambda s, sel_r, pt_r: (s, 0, 0))
    dense = [q, sk, sv, wk, wv, win_k, win_v, o_cmp, gl]
    return pl.pallas_call(
        functools.partial(_sample_attend_kernel, pos=pos, past_blocks=past_blocks),
        out_shape=jax.ShapeDtypeStruct((n, NSA_HEADS, HEAD_DIM), F32),
        grid_spec=pltpu.PrefetchScalarGridSpec(
            num_scalar_prefetch=2,
            grid=(n,),
            in_specs=[per_seq(a) for a in dense] + slots + slots,
            out_specs=pl.BlockSpec((1, NSA_HEADS, HEAD_DIM), lambda s, sel_r, pt_r: (s, 0, 0))),
        compiler_params=pltpu.CompilerParams(dimension_semantics=("arbitrary",), vmem_limit_bytes=VMEM_LIMIT),
        name="sample_attend",
    )(sel, page_table.reshape(-1), *dense, *([pool_k] * N_SLOTS), *([pool_v] * N_SLOTS))


def _retention_tables(seq):
    c = RET_CHUNK
    log_g = jnp.log1p(-jnp.exp2(-5.0 - jnp.arange(RET_HEADS, dtype=F32)))
    i = jnp.arange(c, dtype=F32)
    rel = i[:, None] - i[None, :]
    dmask = jnp.where(rel[None] >= 0, jnp.exp(jnp.maximum(rel[None], 0.0) * log_g[:, None, None]), 0.0)
    q_dec = jnp.exp((i + 1.0)[None] * log_g[:, None])[..., None]
    k_dec = jnp.exp((c - 1.0 - i)[None] * log_g[:, None])[..., None]
    c_dec = jnp.exp(c * log_g)[:, None, None]
    bc = lambda t: jnp.broadcast_to(t, (RET_HEADS, c, HEAD_DIM))
    inv = RET_ROPE_THETA ** (-jnp.arange(0, HEAD_DIM, 2, dtype=F32) / HEAD_DIM)
    ang = jnp.arange(seq)[:, None].astype(F32) * inv
    cos, sin = jnp.cos(ang), jnp.sin(ang)
    return (dmask, bc(q_dec), bc(k_dec), jnp.broadcast_to(c_dec, (RET_HEADS, 1, HEAD_DIM)),
            jnp.concatenate([cos, cos], -1), jnp.concatenate([-sin, sin], -1))


def _retention_kernel(q_ref, k_ref, v_ref, g_ref, cos_ref, sin_ref, dmask_ref, qdec_ref, kdec_ref, cdec_ref, gn_ref,
                      o_ref, st_ref, s_scr):
    c = pl.program_id(1)

    @pl.when(c == 0)
    def _():
        s_scr[...] = jnp.zeros(s_scr.shape, F32)

    cosf, sins = cos_ref[...], sin_ref[...]
    half = HEAD_DIM // 2
    for h in range(RET_HEADS):
        sl = slice(h * HEAD_DIM, (h + 1) * HEAD_DIM)
        qh, kh = q_ref[:, sl], k_ref[:, sl]
        qr = qh * cosf + pltpu.roll(qh, half, 1) * sins
        kr = (kh * cosf + pltpu.roll(kh, half, 1) * sins) * (HEAD_DIM ** -0.5)
        vb = v_ref[:, sl].astype(BF16)
        att = _dot_nt(qr.astype(BF16), kr.astype(BF16)) * dmask_ref[h]
        s_prev = s_scr[h]
        o = (jnp.dot(att.astype(BF16), vb, preferred_element_type=F32)
             + jnp.dot((qr * qdec_ref[h]).astype(BF16), s_prev.astype(BF16), preferred_element_type=F32))
        s_scr[h] = cdec_ref[h] * s_prev + lax.dot_general(
            (kr * kdec_ref[h]).astype(BF16), vb, (((0,), (0,)), ((), ())), preferred_element_type=F32)
        mu = jnp.mean(o, axis=-1, keepdims=True)
        var = jnp.mean(jnp.square(o - mu), axis=-1, keepdims=True)
        on = (o - mu) * lax.rsqrt(var + GN_EPS) * gn_ref[:, sl]
        gg = g_ref[:, sl]
        o_ref[:, sl] = gg * (1.0 / (1.0 + jnp.exp(-gg))) * on

    @pl.when(c == pl.num_programs(1) - 1)
    def _():
        st_ref[0] = s_scr[...]


def _retention_prompt(main, gn_g, batch, seq):
    nc = seq // RET_CHUNK
    dmask, q_dec, k_dec, c_dec, cosf, sins = _retention_tables(seq)
    col = lambda j: pl.BlockSpec((RET_CHUNK, RET_W), lambda b, c: (b * nc + c, j))
    pos_tab = pl.BlockSpec((RET_CHUNK, HEAD_DIM), lambda b, c: (c, 0))
    full = lambda a: pl.BlockSpec(a.shape, lambda b, c: (0,) * a.ndim)
    gn = gn_g.reshape(1, RET_W)
    return pl.pallas_call(
        _retention_kernel,
        out_shape=[jax.ShapeDtypeStruct((batch * seq, RET_W), F32),
                   jax.ShapeDtypeStruct((batch, RET_HEADS, HEAD_DIM, HEAD_DIM), F32)],
        grid=(batch, nc),
        in_specs=[col(0), col(1), col(2), col(3), pos_tab, pos_tab,
                  full(dmask), full(q_dec), full(k_dec), full(c_dec), full(gn)],
        out_specs=[pl.BlockSpec((RET_CHUNK, RET_W), lambda b, c: (b * nc + c, 0)),
                   pl.BlockSpec((1, RET_HEADS, HEAD_DIM, HEAD_DIM), lambda b, c: (b, 0, 0, 0))],
        scratch_shapes=[pltpu.VMEM((RET_HEADS, HEAD_DIM, HEAD_DIM), F32)],
        compiler_params=pltpu.CompilerParams(dimension_semantics=("arbitrary", "arbitrary"),
                                             vmem_limit_bytes=VMEM_LIMIT),
        name="retention_prompt",
    )(main, main, main, main, cosf, sins, dmask, q_dec, k_dec, c_dec, gn)


NSA_TQ = 128
NSA_KEY_CHUNK = 512
NSA_GROUP = NSA_HEADS // NSA_KV_HEADS


def _dot_nt(a, b):
    return lax.dot_general(a, b, (((1,), (1,)), ((), ())), preferred_element_type=F32)


def _masked_softmax(s, m):
    sm = jnp.where(m, s, NEG)
    e = jnp.exp(sm - jnp.max(sm, axis=-1, keepdims=True))
    return jnp.where(m, e / jnp.sum(e, axis=-1, keepdims=True), 0.0)


def _select_mask_t(imp_t, pos_t):
    nb = imp_t.shape[0]
    blk = lax.broadcasted_iota(jnp.int32, imp_t.shape, 0)
    cur = jnp.right_shift(pos_t, int(math.log2(CMP_BLOCK)))
    forced = (blk == 0) | (blk == cur) | (blk == cur - 1)
    score = jnp.where(blk > cur, NEG, jnp.where(forced, -NEG, imp_t))
    rank = jnp.zeros(score.shape, jnp.int32)
    for i in range(nb):
        si = score[i:i + 1, :]
        ahead = (si > score) | ((si == score) & (blk > i))
        rank = rank + ahead.astype(jnp.int32)
    return (rank < SEL_TOPK) & (score > 0.5 * NEG)


def _nsa_prompt_kernel(q_ref, kc_ref, vc_ref, sk_ref, sv_ref, wk_ref, wv_ref, gl_ref, o_ref, skt, wkt):
    tq, g, kc_n = NSA_TQ, NSA_GROUP, NSA_KEY_CHUNK
    qi = pl.program_id(2)
    nb = kc_ref.shape[1]

    @pl.when(qi == 0)
    def _():
        blk_row = lax.broadcasted_iota(jnp.int32, (nb, kc_n), 0)
        key_col = lax.broadcasted_iota(jnp.int32, (nb, kc_n), 1)
        for c in range(skt.shape[0]):
            member = jnp.right_shift(key_col + c * kc_n, int(math.log2(CMP_BLOCK))) == blk_row
            skt[c] = jnp.concatenate([sk_ref[c * kc_n:(c + 1) * kc_n, :].T.astype(BF16),
                                      jnp.where(member, 1.0, 0.0).astype(BF16)], axis=0)
        for w in range(wkt.shape[0]):
            wkt[w] = wk_ref[w * tq:(w + 1) * tq, :].T.astype(BF16)

    q4 = q_ref[...] * (HEAD_DIM ** -0.5)
    qs = jnp.concatenate([q4[:, i * HEAD_DIM:(i + 1) * HEAD_DIM] for i in range(g)], axis=0).astype(BF16)
    pos = qi * tq + lax.broadcasted_iota(jnp.int32, (tq, 1), 0)
    pos4 = jnp.concatenate([pos] * g, axis=0)

    assert tq == LANES
    s_c = _dot_nt(kc_ref[0].astype(BF16), qs)
    blk_t = lax.broadcasted_iota(jnp.int32, (nb, g * tq), 0)
    pos_t = qi * tq + lax.broadcasted_iota(jnp.int32, (1, tq), 1)
    m_c = (blk_t + 1) * CMP_BLOCK - 1 <= jnp.concatenate([pos_t] * g, axis=1)
    sm_c = jnp.where(m_c, s_c, NEG)
    e_c = jnp.exp(sm_c - jnp.max(sm_c, axis=0, keepdims=True))
    p_c = jnp.where(m_c, e_c / jnp.sum(e_c, axis=0, keepdims=True), 0.0)
    o_cmp = lax.dot_general(p_c.astype(BF16), vc_ref[0].astype(BF16), (((0,), (0,)), ((), ())),
                            preferred_element_type=F32)
    imp_t = p_c[:, 0:tq]
    for i in range(1, g):
        imp_t = imp_t + p_c[:, i * tq:(i + 1) * tq]

    seln_t = jnp.where(_select_mask_t(imp_t, pos_t), 0.0, NEG)
    seln = jnp.concatenate([seln_t, jnp.zeros((tq - nb, tq), F32)], axis=0).T[:, :nb].astype(BF16)
    q_aug = jnp.concatenate([qs, jnp.concatenate([seln] * g, axis=0)], axis=1)
    key_row = lax.broadcasted_iota(jnp.int32, (1, kc_n), 1)

    def scores(c):
        return jnp.dot(q_aug, skt[c], preferred_element_type=F32)

    def values(c):
        return sv_ref[pl.ds(pl.multiple_of(c * kc_n, kc_n), kc_n), :].astype(BF16)

    assert kc_n % tq == 0
    c_diag = _div(qi, kc_n // tq)
    s_d = jnp.where(key_row + c_diag * kc_n <= pos4, scores(c_diag), NEG)
    m_d = jnp.max(s_d, axis=-1, keepdims=True)
    p_d = jnp.exp(s_d - m_d)
    first = (m_d, jnp.sum(p_d, axis=-1, keepdims=True),
             jnp.dot(p_d.astype(BF16), values(c_diag), preferred_element_type=F32))

    def chunk(c, carry):
        m_i, l_i, acc = carry
        s = scores(c)
        m_new = jnp.maximum(m_i, jnp.max(s, axis=-1, keepdims=True))
        alpha = jnp.exp(m_i - m_new)
        p = jnp.exp(s - m_new)
        l_new = alpha * l_i + jnp.sum(p, axis=-1, keepdims=True)
        return m_new, l_new, alpha * acc + jnp.dot(p.astype(BF16), values(c), preferred_element_type=F32)

    _, l_f, acc_f = lax.fori_loop(0, c_diag, chunk, first)
    o_sel = acc_f / l_f

    nwk = WINDOW + tq
    kstart = pl.multiple_of(jnp.maximum(qi * tq - WINDOW, 0), tq)
    kpos = kstart + lax.broadcasted_iota(jnp.int32, (1, nwk), 1)
    band = jnp.where((kpos <= pos) & (pos - kpos < WINDOW), 0.0, NEG)
    w0 = jnp.maximum(qi - WINDOW // tq, 0)
    kwt = jnp.concatenate([wkt[w0 + m] for m in range(nwk // tq)], axis=1)
    s_w = jnp.dot(qs, kwt, preferred_element_type=F32) + jnp.concatenate([band] * g, axis=0)
    e_w = jnp.exp(s_w - jnp.max(s_w, axis=-1, keepdims=True))
    o_win = (jnp.dot(e_w.astype(BF16), wv_ref[pl.ds(kstart, nwk), :].astype(BF16), preferred_element_type=F32)
             / jnp.sum(e_w, axis=-1, keepdims=True))

    gate = 1.0 / (1.0 + jnp.exp(-gl_ref[...]))
    for i in range(g):
        rows = slice(i * tq, (i + 1) * tq)
        o_ref[:, i * HEAD_DIM:(i + 1) * HEAD_DIM] = (gate[:, 3 * i:3 * i + 1] * o_cmp[rows]
                                                     + gate[:, 3 * i + 1:3 * i + 2] * o_sel[rows]
                                                     + gate[:, 3 * i + 2:3 * i + 3] * o_win[rows])


def _nsa_prompt(main, kc, vc, batch, seq):
    nq = seq // NSA_TQ
    gw = NSA_GROUP * HEAD_DIM
    qcol, glcol = _split_col0(4) // gw, GL_COL0 // LANES
    seq_spec = lambda split: pl.BlockSpec((seq, HEAD_DIM),
                                          lambda b, k, i: (b, _split_col0(split) // HEAD_DIM + k))
    cmp_spec = pl.BlockSpec((1, seq // CMP_BLOCK, HEAD_DIM), lambda b, k, i: (b, 0, k))
    return pl.pallas_call(
        _nsa_prompt_kernel,
        out_shape=jax.ShapeDtypeStruct((batch * seq, NSA_W), F32),
        grid=(batch, NSA_KV_HEADS, nq),
        in_specs=[pl.BlockSpec((NSA_TQ, gw), lambda b, k, i: (b * nq + i, qcol + k)), cmp_spec, cmp_spec,
                  seq_spec(7), seq_spec(8), seq_spec(9), seq_spec(10),
                  pl.BlockSpec((NSA_TQ, LANES), lambda b, k, i: (b * nq + i, glcol + k))],
        out_specs=pl.BlockSpec((NSA_TQ, gw), lambda b, k, i: (b * nq + i, k)),
        scratch_shapes=[pltpu.VMEM((seq // NSA_KEY_CHUNK, HEAD_DIM + seq // CMP_BLOCK, NSA_KEY_CHUNK), BF16),
                        pltpu.VMEM((seq // NSA_TQ, HEAD_DIM, NSA_TQ), BF16)],
        compiler_params=pltpu.CompilerParams(dimension_semantics=("arbitrary",) * 3,
                                             vmem_limit_bytes=VMEM_LIMIT),
        name="nsa_prompt",
    )(main, kc, vc, main, main, main, main, main)


WO_TM = 512


def _layer_norm_rows(y, g, b):
    mu = jnp.mean(y, axis=-1, keepdims=True)
    var = jnp.mean(jnp.square(y - mu), axis=-1, keepdims=True)
    return (y - mu) * lax.rsqrt(var + LN_EPS) * g + b


EXPERT_LANE0 = N_GROUPS
R_EID, R_RANK, R_GATE = 0, EXPERT_TOPK, 2 * EXPERT_TOPK


def _route(h, wr_ref, br_ref, carry):
    tm = h.shape[0]
    logit = jnp.dot(h.astype(BF16), wr_ref[...], preferred_element_type=F32) + br_ref[...]
    lane = lax.broadcasted_iota(jnp.int32, (tm, LANES), 1)
    lanef = lane.astype(F32)
    first_lane = lambda hit: jnp.min(jnp.where(hit, lanef, float(LANES)), axis=1, keepdims=True)
    is_g = lane < N_GROUPS
    gl = jnp.where(is_g, logit, LOWEST)
    gmx = jnp.max(gl, axis=1, keepdims=True)
    grp = first_lane(gl == gmx)
    p_grp = 1.0 / jnp.sum(jnp.where(is_g, jnp.exp(gl - gmx), 0.0), axis=1, keepdims=True)
    lane_grp = jnp.right_shift(lane - EXPERT_LANE0, int(math.log2(EXPERTS_PER_GROUP)))
    in_grp = lane_grp.astype(F32) == grp
    el = jnp.where(in_grp, logit, LOWEST)
    ee = jnp.where(in_grp, jnp.exp(el - jnp.max(el, axis=1, keepdims=True)), 0.0)
    pe = jnp.where(in_grp, ee / jnp.sum(ee, axis=1, keepdims=True), -1.0)
    p1 = jnp.max(pe, axis=1, keepdims=True)
    l1 = first_lane(pe == p1)
    pe2 = jnp.where(lanef == l1, -1.0, pe)
    p2 = jnp.max(pe2, axis=1, keepdims=True)
    l2 = first_lane(pe2 == p2)
    den = p1 + p2
    o1, o2 = lanef == l1, lanef == l2
    onehot = jnp.where(o1 | o2, 1.0, 0.0)
    r = lax.broadcasted_iota(jnp.int32, (tm, tm), 0)
    c = lax.broadcasted_iota(jnp.int32, (tm, tm), 1)
    earlier = jnp.where(c < r, 1.0, 0.0).astype(BF16)
    prefix = jnp.dot(earlier, onehot.astype(BF16), preferred_element_type=F32) + carry[0:1, :]
    rank1 = jnp.sum(jnp.where(o1, prefix, 0.0), axis=1, keepdims=True)
    rank2 = jnp.sum(jnp.where(o2, prefix, 0.0), axis=1, keepdims=True)
    carry[0:1, :] = carry[0:1, :] + jnp.sum(onehot, axis=0, keepdims=True)
    fields = [l1 - EXPERT_LANE0, l2 - EXPERT_LANE0, rank1, rank2, p_grp * p1 / den, p_grp * p2 / den]
    rec = jnp.zeros((tm, LANES), F32)
    for j, f in enumerate(fields):
        rec = jnp.where(lane == j, f, rec)
    return rec


def _pack_bf16_pairs(x):
    w = x.shape[1] // 2
    return pltpu.pack_elementwise([x[:, :w], x[:, w:]], packed_dtype=jnp.bfloat16)


def _rows_to_tiles(ref, index, p):
    r = p.shape[0]
    for s in range(SUBLANES):
        ref[index + (pl.ds(s, r, stride=SUBLANES), slice(None))] = p[:, s * LANES:(s + 1) * LANES]


def _tiles_to_rows(ref, index, r):
    return jnp.concatenate([ref[index + (pl.ds(s, r, stride=SUBLANES), slice(None))] for s in range(SUBLANES)],
                           axis=1)


def _unpack_bf16_pairs(p):
    halves = [pltpu.unpack_elementwise(p, index=i, packed_dtype=jnp.bfloat16, unpacked_dtype=F32) for i in (0, 1)]
    return jnp.concatenate(halves, axis=1)


def _wo_ln_route_kernel(fr_ref, fn_ref, x_ref, w_ref, g_ref, b_ref, wr_ref, br_ref, base_ref,
                        h_ref, hp_ref, route_ref, cnt_ref, carry):
    i = pl.program_id(0)

    @pl.when(i == 0)
    def _():
        carry[...] = base_ref[...]

    y = (DEEPNORM_ALPHA * x_ref[...]
         + jnp.dot(fr_ref[...].astype(BF16), w_ref[0:RET_W, :], preferred_element_type=F32)
         + jnp.dot(fn_ref[...].astype(BF16), w_ref[RET_W:MIX_W, :], preferred_element_type=F32))
    h = _layer_norm_rows(y, g_ref[...], b_ref[...])
    h_ref[...] = h
    _rows_to_tiles(hp_ref, (), _pack_bf16_pairs(h))
    route_ref[...] = _route(h, wr_ref, br_ref, carry)

    @pl.when(i == pl.num_programs(0) - 1)
    def _():
        cnt_ref[...] = carry[...]


def _wo_ln_route(f_ret, f_nsa, x, w_o_bf16, ln_g, ln_b, w_route, b_route, base_counts):
    t, d = x.shape
    tm = min(WO_TM, t)
    assert t % tm == 0
    row = lambda w: pl.BlockSpec((tm, w), lambda i: (i, 0))
    full = lambda a: pl.BlockSpec(a.shape, lambda i: (0,) * a.ndim)
    lg, lb = ln_g.reshape(1, d), ln_b.reshape(1, d)
    return pl.pallas_call(
        _wo_ln_route_kernel,
        out_shape=[jax.ShapeDtypeStruct((t, d), F32), jax.ShapeDtypeStruct((t * SUBLANES, LANES), jnp.uint32),
                   jax.ShapeDtypeStruct((t, LANES), F32), jax.ShapeDtypeStruct((SUBLANES, LANES), F32)],
        grid=(t // tm,),
        in_specs=[row(RET_W), row(NSA_W), row(d), full(w_o_bf16), full(lg), full(lb),
                  full(w_route), full(b_route), full(base_counts)],
        out_specs=[row(d), pl.BlockSpec((tm * SUBLANES, LANES), lambda i: (i, 0)), row(LANES),
                   pl.BlockSpec((SUBLANES, LANES), lambda i: (0, 0))],
        scratch_shapes=[pltpu.VMEM((SUBLANES, LANES), F32)],
        compiler_params=pltpu.CompilerParams(dimension_semantics=("arbitrary",), vmem_limit_bytes=VMEM_LIMIT),
        name="wo_ln1_route",
    )(f_ret, f_nsa, x, w_o_bf16, lg, lb, w_route, b_route, base_counts)


MOE_BM = 256
MOE_SUB = 64
TABLE_UNROLL = 8
META_N_USED, META_FILL_TRIPS, META_PAD_LO, META_PAD_HI = 0, 1, 2, 2 + N_EXPERTS


def _expert_kernel(blk_e_ref, meta_ref, slot_ref, h_ref, wg_ref, wu_ref, wd_ref, y_ref,
                   src_tok, dst_row, xbuf, obuf, gsem, ssem, *, plane):
    del blk_e_ref
    n_sub, sub = xbuf.shape[1], xbuf.shape[2] // SUBLANES
    bm = n_sub * sub
    i = pl.program_id(0)
    n_used = meta_ref[META_N_USED]
    n_asg = slot_ref.shape[0]
    dump0 = EXPERT_TOPK * plane
    assert bm & (bm - 1) == 0

    tile = lambda row: pl.ds(pl.multiple_of(row * SUBLANES, SUBLANES), SUBLANES)

    def gather(blk, buf_slot):
        def sub_block(j, carry):
            base = blk * bm + j * sub
            for u in range(sub):
                pltpu.make_async_copy(h_ref.at[tile(src_tok[base + u]), :],
                                      xbuf.at[buf_slot, j, tile(u), :], gsem.at[buf_slot]).start(priority=u % 2)
            return carry
        lax.fori_loop(0, n_sub, sub_block, 0)

    def scatter(blk, buf_slot):
        def sub_block(j, carry):
            base = blk * bm + j * sub
            for u in range(sub):
                pltpu.make_async_copy(obuf.at[buf_slot, j, tile(u), :],
                                      y_ref.at[tile(dst_row[base + u]), :], ssem.at[buf_slot]).start(priority=u % 2)
            return carry
        lax.fori_loop(0, n_sub, sub_block, 0)

    def wait_block(buf, sem, buf_slot):
        pltpu.make_async_copy(buf.at[buf_slot], buf.at[buf_slot], sem.at[buf_slot]).wait()

    @pl.when(i == 0)
    def _():
        def clear_expert(e, carry):
            def clear(r, c):
                src_tok[r] = 0
                dst_row[r] = dump0 + jnp.bitwise_and(r, 2 * bm - 1)
                return c
            lax.fori_loop(meta_ref[META_PAD_LO + e], meta_ref[META_PAD_HI + e], clear, 0)
            return carry
        lax.fori_loop(0, N_EXPERTS, clear_expert, 0)

        def fill(t, carry):
            for u in range(TABLE_UNROLL):
                a = t * TABLE_UNROLL + u
                tok = jnp.right_shift(a, 1)
                src_tok[slot_ref[a]] = tok
                dst_row[slot_ref[a]] = jnp.bitwise_and(a, 1) * plane + tok
            return carry
        lax.fori_loop(0, meta_ref[META_FILL_TRIPS], fill, 0)
        gather(0, 0)
        n_tok = h_ref.shape[0] // SUBLANES
        tail = plane - n_tok
        assert tail >= 0
        obuf[1] = jnp.zeros(obuf.shape[1:], obuf.dtype)
        spans = [(dump0, 2 * bm)] + ([(k * plane + n_tok, tail) for k in range(EXPERT_TOPK)] if tail else [])
        copies = []
        for first, count in spans:
            for j in range(-(-count // sub)):
                rows = min(sub, count - j * sub)
                copies.append(pltpu.make_async_copy(
                    obuf.at[1, j % n_sub, pl.ds(0, rows * SUBLANES), :],
                    y_ref.at[pl.ds((first + j * sub) * SUBLANES, rows * SUBLANES), :], ssem.at[1]))
        for cp in copies:
            cp.start()
        for cp in copies:
            cp.wait()

    slot = jnp.bitwise_and(i, 1)

    @pl.when(i + 1 < n_used)
    def _():
        gather(i + 1, 1 - slot)

    @pl.when(i < n_used)
    def _():
        wait_block(xbuf, gsem, slot)

        @pl.when(i >= 2)
        def _():
            wait_block(obuf, ssem, slot)

        packed = jnp.concatenate([_tiles_to_rows(xbuf, (slot, j), sub) for j in range(n_sub)], axis=0)
        xb = _unpack_bf16_pairs(packed).astype(BF16)
        hg = jnp.dot(xb, wg_ref[0].astype(BF16), preferred_element_type=F32)
        hu = jnp.dot(xb, wu_ref[0].astype(BF16), preferred_element_type=F32)
        hb = hg * (1.0 / (1.0 + jnp.exp(-hg))) * hu
        yb = jnp.dot(hb.astype(BF16), wd_ref[0].astype(BF16), preferred_element_type=F32)
        yp = _pack_bf16_pairs(yb)
        for j in range(n_sub):
            _rows_to_tiles(obuf, (slot, j), yp[j * sub:(j + 1) * sub])
        scatter(i, slot)

    @pl.when(i == pl.num_programs(0) - 1)
    def _():
        @pl.when(n_used >= 2)
        def _():
            wait_block(obuf, ssem, jnp.bitwise_and(n_used, 1))
        wait_block(obuf, ssem, jnp.bitwise_and(n_used - 1, 1))


def _expert_ffn(h, slot, blk_e, meta, w_gate, w_up, w_down, plane):
    t = h.shape[0] // SUBLANES
    dp = SUBLANES * LANES
    d = 2 * dp
    assert w_gate.shape[1] == d
    n_asg = slot.shape[0]
    assert n_asg == t * EXPERT_TOPK and EXPERT_TOPK == 2
    n_blk = -(-(n_asg + N_EXPERTS * (MOE_BM - 1)) // MOE_BM)
    de = w_gate.shape[2]
    assert n_asg % TABLE_UNROLL == 0 and meta.shape == (META_PAD_HI + N_EXPERTS,)
    wspec = lambda shape: pl.BlockSpec((1,) + shape, lambda i, be, nu, sl: (be[i], 0, 0))
    return pl.pallas_call(
        functools.partial(_expert_kernel, plane=plane),
        out_shape=jax.ShapeDtypeStruct(((EXPERT_TOPK * plane + 2 * MOE_BM) * SUBLANES, LANES), jnp.uint32),
        grid_spec=pltpu.PrefetchScalarGridSpec(
            num_scalar_prefetch=3,
            grid=(n_blk,),
            in_specs=[pl.BlockSpec(memory_space=pl.ANY), wspec((d, de)), wspec((d, de)), wspec((de, d))],
            out_specs=pl.BlockSpec(memory_space=pl.ANY),
            scratch_shapes=[pltpu.SMEM((n_blk * MOE_BM,), jnp.int32), pltpu.SMEM((n_blk * MOE_BM,), jnp.int32),
                            pltpu.VMEM((2, MOE_BM // MOE_SUB, MOE_SUB * SUBLANES, LANES), jnp.uint32),
                            pltpu.VMEM((2, MOE_BM // MOE_SUB, MOE_SUB * SUBLANES, LANES), jnp.uint32),
                            pltpu.SemaphoreType.DMA((2,)), pltpu.SemaphoreType.DMA((2,))]),
        compiler_params=pltpu.CompilerParams(dimension_semantics=("arbitrary",), vmem_limit_bytes=VMEM_LIMIT),
        name="expert_ffn",
    )(blk_e, meta, slot, h, w_gate, w_up, w_down)


def _moe_ln_kernel(h_ref, y0_ref, y1_ref, route_ref, g_ref, b_ref, o_ref):
    rec = route_ref[...]
    tm = h_ref.shape[0]
    y = (DEEPNORM_ALPHA * h_ref[...]
         + rec[:, R_GATE:R_GATE + 1] * _unpack_bf16_pairs(_tiles_to_rows(y0_ref, (), tm))
         + rec[:, R_GATE + 1:R_GATE + 2] * _unpack_bf16_pairs(_tiles_to_rows(y1_ref, (), tm)))
    o_ref[...] = _layer_norm_rows(y, g_ref[...], b_ref[...])


def _moe_ln(h, y, route, ln_g, ln_b, row0, plane):
    n_rows, d = h.shape
    tm = min(WO_TM, n_rows)
    assert n_rows % tm == 0 and row0 % tm == 0 and plane % tm == 0 and EXPERT_TOPK == 2
    off = row0 // tm
    row = lambda w, o: pl.BlockSpec((tm, w), lambda i: (i + o, 0))
    tiles = lambda o: pl.BlockSpec((tm * SUBLANES, LANES), lambda i: (i + o, 0))
    vec = pl.BlockSpec((1, d), lambda i: (0, 0))
    return pl.pallas_call(
        _moe_ln_kernel,
        out_shape=jax.ShapeDtypeStruct((n_rows, d), F32),
        grid=(n_rows // tm,),
        in_specs=[row(d, 0), tiles(off), tiles(off + plane // tm), row(LANES, off), vec, vec],
        out_specs=row(d, 0),
        compiler_params=pltpu.CompilerParams(dimension_semantics=("arbitrary",), vmem_limit_bytes=VMEM_LIMIT),
        name="moe_ln2",
    )(h, y, y, route, ln_g.reshape(1, d), ln_b.reshape(1, d))


def _layer_norm(x, g, b):
    xf = x.astype(F32)
    mu = xf.mean(-1, keepdims=True)
    var = jnp.square(xf - mu).mean(-1, keepdims=True)
    return ((xf - mu) * lax.rsqrt(var + LN_EPS) * g + b).astype(x.dtype)


def _rope(x, pos, rot_dim, theta):
    half = rot_dim // 2
    inv = theta ** (-jnp.arange(0, rot_dim, 2, dtype=F32) / rot_dim)
    ang = pos[..., None].astype(F32) * inv
    cos = jnp.cos(ang)[:, :, None, :]
    sin = jnp.sin(ang)[:, :, None, :]
    xr = x[..., :rot_dim].astype(F32)
    x1, x2 = xr[..., :half], xr[..., half:]
    rot = jnp.concatenate([x1 * cos - x2 * sin, x2 * cos + x1 * sin], -1).astype(x.dtype)
    return jnp.concatenate([rot, x[..., rot_dim:]], -1)


def _heads(t, n):
    return t.reshape(t.shape[0], t.shape[1], n, HEAD_DIM)


def _chunk_retention(q, k, v, s0):
    n, l, h, d = q.shape
    c = RET_CHUNK if l % RET_CHUNK == 0 else l
    nc = l // c
    log_g = jnp.log1p(-jnp.exp2(-5.0 - jnp.arange(h, dtype=F32)))
    i = jnp.arange(c, dtype=F32)
    rel = i[:, None] - i[None, :]
    dmask = jnp.where(rel[None] >= 0, jnp.exp(jnp.maximum(rel[None], 0.0) * log_g[:, None, None]), 0.0)
    q_dec = jnp.exp((i + 1.0)[None] * log_g[:, None])[..., None]
    k_dec = jnp.exp((c - 1.0 - i)[None] * log_g[:, None])[..., None]
    c_dec = jnp.exp(c * log_g)[:, None, None]

    def to_chunks(t):
        return t.astype(F32).reshape(n, nc, c, h, d).transpose(1, 0, 3, 2, 4)

    def step(s, qkv):
        qc, kc, vc = qkv
        att = jnp.einsum('bhid,bhjd->bhij', qc, kc) * dmask
        o = jnp.einsum('bhij,bhjd->bhid', att, vc) + jnp.einsum('bhid,bhde->bhie', qc * q_dec, s)
        s = c_dec * s + jnp.einsum('bhjd,bhje->bhde', kc * k_dec, vc)
        return s, o

    s, o = lax.scan(step, s0.astype(F32), (to_chunks(q), to_chunks(k), to_chunks(v)))
    return o.transpose(1, 0, 3, 2, 4).reshape(n, l, h, d), s


def _retention_group(rq, rk, rv, rg, pos, s0, gn_g):
    q = _rope(_heads(rq, RET_HEADS), pos, HEAD_DIM, RET_ROPE_THETA)
    k = _rope(_heads(rk, RET_HEADS), pos, HEAD_DIM, RET_ROPE_THETA) * (HEAD_DIM ** -0.5)
    v = _heads(rv, RET_HEADS)
    o, s = _chunk_retention(q, k, v, s0)
    mu = o.mean(-1, keepdims=True)
    var = jnp.square(o - mu).mean(-1, keepdims=True)
    on = (o - mu) * lax.rsqrt(var + GN_EPS) * gn_g.reshape(RET_HEADS, HEAD_DIM).astype(F32)
    out = jax.nn.silu(rg.astype(F32)) * on.reshape(rg.shape)
    return out.astype(rq.dtype), s


def _gqa_attend(q, k, v, mask):
    n, lq, h, d = q.shape
    kv = k.shape[2]
    qg = q.reshape(n, lq, kv, h // kv, d)
    s = jnp.einsum('nqkgd,nskd->nkgqs', qg, k).astype(F32) * (d ** -0.5)
    m = mask[:, None, None]
    p = jax.nn.softmax(jnp.where(m, s, NEG), axis=-1) * m
    o = jnp.einsum('nkgqs,nskd->nqkgd', p.astype(v.dtype), v)
    return o.reshape(n, lq, h, d), p


def _nsa_heads(nq, ck, sk, wk, cv, sv, wv, pos):
    rp = lambda t, nh: _rope(_heads(t, nh), pos, ROT_DIM, ROPE_THETA)
    return (rp(nq, NSA_HEADS), rp(ck, NSA_KV_HEADS), rp(sk, NSA_KV_HEADS), rp(wk, NSA_KV_HEADS),
            _heads(cv, NSA_KV_HEADS), _heads(sv, NSA_KV_HEADS), _heads(wv, NSA_KV_HEADS))


def _compress(rows, w):
    n, t, kv, d = rows.shape
    return jnp.einsum('nbjkd,jd->nbkd', rows.reshape(n, t // CMP_BLOCK, CMP_BLOCK, kv, d), w)


def _cmp_branch(q, pos, kc, vc):
    nb = kc.shape[1]
    blk_end = (jnp.arange(nb) + 1) * CMP_BLOCK - 1
    mask = blk_end[None, None, :] <= pos[:, :, None]
    o, p = _gqa_attend(q, kc, vc, mask)
    imp = p.sum(axis=2).transpose(0, 2, 1, 3)
    return o, imp


def _select_blocks(imp, pos, n_sel):
    nb = imp.shape[-1]
    imp = jnp.pad(imp, ((0, 0), (0, 0), (0, 0), (0, n_sel - nb)))
    blk = jnp.arange(n_sel)
    cur = (pos // CMP_BLOCK)[:, :, None, None]
    forced = (blk == 0) | (blk == cur) | (blk == cur - 1)
    score = jnp.where(blk > cur, NEG, jnp.where(forced, -NEG, imp))
    top, idx = lax.top_k(score, min(SEL_TOPK, n_sel))
    return idx, top > 0.5 * NEG


def _sel_attend(q, pos, ks, vs, idx, valid):
    n, lq, kv, kk, cb, d = ks.shape
    h = q.shape[2]
    kpos = idx[..., None] * CMP_BLOCK + jnp.arange(CMP_BLOCK)
    m = ((kpos <= pos[:, :, None, None, None]) & valid[..., None]).reshape(n, lq, kv, 1, kk * cb)
    qg = q.reshape(n, lq, kv, h // kv, d)
    kf = ks.reshape(n, lq, kv, kk * cb, d)
    vf = vs.reshape(n, lq, kv, kk * cb, d)
    s = jnp.einsum('nqkgd,nqkjd->nqkgj', qg, kf).astype(F32) * (d ** -0.5)
    p = jax.nn.softmax(jnp.where(m, s, NEG), axis=-1) * m
    o = jnp.einsum('nqkgj,nqkjd->nqkgd', p.astype(vf.dtype), vf)
    return o.reshape(n, lq, h, d)


def _sel_prompt(q, pos, k, v, idx, valid):
    b, s, h, d = q.shape
    kv = k.shape[2]
    nb = s // CMP_BLOCK
    nq = s // SEL_Q_BLOCK
    kb = k.reshape(b, nb, CMP_BLOCK, kv, d).transpose(0, 3, 1, 2, 4)
    vb = v.reshape(b, nb, CMP_BLOCK, kv, d).transpose(0, 3, 1, 2, 4)
    bi = jnp.arange(b)[:, None, None, None]
    hi = jnp.arange(kv)[None, None, :, None]

    def blockwise(t):
        return t.reshape(t.shape[0], nq, SEL_Q_BLOCK, *t.shape[2:]).swapaxes(0, 1)

    def one(args):
        qc, pc, ic, vc = args
        return _sel_attend(qc, pc, kb[bi, hi, ic], vb[bi, hi, ic], ic, vc)

    o = lax.map(one, (blockwise(q), blockwise(pos), blockwise(idx), blockwise(valid)))
    return o.swapaxes(0, 1).reshape(b, s, h, d)


def _win_prompt(q, k, v):
    b, s, h, d = q.shape
    kv = k.shape[2]
    nb = s // WIN_Q_BLOCK
    nprev = WINDOW // WIN_Q_BLOCK
    nw = nprev + 1
    padw = ((0, 0), (WINDOW, 0), (0, 0), (0, 0))
    kp = jnp.pad(k, padw).reshape(b, nb + nprev, WIN_Q_BLOCK, kv, d)
    vp = jnp.pad(v, padw).reshape(b, nb + nprev, WIN_Q_BLOCK, kv, d)
    kw = jnp.concatenate([kp[:, i:i + nb] for i in range(nw)], axis=2)
    vw = jnp.concatenate([vp[:, i:i + nb] for i in range(nw)], axis=2)
    qpos = jnp.arange(s).reshape(nb, WIN_Q_BLOCK)
    kpos = (jnp.arange(nb)[:, None] - nprev) * WIN_Q_BLOCK + jnp.arange(nw * WIN_Q_BLOCK)[None]
    qq, kk = qpos[:, :, None], kpos[:, None, :]
    mask = (kk <= qq) & (qq - kk < WINDOW) & (kk >= 0)
    mask = jnp.broadcast_to(mask[None], (b,) + mask.shape).reshape(b * nb, WIN_Q_BLOCK, nw * WIN_Q_BLOCK)
    o, _ = _gqa_attend(q.reshape(b * nb, WIN_Q_BLOCK, h, d), kw.reshape(b * nb, nw * WIN_Q_BLOCK, kv, d),
                       vw.reshape(b * nb, nw * WIN_Q_BLOCK, kv, d), mask)
    return o.reshape(b, s, h, d)


def _gather_selected(pool, new_rows, page_table, idx):
    n, l, kv, d = new_rows.shape
    n_pages = page_table.shape[1]
    past_blocks = n_pages * PAGE_SIZE // CMP_BLOCK
    nbn = -(-l // CMP_BLOCK)
    newb = jnp.pad(new_rows, ((0, 0), (0, nbn * CMP_BLOCK - l), (0, 0), (0, 0)))
    newb = newb.reshape(n, nbn, CMP_BLOCK, kv, d).transpose(0, 3, 1, 2, 4)
    bi = jnp.arange(n)[:, None, None, None]
    hi = jnp.arange(kv)[None, None, :, None]
    start = idx * CMP_BLOCK
    phys = page_table[bi, jnp.minimum(start // PAGE_SIZE, n_pages - 1)]
    off = (start % PAGE_SIZE)[..., None] + jnp.arange(CMP_BLOCK)
    past = pool[phys[..., None], off, hi[..., None]]
    new = newb[bi, hi, jnp.clip(idx - past_blocks, 0, nbn - 1)]
    return jnp.where((idx < past_blocks)[..., None, None], past, new)


def _nsa_combine(gl, o_cmp, o_sel, o_win):
    n, l = gl.shape[0], gl.shape[1]
    g = jax.nn.sigmoid(gl.astype(F32)).reshape(n, l, NSA_HEADS, 3, 1)
    o = g[..., 0, :] * o_cmp + g[..., 1, :] * o_sel + g[..., 2, :] * o_win
    return o.reshape(n, l, NSA_W).astype(o_cmp.dtype)


def _prompt_mixer(x, win_buf, w_in, w_cmp_k, w_cmp_v, gn_g):
    n, s, _ = x.shape
    main, (ck, cv, sk, sv, wk, wv) = _project(x, w_in, jnp.arange(s))
    ret_out, s_fin = _retention_prompt(main, gn_g, n, s)
    kc, vc = _compress_prompt(main, w_cmp_k, w_cmp_v)
    nsa = _nsa_prompt(main, kc.reshape(n, s // CMP_BLOCK, KV_W), vc.reshape(n, s // CMP_BLOCK, KV_W), n, s)
    feats = (ret_out, nsa)
    if s >= win_buf:
        bk, bv = wk[:, s - win_buf:], wv[:, s - win_buf:]
    else:
        padb = ((0, 0), (win_buf - s, 0), (0, 0), (0, 0))
        bk, bv = jnp.pad(wk, padb), jnp.pad(wv, padb)
    return feats, (ck, cv, sk, sv, bk, bv, s_fin)


def _sample_mixer(x, c_cmp_k, c_cmp_v, c_sel_k, c_sel_v, c_win_k, c_win_v, s_ret, page_table,
                  w_in, w_cmp_k, w_cmp_v, gn_g):
    n, l, _ = x.shape
    past = page_table.shape[1] * PAGE_SIZE
    pos = past + jnp.arange(l)[None]
    assert l == 1
    main, (ck, cv, sk, sv, wk, wv) = _project(x, w_in, jnp.full((n,), past, jnp.int32))
    rq, rk, rv, rg, nq = _split_main(main, n, l)[:5]
    gl = _gate_logits(main, n, l)
    ret_out, s_new = _retention_group(rq, rk, rv, rg, pos, s_ret, gn_g)
    q = _heads(nq, NSA_HEADS)
    assert l == 1 and past % CMP_BLOCK == 0 and c_win_k.shape[1] <= WINDOW
    kc, vc = _compress_paged(c_cmp_k, c_cmp_v, page_table, w_cmp_k, w_cmp_v)
    o_cmp, sel = _sample_select(q[:, 0], kc, vc, past)
    sel = sel[:, :SEL_TOPK].reshape(-1)
    gl_pad = jnp.pad(gl.reshape(n, NSA_HEADS, 3), ((0, 0), (0, 0), (0, LANES - 3)))
    nsa = _sample_attend(sel, page_table, q[:, 0], sk[:, 0], sv[:, 0], wk[:, 0], wv[:, 0],
                         _interleaved(c_win_k), _interleaved(c_win_v), _interleaved(c_sel_k), _interleaved(c_sel_v),
                         o_cmp, gl_pad, past)
    feats = (ret_out.reshape(n * l, RET_W), nsa.reshape(n * l, NSA_W))
    kw = jnp.concatenate([c_win_k, wk], 1)
    vw = jnp.concatenate([c_win_v, wv], 1)
    return feats, (ck, cv, sk, sv, kw[:, l:], vw[:, l:], s_new)


def _route_params(w_group, b_group, w_expert, b_expert):
    w = jnp.concatenate([w_group, w_expert], axis=1)
    b = jnp.concatenate([b_group, b_expert], axis=0)
    pad = LANES - w.shape[1]
    return jnp.pad(w, ((0, 0), (0, pad))).astype(BF16), jnp.pad(b, (0, pad)).reshape(1, LANES)


def _dispatch_plan(route, counts_tile):
    counts = counts_tile[0, EXPERT_LANE0:EXPERT_LANE0 + N_EXPERTS].astype(jnp.int32)
    padded = (counts + MOE_BM - 1) // MOE_BM * MOE_BM
    pad_end = jnp.cumsum(padded)
    pad_start = pad_end - padded
    n_asg = route.shape[0] * EXPERT_TOPK
    n_blk = -(-(n_asg + N_EXPERTS * (MOE_BM - 1)) // MOE_BM)
    blk_first = jnp.arange(n_blk, dtype=jnp.int32) * MOE_BM
    blk_e = jnp.minimum(jnp.sum(pad_end[None, :] <= blk_first[:, None], axis=1), N_EXPERTS - 1).astype(jnp.int32)
    meta = jnp.concatenate([pad_end[-1:] // MOE_BM, jnp.array([n_asg // TABLE_UNROLL]), pad_start + counts,
                            pad_end]).astype(jnp.int32)
    eid = route[:, R_EID:R_EID + EXPERT_TOPK].astype(jnp.int32)
    rank = route[:, R_RANK:R_RANK + EXPERT_TOPK].astype(jnp.int32)
    start = jnp.sum(jnp.where(eid[..., None] == jnp.arange(N_EXPERTS), pad_start, 0), axis=-1)
    return (start + rank).reshape(-1).astype(jnp.int32), blk_e, meta


def kernel(x_prompt, x_sample, cache_cmp_k, cache_cmp_v, cache_sel_k, cache_sel_v, cache_win_k, cache_win_v,
           state_ret, page_table, w_in, w_cmp_k, w_cmp_v, ret_gn_g, w_o, ln1_g, ln1_b, w_group, b_group,
           w_expert, b_expert, w_gate, w_up, w_down, ln2_g, ln2_b):
    win_buf = cache_win_k.shape[2]
    hp, hs = x_prompt, x_sample
    acc_p = [[] for _ in range(7)]
    acc_s = [[] for _ in range(7)]
    for l in range(DEPTH):
        fp, st_p = _prompt_mixer(hp, win_buf, w_in[l], w_cmp_k[l], w_cmp_v[l], ret_gn_g[l])
        fs, st_s = _sample_mixer(hs, cache_cmp_k[l], cache_cmp_v[l], cache_sel_k[l], cache_sel_v[l],
                                 cache_win_k[l], cache_win_v[l], state_ret[l], page_table,
                                 w_in[l], w_cmp_k[l], w_cmp_v[l], ret_gn_g[l])
        w_o_bf16 = w_o[l].astype(BF16)
        w_route, b_route = _route_params(w_group[l], b_group[l], w_expert[l], b_expert[l])
        tp = hp.shape[0] * hp.shape[1]
        ts = hs.shape[0] * hs.shape[1]
        no_counts = jnp.zeros((SUBLANES, LANES), F32)
        h1p, packed_p, route_p, counts_p = _wo_ln_route(fp[0], fp[1], hp.reshape(tp, D_MODEL), w_o_bf16, ln1_g[l],
                                                        ln1_b[l], w_route, b_route, no_counts)
        h1s, packed_s, route_s, counts = _wo_ln_route(fs[0], fs[1], hs.reshape(ts, D_MODEL), w_o_bf16, ln1_g[l],
                                                      ln1_b[l], w_route, b_route, counts_p)
        packed = jnp.concatenate([packed_p, packed_s], axis=0)
        route = jnp.concatenate([route_p, route_s], axis=0)
        slot, blk_e, meta = _dispatch_plan(route, counts)
        plane = -(-(tp + ts) // WO_TM) * WO_TM
        y = _expert_ffn(packed, slot, blk_e, meta, w_gate[l], w_up[l], w_down[l], plane)
        hp = _moe_ln(h1p, y, route, ln2_g[l], ln2_b[l], 0, plane).reshape(hp.shape)
        hs = _moe_ln(h1s, y, route, ln2_g[l], ln2_b[l], tp, plane).reshape(hs.shape)
        for acc, t in zip(acc_p, st_p):
            acc.append(t)
        for acc, t in zip(acc_s, st_s):
            acc.append(t)
    p_cmp_k, p_cmp_v, p_sel_k, p_sel_v, p_win_k, p_win_v, p_ret = [jnp.stack(a) for a in acc_p]
    s_cmp_k, s_cmp_v, s_sel_k, s_sel_v, s_win_k, s_win_v, s_ret = [jnp.stack(a) for a in acc_s]
    return (hp, hs, p_cmp_k, p_cmp_v, p_sel_k, p_sel_v, p_win_k, p_win_v, p_ret.astype(state_ret.dtype),
            s_cmp_k, s_cmp_v, s_sel_k, s_sel_v, s_win_k, s_win_v, s_ret.astype(state_ret.dtype))
```

```python
import functools
import math

import jax
import jax.numpy as jnp
import numpy as np
from jax import lax
from jax.experimental import pallas as pl
from jax.experimental.pallas import tpu as pltpu

D_MODEL = 2048
DEPTH = 1
PAGE_SIZE = 128

F32 = jnp.float32
BF16 = jnp.bfloat16
HEAD_DIM = 128
RET_HEADS = D_MODEL // (2 * HEAD_DIM)
NSA_HEADS = D_MODEL // (2 * HEAD_DIM)
NSA_KV_HEADS = 2
RET_W = RET_HEADS * HEAD_DIM
NSA_W = NSA_HEADS * HEAD_DIM
KV_W = NSA_KV_HEADS * HEAD_DIM
MIX_W = RET_W + NSA_W
RET_CHUNK = 128
RET_ROPE_THETA = 10000.0
ROPE_THETA = 500000.0
ROT_DIM = HEAD_DIM // 4
CMP_BLOCK = 64
SEL_TOPK = 16
WINDOW = 512
WIN_Q_BLOCK = 128
SEL_Q_BLOCK = 64
N_GROUPS = 4
EXPERTS_PER_GROUP = 8
N_EXPERTS = N_GROUPS * EXPERTS_PER_GROUP
EXPERT_TOPK = 2
D_EXPERT = 512
MOE_BLOCK = 128
LN_EPS = 1e-5
GN_EPS = 1e-5
NEG = -1e30
DEEPNORM_ALPHA = (2 * DEPTH) ** 0.25
DEEPNORM_BETA = (8 * DEPTH) ** -0.25
SPLITS = (RET_W, RET_W, RET_W, RET_W, NSA_W, KV_W, KV_W, KV_W, KV_W, KV_W, KV_W, NSA_HEADS * 3)
IN_W = sum(SPLITS)
GATE_W = NSA_HEADS * 3
MAIN_W = IN_W - GATE_W
LANES = 128
VMEM_LIMIT = 48 * 1024 * 1024


PROJ_TN = 512
GL_COL0 = MAIN_W
PROJ_W = -(-(MAIN_W + NSA_KV_HEADS * LANES) // PROJ_TN) * PROJ_TN
ROTATED_SPLITS = (4, 5, 7, 9)


def _rotated_heads():
    cuts = np.cumsum((0,) + SPLITS)
    tiles = []
    for j in range(PROJ_W // PROJ_TN):
        heads = []
        for h in range(PROJ_TN // HEAD_DIM):
            c0 = j * PROJ_TN + h * HEAD_DIM
            split = int(np.searchsorted(cuts, c0, side="right")) - 1
            heads.append(split in ROTATED_SPLITS and c0 < MAIN_W)
        tiles.append(tuple(heads))
    return tiles


def _nsa_rope_tables(pos):
    half = ROT_DIM // 2
    inv = ROPE_THETA ** (-jnp.arange(0, ROT_DIM, 2, dtype=F32) / ROT_DIM)
    ang = pos[:, None].astype(F32) * inv
    cos, sin = jnp.cos(ang), jnp.sin(ang)
    rest = HEAD_DIM - ROT_DIM
    zeros = jnp.zeros((pos.shape[0], half), F32)
    pad = lambda t, fill: jnp.pad(t, ((0, 0), (0, rest)), constant_values=fill)
    return (pad(jnp.concatenate([cos, cos], 1), 1.0), pad(jnp.concatenate([zeros, sin], 1), 0.0),
            pad(jnp.concatenate([-sin, zeros], 1), 0.0))


KV_SPLITS = (5, 6, 7, 8, 9, 10)


def _kv_heads():
    cuts = np.cumsum((0,) + SPLITS)
    tiles = []
    for j in range(MAIN_W // PROJ_TN):
        heads = []
        for h in range(PROJ_TN // HEAD_DIM):
            c0 = j * PROJ_TN + h * HEAD_DIM
            split = int(np.searchsorted(cuts, c0, side="right")) - 1
            heads.append((KV_SPLITS.index(split), (c0 - int(cuts[split])) // HEAD_DIM) if split in KV_SPLITS else None)
        tiles.append(tuple(heads))
    return tiles


def _proj_kernel(x_ref, w_ref, wt_ref, cos_ref, up_ref, dn_ref, o_ref, *kv_refs, patterns, kv_heads):
    j = pl.program_id(1)
    n_main = MAIN_W // PROJ_TN
    xb = x_ref[...].astype(BF16)
    half = ROT_DIM // 2

    def rotated(acc, heads):
        cos, up, dn = cos_ref[...], up_ref[...], dn_ref[...]
        parts = []
        for h, rot in enumerate(heads):
            xh = acc[:, h * HEAD_DIM:(h + 1) * HEAD_DIM]
            if rot:
                xh = xh * cos + pltpu.roll(xh, half, 1) * up + pltpu.roll(xh, HEAD_DIM - half, 1) * dn
            parts.append(xh)
        return jnp.concatenate(parts, axis=1)

    plans = list(zip(patterns[:n_main], kv_heads))
    for plan in sorted(set(plans), key=repr):
        tiles = [t for t, p in enumerate(plans) if p == plan]
        hit = functools.reduce(jnp.logical_or, [j == t for t in tiles])

        @pl.when(hit)
        def _(plan=plan):
            heads, dests = plan
            acc = jnp.dot(xb, w_ref[...], preferred_element_type=F32)
            vals = rotated(acc, heads) if any(heads) else acc
            o_ref[...] = vals
            for h, dest in enumerate(dests):
                if dest is not None:
                    out, k = dest
                    kv_refs[out][pl.ds(k, vals.shape[0], stride=NSA_KV_HEADS), :] = (
                        vals[:, h * HEAD_DIM:(h + 1) * HEAD_DIM])

    @pl.when(j >= n_main)
    def _():
        o_ref[...] = jnp.dot(xb, wt_ref[...], preferred_element_type=F32)


def _project(x, w_in, pos):
    n, l, d = x.shape
    t = n * l
    xt = x.reshape(t, d)
    tm = min(1024, t)
    gpg = GATE_W // NSA_KV_HEADS
    n_main = MAIN_W // PROJ_TN
    assert PROJ_W == MAIN_W + PROJ_TN and not any(any(p) for p in _rotated_heads()[n_main:])
    gate_tiles = [jnp.pad(w_in[:, MAIN_W + k * gpg:MAIN_W + (k + 1) * gpg], ((0, 0), (0, LANES - gpg)))
                  for k in range(NSA_KV_HEADS)]
    fill = jnp.zeros((d, PROJ_TN - NSA_KV_HEADS * LANES), F32)
    w_tail = jnp.concatenate(gate_tiles + [fill], axis=1).astype(BF16)
    tables = _nsa_rope_tables(pos)
    pb = pos.shape[0] // tm
    tab = pl.BlockSpec((tm, HEAD_DIM), lambda i, j: (i % pb, 0))
    kv_rows = pl.BlockSpec((tm * NSA_KV_HEADS, HEAD_DIM), lambda i, j: (i, 0))
    outs = pl.pallas_call(
        functools.partial(_proj_kernel, patterns=_rotated_heads(), kv_heads=_kv_heads()),
        out_shape=[jax.ShapeDtypeStruct((t, PROJ_W), F32)]
        + [jax.ShapeDtypeStruct((t * NSA_KV_HEADS, HEAD_DIM), F32)] * len(KV_SPLITS),
        grid=(t // tm, PROJ_W // PROJ_TN),
        in_specs=[pl.BlockSpec((tm, d), lambda i, j: (i, 0)),
                  pl.BlockSpec((d, PROJ_TN), lambda i, j: (0, jnp.minimum(j, n_main - 1))),
                  pl.BlockSpec((d, PROJ_TN), lambda i, j: (0, 0)), tab, tab, tab],
        out_specs=[pl.BlockSpec((tm, PROJ_TN), lambda i, j: (i, j))] + [kv_rows] * len(KV_SPLITS),
        compiler_params=pltpu.CompilerParams(dimension_semantics=("arbitrary", "arbitrary"),
                                             vmem_limit_bytes=VMEM_LIMIT),
        name="in_proj",
    )(xt, w_in.astype(BF16), w_tail, *tables)
    return outs[0], [o.reshape(n, l, NSA_KV_HEADS, HEAD_DIM) for o in outs[1:]]


def _split_main(main, n, l, first=0):
    cuts = [0] + [int(c) for c in np.cumsum(SPLITS)[:-1]]
    return [main[:, cuts[i]:cuts[i + 1]].reshape(n, l, -1) for i in range(first, len(SPLITS) - 1)]


def _split_col0(i):
    return int(np.cumsum((0,) + SPLITS)[i])


def _gate_logits(main, n, l):
    gpg = GATE_W // NSA_KV_HEADS
    cols = [main[:, GL_COL0 + k * LANES:GL_COL0 + k * LANES + gpg] for k in range(NSA_KV_HEADS)]
    return jnp.concatenate(cols, axis=1).reshape(n, l, GATE_W)


def _compress_kernel(k_ref, v_ref, wk_ref, wv_ref, ko_ref, vo_ref):
    r = ko_ref.shape[0]
    ko_ref[...] = jnp.sum(k_ref[...].reshape(r, CMP_BLOCK, KV_W) * wk_ref[...][None], axis=1)
    vo_ref[...] = jnp.sum(v_ref[...].reshape(r, CMP_BLOCK, KV_W) * wv_ref[...][None], axis=1)


def _compress_prompt(main, w_cmp_k, w_cmp_v):
    t = main.shape[0]
    r = 32
    wk2 = jnp.tile(w_cmp_k, (1, NSA_KV_HEADS))
    wv2 = jnp.tile(w_cmp_v, (1, NSA_KV_HEADS))
    kcol, vcol = _split_col0(5) // KV_W, _split_col0(6) // KV_W
    wsp = pl.BlockSpec((CMP_BLOCK, KV_W), lambda i: (0, 0))
    osp = pl.BlockSpec((r, KV_W), lambda i: (i, 0))
    return pl.pallas_call(
        _compress_kernel,
        out_shape=[jax.ShapeDtypeStruct((t // CMP_BLOCK, KV_W), F32)] * 2,
        grid=(t // (r * CMP_BLOCK),),
        in_specs=[pl.BlockSpec((r * CMP_BLOCK, KV_W), lambda i: (i, kcol)),
                  pl.BlockSpec((r * CMP_BLOCK, KV_W), lambda i: (i, vcol)), wsp, wsp],
        out_specs=[osp, osp],
        compiler_params=pltpu.CompilerParams(dimension_semantics=("arbitrary",)),
        name="compress_prompt",
    )(main, main, wk2, wv2)


PAGES_PER_STEP = 16
BLOCKS_PER_PAGE = PAGE_SIZE // CMP_BLOCK


SUBLANES = 8
ROWS_PER_BLOCK = CMP_BLOCK * NSA_KV_HEADS
ROWS_PER_PAGE = PAGE_SIZE * NSA_KV_HEADS
BLOCKS_PER_TILE = SUBLANES // NSA_KV_HEADS


def _div(x, n):
    assert n & (n - 1) == 0
    return jnp.right_shift(x, n.bit_length() - 1)


def _mod(x, n):
    assert n & (n - 1) == 0
    return jnp.bitwise_and(x, n - 1)


def _interleaved(pool):
    return pool.reshape(pool.shape[:-3] + (pool.shape[-3] * NSA_KV_HEADS, HEAD_DIM))


def _compress_paged_kernel(pt_ref, pk_ref, pv_ref, wk_ref, wv_ref, ko_ref, vo_ref, kbuf, vbuf, sem):
    s, i = pl.program_id(0), pl.program_id(1)
    n_i = pl.num_programs(1)
    step = s * n_i + i
    last = pl.num_programs(0) * n_i - 1
    pair = _div(lax.broadcasted_iota(jnp.int32, (SUBLANES, HEAD_DIM), 0), NSA_KV_HEADS)

    def fetch(seq, blk, slot):
        for j in range(PAGES_PER_STEP):
            page = pt_ref[seq, blk * PAGES_PER_STEP + j]
            pltpu.make_async_copy(pk_ref.at[page], kbuf.at[slot, j], sem.at[slot]).start(priority=j % 2)
            pltpu.make_async_copy(pv_ref.at[page], vbuf.at[slot, j], sem.at[slot]).start(priority=(j + 1) % 2)

    @pl.when(step == 0)
    def _():
        fetch(0, 0, 0)

    slot = jnp.bitwise_and(step, 1)

    @pl.when(step < last)
    def _():
        wrap = i == n_i - 1
        fetch(jnp.where(wrap, s + 1, s), jnp.where(wrap, 0, i + 1), 1 - slot)

    pltpu.make_async_copy(kbuf.at[slot], kbuf.at[slot], sem.at[slot]).wait()
    pltpu.make_async_copy(vbuf.at[slot], vbuf.at[slot], sem.at[slot]).wait()

    def summaries(buf, w):
        sums = []
        for j in range(PAGES_PER_STEP):
            for b in range(BLOCKS_PER_PAGE):
                y = buf[slot, j, b * ROWS_PER_BLOCK:(b + 1) * ROWS_PER_BLOCK, :] * w
                acc = jnp.sum(y.reshape(ROWS_PER_BLOCK // SUBLANES, SUBLANES, HEAD_DIM), axis=0)
                shift = SUBLANES // 2
                while shift >= NSA_KV_HEADS:
                    acc = acc + pltpu.roll(acc, shift, 0)
                    shift //= 2
                sums.append(acc)
        tiles = []
        for t in range(len(sums) // BLOCKS_PER_TILE):
            tile = sums[t * BLOCKS_PER_TILE]
            for j in range(1, BLOCKS_PER_TILE):
                tile = jnp.where(pair == j, sums[t * BLOCKS_PER_TILE + j], tile)
            tiles.append(tile)
        return jnp.concatenate(tiles, axis=0)

    ko_ref[0] = summaries(kbuf, wk_ref[...])
    vo_ref[0] = summaries(vbuf, wv_ref[...])


def _compress_paged(pool_k, pool_v, page_table, w_cmp_k, w_cmp_v):
    n, n_pages = page_table.shape
    wk2 = jnp.repeat(w_cmp_k, NSA_KV_HEADS, axis=0)
    wv2 = jnp.repeat(w_cmp_v, NSA_KV_HEADS, axis=0)
    anywhere = pl.BlockSpec(memory_space=pl.ANY)
    wsp = pl.BlockSpec((ROWS_PER_BLOCK, HEAD_DIM), lambda s, i, pt: (0, 0))
    rows = PAGES_PER_STEP * BLOCKS_PER_PAGE * NSA_KV_HEADS
    osp = pl.BlockSpec((1, rows, HEAD_DIM), lambda s, i, pt: (s, i, 0))
    page_buf = pltpu.VMEM((2, PAGES_PER_STEP, ROWS_PER_PAGE, HEAD_DIM), F32)
    return pl.pallas_call(
        _compress_paged_kernel,
        out_shape=[jax.ShapeDtypeStruct((n, n_pages * BLOCKS_PER_PAGE * NSA_KV_HEADS, HEAD_DIM), F32)] * 2,
        grid_spec=pltpu.PrefetchScalarGridSpec(
            num_scalar_prefetch=1,
            grid=(n, n_pages // PAGES_PER_STEP),
            in_specs=[anywhere, anywhere, wsp, wsp],
            out_specs=[osp, osp],
            scratch_shapes=[page_buf, page_buf, pltpu.SemaphoreType.DMA((2,))]),
        compiler_params=pltpu.CompilerParams(dimension_semantics=("arbitrary", "arbitrary")),
        name="compress_paged",
    )(page_table, _interleaved(pool_k), _interleaved(pool_v), wk2, wv2)


SS_SEQ = 8
LOWEST = -3.0e38


def _sample_select_kernel(q_ref, kc_ref, vc_ref, ocmp_ref, sel_ref, *, pos):
    ss, nbk = q_ref.shape[0], kc_ref.shape[1]
    kv, g = NSA_KV_HEADS, NSA_GROUP
    head = lax.broadcasted_iota(jnp.int32, (NSA_HEADS, nbk), 0)
    col = lax.broadcasted_iota(jnp.int32, (NSA_HEADS, nbk), 1)
    m = (_mod(col, kv) == _div(head, g)) & ((_div(col, kv) + 1) * CMP_BLOCK - 1 <= pos)
    width = nbk + LANES
    rows = []
    for i in range(ss):
        q = (q_ref[i] * (HEAD_DIM ** -0.5)).astype(BF16)
        p = _masked_softmax(_dot_nt(q, kc_ref[i].astype(BF16)), m)
        ocmp_ref[i] = jnp.dot(p.astype(BF16), vc_ref[i].astype(BF16), preferred_element_type=F32)
        for k in range(kv):
            imp = jnp.sum(p[k * g:(k + 1) * g], axis=0, keepdims=True)
            rows.append(jnp.concatenate([imp, jnp.zeros((1, LANES), F32)], axis=1))
    nrow = ss * kv
    r_iota = lax.broadcasted_iota(jnp.int32, (nrow, width), 0)
    ccol = lax.broadcasted_iota(jnp.int32, (nrow, width), 1)
    cand = jnp.zeros((nrow, width), F32)
    for r, row in enumerate(rows):
        cand = jnp.where(r_iota == r, row, cand)
    cblk = _div(ccol, kv)
    cur = pos // CMP_BLOCK
    n_sel = -(-(pos + 1) // CMP_BLOCK)
    forced = (cblk == 0) | (cblk == cur) | (cblk == cur - 1)
    score = jnp.where(cblk > cur, NEG, jnp.where(forced, -NEG, cand))
    score = jnp.where((_mod(ccol, kv) == _mod(r_iota, kv)) & (cblk < n_sel), score, LOWEST)
    colf = ccol.astype(F32)
    lane = lax.broadcasted_iota(jnp.int32, (nrow, LANES), 1)
    sel = jnp.zeros((nrow, LANES), jnp.int32)
    for t in range(SEL_TOPK):
        mx = jnp.max(score, axis=1, keepdims=True)
        c = jnp.min(jnp.where(score == mx, colf, -LOWEST), axis=1, keepdims=True)
        picked = jnp.where(mx > 0.5 * NEG, _div(c.astype(jnp.int32), kv), -1)
        sel = jnp.where(lane == t, picked, sel)
        score = jnp.where(colf == c, LOWEST, score)
    sel_ref[...] = sel


def _sample_select(q, kc, vc, pos):
    n = q.shape[0]
    nbk = kc.shape[1]
    qsp = pl.BlockSpec((SS_SEQ, NSA_HEADS, HEAD_DIM), lambda i: (i, 0, 0))
    csp = pl.BlockSpec((SS_SEQ, nbk, HEAD_DIM), lambda i: (i, 0, 0))
    return pl.pallas_call(
        functools.partial(_sample_select_kernel, pos=pos),
        out_shape=[jax.ShapeDtypeStruct((n, NSA_HEADS, HEAD_DIM), F32),
                   jax.ShapeDtypeStruct((n * NSA_KV_HEADS, LANES), jnp.int32)],
        grid=(n // SS_SEQ,),
        in_specs=[qsp, csp, csp],
        out_specs=[qsp, pl.BlockSpec((SS_SEQ * NSA_KV_HEADS, LANES), lambda i: (i, 0))],
        compiler_params=pltpu.CompilerParams(dimension_semantics=("arbitrary",), vmem_limit_bytes=VMEM_LIMIT),
        name="sample_select",
    )(q, kc, vc)


N_SLOTS = NSA_KV_HEADS * SEL_TOPK


def _sample_attend_kernel(sel_ref, pt_ref, q_ref, knew_ref, vnew_ref, wknew_ref, wvnew_ref, wkb_ref, wvb_ref,
                          ocmp_ref, gl_ref, *rest, pos, past_blocks):
    del pt_ref
    kblk, vblk, o_ref = rest[:N_SLOTS], rest[N_SLOTS:2 * N_SLOTS], rest[2 * N_SLOTS]
    kv, g, rb = NSA_KV_HEADS, NSA_GROUP, ROWS_PER_BLOCK
    n = pl.program_id(0)
    q = q_ref[0] * (HEAD_DIM ** -0.5)
    row = lax.broadcasted_iota(jnp.int32, (rb, HEAD_DIM), 0)
    col = lax.broadcasted_iota(jnp.int32, (1, rb), 1)

    def new_block(ref):
        out = jnp.zeros((rb, HEAD_DIM), F32)
        for k in range(kv):
            out = jnp.where(row == k, ref[0, k:k + 1, :], out)
        return out

    def per_head_rows(ref):
        return jnp.concatenate([jnp.broadcast_to(ref[0, k:k + 1, :], (g, HEAD_DIM)) for k in range(kv)], axis=0)

    knew, vnew = new_block(knew_ref), new_block(vnew_ref)
    o_sel = []
    for k in range(kv):
        ks, vs, ms = [], [], []
        for j in range(SEL_TOPK):
            b = sel_ref[(n * kv + k) * SEL_TOPK + j]
            is_new = jnp.broadcast_to(b, (rb, HEAD_DIM)) >= past_blocks
            ks.append(jnp.where(is_new, knew, kblk[k * SEL_TOPK + j][0]))
            vs.append(jnp.where(is_new, vnew, vblk[k * SEL_TOPK + j][0]))
            first = jnp.where(b >= 0, b, 1 << 24) * CMP_BLOCK
            ms.append((first + _div(col, kv) <= pos) & (_mod(col, kv) == k))
        s = _dot_nt(q[k * g:(k + 1) * g].astype(BF16), jnp.concatenate(ks, axis=0).astype(BF16))
        p = _masked_softmax(s, jnp.concatenate(ms, axis=1))
        o_sel.append(jnp.dot(p.astype(BF16), jnp.concatenate(vs, axis=0).astype(BF16), preferred_element_type=F32))
    o_sel = jnp.concatenate(o_sel, axis=0)

    nwr = wkb_ref.shape[1]
    s_w = _dot_nt(q.astype(BF16), wkb_ref[0].astype(BF16))
    cw = lax.broadcasted_iota(jnp.int32, (NSA_HEADS, nwr), 1)
    hw = lax.broadcasted_iota(jnp.int32, (NSA_HEADS, nwr), 0)
    kpos = pos - nwr // kv + _div(cw, kv)
    mw = (kpos <= pos) & (pos - kpos < WINDOW) & (_mod(cw, kv) == _div(hw, g))
    s_n = jnp.sum(q * per_head_rows(wknew_ref), axis=1, keepdims=True)
    smw = jnp.where(mw, s_w, NEG)
    mx = jnp.maximum(jnp.max(smw, axis=1, keepdims=True), s_n)
    e_w = jnp.where(mw, jnp.exp(smw - mx), 0.0)
    e_n = jnp.exp(s_n - mx)
    den = jnp.sum(e_w, axis=1, keepdims=True) + e_n
    o_win = (jnp.dot(e_w.astype(BF16), wvb_ref[0].astype(BF16), preferred_element_type=F32)
             + e_n * per_head_rows(wvnew_ref)) / den

    gate = 1.0 / (1.0 + jnp.exp(-gl_ref[0]))
    o_ref[0] = gate[:, 0:1] * ocmp_ref[0] + gate[:, 1:2] * o_sel + gate[:, 2:3] * o_win


def _sample_attend(sel, page_table, q, sk, sv, wk, wv, win_k, win_v, pool_k, pool_v, o_cmp, gl, pos):
    n, n_pages = page_table.shape
    past_blocks = n_pages * BLOCKS_PER_PAGE

    def slot_spec(k, j):
        def imap(s, sel_r, pt_r):
            b = jnp.maximum(sel_r[(s * NSA_KV_HEADS + k) * SEL_TOPK + j], 0)
            page = jnp.minimum(_div(b, BLOCKS_PER_PAGE), n_pages - 1)
            return (pt_r[s * n_pages + page], _mod(b, BLOCKS_PER_PAGE), 0)
        return pl.BlockSpec((1, ROWS_PER_BLOCK, HEAD_DIM), imap)

    slots = [slot_spec(k, j) for k in range(NSA_KV_HEADS) for j in range(SEL_TOPK)]
    per_seq = lambda a: pl.BlockSpec((1,) + a.shape[1:], lambda s, sel_r, pt_r: (s, 0, 0))
    dense = [q, sk, sv, wk, wv, win_k, win_v, o_cmp, gl]
    return pl.pallas_call(
        functools.partial(_sample_attend_kernel, pos=pos, past_blocks=past_blocks),
        out_shape=jax.ShapeDtypeStruct((n, NSA_HEADS, HEAD_DIM), F32),
        grid_spec=pltpu.PrefetchScalarGridSpec(
            num_scalar_prefetch=2,
            grid=(n,),
            in_specs=[per_seq(a) for a in dense] + slots + slots,
            out_specs=pl.BlockSpec((1, NSA_HEADS, HEAD_DIM), lambda s, sel_r, pt_r: (s, 0, 0))),
        compiler_params=pltpu.CompilerParams(dimension_semantics=("arbitrary",), vmem_limit_bytes=VMEM_LIMIT),
        name="sample_attend",
    )(sel, page_table.reshape(-1), *dense, *([pool_k] * N_SLOTS), *([pool_v] * N_SLOTS))


def _retention_tables(seq):
    c = RET_CHUNK
    log_g = jnp.log1p(-jnp.exp2(-5.0 - jnp.arange(RET_HEADS, dtype=F32)))
    i = jnp.arange(c, dtype=F32)
    rel = i[:, None] - i[None, :]
    dmask = jnp.where(rel[None] >= 0, jnp.exp(jnp.maximum(rel[None], 0.0) * log_g[:, None, None]), 0.0)
    q_dec = jnp.exp((i + 1.0)[None] * log_g[:, None])[..., None]
    k_dec = jnp.exp((c - 1.0 - i)[None] * log_g[:, None])[..., None]
    c_dec = jnp.exp(c * log_g)[:, None, None]
    bc = lambda t: jnp.broadcast_to(t, (RET_HEADS, c, HEAD_DIM))
    inv = RET_ROPE_THETA ** (-jnp.arange(0, HEAD_DIM, 2, dtype=F32) / HEAD_DIM)
    ang = jnp.arange(seq)[:, None].astype(F32) * inv
    cos, sin = jnp.cos(ang), jnp.sin(ang)
    return (dmask, bc(q_dec), bc(k_dec), jnp.broadcast_to(c_dec, (RET_HEADS, 1, HEAD_DIM)),
            jnp.concatenate([cos, cos], -1), jnp.concatenate([-sin, sin], -1))


def _retention_kernel(q_ref, k_ref, v_ref, g_ref, cos_ref, sin_ref, dmask_ref, qdec_ref, kdec_ref, cdec_ref, gn_ref,
                      o_ref, st_ref, s_scr):
    c = pl.program_id(1)

    @pl.when(c == 0)
    def _():
        s_scr[...] = jnp.zeros(s_scr.shape, F32)

    cosf, sins = cos_ref[...], sin_ref[...]
    half = HEAD_DIM // 2
    for h in range(RET_HEADS):
        sl = slice(h * HEAD_DIM, (h + 1) * HEAD_DIM)
        qh, kh = q_ref[:, sl], k_ref[:, sl]
        qr = qh * cosf + pltpu.roll(qh, half, 1) * sins
        kr = (kh * cosf + pltpu.roll(kh, half, 1) * sins) * (HEAD_DIM ** -0.5)
        vb = v_ref[:, sl].astype(BF16)
        att = _dot_nt(qr.astype(BF16), kr.astype(BF16)) * dmask_ref[h]
        s_prev = s_scr[h]
        o = (jnp.dot(att.astype(BF16), vb, preferred_element_type=F32)
             + jnp.dot((qr * qdec_ref[h]).astype(BF16), s_prev.astype(BF16), preferred_element_type=F32))
        s_scr[h] = cdec_ref[h] * s_prev + lax.dot_general(
            (kr * kdec_ref[h]).astype(BF16), vb, (((0,), (0,)), ((), ())), preferred_element_type=F32)
        mu = jnp.mean(o, axis=-1, keepdims=True)
        var = jnp.mean(jnp.square(o - mu), axis=-1, keepdims=True)
        on = (o - mu) * lax.rsqrt(var + GN_EPS) * gn_ref[:, sl]
        gg = g_ref[:, sl]
        o_ref[:, sl] = gg * (1.0 / (1.0 + jnp.exp(-gg))) * on

    @pl.when(c == pl.num_programs(1) - 1)
    def _():
        st_ref[0] = s_scr[...]


def _retention_prompt(main, gn_g, batch, seq):
    nc = seq // RET_CHUNK
    dmask, q_dec, k_dec, c_dec, cosf, sins = _retention_tables(seq)
    col = lambda j: pl.BlockSpec((RET_CHUNK, RET_W), lambda b, c: (b * nc + c, j))
    pos_tab = pl.BlockSpec((RET_CHUNK, HEAD_DIM), lambda b, c: (c, 0))
    full = lambda a: pl.BlockSpec(a.shape, lambda b, c: (0,) * a.ndim)
    gn = gn_g.reshape(1, RET_W)
    return pl.pallas_call(
        _retention_kernel,
        out_shape=[jax.ShapeDtypeStruct((batch * seq, RET_W), F32),
                   jax.ShapeDtypeStruct((batch, RET_HEADS, HEAD_DIM, HEAD_DIM), F32)],
        grid=(batch, nc),
        in_specs=[col(0), col(1), col(2), col(3), pos_tab, pos_tab,
                  full(dmask), full(q_dec), full(k_dec), full(c_dec), full(gn)],
        out_specs=[pl.BlockSpec((RET_CHUNK, RET_W), lambda b, c: (b * nc + c, 0)),
                   pl.BlockSpec((1, RET_HEADS, HEAD_DIM, HEAD_DIM), lambda b, c: (b, 0, 0, 0))],
        scratch_shapes=[pltpu.VMEM((RET_HEADS, HEAD_DIM, HEAD_DIM), F32)],
        compiler_params=pltpu.CompilerParams(dimension_semantics=("arbitrary", "arbitrary"),
                                             vmem_limit_bytes=VMEM_LIMIT),
        name="retention_prompt",
    )(main, main, main, main, cosf, sins, dmask, q_dec, k_dec, c_dec, gn)


NSA_TQ = 128
NSA_KEY_CHUNK = 512
NSA_GROUP = NSA_HEADS // NSA_KV_HEADS


def _dot_nt(a, b):
    return lax.dot_general(a, b, (((1,), (1,)), ((), ())), preferred_element_type=F32)


def _masked_softmax(s, m):
    sm = jnp.where(m, s, NEG)
    e = jnp.exp(sm - jnp.max(sm, axis=-1, keepdims=True))
    return jnp.where(m, e / jnp.sum(e, axis=-1, keepdims=True), 0.0)


def _select_mask_t(imp_t, pos_t):
    nb = imp_t.shape[0]
    blk = lax.broadcasted_iota(jnp.int32, imp_t.shape, 0)
    cur = jnp.right_shift(pos_t, int(math.log2(CMP_BLOCK)))
    forced = (blk == 0) | (blk == cur) | (blk == cur - 1)
    score = jnp.where(blk > cur, NEG, jnp.where(forced, -NEG, imp_t))
    rank = jnp.zeros(score.shape, jnp.int32)
    for i in range(nb):
        si = score[i:i + 1, :]
        ahead = (si > score) | ((si == score) & (blk > i))
        rank = rank + ahead.astype(jnp.int32)
    return (rank < SEL_TOPK) & (score > 0.5 * NEG)


def _nsa_prompt_kernel(q_ref, kc_ref, vc_ref, sk_ref, sv_ref, wk_ref, wv_ref, gl_ref, o_ref, skt, wkt):
    tq, g, kc_n = NSA_TQ, NSA_GROUP, NSA_KEY_CHUNK
    qi = pl.program_id(2)
    nb = kc_ref.shape[1]

    @pl.when(qi == 0)
    def _():
        blk_row = lax.broadcasted_iota(jnp.int32, (nb, kc_n), 0)
        key_col = lax.broadcasted_iota(jnp.int32, (nb, kc_n), 1)
        for c in range(skt.shape[0]):
            member = jnp.right_shift(key_col + c * kc_n, int(math.log2(CMP_BLOCK))) == blk_row
            skt[c] = jnp.concatenate([sk_ref[c * kc_n:(c + 1) * kc_n, :].T.astype(BF16),
                                      jnp.where(member, 1.0, 0.0).astype(BF16)], axis=0)
        for w in range(wkt.shape[0]):
            wkt[w] = wk_ref[w * tq:(w + 1) * tq, :].T.astype(BF16)

    q4 = q_ref[...] * (HEAD_DIM ** -0.5)
    qs = jnp.concatenate([q4[:, i * HEAD_DIM:(i + 1) * HEAD_DIM] for i in range(g)], axis=0).astype(BF16)
    pos = qi * tq + lax.broadcasted_iota(jnp.int32, (tq, 1), 0)
    pos4 = jnp.concatenate([pos] * g, axis=0)

    assert tq == LANES
    s_c = _dot_nt(kc_ref[0].astype(BF16), qs)
    blk_t = lax.broadcasted_iota(jnp.int32, (nb, g * tq), 0)
    pos_t = qi * tq + lax.broadcasted_iota(jnp.int32, (1, tq), 1)
    m_c = (blk_t + 1) * CMP_BLOCK - 1 <= jnp.concatenate([pos_t] * g, axis=1)
    sm_c = jnp.where(m_c, s_c, NEG)
    e_c = jnp.exp(sm_c - jnp.max(sm_c, axis=0, keepdims=True))
    p_c = jnp.where(m_c, e_c / jnp.sum(e_c, axis=0, keepdims=True), 0.0)
    o_cmp = lax.dot_general(p_c.astype(BF16), vc_ref[0].astype(BF16), (((0,), (0,)), ((), ())),
                            preferred_element_type=F32)
    imp_t = p_c[:, 0:tq]
    for i in range(1, g):
        imp_t = imp_t + p_c[:, i * tq:(i + 1) * tq]

    seln_t = jnp.where(_select_mask_t(imp_t, pos_t), 0.0, NEG)
    seln = jnp.concatenate([seln_t, jnp.zeros((tq - nb, tq), F32)], axis=0).T[:, :nb].astype(BF16)
    q_aug = jnp.concatenate([qs, jnp.concatenate([seln] * g, axis=0)], axis=1)
    key_row = lax.broadcasted_iota(jnp.int32, (1, kc_n), 1)

    def scores(c):
        return jnp.dot(q_aug, skt[c], preferred_element_type=F32)

    def values(c):
        return sv_ref[pl.ds(pl.multiple_of(c * kc_n, kc_n), kc_n), :].astype(BF16)

    assert kc_n % tq == 0
    c_diag = _div(qi, kc_n // tq)
    s_d = jnp.where(key_row + c_diag * kc_n <= pos4, scores(c_diag), NEG)
    m_d = jnp.max(s_d, axis=-1, keepdims=True)
    p_d = jnp.exp(s_d - m_d)
    first = (m_d, jnp.sum(p_d, axis=-1, keepdims=True),
             jnp.dot(p_d.astype(BF16), values(c_diag), preferred_element_type=F32))

    def chunk(c, carry):
        m_i, l_i, acc = carry
        s = scores(c)
        m_new = jnp.maximum(m_i, jnp.max(s, axis=-1, keepdims=True))
        alpha = jnp.exp(m_i - m_new)
        p = jnp.exp(s - m_new)
        l_new = alpha * l_i + jnp.sum(p, axis=-1, keepdims=True)
        return m_new, l_new, alpha * acc + jnp.dot(p.astype(BF16), values(c), preferred_element_type=F32)

    _, l_f, acc_f = lax.fori_loop(0, c_diag, chunk, first)
    o_sel = acc_f / l_f

    nwk = WINDOW + tq
    kstart = pl.multiple_of(jnp.maximum(qi * tq - WINDOW, 0), tq)
    kpos = kstart + lax.broadcasted_iota(jnp.int32, (1, nwk), 1)
    band = jnp.where((kpos <= pos) & (pos - kpos < WINDOW), 0.0, NEG)
    w0 = jnp.maximum(qi - WINDOW // tq, 0)
    kwt = jnp.concatenate([wkt[w0 + m] for m in range(nwk // tq)], axis=1)
    s_w = jnp.dot(qs, kwt, preferred_element_type=F32) + jnp.concatenate([band] * g, axis=0)
    e_w = jnp.exp(s_w - jnp.max(s_w, axis=-1, keepdims=True))
    o_win = (jnp.dot(e_w.astype(BF16), wv_ref[pl.ds(kstart, nwk), :].astype(BF16), preferred_element_type=F32)
             / jnp.sum(e_w, axis=-1, keepdims=True))

    gate = 1.0 / (1.0 + jnp.exp(-gl_ref[...]))
    for i in range(g):
        rows = slice(i * tq, (i + 1) * tq)
        o_ref[:, i * HEAD_DIM:(i + 1) * HEAD_DIM] = (gate[:, 3 * i:3 * i + 1] * o_cmp[rows]
                                                     + gate[:, 3 * i + 1:3 * i + 2] * o_sel[rows]
                                                     + gate[:, 3 * i + 2:3 * i + 3] * o_win[rows])


def _nsa_prompt(main, kc, vc, batch, seq):
    nq = seq // NSA_TQ
    gw = NSA_GROUP * HEAD_DIM
    qcol, glcol = _split_col0(4) // gw, GL_COL0 // LANES
    seq_spec = lambda split: pl.BlockSpec((seq, HEAD_DIM),
                                          lambda b, k, i: (b, _split_col0(split) // HEAD_DIM + k))
    cmp_spec = pl.BlockSpec((1, seq // CMP_BLOCK, HEAD_DIM), lambda b, k, i: (b, 0, k))
    return pl.pallas_call(
        _nsa_prompt_kernel,
        out_shape=jax.ShapeDtypeStruct((batch * seq, NSA_W), F32),
        grid=(batch, NSA_KV_HEADS, nq),
        in_specs=[pl.BlockSpec((NSA_TQ, gw), lambda b, k, i: (b * nq + i, qcol + k)), cmp_spec, cmp_spec,
                  seq_spec(7), seq_spec(8), seq_spec(9), seq_spec(10),
                  pl.BlockSpec((NSA_TQ, LANES), lambda b, k, i: (b * nq + i, glcol + k))],
        out_specs=pl.BlockSpec((NSA_TQ, gw), lambda b, k, i: (b * nq + i, k)),
        scratch_shapes=[pltpu.VMEM((seq // NSA_KEY_CHUNK, HEAD_DIM + seq // CMP_BLOCK, NSA_KEY_CHUNK), BF16),
                        pltpu.VMEM((seq // NSA_TQ, HEAD_DIM, NSA_TQ), BF16)],
        compiler_params=pltpu.CompilerParams(dimension_semantics=("arbitrary",) * 3,
                                             vmem_limit_bytes=VMEM_LIMIT),
        name="nsa_prompt",
    )(main, kc, vc, main, main, main, main, main)


WO_TM = 512


def _layer_norm_rows(y, g, b):
    mu = jnp.mean(y, axis=-1, keepdims=True)
    var = jnp.mean(jnp.square(y - mu), axis=-1, keepdims=True)
    return (y - mu) * lax.rsqrt(var + LN_EPS) * g + b


EXPERT_LANE0 = N_GROUPS
R_EID, R_RANK, R_GATE = 0, EXPERT_TOPK, 2 * EXPERT_TOPK


def _route(h, wr_ref, br_ref, carry):
    tm = h.shape[0]
    logit = jnp.dot(h.astype(BF16), wr_ref[...], preferred_element_type=F32) + br_ref[...]
    lane = lax.broadcasted_iota(jnp.int32, (tm, LANES), 1)
    lanef = lane.astype(F32)
    first_lane = lambda hit: jnp.min(jnp.where(hit, lanef, float(LANES)), axis=1, keepdims=True)
    is_g = lane < N_GROUPS
    gl = jnp.where(is_g, logit, LOWEST)
    gmx = jnp.max(gl, axis=1, keepdims=True)
    grp = first_lane(gl == gmx)
    p_grp = 1.0 / jnp.sum(jnp.where(is_g, jnp.exp(gl - gmx), 0.0), axis=1, keepdims=True)
    lane_grp = jnp.right_shift(lane - EXPERT_LANE0, int(math.log2(EXPERTS_PER_GROUP)))
    in_grp = lane_grp.astype(F32) == grp
    el = jnp.where(in_grp, logit, LOWEST)
    ee = jnp.where(in_grp, jnp.exp(el - jnp.max(el, axis=1, keepdims=True)), 0.0)
    pe = jnp.where(in_grp, ee / jnp.sum(ee, axis=1, keepdims=True), -1.0)
    p1 = jnp.max(pe, axis=1, keepdims=True)
    l1 = first_lane(pe == p1)
    pe2 = jnp.where(lanef == l1, -1.0, pe)
    p2 = jnp.max(pe2, axis=1, keepdims=True)
    l2 = first_lane(pe2 == p2)
    den = p1 + p2
    o1, o2 = lanef == l1, lanef == l2
    onehot = jnp.where(o1 | o2, 1.0, 0.0)
    r = lax.broadcasted_iota(jnp.int32, (tm, tm), 0)
    c = lax.broadcasted_iota(jnp.int32, (tm, tm), 1)
    earlier = jnp.where(c < r, 1.0, 0.0).astype(BF16)
    prefix = jnp.dot(earlier, onehot.astype(BF16), preferred_element_type=F32) + carry[0:1, :]
    rank1 = jnp.sum(jnp.where(o1, prefix, 0.0), axis=1, keepdims=True)
    rank2 = jnp.sum(jnp.where(o2, prefix, 0.0), axis=1, keepdims=True)
    carry[0:1, :] = carry[0:1, :] + jnp.sum(onehot, axis=0, keepdims=True)
    fields = [l1 - EXPERT_LANE0, l2 - EXPERT_LANE0, rank1, rank2, p_grp * p1 / den, p_grp * p2 / den]
    rec = jnp.zeros((tm, LANES), F32)
    for j, f in enumerate(fields):
        rec = jnp.where(lane == j, f, rec)
    return rec


def _pack_bf16_pairs(x):
    w = x.shape[1] // 2
    return pltpu.pack_elementwise([x[:, :w], x[:, w:]], packed_dtype=jnp.bfloat16)


def _rows_to_tiles(ref, index, p):
    r = p.shape[0]
    for s in range(SUBLANES):
        ref[index + (pl.ds(s, r, stride=SUBLANES), slice(None))] = p[:, s * LANES:(s + 1) * LANES]


def _tiles_to_rows(ref, index, r):
    return jnp.concatenate([ref[index + (pl.ds(s, r, stride=SUBLANES), slice(None))] for s in range(SUBLANES)],
                           axis=1)


def _unpack_bf16_pairs(p):
    halves = [pltpu.unpack_elementwise(p, index=i, packed_dtype=jnp.bfloat16, unpacked_dtype=F32) for i in (0, 1)]
    return jnp.concatenate(halves, axis=1)


def _wo_ln_route_kernel(fr_ref, fn_ref, x_ref, w_ref, g_ref, b_ref, wr_ref, br_ref, base_ref,
                        h_ref, hp_ref, route_ref, cnt_ref, carry):
    i = pl.program_id(0)

    @pl.when(i == 0)
    def _():
        carry[...] = base_ref[...]

    y = (DEEPNORM_ALPHA * x_ref[...]
         + jnp.dot(fr_ref[...].astype(BF16), w_ref[0:RET_W, :], preferred_element_type=F32)
         + jnp.dot(fn_ref[...].astype(BF16), w_ref[RET_W:MIX_W, :], preferred_element_type=F32))
    h = _layer_norm_rows(y, g_ref[...], b_ref[...])
    h_ref[...] = h
    _rows_to_tiles(hp_ref, (), _pack_bf16_pairs(h))
    route_ref[...] = _route(h, wr_ref, br_ref, carry)

    @pl.when(i == pl.num_programs(0) - 1)
    def _():
        cnt_ref[...] = carry[...]


def _wo_ln_route(f_ret, f_nsa, x, w_o_bf16, ln_g, ln_b, w_route, b_route, base_counts):
    t, d = x.shape
    tm = min(WO_TM, t)
    assert t % tm == 0
    row = lambda w: pl.BlockSpec((tm, w), lambda i: (i, 0))
    full = lambda a: pl.BlockSpec(a.shape, lambda i: (0,) * a.ndim)
    lg, lb = ln_g.reshape(1, d), ln_b.reshape(1, d)
    return pl.pallas_call(
        _wo_ln_route_kernel,
        out_shape=[jax.ShapeDtypeStruct((t, d), F32), jax.ShapeDtypeStruct((t * SUBLANES, LANES), jnp.uint32),
                   jax.ShapeDtypeStruct((t, LANES), F32), jax.ShapeDtypeStruct((SUBLANES, LANES), F32)],
        grid=(t // tm,),
        in_specs=[row(RET_W), row(NSA_W), row(d), full(w_o_bf16), full(lg), full(lb),
                  full(w_route), full(b_route), full(base_counts)],
        out_specs=[row(d), pl.BlockSpec((tm * SUBLANES, LANES), lambda i: (i, 0)), row(LANES),
                   pl.BlockSpec((SUBLANES, LANES), lambda i: (0, 0))],
        scratch_shapes=[pltpu.VMEM((SUBLANES, LANES), F32)],
        compiler_params=pltpu.CompilerParams(dimension_semantics=("arbitrary",), vmem_limit_bytes=VMEM_LIMIT),
        name="wo_ln1_route",
    )(f_ret, f_nsa, x, w_o_bf16, lg, lb, w_route, b_route, base_counts)


MOE_BM = 256
MOE_SUB = 64
TABLE_UNROLL = 8
META_N_USED, META_FILL_TRIPS, META_PAD_LO, META_PAD_HI = 0, 1, 2, 2 + N_EXPERTS


def _expert_kernel(blk_e_ref, meta_ref, slot_ref, h_ref, wg_ref, wu_ref, wd_ref, y_ref,
                   src_tok, dst_row, xbuf, obuf, gsem, ssem, *, plane):
    del blk_e_ref
    n_sub, sub = xbuf.shape[1], xbuf.shape[2] // SUBLANES
    bm = n_sub * sub
    i = pl.program_id(0)
    n_used = meta_ref[META_N_USED]
    n_asg = slot_ref.shape[0]
    dump0 = EXPERT_TOPK * plane
    assert bm & (bm - 1) == 0

    tile = lambda row: pl.ds(pl.multiple_of(row * SUBLANES, SUBLANES), SUBLANES)

    def gather(blk, buf_slot, j):
        base = blk * bm + j * sub
        for u in range(sub):
            pltpu.make_async_copy(h_ref.at[tile(src_tok[base + u]), :],
                                  xbuf.at[buf_slot, j, tile(u), :], gsem.at[buf_slot]).start(priority=u % 2)

    def scatter(blk, buf_slot):
        def sub_block(j, carry):
            base = blk * bm + j * sub
            for u in range(sub):
                pltpu.make_async_copy(obuf.at[buf_slot, j, tile(u), :],
                                      y_ref.at[tile(dst_row[base + u]), :], ssem.at[buf_slot]).start(priority=u % 2)
            return carry
        lax.fori_loop(0, n_sub, sub_block, 0)

    def wait_block(buf, sem, buf_slot):
        pltpu.make_async_copy(buf.at[buf_slot], buf.at[buf_slot], sem.at[buf_slot]).wait()

    @pl.when(i == 0)
    def _():
        def clear_expert(e, carry):
            def clear(r, c):
                src_tok[r] = 0
                dst_row[r] = dump0 + jnp.bitwise_and(r, 2 * bm - 1)
                return c
            lax.fori_loop(meta_ref[META_PAD_LO + e], meta_ref[META_PAD_HI + e], clear, 0)
            return carry
        lax.fori_loop(0, N_EXPERTS, clear_expert, 0)

        def fill(t, carry):
            for u in range(TABLE_UNROLL):
                a = t * TABLE_UNROLL + u
                tok = jnp.right_shift(a, 1)
                src_tok[slot_ref[a]] = tok
                dst_row[slot_ref[a]] = jnp.bitwise_and(a, 1) * plane + tok
            return carry
        lax.fori_loop(0, meta_ref[META_FILL_TRIPS], fill, 0)
        for j in range(n_sub):
            gather(0, 0, j)
        n_tok = h_ref.shape[0] // SUBLANES
        tail = plane - n_tok
        assert tail >= 0
        obuf[1] = jnp.zeros(obuf.shape[1:], obuf.dtype)
        spans = [(dump0, 2 * bm)] + ([(k * plane + n_tok, tail) for k in range(EXPERT_TOPK)] if tail else [])
        copies = []
        for first, count in spans:
            for j in range(-(-count // sub)):
                rows = min(sub, count - j * sub)
                copies.append(pltpu.make_async_copy(
                    obuf.at[1, j % n_sub, pl.ds(0, rows * SUBLANES), :],
                    y_ref.at[pl.ds((first + j * sub) * SUBLANES, rows * SUBLANES), :], ssem.at[1]))
        for cp in copies:
            cp.start()
        for cp in copies:
            cp.wait()

    slot = jnp.bitwise_and(i, 1)

    @pl.when(i < n_used)
    def _():
        nxt = jnp.minimum(i + 1, n_used - 1)
        assert n_sub == 4
        wait_block(xbuf, gsem, slot)

        @pl.when(i >= 2)
        def _():
            wait_block(obuf, ssem, slot)

        packed = jnp.concatenate([_tiles_to_rows(xbuf, (slot, j), sub) for j in range(n_sub)], axis=0)
        xb = _unpack_bf16_pairs(packed).astype(BF16)
        gather(nxt, 1 - slot, 0)
        hg = jnp.dot(xb, wg_ref[0].astype(BF16), preferred_element_type=F32)
        gather(nxt, 1 - slot, 1)
        hu = jnp.dot(xb, wu_ref[0].astype(BF16), preferred_element_type=F32)
        gather(nxt, 1 - slot, 2)
        hb = hg * (1.0 / (1.0 + jnp.exp(-hg))) * hu
        yb = jnp.dot(hb.astype(BF16), wd_ref[0].astype(BF16), preferred_element_type=F32)
        gather(nxt, 1 - slot, 3)
        yp = _pack_bf16_pairs(yb)
        for j in range(n_sub):
            _rows_to_tiles(obuf, (slot, j), yp[j * sub:(j + 1) * sub])
        scatter(i, slot)

    @pl.when(i == pl.num_programs(0) - 1)
    def _():
        wait_block(xbuf, gsem, jnp.bitwise_and(n_used, 1))

        @pl.when(n_used >= 2)
        def _():
            wait_block(obuf, ssem, jnp.bitwise_and(n_used, 1))
        wait_block(obuf, ssem, jnp.bitwise_and(n_used - 1, 1))


def _expert_ffn(h, slot, blk_e, meta, w_gate, w_up, w_down, plane):
    t = h.shape[0] // SUBLANES
    dp = SUBLANES * LANES
    d = 2 * dp
    assert w_gate.shape[1] == d
    n_asg = slot.shape[0]
    assert n_asg == t * EXPERT_TOPK and EXPERT_TOPK == 2
    n_blk = -(-(n_asg + N_EXPERTS * (MOE_BM - 1)) // MOE_BM)
    de = w_gate.shape[2]
    assert n_asg % TABLE_UNROLL == 0 and meta.shape == (META_PAD_HI + N_EXPERTS,)
    wspec = lambda shape: pl.BlockSpec((1,) + shape, lambda i, be, nu, sl: (be[i], 0, 0))
    return pl.pallas_call(
        functools.partial(_expert_kernel, plane=plane),
        out_shape=jax.ShapeDtypeStruct(((EXPERT_TOPK * plane + 2 * MOE_BM) * SUBLANES, LANES), jnp.uint32),
        grid_spec=pltpu.PrefetchScalarGridSpec(
            num_scalar_prefetch=3,
            grid=(n_blk,),
            in_specs=[pl.BlockSpec(memory_space=pl.ANY), wspec((d, de)), wspec((d, de)), wspec((de, d))],
            out_specs=pl.BlockSpec(memory_space=pl.ANY),
            scratch_shapes=[pltpu.SMEM((n_blk * MOE_BM,), jnp.int32), pltpu.SMEM((n_blk * MOE_BM,), jnp.int32),
                            pltpu.VMEM((2, MOE_BM // MOE_SUB, MOE_SUB * SUBLANES, LANES), jnp.uint32),
                            pltpu.VMEM((2, MOE_BM // MOE_SUB, MOE_SUB * SUBLANES, LANES), jnp.uint32),
                            pltpu.SemaphoreType.DMA((2,)), pltpu.SemaphoreType.DMA((2,))]),
        compiler_params=pltpu.CompilerParams(dimension_semantics=("arbitrary",), vmem_limit_bytes=VMEM_LIMIT),
        name="expert_ffn",
    )(blk_e, meta, slot, h, w_gate, w_up, w_down)


def _moe_ln_kernel(h_ref, y0_ref, y1_ref, route_ref, g_ref, b_ref, o_ref):
    rec = route_ref[...]
    tm = h_ref.shape[0]
    y = (DEEPNORM_ALPHA * h_ref[...]
         + rec[:, R_GATE:R_GATE + 1] * _unpack_bf16_pairs(_tiles_to_rows(y0_ref, (), tm))
         + rec[:, R_GATE + 1:R_GATE + 2] * _unpack_bf16_pairs(_tiles_to_rows(y1_ref, (), tm)))
    o_ref[...] = _layer_norm_rows(y, g_ref[...], b_ref[...])


def _moe_ln(h, y, route, ln_g, ln_b, row0, plane):
    n_rows, d = h.shape
    tm = min(WO_TM, n_rows)
    assert n_rows % tm == 0 and row0 % tm == 0 and plane % tm == 0 and EXPERT_TOPK == 2
    off = row0 // tm
    row = lambda w, o: pl.BlockSpec((tm, w), lambda i: (i + o, 0))
    tiles = lambda o: pl.BlockSpec((tm * SUBLANES, LANES), lambda i: (i + o, 0))
    vec = pl.BlockSpec((1, d), lambda i: (0, 0))
    return pl.pallas_call(
        _moe_ln_kernel,
        out_shape=jax.ShapeDtypeStruct((n_rows, d), F32),
        grid=(n_rows // tm,),
        in_specs=[row(d, 0), tiles(off), tiles(off + plane // tm), row(LANES, off), vec, vec],
        out_specs=row(d, 0),
        compiler_params=pltpu.CompilerParams(dimension_semantics=("arbitrary",), vmem_limit_bytes=VMEM_LIMIT),
        name="moe_ln2",
    )(h, y, y, route, ln_g.reshape(1, d), ln_b.reshape(1, d))


def _layer_norm(x, g, b):
    xf = x.astype(F32)
    mu = xf.mean(-1, keepdims=True)
    var = jnp.square(xf - mu).mean(-1, keepdims=True)
    return ((xf - mu) * lax.rsqrt(var + LN_EPS) * g + b).astype(x.dtype)


def _rope(x, pos, rot_dim, theta):
    half = rot_dim // 2
    inv = theta ** (-jnp.arange(0, rot_dim, 2, dtype=F32) / rot_dim)
    ang = pos[..., None].astype(F32) * inv
    cos = jnp.cos(ang)[:, :, None, :]
    sin = jnp.sin(ang)[:, :, None, :]
    xr = x[..., :rot_dim].astype(F32)
    x1, x2 = xr[..., :half], xr[..., half:]
    rot = jnp.concatenate([x1 * cos - x2 * sin, x2 * cos + x1 * sin], -1).astype(x.dtype)
    return jnp.concatenate([rot, x[..., rot_dim:]], -1)


def _heads(t, n):
    return t.reshape(t.shape[0], t.shape[1], n, HEAD_DIM)


def _chunk_retention(q, k, v, s0):
    n, l, h, d = q.shape
    c = RET_CHUNK if l % RET_CHUNK == 0 else l
    nc = l // c
    log_g = jnp.log1p(-jnp.exp2(-5.0 - jnp.arange(h, dtype=F32)))
    i = jnp.arange(c, dtype=F32)
    rel = i[:, None] - i[None, :]
    dmask = jnp.where(rel[None] >= 0, jnp.exp(jnp.maximum(rel[None], 0.0) * log_g[:, None, None]), 0.0)
    q_dec = jnp.exp((i + 1.0)[None] * log_g[:, None])[..., None]
    k_dec = jnp.exp((c - 1.0 - i)[None] * log_g[:, None])[..., None]
    c_dec = jnp.exp(c * log_g)[:, None, None]

    def to_chunks(t):
        return t.astype(F32).reshape(n, nc, c, h, d).transpose(1, 0, 3, 2, 4)

    def step(s, qkv):
        qc, kc, vc = qkv
        att = jnp.einsum('bhid,bhjd->bhij', qc, kc) * dmask
        o = jnp.einsum('bhij,bhjd->bhid', att, vc) + jnp.einsum('bhid,bhde->bhie', qc * q_dec, s)
        s = c_dec * s + jnp.einsum('bhjd,bhje->bhde', kc * k_dec, vc)
        return s, o

    s, o = lax.scan(step, s0.astype(F32), (to_chunks(q), to_chunks(k), to_chunks(v)))
    return o.transpose(1, 0, 3, 2, 4).reshape(n, l, h, d), s


def _retention_group(rq, rk, rv, rg, pos, s0, gn_g):
    q = _rope(_heads(rq, RET_HEADS), pos, HEAD_DIM, RET_ROPE_THETA)
    k = _rope(_heads(rk, RET_HEADS), pos, HEAD_DIM, RET_ROPE_THETA) * (HEAD_DIM ** -0.5)
    v = _heads(rv, RET_HEADS)
    o, s = _chunk_retention(q, k, v, s0)
    mu = o.mean(-1, keepdims=True)
    var = jnp.square(o - mu).mean(-1, keepdims=True)
    on = (o - mu) * lax.rsqrt(var + GN_EPS) * gn_g.reshape(RET_HEADS, HEAD_DIM).astype(F32)
    out = jax.nn.silu(rg.astype(F32)) * on.reshape(rg.shape)
    return out.astype(rq.dtype), s


def _gqa_attend(q, k, v, mask):
    n, lq, h, d = q.shape
    kv = k.shape[2]
    qg = q.reshape(n, lq, kv, h // kv, d)
    s = jnp.einsum('nqkgd,nskd->nkgqs', qg, k).astype(F32) * (d ** -0.5)
    m = mask[:, None, None]
    p = jax.nn.softmax(jnp.where(m, s, NEG), axis=-1) * m
    o = jnp.einsum('nkgqs,nskd->nqkgd', p.astype(v.dtype), v)
    return o.reshape(n, lq, h, d), p


def _nsa_heads(nq, ck, sk, wk, cv, sv, wv, pos):
    rp = lambda t, nh: _rope(_heads(t, nh), pos, ROT_DIM, ROPE_THETA)
    return (rp(nq, NSA_HEADS), rp(ck, NSA_KV_HEADS), rp(sk, NSA_KV_HEADS), rp(wk, NSA_KV_HEADS),
            _heads(cv, NSA_KV_HEADS), _heads(sv, NSA_KV_HEADS), _heads(wv, NSA_KV_HEADS))


def _compress(rows, w):
    n, t, kv, d = rows.shape
    return jnp.einsum('nbjkd,jd->nbkd', rows.reshape(n, t // CMP_BLOCK, CMP_BLOCK, kv, d), w)


def _cmp_branch(q, pos, kc, vc):
    nb = kc.shape[1]
    blk_end = (jnp.arange(nb) + 1) * CMP_BLOCK - 1
    mask = blk_end[None, None, :] <= pos[:, :, None]
    o, p = _gqa_attend(q, kc, vc, mask)
    imp = p.sum(axis=2).transpose(0, 2, 1, 3)
    return o, imp


def _select_blocks(imp, pos, n_sel):
    nb = imp.shape[-1]
    imp = jnp.pad(imp, ((0, 0), (0, 0), (0, 0), (0, n_sel - nb)))
    blk = jnp.arange(n_sel)
    cur = (pos // CMP_BLOCK)[:, :, None, None]
    forced = (blk == 0) | (blk == cur) | (blk == cur - 1)
    score = jnp.where(blk > cur, NEG, jnp.where(forced, -NEG, imp))
    top, idx = lax.top_k(score, min(SEL_TOPK, n_sel))
    return idx, top > 0.5 * NEG


def _sel_attend(q, pos, ks, vs, idx, valid):
    n, lq, kv, kk, cb, d = ks.shape
    h = q.shape[2]
    kpos = idx[..., None] * CMP_BLOCK + jnp.arange(CMP_BLOCK)
    m = ((kpos <= pos[:, :, None, None, None]) & valid[..., None]).reshape(n, lq, kv, 1, kk * cb)
    qg = q.reshape(n, lq, kv, h // kv, d)
    kf = ks.reshape(n, lq, kv, kk * cb, d)
    vf = vs.reshape(n, lq, kv, kk * cb, d)
    s = jnp.einsum('nqkgd,nqkjd->nqkgj', qg, kf).astype(F32) * (d ** -0.5)
    p = jax.nn.softmax(jnp.where(m, s, NEG), axis=-1) * m
    o = jnp.einsum('nqkgj,nqkjd->nqkgd', p.astype(vf.dtype), vf)
    return o.reshape(n, lq, h, d)


def _sel_prompt(q, pos, k, v, idx, valid):
    b, s, h, d = q.shape
    kv = k.shape[2]
    nb = s // CMP_BLOCK
    nq = s // SEL_Q_BLOCK
    kb = k.reshape(b, nb, CMP_BLOCK, kv, d).transpose(0, 3, 1, 2, 4)
    vb = v.reshape(b, nb, CMP_BLOCK, kv, d).transpose(0, 3, 1, 2, 4)
    bi = jnp.arange(b)[:, None, None, None]
    hi = jnp.arange(kv)[None, None, :, None]

    def blockwise(t):
        return t.reshape(t.shape[0], nq, SEL_Q_BLOCK, *t.shape[2:]).swapaxes(0, 1)

    def one(args):
        qc, pc, ic, vc = args
        return _sel_attend(qc, pc, kb[bi, hi, ic], vb[bi, hi, ic], ic, vc)

    o = lax.map(one, (blockwise(q), blockwise(pos), blockwise(idx), blockwise(valid)))
    return o.swapaxes(0, 1).reshape(b, s, h, d)


def _win_prompt(q, k, v):
    b, s, h, d = q.shape
    kv = k.shape[2]
    nb = s // WIN_Q_BLOCK
    nprev = WINDOW // WIN_Q_BLOCK
    nw = nprev + 1
    padw = ((0, 0), (WINDOW, 0), (0, 0), (0, 0))
    kp = jnp.pad(k, padw).reshape(b, nb + nprev, WIN_Q_BLOCK, kv, d)
    vp = jnp.pad(v, padw).reshape(b, nb + nprev, WIN_Q_BLOCK, kv, d)
    kw = jnp.concatenate([kp[:, i:i + nb] for i in range(nw)], axis=2)
    vw = jnp.concatenate([vp[:, i:i + nb] for i in range(nw)], axis=2)
    qpos = jnp.arange(s).reshape(nb, WIN_Q_BLOCK)
    kpos = (jnp.arange(nb)[:, None] - nprev) * WIN_Q_BLOCK + jnp.arange(nw * WIN_Q_BLOCK)[None]
    qq, kk = qpos[:, :, None], kpos[:, None, :]
    mask = (kk <= qq) & (qq - kk < WINDOW) & (kk >= 0)
    mask = jnp.broadcast_to(mask[None], (b,) + mask.shape).reshape(b * nb, WIN_Q_BLOCK, nw * WIN_Q_BLOCK)
    o, _ = _gqa_attend(q.reshape(b * nb, WIN_Q_BLOCK, h, d), kw.reshape(b * nb, nw * WIN_Q_BLOCK, kv, d),
                       vw.reshape(b * nb, nw * WIN_Q_BLOCK, kv, d), mask)
    return o.reshape(b, s, h, d)


def _gather_selected(pool, new_rows, page_table, idx):
    n, l, kv, d = new_rows.shape
    n_pages = page_table.shape[1]
    past_blocks = n_pages * PAGE_SIZE // CMP_BLOCK
    nbn = -(-l // CMP_BLOCK)
    newb = jnp.pad(new_rows, ((0, 0), (0, nbn * CMP_BLOCK - l), (0, 0), (0, 0)))
    newb = newb.reshape(n, nbn, CMP_BLOCK, kv, d).transpose(0, 3, 1, 2, 4)
    bi = jnp.arange(n)[:, None, None, None]
    hi = jnp.arange(kv)[None, None, :, None]
    start = idx * CMP_BLOCK
    phys = page_table[bi, jnp.minimum(start // PAGE_SIZE, n_pages - 1)]
    off = (start % PAGE_SIZE)[..., None] + jnp.arange(CMP_BLOCK)
    past = pool[phys[..., None], off, hi[..., None]]
    new = newb[bi, hi, jnp.clip(idx - past_blocks, 0, nbn - 1)]
    return jnp.where((idx < past_blocks)[..., None, None], past, new)


def _nsa_combine(gl, o_cmp, o_sel, o_win):
    n, l = gl.shape[0], gl.shape[1]
    g = jax.nn.sigmoid(gl.astype(F32)).reshape(n, l, NSA_HEADS, 3, 1)
    o = g[..., 0, :] * o_cmp + g[..., 1, :] * o_sel + g[..., 2, :] * o_win
    return o.reshape(n, l, NSA_W).astype(o_cmp.dtype)


def _prompt_mixer(x, win_buf, w_in, w_cmp_k, w_cmp_v, gn_g):
    n, s, _ = x.shape
    main, (ck, cv, sk, sv, wk, wv) = _project(x, w_in, jnp.arange(s))
    ret_out, s_fin = _retention_prompt(main, gn_g, n, s)
    kc, vc = _compress_prompt(main, w_cmp_k, w_cmp_v)
    nsa = _nsa_prompt(main, kc.reshape(n, s // CMP_BLOCK, KV_W), vc.reshape(n, s // CMP_BLOCK, KV_W), n, s)
    feats = (ret_out, nsa)
    if s >= win_buf:
        bk, bv = wk[:, s - win_buf:], wv[:, s - win_buf:]
    else:
        padb = ((0, 0), (win_buf - s, 0), (0, 0), (0, 0))
        bk, bv = jnp.pad(wk, padb), jnp.pad(wv, padb)
    return feats, (ck, cv, sk, sv, bk, bv, s_fin)


def _sample_mixer(x, c_cmp_k, c_cmp_v, c_sel_k, c_sel_v, c_win_k, c_win_v, s_ret, page_table,
                  w_in, w_cmp_k, w_cmp_v, gn_g):
    n, l, _ = x.shape
    past = page_table.shape[1] * PAGE_SIZE
    pos = past + jnp.arange(l)[None]
    assert l == 1
    main, (ck, cv, sk, sv, wk, wv) = _project(x, w_in, jnp.full((n,), past, jnp.int32))
    rq, rk, rv, rg, nq = _split_main(main, n, l)[:5]
    gl = _gate_logits(main, n, l)
    ret_out, s_new = _retention_group(rq, rk, rv, rg, pos, s_ret, gn_g)
    q = _heads(nq, NSA_HEADS)
    assert l == 1 and past % CMP_BLOCK == 0 and c_win_k.shape[1] <= WINDOW
    kc, vc = _compress_paged(c_cmp_k, c_cmp_v, page_table, w_cmp_k, w_cmp_v)
    o_cmp, sel = _sample_select(q[:, 0], kc, vc, past)
    sel = sel[:, :SEL_TOPK].reshape(-1)
    gl_pad = jnp.pad(gl.reshape(n, NSA_HEADS, 3), ((0, 0), (0, 0), (0, LANES - 3)))
    nsa = _sample_attend(sel, page_table, q[:, 0], sk[:, 0], sv[:, 0], wk[:, 0], wv[:, 0],
                         _interleaved(c_win_k), _interleaved(c_win_v), _interleaved(c_sel_k), _interleaved(c_sel_v),
                         o_cmp, gl_pad, past)
    feats = (ret_out.reshape(n * l, RET_W), nsa.reshape(n * l, NSA_W))
    kw = jnp.concatenate([c_win_k, wk], 1)
    vw = jnp.concatenate([c_win_v, wv], 1)
    return feats, (ck, cv, sk, sv, kw[:, l:], vw[:, l:], s_new)


def _route_params(w_group, b_group, w_expert, b_expert):
    w = jnp.concatenate([w_group, w_expert], axis=1)
    b = jnp.concatenate([b_group, b_expert], axis=0)
    pad = LANES - w.shape[1]
    return jnp.pad(w, ((0, 0), (0, pad))).astype(BF16), jnp.pad(b, (0, pad)).reshape(1, LANES)


def _dispatch_plan(route, counts_tile):
    counts = counts_tile[0, EXPERT_LANE0:EXPERT_LANE0 + N_EXPERTS].astype(jnp.int32)
    padded = (counts + MOE_BM - 1) // MOE_BM * MOE_BM
    pad_end = jnp.cumsum(padded)
    pad_start = pad_end - padded
    n_asg = route.shape[0] * EXPERT_TOPK
    n_blk = -(-(n_asg + N_EXPERTS * (MOE_BM - 1)) // MOE_BM)
    blk_first = jnp.arange(n_blk, dtype=jnp.int32) * MOE_BM
    blk_e = jnp.minimum(jnp.sum(pad_end[None, :] <= blk_first[:, None], axis=1), N_EXPERTS - 1).astype(jnp.int32)
    meta = jnp.concatenate([pad_end[-1:] // MOE_BM, jnp.array([n_asg // TABLE_UNROLL]), pad_start + counts,
                            pad_end]).astype(jnp.int32)
    eid = route[:, R_EID:R_EID + EXPERT_TOPK].astype(jnp.int32)
    rank = route[:, R_RANK:R_RANK + EXPERT_TOPK].astype(jnp.int32)
    start = jnp.sum(jnp.where(eid[..., None] == jnp.arange(N_EXPERTS), pad_start, 0), axis=-1)
    return (start + rank).reshape(-1).astype(jnp.int32), blk_e, meta


def kernel(x_prompt, x_sample, cache_cmp_k, cache_cmp_v, cache_sel_k, cache_sel_v, cache_win_k, cache_win_v,
           state_ret, page_table, w_in, w_cmp_k, w_cmp_v, ret_gn_g, w_o, ln1_g, ln1_b, w_group, b_group,
           w_expert, b_expert, w_gate, w_up, w_down, ln2_g, ln2_b):
    win_buf = cache_win_k.shape[2]
    hp, hs = x_prompt, x_sample
    acc_p = [[] for _ in range(7)]
    acc_s = [[] for _ in range(7)]
    for l in range(DEPTH):
        fp, st_p = _prompt_mixer(hp, win_buf, w_in[l], w_cmp_k[l], w_cmp_v[l], ret_gn_g[l])
        fs, st_s = _sample_mixer(hs, cache_cmp_k[l], cache_cmp_v[l], cache_sel_k[l], cache_sel_v[l],
                                 cache_win_k[l], cache_win_v[l], state_ret[l], page_table,
                                 w_in[l], w_cmp_k[l], w_cmp_v[l], ret_gn_g[l])
        w_o_bf16 = w_o[l].astype(BF16)
        w_route, b_route = _route_params(w_group[l], b_group[l], w_expert[l], b_expert[l])
        tp = hp.shape[0] * hp.shape[1]
        ts = hs.shape[0] * hs.shape[1]
        no_counts = jnp.zeros((SUBLANES, LANES), F32)
        h1p, packed_p, route_p, counts_p = _wo_ln_route(fp[0], fp[1], hp.reshape(tp, D_MODEL), w_o_bf16, ln1_g[l],
                                                        ln1_b[l], w_route, b_route, no_counts)
        h1s, packed_s, route_s, counts = _wo_ln_route(fs[0], fs[1], hs.reshape(ts, D_MODEL), w_o_bf16, ln1_g[l],
                                                      ln1_b[l], w_route, b_route, counts_p)
        packed = jnp.concatenate([packed_p, packed_s], axis=0)
        route = jnp.concatenate([route_p, route_s], axis=0)
        slot, blk_e, meta = _dispatch_plan(route, counts)
        plane = -(-(tp + ts) // WO_TM) * WO_TM
        y = _expert_ffn(packed, slot, blk_e, meta, w_gate[l], w_up[l], w_down[l], plane)
        hp = _moe_ln(h1p, y, route, ln2_g[l], ln2_b[l], 0, plane).reshape(hp.shape)
        hs = _moe_ln(h1s, y, route, ln2_g[l], ln2_b[l], tp, plane).reshape(hs.shape)
        for acc, t in zip(acc_p, st_p):
            acc.append(t)
        for acc, t in zip(acc_s, st_s):
            acc.append(t)
    p_cmp_k, p_cmp_v, p_sel_k, p_sel_v, p_win_k, p_win_v, p_ret = [jnp.stack(a) for a in acc_p]
    s_cmp_k, s_cmp_v, s_sel_k, s_sel_v, s_win_k, s_win_v, s_ret = [jnp.stack(a) for a in acc_s]
    return (hp, hs, p_cmp_k, p_cmp_v, p_sel_k, p_sel_v, p_win_k, p_win_v, p_ret.astype(state_ret.dtype),
            s_cmp_k, s_cmp_v, s_sel_k, s_sel_v, s_win_k, s_win_v, s_ret.astype(state_ret.dtype))
```

```python
import functools
import math

import jax
import jax.numpy as jnp
import numpy as np
from jax import lax
from jax.experimental import pallas as pl
from jax.experimental.pallas import tpu as pltpu

D_MODEL = 2048
DEPTH = 1
PAGE_SIZE = 128

F32 = jnp.float32
BF16 = jnp.bfloat16
HEAD_DIM = 128
RET_HEADS = D_MODEL // (2 * HEAD_DIM)
NSA_HEADS = D_MODEL // (2 * HEAD_DIM)
NSA_KV_HEADS = 2
RET_W = RET_HEADS * HEAD_DIM
NSA_W = NSA_HEADS * HEAD_DIM
KV_W = NSA_KV_HEADS * HEAD_DIM
MIX_W = RET_W + NSA_W
RET_CHUNK = 128
RET_ROPE_THETA = 10000.0
ROPE_THETA = 500000.0
ROT_DIM = HEAD_DIM // 4
CMP_BLOCK = 64
SEL_TOPK = 16
WINDOW = 512
WIN_Q_BLOCK = 128
SEL_Q_BLOCK = 64
N_GROUPS = 4
EXPERTS_PER_GROUP = 8
N_EXPERTS = N_GROUPS * EXPERTS_PER_GROUP
EXPERT_TOPK = 2
D_EXPERT = 512
MOE_BLOCK = 128
LN_EPS = 1e-5
GN_EPS = 1e-5
NEG = -1e30
DEEPNORM_ALPHA = (2 * DEPTH) ** 0.25
DEEPNORM_BETA = (8 * DEPTH) ** -0.25
SPLITS = (RET_W, RET_W, RET_W, RET_W, NSA_W, KV_W, KV_W, KV_W, KV_W, KV_W, KV_W, NSA_HEADS * 3)
IN_W = sum(SPLITS)
GATE_W = NSA_HEADS * 3
MAIN_W = IN_W - GATE_W
LANES = 128
VMEM_LIMIT = 48 * 1024 * 1024


PROJ_TN = 512
GL_COL0 = MAIN_W
PROJ_W = -(-(MAIN_W + NSA_KV_HEADS * LANES) // PROJ_TN) * PROJ_TN
ROTATED_SPLITS = (4, 5, 7, 9)


def _rotated_heads():
    cuts = np.cumsum((0,) + SPLITS)
    tiles = []
    for j in range(PROJ_W // PROJ_TN):
        heads = []
        for h in range(PROJ_TN // HEAD_DIM):
            c0 = j * PROJ_TN + h * HEAD_DIM
            split = int(np.searchsorted(cuts, c0, side="right")) - 1
            heads.append(split in ROTATED_SPLITS and c0 < MAIN_W)
        tiles.append(tuple(heads))
    return tiles


def _nsa_rope_tables(pos):
    half = ROT_DIM // 2
    inv = ROPE_THETA ** (-jnp.arange(0, ROT_DIM, 2, dtype=F32) / ROT_DIM)
    ang = pos[:, None].astype(F32) * inv
    cos, sin = jnp.cos(ang), jnp.sin(ang)
    rest = HEAD_DIM - ROT_DIM
    zeros = jnp.zeros((pos.shape[0], half), F32)
    pad = lambda t, fill: jnp.pad(t, ((0, 0), (0, rest)), constant_values=fill)
    return (pad(jnp.concatenate([cos, cos], 1), 1.0), pad(jnp.concatenate([zeros, sin], 1), 0.0),
            pad(jnp.concatenate([-sin, zeros], 1), 0.0))


KV_SPLITS = (5, 6, 7, 8, 9, 10)


def _kv_heads():
    cuts = np.cumsum((0,) + SPLITS)
    tiles = []
    for j in range(MAIN_W // PROJ_TN):
        heads = []
        for h in range(PROJ_TN // HEAD_DIM):
            c0 = j * PROJ_TN + h * HEAD_DIM
            split = int(np.searchsorted(cuts, c0, side="right")) - 1
            heads.append((KV_SPLITS.index(split), (c0 - int(cuts[split])) // HEAD_DIM) if split in KV_SPLITS else None)
        tiles.append(tuple(heads))
    return tiles


def _proj_kernel(x_ref, w_ref, wt_ref, cos_ref, up_ref, dn_ref, o_ref, *kv_refs, patterns, kv_heads):
    j = pl.program_id(1)
    n_main = MAIN_W // PROJ_TN
    xb = x_ref[...].astype(BF16)
    half = ROT_DIM // 2

    def rotated(acc, heads):
        cos, up, dn = cos_ref[...], up_ref[...], dn_ref[...]
        parts = []
        for h, rot in enumerate(heads):
            xh = acc[:, h * HEAD_DIM:(h + 1) * HEAD_DIM]
            if rot:
                xh = xh * cos + pltpu.roll(xh, half, 1) * up + pltpu.roll(xh, HEAD_DIM - half, 1) * dn
            parts.append(xh)
        return jnp.concatenate(parts, axis=1)

    plans = list(zip(patterns[:n_main], kv_heads))
    for plan in sorted(set(plans), key=repr):
        tiles = [t for t, p in enumerate(plans) if p == plan]
        hit = functools.reduce(jnp.logical_or, [j == t for t in tiles])

        @pl.when(hit)
        def _(plan=plan):
            heads, dests = plan
            acc = jnp.dot(xb, w_ref[...], preferred_element_type=F32)
            vals = rotated(acc, heads) if any(heads) else acc
            o_ref[...] = vals
            for h, dest in enumerate(dests):
                if dest is not None:
                    out, k = dest
                    kv_refs[out][pl.ds(k, vals.shape[0], stride=NSA_KV_HEADS), :] = (
                        vals[:, h * HEAD_DIM:(h + 1) * HEAD_DIM])

    @pl.when(j >= n_main)
    def _():
        o_ref[...] = jnp.dot(xb, wt_ref[...], preferred_element_type=F32)


def _project(x, w_in, pos):
    n, l, d = x.shape
    t = n * l
    xt = x.reshape(t, d)
    tm = min(1024, t)
    gpg = GATE_W // NSA_KV_HEADS
    n_main = MAIN_W // PROJ_TN
    assert PROJ_W == MAIN_W + PROJ_TN and not any(any(p) for p in _rotated_heads()[n_main:])
    gate_tiles = [jnp.pad(w_in[:, MAIN_W + k * gpg:MAIN_W + (k + 1) * gpg], ((0, 0), (0, LANES - gpg)))
                  for k in range(NSA_KV_HEADS)]
    fill = jnp.zeros((d, PROJ_TN - NSA_KV_HEADS * LANES), F32)
    w_tail = jnp.concatenate(gate_tiles + [fill], axis=1).astype(BF16)
    tables = _nsa_rope_tables(pos)
    pb = pos.shape[0] // tm
    tab = pl.BlockSpec((tm, HEAD_DIM), lambda i, j: (i % pb, 0))
    kv_rows = pl.BlockSpec((tm * NSA_KV_HEADS, HEAD_DIM), lambda i, j: (i, 0))
    outs = pl.pallas_call(
        functools.partial(_proj_kernel, patterns=_rotated_heads(), kv_heads=_kv_heads()),
        out_shape=[jax.ShapeDtypeStruct((t, PROJ_W), F32)]
        + [jax.ShapeDtypeStruct((t * NSA_KV_HEADS, HEAD_DIM), F32)] * len(KV_SPLITS),
        grid=(t // tm, PROJ_W // PROJ_TN),
        in_specs=[pl.BlockSpec((tm, d), lambda i, j: (i, 0)),
                  pl.BlockSpec((d, PROJ_TN), lambda i, j: (0, jnp.minimum(j, n_main - 1))),
                  pl.BlockSpec((d, PROJ_TN), lambda i, j: (0, 0)), tab, tab, tab],
        out_specs=[pl.BlockSpec((tm, PROJ_TN), lambda i, j: (i, j))] + [kv_rows] * len(KV_SPLITS),
        compiler_params=pltpu.CompilerParams(dimension_semantics=("arbitrary", "arbitrary"),
                                             vmem_limit_bytes=VMEM_LIMIT),
        name="in_proj",
    )(xt, w_in.astype(BF16), w_tail, *tables)
    return outs[0], [o.reshape(n, l, NSA_KV_HEADS, HEAD_DIM) for o in outs[1:]]


def _split_main(main, n, l, first=0):
    cuts = [0] + [int(c) for c in np.cumsum(SPLITS)[:-1]]
    return [main[:, cuts[i]:cuts[i + 1]].reshape(n, l, -1) for i in range(first, len(SPLITS) - 1)]


def _split_col0(i):
    return int(np.cumsum((0,) + SPLITS)[i])


def _gate_logits(main, n, l):
    gpg = GATE_W // NSA_KV_HEADS
    cols = [main[:, GL_COL0 + k * LANES:GL_COL0 + k * LANES + gpg] for k in range(NSA_KV_HEADS)]
    return jnp.concatenate(cols, axis=1).reshape(n, l, GATE_W)


def _compress_kernel(k_ref, v_ref, wk_ref, wv_ref, ko_ref, vo_ref):
    r = ko_ref.shape[0]
    ko_ref[...] = jnp.sum(k_ref[...].reshape(r, CMP_BLOCK, KV_W) * wk_ref[...][None], axis=1)
    vo_ref[...] = jnp.sum(v_ref[...].reshape(r, CMP_BLOCK, KV_W) * wv_ref[...][None], axis=1)


def _compress_prompt(main, w_cmp_k, w_cmp_v):
    t = main.shape[0]
    r = 32
    wk2 = jnp.tile(w_cmp_k, (1, NSA_KV_HEADS))
    wv2 = jnp.tile(w_cmp_v, (1, NSA_KV_HEADS))
    kcol, vcol = _split_col0(5) // KV_W, _split_col0(6) // KV_W
    wsp = pl.BlockSpec((CMP_BLOCK, KV_W), lambda i: (0, 0))
    osp = pl.BlockSpec((r, KV_W), lambda i: (i, 0))
    return pl.pallas_call(
        _compress_kernel,
        out_shape=[jax.ShapeDtypeStruct((t // CMP_BLOCK, KV_W), F32)] * 2,
        grid=(t // (r * CMP_BLOCK),),
        in_specs=[pl.BlockSpec((r * CMP_BLOCK, KV_W), lambda i: (i, kcol)),
                  pl.BlockSpec((r * CMP_BLOCK, KV_W), lambda i: (i, vcol)), wsp, wsp],
        out_specs=[osp, osp],
        compiler_params=pltpu.CompilerParams(dimension_semantics=("arbitrary",)),
        name="compress_prompt",
    )(main, main, wk2, wv2)


PAGES_PER_STEP = 32
BLOCKS_PER_PAGE = PAGE_SIZE // CMP_BLOCK


SUBLANES = 8
ROWS_PER_BLOCK = CMP_BLOCK * NSA_KV_HEADS
ROWS_PER_PAGE = PAGE_SIZE * NSA_KV_HEADS
BLOCKS_PER_TILE = SUBLANES // NSA_KV_HEADS


def _div(x, n):
    assert n & (n - 1) == 0
    return jnp.right_shift(x, n.bit_length() - 1)


def _mod(x, n):
    assert n & (n - 1) == 0
    return jnp.bitwise_and(x, n - 1)


def _interleaved(pool):
    return pool.reshape(pool.shape[:-3] + (pool.shape[-3] * NSA_KV_HEADS, HEAD_DIM))


def _compress_paged_kernel(pt_ref, pk_ref, pv_ref, wk_ref, wv_ref, ko_ref, vo_ref, kbuf, vbuf, sem):
    s, i = pl.program_id(0), pl.program_id(1)
    n_i = pl.num_programs(1)
    step = s * n_i + i
    last = pl.num_programs(0) * n_i - 1
    pair = _div(lax.broadcasted_iota(jnp.int32, (SUBLANES, HEAD_DIM), 0), NSA_KV_HEADS)

    def fetch(seq, blk, slot):
        for j in range(PAGES_PER_STEP):
            page = pt_ref[seq, blk * PAGES_PER_STEP + j]
            pltpu.make_async_copy(pk_ref.at[page], kbuf.at[slot, j], sem.at[slot]).start(priority=j % 2)
            pltpu.make_async_copy(pv_ref.at[page], vbuf.at[slot, j], sem.at[slot]).start(priority=(j + 1) % 2)

    @pl.when(step == 0)
    def _():
        fetch(0, 0, 0)

    slot = jnp.bitwise_and(step, 1)

    @pl.when(step < last)
    def _():
        wrap = i == n_i - 1
        fetch(jnp.where(wrap, s + 1, s), jnp.where(wrap, 0, i + 1), 1 - slot)

    pltpu.make_async_copy(kbuf.at[slot], kbuf.at[slot], sem.at[slot]).wait()
    pltpu.make_async_copy(vbuf.at[slot], vbuf.at[slot], sem.at[slot]).wait()

    def summaries(buf, w):
        sums = []
        for j in range(PAGES_PER_STEP):
            for b in range(BLOCKS_PER_PAGE):
                y = buf[slot, j, b * ROWS_PER_BLOCK:(b + 1) * ROWS_PER_BLOCK, :] * w
                acc = jnp.sum(y.reshape(ROWS_PER_BLOCK // SUBLANES, SUBLANES, HEAD_DIM), axis=0)
                shift = SUBLANES // 2
                while shift >= NSA_KV_HEADS:
                    acc = acc + pltpu.roll(acc, shift, 0)
                    shift //= 2
                sums.append(acc)
        tiles = []
        for t in range(len(sums) // BLOCKS_PER_TILE):
            tile = sums[t * BLOCKS_PER_TILE]
            for j in range(1, BLOCKS_PER_TILE):
                tile = jnp.where(pair == j, sums[t * BLOCKS_PER_TILE + j], tile)
            tiles.append(tile)
        return jnp.concatenate(tiles, axis=0)

    ko_ref[0] = summaries(kbuf, wk_ref[...])
    vo_ref[0] = summaries(vbuf, wv_ref[...])


def _compress_paged(pool_k, pool_v, page_table, w_cmp_k, w_cmp_v):
    n, n_pages = page_table.shape
    wk2 = jnp.repeat(w_cmp_k, NSA_KV_HEADS, axis=0)
    wv2 = jnp.repeat(w_cmp_v, NSA_KV_HEADS, axis=0)
    anywhere = pl.BlockSpec(memory_space=pl.ANY)
    wsp = pl.BlockSpec((ROWS_PER_BLOCK, HEAD_DIM), lambda s, i, pt: (0, 0))
    rows = PAGES_PER_STEP * BLOCKS_PER_PAGE * NSA_KV_HEADS
    osp = pl.BlockSpec((1, rows, HEAD_DIM), lambda s, i, pt: (s, i, 0))
    page_buf = pltpu.VMEM((2, PAGES_PER_STEP, ROWS_PER_PAGE, HEAD_DIM), F32)
    return pl.pallas_call(
        _compress_paged_kernel,
        out_shape=[jax.ShapeDtypeStruct((n, n_pages * BLOCKS_PER_PAGE * NSA_KV_HEADS, HEAD_DIM), F32)] * 2,
        grid_spec=pltpu.PrefetchScalarGridSpec(
            num_scalar_prefetch=1,
            grid=(n, n_pages // PAGES_PER_STEP),
            in_specs=[anywhere, anywhere, wsp, wsp],
            out_specs=[osp, osp],
            scratch_shapes=[page_buf, page_buf, pltpu.SemaphoreType.DMA((2,))]),
        compiler_params=pltpu.CompilerParams(dimension_semantics=("arbitrary", "arbitrary")),
        name="compress_paged",
    )(page_table, _interleaved(pool_k), _interleaved(pool_v), wk2, wv2)


SS_SEQ = 8
LOWEST = -3.0e38


def _sample_select_kernel(q_ref, kc_ref, vc_ref, ocmp_ref, sel_ref, *, pos):
    ss, nbk = q_ref.shape[0], kc_ref.shape[1]
    kv, g = NSA_KV_HEADS, NSA_GROUP
    head = lax.broadcasted_iota(jnp.int32, (NSA_HEADS, nbk), 0)
    col = lax.broadcasted_iota(jnp.int32, (NSA_HEADS, nbk), 1)
    m = (_mod(col, kv) == _div(head, g)) & ((_div(col, kv) + 1) * CMP_BLOCK - 1 <= pos)
    width = nbk + LANES
    rows = []
    for i in range(ss):
        q = (q_ref[i] * (HEAD_DIM ** -0.5)).astype(BF16)
        p = _masked_softmax(_dot_nt(q, kc_ref[i].astype(BF16)), m)
        ocmp_ref[i] = jnp.dot(p.astype(BF16), vc_ref[i].astype(BF16), preferred_element_type=F32)
        for k in range(kv):
            imp = jnp.sum(p[k * g:(k + 1) * g], axis=0, keepdims=True)
            rows.append(jnp.concatenate([imp, jnp.zeros((1, LANES), F32)], axis=1))
    nrow = ss * kv
    r_iota = lax.broadcasted_iota(jnp.int32, (nrow, width), 0)
    ccol = lax.broadcasted_iota(jnp.int32, (nrow, width), 1)
    cand = jnp.zeros((nrow, width), F32)
    for r, row in enumerate(rows):
        cand = jnp.where(r_iota == r, row, cand)
    cblk = _div(ccol, kv)
    cur = pos // CMP_BLOCK
    n_sel = -(-(pos + 1) // CMP_BLOCK)
    forced = (cblk == 0) | (cblk == cur) | (cblk == cur - 1)
    score = jnp.where(cblk > cur, NEG, jnp.where(forced, -NEG, cand))
    score = jnp.where((_mod(ccol, kv) == _mod(r_iota, kv)) & (cblk < n_sel), score, LOWEST)
    colf = ccol.astype(F32)
    lane = lax.broadcasted_iota(jnp.int32, (nrow, LANES), 1)
    sel = jnp.zeros((nrow, LANES), jnp.int32)
    for t in range(SEL_TOPK):
        mx = jnp.max(score, axis=1, keepdims=True)
        c = jnp.min(jnp.where(score == mx, colf, -LOWEST), axis=1, keepdims=True)
        picked = jnp.where(mx > 0.5 * NEG, _div(c.astype(jnp.int32), kv), -1)
        sel = jnp.where(lane == t, picked, sel)
        score = jnp.where(colf == c, LOWEST, score)
    sel_ref[...] = sel


def _sample_select(q, kc, vc, pos):
    n = q.shape[0]
    nbk = kc.shape[1]
    qsp = pl.BlockSpec((SS_SEQ, NSA_HEADS, HEAD_DIM), lambda i: (i, 0, 0))
    csp = pl.BlockSpec((SS_SEQ, nbk, HEAD_DIM), lambda i: (i, 0, 0))
    return pl.pallas_call(
        functools.partial(_sample_select_kernel, pos=pos),
        out_shape=[jax.ShapeDtypeStruct((n, NSA_HEADS, HEAD_DIM), F32),
                   jax.ShapeDtypeStruct((n * NSA_KV_HEADS, LANES), jnp.int32)],
        grid=(n // SS_SEQ,),
        in_specs=[qsp, csp, csp],
        out_specs=[qsp, pl.BlockSpec((SS_SEQ * NSA_KV_HEADS, LANES), lambda i: (i, 0))],
        compiler_params=pltpu.CompilerParams(dimension_semantics=("arbitrary",), vmem_limit_bytes=VMEM_LIMIT),
        name="sample_select",
    )(q, kc, vc)


N_SLOTS = NSA_KV_HEADS * SEL_TOPK


def _sample_attend_kernel(sel_ref, pt_ref, q_ref, knew_ref, vnew_ref, wknew_ref, wvnew_ref, wkb_ref, wvb_ref,
                          ocmp_ref, gl_ref, *rest, pos, past_blocks):
    del pt_ref
    kblk, vblk, o_ref = rest[:N_SLOTS], rest[N_SLOTS:2 * N_SLOTS], rest[2 * N_SLOTS]
    kv, g, rb = NSA_KV_HEADS, NSA_GROUP, ROWS_PER_BLOCK
    n = pl.program_id(0)
    q = q_ref[0] * (HEAD_DIM ** -0.5)
    row = lax.broadcasted_iota(jnp.int32, (rb, HEAD_DIM), 0)
    col = lax.broadcasted_iota(jnp.int32, (1, rb), 1)

    def new_block(ref):
        out = jnp.zeros((rb, HEAD_DIM), F32)
        for k in range(kv):
            out = jnp.where(row == k, ref[0, k:k + 1, :], out)
        return out

    def per_head_rows(ref):
        return jnp.concatenate([jnp.broadcast_to(ref[0, k:k + 1, :], (g, HEAD_DIM)) for k in range(kv)], axis=0)

    knew, vnew = new_block(knew_ref), new_block(vnew_ref)
    o_sel = []
    for k in range(kv):
        ks, vs, ms = [], [], []
        for j in range(SEL_TOPK):
            b = sel_ref[(n * kv + k) * SEL_TOPK + j]
            is_new = jnp.broadcast_to(b, (rb, HEAD_DIM)) >= past_blocks
            ks.append(jnp.where(is_new, knew, kblk[k * SEL_TOPK + j][0]))
            vs.append(jnp.where(is_new, vnew, vblk[k * SEL_TOPK + j][0]))
            first = jnp.where(b >= 0, b, 1 << 24) * CMP_BLOCK
            ms.append((first + _div(col, kv) <= pos) & (_mod(col, kv) == k))
        s = _dot_nt(q[k * g:(k + 1) * g].astype(BF16), jnp.concatenate(ks, axis=0).astype(BF16))
        p = _masked_softmax(s, jnp.concatenate(ms, axis=1))
        o_sel.append(jnp.dot(p.astype(BF16), jnp.concatenate(vs, axis=0).astype(BF16), preferred_element_type=F32))
    o_sel = jnp.concatenate(o_sel, axis=0)

    nwr = wkb_ref.shape[1]
    s_w = _dot_nt(q.astype(BF16), wkb_ref[0].astype(BF16))
    cw = lax.broadcasted_iota(jnp.int32, (NSA_HEADS, nwr), 1)
    hw = lax.broadcasted_iota(jnp.int32, (NSA_HEADS, nwr), 0)
    kpos = pos - nwr // kv + _div(cw, kv)
    mw = (kpos <= pos) & (pos - kpos < WINDOW) & (_mod(cw, kv) == _div(hw, g))
    s_n = jnp.sum(q * per_head_rows(wknew_ref), axis=1, keepdims=True)
    smw = jnp.where(mw, s_w, NEG)
    mx = jnp.maximum(jnp.max(smw, axis=1, keepdims=True), s_n)
    e_w = jnp.where(mw, jnp.exp(smw - mx), 0.0)
    e_n = jnp.exp(s_n - mx)
    den = jnp.sum(e_w, axis=1, keepdims=True) + e_n
    o_win = (jnp.dot(e_w.astype(BF16), wvb_ref[0].astype(BF16), preferred_element_type=F32)
             + e_n * per_head_rows(wvnew_ref)) / den

    gate = 1.0 / (1.0 + jnp.exp(-gl_ref[0]))
    o_ref[0] = gate[:, 0:1] * ocmp_ref[0] + gate[:, 1:2] * o_sel + gate[:, 2:3] * o_win


def _sample_attend(sel, page_table, q, sk, sv, wk, wv, win_k, win_v, pool_k, pool_v, o_cmp, gl, pos):
    n, n_pages = page_table.shape
    past_blocks = n_pages * BLOCKS_PER_PAGE

    def slot_spec(k, j):
        def imap(s, sel_r, pt_r):
            b = jnp.maximum(sel_r[(s * NSA_KV_HEADS + k) * SEL_TOPK + j], 0)
            page = jnp.minimum(_div(b, BLOCKS_PER_PAGE), n_pages - 1)
            return (pt_r[s * n_pages + page], _mod(b, BLOCKS_PER_PAGE), 0)
        return pl.BlockSpec((1, ROWS_PER_BLOCK, HEAD_DIM), imap)

    slots = [slot_spec(k, j) for k in range(NSA_KV_HEADS) for j in range(SEL_TOPK)]
    per_seq = lambda a: pl.BlockSpec((1,) + a.shape[1:], lambda s, sel_r, pt_r: (s, 0, 0))
    dense = [q, sk, sv, wk, wv, win_k, win_v, o_cmp, gl]
    return pl.pallas_call(
        functools.partial(_sample_attend_kernel, pos=pos, past_blocks=past_blocks),
        out_shape=jax.ShapeDtypeStruct((n, NSA_HEADS, HEAD_DIM), F32),
        grid_spec=pltpu.PrefetchScalarGridSpec(
            num_scalar_prefetch=2,
            grid=(n,),
            in_specs=[per_seq(a) for a in dense] + slots + slots,
            out_specs=pl.BlockSpec((1, NSA_HEADS, HEAD_DIM), lambda s, sel_r, pt_r: (s, 0, 0))),
        compiler_params=pltpu.CompilerParams(dimension_semantics=("arbitrary",), vmem_limit_bytes=VMEM_LIMIT),
        name="sample_attend",
    )(sel, page_table.reshape(-1), *dense, *([pool_k] * N_SLOTS), *([pool_v] * N_SLOTS))


def _retention_tables(seq):
    c = RET_CHUNK
    log_g = jnp.log1p(-jnp.exp2(-5.0 - jnp.arange(RET_HEADS, dtype=F32)))
    i = jnp.arange(c, dtype=F32)
    rel = i[:, None] - i[None, :]
    dmask = jnp.where(rel[None] >= 0, jnp.exp(jnp.maximum(rel[None], 0.0) * log_g[:, None, None]), 0.0)
    q_dec = jnp.exp((i + 1.0)[None] * log_g[:, None])[..., None]
    k_dec = jnp.exp((c - 1.0 - i)[None] * log_g[:, None])[..., None]
    c_dec = jnp.exp(c * log_g)[:, None, None]
    bc = lambda t: jnp.broadcast_to(t, (RET_HEADS, c, HEAD_DIM))
    inv = RET_ROPE_THETA ** (-jnp.arange(0, HEAD_DIM, 2, dtype=F32) / HEAD_DIM)
    ang = jnp.arange(seq)[:, None].astype(F32) * inv
    cos, sin = jnp.cos(ang), jnp.sin(ang)
    return (dmask, bc(q_dec), bc(k_dec), jnp.broadcast_to(c_dec, (RET_HEADS, 1, HEAD_DIM)),
            jnp.concatenate([cos, cos], -1), jnp.concatenate([-sin, sin], -1))


def _retention_kernel(q_ref, k_ref, v_ref, g_ref, cos_ref, sin_ref, dmask_ref, qdec_ref, kdec_ref, cdec_ref, gn_ref,
                      o_ref, st_ref, s_scr):
    c = pl.program_id(1)

    @pl.when(c == 0)
    def _():
        s_scr[...] = jnp.zeros(s_scr.shape, F32)

    cosf, sins = cos_ref[...], sin_ref[...]
    half = HEAD_DIM // 2
    for h in range(RET_HEADS):
        sl = slice(h * HEAD_DIM, (h + 1) * HEAD_DIM)
        qh, kh = q_ref[:, sl], k_ref[:, sl]
        qr = qh * cosf + pltpu.roll(qh, half, 1) * sins
        kr = (kh * cosf + pltpu.roll(kh, half, 1) * sins) * (HEAD_DIM ** -0.5)
        vb = v_ref[:, sl].astype(BF16)
        att = _dot_nt(qr.astype(BF16), kr.astype(BF16)) * dmask_ref[h]
        s_prev = s_scr[h]
        o = (jnp.dot(att.astype(BF16), vb, preferred_element_type=F32)
             + jnp.dot((qr * qdec_ref[h]).astype(BF16), s_prev.astype(BF16), preferred_element_type=F32))
        s_scr[h] = cdec_ref[h] * s_prev + lax.dot_general(
            (kr * kdec_ref[h]).astype(BF16), vb, (((0,), (0,)), ((), ())), preferred_element_type=F32)
        mu = jnp.mean(o, axis=-1, keepdims=True)
        var = jnp.mean(jnp.square(o - mu), axis=-1, keepdims=True)
        on = (o - mu) * lax.rsqrt(var + GN_EPS) * gn_ref[:, sl]
        gg = g_ref[:, sl]
        o_ref[:, sl] = gg * (1.0 / (1.0 + jnp.exp(-gg))) * on

    @pl.when(c == pl.num_programs(1) - 1)
    def _():
        st_ref[0] = s_scr[...]


def _retention_prompt(main, gn_g, batch, seq):
    nc = seq // RET_CHUNK
    dmask, q_dec, k_dec, c_dec, cosf, sins = _retention_tables(seq)
    col = lambda j: pl.BlockSpec((RET_CHUNK, RET_W), lambda b, c: (b * nc + c, j))
    pos_tab = pl.BlockSpec((RET_CHUNK, HEAD_DIM), lambda b, c: (c, 0))
    full = lambda a: pl.BlockSpec(a.shape, lambda b, c: (0,) * a.ndim)
    gn = gn_g.reshape(1, RET_W)
    return pl.pallas_call(
        _retention_kernel,
        out_shape=[jax.ShapeDtypeStruct((batch * seq, RET_W), F32),
                   jax.ShapeDtypeStruct((batch, RET_HEADS, HEAD_DIM, HEAD_DIM), F32)],
        grid=(batch, nc),
        in_specs=[col(0), col(1), col(2), col(3), pos_tab, pos_tab,
                  full(dmask), full(q_dec), full(k_dec), full(c_dec), full(gn)],
        out_specs=[pl.BlockSpec((RET_CHUNK, RET_W), lambda b, c: (b * nc + c, 0)),
                   pl.BlockSpec((1, RET_HEADS, HEAD_DIM, HEAD_DIM), lambda b, c: (b, 0, 0, 0))],
        scratch_shapes=[pltpu.VMEM((RET_HEADS, HEAD_DIM, HEAD_DIM), F32)],
        compiler_params=pltpu.CompilerParams(dimension_semantics=("arbitrary", "arbitrary"),
                                             vmem_limit_bytes=VMEM_LIMIT),
        name="retention_prompt",
    )(main, main, main, main, cosf, sins, dmask, q_dec, k_dec, c_dec, gn)


NSA_TQ = 128
NSA_KEY_CHUNK = 512
NSA_GROUP = NSA_HEADS // NSA_KV_HEADS


def _dot_nt(a, b):
    return lax.dot_general(a, b, (((1,), (1,)), ((), ())), preferred_element_type=F32)


def _masked_softmax(s, m):
    sm = jnp.where(m, s, NEG)
    e = jnp.exp(sm - jnp.max(sm, axis=-1, keepdims=True))
    return jnp.where(m, e / jnp.sum(e, axis=-1, keepdims=True), 0.0)


def _select_mask_t(imp_t, pos_t):
    nb = imp_t.shape[0]
    blk = lax.broadcasted_iota(jnp.int32, imp_t.shape, 0)
    cur = jnp.right_shift(pos_t, int(math.log2(CMP_BLOCK)))
    forced = (blk == 0) | (blk == cur) | (blk == cur - 1)
    score = jnp.where(blk > cur, NEG, jnp.where(forced, -NEG, imp_t))
    rank = jnp.zeros(score.shape, jnp.int32)
    for i in range(nb):
        si = score[i:i + 1, :]
        ahead = (si > score) | ((si == score) & (blk > i))
        rank = rank + ahead.astype(jnp.int32)
    return (rank < SEL_TOPK) & (score > 0.5 * NEG)


def _nsa_prompt_kernel(q_ref, kc_ref, vc_ref, sk_ref, sv_ref, wk_ref, wv_ref, gl_ref, o_ref, skt, wkt):
    tq, g, kc_n = NSA_TQ, NSA_GROUP, NSA_KEY_CHUNK
    qi = pl.program_id(2)
    nb = kc_ref.shape[1]

    @pl.when(qi == 0)
    def _():
        blk_row = lax.broadcasted_iota(jnp.int32, (nb, kc_n), 0)
        key_col = lax.broadcasted_iota(jnp.int32, (nb, kc_n), 1)
        for c in range(skt.shape[0]):
            member = jnp.right_shift(key_col + c * kc_n, int(math.log2(CMP_BLOCK))) == blk_row
            skt[c] = jnp.concatenate([sk_ref[c * kc_n:(c + 1) * kc_n, :].T.astype(BF16),
                                      jnp.where(member, 1.0, 0.0).astype(BF16)], axis=0)
        for w in range(wkt.shape[0]):
            wkt[w] = wk_ref[w * tq:(w + 1) * tq, :].T.astype(BF16)

    q4 = q_ref[...] * (HEAD_DIM ** -0.5)
    qs = jnp.concatenate([q4[:, i * HEAD_DIM:(i + 1) * HEAD_DIM] for i in range(g)], axis=0).astype(BF16)
    pos = qi * tq + lax.broadcasted_iota(jnp.int32, (tq, 1), 0)
    pos4 = jnp.concatenate([pos] * g, axis=0)

    assert tq == LANES
    s_c = _dot_nt(kc_ref[0].astype(BF16), qs)
    blk_t = lax.broadcasted_iota(jnp.int32, (nb, g * tq), 0)
    pos_t = qi * tq + lax.broadcasted_iota(jnp.int32, (1, tq), 1)
    m_c = (blk_t + 1) * CMP_BLOCK - 1 <= jnp.concatenate([pos_t] * g, axis=1)
    sm_c = jnp.where(m_c, s_c, NEG)
    e_c = jnp.exp(sm_c - jnp.max(sm_c, axis=0, keepdims=True))
    p_c = jnp.where(m_c, e_c / jnp.sum(e_c, axis=0, keepdims=True), 0.0)
    o_cmp = lax.dot_general(p_c.astype(BF16), vc_ref[0].astype(BF16), (((0,), (0,)), ((), ())),
                            preferred_element_type=F32)
    imp_t = p_c[:, 0:tq]
    for i in range(1, g):
        imp_t = imp_t + p_c[:, i * tq:(i + 1) * tq]

    seln_t = jnp.where(_select_mask_t(imp_t, pos_t), 0.0, NEG)
    seln = jnp.concatenate([seln_t, jnp.zeros((tq - nb, tq), F32)], axis=0).T[:, :nb].astype(BF16)
    q_aug = jnp.concatenate([qs, jnp.concatenate([seln] * g, axis=0)], axis=1)
    key_row = lax.broadcasted_iota(jnp.int32, (1, kc_n), 1)

    def scores(c):
        return jnp.dot(q_aug, skt[c], preferred_element_type=F32)

    def values(c):
        return sv_ref[pl.ds(pl.multiple_of(c * kc_n, kc_n), kc_n), :].astype(BF16)

    assert kc_n % tq == 0
    c_diag = _div(qi, kc_n // tq)
    s_d = jnp.where(key_row + c_diag * kc_n <= pos4, scores(c_diag), NEG)
    m_d = jnp.max(s_d, axis=-1, keepdims=True)
    p_d = jnp.exp(s_d - m_d)
    first = (m_d, jnp.sum(p_d, axis=-1, keepdims=True),
             jnp.dot(p_d.astype(BF16), values(c_diag), preferred_element_type=F32))

    def chunk(c, carry):
        m_i, l_i, acc = carry
        s = scores(c)
        m_new = jnp.maximum(m_i, jnp.max(s, axis=-1, keepdims=True))
        alpha = jnp.exp(m_i - m_new)
        p = jnp.exp(s - m_new)
        l_new = alpha * l_i + jnp.sum(p, axis=-1, keepdims=True)
        return m_new, l_new, alpha * acc + jnp.dot(p.astype(BF16), values(c), preferred_element_type=F32)

    _, l_f, acc_f = lax.fori_loop(0, c_diag, chunk, first)
    o_sel = acc_f / l_f

    nwk = WINDOW + tq
    kstart = pl.multiple_of(jnp.maximum(qi * tq - WINDOW, 0), tq)
    kpos = kstart + lax.broadcasted_iota(jnp.int32, (1, nwk), 1)
    band = jnp.where((kpos <= pos) & (pos - kpos < WINDOW), 0.0, NEG)
    w0 = jnp.maximum(qi - WINDOW // tq, 0)
    kwt = jnp.concatenate([wkt[w0 + m] for m in range(nwk // tq)], axis=1)
    s_w = jnp.dot(qs, kwt, preferred_element_type=F32) + jnp.concatenate([band] * g, axis=0)
    e_w = jnp.exp(s_w - jnp.max(s_w, axis=-1, keepdims=True))
    o_win = (jnp.dot(e_w.astype(BF16), wv_ref[pl.ds(kstart, nwk), :].astype(BF16), preferred_element_type=F32)
             / jnp.sum(e_w, axis=-1, keepdims=True))

    gate = 1.0 / (1.0 + jnp.exp(-gl_ref[...]))
    for i in range(g):
        rows = slice(i * tq, (i + 1) * tq)
        o_ref[:, i * HEAD_DIM:(i + 1) * HEAD_DIM] = (gate[:, 3 * i:3 * i + 1] * o_cmp[rows]
                                                     + gate[:, 3 * i + 1:3 * i + 2] * o_sel[rows]
                                                     + gate[:, 3 * i + 2:3 * i + 3] * o_win[rows])


def _nsa_prompt(main, kc, vc, batch, seq):
    nq = seq // NSA_TQ
    gw = NSA_GROUP * HEAD_DIM
    qcol, glcol = _split_col0(4) // gw, GL_COL0 // LANES
    seq_spec = lambda split: pl.BlockSpec((seq, HEAD_DIM),
                                          lambda b, k, i: (b, _split_col0(split) // HEAD_DIM + k))
    cmp_spec = pl.BlockSpec((1, seq // CMP_BLOCK, HEAD_DIM), lambda b, k, i: (b, 0, k))
    return pl.pallas_call(
        _nsa_prompt_kernel,
        out_shape=jax.ShapeDtypeStruct((batch * seq, NSA_W), F32),
        grid=(batch, NSA_KV_HEADS, nq),
        in_specs=[pl.BlockSpec((NSA_TQ, gw), lambda b, k, i: (b * nq + i, qcol + k)), cmp_spec, cmp_spec,
                  seq_spec(7), seq_spec(8), seq_spec(9), seq_spec(10),
                  pl.BlockSpec((NSA_TQ, LANES), lambda b, k, i: (b * nq + i, glcol + k))],
        out_specs=pl.BlockSpec((NSA_TQ, gw), lambda b, k, i: (b * nq + i, k)),
        scratch_shapes=[pltpu.VMEM((seq // NSA_KEY_CHUNK, HEAD_DIM + seq // CMP_BLOCK, NSA_KEY_CHUNK), BF16),
                        pltpu.VMEM((seq // NSA_TQ, HEAD_DIM, NSA_TQ), BF16)],
        compiler_params=pltpu.CompilerParams(dimension_semantics=("arbitrary",) * 3,
                                             vmem_limit_bytes=VMEM_LIMIT),
        name="nsa_prompt",
    )(main, kc, vc, main, main, main, main, main)


WO_TM = 512


def _layer_norm_rows(y, g, b):
    mu = jnp.mean(y, axis=-1, keepdims=True)
    var = jnp.mean(jnp.square(y - mu), axis=-1, keepdims=True)
    return (y - mu) * lax.rsqrt(var + LN_EPS) * g + b


EXPERT_LANE0 = N_GROUPS
R_EID, R_RANK, R_GATE = 0, EXPERT_TOPK, 2 * EXPERT_TOPK


def _route(h, wr_ref, br_ref, carry):
    tm = h.shape[0]
    logit = jnp.dot(h.astype(BF16), wr_ref[...], preferred_element_type=F32) + br_ref[...]
    lane = lax.broadcasted_iota(jnp.int32, (tm, LANES), 1)
    lanef = lane.astype(F32)
    first_lane = lambda hit: jnp.min(jnp.where(hit, lanef, float(LANES)), axis=1, keepdims=True)
    is_g = lane < N_GROUPS
    gl = jnp.where(is_g, logit, LOWEST)
    gmx = jnp.max(gl, axis=1, keepdims=True)
    grp = first_lane(gl == gmx)
    p_grp = 1.0 / jnp.sum(jnp.where(is_g, jnp.exp(gl - gmx), 0.0), axis=1, keepdims=True)
    lane_grp = jnp.right_shift(lane - EXPERT_LANE0, int(math.log2(EXPERTS_PER_GROUP)))
    in_grp = lane_grp.astype(F32) == grp
    el = jnp.where(in_grp, logit, LOWEST)
    ee = jnp.where(in_grp, jnp.exp(el - jnp.max(el, axis=1, keepdims=True)), 0.0)
    pe = jnp.where(in_grp, ee / jnp.sum(ee, axis=1, keepdims=True), -1.0)
    p1 = jnp.max(pe, axis=1, keepdims=True)
    l1 = first_lane(pe == p1)
    pe2 = jnp.where(lanef == l1, -1.0, pe)
    p2 = jnp.max(pe2, axis=1, keepdims=True)
    l2 = first_lane(pe2 == p2)
    den = p1 + p2
    o1, o2 = lanef == l1, lanef == l2
    onehot = jnp.where(o1 | o2, 1.0, 0.0)
    r = lax.broadcasted_iota(jnp.int32, (tm, tm), 0)
    c = lax.broadcasted_iota(jnp.int32, (tm, tm), 1)
    earlier = jnp.where(c < r, 1.0, 0.0).astype(BF16)
    prefix = jnp.dot(earlier, onehot.astype(BF16), preferred_element_type=F32) + carry[0:1, :]
    rank1 = jnp.sum(jnp.where(o1, prefix, 0.0), axis=1, keepdims=True)
    rank2 = jnp.sum(jnp.where(o2, prefix, 0.0), axis=1, keepdims=True)
    carry[0:1, :] = carry[0:1, :] + jnp.sum(onehot, axis=0, keepdims=True)
    fields = [l1 - EXPERT_LANE0, l2 - EXPERT_LANE0, rank1, rank2, p_grp * p1 / den, p_grp * p2 / den]
    rec = jnp.zeros((tm, LANES), F32)
    for j, f in enumerate(fields):
        rec = jnp.where(lane == j, f, rec)
    return rec


def _pack_bf16_pairs(x):
    w = x.shape[1] // 2
    return pltpu.pack_elementwise([x[:, :w], x[:, w:]], packed_dtype=jnp.bfloat16)


def _rows_to_tiles(ref, index, p):
    r = p.shape[0]
    for s in range(SUBLANES):
        ref[index + (pl.ds(s, r, stride=SUBLANES), slice(None))] = p[:, s * LANES:(s + 1) * LANES]


def _tiles_to_rows(ref, index, r):
    return jnp.concatenate([ref[index + (pl.ds(s, r, stride=SUBLANES), slice(None))] for s in range(SUBLANES)],
                           axis=1)


def _unpack_bf16_pairs(p):
    halves = [pltpu.unpack_elementwise(p, index=i, packed_dtype=jnp.bfloat16, unpacked_dtype=F32) for i in (0, 1)]
    return jnp.concatenate(halves, axis=1)


def _wo_ln_route_kernel(fr_ref, fn_ref, x_ref, w_ref, g_ref, b_ref, wr_ref, br_ref, base_ref,
                        h_ref, hp_ref, route_ref, cnt_ref, carry):
    i = pl.program_id(0)

    @pl.when(i == 0)
    def _():
        carry[...] = base_ref[...]

    y = (DEEPNORM_ALPHA * x_ref[...]
         + jnp.dot(fr_ref[...].astype(BF16), w_ref[0:RET_W, :], preferred_element_type=F32)
         + jnp.dot(fn_ref[...].astype(BF16), w_ref[RET_W:MIX_W, :], preferred_element_type=F32))
    h = _layer_norm_rows(y, g_ref[...], b_ref[...])
    h_ref[...] = h
    _rows_to_tiles(hp_ref, (), _pack_bf16_pairs(h))
    route_ref[...] = _route(h, wr_ref, br_ref, carry)

    @pl.when(i == pl.num_programs(0) - 1)
    def _():
        cnt_ref[...] = carry[...]


def _wo_ln_route(f_ret, f_nsa, x, w_o_bf16, ln_g, ln_b, w_route, b_route, base_counts):
    t, d = x.shape
    tm = min(WO_TM, t)
    assert t % tm == 0
    row = lambda w: pl.BlockSpec((tm, w), lambda i: (i, 0))
    full = lambda a: pl.BlockSpec(a.shape, lambda i: (0,) * a.ndim)
    lg, lb = ln_g.reshape(1, d), ln_b.reshape(1, d)
    return pl.pallas_call(
        _wo_ln_route_kernel,
        out_shape=[jax.ShapeDtypeStruct((t, d), F32), jax.ShapeDtypeStruct((t * SUBLANES, LANES), jnp.uint32),
                   jax.ShapeDtypeStruct((t, LANES), F32), jax.ShapeDtypeStruct((SUBLANES, LANES), F32)],
        grid=(t // tm,),
        in_specs=[row(RET_W), row(NSA_W), row(d), full(w_o_bf16), full(lg), full(lb),
                  full(w_route), full(b_route), full(base_counts)],
        out_specs=[row(d), pl.BlockSpec((tm * SUBLANES, LANES), lambda i: (i, 0)), row(LANES),
                   pl.BlockSpec((SUBLANES, LANES), lambda i: (0, 0))],
        scratch_shapes=[pltpu.VMEM((SUBLANES, LANES), F32)],
        compiler_params=pltpu.CompilerParams(dimension_semantics=("arbitrary",), vmem_limit_bytes=VMEM_LIMIT),
        name="wo_ln1_route",
    )(f_ret, f_nsa, x, w_o_bf16, lg, lb, w_route, b_route, base_counts)


MOE_BM = 256
MOE_SUB = 64
TABLE_UNROLL = 8
META_N_USED, META_FILL_TRIPS, META_PAD_LO, META_PAD_HI = 0, 1, 2, 2 + N_EXPERTS


def _expert_kernel(blk_e_ref, meta_ref, slot_ref, h_ref, wg_ref, wu_ref, wd_ref, y_ref,
                   src_tok, dst_row, xbuf, obuf, gsem, ssem, *, plane):
    del blk_e_ref
    n_sub, sub = xbuf.shape[1], xbuf.shape[2] // SUBLANES
    bm = n_sub * sub
    i = pl.program_id(0)
    n_used = meta_ref[META_N_USED]
    n_asg = slot_ref.shape[0]
    dump0 = EXPERT_TOPK * plane
    assert bm & (bm - 1) == 0

    tile = lambda row: pl.ds(pl.multiple_of(row * SUBLANES, SUBLANES), SUBLANES)

    def gather(blk, buf_slot):
        def sub_block(j, carry):
            base = blk * bm + j * sub
            for u in range(sub):
                pltpu.make_async_copy(h_ref.at[tile(src_tok[base + u]), :],
                                      xbuf.at[buf_slot, j, tile(u), :], gsem.at[buf_slot]).start(priority=u % 2)
            return carry
        lax.fori_loop(0, n_sub, sub_block, 0)

    def scatter(blk, buf_slot):
        def sub_block(j, carry):
            base = blk * bm + j * sub
            for u in range(sub):
                pltpu.make_async_copy(obuf.at[buf_slot, j, tile(u), :],
                                      y_ref.at[tile(dst_row[base + u]), :], ssem.at[buf_slot]).start(priority=u % 2)
            return carry
        lax.fori_loop(0, n_sub, sub_block, 0)

    def wait_block(buf, sem, buf_slot):
        pltpu.make_async_copy(buf.at[buf_slot], buf.at[buf_slot], sem.at[buf_slot]).wait()

    @pl.when(i == 0)
    def _():
        def clear_expert(e, carry):
            def clear(r, c):
                src_tok[r] = 0
                dst_row[r] = dump0 + jnp.bitwise_and(r, 2 * bm - 1)
                return c
            lax.fori_loop(meta_ref[META_PAD_LO + e], meta_ref[META_PAD_HI + e], clear, 0)
            return carry
        lax.fori_loop(0, N_EXPERTS, clear_expert, 0)

        def fill(t, carry):
            for u in range(TABLE_UNROLL):
                a = t * TABLE_UNROLL + u
                tok = jnp.right_shift(a, 1)
                src_tok[slot_ref[a]] = tok
                dst_row[slot_ref[a]] = jnp.bitwise_and(a, 1) * plane + tok
            return carry
        lax.fori_loop(0, meta_ref[META_FILL_TRIPS], fill, 0)
        gather(0, 0)
        n_tok = h_ref.shape[0] // SUBLANES
        tail = plane - n_tok
        assert tail >= 0
        obuf[1] = jnp.zeros(obuf.shape[1:], obuf.dtype)
        spans = [(dump0, 2 * bm)] + ([(k * plane + n_tok, tail) for k in range(EXPERT_TOPK)] if tail else [])
        copies = []
        for first, count in spans:
            for j in range(-(-count // sub)):
                rows = min(sub, count - j * sub)
                copies.append(pltpu.make_async_copy(
                    obuf.at[1, j % n_sub, pl.ds(0, rows * SUBLANES), :],
                    y_ref.at[pl.ds((first + j * sub) * SUBLANES, rows * SUBLANES), :], ssem.at[1]))
        for cp in copies:
            cp.start()
        for cp in copies:
            cp.wait()

    slot = jnp.bitwise_and(i, 1)

    @pl.when(i + 1 < n_used)
    def _():
        gather(i + 1, 1 - slot)

    @pl.when(i < n_used)
    def _():
        wait_block(xbuf, gsem, slot)

        @pl.when(i >= 2)
        def _():
            wait_block(obuf, ssem, slot)

        packed = jnp.concatenate([_tiles_to_rows(xbuf, (slot, j), sub) for j in range(n_sub)], axis=0)
        xb = _unpack_bf16_pairs(packed).astype(BF16)
        hg = jnp.dot(xb, wg_ref[0].astype(BF16), preferred_element_type=F32)
        hu = jnp.dot(xb, wu_ref[0].astype(BF16), preferred_element_type=F32)
        hb = hg * (1.0 / (1.0 + jnp.exp(-hg))) * hu
        yb = jnp.dot(hb.astype(BF16), wd_ref[0].astype(BF16), preferred_element_type=F32)
        yp = _pack_bf16_pairs(yb)
        for j in range(n_sub):
            _rows_to_tiles(obuf, (slot, j), yp[j * sub:(j + 1) * sub])
        scatter(i, slot)

    @pl.when(i == pl.num_programs(0) - 1)
    def _():
        @pl.when(n_used >= 2)
        def _():
            wait_block(obuf, ssem, jnp.bitwise_and(n_used, 1))
        wait_block(obuf, ssem, jnp.bitwise_and(n_used - 1, 1))


def _expert_ffn(h, slot, blk_e, meta, w_gate, w_up, w_down, plane):
    t = h.shape[0] // SUBLANES
    dp = SUBLANES * LANES
    d = 2 * dp
    assert w_gate.shape[1] == d
    n_asg = slot.shape[0]
    assert n_asg == t * EXPERT_TOPK and EXPERT_TOPK == 2
    n_blk = -(-(n_asg + N_EXPERTS * (MOE_BM - 1)) // MOE_BM)
    de = w_gate.shape[2]
    assert n_asg % TABLE_UNROLL == 0 and meta.shape == (META_PAD_HI + N_EXPERTS,)
    wspec = lambda shape: pl.BlockSpec((1,) + shape, lambda i, be, nu, sl: (be[i], 0, 0))
    return pl.pallas_call(
        functools.partial(_expert_kernel, plane=plane),
        out_shape=jax.ShapeDtypeStruct(((EXPERT_TOPK * plane + 2 * MOE_BM) * SUBLANES, LANES), jnp.uint32),
        grid_spec=pltpu.PrefetchScalarGridSpec(
            num_scalar_prefetch=3,
            grid=(n_blk,),
            in_specs=[pl.BlockSpec(memory_space=pl.ANY), wspec((d, de)), wspec((d, de)), wspec((de, d))],
            out_specs=pl.BlockSpec(memory_space=pl.ANY),
            scratch_shapes=[pltpu.SMEM((n_blk * MOE_BM,), jnp.int32), pltpu.SMEM((n_blk * MOE_BM,), jnp.int32),
                            pltpu.VMEM((2, MOE_BM // MOE_SUB, MOE_SUB * SUBLANES, LANES), jnp.uint32),
                            pltpu.VMEM((2, MOE_BM // MOE_SUB, MOE_SUB * SUBLANES, LANES), jnp.uint32),
                            pltpu.SemaphoreType.DMA((2,)), pltpu.SemaphoreType.DMA((2,))]),
        compiler_params=pltpu.CompilerParams(dimension_semantics=("arbitrary",), vmem_limit_bytes=VMEM_LIMIT),
        name="expert_ffn",
    )(blk_e, meta, slot, h, w_gate, w_up, w_down)


def _moe_ln_kernel(h_ref, y0_ref, y1_ref, route_ref, g_ref, b_ref, o_ref):
    rec = route_ref[...]
    tm = h_ref.shape[0]
    y = (DEEPNORM_ALPHA * h_ref[...]
         + rec[:, R_GATE:R_GATE + 1] * _unpack_bf16_pairs(_tiles_to_rows(y0_ref, (), tm))
         + rec[:, R_GATE + 1:R_GATE + 2] * _unpack_bf16_pairs(_tiles_to_rows(y1_ref, (), tm)))
    o_ref[...] = _layer_norm_rows(y, g_ref[...], b_ref[...])


def _moe_ln(h, y, route, ln_g, ln_b, row0, plane):
    n_rows, d = h.shape
    tm = min(WO_TM, n_rows)
    assert n_rows % tm == 0 and row0 % tm == 0 and plane % tm == 0 and EXPERT_TOPK == 2
    off = row0 // tm
    row = lambda w, o: pl.BlockSpec((tm, w), lambda i: (i + o, 0))
    tiles = lambda o: pl.BlockSpec((tm * SUBLANES, LANES), lambda i: (i + o, 0))
    vec = pl.BlockSpec((1, d), lambda i: (0, 0))
    return pl.pallas_call(
        _moe_ln_kernel,
        out_shape=jax.ShapeDtypeStruct((n_rows, d), F32),
        grid=(n_rows // tm,),
        in_specs=[row(d, 0), tiles(off), tiles(off + plane // tm), row(LANES, off), vec, vec],
        out_specs=row(d, 0),
        compiler_params=pltpu.CompilerParams(dimension_semantics=("arbitrary",), vmem_limit_bytes=VMEM_LIMIT),
        name="moe_ln2",
    )(h, y, y, route, ln_g.reshape(1, d), ln_b.reshape(1, d))


def _layer_norm(x, g, b):
    xf = x.astype(F32)
    mu = xf.mean(-1, keepdims=True)
    var = jnp.square(xf - mu).mean(-1, keepdims=True)
    return ((xf - mu) * lax.rsqrt(var + LN_EPS) * g + b).astype(x.dtype)


def _rope(x, pos, rot_dim, theta):
    half = rot_dim // 2
    inv = theta ** (-jnp.arange(0, rot_dim, 2, dtype=F32) / rot_dim)
    ang = pos[..., None].astype(F32) * inv
    cos = jnp.cos(ang)[:, :, None, :]
    sin = jnp.sin(ang)[:, :, None, :]
    xr = x[..., :rot_dim].astype(F32)
    x1, x2 = xr[..., :half], xr[..., half:]
    rot = jnp.concatenate([x1 * cos - x2 * sin, x2 * cos + x1 * sin], -1).astype(x.dtype)
    return jnp.concatenate([rot, x[..., rot_dim:]], -1)


def _heads(t, n):
    return t.reshape(t.shape[0], t.shape[1], n, HEAD_DIM)


def _chunk_retention(q, k, v, s0):
    n, l, h, d = q.shape
    c = RET_CHUNK if l % RET_CHUNK == 0 else l
    nc = l // c
    log_g = jnp.log1p(-jnp.exp2(-5.0 - jnp.arange(h, dtype=F32)))
    i = jnp.arange(c, dtype=F32)
    rel = i[:, None] - i[None, :]
    dmask = jnp.where(rel[None] >= 0, jnp.exp(jnp.maximum(rel[None], 0.0) * log_g[:, None, None]), 0.0)
    q_dec = jnp.exp((i + 1.0)[None] * log_g[:, None])[..., None]
    k_dec = jnp.exp((c - 1.0 - i)[None] * log_g[:, None])[..., None]
    c_dec = jnp.exp(c * log_g)[:, None, None]

    def to_chunks(t):
        return t.astype(F32).reshape(n, nc, c, h, d).transpose(1, 0, 3, 2, 4)

    def step(s, qkv):
        qc, kc, vc = qkv
        att = jnp.einsum('bhid,bhjd->bhij', qc, kc) * dmask
        o = jnp.einsum('bhij,bhjd->bhid', att, vc) + jnp.einsum('bhid,bhde->bhie', qc * q_dec, s)
        s = c_dec * s + jnp.einsum('bhjd,bhje->bhde', kc * k_dec, vc)
        return s, o

    s, o = lax.scan(step, s0.astype(F32), (to_chunks(q), to_chunks(k), to_chunks(v)))
    return o.transpose(1, 0, 3, 2, 4).reshape(n, l, h, d), s


def _retention_group(rq, rk, rv, rg, pos, s0, gn_g):
    q = _rope(_heads(rq, RET_HEADS), pos, HEAD_DIM, RET_ROPE_THETA)
    k = _rope(_heads(rk, RET_HEADS), pos, HEAD_DIM, RET_ROPE_THETA) * (HEAD_DIM ** -0.5)
    v = _heads(rv, RET_HEADS)
    o, s = _chunk_retention(q, k, v, s0)
    mu = o.mean(-1, keepdims=True)
    var = jnp.square(o - mu).mean(-1, keepdims=True)
    on = (o - mu) * lax.rsqrt(var + GN_EPS) * gn_g.reshape(RET_HEADS, HEAD_DIM).astype(F32)
    out = jax.nn.silu(rg.astype(F32)) * on.reshape(rg.shape)
    return out.astype(rq.dtype), s


def _gqa_attend(q, k, v, mask):
    n, lq, h, d = q.shape
    kv = k.shape[2]
    qg = q.reshape(n, lq, kv, h // kv, d)
    s = jnp.einsum('nqkgd,nskd->nkgqs', qg, k).astype(F32) * (d ** -0.5)
    m = mask[:, None, None]
    p = jax.nn.softmax(jnp.where(m, s, NEG), axis=-1) * m
    o = jnp.einsum('nkgqs,nskd->nqkgd', p.astype(v.dtype), v)
    return o.reshape(n, lq, h, d), p


def _nsa_heads(nq, ck, sk, wk, cv, sv, wv, pos):
    rp = lambda t, nh: _rope(_heads(t, nh), pos, ROT_DIM, ROPE_THETA)
    return (rp(nq, NSA_HEADS), rp(ck, NSA_KV_HEADS), rp(sk, NSA_KV_HEADS), rp(wk, NSA_KV_HEADS),
            _heads(cv, NSA_KV_HEADS), _heads(sv, NSA_KV_HEADS), _heads(wv, NSA_KV_HEADS))


def _compress(rows, w):
    n, t, kv, d = rows.shape
    return jnp.einsum('nbjkd,jd->nbkd', rows.reshape(n, t // CMP_BLOCK, CMP_BLOCK, kv, d), w)


def _cmp_branch(q, pos, kc, vc):
    nb = kc.shape[1]
    blk_end = (jnp.arange(nb) + 1) * CMP_BLOCK - 1
    mask = blk_end[None, None, :] <= pos[:, :, None]
    o, p = _gqa_attend(q, kc, vc, mask)
    imp = p.sum(axis=2).transpose(0, 2, 1, 3)
    return o, imp


def _select_blocks(imp, pos, n_sel):
    nb = imp.shape[-1]
    imp = jnp.pad(imp, ((0, 0), (0, 0), (0, 0), (0, n_sel - nb)))
    blk = jnp.arange(n_sel)
    cur = (pos // CMP_BLOCK)[:, :, None, None]
    forced = (blk == 0) | (blk == cur) | (blk == cur - 1)
    score = jnp.where(blk > cur, NEG, jnp.where(forced, -NEG, imp))
    top, idx = lax.top_k(score, min(SEL_TOPK, n_sel))
    return idx, top > 0.5 * NEG


def _sel_attend(q, pos, ks, vs, idx, valid):
    n, lq, kv, kk, cb, d = ks.shape
    h = q.shape[2]
    kpos = idx[..., None] * CMP_BLOCK + jnp.arange(CMP_BLOCK)
    m = ((kpos <= pos[:, :, None, None, None]) & valid[..., None]).reshape(n, lq, kv, 1, kk * cb)
    qg = q.reshape(n, lq, kv, h // kv, d)
    kf = ks.reshape(n, lq, kv, kk * cb, d)
    vf = vs.reshape(n, lq, kv, kk * cb, d)
    s = jnp.einsum('nqkgd,nqkjd->nqkgj', qg, kf).astype(F32) * (d ** -0.5)
    p = jax.nn.softmax(jnp.where(m, s, NEG), axis=-1) * m
    o = jnp.einsum('nqkgj,nqkjd->nqkgd', p.astype(vf.dtype), vf)
    return o.reshape(n, lq, h, d)


def _sel_prompt(q, pos, k, v, idx, valid):
    b, s, h, d = q.shape
    kv = k.shape[2]
    nb = s // CMP_BLOCK
    nq = s // SEL_Q_BLOCK
    kb = k.reshape(b, nb, CMP_BLOCK, kv, d).transpose(0, 3, 1, 2, 4)
    vb = v.reshape(b, nb, CMP_BLOCK, kv, d).transpose(0, 3, 1, 2, 4)
    bi = jnp.arange(b)[:, None, None, None]
    hi = jnp.arange(kv)[None, None, :, None]

    def blockwise(t):
        return t.reshape(t.shape[0], nq, SEL_Q_BLOCK, *t.shape[2:]).swapaxes(0, 1)

    def one(args):
        qc, pc, ic, vc = args
        return _sel_attend(qc, pc, kb[bi, hi, ic], vb[bi, hi, ic], ic, vc)

    o = lax.map(one, (blockwise(q), blockwise(pos), blockwise(idx), blockwise(valid)))
    return o.swapaxes(0, 1).reshape(b, s, h, d)


def _win_prompt(q, k, v):
    b, s, h, d = q.shape
    kv = k.shape[2]
    nb = s // WIN_Q_BLOCK
    nprev = WINDOW // WIN_Q_BLOCK
    nw = nprev + 1
    padw = ((0, 0), (WINDOW, 0), (0, 0), (0, 0))
    kp = jnp.pad(k, padw).reshape(b, nb + nprev, WIN_Q_BLOCK, kv, d)
    vp = jnp.pad(v, padw).reshape(b, nb + nprev, WIN_Q_BLOCK, kv, d)
    kw = jnp.concatenate([kp[:, i:i + nb] for i in range(nw)], axis=2)
    vw = jnp.concatenate([vp[:, i:i + nb] for i in range(nw)], axis=2)
    qpos = jnp.arange(s).reshape(nb, WIN_Q_BLOCK)
    kpos = (jnp.arange(nb)[:, None] - nprev) * WIN_Q_BLOCK + jnp.arange(nw * WIN_Q_BLOCK)[None]
    qq, kk = qpos[:, :, None], kpos[:, None, :]
    mask = (kk <= qq) & (qq - kk < WINDOW) & (kk >= 0)
    mask = jnp.broadcast_to(mask[None], (b,) + mask.shape).reshape(b * nb, WIN_Q_BLOCK, nw * WIN_Q_BLOCK)
    o, _ = _gqa_attend(q.reshape(b * nb, WIN_Q_BLOCK, h, d), kw.reshape(b * nb, nw * WIN_Q_BLOCK, kv, d),
                       vw.reshape(b * nb, nw * WIN_Q_BLOCK, kv, d), mask)
    return o.reshape(b, s, h, d)


def _gather_selected(pool, new_rows, page_table, idx):
    n, l, kv, d = new_rows.shape
    n_pages = page_table.shape[1]
    past_blocks = n_pages * PAGE_SIZE // CMP_BLOCK
    nbn = -(-l // CMP_BLOCK)
    newb = jnp.pad(new_rows, ((0, 0), (0, nbn * CMP_BLOCK - l), (0, 0), (0, 0)))
    newb = newb.reshape(n, nbn, CMP_BLOCK, kv, d).transpose(0, 3, 1, 2, 4)
    bi = jnp.arange(n)[:, None, None, None]
    hi = jnp.arange(kv)[None, None, :, None]
    start = idx * CMP_BLOCK
    phys = page_table[bi, jnp.minimum(start // PAGE_SIZE, n_pages - 1)]
    off = (start % PAGE_SIZE)[..., None] + jnp.arange(CMP_BLOCK)
    past = pool[phys[..., None], off, hi[..., None]]
    new = newb[bi, hi, jnp.clip(idx - past_blocks, 0, nbn - 1)]
    return jnp.where((idx < past_blocks)[..., None, None], past, new)


def _nsa_combine(gl, o_cmp, o_sel, o_win):
    n, l = gl.shape[0], gl.shape[1]
    g = jax.nn.sigmoid(gl.astype(F32)).reshape(n, l, NSA_HEADS, 3, 1)
    o = g[..., 0, :] * o_cmp + g[..., 1, :] * o_sel + g[..., 2, :] * o_win
    return o.reshape(n, l, NSA_W).astype(o_cmp.dtype)


def _prompt_mixer(x, win_buf, w_in, w_cmp_k, w_cmp_v, gn_g):
    n, s, _ = x.shape
    main, (ck, cv, sk, sv, wk, wv) = _project(x, w_in, jnp.arange(s))
    ret_out, s_fin = _retention_prompt(main, gn_g, n, s)
    kc, vc = _compress_prompt(main, w_cmp_k, w_cmp_v)
    nsa = _nsa_prompt(main, kc.reshape(n, s // CMP_BLOCK, KV_W), vc.reshape(n, s // CMP_BLOCK, KV_W), n, s)
    feats = (ret_out, nsa)
    if s >= win_buf:
        bk, bv = wk[:, s - win_buf:], wv[:, s - win_buf:]
    else:
        padb = ((0, 0), (win_buf - s, 0), (0, 0), (0, 0))
        bk, bv = jnp.pad(wk, padb), jnp.pad(wv, padb)
    return feats, (ck, cv, sk, sv, bk, bv, s_fin)


def _sample_mixer(x, c_cmp_k, c_cmp_v, c_sel_k, c_sel_v, c_win_k, c_win_v, s_ret, page_table,
                  w_in, w_cmp_k, w_cmp_v, gn_g):
    n, l, _ = x.shape
    past = page_table.shape[1] * PAGE_SIZE
    pos = past + jnp.arange(l)[None]
    assert l == 1
    main, (ck, cv, sk, sv, wk, wv) = _project(x, w_in, jnp.full((n,), past, jnp.int32))
    rq, rk, rv, rg, nq = _split_main(main, n, l)[:5]
    gl = _gate_logits(main, n, l)
    ret_out, s_new = _retention_group(rq, rk, rv, rg, pos, s_ret, gn_g)
    q = _heads(nq, NSA_HEADS)
    assert l == 1 and past % CMP_BLOCK == 0 and c_win_k.shape[1] <= WINDOW
    kc, vc = _compress_paged(c_cmp_k, c_cmp_v, page_table, w_cmp_k, w_cmp_v)
    o_cmp, sel = _sample_select(q[:, 0], kc, vc, past)
    sel = sel[:, :SEL_TOPK].reshape(-1)
    gl_pad = jnp.pad(gl.reshape(n, NSA_HEADS, 3), ((0, 0), (0, 0), (0, LANES - 3)))
    nsa = _sample_attend(sel, page_table, q[:, 0], sk[:, 0], sv[:, 0], wk[:, 0], wv[:, 0],
                         _interleaved(c_win_k), _interleaved(c_win_v), _interleaved(c_sel_k), _interleaved(c_sel_v),
                         o_cmp, gl_pad, past)
    feats = (ret_out.reshape(n * l, RET_W), nsa.reshape(n * l, NSA_W))
    kw = jnp.concatenate([c_win_k, wk], 1)
    vw = jnp.concatenate([c_win_v, wv], 1)
    return feats, (ck, cv, sk, sv, kw[:, l:], vw[:, l:], s_new)


def _route_params(w_group, b_group, w_expert, b_expert):
    w = jnp.concatenate([w_group, w_expert], axis=1)
    b = jnp.concatenate([b_group, b_expert], axis=0)
    pad = LANES - w.shape[1]
    return jnp.pad(w, ((0, 0), (0, pad))).astype(BF16), jnp.pad(b, (0, pad)).reshape(1, LANES)


def _dispatch_plan(route, counts_tile):
    counts = counts_tile[0, EXPERT_LANE0:EXPERT_LANE0 + N_EXPERTS].astype(jnp.int32)
    padded = (counts + MOE_BM - 1) // MOE_BM * MOE_BM
    pad_end = jnp.cumsum(padded)
    pad_start = pad_end - padded
    n_asg = route.shape[0] * EXPERT_TOPK
    n_blk = -(-(n_asg + N_EXPERTS * (MOE_BM - 1)) // MOE_BM)
    blk_first = jnp.arange(n_blk, dtype=jnp.int32) * MOE_BM
    blk_e = jnp.minimum(jnp.sum(pad_end[None, :] <= blk_first[:, None], axis=1), N_EXPERTS - 1).astype(jnp.int32)
    meta = jnp.concatenate([pad_end[-1:] // MOE_BM, jnp.array([n_asg // TABLE_UNROLL]), pad_start + counts,
                            pad_end]).astype(jnp.int32)
    eid = route[:, R_EID:R_EID + EXPERT_TOPK].astype(jnp.int32)
    rank = route[:, R_RANK:R_RANK + EXPERT_TOPK].astype(jnp.int32)
    start = jnp.sum(jnp.where(eid[..., None] == jnp.arange(N_EXPERTS), pad_start, 0), axis=-1)
    return (start + rank).reshape(-1).astype(jnp.int32), blk_e, meta


def kernel(x_prompt, x_sample, cache_cmp_k, cache_cmp_v, cache_sel_k, cache_sel_v, cache_win_k, cache_win_v,
           state_ret, page_table, w_in, w_cmp_k, w_cmp_v, ret_gn_g, w_o, ln1_g, ln1_b, w_group, b_group,
           w_expert, b_expert, w_gate, w_up, w_down, ln2_g, ln2_b):
    win_buf = cache_win_k.shape[2]
    hp, hs = x_prompt, x_sample
    acc_p = [[] for _ in range(7)]
    acc_s = [[] for _ in range(7)]
    for l in range(DEPTH):
        fp, st_p = _prompt_mixer(hp, win_buf, w_in[l], w_cmp_k[l], w_cmp_v[l], ret_gn_g[l])
        fs, st_s = _sample_mixer(hs, cache_cmp_k[l], cache_cmp_v[l], cache_sel_k[l], cache_sel_v[l],
                                 cache_win_k[l], cache_win_v[l], state_ret[l], page_table,
                                 w_in[l], w_cmp_k[l], w_cmp_v[l], ret_gn_g[l])
        w_o_bf16 = w_o[l].astype(BF16)
        w_route, b_route = _route_params(w_group[l], b_group[l], w_expert[l], b_expert[l])
        tp = hp.shape[0] * hp.shape[1]
        ts = hs.shape[0] * hs.shape[1]
        no_counts = jnp.zeros((SUBLANES, LANES), F32)
        h1p, packed_p, route_p, counts_p = _wo_ln_route(fp[0], fp[1], hp.reshape(tp, D_MODEL), w_o_bf16, ln1_g[l],
                                                        ln1_b[l], w_route, b_route, no_counts)
        h1s, packed_s, route_s, counts = _wo_ln_route(fs[0], fs[1], hs.reshape(ts, D_MODEL), w_o_bf16, ln1_g[l],
                                                      ln1_b[l], w_route, b_route, counts_p)
        packed = jnp.concatenate([packed_p, packed_s], axis=0)
        route = jnp.concatenate([route_p, route_s], axis=0)
        slot, blk_e, meta = _dispatch_plan(route, counts)
        plane = -(-(tp + ts) // WO_TM) * WO_TM
        y = _expert_ffn(packed, slot, blk_e, meta, w_gate[l], w_up[l], w_down[l], plane)
        hp = _moe_ln(h1p, y, route, ln2_g[l], ln2_b[l], 0, plane).reshape(hp.shape)
        hs = _moe_ln(h1s, y, route, ln2_g[l], ln2_b[l], tp, plane).reshape(hs.shape)
        for acc, t in zip(acc_p, st_p):
            acc.append(t)
        for acc, t in zip(acc_s, st_s):
            acc.append(t)
    p_cmp_k, p_cmp_v, p_sel_k, p_sel_v, p_win_k, p_win_v, p_ret = [jnp.stack(a) for a in acc_p]
    s_cmp_k, s_cmp_v, s_sel_k, s_sel_v, s_win_k, s_win_v, s_ret = [jnp.stack(a) for a in acc_s]
    return (hp, hs, p_cmp_k, p_cmp_v, p_sel_k, p_sel_v, p_win_k, p_win_v, p_ret.astype(state_ret.dtype),
            s_cmp_k, s_cmp_v, s_sel_k, s_sel_v, s_win_k, s_win_v, s_ret.astype(state_ret.dtype))
```

```python
import functools
import math

import jax
import jax.numpy as jnp
import numpy as np
from jax import lax
from jax.experimental import pallas as pl
from jax.experimental.pallas import tpu as pltpu

D_MODEL = 2048
DEPTH = 1
PAGE_SIZE = 128

F32 = jnp.float32
BF16 = jnp.bfloat16
HEAD_DIM = 128
RET_HEADS = D_MODEL // (2 * HEAD_DIM)
NSA_HEADS = D_MODEL // (2 * HEAD_DIM)
NSA_KV_HEADS = 2
RET_W = RET_HEADS * HEAD_DIM
NSA_W = NSA_HEADS * HEAD_DIM
KV_W = NSA_KV_HEADS * HEAD_DIM
MIX_W = RET_W + NSA_W
RET_CHUNK = 128
RET_ROPE_THETA = 10000.0
ROPE_THETA = 500000.0
ROT_DIM = HEAD_DIM // 4
CMP_BLOCK = 64
SEL_TOPK = 16
WINDOW = 512
WIN_Q_BLOCK = 128
SEL_Q_BLOCK = 64
N_GROUPS = 4
EXPERTS_PER_GROUP = 8
N_EXPERTS = N_GROUPS * EXPERTS_PER_GROUP
EXPERT_TOPK = 2
D_EXPERT = 512
MOE_BLOCK = 128
LN_EPS = 1e-5
GN_EPS = 1e-5
NEG = -1e30
DEEPNORM_ALPHA = (2 * DEPTH) ** 0.25
DEEPNORM_BETA = (8 * DEPTH) ** -0.25
SPLITS = (RET_W, RET_W, RET_W, RET_W, NSA_W, KV_W, KV_W, KV_W, KV_W, KV_W, KV_W, NSA_HEADS * 3)
IN_W = sum(SPLITS)
GATE_W = NSA_HEADS * 3
MAIN_W = IN_W - GATE_W
LANES = 128
VMEM_LIMIT = 48 * 1024 * 1024


PROJ_TN = 512
GL_COL0 = MAIN_W
PROJ_W = -(-(MAIN_W + NSA_KV_HEADS * LANES) // PROJ_TN) * PROJ_TN
ROTATED_SPLITS = (4, 5, 7, 9)


def _rotated_heads():
    cuts = np.cumsum((0,) + SPLITS)
    tiles = []
    for j in range(PROJ_W // PROJ_TN):
        heads = []
        for h in range(PROJ_TN // HEAD_DIM):
            c0 = j * PROJ_TN + h * HEAD_DIM
            split = int(np.searchsorted(cuts, c0, side="right")) - 1
            heads.append(split in ROTATED_SPLITS and c0 < MAIN_W)
        tiles.append(tuple(heads))
    return tiles


def _nsa_rope_tables(pos):
    half = ROT_DIM // 2
    inv = ROPE_THETA ** (-jnp.arange(0, ROT_DIM, 2, dtype=F32) / ROT_DIM)
    ang = pos[:, None].astype(F32) * inv
    cos, sin = jnp.cos(ang), jnp.sin(ang)
    rest = HEAD_DIM - ROT_DIM
    zeros = jnp.zeros((pos.shape[0], half), F32)
    pad = lambda t, fill: jnp.pad(t, ((0, 0), (0, rest)), constant_values=fill)
    return (pad(jnp.concatenate([cos, cos], 1), 1.0), pad(jnp.concatenate([zeros, sin], 1), 0.0),
            pad(jnp.concatenate([-sin, zeros], 1), 0.0))


KV_SPLITS = (5, 6, 7, 8, 9, 10)


def _kv_heads():
    cuts = np.cumsum((0,) + SPLITS)
    tiles = []
    for j in range(MAIN_W // PROJ_TN):
        heads = []
        for h in range(PROJ_TN // HEAD_DIM):
            c0 = j * PROJ_TN + h * HEAD_DIM
            split = int(np.searchsorted(cuts, c0, side="right")) - 1
            heads.append((KV_SPLITS.index(split), (c0 - int(cuts[split])) // HEAD_DIM) if split in KV_SPLITS else None)
        tiles.append(tuple(heads))
    return tiles


def _proj_kernel(x_ref, w_ref, wt_ref, cos_ref, up_ref, dn_ref, o_ref, *kv_refs, patterns, kv_heads):
    j = pl.program_id(1)
    n_main = MAIN_W // PROJ_TN
    xb = x_ref[...].astype(BF16)
    half = ROT_DIM // 2

    def rotated(acc, heads):
        cos, up, dn = cos_ref[...], up_ref[...], dn_ref[...]
        parts = []
        for h, rot in enumerate(heads):
            xh = acc[:, h * HEAD_DIM:(h + 1) * HEAD_DIM]
            if rot:
                xh = xh * cos + pltpu.roll(xh, half, 1) * up + pltpu.roll(xh, HEAD_DIM - half, 1) * dn
            parts.append(xh)
        return jnp.concatenate(parts, axis=1)

    plans = list(zip(patterns[:n_main], kv_heads))
    for plan in sorted(set(plans), key=repr):
        tiles = [t for t, p in enumerate(plans) if p == plan]
        hit = functools.reduce(jnp.logical_or, [j == t for t in tiles])

        @pl.when(hit)
        def _(plan=plan):
            heads, dests = plan
            acc = jnp.dot(xb, w_ref[...], preferred_element_type=F32)
            vals = rotated(acc, heads) if any(heads) else acc
            o_ref[...] = vals
            for h, dest in enumerate(dests):
                if dest is not None:
                    out, k = dest
                    kv_refs[out][pl.ds(k, vals.shape[0], stride=NSA_KV_HEADS), :] = (
                        vals[:, h * HEAD_DIM:(h + 1) * HEAD_DIM])

    @pl.when(j >= n_main)
    def _():
        o_ref[...] = jnp.dot(xb, wt_ref[...], preferred_element_type=F32)


def _project(x, w_in, pos):
    n, l, d = x.shape
    t = n * l
    xt = x.reshape(t, d)
    tm = min(1024, t)
    gpg = GATE_W // NSA_KV_HEADS
    n_main = MAIN_W // PROJ_TN
    assert PROJ_W == MAIN_W + PROJ_TN and not any(any(p) for p in _rotated_heads()[n_main:])
    gate_tiles = [jnp.pad(w_in[:, MAIN_W + k * gpg:MAIN_W + (k + 1) * gpg], ((0, 0), (0, LANES - gpg)))
                  for k in range(NSA_KV_HEADS)]
    fill = jnp.zeros((d, PROJ_TN - NSA_KV_HEADS * LANES), F32)
    w_tail = jnp.concatenate(gate_tiles + [fill], axis=1).astype(BF16)
    tables = _nsa_rope_tables(pos)
    pb = pos.shape[0] // tm
    tab = pl.BlockSpec((tm, HEAD_DIM), lambda i, j: (i % pb, 0))
    kv_rows = pl.BlockSpec((tm * NSA_KV_HEADS, HEAD_DIM), lambda i, j: (i, 0))
    outs = pl.pallas_call(
        functools.partial(_proj_kernel, patterns=_rotated_heads(), kv_heads=_kv_heads()),
        out_shape=[jax.ShapeDtypeStruct((t, PROJ_W), F32)]
        + [jax.ShapeDtypeStruct((t * NSA_KV_HEADS, HEAD_DIM), F32)] * len(KV_SPLITS),
        grid=(t // tm, PROJ_W // PROJ_TN),
        in_specs=[pl.BlockSpec((tm, d), lambda i, j: (i, 0)),
                  pl.BlockSpec((d, PROJ_TN), lambda i, j: (0, jnp.minimum(j, n_main - 1))),
                  pl.BlockSpec((d, PROJ_TN), lambda i, j: (0, 0)), tab, tab, tab],
        out_specs=[pl.BlockSpec((tm, PROJ_TN), lambda i, j: (i, j))] + [kv_rows] * len(KV_SPLITS),
        compiler_params=pltpu.CompilerParams(dimension_semantics=("arbitrary", "arbitrary"),
                                             vmem_limit_bytes=VMEM_LIMIT),
        name="in_proj",
    )(xt, w_in.astype(BF16), w_tail, *tables)
    return outs[0], [o.reshape(n, l, NSA_KV_HEADS, HEAD_DIM) for o in outs[1:]]


def _split_main(main, n, l, first=0):
    cuts = [0] + [int(c) for c in np.cumsum(SPLITS)[:-1]]
    return [main[:, cuts[i]:cuts[i + 1]].reshape(n, l, -1) for i in range(first, len(SPLITS) - 1)]


def _split_col0(i):
    return int(np.cumsum((0,) + SPLITS)[i])


def _gate_logits(main, n, l):
    gpg = GATE_W // NSA_KV_HEADS
    cols = [main[:, GL_COL0 + k * LANES:GL_COL0 + k * LANES + gpg] for k in range(NSA_KV_HEADS)]
    return jnp.concatenate(cols, axis=1).reshape(n, l, GATE_W)


def _compress_kernel(k_ref, v_ref, wk_ref, wv_ref, ko_ref, vo_ref):
    r = ko_ref.shape[0]
    ko_ref[...] = jnp.sum(k_ref[...].reshape(r, CMP_BLOCK, KV_W) * wk_ref[...][None], axis=1)
    vo_ref[...] = jnp.sum(v_ref[...].reshape(r, CMP_BLOCK, KV_W) * wv_ref[...][None], axis=1)


def _compress_prompt(main, w_cmp_k, w_cmp_v):
    t = main.shape[0]
    r = 32
    wk2 = jnp.tile(w_cmp_k, (1, NSA_KV_HEADS))
    wv2 = jnp.tile(w_cmp_v, (1, NSA_KV_HEADS))
    kcol, vcol = _split_col0(5) // KV_W, _split_col0(6) // KV_W
    wsp = pl.BlockSpec((CMP_BLOCK, KV_W), lambda i: (0, 0))
    osp = pl.BlockSpec((r, KV_W), lambda i: (i, 0))
    return pl.pallas_call(
        _compress_kernel,
        out_shape=[jax.ShapeDtypeStruct((t // CMP_BLOCK, KV_W), F32)] * 2,
        grid=(t // (r * CMP_BLOCK),),
        in_specs=[pl.BlockSpec((r * CMP_BLOCK, KV_W), lambda i: (i, kcol)),
                  pl.BlockSpec((r * CMP_BLOCK, KV_W), lambda i: (i, vcol)), wsp, wsp],
        out_specs=[osp, osp],
        compiler_params=pltpu.CompilerParams(dimension_semantics=("arbitrary",)),
        name="compress_prompt",
    )(main, main, wk2, wv2)


PAGES_PER_STEP = 64
BLOCKS_PER_PAGE = PAGE_SIZE // CMP_BLOCK


SUBLANES = 8
ROWS_PER_BLOCK = CMP_BLOCK * NSA_KV_HEADS
ROWS_PER_PAGE = PAGE_SIZE * NSA_KV_HEADS
BLOCKS_PER_TILE = SUBLANES // NSA_KV_HEADS


def _div(x, n):
    assert n & (n - 1) == 0
    return jnp.right_shift(x, n.bit_length() - 1)


def _mod(x, n):
    assert n & (n - 1) == 0
    return jnp.bitwise_and(x, n - 1)


def _interleaved(pool):
    return pool.reshape(pool.shape[:-3] + (pool.shape[-3] * NSA_KV_HEADS, HEAD_DIM))


def _compress_paged_kernel(pt_ref, pk_ref, pv_ref, wk_ref, wv_ref, ko_ref, vo_ref, kbuf, vbuf, sem):
    s, i = pl.program_id(0), pl.program_id(1)
    n_i = pl.num_programs(1)
    step = s * n_i + i
    last = pl.num_programs(0) * n_i - 1
    pair = _div(lax.broadcasted_iota(jnp.int32, (SUBLANES, HEAD_DIM), 0), NSA_KV_HEADS)

    def fetch(seq, blk, slot):
        for j in range(PAGES_PER_STEP):
            page = pt_ref[seq, blk * PAGES_PER_STEP + j]
            pltpu.make_async_copy(pk_ref.at[page], kbuf.at[slot, j], sem.at[slot]).start(priority=j % 2)
            pltpu.make_async_copy(pv_ref.at[page], vbuf.at[slot, j], sem.at[slot]).start(priority=(j + 1) % 2)

    @pl.when(step == 0)
    def _():
        fetch(0, 0, 0)

    slot = jnp.bitwise_and(step, 1)

    @pl.when(step < last)
    def _():
        wrap = i == n_i - 1
        fetch(jnp.where(wrap, s + 1, s), jnp.where(wrap, 0, i + 1), 1 - slot)

    pltpu.make_async_copy(kbuf.at[slot], kbuf.at[slot], sem.at[slot]).wait()
    pltpu.make_async_copy(vbuf.at[slot], vbuf.at[slot], sem.at[slot]).wait()

    def summaries(buf, w):
        sums = []
        for j in range(PAGES_PER_STEP):
            for b in range(BLOCKS_PER_PAGE):
                y = buf[slot, j, b * ROWS_PER_BLOCK:(b + 1) * ROWS_PER_BLOCK, :] * w
                acc = jnp.sum(y.reshape(ROWS_PER_BLOCK // SUBLANES, SUBLANES, HEAD_DIM), axis=0)
                shift = SUBLANES // 2
                while shift >= NSA_KV_HEADS:
                    acc = acc + pltpu.roll(acc, shift, 0)
                    shift //= 2
                sums.append(acc)
        tiles = []
        for t in range(len(sums) // BLOCKS_PER_TILE):
            tile = sums[t * BLOCKS_PER_TILE]
            for j in range(1, BLOCKS_PER_TILE):
                tile = jnp.where(pair == j, sums[t * BLOCKS_PER_TILE + j], tile)
            tiles.append(tile)
        return jnp.concatenate(tiles, axis=0)

    ko_ref[0] = summaries(kbuf, wk_ref[...])
    vo_ref[0] = summaries(vbuf, wv_ref[...])


def _compress_paged(pool_k, pool_v, page_table, w_cmp_k, w_cmp_v):
    n, n_pages = page_table.shape
    wk2 = jnp.repeat(w_cmp_k, NSA_KV_HEADS, axis=0)
    wv2 = jnp.repeat(w_cmp_v, NSA_KV_HEADS, axis=0)
    anywhere = pl.BlockSpec(memory_space=pl.ANY)
    wsp = pl.BlockSpec((ROWS_PER_BLOCK, HEAD_DIM), lambda s, i, pt: (0, 0))
    rows = PAGES_PER_STEP * BLOCKS_PER_PAGE * NSA_KV_HEADS
    osp = pl.BlockSpec((1, rows, HEAD_DIM), lambda s, i, pt: (s, i, 0))
    page_buf = pltpu.VMEM((2, PAGES_PER_STEP, ROWS_PER_PAGE, HEAD_DIM), F32)
    return pl.pallas_call(
        _compress_paged_kernel,
        out_shape=[jax.ShapeDtypeStruct((n, n_pages * BLOCKS_PER_PAGE * NSA_KV_HEADS, HEAD_DIM), F32)] * 2,
        grid_spec=pltpu.PrefetchScalarGridSpec(
            num_scalar_prefetch=1,
            grid=(n, n_pages // PAGES_PER_STEP),
            in_specs=[anywhere, anywhere, wsp, wsp],
            out_specs=[osp, osp],
            scratch_shapes=[page_buf, page_buf, pltpu.SemaphoreType.DMA((2,))]),
        compiler_params=pltpu.CompilerParams(dimension_semantics=("arbitrary", "arbitrary"),
                                             vmem_limit_bytes=VMEM_LIMIT),
        name="compress_paged",
    )(page_table, _interleaved(pool_k), _interleaved(pool_v), wk2, wv2)


SS_SEQ = 8
LOWEST = -3.0e38


def _sample_select_kernel(q_ref, kc_ref, vc_ref, ocmp_ref, sel_ref, *, pos):
    ss, nbk = q_ref.shape[0], kc_ref.shape[1]
    kv, g = NSA_KV_HEADS, NSA_GROUP
    head = lax.broadcasted_iota(jnp.int32, (NSA_HEADS, nbk), 0)
    col = lax.broadcasted_iota(jnp.int32, (NSA_HEADS, nbk), 1)
    m = (_mod(col, kv) == _div(head, g)) & ((_div(col, kv) + 1) * CMP_BLOCK - 1 <= pos)
    width = nbk + LANES
    rows = []
    for i in range(ss):
        q = (q_ref[i] * (HEAD_DIM ** -0.5)).astype(BF16)
        p = _masked_softmax(_dot_nt(q, kc_ref[i].astype(BF16)), m)
        ocmp_ref[i] = jnp.dot(p.astype(BF16), vc_ref[i].astype(BF16), preferred_element_type=F32)
        for k in range(kv):
            imp = jnp.sum(p[k * g:(k + 1) * g], axis=0, keepdims=True)
            rows.append(jnp.concatenate([imp, jnp.zeros((1, LANES), F32)], axis=1))
    nrow = ss * kv
    r_iota = lax.broadcasted_iota(jnp.int32, (nrow, width), 0)
    ccol = lax.broadcasted_iota(jnp.int32, (nrow, width), 1)
    cand = jnp.zeros((nrow, width), F32)
    for r, row in enumerate(rows):
        cand = jnp.where(r_iota == r, row, cand)
    cblk = _div(ccol, kv)
    cur = pos // CMP_BLOCK
    n_sel = -(-(pos + 1) // CMP_BLOCK)
    forced = (cblk == 0) | (cblk == cur) | (cblk == cur - 1)
    score = jnp.where(cblk > cur, NEG, jnp.where(forced, -NEG, cand))
    score = jnp.where((_mod(ccol, kv) == _mod(r_iota, kv)) & (cblk < n_sel), score, LOWEST)
    colf = ccol.astype(F32)
    lane = lax.broadcasted_iota(jnp.int32, (nrow, LANES), 1)
    sel = jnp.zeros((nrow, LANES), jnp.int32)
    for t in range(SEL_TOPK):
        mx = jnp.max(score, axis=1, keepdims=True)
        c = jnp.min(jnp.where(score == mx, colf, -LOWEST), axis=1, keepdims=True)
        picked = jnp.where(mx > 0.5 * NEG, _div(c.astype(jnp.int32), kv), -1)
        sel = jnp.where(lane == t, picked, sel)
        score = jnp.where(colf == c, LOWEST, score)
    sel_ref[...] = sel


def _sample_select(q, kc, vc, pos):
    n = q.shape[0]
    nbk = kc.shape[1]
    qsp = pl.BlockSpec((SS_SEQ, NSA_HEADS, HEAD_DIM), lambda i: (i, 0, 0))
    csp = pl.BlockSpec((SS_SEQ, nbk, HEAD_DIM), lambda i: (i, 0, 0))
    return pl.pallas_call(
        functools.partial(_sample_select_kernel, pos=pos),
        out_shape=[jax.ShapeDtypeStruct((n, NSA_HEADS, HEAD_DIM), F32),
                   jax.ShapeDtypeStruct((n * NSA_KV_HEADS, LANES), jnp.int32)],
        grid=(n // SS_SEQ,),
        in_specs=[qsp, csp, csp],
        out_specs=[qsp, pl.BlockSpec((SS_SEQ * NSA_KV_HEADS, LANES), lambda i: (i, 0))],
        compiler_params=pltpu.CompilerParams(dimension_semantics=("arbitrary",), vmem_limit_bytes=VMEM_LIMIT),
        name="sample_select",
    )(q, kc, vc)


N_SLOTS = NSA_KV_HEADS * SEL_TOPK


def _sample_attend_kernel(sel_ref, pt_ref, q_ref, knew_ref, vnew_ref, wknew_ref, wvnew_ref, wkb_ref, wvb_ref,
                          ocmp_ref, gl_ref, *rest, pos, past_blocks):
    del pt_ref
    kblk, vblk, o_ref = rest[:N_SLOTS], rest[N_SLOTS:2 * N_SLOTS], rest[2 * N_SLOTS]
    kv, g, rb = NSA_KV_HEADS, NSA_GROUP, ROWS_PER_BLOCK
    n = pl.program_id(0)
    q = q_ref[0] * (HEAD_DIM ** -0.5)
    row = lax.broadcasted_iota(jnp.int32, (rb, HEAD_DIM), 0)
    col = lax.broadcasted_iota(jnp.int32, (1, rb), 1)

    def new_block(ref):
        out = jnp.zeros((rb, HEAD_DIM), F32)
        for k in range(kv):
            out = jnp.where(row == k, ref[0, k:k + 1, :], out)
        return out

    def per_head_rows(ref):
        return jnp.concatenate([jnp.broadcast_to(ref[0, k:k + 1, :], (g, HEAD_DIM)) for k in range(kv)], axis=0)

    knew, vnew = new_block(knew_ref), new_block(vnew_ref)
    o_sel = []
    for k in range(kv):
        ks, vs, ms = [], [], []
        for j in range(SEL_TOPK):
            b = sel_ref[(n * kv + k) * SEL_TOPK + j]
            is_new = jnp.broadcast_to(b, (rb, HEAD_DIM)) >= past_blocks
            ks.append(jnp.where(is_new, knew, kblk[k * SEL_TOPK + j][0]))
            vs.append(jnp.where(is_new, vnew, vblk[k * SEL_TOPK + j][0]))
            first = jnp.where(b >= 0, b, 1 << 24) * CMP_BLOCK
            ms.append((first + _div(col, kv) <= pos) & (_mod(col, kv) == k))
        s = _dot_nt(q[k * g:(k + 1) * g].astype(BF16), jnp.concatenate(ks, axis=0).astype(BF16))
        p = _masked_softmax(s, jnp.concatenate(ms, axis=1))
        o_sel.append(jnp.dot(p.astype(BF16), jnp.concatenate(vs, axis=0).astype(BF16), preferred_element_type=F32))
    o_sel = jnp.concatenate(o_sel, axis=0)

    nwr = wkb_ref.shape[1]
    s_w = _dot_nt(q.astype(BF16), wkb_ref[0].astype(BF16))
    cw = lax.broadcasted_iota(jnp.int32, (NSA_HEADS, nwr), 1)
    hw = lax.broadcasted_iota(jnp.int32, (NSA_HEADS, nwr), 0)
    kpos = pos - nwr // kv + _div(cw, kv)
    mw = (kpos <= pos) & (pos - kpos < WINDOW) & (_mod(cw, kv) == _div(hw, g))
    s_n = jnp.sum(q * per_head_rows(wknew_ref), axis=1, keepdims=True)
    smw = jnp.where(mw, s_w, NEG)
    mx = jnp.maximum(jnp.max(smw, axis=1, keepdims=True), s_n)
    e_w = jnp.where(mw, jnp.exp(smw - mx), 0.0)
    e_n = jnp.exp(s_n - mx)
    den = jnp.sum(e_w, axis=1, keepdims=True) + e_n
    o_win = (jnp.dot(e_w.astype(BF16), wvb_ref[0].astype(BF16), preferred_element_type=F32)
             + e_n * per_head_rows(wvnew_ref)) / den

    gate = 1.0 / (1.0 + jnp.exp(-gl_ref[0]))
    o_ref[0] = gate[:, 0:1] * ocmp_ref[0] + gate[:, 1:2] * o_sel + gate[:, 2:3] * o_win


def _sample_attend(sel, page_table, q, sk, sv, wk, wv, win_k, win_v, pool_k, pool_v, o_cmp, gl, pos):
    n, n_pages = page_table.shape
    past_blocks = n_pages * BLOCKS_PER_PAGE

    def slot_spec(k, j):
        def imap(s, sel_r, pt_r):
            b = jnp.maximum(sel_r[(s * NSA_KV_HEADS + k) * SEL_TOPK + j], 0)
            page = jnp.minimum(_div(b, BLOCKS_PER_PAGE), n_pages - 1)
            return (pt_r[s * n_pages + page], _mod(b, BLOCKS_PER_PAGE), 0)
        return pl.BlockSpec((1, ROWS_PER_BLOCK, HEAD_DIM), imap)

    slots = [slot_spec(k, j) for k in range(NSA_KV_HEADS) for j in range(SEL_TOPK)]
    per_seq = lambda a: pl.BlockSpec((1,) + a.shape[1:], lambda s, sel_r, pt_r: (s, 0, 0))
    dense = [q, sk, sv, wk, wv, win_k, win_v, o_cmp, gl]
    return pl.pallas_call(
        functools.partial(_sample_attend_kernel, pos=pos, past_blocks=past_blocks),
        out_shape=jax.ShapeDtypeStruct((n, NSA_HEADS, HEAD_DIM), F32),
        grid_spec=pltpu.PrefetchScalarGridSpec(
            num_scalar_prefetch=2,
            grid=(n,),
            in_specs=[per_seq(a) for a in dense] + slots + slots,
            out_specs=pl.BlockSpec((1, NSA_HEADS, HEAD_DIM), lambda s, sel_r, pt_r: (s, 0, 0))),
        compiler_params=pltpu.CompilerParams(dimension_semantics=("arbitrary",), vmem_limit_bytes=VMEM_LIMIT),
        name="sample_attend",
    )(sel, page_table.reshape(-1), *dense, *([pool_k] * N_SLOTS), *([pool_v] * N_SLOTS))


def _retention_tables(seq):
    c = RET_CHUNK
    log_g = jnp.log1p(-jnp.exp2(-5.0 - jnp.arange(RET_HEADS, dtype=F32)))
    i = jnp.arange(c, dtype=F32)
    rel = i[:, None] - i[None, :]
    dmask = jnp.where(rel[None] >= 0, jnp.exp(jnp.maximum(rel[None], 0.0) * log_g[:, None, None]), 0.0)
    q_dec = jnp.exp((i + 1.0)[None] * log_g[:, None])[..., None]
    k_dec = jnp.exp((c - 1.0 - i)[None] * log_g[:, None])[..., None]
    c_dec = jnp.exp(c * log_g)[:, None, None]
    bc = lambda t: jnp.broadcast_to(t, (RET_HEADS, c, HEAD_DIM))
    inv = RET_ROPE_THETA ** (-jnp.arange(0, HEAD_DIM, 2, dtype=F32) / HEAD_DIM)
    ang = jnp.arange(seq)[:, None].astype(F32) * inv
    cos, sin = jnp.cos(ang), jnp.sin(ang)
    return (dmask, bc(q_dec), bc(k_dec), jnp.broadcast_to(c_dec, (RET_HEADS, 1, HEAD_DIM)),
            jnp.concatenate([cos, cos], -1), jnp.concatenate([-sin, sin], -1))


def _retention_kernel(q_ref, k_ref, v_ref, g_ref, cos_ref, sin_ref, dmask_ref, qdec_ref, kdec_ref, cdec_ref, gn_ref,
                      o_ref, st_ref, s_scr):
    c = pl.program_id(1)

    @pl.when(c == 0)
    def _():
        s_scr[...] = jnp.zeros(s_scr.shape, F32)

    cosf, sins = cos_ref[...], sin_ref[...]
    half = HEAD_DIM // 2
    for h in range(RET_HEADS):
        sl = slice(h * HEAD_DIM, (h + 1) * HEAD_DIM)
        qh, kh = q_ref[:, sl], k_ref[:, sl]
        qr = qh * cosf + pltpu.roll(qh, half, 1) * sins
        kr = (kh * cosf + pltpu.roll(kh, half, 1) * sins) * (HEAD_DIM ** -0.5)
        vb = v_ref[:, sl].astype(BF16)
        att = _dot_nt(qr.astype(BF16), kr.astype(BF16)) * dmask_ref[h]
        s_prev = s_scr[h]
        o = (jnp.dot(att.astype(BF16), vb, preferred_element_type=F32)
             + jnp.dot((qr * qdec_ref[h]).astype(BF16), s_prev.astype(BF16), preferred_element_type=F32))
        s_scr[h] = cdec_ref[h] * s_prev + lax.dot_general(
            (kr * kdec_ref[h]).astype(BF16), vb, (((0,), (0,)), ((), ())), preferred_element_type=F32)
        mu = jnp.mean(o, axis=-1, keepdims=True)
        var = jnp.mean(jnp.square(o - mu), axis=-1, keepdims=True)
        on = (o - mu) * lax.rsqrt(var + GN_EPS) * gn_ref[:, sl]
        gg = g_ref[:, sl]
        o_ref[:, sl] = gg * (1.0 / (1.0 + jnp.exp(-gg))) * on

    @pl.when(c == pl.num_programs(1) - 1)
    def _():
        st_ref[0] = s_scr[...]


def _retention_prompt(main, gn_g, batch, seq):
    nc = seq // RET_CHUNK
    dmask, q_dec, k_dec, c_dec, cosf, sins = _retention_tables(seq)
    col = lambda j: pl.BlockSpec((RET_CHUNK, RET_W), lambda b, c: (b * nc + c, j))
    pos_tab = pl.BlockSpec((RET_CHUNK, HEAD_DIM), lambda b, c: (c, 0))
    full = lambda a: pl.BlockSpec(a.shape, lambda b, c: (0,) * a.ndim)
    gn = gn_g.reshape(1, RET_W)
    return pl.pallas_call(
        _retention_kernel,
        out_shape=[jax.ShapeDtypeStruct((batch * seq, RET_W), F32),
                   jax.ShapeDtypeStruct((batch, RET_HEADS, HEAD_DIM, HEAD_DIM), F32)],
        grid=(batch, nc),
        in_specs=[col(0), col(1), col(2), col(3), pos_tab, pos_tab,
                  full(dmask), full(q_dec), full(k_dec), full(c_dec), full(gn)],
        out_specs=[pl.BlockSpec((RET_CHUNK, RET_W), lambda b, c: (b * nc + c, 0)),
                   pl.BlockSpec((1, RET_HEADS, HEAD_DIM, HEAD_DIM), lambda b, c: (b, 0, 0, 0))],
        scratch_shapes=[pltpu.VMEM((RET_HEADS, HEAD_DIM, HEAD_DIM), F32)],
        compiler_params=pltpu.CompilerParams(dimension_semantics=("arbitrary", "arbitrary"),
                                             vmem_limit_bytes=VMEM_LIMIT),
        name="retention_prompt",
    )(main, main, main, main, cosf, sins, dmask, q_dec, k_dec, c_dec, gn)


NSA_TQ = 128
NSA_KEY_CHUNK = 512
NSA_GROUP = NSA_HEADS // NSA_KV_HEADS


def _dot_nt(a, b):
    return lax.dot_general(a, b, (((1,), (1,)), ((), ())), preferred_element_type=F32)


def _masked_softmax(s, m):
    sm = jnp.where(m, s, NEG)
    e = jnp.exp(sm - jnp.max(sm, axis=-1, keepdims=True))
    return jnp.where(m, e / jnp.sum(e, axis=-1, keepdims=True), 0.0)


def _select_mask_t(imp_t, pos_t):
    nb = imp_t.shape[0]
    blk = lax.broadcasted_iota(jnp.int32, imp_t.shape, 0)
    cur = jnp.right_shift(pos_t, int(math.log2(CMP_BLOCK)))
    forced = (blk == 0) | (blk == cur) | (blk == cur - 1)
    score = jnp.where(blk > cur, NEG, jnp.where(forced, -NEG, imp_t))
    rank = jnp.zeros(score.shape, jnp.int32)
    for i in range(nb):
        si = score[i:i + 1, :]
        ahead = (si > score) | ((si == score) & (blk > i))
        rank = rank + ahead.astype(jnp.int32)
    return (rank < SEL_TOPK) & (score > 0.5 * NEG)


def _nsa_prompt_kernel(q_ref, kc_ref, vc_ref, sk_ref, sv_ref, wk_ref, wv_ref, gl_ref, o_ref, skt, wkt):
    tq, g, kc_n = NSA_TQ, NSA_GROUP, NSA_KEY_CHUNK
    qi = pl.program_id(2)
    nb = kc_ref.shape[1]

    @pl.when(qi == 0)
    def _():
        blk_row = lax.broadcasted_iota(jnp.int32, (nb, kc_n), 0)
        key_col = lax.broadcasted_iota(jnp.int32, (nb, kc_n), 1)
        for c in range(skt.shape[0]):
            member = jnp.right_shift(key_col + c * kc_n, int(math.log2(CMP_BLOCK))) == blk_row
            skt[c] = jnp.concatenate([sk_ref[c * kc_n:(c + 1) * kc_n, :].T.astype(BF16),
                                      jnp.where(member, 1.0, 0.0).astype(BF16)], axis=0)
        for w in range(wkt.shape[0]):
            wkt[w] = wk_ref[w * tq:(w + 1) * tq, :].T.astype(BF16)

    q4 = q_ref[...] * (HEAD_DIM ** -0.5)
    qs = jnp.concatenate([q4[:, i * HEAD_DIM:(i + 1) * HEAD_DIM] for i in range(g)], axis=0).astype(BF16)
    pos = qi * tq + lax.broadcasted_iota(jnp.int32, (tq, 1), 0)
    pos4 = jnp.concatenate([pos] * g, axis=0)

    assert tq == LANES
    s_c = _dot_nt(kc_ref[0].astype(BF16), qs)
    blk_t = lax.broadcasted_iota(jnp.int32, (nb, g * tq), 0)
    pos_t = qi * tq + lax.broadcasted_iota(jnp.int32, (1, tq), 1)
    m_c = (blk_t + 1) * CMP_BLOCK - 1 <= jnp.concatenate([pos_t] * g, axis=1)
    sm_c = jnp.where(m_c, s_c, NEG)
    e_c = jnp.exp(sm_c - jnp.max(sm_c, axis=0, keepdims=True))
    p_c = jnp.where(m_c, e_c / jnp.sum(e_c, axis=0, keepdims=True), 0.0)
    o_cmp = lax.dot_general(p_c.astype(BF16), vc_ref[0].astype(BF16), (((0,), (0,)), ((), ())),
                            preferred_element_type=F32)
    imp_t = p_c[:, 0:tq]
    for i in range(1, g):
        imp_t = imp_t + p_c[:, i * tq:(i + 1) * tq]

    seln_t = jnp.where(_select_mask_t(imp_t, pos_t), 0.0, NEG)
    seln = jnp.concatenate([seln_t, jnp.zeros((tq - nb, tq), F32)], axis=0).T[:, :nb].astype(BF16)
    q_aug = jnp.concatenate([qs, jnp.concatenate([seln] * g, axis=0)], axis=1)
    key_row = lax.broadcasted_iota(jnp.int32, (1, kc_n), 1)

    def scores(c):
        return jnp.dot(q_aug, skt[c], preferred_element_type=F32)

    def values(c):
        return sv_ref[pl.ds(pl.multiple_of(c * kc_n, kc_n), kc_n), :].astype(BF16)

    assert kc_n % tq == 0
    c_diag = _div(qi, kc_n // tq)
    s_d = jnp.where(key_row + c_diag * kc_n <= pos4, scores(c_diag), NEG)
    m_d = jnp.max(s_d, axis=-1, keepdims=True)
    p_d = jnp.exp(s_d - m_d)
    first = (m_d, jnp.sum(p_d, axis=-1, keepdims=True),
             jnp.dot(p_d.astype(BF16), values(c_diag), preferred_element_type=F32))

    def chunk(c, carry):
        m_i, l_i, acc = carry
        s = scores(c)
        m_new = jnp.maximum(m_i, jnp.max(s, axis=-1, keepdims=True))
        alpha = jnp.exp(m_i - m_new)
        p = jnp.exp(s - m_new)
        l_new = alpha * l_i + jnp.sum(p, axis=-1, keepdims=True)
        return m_new, l_new, alpha * acc + jnp.dot(p.astype(BF16), values(c), preferred_element_type=F32)

    _, l_f, acc_f = lax.fori_loop(0, c_diag, chunk, first)
    o_sel = acc_f / l_f

    nwk = WINDOW + tq
    kstart = pl.multiple_of(jnp.maximum(qi * tq - WINDOW, 0), tq)
    kpos = kstart + lax.broadcasted_iota(jnp.int32, (1, nwk), 1)
    band = jnp.where((kpos <= pos) & (pos - kpos < WINDOW), 0.0, NEG)
    w0 = jnp.maximum(qi - WINDOW // tq, 0)
    kwt = jnp.concatenate([wkt[w0 + m] for m in range(nwk // tq)], axis=1)
    s_w = jnp.dot(qs, kwt, preferred_element_type=F32) + jnp.concatenate([band] * g, axis=0)
    e_w = jnp.exp(s_w - jnp.max(s_w, axis=-1, keepdims=True))
    o_win = (jnp.dot(e_w.astype(BF16), wv_ref[pl.ds(kstart, nwk), :].astype(BF16), preferred_element_type=F32)
             / jnp.sum(e_w, axis=-1, keepdims=True))

    gate = 1.0 / (1.0 + jnp.exp(-gl_ref[...]))
    for i in range(g):
        rows = slice(i * tq, (i + 1) * tq)
        o_ref[:, i * HEAD_DIM:(i + 1) * HEAD_DIM] = (gate[:, 3 * i:3 * i + 1] * o_cmp[rows]
                                                     + gate[:, 3 * i + 1:3 * i + 2] * o_sel[rows]
                                                     + gate[:, 3 * i + 2:3 * i + 3] * o_win[rows])


def _nsa_prompt(main, kc, vc, batch, seq):
    nq = seq // NSA_TQ
    gw = NSA_GROUP * HEAD_DIM
    qcol, glcol = _split_col0(4) // gw, GL_COL0 // LANES
    seq_spec = lambda split: pl.BlockSpec((seq, HEAD_DIM),
                                          lambda b, k, i: (b, _split_col0(split) // HEAD_DIM + k))
    cmp_spec = pl.BlockSpec((1, seq // CMP_BLOCK, HEAD_DIM), lambda b, k, i: (b, 0, k))
    return pl.pallas_call(
        _nsa_prompt_kernel,
        out_shape=jax.ShapeDtypeStruct((batch * seq, NSA_W), F32),
        grid=(batch, NSA_KV_HEADS, nq),
        in_specs=[pl.BlockSpec((NSA_TQ, gw), lambda b, k, i: (b * nq + i, qcol + k)), cmp_spec, cmp_spec,
                  seq_spec(7), seq_spec(8), seq_spec(9), seq_spec(10),
                  pl.BlockSpec((NSA_TQ, LANES), lambda b, k, i: (b * nq + i, glcol + k))],
        out_specs=pl.BlockSpec((NSA_TQ, gw), lambda b, k, i: (b * nq + i, k)),
        scratch_shapes=[pltpu.VMEM((seq // NSA_KEY_CHUNK, HEAD_DIM + seq // CMP_BLOCK, NSA_KEY_CHUNK), BF16),
                        pltpu.VMEM((seq // NSA_TQ, HEAD_DIM, NSA_TQ), BF16)],
        compiler_params=pltpu.CompilerParams(dimension_semantics=("arbitrary",) * 3,
                                             vmem_limit_bytes=VMEM_LIMIT),
        name="nsa_prompt",
    )(main, kc, vc, main, main, main, main, main)


WO_TM = 512


def _layer_norm_rows(y, g, b):
    mu = jnp.mean(y, axis=-1, keepdims=True)
    var = jnp.mean(jnp.square(y - mu), axis=-1, keepdims=True)
    return (y - mu) * lax.rsqrt(var + LN_EPS) * g + b


EXPERT_LANE0 = N_GROUPS
R_EID, R_RANK, R_GATE = 0, EXPERT_TOPK, 2 * EXPERT_TOPK


def _route(h, wr_ref, br_ref, carry):
    tm = h.shape[0]
    logit = jnp.dot(h.astype(BF16), wr_ref[...], preferred_element_type=F32) + br_ref[...]
    lane = lax.broadcasted_iota(jnp.int32, (tm, LANES), 1)
    lanef = lane.astype(F32)
    first_lane = lambda hit: jnp.min(jnp.where(hit, lanef, float(LANES)), axis=1, keepdims=True)
    is_g = lane < N_GROUPS
    gl = jnp.where(is_g, logit, LOWEST)
    gmx = jnp.max(gl, axis=1, keepdims=True)
    grp = first_lane(gl == gmx)
    p_grp = 1.0 / jnp.sum(jnp.where(is_g, jnp.exp(gl - gmx), 0.0), axis=1, keepdims=True)
    lane_grp = jnp.right_shift(lane - EXPERT_LANE0, int(math.log2(EXPERTS_PER_GROUP)))
    in_grp = lane_grp.astype(F32) == grp
    el = jnp.where(in_grp, logit, LOWEST)
    ee = jnp.where(in_grp, jnp.exp(el - jnp.max(el, axis=1, keepdims=True)), 0.0)
    pe = jnp.where(in_grp, ee / jnp.sum(ee, axis=1, keepdims=True), -1.0)
    p1 = jnp.max(pe, axis=1, keepdims=True)
    l1 = first_lane(pe == p1)
    pe2 = jnp.where(lanef == l1, -1.0, pe)
    p2 = jnp.max(pe2, axis=1, keepdims=True)
    l2 = first_lane(pe2 == p2)
    den = p1 + p2
    o1, o2 = lanef == l1, lanef == l2
    onehot = jnp.where(o1 | o2, 1.0, 0.0)
    r = lax.broadcasted_iota(jnp.int32, (tm, tm), 0)
    c = lax.broadcasted_iota(jnp.int32, (tm, tm), 1)
    earlier = jnp.where(c < r, 1.0, 0.0).astype(BF16)
    prefix = jnp.dot(earlier, onehot.astype(BF16), preferred_element_type=F32) + carry[0:1, :]
    rank1 = jnp.sum(jnp.where(o1, prefix, 0.0), axis=1, keepdims=True)
    rank2 = jnp.sum(jnp.where(o2, prefix, 0.0), axis=1, keepdims=True)
    carry[0:1, :] = carry[0:1, :] + jnp.sum(onehot, axis=0, keepdims=True)
    fields = [l1 - EXPERT_LANE0, l2 - EXPERT_LANE0, rank1, rank2, p_grp * p1 / den, p_grp * p2 / den]
    rec = jnp.zeros((tm, LANES), F32)
    for j, f in enumerate(fields):
        rec = jnp.where(lane == j, f, rec)
    return rec


def _pack_bf16_pairs(x):
    w = x.shape[1] // 2
    return pltpu.pack_elementwise([x[:, :w], x[:, w:]], packed_dtype=jnp.bfloat16)


def _rows_to_tiles(ref, index, p):
    r = p.shape[0]
    for s in range(SUBLANES):
        ref[index + (pl.ds(s, r, stride=SUBLANES), slice(None))] = p[:, s * LANES:(s + 1) * LANES]


def _tiles_to_rows(ref, index, r):
    return jnp.concatenate([ref[index + (pl.ds(s, r, stride=SUBLANES), slice(None))] for s in range(SUBLANES)],
                           axis=1)


def _unpack_bf16_pairs(p):
    halves = [pltpu.unpack_elementwise(p, index=i, packed_dtype=jnp.bfloat16, unpacked_dtype=F32) for i in (0, 1)]
    return jnp.concatenate(halves, axis=1)


def _wo_ln_route_kernel(fr_ref, fn_ref, x_ref, w_ref, g_ref, b_ref, wr_ref, br_ref, base_ref,
                        h_ref, hp_ref, route_ref, cnt_ref, carry):
    i = pl.program_id(0)

    @pl.when(i == 0)
    def _():
        carry[...] = base_ref[...]

    y = (DEEPNORM_ALPHA * x_ref[...]
         + jnp.dot(fr_ref[...].astype(BF16), w_ref[0:RET_W, :], preferred_element_type=F32)
         + jnp.dot(fn_ref[...].astype(BF16), w_ref[RET_W:MIX_W, :], preferred_element_type=F32))
    h = _layer_norm_rows(y, g_ref[...], b_ref[...])
    h_ref[...] = h
    _rows_to_tiles(hp_ref, (), _pack_bf16_pairs(h))
    route_ref[...] = _route(h, wr_ref, br_ref, carry)

    @pl.when(i == pl.num_programs(0) - 1)
    def _():
        cnt_ref[...] = carry[...]


def _wo_ln_route(f_ret, f_nsa, x, w_o_bf16, ln_g, ln_b, w_route, b_route, base_counts):
    t, d = x.shape
    tm = min(WO_TM, t)
    assert t % tm == 0
    row = lambda w: pl.BlockSpec((tm, w), lambda i: (i, 0))
    full = lambda a: pl.BlockSpec(a.shape, lambda i: (0,) * a.ndim)
    lg, lb = ln_g.reshape(1, d), ln_b.reshape(1, d)
    return pl.pallas_call(
        _wo_ln_route_kernel,
        out_shape=[jax.ShapeDtypeStruct((t, d), F32), jax.ShapeDtypeStruct((t * SUBLANES, LANES), jnp.uint32),
                   jax.ShapeDtypeStruct((t, LANES), F32), jax.ShapeDtypeStruct((SUBLANES, LANES), F32)],
        grid=(t // tm,),
        in_specs=[row(RET_W), row(NSA_W), row(d), full(w_o_bf16), full(lg), full(lb),
                  full(w_route), full(b_route), full(base_counts)],
        out_specs=[row(d), pl.BlockSpec((tm * SUBLANES, LANES), lambda i: (i, 0)), row(LANES),
                   pl.BlockSpec((SUBLANES, LANES), lambda i: (0, 0))],
        scratch_shapes=[pltpu.VMEM((SUBLANES, LANES), F32)],
        compiler_params=pltpu.CompilerParams(dimension_semantics=("arbitrary",), vmem_limit_bytes=VMEM_LIMIT),
        name="wo_ln1_route",
    )(f_ret, f_nsa, x, w_o_bf16, lg, lb, w_route, b_route, base_counts)


MOE_BM = 256
MOE_SUB = 64
TABLE_UNROLL = 8
META_N_USED, META_FILL_TRIPS, META_PAD_LO, META_PAD_HI = 0, 1, 2, 2 + N_EXPERTS


def _expert_kernel(blk_e_ref, meta_ref, slot_ref, h_ref, wg_ref, wu_ref, wd_ref, y_ref,
                   src_tok, dst_row, xbuf, obuf, gsem, ssem, *, plane):
    del blk_e_ref
    n_sub, sub = xbuf.shape[1], xbuf.shape[2] // SUBLANES
    bm = n_sub * sub
    i = pl.program_id(0)
    n_used = meta_ref[META_N_USED]
    n_asg = slot_ref.shape[0]
    dump0 = EXPERT_TOPK * plane
    assert bm & (bm - 1) == 0

    tile = lambda row: pl.ds(pl.multiple_of(row * SUBLANES, SUBLANES), SUBLANES)

    def gather(blk, buf_slot):
        def sub_block(j, carry):
            base = blk * bm + j * sub
            for u in range(sub):
                pltpu.make_async_copy(h_ref.at[tile(src_tok[base + u]), :],
                                      xbuf.at[buf_slot, j, tile(u), :], gsem.at[buf_slot]).start(priority=u % 2)
            return carry
        lax.fori_loop(0, n_sub, sub_block, 0)

    def scatter(blk, buf_slot):
        def sub_block(j, carry):
            base = blk * bm + j * sub
            for u in range(sub):
                pltpu.make_async_copy(obuf.at[buf_slot, j, tile(u), :],
                                      y_ref.at[tile(dst_row[base + u]), :], ssem.at[buf_slot]).start(priority=u % 2)
            return carry
        lax.fori_loop(0, n_sub, sub_block, 0)

    def wait_block(buf, sem, buf_slot):
        pltpu.make_async_copy(buf.at[buf_slot], buf.at[buf_slot], sem.at[buf_slot]).wait()

    @pl.when(i == 0)
    def _():
        def clear_expert(e, carry):
            def clear(r, c):
                src_tok[r] = 0
                dst_row[r] = dump0 + jnp.bitwise_and(r, 2 * bm - 1)
                return c
            lax.fori_loop(meta_ref[META_PAD_LO + e], meta_ref[META_PAD_HI + e], clear, 0)
            return carry
        lax.fori_loop(0, N_EXPERTS, clear_expert, 0)

        def fill(t, carry):
            for u in range(TABLE_UNROLL):
                a = t * TABLE_UNROLL + u
                tok = jnp.right_shift(a, 1)
                src_tok[slot_ref[a]] = tok
                dst_row[slot_ref[a]] = jnp.bitwise_and(a, 1) * plane + tok
            return carry
        lax.fori_loop(0, meta_ref[META_FILL_TRIPS], fill, 0)
        gather(0, 0)
        n_tok = h_ref.shape[0] // SUBLANES
        tail = plane - n_tok
        assert tail >= 0
        obuf[1] = jnp.zeros(obuf.shape[1:], obuf.dtype)
        spans = [(dump0, 2 * bm)] + ([(k * plane + n_tok, tail) for k in range(EXPERT_TOPK)] if tail else [])
        copies = []
        for first, count in spans:
            for j in range(-(-count // sub)):
                rows = min(sub, count - j * sub)
                copies.append(pltpu.make_async_copy(
                    obuf.at[1, j % n_sub, pl.ds(0, rows * SUBLANES), :],
                    y_ref.at[pl.ds((first + j * sub) * SUBLANES, rows * SUBLANES), :], ssem.at[1]))
        for cp in copies:
            cp.start()
        for cp in copies:
            cp.wait()

    slot = jnp.bitwise_and(i, 1)

    @pl.when(i + 1 < n_used)
    def _():
        gather(i + 1, 1 - slot)

    @pl.when(i < n_used)
    def _():
        wait_block(xbuf, gsem, slot)

        @pl.when(i >= 2)
        def _():
            wait_block(obuf, ssem, slot)

        packed = jnp.concatenate([_tiles_to_rows(xbuf, (slot, j), sub) for j in range(n_sub)], axis=0)
        xb = _unpack_bf16_pairs(packed).astype(BF16)
        hg = jnp.dot(xb, wg_ref[0].astype(BF16), preferred_element_type=F32)
        hu = jnp.dot(xb, wu_ref[0].astype(BF16), preferred_element_type=F32)
        hb = hg * (1.0 / (1.0 + jnp.exp(-hg))) * hu
        yb = jnp.dot(hb.astype(BF16), wd_ref[0].astype(BF16), preferred_element_type=F32)
        yp = _pack_bf16_pairs(yb)
        for j in range(n_sub):
            _rows_to_tiles(obuf, (slot, j), yp[j * sub:(j + 1) * sub])
        scatter(i, slot)

    @pl.when(i == pl.num_programs(0) - 1)
    def _():
        @pl.when(n_used >= 2)
        def _():
            wait_block(obuf, ssem, jnp.bitwise_and(n_used, 1))
        wait_block(obuf, ssem, jnp.bitwise_and(n_used - 1, 1))


def _expert_ffn(h, slot, blk_e, meta, w_gate, w_up, w_down, plane):
    t = h.shape[0] // SUBLANES
    dp = SUBLANES * LANES
    d = 2 * dp
    assert w_gate.shape[1] == d
    n_asg = slot.shape[0]
    assert n_asg == t * EXPERT_TOPK and EXPERT_TOPK == 2
    n_blk = -(-(n_asg + N_EXPERTS * (MOE_BM - 1)) // MOE_BM)
    de = w_gate.shape[2]
    assert n_asg % TABLE_UNROLL == 0 and meta.shape == (META_PAD_HI + N_EXPERTS,)
    wspec = lambda shape: pl.BlockSpec((1,) + shape, lambda i, be, nu, sl: (be[i], 0, 0))
    return pl.pallas_call(
        functools.partial(_expert_kernel, plane=plane),
        out_shape=jax.ShapeDtypeStruct(((EXPERT_TOPK * plane + 2 * MOE_BM) * SUBLANES, LANES), jnp.uint32),
        grid_spec=pltpu.PrefetchScalarGridSpec(
            num_scalar_prefetch=3,
            grid=(n_blk,),
            in_specs=[pl.BlockSpec(memory_space=pl.ANY), wspec((d, de)), wspec((d, de)), wspec((de, d))],
            out_specs=pl.BlockSpec(memory_space=pl.ANY),
            scratch_shapes=[pltpu.SMEM((n_blk * MOE_BM,), jnp.int32), pltpu.SMEM((n_blk * MOE_BM,), jnp.int32),
                            pltpu.VMEM((2, MOE_BM // MOE_SUB, MOE_SUB * SUBLANES, LANES), jnp.uint32),
                            pltpu.VMEM((2, MOE_BM // MOE_SUB, MOE_SUB * SUBLANES, LANES), jnp.uint32),
                            pltpu.SemaphoreType.DMA((2,)), pltpu.SemaphoreType.DMA((2,))]),
        compiler_params=pltpu.CompilerParams(dimension_semantics=("arbitrary",), vmem_limit_bytes=VMEM_LIMIT),
        name="expert_ffn",
    )(blk_e, meta, slot, h, w_gate, w_up, w_down)


def _moe_ln_kernel(h_ref, y0_ref, y1_ref, route_ref, g_ref, b_ref, o_ref):
    rec = route_ref[...]
    tm = h_ref.shape[0]
    y = (DEEPNORM_ALPHA * h_ref[...]
         + rec[:, R_GATE:R_GATE + 1] * _unpack_bf16_pairs(_tiles_to_rows(y0_ref, (), tm))
         + rec[:, R_GATE + 1:R_GATE + 2] * _unpack_bf16_pairs(_tiles_to_rows(y1_ref, (), tm)))
    o_ref[...] = _layer_norm_rows(y, g_ref[...], b_ref[...])


def _moe_ln(h, y, route, ln_g, ln_b, row0, plane):
    n_rows, d = h.shape
    tm = min(WO_TM, n_rows)
    assert n_rows % tm == 0 and row0 % tm == 0 and plane % tm == 0 and EXPERT_TOPK == 2
    off = row0 // tm
    row = lambda w, o: pl.BlockSpec((tm, w), lambda i: (i + o, 0))
    tiles = lambda o: pl.BlockSpec((tm * SUBLANES, LANES), lambda i: (i + o, 0))
    vec = pl.BlockSpec((1, d), lambda i: (0, 0))
    return pl.pallas_call(
        _moe_ln_kernel,
        out_shape=jax.ShapeDtypeStruct((n_rows, d), F32),
        grid=(n_rows // tm,),
        in_specs=[row(d, 0), tiles(off), tiles(off + plane // tm), row(LANES, off), vec, vec],
        out_specs=row(d, 0),
        compiler_params=pltpu.CompilerParams(dimension_semantics=("arbitrary",), vmem_limit_bytes=VMEM_LIMIT),
        name="moe_ln2",
    )(h, y, y, route, ln_g.reshape(1, d), ln_b.reshape(1, d))


def _layer_norm(x, g, b):
    xf = x.astype(F32)
    mu = xf.mean(-1, keepdims=True)
    var = jnp.square(xf - mu).mean(-1, keepdims=True)
    return ((xf - mu) * lax.rsqrt(var + LN_EPS) * g + b).astype(x.dtype)


def _rope(x, pos, rot_dim, theta):
    half = rot_dim // 2
    inv = theta ** (-jnp.arange(0, rot_dim, 2, dtype=F32) / rot_dim)
    ang = pos[..., None].astype(F32) * inv
    cos = jnp.cos(ang)[:, :, None, :]
    sin = jnp.sin(ang)[:, :, None, :]
    xr = x[..., :rot_dim].astype(F32)
    x1, x2 = xr[..., :half], xr[..., half:]
    rot = jnp.concatenate([x1 * cos - x2 * sin, x2 * cos + x1 * sin], -1).astype(x.dtype)
    return jnp.concatenate([rot, x[..., rot_dim:]], -1)


def _heads(t, n):
    return t.reshape(t.shape[0], t.shape[1], n, HEAD_DIM)


def _chunk_retention(q, k, v, s0):
    n, l, h, d = q.shape
    c = RET_CHUNK if l % RET_CHUNK == 0 else l
    nc = l // c
    log_g = jnp.log1p(-jnp.exp2(-5.0 - jnp.arange(h, dtype=F32)))
    i = jnp.arange(c, dtype=F32)
    rel = i[:, None] - i[None, :]
    dmask = jnp.where(rel[None] >= 0, jnp.exp(jnp.maximum(rel[None], 0.0) * log_g[:, None, None]), 0.0)
    q_dec = jnp.exp((i + 1.0)[None] * log_g[:, None])[..., None]
    k_dec = jnp.exp((c - 1.0 - i)[None] * log_g[:, None])[..., None]
    c_dec = jnp.exp(c * log_g)[:, None, None]

    def to_chunks(t):
        return t.astype(F32).reshape(n, nc, c, h, d).transpose(1, 0, 3, 2, 4)

    def step(s, qkv):
        qc, kc, vc = qkv
        att = jnp.einsum('bhid,bhjd->bhij', qc, kc) * dmask
        o = jnp.einsum('bhij,bhjd->bhid', att, vc) + jnp.einsum('bhid,bhde->bhie', qc * q_dec, s)
        s = c_dec * s + jnp.einsum('bhjd,bhje->bhde', kc * k_dec, vc)
        return s, o

    s, o = lax.scan(step, s0.astype(F32), (to_chunks(q), to_chunks(k), to_chunks(v)))
    return o.transpose(1, 0, 3, 2, 4).reshape(n, l, h, d), s


def _retention_group(rq, rk, rv, rg, pos, s0, gn_g):
    q = _rope(_heads(rq, RET_HEADS), pos, HEAD_DIM, RET_ROPE_THETA)
    k = _rope(_heads(rk, RET_HEADS), pos, HEAD_DIM, RET_ROPE_THETA) * (HEAD_DIM ** -0.5)
    v = _heads(rv, RET_HEADS)
    o, s = _chunk_retention(q, k, v, s0)
    mu = o.mean(-1, keepdims=True)
    var = jnp.square(o - mu).mean(-1, keepdims=True)
    on = (o - mu) * lax.rsqrt(var + GN_EPS) * gn_g.reshape(RET_HEADS, HEAD_DIM).astype(F32)
    out = jax.nn.silu(rg.astype(F32)) * on.reshape(rg.shape)
    return out.astype(rq.dtype), s


def _gqa_attend(q, k, v, mask):
    n, lq, h, d = q.shape
    kv = k.shape[2]
    qg = q.reshape(n, lq, kv, h // kv, d)
    s = jnp.einsum('nqkgd,nskd->nkgqs', qg, k).astype(F32) * (d ** -0.5)
    m = mask[:, None, None]
    p = jax.nn.softmax(jnp.where(m, s, NEG), axis=-1) * m
    o = jnp.einsum('nkgqs,nskd->nqkgd', p.astype(v.dtype), v)
    return o.reshape(n, lq, h, d), p


def _nsa_heads(nq, ck, sk, wk, cv, sv, wv, pos):
    rp = lambda t, nh: _rope(_heads(t, nh), pos, ROT_DIM, ROPE_THETA)
    return (rp(nq, NSA_HEADS), rp(ck, NSA_KV_HEADS), rp(sk, NSA_KV_HEADS), rp(wk, NSA_KV_HEADS),
            _heads(cv, NSA_KV_HEADS), _heads(sv, NSA_KV_HEADS), _heads(wv, NSA_KV_HEADS))


def _compress(rows, w):
    n, t, kv, d = rows.shape
    return jnp.einsum('nbjkd,jd->nbkd', rows.reshape(n, t // CMP_BLOCK, CMP_BLOCK, kv, d), w)


def _cmp_branch(q, pos, kc, vc):
    nb = kc.shape[1]
    blk_end = (jnp.arange(nb) + 1) * CMP_BLOCK - 1
    mask = blk_end[None, None, :] <= pos[:, :, None]
    o, p = _gqa_attend(q, kc, vc, mask)
    imp = p.sum(axis=2).transpose(0, 2, 1, 3)
    return o, imp


def _select_blocks(imp, pos, n_sel):
    nb = imp.shape[-1]
    imp = jnp.pad(imp, ((0, 0), (0, 0), (0, 0), (0, n_sel - nb)))
    blk = jnp.arange(n_sel)
    cur = (pos // CMP_BLOCK)[:, :, None, None]
    forced = (blk == 0) | (blk == cur) | (blk == cur - 1)
    score = jnp.where(blk > cur, NEG, jnp.where(forced, -NEG, imp))
    top, idx = lax.top_k(score, min(SEL_TOPK, n_sel))
    return idx, top > 0.5 * NEG


def _sel_attend(q, pos, ks, vs, idx, valid):
    n, lq, kv, kk, cb, d = ks.shape
    h = q.shape[2]
    kpos = idx[..., None] * CMP_BLOCK + jnp.arange(CMP_BLOCK)
    m = ((kpos <= pos[:, :, None, None, None]) & valid[..., None]).reshape(n, lq, kv, 1, kk * cb)
    qg = q.reshape(n, lq, kv, h // kv, d)
    kf = ks.reshape(n, lq, kv, kk * cb, d)
    vf = vs.reshape(n, lq, kv, kk * cb, d)
    s = jnp.einsum('nqkgd,nqkjd->nqkgj', qg, kf).astype(F32) * (d ** -0.5)
    p = jax.nn.softmax(jnp.where(m, s, NEG), axis=-1) * m
    o = jnp.einsum('nqkgj,nqkjd->nqkgd', p.astype(vf.dtype), vf)
    return o.reshape(n, lq, h, d)


def _sel_prompt(q, pos, k, v, idx, valid):
    b, s, h, d = q.shape
    kv = k.shape[2]
    nb = s // CMP_BLOCK
    nq = s // SEL_Q_BLOCK
    kb = k.reshape(b, nb, CMP_BLOCK, kv, d).transpose(0, 3, 1, 2, 4)
    vb = v.reshape(b, nb, CMP_BLOCK, kv, d).transpose(0, 3, 1, 2, 4)
    bi = jnp.arange(b)[:, None, None, None]
    hi = jnp.arange(kv)[None, None, :, None]

    def blockwise(t):
        return t.reshape(t.shape[0], nq, SEL_Q_BLOCK, *t.shape[2:]).swapaxes(0, 1)

    def one(args):
        qc, pc, ic, vc = args
        return _sel_attend(qc, pc, kb[bi, hi, ic], vb[bi, hi, ic], ic, vc)

    o = lax.map(one, (blockwise(q), blockwise(pos), blockwise(idx), blockwise(valid)))
    return o.swapaxes(0, 1).reshape(b, s, h, d)


def _win_prompt(q, k, v):
    b, s, h, d = q.shape
    kv = k.shape[2]
    nb = s // WIN_Q_BLOCK
    nprev = WINDOW // WIN_Q_BLOCK
    nw = nprev + 1
    padw = ((0, 0), (WINDOW, 0), (0, 0), (0, 0))
    kp = jnp.pad(k, padw).reshape(b, nb + nprev, WIN_Q_BLOCK, kv, d)
    vp = jnp.pad(v, padw).reshape(b, nb + nprev, WIN_Q_BLOCK, kv, d)
    kw = jnp.concatenate([kp[:, i:i + nb] for i in range(nw)], axis=2)
    vw = jnp.concatenate([vp[:, i:i + nb] for i in range(nw)], axis=2)
    qpos = jnp.arange(s).reshape(nb, WIN_Q_BLOCK)
    kpos = (jnp.arange(nb)[:, None] - nprev) * WIN_Q_BLOCK + jnp.arange(nw * WIN_Q_BLOCK)[None]
    qq, kk = qpos[:, :, None], kpos[:, None, :]
    mask = (kk <= qq) & (qq - kk < WINDOW) & (kk >= 0)
    mask = jnp.broadcast_to(mask[None], (b,) + mask.shape).reshape(b * nb, WIN_Q_BLOCK, nw * WIN_Q_BLOCK)
    o, _ = _gqa_attend(q.reshape(b * nb, WIN_Q_BLOCK, h, d), kw.reshape(b * nb, nw * WIN_Q_BLOCK, kv, d),
                       vw.reshape(b * nb, nw * WIN_Q_BLOCK, kv, d), mask)
    return o.reshape(b, s, h, d)


def _gather_selected(pool, new_rows, page_table, idx):
    n, l, kv, d = new_rows.shape
    n_pages = page_table.shape[1]
    past_blocks = n_pages * PAGE_SIZE // CMP_BLOCK
    nbn = -(-l // CMP_BLOCK)
    newb = jnp.pad(new_rows, ((0, 0), (0, nbn * CMP_BLOCK - l), (0, 0), (0, 0)))
    newb = newb.reshape(n, nbn, CMP_BLOCK, kv, d).transpose(0, 3, 1, 2, 4)
    bi = jnp.arange(n)[:, None, None, None]
    hi = jnp.arange(kv)[None, None, :, None]
    start = idx * CMP_BLOCK
    phys = page_table[bi, jnp.minimum(start // PAGE_SIZE, n_pages - 1)]
    off = (start % PAGE_SIZE)[..., None] + jnp.arange(CMP_BLOCK)
    past = pool[phys[..., None], off, hi[..., None]]
    new = newb[bi, hi, jnp.clip(idx - past_blocks, 0, nbn - 1)]
    return jnp.where((idx < past_blocks)[..., None, None], past, new)


def _nsa_combine(gl, o_cmp, o_sel, o_win):
    n, l = gl.shape[0], gl.shape[1]
    g = jax.nn.sigmoid(gl.astype(F32)).reshape(n, l, NSA_HEADS, 3, 1)
    o = g[..., 0, :] * o_cmp + g[..., 1, :] * o_sel + g[..., 2, :] * o_win
    return o.reshape(n, l, NSA_W).astype(o_cmp.dtype)


def _prompt_mixer(x, win_buf, w_in, w_cmp_k, w_cmp_v, gn_g):
    n, s, _ = x.shape
    main, (ck, cv, sk, sv, wk, wv) = _project(x, w_in, jnp.arange(s))
    ret_out, s_fin = _retention_prompt(main, gn_g, n, s)
    kc, vc = _compress_prompt(main, w_cmp_k, w_cmp_v)
    nsa = _nsa_prompt(main, kc.reshape(n, s // CMP_BLOCK, KV_W), vc.reshape(n, s // CMP_BLOCK, KV_W), n, s)
    feats = (ret_out, nsa)
    if s >= win_buf:
        bk, bv = wk[:, s - win_buf:], wv[:, s - win_buf:]
    else:
        padb = ((0, 0), (win_buf - s, 0), (0, 0), (0, 0))
        bk, bv = jnp.pad(wk, padb), jnp.pad(wv, padb)
    return feats, (ck, cv, sk, sv, bk, bv, s_fin)


def _sample_mixer(x, c_cmp_k, c_cmp_v, c_sel_k, c_sel_v, c_win_k, c_win_v, s_ret, page_table,
                  w_in, w_cmp_k, w_cmp_v, gn_g):
    n, l, _ = x.shape
    past = page_table.shape[1] * PAGE_SIZE
    pos = past + jnp.arange(l)[None]
    assert l == 1
    main, (ck, cv, sk, sv, wk, wv) = _project(x, w_in, jnp.full((n,), past, jnp.int32))
    rq, rk, rv, rg, nq = _split_main(main, n, l)[:5]
    gl = _gate_logits(main, n, l)
    ret_out, s_new = _retention_group(rq, rk, rv, rg, pos, s_ret, gn_g)
    q = _heads(nq, NSA_HEADS)
    assert l == 1 and past % CMP_BLOCK == 0 and c_win_k.shape[1] <= WINDOW
    kc, vc = _compress_paged(c_cmp_k, c_cmp_v, page_table, w_cmp_k, w_cmp_v)
    o_cmp, sel = _sample_select(q[:, 0], kc, vc, past)
    sel = sel[:, :SEL_TOPK].reshape(-1)
    gl_pad = jnp.pad(gl.reshape(n, NSA_HEADS, 3), ((0, 0), (0, 0), (0, LANES - 3)))
    nsa = _sample_attend(sel, page_table, q[:, 0], sk[:, 0], sv[:, 0], wk[:, 0], wv[:, 0],
                         _interleaved(c_win_k), _interleaved(c_win_v), _interleaved(c_sel_k), _interleaved(c_sel_v),
                         o_cmp, gl_pad, past)
    feats = (ret_out.reshape(n * l, RET_W), nsa.reshape(n * l, NSA_W))
    kw = jnp.concatenate([c_win_k, wk], 1)
    vw = jnp.concatenate([c_win_v, wv], 1)
    return feats, (ck, cv, sk, sv, kw[:, l:], vw[:, l:], s_new)


def _route_params(w_group, b_group, w_expert, b_expert):
    w = jnp.concatenate([w_group, w_expert], axis=1)
    b = jnp.concatenate([b_group, b_expert], axis=0)
    pad = LANES - w.shape[1]
    return jnp.pad(w, ((0, 0), (0, pad))).astype(BF16), jnp.pad(b, (0, pad)).reshape(1, LANES)


def _dispatch_plan(route, counts_tile):
    counts = counts_tile[0, EXPERT_LANE0:EXPERT_LANE0 + N_EXPERTS].astype(jnp.int32)
    padded = (counts + MOE_BM - 1) // MOE_BM * MOE_BM
    pad_end = jnp.cumsum(padded)
    pad_start = pad_end - padded
    n_asg = route.shape[0] * EXPERT_TOPK
    n_blk = -(-(n_asg + N_EXPERTS * (MOE_BM - 1)) // MOE_BM)
    blk_first = jnp.arange(n_blk, dtype=jnp.int32) * MOE_BM
    blk_e = jnp.minimum(jnp.sum(pad_end[None, :] <= blk_first[:, None], axis=1), N_EXPERTS - 1).astype(jnp.int32)
    meta = jnp.concatenate([pad_end[-1:] // MOE_BM, jnp.array([n_asg // TABLE_UNROLL]), pad_start + counts,
                            pad_end]).astype(jnp.int32)
    eid = route[:, R_EID:R_EID + EXPERT_TOPK].astype(jnp.int32)
    rank = route[:, R_RANK:R_RANK + EXPERT_TOPK].astype(jnp.int32)
    start = jnp.sum(jnp.where(eid[..., None] == jnp.arange(N_EXPERTS), pad_start, 0), axis=-1)
    return (start + rank).reshape(-1).astype(jnp.int32), blk_e, meta


def kernel(x_prompt, x_sample, cache_cmp_k, cache_cmp_v, cache_sel_k, cache_sel_v, cache_win_k, cache_win_v,
           state_ret, page_table, w_in, w_cmp_k, w_cmp_v, ret_gn_g, w_o, ln1_g, ln1_b, w_group, b_group,
           w_expert, b_expert, w_gate, w_up, w_down, ln2_g, ln2_b):
    win_buf = cache_win_k.shape[2]
    hp, hs = x_prompt, x_sample
    acc_p = [[] for _ in range(7)]
    acc_s = [[] for _ in range(7)]
    for l in range(DEPTH):
        fp, st_p = _prompt_mixer(hp, win_buf, w_in[l], w_cmp_k[l], w_cmp_v[l], ret_gn_g[l])
        fs, st_s = _sample_mixer(hs, cache_cmp_k[l], cache_cmp_v[l], cache_sel_k[l], cache_sel_v[l],
                                 cache_win_k[l], cache_win_v[l], state_ret[l], page_table,
                                 w_in[l], w_cmp_k[l], w_cmp_v[l], ret_gn_g[l])
        w_o_bf16 = w_o[l].astype(BF16)
        w_route, b_route = _route_params(w_group[l], b_group[l], w_expert[l], b_expert[l])
        tp = hp.shape[0] * hp.shape[1]
        ts = hs.shape[0] * hs.shape[1]
        no_counts = jnp.zeros((SUBLANES, LANES), F32)
        h1p, packed_p, route_p, counts_p = _wo_ln_route(fp[0], fp[1], hp.reshape(tp, D_MODEL), w_o_bf16, ln1_g[l],
                                                        ln1_b[l], w_route, b_route, no_counts)
        h1s, packed_s, route_s, counts = _wo_ln_route(fs[0], fs[1], hs.reshape(ts, D_MODEL), w_o_bf16, ln1_g[l],
                                                      ln1_b[l], w_route, b_route, counts_p)
        packed = jnp.concatenate([packed_p, packed_s], axis=0)
        route = jnp.concatenate([route_p, route_s], axis=0)
        slot, blk_e, meta = _dispatch_plan(route, counts)
        plane = -(-(tp + ts) // WO_TM) * WO_TM
        y = _expert_ffn(packed, slot, blk_e, meta, w_gate[l], w_up[l], w_down[l], plane)
        hp = _moe_ln(h1p, y, route, ln2_g[l], ln2_b[l], 0, plane).reshape(hp.shape)
        hs = _moe_ln(h1s, y, route, ln2_g[l], ln2_b[l], tp, plane).reshape(hs.shape)
        for acc, t in zip(acc_p, st_p):
            acc.append(t)
        for acc, t in zip(acc_s, st_s):
            acc.append(t)
    p_cmp_k, p_cmp_v, p_sel_k, p_sel_v, p_win_k, p_win_v, p_ret = [jnp.stack(a) for a in acc_p]
    s_cmp_k, s_cmp_v, s_sel_k, s_sel_v, s_win_k, s_win_v, s_ret = [jnp.stack(a) for a in acc_s]
    return (hp, hs, p_cmp_k, p_cmp_v, p_sel_k, p_sel_v, p_win_k, p_win_v, p_ret.astype(state_ret.dtype),
            s_cmp_k, s_cmp_v, s_sel_k, s_sel_v, s_win_k, s_win_v, s_ret.astype(state_ret.dtype))
```

```python
import functools
import math

import jax
import jax.numpy as jnp
import numpy as np
from jax import lax
from jax.experimental import pallas as pl
from jax.experimental.pallas import tpu as pltpu

D_MODEL = 2048
DEPTH = 1
PAGE_SIZE = 128

F32 = jnp.float32
BF16 = jnp.bfloat16
HEAD_DIM = 128
RET_HEADS = D_MODEL // (2 * HEAD_DIM)
NSA_HEADS = D_MODEL // (2 * HEAD_DIM)
NSA_KV_HEADS = 2
RET_W = RET_HEADS * HEAD_DIM
NSA_W = NSA_HEADS * HEAD_DIM
KV_W = NSA_KV_HEADS * HEAD_DIM
MIX_W = RET_W + NSA_W
RET_CHUNK = 128
RET_ROPE_THETA = 10000.0
ROPE_THETA = 500000.0
ROT_DIM = HEAD_DIM // 4
CMP_BLOCK = 64
SEL_TOPK = 16
WINDOW = 512
WIN_Q_BLOCK = 128
SEL_Q_BLOCK = 64
N_GROUPS = 4
EXPERTS_PER_GROUP = 8
N_EXPERTS = N_GROUPS * EXPERTS_PER_GROUP
EXPERT_TOPK = 2
D_EXPERT = 512
MOE_BLOCK = 128
LN_EPS = 1e-5
GN_EPS = 1e-5
NEG = -1e30
DEEPNORM_ALPHA = (2 * DEPTH) ** 0.25
DEEPNORM_BETA = (8 * DEPTH) ** -0.25
SPLITS = (RET_W, RET_W, RET_W, RET_W, NSA_W, KV_W, KV_W, KV_W, KV_W, KV_W, KV_W, NSA_HEADS * 3)
IN_W = sum(SPLITS)
GATE_W = NSA_HEADS * 3
MAIN_W = IN_W - GATE_W
LANES = 128
VMEM_LIMIT = 48 * 1024 * 1024


PROJ_TN = 512
GL_COL0 = MAIN_W
PROJ_W = -(-(MAIN_W + NSA_KV_HEADS * LANES) // PROJ_TN) * PROJ_TN
ROTATED_SPLITS = (4, 5, 7, 9)


def _rotated_heads():
    cuts = np.cumsum((0,) + SPLITS)
    tiles = []
    for j in range(PROJ_W // PROJ_TN):
        heads = []
        for h in range(PROJ_TN // HEAD_DIM):
            c0 = j * PROJ_TN + h * HEAD_DIM
            split = int(np.searchsorted(cuts, c0, side="right")) - 1
            heads.append(split in ROTATED_SPLITS and c0 < MAIN_W)
        tiles.append(tuple(heads))
    return tiles


def _nsa_rope_tables(pos):
    half = ROT_DIM // 2
    inv = ROPE_THETA ** (-jnp.arange(0, ROT_DIM, 2, dtype=F32) / ROT_DIM)
    ang = pos[:, None].astype(F32) * inv
    cos, sin = jnp.cos(ang), jnp.sin(ang)
    rest = HEAD_DIM - ROT_DIM
    zeros = jnp.zeros((pos.shape[0], half), F32)
    pad = lambda t, fill: jnp.pad(t, ((0, 0), (0, rest)), constant_values=fill)
    return (pad(jnp.concatenate([cos, cos], 1), 1.0), pad(jnp.concatenate([zeros, sin], 1), 0.0),
            pad(jnp.concatenate([-sin, zeros], 1), 0.0))


KV_SPLITS = (5, 6, 7, 8, 9, 10)


def _kv_heads():
    cuts = np.cumsum((0,) + SPLITS)
    tiles = []
    for j in range(MAIN_W // PROJ_TN):
        heads = []
        for h in range(PROJ_TN // HEAD_DIM):
            c0 = j * PROJ_TN + h * HEAD_DIM
            split = int(np.searchsorted(cuts, c0, side="right")) - 1
            heads.append((KV_SPLITS.index(split), (c0 - int(cuts[split])) // HEAD_DIM) if split in KV_SPLITS else None)
        tiles.append(tuple(heads))
    return tiles


def _proj_kernel(x_ref, w_ref, wt_ref, cos_ref, up_ref, dn_ref, o_ref, *kv_refs, patterns, kv_heads):
    j = pl.program_id(1)
    n_main = MAIN_W // PROJ_TN
    xb = x_ref[...].astype(BF16)
    half = ROT_DIM // 2

    def rotated(acc, heads):
        cos, up, dn = cos_ref[...], up_ref[...], dn_ref[...]
        parts = []
        for h, rot in enumerate(heads):
            xh = acc[:, h * HEAD_DIM:(h + 1) * HEAD_DIM]
            if rot:
                xh = xh * cos + pltpu.roll(xh, half, 1) * up + pltpu.roll(xh, HEAD_DIM - half, 1) * dn
            parts.append(xh)
        return jnp.concatenate(parts, axis=1)

    plans = list(zip(patterns[:n_main], kv_heads))
    for plan in sorted(set(plans), key=repr):
        tiles = [t for t, p in enumerate(plans) if p == plan]
        hit = functools.reduce(jnp.logical_or, [j == t for t in tiles])

        @pl.when(hit)
        def _(plan=plan):
            heads, dests = plan
            acc = jnp.dot(xb, w_ref[...], preferred_element_type=F32)
            vals = rotated(acc, heads) if any(heads) else acc
            o_ref[...] = vals
            for h, dest in enumerate(dests):
                if dest is not None:
                    out, k = dest
                    kv_refs[out][pl.ds(k, vals.shape[0], stride=NSA_KV_HEADS), :] = (
                        vals[:, h * HEAD_DIM:(h + 1) * HEAD_DIM])

    @pl.when(j >= n_main)
    def _():
        o_ref[...] = jnp.dot(xb, wt_ref[...], preferred_element_type=F32)


def _project(x, w_in, pos):
    n, l, d = x.shape
    t = n * l
    xt = x.reshape(t, d)
    tm = min(1024, t)
    gpg = GATE_W // NSA_KV_HEADS
    n_main = MAIN_W // PROJ_TN
    assert PROJ_W == MAIN_W + PROJ_TN and not any(any(p) for p in _rotated_heads()[n_main:])
    gate_tiles = [jnp.pad(w_in[:, MAIN_W + k * gpg:MAIN_W + (k + 1) * gpg], ((0, 0), (0, LANES - gpg)))
                  for k in range(NSA_KV_HEADS)]
    fill = jnp.zeros((d, PROJ_TN - NSA_KV_HEADS * LANES), F32)
    w_tail = jnp.concatenate(gate_tiles + [fill], axis=1).astype(BF16)
    tables = _nsa_rope_tables(pos)
    pb = pos.shape[0] // tm
    tab = pl.BlockSpec((tm, HEAD_DIM), lambda i, j: (i % pb, 0))
    kv_rows = pl.BlockSpec((tm * NSA_KV_HEADS, HEAD_DIM), lambda i, j: (i, 0))
    outs = pl.pallas_call(
        functools.partial(_proj_kernel, patterns=_rotated_heads(), kv_heads=_kv_heads()),
        out_shape=[jax.ShapeDtypeStruct((t, PROJ_W), F32)]
        + [jax.ShapeDtypeStruct((t * NSA_KV_HEADS, HEAD_DIM), F32)] * len(KV_SPLITS),
        grid=(t // tm, PROJ_W // PROJ_TN),
        in_specs=[pl.BlockSpec((tm, d), lambda i, j: (i, 0)),
                  pl.BlockSpec((d, PROJ_TN), lambda i, j: (0, jnp.minimum(j, n_main - 1))),
                  pl.BlockSpec((d, PROJ_TN), lambda i, j: (0, 0)), tab, tab, tab],
        out_specs=[pl.BlockSpec((tm, PROJ_TN), lambda i, j: (i, j))] + [kv_rows] * len(KV_SPLITS),
        compiler_params=pltpu.CompilerParams(dimension_semantics=("arbitrary", "arbitrary"),
                                             vmem_limit_bytes=VMEM_LIMIT),
        name="in_proj",
    )(xt, w_in.astype(BF16), w_tail, *tables)
    return outs[0], [o.reshape(n, l, NSA_KV_HEADS, HEAD_DIM) for o in outs[1:]]


def _split_main(main, n, l, first=0):
    cuts = [0] + [int(c) for c in np.cumsum(SPLITS)[:-1]]
    return [main[:, cuts[i]:cuts[i + 1]].reshape(n, l, -1) for i in range(first, len(SPLITS) - 1)]


def _split_col0(i):
    return int(np.cumsum((0,) + SPLITS)[i])


def _gate_logits(main, n, l):
    gpg = GATE_W // NSA_KV_HEADS
    cols = [main[:, GL_COL0 + k * LANES:GL_COL0 + k * LANES + gpg] for k in range(NSA_KV_HEADS)]
    return jnp.concatenate(cols, axis=1).reshape(n, l, GATE_W)


def _compress_kernel(k_ref, v_ref, wk_ref, wv_ref, ko_ref, vo_ref):
    r = ko_ref.shape[0]
    ko_ref[...] = jnp.sum(k_ref[...].reshape(r, CMP_BLOCK, KV_W) * wk_ref[...][None], axis=1)
    vo_ref[...] = jnp.sum(v_ref[...].reshape(r, CMP_BLOCK, KV_W) * wv_ref[...][None], axis=1)


def _compress_prompt(main, w_cmp_k, w_cmp_v):
    t = main.shape[0]
    r = 32
    wk2 = jnp.tile(w_cmp_k, (1, NSA_KV_HEADS))
    wv2 = jnp.tile(w_cmp_v, (1, NSA_KV_HEADS))
    kcol, vcol = _split_col0(5) // KV_W, _split_col0(6) // KV_W
    wsp = pl.BlockSpec((CMP_BLOCK, KV_W), lambda i: (0, 0))
    osp = pl.BlockSpec((r, KV_W), lambda i: (i, 0))
    return pl.pallas_call(
        _compress_kernel,
        out_shape=[jax.ShapeDtypeStruct((t // CMP_BLOCK, KV_W), F32)] * 2,
        grid=(t // (r * CMP_BLOCK),),
        in_specs=[pl.BlockSpec((r * CMP_BLOCK, KV_W), lambda i: (i, kcol)),
                  pl.BlockSpec((r * CMP_BLOCK, KV_W), lambda i: (i, vcol)), wsp, wsp],
        out_specs=[osp, osp],
        compiler_params=pltpu.CompilerParams(dimension_semantics=("arbitrary",)),
        name="compress_prompt",
    )(main, main, wk2, wv2)


PAGES_PER_STEP = 64
BLOCKS_PER_PAGE = PAGE_SIZE // CMP_BLOCK


SUBLANES = 8
ROWS_PER_BLOCK = CMP_BLOCK * NSA_KV_HEADS
ROWS_PER_PAGE = PAGE_SIZE * NSA_KV_HEADS
BLOCKS_PER_TILE = SUBLANES // NSA_KV_HEADS


def _div(x, n):
    assert n & (n - 1) == 0
    return jnp.right_shift(x, n.bit_length() - 1)


def _mod(x, n):
    assert n & (n - 1) == 0
    return jnp.bitwise_and(x, n - 1)


def _interleaved(pool):
    return pool.reshape(pool.shape[:-3] + (pool.shape[-3] * NSA_KV_HEADS, HEAD_DIM))


def _compress_paged_kernel(pt_ref, pk_ref, pv_ref, wk_ref, wv_ref, ko_ref, vo_ref, kbuf, vbuf, sem):
    s, i = pl.program_id(0), pl.program_id(1)
    n_i = pl.num_programs(1)
    step = s * n_i + i
    last = pl.num_programs(0) * n_i - 1
    pair = _div(lax.broadcasted_iota(jnp.int32, (SUBLANES, HEAD_DIM), 0), NSA_KV_HEADS)

    def fetch(seq, blk, slot):
        for j in range(PAGES_PER_STEP):
            page = pt_ref[seq, blk * PAGES_PER_STEP + j]
            pltpu.make_async_copy(pk_ref.at[page], kbuf.at[slot, j], sem.at[slot]).start(priority=j % 2)
            pltpu.make_async_copy(pv_ref.at[page], vbuf.at[slot, j], sem.at[slot]).start(priority=(j + 1) % 2)

    @pl.when(step == 0)
    def _():
        fetch(0, 0, 0)

    slot = jnp.bitwise_and(step, 1)

    @pl.when(step < last)
    def _():
        wrap = i == n_i - 1
        fetch(jnp.where(wrap, s + 1, s), jnp.where(wrap, 0, i + 1), 1 - slot)

    pltpu.make_async_copy(kbuf.at[slot], kbuf.at[slot], sem.at[slot]).wait()
    pltpu.make_async_copy(vbuf.at[slot], vbuf.at[slot], sem.at[slot]).wait()

    def summaries(buf, w):
        sums = []
        for j in range(PAGES_PER_STEP):
            for b in range(BLOCKS_PER_PAGE):
                y = buf[slot, j, b * ROWS_PER_BLOCK:(b + 1) * ROWS_PER_BLOCK, :] * w
                acc = jnp.sum(y.reshape(ROWS_PER_BLOCK // SUBLANES, SUBLANES, HEAD_DIM), axis=0)
                shift = SUBLANES // 2
                while shift >= NSA_KV_HEADS:
                    acc = acc + pltpu.roll(acc, shift, 0)
                    shift //= 2
                sums.append(acc)
        tiles = []
        for t in range(len(sums) // BLOCKS_PER_TILE):
            tile = sums[t * BLOCKS_PER_TILE]
            for j in range(1, BLOCKS_PER_TILE):
                tile = jnp.where(pair == j, sums[t * BLOCKS_PER_TILE + j], tile)
            tiles.append(tile)
        return jnp.concatenate(tiles, axis=0)

    ko_ref[0] = summaries(kbuf, wk_ref[...])
    vo_ref[0] = summaries(vbuf, wv_ref[...])


def _compress_paged(pool_k, pool_v, page_table, w_cmp_k, w_cmp_v):
    n, n_pages = page_table.shape
    wk2 = jnp.repeat(w_cmp_k, NSA_KV_HEADS, axis=0)
    wv2 = jnp.repeat(w_cmp_v, NSA_KV_HEADS, axis=0)
    anywhere = pl.BlockSpec(memory_space=pl.ANY)
    wsp = pl.BlockSpec((ROWS_PER_BLOCK, HEAD_DIM), lambda s, i, pt: (0, 0))
    rows = PAGES_PER_STEP * BLOCKS_PER_PAGE * NSA_KV_HEADS
    osp = pl.BlockSpec((1, rows, HEAD_DIM), lambda s, i, pt: (s, i, 0))
    page_buf = pltpu.VMEM((2, PAGES_PER_STEP, ROWS_PER_PAGE, HEAD_DIM), F32)
    return pl.pallas_call(
        _compress_paged_kernel,
        out_shape=[jax.ShapeDtypeStruct((n, n_pages * BLOCKS_PER_PAGE * NSA_KV_HEADS, HEAD_DIM), F32)] * 2,
        grid_spec=pltpu.PrefetchScalarGridSpec(
            num_scalar_prefetch=1,
            grid=(n, n_pages // PAGES_PER_STEP),
            in_specs=[anywhere, anywhere, wsp, wsp],
            out_specs=[osp, osp],
            scratch_shapes=[page_buf, page_buf, pltpu.SemaphoreType.DMA((2,))]),
        compiler_params=pltpu.CompilerParams(dimension_semantics=("arbitrary", "arbitrary"),
                                             vmem_limit_bytes=VMEM_LIMIT),
        name="compress_paged",
    )(page_table, _interleaved(pool_k), _interleaved(pool_v), wk2, wv2)


SS_SEQ = 8
LOWEST = -3.0e38


def _sample_select_kernel(q_ref, kc_ref, vc_ref, ocmp_ref, sel_ref, *, pos):
    ss, nbk = q_ref.shape[0], kc_ref.shape[1]
    kv, g = NSA_KV_HEADS, NSA_GROUP
    head = lax.broadcasted_iota(jnp.int32, (NSA_HEADS, nbk), 0)
    col = lax.broadcasted_iota(jnp.int32, (NSA_HEADS, nbk), 1)
    m = (_mod(col, kv) == _div(head, g)) & ((_div(col, kv) + 1) * CMP_BLOCK - 1 <= pos)
    width = nbk + LANES
    rows = []
    for i in range(ss):
        q = (q_ref[i] * (HEAD_DIM ** -0.5)).astype(BF16)
        p = _masked_softmax(_dot_nt(q, kc_ref[i].astype(BF16)), m)
        ocmp_ref[i] = jnp.dot(p.astype(BF16), vc_ref[i].astype(BF16), preferred_element_type=F32)
        for k in range(kv):
            imp = jnp.sum(p[k * g:(k + 1) * g], axis=0, keepdims=True)
            rows.append(jnp.concatenate([imp, jnp.zeros((1, LANES), F32)], axis=1))
    nrow = ss * kv
    r_iota = lax.broadcasted_iota(jnp.int32, (nrow, width), 0)
    ccol = lax.broadcasted_iota(jnp.int32, (nrow, width), 1)
    cand = jnp.zeros((nrow, width), F32)
    for r, row in enumerate(rows):
        cand = jnp.where(r_iota == r, row, cand)
    cblk = _div(ccol, kv)
    cur = pos // CMP_BLOCK
    n_sel = -(-(pos + 1) // CMP_BLOCK)
    forced = (cblk == 0) | (cblk == cur) | (cblk == cur - 1)
    score = jnp.where(cblk > cur, NEG, jnp.where(forced, -NEG, cand))
    score = jnp.where((_mod(ccol, kv) == _mod(r_iota, kv)) & (cblk < n_sel), score, LOWEST)
    colf = ccol.astype(F32)
    lane = lax.broadcasted_iota(jnp.int32, (nrow, LANES), 1)
    sel = jnp.zeros((nrow, LANES), jnp.int32)
    for t in range(SEL_TOPK):
        mx = jnp.max(score, axis=1, keepdims=True)
        c = jnp.min(jnp.where(score == mx, colf, -LOWEST), axis=1, keepdims=True)
        picked = jnp.where(mx > 0.5 * NEG, _div(c.astype(jnp.int32), kv), -1)
        sel = jnp.where(lane == t, picked, sel)
        score = jnp.where(colf == c, LOWEST, score)
    sel_ref[...] = sel


def _sample_select(q, kc, vc, pos):
    n = q.shape[0]
    nbk = kc.shape[1]
    qsp = pl.BlockSpec((SS_SEQ, NSA_HEADS, HEAD_DIM), lambda i: (i, 0, 0))
    csp = pl.BlockSpec((SS_SEQ, nbk, HEAD_DIM), lambda i: (i, 0, 0))
    return pl.pallas_call(
        functools.partial(_sample_select_kernel, pos=pos),
        out_shape=[jax.ShapeDtypeStruct((n, NSA_HEADS, HEAD_DIM), F32),
                   jax.ShapeDtypeStruct((n * NSA_KV_HEADS, LANES), jnp.int32)],
        grid=(n // SS_SEQ,),
        in_specs=[qsp, csp, csp],
        out_specs=[qsp, pl.BlockSpec((SS_SEQ * NSA_KV_HEADS, LANES), lambda i: (i, 0))],
        compiler_params=pltpu.CompilerParams(dimension_semantics=("arbitrary",), vmem_limit_bytes=VMEM_LIMIT),
        name="sample_select",
    )(q, kc, vc)


N_SLOTS = NSA_KV_HEADS * SEL_TOPK


def _sample_attend_kernel(sel_ref, pt_ref, q_ref, knew_ref, vnew_ref, wknew_ref, wvnew_ref, wkb_ref, wvb_ref,
                          ocmp_ref, gl_ref, *rest, pos, past_blocks):
    del pt_ref
    kblk, vblk, o_ref = rest[:N_SLOTS], rest[N_SLOTS:2 * N_SLOTS], rest[2 * N_SLOTS]
    kv, g, rb = NSA_KV_HEADS, NSA_GROUP, ROWS_PER_BLOCK
    n = pl.program_id(0)
    q = q_ref[0] * (HEAD_DIM ** -0.5)
    row = lax.broadcasted_iota(jnp.int32, (rb, HEAD_DIM), 0)
    col = lax.broadcasted_iota(jnp.int32, (1, rb), 1)

    def new_block(ref):
        out = jnp.zeros((rb, HEAD_DIM), F32)
        for k in range(kv):
            out = jnp.where(row == k, ref[0, k:k + 1, :], out)
        return out

    def per_head_rows(ref):
        return jnp.concatenate([jnp.broadcast_to(ref[0, k:k + 1, :], (g, HEAD_DIM)) for k in range(kv)], axis=0)

    knew, vnew = new_block(knew_ref), new_block(vnew_ref)
    o_sel = []
    for k in range(kv):
        ks, vs, ms = [], [], []
        for j in range(SEL_TOPK):
            b = sel_ref[(n * kv + k) * SEL_TOPK + j]
            is_new = jnp.broadcast_to(b, (rb, HEAD_DIM)) >= past_blocks
            ks.append(jnp.where(is_new, knew, kblk[k * SEL_TOPK + j][0]))
            vs.append(jnp.where(is_new, vnew, vblk[k * SEL_TOPK + j][0]))
            first = jnp.where(b >= 0, b, 1 << 24) * CMP_BLOCK
            ms.append((first + _div(col, kv) <= pos) & (_mod(col, kv) == k))
        s = _dot_nt(q[k * g:(k + 1) * g].astype(BF16), jnp.concatenate(ks, axis=0).astype(BF16))
        p = _masked_softmax(s, jnp.concatenate(ms, axis=1))
        o_sel.append(jnp.dot(p.astype(BF16), jnp.concatenate(vs, axis=0).astype(BF16), preferred_element_type=F32))
    o_sel = jnp.concatenate(o_sel, axis=0)

    nwr = wkb_ref.shape[1]
    s_w = _dot_nt(q.astype(BF16), wkb_ref[0].astype(BF16))
    cw = lax.broadcasted_iota(jnp.int32, (NSA_HEADS, nwr), 1)
    hw = lax.broadcasted_iota(jnp.int32, (NSA_HEADS, nwr), 0)
    kpos = pos - nwr // kv + _div(cw, kv)
    mw = (kpos <= pos) & (pos - kpos < WINDOW) & (_mod(cw, kv) == _div(hw, g))
    s_n = jnp.sum(q * per_head_rows(wknew_ref), axis=1, keepdims=True)
    smw = jnp.where(mw, s_w, NEG)
    mx = jnp.maximum(jnp.max(smw, axis=1, keepdims=True), s_n)
    e_w = jnp.where(mw, jnp.exp(smw - mx), 0.0)
    e_n = jnp.exp(s_n - mx)
    den = jnp.sum(e_w, axis=1, keepdims=True) + e_n
    o_win = (jnp.dot(e_w.astype(BF16), wvb_ref[0].astype(BF16), preferred_element_type=F32)
             + e_n * per_head_rows(wvnew_ref)) / den

    gate = 1.0 / (1.0 + jnp.exp(-gl_ref[0]))
    o_ref[0] = gate[:, 0:1] * ocmp_ref[0] + gate[:, 1:2] * o_sel + gate[:, 2:3] * o_win


def _sample_attend(sel, page_table, q, sk, sv, wk, wv, win_k, win_v, pool_k, pool_v, o_cmp, gl, pos):
    n, n_pages = page_table.shape
    past_blocks = n_pages * BLOCKS_PER_PAGE

    def slot_spec(k, j):
        def imap(s, sel_r, pt_r):
            b = jnp.maximum(sel_r[(s * NSA_KV_HEADS + k) * SEL_TOPK + j], 0)
            page = jnp.minimum(_div(b, BLOCKS_PER_PAGE), n_pages - 1)
            return (pt_r[s * n_pages + page], _mod(b, BLOCKS_PER_PAGE), 0)
        return pl.BlockSpec((1, ROWS_PER_BLOCK, HEAD_DIM), imap)

    slots = [slot_spec(k, j) for k in range(NSA_KV_HEADS) for j in range(SEL_TOPK)]
    per_seq = lambda a: pl.BlockSpec((1,) + a.shape[1:], lambda s, sel_r, pt_r: (s, 0, 0))
    dense = [q, sk, sv, wk, wv, win_k, win_v, o_cmp, gl]
    return pl.pallas_call(
        functools.partial(_sample_attend_kernel, pos=pos, past_blocks=past_blocks),
        out_shape=jax.ShapeDtypeStruct((n, NSA_HEADS, HEAD_DIM), F32),
        grid_spec=pltpu.PrefetchScalarGridSpec(
            num_scalar_prefetch=2,
            grid=(n,),
            in_specs=[per_seq(a) for a in dense] + slots + slots,
            out_specs=pl.BlockSpec((1, NSA_HEADS, HEAD_DIM), lambda s, sel_r, pt_r: (s, 0, 0))),
        compiler_params=pltpu.CompilerParams(dimension_semantics=("arbitrary",), vmem_limit_bytes=VMEM_LIMIT),
        name="sample_attend",
    )(sel, page_table.reshape(-1), *dense, *([pool_k] * N_SLOTS), *([pool_v] * N_SLOTS))


def _retention_tables(seq):
    c = RET_CHUNK
    log_g = jnp.log1p(-jnp.exp2(-5.0 - jnp.arange(RET_HEADS, dtype=F32)))
    i = jnp.arange(c, dtype=F32)
    rel = i[:, None] - i[None, :]
    dmask = jnp.where(rel[None] >= 0, jnp.exp(jnp.maximum(rel[None], 0.0) * log_g[:, None, None]), 0.0)
    q_dec = jnp.exp((i + 1.0)[None] * log_g[:, None])[..., None]
    k_dec = jnp.exp((c - 1.0 - i)[None] * log_g[:, None])[..., None]
    c_dec = jnp.exp(c * log_g)[:, None, None]
    bc = lambda t: jnp.broadcast_to(t, (RET_HEADS, c, HEAD_DIM))
    inv = RET_ROPE_THETA ** (-jnp.arange(0, HEAD_DIM, 2, dtype=F32) / HEAD_DIM)
    ang = jnp.arange(seq)[:, None].astype(F32) * inv
    cos, sin = jnp.cos(ang), jnp.sin(ang)
    return (dmask, bc(q_dec), bc(k_dec), jnp.broadcast_to(c_dec, (RET_HEADS, 1, HEAD_DIM)),
            jnp.concatenate([cos, cos], -1), jnp.concatenate([-sin, sin], -1))


def _retention_kernel(q_ref, k_ref, v_ref, g_ref, cos_ref, sin_ref, dmask_ref, qdec_ref, kdec_ref, cdec_ref, gn_ref,
                      o_ref, st_ref, s_scr):
    c = pl.program_id(1)

    @pl.when(c == 0)
    def _():
        s_scr[...] = jnp.zeros(s_scr.shape, F32)

    cosf, sins = cos_ref[...], sin_ref[...]
    half = HEAD_DIM // 2
    for h in range(RET_HEADS):
        sl = slice(h * HEAD_DIM, (h + 1) * HEAD_DIM)
        qh, kh = q_ref[:, sl], k_ref[:, sl]
        qr = qh * cosf + pltpu.roll(qh, half, 1) * sins
        kr = (kh * cosf + pltpu.roll(kh, half, 1) * sins) * (HEAD_DIM ** -0.5)
        vb = v_ref[:, sl].astype(BF16)
        att = _dot_nt(qr.astype(BF16), kr.astype(BF16)) * dmask_ref[h]
        s_prev = s_scr[h]
        o = (jnp.dot(att.astype(BF16), vb, preferred_element_type=F32)
             + jnp.dot((qr * qdec_ref[h]).astype(BF16), s_prev.astype(BF16), preferred_element_type=F32))
        s_scr[h] = cdec_ref[h] * s_prev + lax.dot_general(
            (kr * kdec_ref[h]).astype(BF16), vb, (((0,), (0,)), ((), ())), preferred_element_type=F32)
        mu = jnp.mean(o, axis=-1, keepdims=True)
        var = jnp.mean(jnp.square(o - mu), axis=-1, keepdims=True)
        on = (o - mu) * lax.rsqrt(var + GN_EPS) * gn_ref[:, sl]
        gg = g_ref[:, sl]
        o_ref[:, sl] = gg * (1.0 / (1.0 + jnp.exp(-gg))) * on

    @pl.when(c == pl.num_programs(1) - 1)
    def _():
        st_ref[0] = s_scr[...]


def _retention_prompt(main, gn_g, batch, seq):
    nc = seq // RET_CHUNK
    dmask, q_dec, k_dec, c_dec, cosf, sins = _retention_tables(seq)
    col = lambda j: pl.BlockSpec((RET_CHUNK, RET_W), lambda b, c: (b * nc + c, j))
    pos_tab = pl.BlockSpec((RET_CHUNK, HEAD_DIM), lambda b, c: (c, 0))
    full = lambda a: pl.BlockSpec(a.shape, lambda b, c: (0,) * a.ndim)
    gn = gn_g.reshape(1, RET_W)
    return pl.pallas_call(
        _retention_kernel,
        out_shape=[jax.ShapeDtypeStruct((batch * seq, RET_W), F32),
                   jax.ShapeDtypeStruct((batch, RET_HEADS, HEAD_DIM, HEAD_DIM), F32)],
        grid=(batch, nc),
        in_specs=[col(0), col(1), col(2), col(3), pos_tab, pos_tab,
                  full(dmask), full(q_dec), full(k_dec), full(c_dec), full(gn)],
        out_specs=[pl.BlockSpec((RET_CHUNK, RET_W), lambda b, c: (b * nc + c, 0)),
                   pl.BlockSpec((1, RET_HEADS, HEAD_DIM, HEAD_DIM), lambda b, c: (b, 0, 0, 0))],
        scratch_shapes=[pltpu.VMEM((RET_HEADS, HEAD_DIM, HEAD_DIM), F32)],
        compiler_params=pltpu.CompilerParams(dimension_semantics=("arbitrary", "arbitrary"),
                                             vmem_limit_bytes=VMEM_LIMIT),
        name="retention_prompt",
    )(main, main, main, main, cosf, sins, dmask, q_dec, k_dec, c_dec, gn)


NSA_TQ = 128
NSA_KEY_CHUNK = 512
NSA_GROUP = NSA_HEADS // NSA_KV_HEADS


def _dot_nt(a, b):
    return lax.dot_general(a, b, (((1,), (1,)), ((), ())), preferred_element_type=F32)


def _masked_softmax(s, m):
    sm = jnp.where(m, s, NEG)
    e = jnp.exp(sm - jnp.max(sm, axis=-1, keepdims=True))
    return jnp.where(m, e / jnp.sum(e, axis=-1, keepdims=True), 0.0)


def _select_mask_t(imp_t, pos_t):
    nb = imp_t.shape[0]
    blk = lax.broadcasted_iota(jnp.int32, imp_t.shape, 0)
    cur = jnp.right_shift(pos_t, int(math.log2(CMP_BLOCK)))
    forced = (blk == 0) | (blk == cur) | (blk == cur - 1)
    score = jnp.where(blk > cur, NEG, jnp.where(forced, -NEG, imp_t))
    rank = jnp.zeros(score.shape, jnp.int32)
    for i in range(nb):
        si = score[i:i + 1, :]
        ahead = (si > score) | ((si == score) & (blk > i))
        rank = rank + ahead.astype(jnp.int32)
    return (rank < SEL_TOPK) & (score > 0.5 * NEG)


def _nsa_prompt_kernel(q_ref, kc_ref, vc_ref, sk_ref, sv_ref, wk_ref, wv_ref, gl_ref, o_ref, skt, wkt):
    tq, g, kc_n = NSA_TQ, NSA_GROUP, NSA_KEY_CHUNK
    qi = pl.program_id(2)
    nb = kc_ref.shape[1]

    @pl.when(qi == 0)
    def _():
        blk_row = lax.broadcasted_iota(jnp.int32, (nb, kc_n), 0)
        key_col = lax.broadcasted_iota(jnp.int32, (nb, kc_n), 1)
        for c in range(skt.shape[0]):
            member = jnp.right_shift(key_col + c * kc_n, int(math.log2(CMP_BLOCK))) == blk_row
            skt[c] = jnp.concatenate([sk_ref[c * kc_n:(c + 1) * kc_n, :].T.astype(BF16),
                                      jnp.where(member, 1.0, 0.0).astype(BF16)], axis=0)
        for w in range(wkt.shape[0]):
            wkt[w] = wk_ref[w * tq:(w + 1) * tq, :].T.astype(BF16)

    q4 = q_ref[...] * (HEAD_DIM ** -0.5)
    qs = jnp.concatenate([q4[:, i * HEAD_DIM:(i + 1) * HEAD_DIM] for i in range(g)], axis=0).astype(BF16)
    pos = qi * tq + lax.broadcasted_iota(jnp.int32, (tq, 1), 0)
    pos4 = jnp.concatenate([pos] * g, axis=0)

    assert tq == LANES
    s_c = _dot_nt(kc_ref[0].astype(BF16), qs)
    blk_t = lax.broadcasted_iota(jnp.int32, (nb, g * tq), 0)
    pos_t = qi * tq + lax.broadcasted_iota(jnp.int32, (1, tq), 1)
    m_c = (blk_t + 1) * CMP_BLOCK - 1 <= jnp.concatenate([pos_t] * g, axis=1)
    sm_c = jnp.where(m_c, s_c, NEG)
    e_c = jnp.exp(sm_c - jnp.max(sm_c, axis=0, keepdims=True))
    p_c = jnp.where(m_c, e_c / jnp.sum(e_c, axis=0, keepdims=True), 0.0)
    o_cmp = lax.dot_general(p_c.astype(BF16), vc_ref[0].astype(BF16), (((0,), (0,)), ((), ())),
                            preferred_element_type=F32)
    imp_t = p_c[:, 0:tq]
    for i in range(1, g):
        imp_t = imp_t + p_c[:, i * tq:(i + 1) * tq]

    seln_t = jnp.where(_select_mask_t(imp_t, pos_t), 0.0, NEG)
    seln = jnp.concatenate([seln_t, jnp.zeros((tq - nb, tq), F32)], axis=0).T[:, :nb].astype(BF16)
    q_aug = jnp.concatenate([qs, jnp.concatenate([seln] * g, axis=0)], axis=1)
    key_row = lax.broadcasted_iota(jnp.int32, (1, kc_n), 1)

    def scores(c):
        return jnp.dot(q_aug, skt[c], preferred_element_type=F32)

    def values(c):
        return sv_ref[pl.ds(pl.multiple_of(c * kc_n, kc_n), kc_n), :].astype(BF16)

    assert kc_n % tq == 0
    c_diag = _div(qi, kc_n // tq)
    s_d = jnp.where(key_row + c_diag * kc_n <= pos4, scores(c_diag), NEG)
    m_d = jnp.max(s_d, axis=-1, keepdims=True)
    p_d = jnp.exp(s_d - m_d)
    first = (m_d, jnp.sum(p_d, axis=-1, keepdims=True),
             jnp.dot(p_d.astype(BF16), values(c_diag), preferred_element_type=F32))

    def chunk(c, carry):
        m_i, l_i, acc = carry
        s = scores(c)
        m_new = jnp.maximum(m_i, jnp.max(s, axis=-1, keepdims=True))
        alpha = jnp.exp(m_i - m_new)
        p = jnp.exp(s - m_new)
        l_new = alpha * l_i + jnp.sum(p, axis=-1, keepdims=True)
        return m_new, l_new, alpha * acc + jnp.dot(p.astype(BF16), values(c), preferred_element_type=F32)

    _, l_f, acc_f = lax.fori_loop(0, c_diag, chunk, first)
    o_sel = acc_f / l_f

    nwk = WINDOW + tq
    kstart = pl.multiple_of(jnp.maximum(qi * tq - WINDOW, 0), tq)
    kpos = kstart + lax.broadcasted_iota(jnp.int32, (1, nwk), 1)
    band = jnp.where((kpos <= pos) & (pos - kpos < WINDOW), 0.0, NEG)
    w0 = jnp.maximum(qi - WINDOW // tq, 0)
    kwt = jnp.concatenate([wkt[w0 + m] for m in range(nwk // tq)], axis=1)
    s_w = jnp.dot(qs, kwt, preferred_element_type=F32) + jnp.concatenate([band] * g, axis=0)
    e_w = jnp.exp(s_w - jnp.max(s_w, axis=-1, keepdims=True))
    o_win = (jnp.dot(e_w.astype(BF16), wv_ref[pl.ds(kstart, nwk), :].astype(BF16), preferred_element_type=F32)
             / jnp.sum(e_w, axis=-1, keepdims=True))

    gate = 1.0 / (1.0 + jnp.exp(-gl_ref[...]))
    for i in range(g):
        rows = slice(i * tq, (i + 1) * tq)
        o_ref[:, i * HEAD_DIM:(i + 1) * HEAD_DIM] = (gate[:, 3 * i:3 * i + 1] * o_cmp[rows]
                                                     + gate[:, 3 * i + 1:3 * i + 2] * o_sel[rows]
                                                     + gate[:, 3 * i + 2:3 * i + 3] * o_win[rows])


def _nsa_prompt(main, kc, vc, batch, seq):
    nq = seq // NSA_TQ
    gw = NSA_GROUP * HEAD_DIM
    qcol, glcol = _split_col0(4) // gw, GL_COL0 // LANES
    seq_spec = lambda split: pl.BlockSpec((seq, HEAD_DIM),
                                          lambda b, k, i: (b, _split_col0(split) // HEAD_DIM + k))
    cmp_spec = pl.BlockSpec((1, seq // CMP_BLOCK, HEAD_DIM), lambda b, k, i: (b, 0, k))
    return pl.pallas_call(
        _nsa_prompt_kernel,
        out_shape=jax.ShapeDtypeStruct((batch * seq, NSA_W), F32),
        grid=(batch, NSA_KV_HEADS, nq),
        in_specs=[pl.BlockSpec((NSA_TQ, gw), lambda b, k, i: (b * nq + i, qcol + k)), cmp_spec, cmp_spec,
                  seq_spec(7), seq_spec(8), seq_spec(9), seq_spec(10),
                  pl.BlockSpec((NSA_TQ, LANES), lambda b, k, i: (b * nq + i, glcol + k))],
        out_specs=pl.BlockSpec((NSA_TQ, gw), lambda b, k, i: (b * nq + i, k)),
        scratch_shapes=[pltpu.VMEM((seq // NSA_KEY_CHUNK, HEAD_DIM + seq // CMP_BLOCK, NSA_KEY_CHUNK), BF16),
                        pltpu.VMEM((seq // NSA_TQ, HEAD_DIM, NSA_TQ), BF16)],
        compiler_params=pltpu.CompilerParams(dimension_semantics=("arbitrary",) * 3,
                                             vmem_limit_bytes=VMEM_LIMIT),
        name="nsa_prompt",
    )(main, kc, vc, main, main, main, main, main)


WO_TM = 512


def _layer_norm_rows(y, g, b):
    mu = jnp.mean(y, axis=-1, keepdims=True)
    var = jnp.mean(jnp.square(y - mu), axis=-1, keepdims=True)
    return (y - mu) * lax.rsqrt(var + LN_EPS) * g + b


EXPERT_LANE0 = N_GROUPS
R_EID, R_RANK, R_GATE = 0, EXPERT_TOPK, 2 * EXPERT_TOPK


def _route(h, wr_ref, br_ref, carry):
    tm = h.shape[0]
    logit = jnp.dot(h.astype(BF16), wr_ref[...], preferred_element_type=F32) + br_ref[...]
    lane = lax.broadcasted_iota(jnp.int32, (tm, LANES), 1)
    lanef = lane.astype(F32)
    first_lane = lambda hit: jnp.min(jnp.where(hit, lanef, float(LANES)), axis=1, keepdims=True)
    is_g = lane < N_GROUPS
    gl = jnp.where(is_g, logit, LOWEST)
    gmx = jnp.max(gl, axis=1, keepdims=True)
    grp = first_lane(gl == gmx)
    p_grp = 1.0 / jnp.sum(jnp.where(is_g, jnp.exp(gl - gmx), 0.0), axis=1, keepdims=True)
    lane_grp = jnp.right_shift(lane - EXPERT_LANE0, int(math.log2(EXPERTS_PER_GROUP)))
    in_grp = lane_grp.astype(F32) == grp
    el = jnp.where(in_grp, logit, LOWEST)
    ee = jnp.where(in_grp, jnp.exp(el - jnp.max(el, axis=1, keepdims=True)), 0.0)
    pe = jnp.where(in_grp, ee / jnp.sum(ee, axis=1, keepdims=True), -1.0)
    p1 = jnp.max(pe, axis=1, keepdims=True)
    l1 = first_lane(pe == p1)
    pe2 = jnp.where(lanef == l1, -1.0, pe)
    p2 = jnp.max(pe2, axis=1, keepdims=True)
    l2 = first_lane(pe2 == p2)
    den = p1 + p2
    o1, o2 = lanef == l1, lanef == l2
    onehot = jnp.where(o1 | o2, 1.0, 0.0)
    r = lax.broadcasted_iota(jnp.int32, (tm, tm), 0)
    c = lax.broadcasted_iota(jnp.int32, (tm, tm), 1)
    earlier = jnp.where(c < r, 1.0, 0.0).astype(BF16)
    prefix = jnp.dot(earlier, onehot.astype(BF16), preferred_element_type=F32) + carry[0:1, :]
    rank1 = jnp.sum(jnp.where(o1, prefix, 0.0), axis=1, keepdims=True)
    rank2 = jnp.sum(jnp.where(o2, prefix, 0.0), axis=1, keepdims=True)
    carry[0:1, :] = carry[0:1, :] + jnp.sum(onehot, axis=0, keepdims=True)
    fields = [l1 - EXPERT_LANE0, l2 - EXPERT_LANE0, rank1, rank2, p_grp * p1 / den, p_grp * p2 / den]
    rec = jnp.zeros((tm, LANES), F32)
    for j, f in enumerate(fields):
        rec = jnp.where(lane == j, f, rec)
    return rec


def _pack_bf16_pairs(x):
    w = x.shape[1] // 2
    return pltpu.pack_elementwise([x[:, :w], x[:, w:]], packed_dtype=jnp.bfloat16)


def _rows_to_tiles(ref, index, p):
    r = p.shape[0]
    for s in range(SUBLANES):
        ref[index + (pl.ds(s, r, stride=SUBLANES), slice(None))] = p[:, s * LANES:(s + 1) * LANES]


def _tiles_to_rows(ref, index, r):
    return jnp.concatenate([ref[index + (pl.ds(s, r, stride=SUBLANES), slice(None))] for s in range(SUBLANES)],
                           axis=1)


def _unpack_bf16_pairs(p):
    halves = [pltpu.unpack_elementwise(p, index=i, packed_dtype=jnp.bfloat16, unpacked_dtype=F32) for i in (0, 1)]
    return jnp.concatenate(halves, axis=1)


def _wo_ln_route_kernel(fr_ref, fn_ref, x_ref, w_ref, g_ref, b_ref, wr_ref, br_ref, base_ref,
                        h_ref, hp_ref, route_ref, cnt_ref, carry):
    i = pl.program_id(0)

    @pl.when(i == 0)
    def _():
        carry[...] = base_ref[...]

    y = (DEEPNORM_ALPHA * x_ref[...]
         + jnp.dot(fr_ref[...].astype(BF16), w_ref[0:RET_W, :], preferred_element_type=F32)
         + jnp.dot(fn_ref[...].astype(BF16), w_ref[RET_W:MIX_W, :], preferred_element_type=F32))
    h = _layer_norm_rows(y, g_ref[...], b_ref[...])
    h_ref[...] = h
    _rows_to_tiles(hp_ref, (), _pack_bf16_pairs(h))
    route_ref[...] = _route(h, wr_ref, br_ref, carry)

    @pl.when(i == pl.num_programs(0) - 1)
    def _():
        cnt_ref[...] = carry[...]


def _wo_ln_route(f_ret, f_nsa, x, w_o_bf16, ln_g, ln_b, w_route, b_route, base_counts):
    t, d = x.shape
    tm = min(WO_TM, t)
    assert t % tm == 0
    row = lambda w: pl.BlockSpec((tm, w), lambda i: (i, 0))
    full = lambda a: pl.BlockSpec(a.shape, lambda i: (0,) * a.ndim)
    lg, lb = ln_g.reshape(1, d), ln_b.reshape(1, d)
    return pl.pallas_call(
        _wo_ln_route_kernel,
        out_shape=[jax.ShapeDtypeStruct((t, d), F32), jax.ShapeDtypeStruct((t * SUBLANES, LANES), jnp.uint32),
                   jax.ShapeDtypeStruct((t, LANES), F32), jax.ShapeDtypeStruct((SUBLANES, LANES), F32)],
        grid=(t // tm,),
        in_specs=[row(RET_W), row(NSA_W), row(d), full(w_o_bf16), full(lg), full(lb),
                  full(w_route), full(b_route), full(base_counts)],
        out_specs=[row(d), pl.BlockSpec((tm * SUBLANES, LANES), lambda i: (i, 0)), row(LANES),
                   pl.BlockSpec((SUBLANES, LANES), lambda i: (0, 0))],
        scratch_shapes=[pltpu.VMEM((SUBLANES, LANES), F32)],
        compiler_params=pltpu.CompilerParams(dimension_semantics=("arbitrary",), vmem_limit_bytes=VMEM_LIMIT),
        name="wo_ln1_route",
    )(f_ret, f_nsa, x, w_o_bf16, lg, lb, w_route, b_route, base_counts)


MOE_BM = 256
MOE_SUB = 64
TABLE_UNROLL = 8
META_N_USED, META_FILL_TRIPS, META_PAD_LO, META_PAD_HI = 0, 1, 2, 2 + N_EXPERTS


def _expert_kernel(blk_e_ref, meta_ref, slot_ref, h_ref, wg_ref, wu_ref, wd_ref, y_ref,
                   src_tok, dst_row, xbuf, obuf, gsem, ssem, *, plane):
    del blk_e_ref
    n_sub, sub = xbuf.shape[1], xbuf.shape[2] // SUBLANES
    bm = n_sub * sub
    i = pl.program_id(0)
    n_used = meta_ref[META_N_USED]
    n_asg = slot_ref.shape[0]
    dump0 = EXPERT_TOPK * plane
    assert bm & (bm - 1) == 0

    tile = lambda row: pl.ds(pl.multiple_of(row * SUBLANES, SUBLANES), SUBLANES)

    def gather(blk, buf_slot):
        def sub_block(j, carry):
            base = blk * bm + j * sub
            for u in range(sub):
                pltpu.make_async_copy(h_ref.at[tile(src_tok[base + u]), :],
                                      xbuf.at[buf_slot, j, tile(u), :], gsem.at[buf_slot]).start()
            return carry
        lax.fori_loop(0, n_sub, sub_block, 0)

    def scatter(blk, buf_slot):
        def sub_block(j, carry):
            base = blk * bm + j * sub
            for u in range(sub):
                pltpu.make_async_copy(obuf.at[buf_slot, j, tile(u), :],
                                      y_ref.at[tile(dst_row[base + u]), :], ssem.at[buf_slot]).start()
            return carry
        lax.fori_loop(0, n_sub, sub_block, 0)

    def wait_block(buf, sem, buf_slot):
        pltpu.make_async_copy(buf.at[buf_slot], buf.at[buf_slot], sem.at[buf_slot]).wait()

    @pl.when(i == 0)
    def _():
        def clear_expert(e, carry):
            def clear(r, c):
                src_tok[r] = 0
                dst_row[r] = dump0 + jnp.bitwise_and(r, 2 * bm - 1)
                return c
            lax.fori_loop(meta_ref[META_PAD_LO + e], meta_ref[META_PAD_HI + e], clear, 0)
            return carry
        lax.fori_loop(0, N_EXPERTS, clear_expert, 0)

        def fill(t, carry):
            for u in range(TABLE_UNROLL):
                a = t * TABLE_UNROLL + u
                tok = jnp.right_shift(a, 1)
                src_tok[slot_ref[a]] = tok
                dst_row[slot_ref[a]] = jnp.bitwise_and(a, 1) * plane + tok
            return carry
        lax.fori_loop(0, meta_ref[META_FILL_TRIPS], fill, 0)
        gather(0, 0)
        n_tok = h_ref.shape[0] // SUBLANES
        tail = plane - n_tok
        assert tail >= 0
        obuf[1] = jnp.zeros(obuf.shape[1:], obuf.dtype)
        spans = [(dump0, 2 * bm)] + ([(k * plane + n_tok, tail) for k in range(EXPERT_TOPK)] if tail else [])
        copies = []
        for first, count in spans:
            for j in range(-(-count // sub)):
                rows = min(sub, count - j * sub)
                copies.append(pltpu.make_async_copy(
                    obuf.at[1, j % n_sub, pl.ds(0, rows * SUBLANES), :],
                    y_ref.at[pl.ds((first + j * sub) * SUBLANES, rows * SUBLANES), :], ssem.at[1]))
        for cp in copies:
            cp.start()
        for cp in copies:
            cp.wait()

    slot = jnp.bitwise_and(i, 1)

    @pl.when(i + 1 < n_used)
    def _():
        gather(i + 1, 1 - slot)

    @pl.when(i < n_used)
    def _():
        wait_block(xbuf, gsem, slot)

        @pl.when(i >= 2)
        def _():
            wait_block(obuf, ssem, slot)

        packed = jnp.concatenate([_tiles_to_rows(xbuf, (slot, j), sub) for j in range(n_sub)], axis=0)
        xb = _unpack_bf16_pairs(packed).astype(BF16)
        hg = jnp.dot(xb, wg_ref[0].astype(BF16), preferred_element_type=F32)
        hu = jnp.dot(xb, wu_ref[0].astype(BF16), preferred_element_type=F32)
        hb = hg * (1.0 / (1.0 + jnp.exp(-hg))) * hu
        yb = jnp.dot(hb.astype(BF16), wd_ref[0].astype(BF16), preferred_element_type=F32)
        yp = _pack_bf16_pairs(yb)
        for j in range(n_sub):
            _rows_to_tiles(obuf, (slot, j), yp[j * sub:(j + 1) * sub])
        scatter(i, slot)

    @pl.when(i == pl.num_programs(0) - 1)
    def _():
        @pl.when(n_used >= 2)
        def _():
            wait_block(obuf, ssem, jnp.bitwise_and(n_used, 1))
        wait_block(obuf, ssem, jnp.bitwise_and(n_used - 1, 1))


def _expert_ffn(h, slot, blk_e, meta, w_gate, w_up, w_down, plane):
    t = h.shape[0] // SUBLANES
    dp = SUBLANES * LANES
    d = 2 * dp
    assert w_gate.shape[1] == d
    n_asg = slot.shape[0]
    assert n_asg == t * EXPERT_TOPK and EXPERT_TOPK == 2
    n_blk = -(-(n_asg + N_EXPERTS * (MOE_BM - 1)) // MOE_BM)
    de = w_gate.shape[2]
    assert n_asg % TABLE_UNROLL == 0 and meta.shape == (META_PAD_HI + N_EXPERTS,)
    wspec = lambda shape: pl.BlockSpec((1,) + shape, lambda i, be, nu, sl: (be[i], 0, 0))
    return pl.pallas_call(
        functools.partial(_expert_kernel, plane=plane),
        out_shape=jax.ShapeDtypeStruct(((EXPERT_TOPK * plane + 2 * MOE_BM) * SUBLANES, LANES), jnp.uint32),
        grid_spec=pltpu.PrefetchScalarGridSpec(
            num_scalar_prefetch=3,
            grid=(n_blk,),
            in_specs=[pl.BlockSpec(memory_space=pl.ANY), wspec((d, de)), wspec((d, de)), wspec((de, d))],
            out_specs=pl.BlockSpec(memory_space=pl.ANY),
            scratch_shapes=[pltpu.SMEM((n_blk * MOE_BM,), jnp.int32), pltpu.SMEM((n_blk * MOE_BM,), jnp.int32),
                            pltpu.VMEM((2, MOE_BM // MOE_SUB, MOE_SUB * SUBLANES, LANES), jnp.uint32),
                            pltpu.VMEM((2, MOE_BM // MOE_SUB, MOE_SUB * SUBLANES, LANES), jnp.uint32),
                            pltpu.SemaphoreType.DMA((2,)), pltpu.SemaphoreType.DMA((2,))]),
        compiler_params=pltpu.CompilerParams(dimension_semantics=("arbitrary",), vmem_limit_bytes=VMEM_LIMIT),
        name="expert_ffn",
    )(blk_e, meta, slot, h, w_gate, w_up, w_down)


def _moe_ln_kernel(h_ref, y0_ref, y1_ref, route_ref, g_ref, b_ref, o_ref):
    rec = route_ref[...]
    tm = h_ref.shape[0]
    y = (DEEPNORM_ALPHA * h_ref[...]
         + rec[:, R_GATE:R_GATE + 1] * _unpack_bf16_pairs(_tiles_to_rows(y0_ref, (), tm))
         + rec[:, R_GATE + 1:R_GATE + 2] * _unpack_bf16_pairs(_tiles_to_rows(y1_ref, (), tm)))
    o_ref[...] = _layer_norm_rows(y, g_ref[...], b_ref[...])


def _moe_ln(h, y, route, ln_g, ln_b, row0, plane):
    n_rows, d = h.shape
    tm = min(WO_TM, n_rows)
    assert n_rows % tm == 0 and row0 % tm == 0 and plane % tm == 0 and EXPERT_TOPK == 2
    off = row0 // tm
    row = lambda w, o: pl.BlockSpec((tm, w), lambda i: (i + o, 0))
    tiles = lambda o: pl.BlockSpec((tm * SUBLANES, LANES), lambda i: (i + o, 0))
    vec = pl.BlockSpec((1, d), lambda i: (0, 0))
    return pl.pallas_call(
        _moe_ln_kernel,
        out_shape=jax.ShapeDtypeStruct((n_rows, d), F32),
        grid=(n_rows // tm,),
        in_specs=[row(d, 0), tiles(off), tiles(off + plane // tm), row(LANES, off), vec, vec],
        out_specs=row(d, 0),
        compiler_params=pltpu.CompilerParams(dimension_semantics=("arbitrary",), vmem_limit_bytes=VMEM_LIMIT),
        name="moe_ln2",
    )(h, y, y, route, ln_g.reshape(1, d), ln_b.reshape(1, d))


def _layer_norm(x, g, b):
    xf = x.astype(F32)
    mu = xf.mean(-1, keepdims=True)
    var = jnp.square(xf - mu).mean(-1, keepdims=True)
    return ((xf - mu) * lax.rsqrt(var + LN_EPS) * g + b).astype(x.dtype)


def _rope(x, pos, rot_dim, theta):
    half = rot_dim // 2
    inv = theta ** (-jnp.arange(0, rot_dim, 2, dtype=F32) / rot_dim)
    ang = pos[..., None].astype(F32) * inv
    cos = jnp.cos(ang)[:, :, None, :]
    sin = jnp.sin(ang)[:, :, None, :]
    xr = x[..., :rot_dim].astype(F32)
    x1, x2 = xr[..., :half], xr[..., half:]
    rot = jnp.concatenate([x1 * cos - x2 * sin, x2 * cos + x1 * sin], -1).astype(x.dtype)
    return jnp.concatenate([rot, x[..., rot_dim:]], -1)


def _heads(t, n):
    return t.reshape(t.shape[0], t.shape[1], n, HEAD_DIM)


def _chunk_retention(q, k, v, s0):
    n, l, h, d = q.shape
    c = RET_CHUNK if l % RET_CHUNK == 0 else l
    nc = l // c
    log_g = jnp.log1p(-jnp.exp2(-5.0 - jnp.arange(h, dtype=F32)))
    i = jnp.arange(c, dtype=F32)
    rel = i[:, None] - i[None, :]
    dmask = jnp.where(rel[None] >= 0, jnp.exp(jnp.maximum(rel[None], 0.0) * log_g[:, None, None]), 0.0)
    q_dec = jnp.exp((i + 1.0)[None] * log_g[:, None])[..., None]
    k_dec = jnp.exp((c - 1.0 - i)[None] * log_g[:, None])[..., None]
    c_dec = jnp.exp(c * log_g)[:, None, None]

    def to_chunks(t):
        return t.astype(F32).reshape(n, nc, c, h, d).transpose(1, 0, 3, 2, 4)

    def step(s, qkv):
        qc, kc, vc = qkv
        att = jnp.einsum('bhid,bhjd->bhij', qc, kc) * dmask
        o = jnp.einsum('bhij,bhjd->bhid', att, vc) + jnp.einsum('bhid,bhde->bhie', qc * q_dec, s)
        s = c_dec * s + jnp.einsum('bhjd,bhje->bhde', kc * k_dec, vc)
        return s, o

    s, o = lax.scan(step, s0.astype(F32), (to_chunks(q), to_chunks(k), to_chunks(v)))
    return o.transpose(1, 0, 3, 2, 4).reshape(n, l, h, d), s


def _retention_group(rq, rk, rv, rg, pos, s0, gn_g):
    q = _rope(_heads(rq, RET_HEADS), pos, HEAD_DIM, RET_ROPE_THETA)
    k = _rope(_heads(rk, RET_HEADS), pos, HEAD_DIM, RET_ROPE_THETA) * (HEAD_DIM ** -0.5)
    v = _heads(rv, RET_HEADS)
    o, s = _chunk_retention(q, k, v, s0)
    mu = o.mean(-1, keepdims=True)
    var = jnp.square(o - mu).mean(-1, keepdims=True)
    on = (o - mu) * lax.rsqrt(var + GN_EPS) * gn_g.reshape(RET_HEADS, HEAD_DIM).astype(F32)
    out = jax.nn.silu(rg.astype(F32)) * on.reshape(rg.shape)
    return out.astype(rq.dtype), s


def _gqa_attend(q, k, v, mask):
    n, lq, h, d = q.shape
    kv = k.shape[2]
    qg = q.reshape(n, lq, kv, h // kv, d)
    s = jnp.einsum('nqkgd,nskd->nkgqs', qg, k).astype(F32) * (d ** -0.5)
    m = mask[:, None, None]
    p = jax.nn.softmax(jnp.where(m, s, NEG), axis=-1) * m
    o = jnp.einsum('nkgqs,nskd->nqkgd', p.astype(v.dtype), v)
    return o.reshape(n, lq, h, d), p


def _nsa_heads(nq, ck, sk, wk, cv, sv, wv, pos):
    rp = lambda t, nh: _rope(_heads(t, nh), pos, ROT_DIM, ROPE_THETA)
    return (rp(nq, NSA_HEADS), rp(ck, NSA_KV_HEADS), rp(sk, NSA_KV_HEADS), rp(wk, NSA_KV_HEADS),
            _heads(cv, NSA_KV_HEADS), _heads(sv, NSA_KV_HEADS), _heads(wv, NSA_KV_HEADS))


def _compress(rows, w):
    n, t, kv, d = rows.shape
    return jnp.einsum('nbjkd,jd->nbkd', rows.reshape(n, t // CMP_BLOCK, CMP_BLOCK, kv, d), w)


def _cmp_branch(q, pos, kc, vc):
    nb = kc.shape[1]
    blk_end = (jnp.arange(nb) + 1) * CMP_BLOCK - 1
    mask = blk_end[None, None, :] <= pos[:, :, None]
    o, p = _gqa_attend(q, kc, vc, mask)
    imp = p.sum(axis=2).transpose(0, 2, 1, 3)
    return o, imp


def _select_blocks(imp, pos, n_sel):
    nb = imp.shape[-1]
    imp = jnp.pad(imp, ((0, 0), (0, 0), (0, 0), (0, n_sel - nb)))
    blk = jnp.arange(n_sel)
    cur = (pos // CMP_BLOCK)[:, :, None, None]
    forced = (blk == 0) | (blk == cur) | (blk == cur - 1)
    score = jnp.where(blk > cur, NEG, jnp.where(forced, -NEG, imp))
    top, idx = lax.top_k(score, min(SEL_TOPK, n_sel))
    return idx, top > 0.5 * NEG


def _sel_attend(q, pos, ks, vs, idx, valid):
    n, lq, kv, kk, cb, d = ks.shape
    h = q.shape[2]
    kpos = idx[..., None] * CMP_BLOCK + jnp.arange(CMP_BLOCK)
    m = ((kpos <= pos[:, :, None, None, None]) & valid[..., None]).reshape(n, lq, kv, 1, kk * cb)
    qg = q.reshape(n, lq, kv, h // kv, d)
    kf = ks.reshape(n, lq, kv, kk * cb, d)
    vf = vs.reshape(n, lq, kv, kk * cb, d)
    s = jnp.einsum('nqkgd,nqkjd->nqkgj', qg, kf).astype(F32) * (d ** -0.5)
    p = jax.nn.softmax(jnp.where(m, s, NEG), axis=-1) * m
    o = jnp.einsum('nqkgj,nqkjd->nqkgd', p.astype(vf.dtype), vf)
    return o.reshape(n, lq, h, d)


def _sel_prompt(q, pos, k, v, idx, valid):
    b, s, h, d = q.shape
    kv = k.shape[2]
    nb = s // CMP_BLOCK
    nq = s // SEL_Q_BLOCK
    kb = k.reshape(b, nb, CMP_BLOCK, kv, d).transpose(0, 3, 1, 2, 4)
    vb = v.reshape(b, nb, CMP_BLOCK, kv, d).transpose(0, 3, 1, 2, 4)
    bi = jnp.arange(b)[:, None, None, None]
    hi = jnp.arange(kv)[None, None, :, None]

    def blockwise(t):
        return t.reshape(t.shape[0], nq, SEL_Q_BLOCK, *t.shape[2:]).swapaxes(0, 1)

    def one(args):
        qc, pc, ic, vc = args
        return _sel_attend(qc, pc, kb[bi, hi, ic], vb[bi, hi, ic], ic, vc)

    o = lax.map(one, (blockwise(q), blockwise(pos), blockwise(idx), blockwise(valid)))
    return o.swapaxes(0, 1).reshape(b, s, h, d)


def _win_prompt(q, k, v):
    b, s, h, d = q.shape
    kv = k.shape[2]
    nb = s // WIN_Q_BLOCK
    nprev = WINDOW // WIN_Q_BLOCK
    nw = nprev + 1
    padw = ((0, 0), (WINDOW, 0), (0, 0), (0, 0))
    kp = jnp.pad(k, padw).reshape(b, nb + nprev, WIN_Q_BLOCK, kv, d)
    vp = jnp.pad(v, padw).reshape(b, nb + nprev, WIN_Q_BLOCK, kv, d)
    kw = jnp.concatenate([kp[:, i:i + nb] for i in range(nw)], axis=2)
    vw = jnp.concatenate([vp[:, i:i + nb] for i in range(nw)], axis=2)
    qpos = jnp.arange(s).reshape(nb, WIN_Q_BLOCK)
    kpos = (jnp.arange(nb)[:, None] - nprev) * WIN_Q_BLOCK + jnp.arange(nw * WIN_Q_BLOCK)[None]
    qq, kk = qpos[:, :, None], kpos[:, None, :]
    mask = (kk <= qq) & (qq - kk < WINDOW) & (kk >= 0)
    mask = jnp.broadcast_to(mask[None], (b,) + mask.shape).reshape(b * nb, WIN_Q_BLOCK, nw * WIN_Q_BLOCK)
    o, _ = _gqa_attend(q.reshape(b * nb, WIN_Q_BLOCK, h, d), kw.reshape(b * nb, nw * WIN_Q_BLOCK, kv, d),
                       vw.reshape(b * nb, nw * WIN_Q_BLOCK, kv, d), mask)
    return o.reshape(b, s, h, d)


def _gather_selected(pool, new_rows, page_table, idx):
    n, l, kv, d = new_rows.shape
    n_pages = page_table.shape[1]
    past_blocks = n_pages * PAGE_SIZE // CMP_BLOCK
    nbn = -(-l // CMP_BLOCK)
    newb = jnp.pad(new_rows, ((0, 0), (0, nbn * CMP_BLOCK - l), (0, 0), (0, 0)))
    newb = newb.reshape(n, nbn, CMP_BLOCK, kv, d).transpose(0, 3, 1, 2, 4)
    bi = jnp.arange(n)[:, None, None, None]
    hi = jnp.arange(kv)[None, None, :, None]
    start = idx * CMP_BLOCK
    phys = page_table[bi, jnp.minimum(start // PAGE_SIZE, n_pages - 1)]
    off = (start % PAGE_SIZE)[..., None] + jnp.arange(CMP_BLOCK)
    past = pool[phys[..., None], off, hi[..., None]]
    new = newb[bi, hi, jnp.clip(idx - past_blocks, 0, nbn - 1)]
    return jnp.where((idx < past_blocks)[..., None, None], past, new)


def _nsa_combine(gl, o_cmp, o_sel, o_win):
    n, l = gl.shape[0], gl.shape[1]
    g = jax.nn.sigmoid(gl.astype(F32)).reshape(n, l, NSA_HEADS, 3, 1)
    o = g[..., 0, :] * o_cmp + g[..., 1, :] * o_sel + g[..., 2, :] * o_win
    return o.reshape(n, l, NSA_W).astype(o_cmp.dtype)


def _prompt_mixer(x, win_buf, w_in, w_cmp_k, w_cmp_v, gn_g):
    n, s, _ = x.shape
    main, (ck, cv, sk, sv, wk, wv) = _project(x, w_in, jnp.arange(s))
    ret_out, s_fin = _retention_prompt(main, gn_g, n, s)
    kc, vc = _compress_prompt(main, w_cmp_k, w_cmp_v)
    nsa = _nsa_prompt(main, kc.reshape(n, s // CMP_BLOCK, KV_W), vc.reshape(n, s // CMP_BLOCK, KV_W), n, s)
    feats = (ret_out, nsa)
    if s >= win_buf:
        bk, bv = wk[:, s - win_buf:], wv[:, s - win_buf:]
    else:
        padb = ((0, 0), (win_buf - s, 0), (0, 0), (0, 0))
        bk, bv = jnp.pad(wk, padb), jnp.pad(wv, padb)
    return feats, (ck, cv, sk, sv, bk, bv, s_fin)


def _sample_mixer(x, c_cmp_k, c_cmp_v, c_sel_k, c_sel_v, c_win_k, c_win_v, s_ret, page_table,
                  w_in, w_cmp_k, w_cmp_v, gn_g):
    n, l, _ = x.shape
    past = page_table.shape[1] * PAGE_SIZE
    pos = past + jnp.arange(l)[None]
    assert l == 1
    main, (ck, cv, sk, sv, wk, wv) = _project(x, w_in, jnp.full((n,), past, jnp.int32))
    rq, rk, rv, rg, nq = _split_main(main, n, l)[:5]
    gl = _gate_logits(main, n, l)
    ret_out, s_new = _retention_group(rq, rk, rv, rg, pos, s_ret, gn_g)
    q = _heads(nq, NSA_HEADS)
    assert l == 1 and past % CMP_BLOCK == 0 and c_win_k.shape[1] <= WINDOW
    kc, vc = _compress_paged(c_cmp_k, c_cmp_v, page_table, w_cmp_k, w_cmp_v)
    o_cmp, sel = _sample_select(q[:, 0], kc, vc, past)
    sel = sel[:, :SEL_TOPK].reshape(-1)
    gl_pad = jnp.pad(gl.reshape(n, NSA_HEADS, 3), ((0, 0), (0, 0), (0, LANES - 3)))
    nsa = _sample_attend(sel, page_table, q[:, 0], sk[:, 0], sv[:, 0], wk[:, 0], wv[:, 0],
                         _interleaved(c_win_k), _interleaved(c_win_v), _interleaved(c_sel_k), _interleaved(c_sel_v),
                         o_cmp, gl_pad, past)
    feats = (ret_out.reshape(n * l, RET_W), nsa.reshape(n * l, NSA_W))
    kw = jnp.concatenate([c_win_k, wk], 1)
    vw = jnp.concatenate([c_win_v, wv], 1)
    return feats, (ck, cv, sk, sv, kw[:, l:], vw[:, l:], s_new)


def _route_params(w_group, b_group, w_expert, b_expert):
    w = jnp.concatenate([w_group, w_expert], axis=1)
    b = jnp.concatenate([b_group, b_expert], axis=0)
    pad = LANES - w.shape[1]
    return jnp.pad(w, ((0, 0), (0, pad))).astype(BF16), jnp.pad(b, (0, pad)).reshape(1, LANES)


def _dispatch_plan(route, counts_tile):
    counts = counts_tile[0, EXPERT_LANE0:EXPERT_LANE0 + N_EXPERTS].astype(jnp.int32)
    padded = (counts + MOE_BM - 1) // MOE_BM * MOE_BM
    pad_end = jnp.cumsum(padded)
    pad_start = pad_end - padded
    n_asg = route.shape[0] * EXPERT_TOPK
    n_blk = -(-(n_asg + N_EXPERTS * (MOE_BM - 1)) // MOE_BM)
    blk_first = jnp.arange(n_blk, dtype=jnp.int32) * MOE_BM
    blk_e = jnp.minimum(jnp.sum(pad_end[None, :] <= blk_first[:, None], axis=1), N_EXPERTS - 1).astype(jnp.int32)
    meta = jnp.concatenate([pad_end[-1:] // MOE_BM, jnp.array([n_asg // TABLE_UNROLL]), pad_start + counts,
                            pad_end]).astype(jnp.int32)
    eid = route[:, R_EID:R_EID + EXPERT_TOPK].astype(jnp.int32)
    rank = route[:, R_RANK:R_RANK + EXPERT_TOPK].astype(jnp.int32)
    start = jnp.sum(jnp.where(eid[..., None] == jnp.arange(N_EXPERTS), pad_start, 0), axis=-1)
    return (start + rank).reshape(-1).astype(jnp.int32), blk_e, meta


def kernel(x_prompt, x_sample, cache_cmp_k, cache_cmp_v, cache_sel_k, cache_sel_v, cache_win_k, cache_win_v,
           state_ret, page_table, w_in, w_cmp_k, w_cmp_v, ret_gn_g, w_o, ln1_g, ln1_b, w_group, b_group,
           w_expert, b_expert, w_gate, w_up, w_down, ln2_g, ln2_b):
    win_buf = cache_win_k.shape[2]
    hp, hs = x_prompt, x_sample
    acc_p = [[] for _ in range(7)]
    acc_s = [[] for _ in range(7)]
    for l in range(DEPTH):
        fp, st_p = _prompt_mixer(hp, win_buf, w_in[l], w_cmp_k[l], w_cmp_v[l], ret_gn_g[l])
        fs, st_s = _sample_mixer(hs, cache_cmp_k[l], cache_cmp_v[l], cache_sel_k[l], cache_sel_v[l],
                                 cache_win_k[l], cache_win_v[l], state_ret[l], page_table,
                                 w_in[l], w_cmp_k[l], w_cmp_v[l], ret_gn_g[l])
        w_o_bf16 = w_o[l].astype(BF16)
        w_route, b_route = _route_params(w_group[l], b_group[l], w_expert[l], b_expert[l])
        tp = hp.shape[0] * hp.shape[1]
        ts = hs.shape[0] * hs.shape[1]
        no_counts = jnp.zeros((SUBLANES, LANES), F32)
        h1p, packed_p, route_p, counts_p = _wo_ln_route(fp[0], fp[1], hp.reshape(tp, D_MODEL), w_o_bf16, ln1_g[l],
                                                        ln1_b[l], w_route, b_route, no_counts)
        h1s, packed_s, route_s, counts = _wo_ln_route(fs[0], fs[1], hs.reshape(ts, D_MODEL), w_o_bf16, ln1_g[l],
                                                      ln1_b[l], w_route, b_route, counts_p)
        packed = jnp.concatenate([packed_p, packed_s], axis=0)
        route = jnp.concatenate([route_p, route_s], axis=0)
        slot, blk_e, meta = _dispatch_plan(route, counts)
        plane = -(-(tp + ts) // WO_TM) * WO_TM
        y = _expert_ffn(packed, slot, blk_e, meta, w_gate[l], w_up[l], w_down[l], plane)
        hp = _moe_ln(h1p, y, route, ln2_g[l], ln2_b[l], 0, plane).reshape(hp.shape)
        hs = _moe_ln(h1s, y, route, ln2_g[l], ln2_b[l], tp, plane).reshape(hs.shape)
        for acc, t in zip(acc_p, st_p):
            acc.append(t)
        for acc, t in zip(acc_s, st_s):
            acc.append(t)
    p_cmp_k, p_cmp_v, p_sel_k, p_sel_v, p_win_k, p_win_v, p_ret = [jnp.stack(a) for a in acc_p]
    s_cmp_k, s_cmp_v, s_sel_k, s_sel_v, s_win_k, s_win_v, s_ret = [jnp.stack(a) for a in acc_s]
    return (hp, hs, p_cmp_k, p_cmp_v, p_sel_k, p_sel_v, p_win_k, p_win_v, p_ret.astype(state_ret.dtype),
            s_cmp_k, s_cmp_v, s_sel_k, s_sel_v, s_win_k, s_win_v, s_ret.astype(state_ret.dtype))
```
